```python
import math
import jax, jax.numpy as jnp
from jax import lax
import numpy as np

D_MODEL = 2048
BATCH = 8
SEQ = 4096
DEPTH = 1

HEAD_DIM = 128
N_HEADS = D_MODEL // HEAD_DIM
H_A = N_HEADS // 2
H_B = N_HEADS - H_A
D_A = H_A * HEAD_DIM
D_B = H_B * HEAD_DIM
D_PROJ = 3 * D_A + 3 * D_B + H_B
DILATED_PATTERNS = ((128, 1), (512, 4), (2048, 16))
NUM_BUCKETS = 32
MAX_DISTANCE = 2048
FOX_BLOCK = 128
D_FF = ((8 * D_MODEL // 3 + 127) // 128) * 128
RMS_EPS = 1e-6
NEG_INF = -1e30
ATTN_SCALE = HEAD_DIM ** -0.5

kernel_name = "hybrid_dilated_fox_macaron"


def rmsnorm(x, g):
    xf = x.astype(jnp.float32)
    y = xf * lax.rsqrt(jnp.mean(xf * xf, axis=-1, keepdims=True) + RMS_EPS)
    return (y * g.astype(jnp.float32)).astype(x.dtype)


def swiglu_ffn(x, w_in, w_out):
    gate, up = jnp.split(x @ w_in, 2, axis=-1)
    return (jax.nn.silu(gate) * up) @ w_out


def t5_causal_bucket(dist):
    max_exact = NUM_BUCKETS // 2
    d = dist.astype(jnp.float32)
    large = max_exact + (jnp.log(jnp.maximum(d, 1.0) / max_exact)
                         / math.log(MAX_DISTANCE / max_exact)
                         * (NUM_BUCKETS - max_exact)).astype(jnp.int32)
    large = jnp.minimum(large, NUM_BUCKETS - 1)
    return jnp.where(dist < max_exact, dist, large)


def dilated_window_partial(q, k, v, rel_table, window, dilation):
    B, S, H, D = q.shape
    m = window // dilation
    L = m
    span = dilation * L
    Sp = -(-S // span) * span
    n = Sp // dilation
    nb = n // L

    def to_sub(t):
        t = jnp.pad(t, ((0, 0), (0, Sp - S), (0, 0), (0, 0)))
        t = t.reshape(B, n, dilation, H, D).transpose(0, 2, 3, 1, 4)
        return t.reshape(B, dilation, H, nb, L, D)

    def with_prev(t):
        prev = jnp.pad(t, ((0, 0), (0, 0), (0, 0), (1, 0), (0, 0), (0, 0)))[:, :, :, :-1]
        return jnp.concatenate([prev, t], axis=4)

    qs = to_sub(q)
    kw = with_prev(to_sub(k))
    vw = with_prev(to_sub(v))

    iq = jnp.arange(L)[:, None] + L
    ik = jnp.arange(2 * L)[None, :]
    delta = iq - ik
    band = (delta >= 0) & (delta <= m)
    bucket = t5_causal_bucket(jnp.clip(delta, 0, None) * dilation)
    bias = rel_table[bucket].astype(jnp.float32).transpose(2, 0, 1)
    blk = jnp.arange(nb)
    key_ok = (blk[:, None] * L + ik - L) >= 0
    mask = band[None] & key_ok[:, None, :]

    logits = jnp.einsum('bzhnqd,bzhnkd->bzhnqk', qs, kw) * ATTN_SCALE + bias[:, None]
    logits = jnp.where(mask, logits, NEG_INF)
    mx = jnp.max(logits, axis=-1)
    p = jnp.exp(logits - mx[..., None])
    den = jnp.sum(p, axis=-1)
    num = jnp.einsum('bzhnqk,bzhnkd->bzhnqd', p, vw)

    def from_sub(t):
        rest = t.shape[5:]
        t = t.reshape((B, dilation, H, n) + rest)
        t = jnp.moveaxis(t, 3, 1)
        return t.reshape((B, Sp, H) + rest)[:, :S]

    return from_sub(num), from_sub(den), from_sub(mx)


def dilated_attention(q, k, v, rel_table):
    parts = [dilated_window_partial(q, k, v, rel_table, w, d) for (w, d) in DILATED_PATTERNS]
    mx = jnp.max(jnp.stack([pt[2] for pt in parts], axis=0), axis=0)
    num = None
    den = None
    for pnum, pden, pmx in parts:
        scale = jnp.exp(pmx - mx)
        num = pnum * scale[..., None] if num is None else num + pnum * scale[..., None]
        den = pden * scale if den is None else den + pden * scale
    return num / den[..., None]


def forgetting_attention(q, k, v, log_f):
    B, S, H, D = q.shape
    c = jnp.cumsum(log_f, axis=1)
    c_k = c.transpose(0, 2, 1)
    nblk = S // FOX_BLOCK
    q_blocks = q.reshape(B, nblk, FOX_BLOCK, H, D).transpose(1, 0, 3, 2, 4)
    c_blocks = c.reshape(B, nblk, FOX_BLOCK, H).transpose(1, 0, 3, 2)
    kpos = jnp.arange(S)

    def block(args):
        q_blk, c_blk, i = args
        qpos = i * FOX_BLOCK + jnp.arange(FOX_BLOCK)
        logits = (jnp.einsum('bhqd,bkhd->bhqk', q_blk, k) * ATTN_SCALE
                  + c_blk[..., None] - c_k[:, :, None, :])
        logits = jnp.where(kpos[None, :] <= qpos[:, None], logits, NEG_INF)
        p = jax.nn.softmax(logits, axis=-1)
        return jnp.einsum('bhqk,bkhd->bhqd', p, v)

    out = lax.map(block, (q_blocks, c_blocks, jnp.arange(nblk)))
    return out.transpose(1, 0, 3, 2, 4).reshape(B, S, H, D)


def _fwd_setup_inputs(seed: int = 0) -> dict:
    key = jax.random.key(seed)
    ks = jax.random.split(key, 17)
    f32 = jnp.float32

    def gain(k, shape):
        return (1.0 + 0.02 * jax.random.normal(k, shape)).astype(f32)

    x = jax.random.normal(ks[0], (BATCH, SEQ, D_MODEL), f32)
    ffn1_norm = gain(ks[1], (DEPTH, D_MODEL))
    ffn1_w_in = jax.random.normal(ks[2], (DEPTH, D_MODEL, 2 * D_FF), f32) * D_MODEL ** -0.5
    ffn1_w_out = jax.random.normal(ks[3], (DEPTH, D_FF, D_MODEL), f32) * D_FF ** -0.5
    mix_norm = gain(ks[4], (DEPTH, D_MODEL))
    w_in = jax.random.normal(ks[5], (DEPTH, D_MODEL, D_PROJ), f32) * D_MODEL ** -0.5
    q_norm_a = gain(ks[6], (DEPTH, HEAD_DIM))
    k_norm_a = gain(ks[7], (DEPTH, HEAD_DIM))
    q_norm_b = gain(ks[8], (DEPTH, HEAD_DIM))
    k_norm_b = gain(ks[9], (DEPTH, HEAD_DIM))
    forget_bias = (jnp.linspace(1.0, 4.0, H_B, dtype=f32)[None, :]
                   + 0.1 * jax.random.normal(ks[10], (DEPTH, H_B), f32))
    rel_bias_table = 0.5 * jax.random.normal(ks[11], (NUM_BUCKETS, H_A), f32)
    w_out = jax.random.normal(ks[12], (DEPTH, D_A + D_B, D_MODEL), f32) * (D_A + D_B) ** -0.5
    ffn2_norm = gain(ks[13], (DEPTH, D_MODEL))
    ffn2_w_in = jax.random.normal(ks[14], (DEPTH, D_MODEL, 2 * D_FF), f32) * D_MODEL ** -0.5
    ffn2_w_out = jax.random.normal(ks[15], (DEPTH, D_FF, D_MODEL), f32) * D_FF ** -0.5
    return {"x": x, "ffn1_norm": ffn1_norm, "ffn1_w_in": ffn1_w_in, "ffn1_w_out": ffn1_w_out,
            "mix_norm": mix_norm, "w_in": w_in, "q_norm_a": q_norm_a, "k_norm_a": k_norm_a,
            "q_norm_b": q_norm_b, "k_norm_b": k_norm_b, "forget_bias": forget_bias,
            "rel_bias_table": rel_bias_table, "w_out": w_out, "ffn2_norm": ffn2_norm,
            "ffn2_w_in": ffn2_w_in, "ffn2_w_out": ffn2_w_out}


def _fwd_reference(x, ffn1_norm, ffn1_w_in, ffn1_w_out, mix_norm, w_in, q_norm_a, k_norm_a,
              q_norm_b, k_norm_b, forget_bias, rel_bias_table, w_out, ffn2_norm,
              ffn2_w_in, ffn2_w_out):
    B, S, _ = x.shape
    splits = np.cumsum([D_A, D_A, D_A, D_B, D_B, D_B]).tolist()
    for l in range(DEPTH):
        x = x + 0.5 * swiglu_ffn(rmsnorm(x, ffn1_norm[l]), ffn1_w_in[l], ffn1_w_out[l])

        h = rmsnorm(x, mix_norm[l])
        proj = h @ w_in[l]
        qa, ka, va, qb, kb, vb, f_pre = jnp.split(proj, splits, axis=-1)

        def heads(t, n_h):
            return t.reshape(B, S, n_h, HEAD_DIM).astype(jnp.float32)

        qa = rmsnorm(heads(qa, H_A), q_norm_a[l])
        ka = rmsnorm(heads(ka, H_A), k_norm_a[l])
        va = heads(va, H_A)
        qb = rmsnorm(heads(qb, H_B), q_norm_b[l])
        kb = rmsnorm(heads(kb, H_B), k_norm_b[l])
        vb = heads(vb, H_B)
        log_f = jax.nn.log_sigmoid(f_pre.astype(jnp.float32) + forget_bias[l].astype(jnp.float32))

        out_a = dilated_attention(qa, ka, va, rel_bias_table)
        out_b = forgetting_attention(qb, kb, vb, log_f)
        mixed = jnp.concatenate([out_a.reshape(B, S, D_A), out_b.reshape(B, S, D_B)],
                                axis=-1).astype(x.dtype)
        x = x + mixed @ w_out[l]

        x = x + 0.5 * swiglu_ffn(rmsnorm(x, ffn2_norm[l]), ffn2_w_in[l], ffn2_w_out[l])
    return x


import jax as _jax
import jax.numpy as _jnp

TWIN_FORMAT = 'train_step'
FWD_PARAMS = ['x', 'ffn1_norm', 'ffn1_w_in', 'ffn1_w_out', 'mix_norm', 'w_in', 'q_norm_a', 'k_norm_a', 'q_norm_b', 'k_norm_b', 'forget_bias', 'rel_bias_table', 'w_out', 'ffn2_norm', 'ffn2_w_in', 'ffn2_w_out']
TWIN_WEIGHTS = ['ffn1_norm', 'ffn1_w_in', 'ffn1_w_out', 'mix_norm', 'w_in', 'q_norm_a', 'k_norm_a', 'q_norm_b', 'k_norm_b', 'forget_bias', 'rel_bias_table', 'w_out', 'ffn2_norm', 'ffn2_w_in', 'ffn2_w_out']
TWIN_DIFF_INPUT = 'x'
TWIN_INPUTS = ['x', 'ffn1_norm', 'ffn1_w_in', 'ffn1_w_out', 'mix_norm', 'w_in', 'q_norm_a', 'k_norm_a', 'q_norm_b', 'k_norm_b', 'forget_bias', 'rel_bias_table', 'w_out', 'ffn2_norm', 'ffn2_w_in', 'ffn2_w_out', 'loss_target', 'm_ffn1_norm', 'm_ffn1_w_in', 'm_ffn1_w_out', 'm_mix_norm', 'm_w_in', 'm_q_norm_a', 'm_k_norm_a', 'm_q_norm_b', 'm_k_norm_b', 'm_forget_bias', 'm_rel_bias_table', 'm_w_out', 'm_ffn2_norm', 'm_ffn2_w_in', 'm_ffn2_w_out', 'v_ffn1_norm', 'v_ffn1_w_in', 'v_ffn1_w_out', 'v_mix_norm', 'v_w_in', 'v_q_norm_a', 'v_k_norm_a', 'v_q_norm_b', 'v_k_norm_b', 'v_forget_bias', 'v_rel_bias_table', 'v_w_out', 'v_ffn2_norm', 'v_ffn2_w_in', 'v_ffn2_w_out']
TWIN_OUTPUTS = ['loss', 'grad_x', 'grad_ffn1_norm', 'grad_ffn1_w_in', 'grad_ffn1_w_out', 'grad_mix_norm', 'grad_w_in', 'grad_q_norm_a', 'grad_k_norm_a', 'grad_q_norm_b', 'grad_k_norm_b', 'grad_forget_bias', 'grad_rel_bias_table', 'grad_w_out', 'grad_ffn2_norm', 'grad_ffn2_w_in', 'grad_ffn2_w_out', 'delta_ffn1_norm', 'delta_ffn1_w_in', 'delta_ffn1_w_out', 'delta_mix_norm', 'delta_w_in', 'delta_q_norm_a', 'delta_k_norm_a', 'delta_q_norm_b', 'delta_k_norm_b', 'delta_forget_bias', 'delta_rel_bias_table', 'delta_w_out', 'delta_ffn2_norm', 'delta_ffn2_w_in', 'delta_ffn2_w_out', 'new_m_ffn1_norm', 'new_m_ffn1_w_in', 'new_m_ffn1_w_out', 'new_m_mix_norm', 'new_m_w_in', 'new_m_q_norm_a', 'new_m_k_norm_a', 'new_m_q_norm_b', 'new_m_k_norm_b', 'new_m_forget_bias', 'new_m_rel_bias_table', 'new_m_w_out', 'new_m_ffn2_norm', 'new_m_ffn2_w_in', 'new_m_ffn2_w_out', 'new_v_ffn1_norm', 'new_v_ffn1_w_in', 'new_v_ffn1_w_out', 'new_v_mix_norm', 'new_v_w_in', 'new_v_q_norm_a', 'new_v_k_norm_a', 'new_v_q_norm_b', 'new_v_k_norm_b', 'new_v_forget_bias', 'new_v_rel_bias_table', 'new_v_w_out', 'new_v_ffn2_norm', 'new_v_ffn2_w_in', 'new_v_ffn2_w_out']
TWIN_LEAF_KINDS = {'loss': 'loss', 'grad_x': 'grad_x', 'grad_ffn1_norm': 'grad_w', 'grad_ffn1_w_in': 'grad_w', 'grad_ffn1_w_out': 'grad_w', 'grad_mix_norm': 'grad_w', 'grad_w_in': 'grad_w', 'grad_q_norm_a': 'grad_w', 'grad_k_norm_a': 'grad_w', 'grad_q_norm_b': 'grad_w', 'grad_k_norm_b': 'grad_w', 'grad_forget_bias': 'grad_w', 'grad_rel_bias_table': 'grad_w', 'grad_w_out': 'grad_w', 'grad_ffn2_norm': 'grad_w', 'grad_ffn2_w_in': 'grad_w', 'grad_ffn2_w_out': 'grad_w', 'delta_ffn1_norm': 'delta_w', 'delta_ffn1_w_in': 'delta_w', 'delta_ffn1_w_out': 'delta_w', 'delta_mix_norm': 'delta_w', 'delta_w_in': 'delta_w', 'delta_q_norm_a': 'delta_w', 'delta_k_norm_a': 'delta_w', 'delta_q_norm_b': 'delta_w', 'delta_k_norm_b': 'delta_w', 'delta_forget_bias': 'delta_w', 'delta_rel_bias_table': 'delta_w', 'delta_w_out': 'delta_w', 'delta_ffn2_norm': 'delta_w', 'delta_ffn2_w_in': 'delta_w', 'delta_ffn2_w_out': 'delta_w', 'new_m_ffn1_norm': 'new_m', 'new_m_ffn1_w_in': 'new_m', 'new_m_ffn1_w_out': 'new_m', 'new_m_mix_norm': 'new_m', 'new_m_w_in': 'new_m', 'new_m_q_norm_a': 'new_m', 'new_m_k_norm_a': 'new_m', 'new_m_q_norm_b': 'new_m', 'new_m_k_norm_b': 'new_m', 'new_m_forget_bias': 'new_m', 'new_m_rel_bias_table': 'new_m', 'new_m_w_out': 'new_m', 'new_m_ffn2_norm': 'new_m', 'new_m_ffn2_w_in': 'new_m', 'new_m_ffn2_w_out': 'new_m', 'new_v_ffn1_norm': 'new_v', 'new_v_ffn1_w_in': 'new_v', 'new_v_ffn1_w_out': 'new_v', 'new_v_mix_norm': 'new_v', 'new_v_w_in': 'new_v', 'new_v_q_norm_a': 'new_v', 'new_v_k_norm_a': 'new_v', 'new_v_q_norm_b': 'new_v', 'new_v_k_norm_b': 'new_v', 'new_v_forget_bias': 'new_v', 'new_v_rel_bias_table': 'new_v', 'new_v_w_out': 'new_v', 'new_v_ffn2_norm': 'new_v', 'new_v_ffn2_w_in': 'new_v', 'new_v_ffn2_w_out': 'new_v'}


def _forward(args):
    return _fwd_reference(*[args[k] for k in FWD_PARAMS])


def _output_shape():
    def fwd():
        inp = _fwd_setup_inputs(0)
        return _fwd_reference(*[inp[k] for k in FWD_PARAMS])
    out = _jax.eval_shape(fwd)
    return out.shape, out.dtype

N_MICROBATCH = 1
ADAM_LR = 0.001
ADAM_B1 = 0.9
ADAM_B2 = 0.999
ADAM_EPS = 1e-08
ADAM_WD = 0.01
ADAM_STEP = 10
PER_EXAMPLE_BATCH_AXIS = {'x': 0, 'loss_target': 0}
SHARED_INPUTS = []
_WEIGHT_DTYPES = {'ffn1_norm': _jnp.float32, 'ffn1_w_in': _jnp.float32, 'ffn1_w_out': _jnp.float32, 'mix_norm': _jnp.float32, 'w_in': _jnp.float32, 'q_norm_a': _jnp.float32, 'k_norm_a': _jnp.float32, 'q_norm_b': _jnp.float32, 'k_norm_b': _jnp.float32, 'forget_bias': _jnp.float32, 'rel_bias_table': _jnp.float32, 'w_out': _jnp.float32, 'ffn2_norm': _jnp.float32, 'ffn2_w_in': _jnp.float32, 'ffn2_w_out': _jnp.float32}
MOMENT_SCALE = {'ffn1_norm': 3.088246e+00, 'ffn1_w_in': 3.709881e-02, 'ffn1_w_out': 6.118581e-02, 'mix_norm': 1.295787e+00, 'w_in': 7.016959e-02, 'q_norm_a': 1.357112e+00, 'k_norm_a': 1.364066e+00, 'q_norm_b': 6.361924e+00, 'k_norm_b': 6.368581e+00, 'forget_bias': 8.971146e+01, 'rel_bias_table': 4.627441e-01, 'w_out': 5.376305e-02, 'ffn2_norm': 3.094942e+00, 'ffn2_w_in': 2.990260e-02, 'ffn2_w_out': 5.021646e-02}


def _to_microbatches(a, axis):
    t = _jnp.moveaxis(a, axis, 0)
    t = t.reshape((N_MICROBATCH, t.shape[0] // N_MICROBATCH) + t.shape[1:])
    return _jnp.moveaxis(t, 1, axis + 1)


def setup_inputs(seed: int = 0) -> dict:
    inp = _fwd_setup_inputs(seed)
    key = _jax.random.fold_in(_jax.random.key(seed), 7919)
    shape, _ = _output_shape()
    out = dict(inp)
    out["loss_target"] = _jax.random.normal(_jax.random.fold_in(key, 0), shape, _jnp.float32)
    for i, name in enumerate(TWIN_WEIGHTS):
        w = inp[name].astype(_jnp.float32)
        if MOMENT_SCALE is None:
            s = _jnp.sqrt(_jnp.mean(_jnp.square(w)) + 1e-30)
        else:
            s = MOMENT_SCALE[name]
        km, kv = _jax.random.split(_jax.random.fold_in(key, i + 1))
        out[name] = w
        out["m_" + name] = s * _jax.random.normal(km, w.shape, _jnp.float32)
        out["v_" + name] = (s * s) * _jax.random.uniform(kv, w.shape, _jnp.float32, 0.5, 1.5)
    if N_MICROBATCH > 1:
        for name, axis in PER_EXAMPLE_BATCH_AXIS.items():
            out[name] = _to_microbatches(out[name], axis)
    return {'x': out['x'], 'ffn1_norm': out['ffn1_norm'], 'ffn1_w_in': out['ffn1_w_in'], 'ffn1_w_out': out['ffn1_w_out'], 'mix_norm': out['mix_norm'], 'w_in': out['w_in'], 'q_norm_a': out['q_norm_a'], 'k_norm_a': out['k_norm_a'], 'q_norm_b': out['q_norm_b'], 'k_norm_b': out['k_norm_b'], 'forget_bias': out['forget_bias'], 'rel_bias_table': out['rel_bias_table'], 'w_out': out['w_out'], 'ffn2_norm': out['ffn2_norm'], 'ffn2_w_in': out['ffn2_w_in'], 'ffn2_w_out': out['ffn2_w_out'], 'loss_target': out['loss_target'], 'm_ffn1_norm': out['m_ffn1_norm'], 'm_ffn1_w_in': out['m_ffn1_w_in'], 'm_ffn1_w_out': out['m_ffn1_w_out'], 'm_mix_norm': out['m_mix_norm'], 'm_w_in': out['m_w_in'], 'm_q_norm_a': out['m_q_norm_a'], 'm_k_norm_a': out['m_k_norm_a'], 'm_q_norm_b': out['m_q_norm_b'], 'm_k_norm_b': out['m_k_norm_b'], 'm_forget_bias': out['m_forget_bias'], 'm_rel_bias_table': out['m_rel_bias_table'], 'm_w_out': out['m_w_out'], 'm_ffn2_norm': out['m_ffn2_norm'], 'm_ffn2_w_in': out['m_ffn2_w_in'], 'm_ffn2_w_out': out['m_ffn2_w_out'], 'v_ffn1_norm': out['v_ffn1_norm'], 'v_ffn1_w_in': out['v_ffn1_w_in'], 'v_ffn1_w_out': out['v_ffn1_w_out'], 'v_mix_norm': out['v_mix_norm'], 'v_w_in': out['v_w_in'], 'v_q_norm_a': out['v_q_norm_a'], 'v_k_norm_a': out['v_k_norm_a'], 'v_q_norm_b': out['v_q_norm_b'], 'v_k_norm_b': out['v_k_norm_b'], 'v_forget_bias': out['v_forget_bias'], 'v_rel_bias_table': out['v_rel_bias_table'], 'v_w_out': out['v_w_out'], 'v_ffn2_norm': out['v_ffn2_norm'], 'v_ffn2_w_in': out['v_ffn2_w_in'], 'v_ffn2_w_out': out['v_ffn2_w_out']}


def _loss(weights, diff, rest, loss_target):
    with _jax.named_scope("forward"):
        args = {**rest, TWIN_DIFF_INPUT: diff, **{k: w.astype(_WEIGHT_DTYPES[k]) for k, w in weights.items()}}
        y = _forward(args)
    with _jax.named_scope("loss_head"):
        err = _jnp.square(y.astype(_jnp.float32) - loss_target)
        return 0.5 * _jnp.sum(_jnp.mean(err, axis=-1)) if err.ndim else 0.5 * err


def _adamw(w, g, m, v):
    m = ADAM_B1 * m + (1.0 - ADAM_B1) * g
    v = ADAM_B2 * v + (1.0 - ADAM_B2) * _jnp.square(g)
    m_hat = m / (1.0 - ADAM_B1 ** ADAM_STEP)
    v_hat = v / (1.0 - ADAM_B2 ** ADAM_STEP)
    delta = -ADAM_LR * (m_hat / (_jnp.sqrt(v_hat) + ADAM_EPS) + ADAM_WD * w)
    return delta, m, v


def reference(x, ffn1_norm, ffn1_w_in, ffn1_w_out, mix_norm, w_in, q_norm_a, k_norm_a, q_norm_b, k_norm_b, forget_bias, rel_bias_table, w_out, ffn2_norm, ffn2_w_in, ffn2_w_out, loss_target, m_ffn1_norm, m_ffn1_w_in, m_ffn1_w_out, m_mix_norm, m_w_in, m_q_norm_a, m_k_norm_a, m_q_norm_b, m_k_norm_b, m_forget_bias, m_rel_bias_table, m_w_out, m_ffn2_norm, m_ffn2_w_in, m_ffn2_w_out, v_ffn1_norm, v_ffn1_w_in, v_ffn1_w_out, v_mix_norm, v_w_in, v_q_norm_a, v_k_norm_a, v_q_norm_b, v_k_norm_b, v_forget_bias, v_rel_bias_table, v_w_out, v_ffn2_norm, v_ffn2_w_in, v_ffn2_w_out):
    given = dict(x=x, ffn1_norm=ffn1_norm, ffn1_w_in=ffn1_w_in, ffn1_w_out=ffn1_w_out, mix_norm=mix_norm, w_in=w_in, q_norm_a=q_norm_a, k_norm_a=k_norm_a, q_norm_b=q_norm_b, k_norm_b=k_norm_b, forget_bias=forget_bias, rel_bias_table=rel_bias_table, w_out=w_out, ffn2_norm=ffn2_norm, ffn2_w_in=ffn2_w_in, ffn2_w_out=ffn2_w_out, loss_target=loss_target, m_ffn1_norm=m_ffn1_norm, m_ffn1_w_in=m_ffn1_w_in, m_ffn1_w_out=m_ffn1_w_out, m_mix_norm=m_mix_norm, m_w_in=m_w_in, m_q_norm_a=m_q_norm_a, m_k_norm_a=m_k_norm_a, m_q_norm_b=m_q_norm_b, m_k_norm_b=m_k_norm_b, m_forget_bias=m_forget_bias, m_rel_bias_table=m_rel_bias_table, m_w_out=m_w_out, m_ffn2_norm=m_ffn2_norm, m_ffn2_w_in=m_ffn2_w_in, m_ffn2_w_out=m_ffn2_w_out, v_ffn1_norm=v_ffn1_norm, v_ffn1_w_in=v_ffn1_w_in, v_ffn1_w_out=v_ffn1_w_out, v_mix_norm=v_mix_norm, v_w_in=v_w_in, v_q_norm_a=v_q_norm_a, v_k_norm_a=v_k_norm_a, v_q_norm_b=v_q_norm_b, v_k_norm_b=v_k_norm_b, v_forget_bias=v_forget_bias, v_rel_bias_table=v_rel_bias_table, v_w_out=v_w_out, v_ffn2_norm=v_ffn2_norm, v_ffn2_w_in=v_ffn2_w_in, v_ffn2_w_out=v_ffn2_w_out)
    weights = {n: given[n] for n in TWIN_WEIGHTS}
    shared = {n: given[n] for n in SHARED_INPUTS}
    per_example = {n: given[n] for n in ['x']}
    grad_fn = _jax.value_and_grad(_loss, argnums=(0, 1))

    def one_microbatch(ex, loss_target):
        ex = dict(ex)
        diff = ex.pop(TWIN_DIFF_INPUT)
        return grad_fn(weights, diff, {**shared, **ex}, loss_target)

    if N_MICROBATCH == 1:
        loss, (grad_w, grad_x) = one_microbatch(per_example, given["loss_target"])
    else:
        def body(carry, xs):
            loss_sum, grad_sum = carry
            l_k, (gw_k, gx_k) = one_microbatch(xs[0], xs[1])
            with _jax.named_scope("update"):
                return (loss_sum + l_k, _jax.tree.map(_jnp.add, grad_sum, gw_k)), gx_k

        init = (_jnp.zeros((), _jnp.float32), _jax.tree.map(_jnp.zeros_like, weights))
        (loss, grad_w), grad_x = _jax.lax.scan(body, init, (per_example, given["loss_target"]))
    with _jax.named_scope("update"):
        delta_w, new_m, new_v = {}, {}, {}
        for n in TWIN_WEIGHTS:
            delta_w[n], new_m[n], new_v[n] = _adamw(weights[n], grad_w[n], given["m_" + n], given["v_" + n])
    return (loss, grad_x, *[grad_w[n] for n in TWIN_WEIGHTS], *[delta_w[n] for n in TWIN_WEIGHTS],
            *[new_m[n] for n in TWIN_WEIGHTS], *[new_v[n] for n in TWIN_WEIGHTS])
```

```python
import functools
import math

import numpy as np
import jax
import jax.numpy as jnp
from jax import lax
from jax.experimental import pallas as pl
from jax.experimental.pallas import tpu as pltpu

F32, BF16 = jnp.float32, jnp.bfloat16
HEAD = 128
NSLAB = 16
BLK = 128
DILATIONS = (1, 4, 16)
NUM_BUCKETS, MAX_DISTANCE = 32, 2048
RMS_EPS = 1e-6
NEG = -1e30
SCALE = HEAD ** -0.5
LR, B1, B2, EPS, WD, STEP = 0.001, 0.9, 0.999, 1e-08, 0.01, 10
N_DEV = 8
VMEM_LIMIT_BYTES = 56 << 20
MESH = pl.DeviceIdType.MESH


def _cparams(**kw):
    return pltpu.CompilerParams(vmem_limit_bytes=VMEM_LIMIT_BYTES, **kw)


def _nn(a, b):
    return jnp.dot(a, b, preferred_element_type=F32)


def _nt(a, b):
    return lax.dot_general(a, b, (((1,), (1,)), ((), ())), preferred_element_type=F32)


def _tn(a, b):
    return lax.dot_general(a, b, (((0,), (0,)), ((), ())), preferred_element_type=F32)


def _sds(shape, dtype):
    return jax.ShapeDtypeStruct(shape, dtype)


def _ffn_fwd(x, g, win, wout, tm, name):
    T, D = x.shape
    nc, tf = wout.shape[0], wout.shape[1]

    def body(x_ref, g_ref, win_ref, wout_ref, y_ref, h_ref, gu_ref):
        j = pl.program_id(1)

        @pl.when(j == 0)
        def _():
            xv = x_ref[...]
            r = lax.rsqrt(jnp.mean(xv * xv, axis=-1, keepdims=True) + RMS_EPS)
            h_ref[...] = (xv * r * g_ref[...]).astype(BF16)
            y_ref[...] = jnp.zeros_like(y_ref)

        hb = h_ref[...]
        gt = _nt(hb, win_ref[0])
        up = _nt(hb, win_ref[1])
        gu_ref[0] = gt.astype(BF16)
        gu_ref[1] = up.astype(BF16)
        act = (gt * jax.nn.sigmoid(gt) * up).astype(BF16)
        y_ref[...] += _nn(act, wout_ref[...])

        @pl.when(j == nc - 1)
        def _():
            y_ref[...] = x_ref[...] + 0.5 * y_ref[...]

    return pl.pallas_call(
        body, name=name, grid=(T // tm, nc),
        in_specs=[pl.BlockSpec((tm, D), lambda i, j: (i, 0)),
                  pl.BlockSpec((1, D), lambda i, j: (0, 0)),
                  pl.BlockSpec((2, None, tf, D), lambda i, j: (0, j, 0, 0)),
                  pl.BlockSpec((None, tf, D), lambda i, j: (j, 0, 0))],
        out_specs=[pl.BlockSpec((tm, D), lambda i, j: (i, 0)),
                   pl.BlockSpec((tm, D), lambda i, j: (i, 0)),
                   pl.BlockSpec((2, None, tm, tf), lambda i, j: (0, j, i, 0))],
        out_shape=[_sds((T, D), F32), _sds((T, D), BF16), _sds((2, nc, T, tf), BF16)],
        compiler_params=_cparams(),
    )(x, g, win, wout)


def _ffn_bwd(dy, gu, win, wout, tm, name):
    T, D = dy.shape
    nc, tf = wout.shape[0], wout.shape[1]

    def body(dy_ref, gu_ref, win_ref, wout_ref, dh_ref, dgu_ref, act_ref):
        j = pl.program_id(1)

        @pl.when(j == 0)
        def _():
            dh_ref[...] = jnp.zeros_like(dh_ref)

        dyb = (0.5 * dy_ref[...]).astype(BF16)
        dact = _nt(dyb, wout_ref[...])
        gt = gu_ref[0].astype(F32)
        up = gu_ref[1].astype(F32)
        s = jax.nn.sigmoid(gt)
        silu = gt * s
        dgb = (dact * up * (s * (1.0 + gt * (1.0 - s)))).astype(BF16)
        dub = (dact * silu).astype(BF16)
        dgu_ref[0] = dgb
        dgu_ref[1] = dub
        act_ref[...] = (0.5 * silu * up).astype(BF16)
        dh_ref[...] += _nn(dgb, win_ref[0]) + _nn(dub, win_ref[1])

    return pl.pallas_call(
        body, name=name, grid=(T // tm, nc),
        in_specs=[pl.BlockSpec((tm, D), lambda i, j: (i, 0)),
                  pl.BlockSpec((2, None, tm, tf), lambda i, j: (0, j, i, 0)),
                  pl.BlockSpec((2, None, tf, D), lambda i, j: (0, j, 0, 0)),
                  pl.BlockSpec((None, tf, D), lambda i, j: (j, 0, 0))],
        out_specs=[pl.BlockSpec((tm, D), lambda i, j: (i, 0)),
                   pl.BlockSpec((2, None, tm, tf), lambda i, j: (0, j, i, 0)),
                   pl.BlockSpec((None, tm, tf), lambda i, j: (j, i, 0))],
        out_shape=[_sds((T, D), F32), _sds((2, nc, T, tf), BF16), _sds((nc, T, tf), BF16)],
        compiler_params=_cparams(),
    )(dy, gu, win, wout)


def _rms_bwd(x, g, dh, dres, tm, name):
    T, D = x.shape

    def body(x_ref, g_ref, dh_ref, dres_ref, dx_ref, dg_ref):
        @pl.when(pl.program_id(0) == 0)
        def _():
            dg_ref[...] = jnp.zeros_like(dg_ref)

        xv = x_ref[...]
        r = lax.rsqrt(jnp.mean(xv * xv, axis=-1, keepdims=True) + RMS_EPS)
        xhat = xv * r
        dh = dh_ref[...]
        gd = dh * g_ref[...]
        dx_ref[...] = dres_ref[...] + r * (gd - xhat * jnp.mean(gd * xhat, axis=-1, keepdims=True))
        dg_ref[...] += jnp.sum(dh * xhat, axis=0, keepdims=True)

    row = pl.BlockSpec((tm, D), lambda i: (i, 0))
    one = pl.BlockSpec((1, D), lambda i: (0, 0))
    return pl.pallas_call(
        body, name=name, grid=(T // tm,), in_specs=[row, one, row, row], out_specs=[row, one],
        out_shape=[_sds((T, D), F32), _sds((1, D), F32)], compiler_params=_cparams(),
    )(x, g, dh, dres)


def _mm_tn(a, b, *, bm, bn, bt, name, b_slabs=False):
    nz, T, M = a.shape
    if b_slabs:
        N = b.shape[1] // NSLAB
        assert bt == T // NSLAB
        b_spec = pl.BlockSpec((bt, bn), lambda z, m, n, t: (0, t * (N // bn) + n))
    else:
        N = b.shape[1]
        b_spec = pl.BlockSpec((bt, bn), lambda z, m, n, t: (t, n))
    assert M % bm == 0 and N % bn == 0 and T % bt == 0, (M, bm, N, bn, T, bt)

    def body(a_ref, b_ref, c_ref):
        @pl.when(pl.program_id(3) == 0)
        def _():
            c_ref[...] = jnp.zeros_like(c_ref)

        c_ref[...] += _tn(a_ref[...].astype(BF16), b_ref[...].astype(BF16))

    return pl.pallas_call(
        body, name=name, grid=(nz, M // bm, N // bn, T // bt),
        in_specs=[pl.BlockSpec((None, bt, bm), lambda z, m, n, t: (z, t, m)), b_spec],
        out_specs=pl.BlockSpec((None, bm, bn), lambda z, m, n, t: (z, m, n)),
        out_shape=_sds((nz, M, N), F32), compiler_params=_cparams(),
    )(a, b)


def _proj(x, g, w, gains, modes, *, slabs, tm, normed_dtype, name):
    T, D = x.shape
    N = w.shape[1]
    ntile = len(modes)
    tn = N // ntile
    n16 = T // NSLAB
    if slabs:
        assert tm == n16
        x_in = x.reshape(n16, NSLAB * D)
        x_spec = pl.BlockSpec((tm, D), lambda i, n: (0, i))
        oshape = lambda c: (NSLAB, n16, c)
        ospec = lambda bc, cm: pl.BlockSpec((None, tm, bc), lambda i, n: (i, 0, cm(n)))
    else:
        x_in = x
        x_spec = pl.BlockSpec((tm, D), lambda i, n: (i, 0))
        oshape = lambda c: (T, c)
        ospec = lambda bc, cm: pl.BlockSpec((tm, bc), lambda i, n: (i, cm(n)))

    def body(x_ref, g_ref, w_ref, gains_ref, raw_ref, nrm_ref, h_ref):
        n = pl.program_id(1)

        @pl.when(n == 0)
        def _():
            xv = x_ref[...]
            r = lax.rsqrt(jnp.mean(xv * xv, axis=-1, keepdims=True) + RMS_EPS)
            h_ref[...] = (xv * r * g_ref[...]).astype(BF16)

        y = _nn(h_ref[...], w_ref[...])
        raw_ref[...] = y
        for t, mode in enumerate(modes):
            @pl.when(n == t)
            def _(t=t, mode=mode):
                if not mode:
                    nrm_ref[...] = y.astype(nrm_ref.dtype)
                    return
                gain = gains_ref[t]
                for k in range(tn // HEAD):
                    yk = y[:, k * HEAD:(k + 1) * HEAD]
                    r = lax.rsqrt(jnp.mean(yk * yk, axis=-1, keepdims=True) + RMS_EPS)
                    nrm_ref[:, k * HEAD:(k + 1) * HEAD] = (yk * r * gain).astype(nrm_ref.dtype)

    return pl.pallas_call(
        body, name=name, grid=(T // tm, ntile),
        in_specs=[x_spec, pl.BlockSpec((1, D), lambda i, n: (0, 0)), pl.BlockSpec((D, tn), lambda i, n: (0, n)),
                  pl.BlockSpec((ntile, 1, HEAD), lambda i, n: (0, 0, 0))],
        out_specs=[ospec(tn, lambda n: n), ospec(tn, lambda n: n), ospec(D, lambda n: 0)],
        out_shape=[_sds(oshape(N), F32), _sds(oshape(N), normed_dtype), _sds(oshape(D), BF16)],
        compiler_params=_cparams(),
    )(x_in, g, w, gains)


def _mm(a, w, *, nt, tk, tm, a_layout, out_layout, resid=None, name):
    if a_layout == "slab":
        T, K = a.shape[0] * a.shape[1], a.shape[2]
    else:
        T, K = a.shape
    N = w.shape[0] if nt else w.shape[1]
    n16 = T // NSLAB
    nk = K // tk

    def spec(layout, C, bc, colmap):
        if layout == "nat":
            return pl.BlockSpec((tm, bc), lambda i, k: (i, colmap(k)))
        assert tm == n16
        if layout == "slab":
            return pl.BlockSpec((None, tm, bc), lambda i, k: (i, 0, colmap(k)))
        assert bc == C
        return pl.BlockSpec((tm, C), lambda i, k: (0, i))

    a_in = a.reshape(n16, NSLAB * K) if a_layout == "view" else a
    w_spec = pl.BlockSpec((N, tk), lambda i, k: (0, k)) if nt else pl.BlockSpec((tk, N), lambda i, k: (k, 0))
    o_spec = spec(out_layout, N, N, lambda k: 0)
    oshape = {"nat": (T, N), "slab": (NSLAB, n16, N), "view": (n16, NSLAB * N)}[out_layout]
    has_resid = resid is not None

    def body(*refs):
        a_ref, w_ref = refs[0], refs[1]
        o_ref = refs[-1]
        k = pl.program_id(1)

        @pl.when(k == 0)
        def _():
            o_ref[...] = refs[2][...] if has_resid else jnp.zeros_like(o_ref)

        ab = a_ref[...].astype(BF16)
        o_ref[...] += _nt(ab, w_ref[...]) if nt else _nn(ab, w_ref[...])

    ins = [a_in, w]
    in_specs = [spec(a_layout, K, tk, lambda k: k), w_spec]
    if has_resid:
        ins.append(resid.reshape(n16, NSLAB * N) if out_layout == "view" else resid)
        in_specs.append(o_spec)
    out = pl.pallas_call(
        body, name=name, grid=(T // tm, nk), in_specs=in_specs, out_specs=o_spec,
        out_shape=_sds(oshape, F32), compiler_params=_cparams(),
    )(*ins)
    return out.reshape(T, N) if out_layout == "view" else out


def _log_sigmoid(z):
    return jnp.minimum(z, 0.0) - jnp.log(1.0 + jnp.exp(-jnp.abs(z)))


def _fox_gate_fwd(f_raw, fbias, name):
    T = f_raw.shape[0]
    cb = 256

    def body(f_ref, b_ref, c_ref):
        row = lax.broadcasted_iota(jnp.int32, (cb, cb), 0)
        col = lax.broadcasted_iota(jnp.int32, (cb, cb), 1)
        tri = (col <= row).astype(F32)
        carry = jnp.zeros((1, HEAD), F32)
        for i in range(T // cb):
            lf = _log_sigmoid(f_ref[i * cb:(i + 1) * cb, :] + b_ref[...])
            c = jnp.dot(tri, lf, preferred_element_type=F32, precision=lax.Precision.HIGHEST) + carry
            c_ref[i * cb:(i + 1) * cb, :] = c
            carry = c[cb - 1:cb, :]

    return pl.pallas_call(body, name=name, out_shape=_sds((T, HEAD), F32), compiler_params=_cparams())(f_raw, fbias)


def _fox_gate_bwd(f_raw, fbias, dc, name):
    T = f_raw.shape[0]
    cb = 256

    def body(f_ref, b_ref, dc_ref, df_ref, db_ref):
        row = lax.broadcasted_iota(jnp.int32, (cb, cb), 0)
        col = lax.broadcasted_iota(jnp.int32, (cb, cb), 1)
        tri = (col >= row).astype(F32)
        carry = jnp.zeros((1, HEAD), F32)
        dbias = jnp.zeros((1, HEAD), F32)
        for i in reversed(range(T // cb)):
            dlf = jnp.dot(tri, dc_ref[i * cb:(i + 1) * cb, :], preferred_element_type=F32,
                          precision=lax.Precision.HIGHEST) + carry
            carry = dlf[0:1, :]
            z = f_ref[i * cb:(i + 1) * cb, :] + b_ref[...]
            df = dlf * jax.nn.sigmoid(-z)
            df_ref[i * cb:(i + 1) * cb, :] = df
            dbias = dbias + jnp.sum(df, axis=0, keepdims=True)
        db_ref[...] = dbias

    return pl.pallas_call(body, name=name, out_shape=[_sds((T, HEAD), F32), _sds((1, HEAD), F32)],
                          compiler_params=_cparams())(f_raw, fbias, dc)


def _fox_fwd(qkv, c_col, c_row, tq, name):
    T = qkv.shape[0]
    H = qkv.shape[1] // (3 * HEAD)
    nq = T // tq

    def body(q_ref, k_ref, v_ref, cq_ref, ck_ref, o_ref, lse_ref, m_s, l_s, acc_s):
        qi, ki = pl.program_id(1), pl.program_id(2)

        @pl.when(ki == 0)
        def _():
            m_s[...] = jnp.full_like(m_s, NEG)
            l_s[...] = jnp.zeros_like(l_s)
            acc_s[...] = jnp.zeros_like(acc_s)

        @pl.when(ki <= qi)
        def _():
            s = _nt(q_ref[...], k_ref[...]) * SCALE + cq_ref[...] - ck_ref[...]
            qpos = qi * tq + lax.broadcasted_iota(jnp.int32, (tq, tq), 0)
            kpos = ki * tq + lax.broadcasted_iota(jnp.int32, (tq, tq), 1)
            s = jnp.where(kpos <= qpos, s, NEG)
            m_new = jnp.maximum(m_s[...], jnp.max(s, axis=-1, keepdims=True))
            alpha = jnp.exp(m_s[...] - m_new)
            p = jnp.exp(s - m_new)
            l_s[...] = alpha * l_s[...] + jnp.sum(p, axis=-1, keepdims=True)
            acc_s[...] = alpha * acc_s[...] + _nn(p.astype(BF16), v_ref[...])
            m_s[...] = m_new

        @pl.when(ki == nq - 1)
        def _():
            o_ref[...] = acc_s[...] / l_s[...]
            lse_ref[...] = m_s[...] + jnp.log(l_s[...])

    kmap = lambda off: (lambda h, qi, ki: (jnp.minimum(ki, qi), off + h))
    return pl.pallas_call(
        body, name=name, grid=(H, nq, nq),
        in_specs=[pl.BlockSpec((tq, HEAD), lambda h, qi, ki: (qi, h)),
                  pl.BlockSpec((tq, HEAD), kmap(H)),
                  pl.BlockSpec((tq, HEAD), kmap(2 * H)),
                  pl.BlockSpec((None, tq, 1), lambda h, qi, ki: (h, qi, 0)),
                  pl.BlockSpec((None, 1, tq), lambda h, qi, ki: (h, 0, jnp.minimum(ki, qi)))],
        out_specs=[pl.BlockSpec((tq, HEAD), lambda h, qi, ki: (qi, h)),
                   pl.BlockSpec((None, tq, 1), lambda h, qi, ki: (h, qi, 0))],
        out_shape=[_sds((T, H * HEAD), F32), _sds((H, T, 1), F32)],
        scratch_shapes=[pltpu.VMEM((tq, 1), F32), pltpu.VMEM((tq, 1), F32), pltpu.VMEM((tq, HEAD), F32)],
        compiler_params=_cparams(),
    )(qkv, qkv, qkv, c_col, c_row)


def _fox_bwd(qkv, c_col, c_row, out, dout, lse, tq, name):
    T = qkv.shape[0]
    H = qkv.shape[1] // (3 * HEAD)
    nq = T // tq

    def body(q_ref, k_ref, v_ref, cq_ref, ck_ref, o_ref, do_ref, lse_ref, dq_ref, dk_ref, dv_ref, dck_ref, dcq_ref):
        ki, qi = pl.program_id(1), pl.program_id(2)

        @pl.when((ki == 0) & (qi == 0))
        def _():
            dq_ref[...] = jnp.zeros_like(dq_ref)
            dcq_ref[...] = jnp.zeros_like(dcq_ref)

        @pl.when(qi == 0)
        def _():
            dk_ref[...] = jnp.zeros_like(dk_ref)
            dv_ref[...] = jnp.zeros_like(dv_ref)
            dck_ref[...] = jnp.zeros_like(dck_ref)

        @pl.when(qi >= ki)
        def _():
            q, k = q_ref[...], k_ref[...]
            s = _nt(q, k) * SCALE + cq_ref[...] - ck_ref[...]
            qpos = qi * tq + lax.broadcasted_iota(jnp.int32, (tq, tq), 0)
            kpos = ki * tq + lax.broadcasted_iota(jnp.int32, (tq, tq), 1)
            s = jnp.where(kpos <= qpos, s, NEG)
            p = jnp.exp(s - lse_ref[...])
            do = do_ref[...]
            delta = jnp.sum(do * o_ref[...], axis=-1, keepdims=True)
            dob = do.astype(BF16)
            dp = _nt(dob, v_ref[...])
            ds = p * (dp - delta)
            dsb = ds.astype(BF16)
            dv_ref[...] += _tn(p.astype(BF16), dob)
            dk_ref[...] += _tn(dsb, q) * SCALE
            rows = pl.ds(pl.multiple_of(qi * tq, tq), tq)
            dq_ref[rows, :] += _nn(dsb, k) * SCALE
            dck_ref[...] -= jnp.sum(ds, axis=0, keepdims=True)
            dcq_ref[rows, :] += jnp.sum(ds, axis=-1, keepdims=True)

    qmap = lambda h, ki, qi: (jnp.maximum(qi, ki), h)
    return pl.pallas_call(
        body, name=name, grid=(H, nq, nq),
        in_specs=[pl.BlockSpec((tq, HEAD), qmap),
                  pl.BlockSpec((tq, HEAD), lambda h, ki, qi: (ki, H + h)),
                  pl.BlockSpec((tq, HEAD), lambda h, ki, qi: (ki, 2 * H + h)),
                  pl.BlockSpec((None, tq, 1), lambda h, ki, qi: (h, jnp.maximum(qi, ki), 0)),
                  pl.BlockSpec((None, 1, tq), lambda h, ki, qi: (h, 0, ki)),
                  pl.BlockSpec((tq, HEAD), qmap),
                  pl.BlockSpec((tq, HEAD), qmap),
                  pl.BlockSpec((None, tq, 1), lambda h, ki, qi: (h, jnp.maximum(qi, ki), 0))],
        out_specs=[pl.BlockSpec((T, HEAD), lambda h, ki, qi: (0, h)),
                   pl.BlockSpec((tq, HEAD), lambda h, ki, qi: (ki, h)),
                   pl.BlockSpec((tq, HEAD), lambda h, ki, qi: (ki, h)),
                   pl.BlockSpec((None, 1, tq), lambda h, ki, qi: (h, 0, ki)),
                   pl.BlockSpec((None, T, 1), lambda h, ki, qi: (h, 0, 0))],
        out_shape=[_sds((T, H * HEAD), F32), _sds((T, H * HEAD), F32), _sds((T, H * HEAD), F32), _sds((H, 1, T), F32),
                   _sds((H, T, 1), F32)],
        compiler_params=_cparams(),
    )(qkv, qkv, qkv, c_col, c_row, out, dout, lse)


def _t5_bucket(dist):
    max_exact = NUM_BUCKETS // 2
    d = dist.astype(np.float32)
    large = max_exact + (np.log(np.maximum(d, np.float32(1.0)) / np.float32(max_exact))
                         / np.float32(math.log(MAX_DISTANCE / max_exact))
                         * np.float32(NUM_BUCKETS - max_exact)).astype(np.int32)
    large = np.minimum(large, NUM_BUCKETS - 1)
    return np.where(dist < max_exact, dist, large)


def _bucket_maps():
    maps = []
    for d in DILATIONS:
        e = NSLAB // d
        rows = BLK // e
        idx = np.arange(BLK)
        pos = e * (idx % rows) + idx // rows
        qpos = pos[:, None] + BLK
        kpos = np.concatenate([pos, pos + BLK])[None, :]
        delta = qpos - kpos
        band = (delta >= 0) & (delta <= BLK)
        bucket = _t5_bucket(np.clip(delta, 0, None) * d)
        maps.append(np.where(band, bucket, -1).astype(np.int32))
    return np.stack(maps)


def _dil_geometry(T):
    n16 = T // NSLAB
    geo = []
    for d in DILATIONS:
        e = NSLAB // d
        rows = BLK // e
        nblk = n16 // rows
        geo.append((d, e, rows, nblk))
    return geo


def _dil_bias(tab_ref, bkt_ref, bias_s, h):
    for p in range(len(DILATIONS)):
        bk = bkt_ref[p]
        bias = jnp.full((BLK, 2 * BLK), NEG, F32)
        for b in range(NUM_BUCKETS):
            bias = jnp.where(bk == b, tab_ref[b, h], bias)
        bias_s[p] = bias


def _dil_rows(d, e, rows, sub, blk):
    start = pl.multiple_of(blk * rows, rows)
    return [(sub + d * j, pl.ds(start, rows)) for j in range(e)]


def _gather(ref, idx):
    return jnp.concatenate([ref[s, r, :] for s, r in idx], axis=0)


def _scatter(ref, idx, val, rows):
    for j, (s, r) in enumerate(idx):
        ref[s, r, :] = val[j * rows:(j + 1) * rows]


def _scatter_add(ref, idx, val, rows):
    for j, (s, r) in enumerate(idx):
        ref[s, r, :] += val[j * rows:(j + 1) * rows]


def _dil_fwd(qkv, table, name):
    n16 = qkv.shape[1]
    T = NSLAB * n16
    H = qkv.shape[2] // (3 * HEAD)
    geo = _dil_geometry(T)
    bkt = jnp.asarray(_bucket_maps())

    def body(tab_ref, bkt_ref, q_ref, k_ref, v_ref, o_ref, lse_ref, bias_s, m_s, l_s):
        h = pl.program_id(0)
        _dil_bias(tab_ref, bkt_ref, bias_s, h)
        first_mask = lax.broadcasted_iota(jnp.int32, (BLK, 2 * BLK), 1) < BLK

        def block(p, d, e, rows, sub, blk):
            cur = _dil_rows(d, e, rows, sub, blk)
            prev = _dil_rows(d, e, rows, sub, jnp.maximum(blk - 1, 0))
            q = _gather(q_ref, cur).astype(BF16)
            kk = jnp.concatenate([_gather(k_ref, prev), _gather(k_ref, cur)], axis=0).astype(BF16)
            vv = jnp.concatenate([_gather(v_ref, prev), _gather(v_ref, cur)], axis=0).astype(BF16)
            s = _nt(q, kk) * SCALE + bias_s[p]
            s = jnp.where(first_mask & (blk == 0), NEG, s)
            m_blk = jnp.max(s, axis=-1, keepdims=True)
            if p == len(DILATIONS) - 1:
                m_new = m_blk
                pr = jnp.exp(s - m_new)
                l_new = jnp.sum(pr, axis=-1, keepdims=True)
                acc = _nn(pr.astype(BF16), vv)
            else:
                m_old = _gather(m_s, cur)
                m_new = jnp.maximum(m_old, m_blk)
                alpha = jnp.exp(m_old - m_new)
                pr = jnp.exp(s - m_new)
                l_new = alpha * _gather(l_s, cur) + jnp.sum(pr, axis=-1, keepdims=True)
                acc = alpha * _gather(o_ref, cur) + _nn(pr.astype(BF16), vv)
            if p == 0:
                _scatter(o_ref, cur, acc / l_new, rows)
                _scatter(lse_ref, cur, m_new + jnp.log(l_new), rows)
            else:
                _scatter(o_ref, cur, acc, rows)
                _scatter(m_s, cur, m_new, rows)
                _scatter(l_s, cur, l_new, rows)

        for p in reversed(range(len(DILATIONS))):
            d, e, rows, nblk = geo[p]

            def step(i, carry, p=p, d=d, e=e, rows=rows, nblk=nblk):
                block(p, d, e, rows, i // nblk, i % nblk)
                return carry

            lax.fori_loop(0, d * nblk, step, 0)

    head = lambda off: pl.BlockSpec((NSLAB, n16, HEAD), lambda h: (0, 0, off + h))
    return pl.pallas_call(
        body, name=name, grid=(H,),
        in_specs=[pl.BlockSpec(memory_space=pltpu.SMEM), pl.BlockSpec((3, BLK, 2 * BLK), lambda h: (0, 0, 0)),
                  head(0), head(H), head(2 * H)],
        out_specs=[head(0), pl.BlockSpec((None, NSLAB, n16, 1), lambda h: (h, 0, 0, 0))],
        out_shape=[_sds((NSLAB, n16, H * HEAD), F32), _sds((H, NSLAB, n16, 1), F32)],
        scratch_shapes=[pltpu.VMEM((3, BLK, 2 * BLK), F32), pltpu.VMEM((NSLAB, n16, 1), F32),
                        pltpu.VMEM((NSLAB, n16, 1), F32)],
        compiler_params=_cparams(),
    )(table, bkt, qkv, qkv, qkv)


def _dil_bwd(qkv, table, out, dout, lse, name):
    n16 = qkv.shape[1]
    T = NSLAB * n16
    H = qkv.shape[2] // (3 * HEAD)
    geo = _dil_geometry(T)
    bkt = jnp.asarray(_bucket_maps())

    def body(tab_ref, bkt_ref, q_ref, k_ref, v_ref, o_ref, do_ref, lse_ref,
             dq_ref, dk_ref, dv_ref, dtab_ref, bias_s, dbias_s, delta_s):
        h = pl.program_id(0)
        _dil_bias(tab_ref, bkt_ref, bias_s, h)
        first_mask = lax.broadcasted_iota(jnp.int32, (BLK, 2 * BLK), 1) < BLK
        dbias_s[...] = jnp.zeros_like(dbias_s)
        dq_ref[...] = jnp.zeros_like(dq_ref)
        dk_ref[...] = jnp.zeros_like(dk_ref)
        dv_ref[...] = jnp.zeros_like(dv_ref)
        for r in range(NSLAB):
            delta_s[r] = jnp.sum(do_ref[r] * o_ref[r], axis=-1, keepdims=True)

        def block(p, d, e, rows, sub, blk):
            cur = _dil_rows(d, e, rows, sub, blk)
            prev = _dil_rows(d, e, rows, sub, jnp.maximum(blk - 1, 0))
            q = _gather(q_ref, cur).astype(BF16)
            kk = jnp.concatenate([_gather(k_ref, prev), _gather(k_ref, cur)], axis=0).astype(BF16)
            vv = jnp.concatenate([_gather(v_ref, prev), _gather(v_ref, cur)], axis=0).astype(BF16)
            dob = _gather(do_ref, cur).astype(BF16)
            s = _nt(q, kk) * SCALE + bias_s[p]
            s = jnp.where(first_mask & (blk == 0), NEG, s)
            pr = jnp.exp(s - _gather(lse_ref, cur))
            dp = _nt(dob, vv)
            ds = pr * (dp - _gather(delta_s, cur))
            dsb = ds.astype(BF16)
            dbias_s[p] += ds
            dvv = _tn(pr.astype(BF16), dob)
            dkk = _tn(dsb, q) * SCALE
            _scatter_add(dq_ref, cur, _nn(dsb, kk) * SCALE, rows)
            _scatter_add(dk_ref, prev, dkk[:BLK], rows)
            _scatter_add(dk_ref, cur, dkk[BLK:], rows)
            _scatter_add(dv_ref, prev, dvv[:BLK], rows)
            _scatter_add(dv_ref, cur, dvv[BLK:], rows)

        for p in range(len(DILATIONS)):
            d, e, rows, nblk = geo[p]

            def step(i, carry, p=p, d=d, e=e, rows=rows, nblk=nblk):
                block(p, d, e, rows, i // nblk, i % nblk)
                return carry

            lax.fori_loop(0, d * nblk, step, 0)

        lane = lax.broadcasted_iota(jnp.int32, (1, HEAD), 1)
        row = jnp.zeros((1, HEAD), F32)
        for b in range(NUM_BUCKETS):
            tot = jnp.zeros((1, 1), F32)
            for p in range(len(DILATIONS)):
                hit = jnp.where(bkt_ref[p] == b, dbias_s[p], 0.0)
                tot = tot + jnp.sum(jnp.sum(hit, axis=0, keepdims=True), axis=1, keepdims=True)
            row = jnp.where(lane == b, tot, row)
        dtab_ref[...] = row

    head = lambda off: pl.BlockSpec((NSLAB, n16, HEAD), lambda h: (0, 0, off + h))
    return pl.pallas_call(
        body, name=name, grid=(H,),
        in_specs=[pl.BlockSpec(memory_space=pltpu.SMEM), pl.BlockSpec((3, BLK, 2 * BLK), lambda h: (0, 0, 0)),
                  head(0), head(H), head(2 * H), head(0), head(0),
                  pl.BlockSpec((None, NSLAB, n16, 1), lambda h: (h, 0, 0, 0))],
        out_specs=[head(0), head(0), head(0), pl.BlockSpec((None, 1, HEAD), lambda h: (h, 0, 0))],
        out_shape=[_sds((NSLAB, n16, H * HEAD), F32)] * 3 + [_sds((H, 1, HEAD), F32)],
        scratch_shapes=[pltpu.VMEM((3, BLK, 2 * BLK), F32), pltpu.VMEM((3, BLK, 2 * BLK), F32),
                        pltpu.VMEM((NSLAB, n16, 1), F32)],
        compiler_params=_cparams(),
    )(table, bkt, qkv, qkv, qkv, out, dout, lse)


def _qknorm_bwd(raw, dq, dk, dv, gains, tm, name):
    T, N = raw.shape
    C = N // 3

    def body(raw_ref, dq_ref, dk_ref, dv_ref, gains_ref, dp_ref, dg_ref):
        @pl.when(pl.program_id(0) == 0)
        def _():
            dg_ref[...] = jnp.zeros_like(dg_ref)

        for t, d_ref in enumerate((dq_ref, dk_ref)):
            gain = gains_ref[t]
            dgain = jnp.zeros((1, HEAD), F32)
            for k in range(C // HEAD):
                y = raw_ref[:, t * C + k * HEAD:t * C + (k + 1) * HEAD]
                dn = d_ref[:, k * HEAD:(k + 1) * HEAD]
                r = lax.rsqrt(jnp.mean(y * y, axis=-1, keepdims=True) + RMS_EPS)
                yhat = y * r
                gd = dn * gain
                dy = r * (gd - yhat * jnp.mean(gd * yhat, axis=-1, keepdims=True))
                dp_ref[:, t * C + k * HEAD:t * C + (k + 1) * HEAD] = dy.astype(BF16)
                dgain = dgain + jnp.sum(dn * yhat, axis=0, keepdims=True)
            dg_ref[t] += dgain
        dp_ref[:, 2 * C:] = dv_ref[...].astype(BF16)

    third = pl.BlockSpec((tm, C), lambda i: (i, 0))
    return pl.pallas_call(
        body, name=name, grid=(T // tm,),
        in_specs=[pl.BlockSpec((tm, N), lambda i: (i, 0)), third, third, third,
                  pl.BlockSpec((2, 1, HEAD), lambda i: (0, 0, 0))],
        out_specs=[pl.BlockSpec((tm, N), lambda i: (i, 0)), pl.BlockSpec((2, 1, HEAD), lambda i: (0, 0, 0))],
        out_shape=[_sds((T, N), BF16), _sds((2, 1, HEAD), F32)], compiler_params=_cparams(),
    )(raw, dq, dk, dv, gains)


def _loss_grad(y, target, tm, name):
    T, D = y.shape

    def body(y_ref, t_ref, dy_ref, loss_ref):
        @pl.when(pl.program_id(0) == 0)
        def _():
            loss_ref[...] = jnp.zeros_like(loss_ref)

        err = y_ref[...] - t_ref[...]
        dy_ref[...] = err * (1.0 / D)
        per_tok = jnp.mean(err * err, axis=-1, keepdims=True)
        tot = 0.5 * jnp.sum(per_tok, axis=0, keepdims=True)
        lane = lax.broadcasted_iota(jnp.int32, (1, HEAD), 1)
        loss_ref[...] += jnp.where(lane == 0, tot, 0.0)

    row = pl.BlockSpec((tm, D), lambda i: (i, 0))
    return pl.pallas_call(
        body, name=name, grid=(T // tm,), in_specs=[row, row],
        out_specs=[row, pl.BlockSpec((1, HEAD), lambda i: (0, 0))],
        out_shape=[_sds((T, D), F32), _sds((1, HEAD), F32)], compiler_params=_cparams(),
    )(y, target)


def _pad_lanes(v, width=HEAD):
    return jnp.pad(v, ((0, 0), (0, width - v.shape[1])))


def _local_step(x, target, small, wts):
    T, D = x.shape
    C = D // 2
    H = C // HEAD
    n16 = T // NSLAB
    tm = min(512, T)
    tq = min(512, T)
    bn = min(1024, D)
    g1, gm, g2 = small["ffn1_norm"], small["mix_norm"], small["ffn2_norm"]
    gains_a = jnp.stack([small["q_norm_a"], small["k_norm_a"], jnp.ones_like(small["q_norm_a"])])
    gains_b = jnp.stack([small["q_norm_b"], small["k_norm_b"], jnp.ones_like(small["q_norm_b"])])
    fbias = _pad_lanes(small["forget_bias"])
    table = small["rel_bias_table"]

    x1, h1, gu1 = _ffn_fwd(x, g1, wts["ffn1_in"], wts["ffn1_out"], tm, "ffn1_fwd")
    raw_a, nrm_a, h2a = _proj(x1, gm, wts["w_a"], gains_a, (True, True, False), slabs=True, tm=n16,
                              normed_dtype=F32, name="proj_a")
    raw_b, nrm_b, h2b = _proj(x1, gm, wts["w_b"], gains_b, (True, True, False), slabs=False, tm=tm,
                              normed_dtype=BF16, name="proj_b")
    f_raw, _, _ = _proj(x1, gm, wts["w_f"], gains_b[:1], (False,), slabs=False, tm=tm, normed_dtype=BF16,
                        name="proj_f")
    c = _fox_gate_fwd(f_raw, fbias, "fox_gate_fwd")
    c_heads = c[:, :H].T
    c_col, c_row = c_heads[:, :, None], c_heads[:, None, :]
    out_a, lse_a = _dil_fwd(nrm_a, table, "dil_fwd")
    out_b, lse_b = _fox_fwd(nrm_b, c_col, c_row, tq, "fox_fwd")
    w_o = wts["w_o"]
    x2a = _mm(out_a, w_o[:C], nt=False, tk=C, tm=n16, a_layout="slab", out_layout="view", resid=x1, name="out_a")
    x2 = _mm(out_b, w_o[C:], nt=False, tk=C, tm=tm, a_layout="nat", out_layout="nat", resid=x2a, name="out_b")
    y, h3, gu3 = _ffn_fwd(x2, g2, wts["ffn2_in"], wts["ffn2_out"], tm, "ffn2_fwd")
    dy, loss_row = _loss_grad(y, target, tm, "loss_grad")

    grads = {}

    def ffn_backward(tag, xin, g, h, gu, win, wout, dres):
        nc, tf = wout.shape[0], wout.shape[1]
        dh, dgu, acth = _ffn_bwd(dres, gu, win, wout, tm, tag + "_bwd")
        dxin, dg = _rms_bwd(xin, g, dh, dres, tm, tag + "_rms_bwd")
        grads[tag + "_norm"] = dg
        grads[tag + "_w_in_t"] = _mm_tn(dgu.reshape(2 * nc, T, tf), h, bm=tf, bn=bn, bt=min(1024, T), name=tag + "_dwin")
        grads[tag + "_w_out"] = _mm_tn(acth, dres, bm=tf, bn=bn, bt=min(1024, T), name=tag + "_dwout")
        return dxin

    dx2 = ffn_backward("ffn2", x2, g2, h3, gu3, wts["ffn2_in"], wts["ffn2_out"], dy)

    dmix_a = _mm(dx2, w_o[:C], nt=True, tk=D, tm=n16, a_layout="view", out_layout="slab", name="dmix_a")
    dmix_b = _mm(dx2, w_o[C:], nt=True, tk=D, tm=tm, a_layout="nat", out_layout="nat", name="dmix_b")
    dwo_a = _mm_tn(out_a.reshape(1, T, C), dx2.reshape(n16, NSLAB * D), bm=C, bn=bn, bt=n16, b_slabs=True, name="dwo_a")
    dwo_b = _mm_tn(out_b.reshape(1, T, C), dx2, bm=C, bn=bn, bt=tm, name="dwo_b")
    grads["w_out"] = jnp.concatenate([dwo_a[0], dwo_b[0]], axis=0)

    dqa, dka, dva, dtab = _dil_bwd(nrm_a, table, out_a, dmix_a, lse_a, "dil_bwd")
    dqb, dkb, dvb, dck, dcq = _fox_bwd(nrm_b, c_col, c_row, out_b, dmix_b, lse_b, tq, "fox_bwd")
    grads["rel_bias_table"] = dtab[:, 0, :NUM_BUCKETS].T
    dc = _pad_lanes((dck[:, 0, :] + dcq[:, :, 0]).T)
    df, dfb = _fox_gate_bwd(f_raw, fbias, dc, "fox_gate_bwd")
    grads["forget_bias"] = dfb[:, :H]

    flat = lambda a: a.reshape(T, a.shape[-1])
    dproj_a, dgain_a = _qknorm_bwd(flat(raw_a), flat(dqa), flat(dka), flat(dva), gains_a[:2], min(256, T), "qknorm_bwd_a")
    dproj_b, dgain_b = _qknorm_bwd(raw_b, dqb, dkb, dvb, gains_b[:2], min(256, T), "qknorm_bwd_b")
    grads["q_norm_a"], grads["k_norm_a"] = dgain_a[0], dgain_a[1]
    grads["q_norm_b"], grads["k_norm_b"] = dgain_b[0], dgain_b[1]
    dproj_a = dproj_a.reshape(NSLAB, n16, 3 * C)

    zero = jnp.zeros((T, D), F32)
    dh2 = _mm(dproj_a, wts["w_a"], nt=True, tk=C, tm=n16, a_layout="slab", out_layout="view", resid=zero, name="dh2_a")
    dh2 = _mm(dproj_b, wts["w_b"], nt=True, tk=C, tm=tm, a_layout="nat", out_layout="nat", resid=dh2, name="dh2_b")
    dh2 = _mm(df, wts["w_f"], nt=True, tk=HEAD, tm=tm, a_layout="nat", out_layout="nat", resid=dh2, name="dh2_f")
    dx1, grads["mix_norm"] = _rms_bwd(x1, gm, dh2, dx2, tm, "mix_rms_bwd")
    bt = min(512, T)
    dw_a = _mm_tn(flat(h2a)[None], flat(dproj_a), bm=bn, bn=min(1024, C), bt=bt, name="dw_a")[0]
    dw_b = _mm_tn(h2b[None], dproj_b, bm=bn, bn=min(1024, C), bt=bt, name="dw_b")[0]
    dw_f = _mm_tn(h2b[None], df, bm=bn, bn=HEAD, bt=bt, name="dw_f")[0]
    grads["w_in"] = jnp.concatenate([dw_a, dw_b, dw_f[:, :H]], axis=1)

    grad_x = ffn_backward("ffn1", x, g1, h1, gu1, wts["ffn1_in"], wts["ffn1_out"], dx1)
    return loss_row, grad_x, grads


ANY = pl.BlockSpec(memory_space=pl.ANY)


def _place():
    x, y, c = lax.axis_index("x"), lax.axis_index("y"), lax.axis_index("c")
    other_chips = [(1 - x, y), (x, 1 - y), (1 - x, 1 - y)]
    return x, y, c, other_chips


def _all_gather(shards, name):
    n = len(shards)

    def body(*refs):
        ins, outs = refs[:n], refs[n:2 * n]
        send_sems, recv_sems, local_sems = refs[2 * n:]
        x, y, c, chips = _place()
        me, sibling = (x, y, c), (x, y, 1 - c)

        def copy(a, k, block, to, src=None):
            px, py, pc = block
            dst = outs[a].at[4 * px + 2 * py + pc]
            return pltpu.make_async_remote_copy(
                src_ref=dst if src is None else src, dst_ref=dst, send_sem=send_sems.at[7 * a + k],
                recv_sem=recv_sems.at[7 * a + k], device_id=to, device_id_type=MESH)

        mine = [pltpu.make_async_copy(ins[a], outs[a].at[4 * x + 2 * y + c], local_sems.at[a]) for a in range(n)]
        for cp in mine:
            cp.start()
        first = []
        for a in range(n):
            first.append(copy(a, 0, me, sibling, src=ins[a]))
            first += [copy(a, 1 + j, me, (*chip, c), src=ins[a]) for j, chip in enumerate(chips)]
        for cp in first:
            cp.start()
        passed = []
        for a in range(n):
            for j, chip in enumerate(chips):
                copy(a, 1 + j, (*chip, c), me).wait_recv()
                fwd = copy(a, 4 + j, (*chip, c), sibling)
                fwd.start()
                passed.append(fwd)
        for a in range(n):
            copy(a, 0, sibling, me).wait_recv()
            for j, chip in enumerate(chips):
                copy(a, 4 + j, (*chip, 1 - c), me).wait_recv()
        for cp in first + passed:
            cp.wait_send()
        for cp in mine:
            cp.wait()

    return pl.pallas_call(
        body, name=name, in_specs=[ANY] * n, out_specs=[ANY] * n,
        out_shape=[_sds((N_DEV,) + s.shape, s.dtype) for s in shards],
        scratch_shapes=[pltpu.SemaphoreType.DMA((7 * n,)), pltpu.SemaphoreType.DMA((7 * n,)),
                        pltpu.SemaphoreType.DMA((n,))],
    )(*shards)


def _exchange_in_chip(gs, name):
    n = len(gs)

    def body(*refs):
        ins, outs = refs[:n], refs[n:2 * n]
        send_sems, recv_sems = refs[2 * n:]
        x, y, c, _ = _place()
        copies = [pltpu.make_async_remote_copy(
            src_ref=ins[a].at[2 * q + 1 - c], dst_ref=outs[a].at[q], send_sem=send_sems.at[4 * a + q],
            recv_sem=recv_sems.at[4 * a + q], device_id=(x, y, 1 - c), device_id_type=MESH)
            for a in range(n) for q in range(4)]
        for cp in copies:
            cp.start()
        for cp in copies:
            cp.wait()

    return pl.pallas_call(
        body, name=name, in_specs=[ANY] * n, out_specs=[ANY] * n,
        out_shape=[_sds((4,) + g.shape[1:], g.dtype) for g in gs],
        scratch_shapes=[pltpu.SemaphoreType.DMA((4 * n,)), pltpu.SemaphoreType.DMA((4 * n,))],
    )(*gs)


def _exchange_between_chips(ps, name):
    n = len(ps)

    def body(*refs):
        ins, outs = refs[:n], refs[n:2 * n]
        send_sems, recv_sems = refs[2 * n:]
        x, y, c, chips = _place()
        copies = [pltpu.make_async_remote_copy(
            src_ref=ins[a].at[2 * cx + cy], dst_ref=outs[a].at[j], send_sem=send_sems.at[3 * a + j],
            recv_sem=recv_sems.at[3 * a + j], device_id=(cx, cy, c), device_id_type=MESH)
            for a in range(n) for j, (cx, cy) in enumerate(chips)]
        for cp in copies:
            cp.start()
        for cp in copies:
            cp.wait()

    return pl.pallas_call(
        body, name=name, in_specs=[ANY] * n, out_specs=[ANY] * n,
        out_shape=[_sds((3,) + p.shape[1:], p.dtype) for p in ps],
        scratch_shapes=[pltpu.SemaphoreType.DMA((3 * n,)), pltpu.SemaphoreType.DMA((3 * n,))],
    )(*ps)


def _all_reduce_small(v, name):
    R = v.shape[0]

    def body(v_ref, sum_ref, all_ref, send_sems, recv_sems):
        x, y, c, _ = _place()
        k = 4 * x + 2 * y + c
        all_ref[k] = v_ref[...]
        copies = []
        for rel in range(1, N_DEV):
            fx, fy, fc = (rel >> 2) & 1, (rel >> 1) & 1, rel & 1
            peer = (1 - x if fx else x, 1 - y if fy else y, 1 - c if fc else c)
            copies.append(pltpu.make_async_remote_copy(
                src_ref=v_ref, dst_ref=all_ref.at[k], send_sem=send_sems.at[rel - 1], recv_sem=recv_sems.at[rel - 1],
                device_id=peer, device_id_type=MESH))
        for cp in copies:
            cp.start()
        for rel in range(1, N_DEV):
            fx, fy, fc = (rel >> 2) & 1, (rel >> 1) & 1, rel & 1
            src = 4 * (1 - x if fx else x) + 2 * (1 - y if fy else y) + (1 - c if fc else c)
            pltpu.make_async_remote_copy(
                src_ref=v_ref, dst_ref=all_ref.at[src], send_sem=send_sems.at[rel - 1], recv_sem=recv_sems.at[rel - 1],
                device_id=(x, y, c), device_id_type=MESH).wait_recv()
        for cp in copies:
            cp.wait_send()
        tot = all_ref[0]
        for d in range(1, N_DEV):
            tot = tot + all_ref[d]
        sum_ref[...] = tot

    vm = pl.BlockSpec(memory_space=pltpu.VMEM)
    return pl.pallas_call(
        body, name=name, in_specs=[vm], out_specs=[vm, vm],
        out_shape=[_sds((R, HEAD), F32), _sds((N_DEV, R, HEAD), F32)],
        scratch_shapes=[pltpu.SemaphoreType.DMA((N_DEV - 1,)), pltpu.SemaphoreType.DMA((N_DEV - 1,))],
    )(v)[0]


def _row_tile(rows):
    for cand in (688, 512, 256):
        if rows % cand == 0:
            return cand
    return rows


def _chip_sum(g, r1, core, name):
    _, R, Cc = g.shape
    tr = _row_tile(R)

    def body(core_ref, g_ref, r_ref, p_ref):
        p_ref[...] = (g_ref[...] + r_ref[...]).astype(BF16)

    blk = lambda f: pl.BlockSpec((None, tr, Cc), f)
    return pl.pallas_call(
        body, name=name,
        grid_spec=pltpu.PrefetchScalarGridSpec(
            num_scalar_prefetch=1, grid=(4, R // tr),
            in_specs=[blk(lambda q, i, core: (2 * q + core[0], i, 0)), blk(lambda q, i, core: (q, i, 0))],
            out_specs=blk(lambda q, i, core: (q, i, 0))),
        out_shape=_sds((4, R, Cc), BF16), compiler_params=_cparams(),
    )(core, g, r1)


def _final_sum(g, r1, r2, where, name):
    _, R, Cc = g.shape
    tr = _row_tile(R)

    def body(where_ref, g_ref, r1_ref, r2_ref, o_ref):
        o_ref[...] = ((g_ref[...] + r1_ref[...]) + r2_ref[0].astype(F32)) + (r2_ref[1].astype(F32) + r2_ref[2].astype(F32))

    return pl.pallas_call(
        body, name=name,
        grid_spec=pltpu.PrefetchScalarGridSpec(
            num_scalar_prefetch=1, grid=(R // tr,),
            in_specs=[pl.BlockSpec((None, tr, Cc), lambda i, w: (w[0], i, 0)),
                      pl.BlockSpec((None, tr, Cc), lambda i, w: (w[1], i, 0)),
                      pl.BlockSpec((3, tr, Cc), lambda i, w: (0, i, 0))],
            out_specs=pl.BlockSpec((tr, Cc), lambda i, w: (i, 0))),
        out_shape=_sds((R, Cc), F32), compiler_params=_cparams(),
    )(where, g, r1, r2)


def _adamw(w, g, m, v, name):
    R, Cc = w.shape
    tr = _row_tile(R)

    def body(w_ref, g_ref, m_ref, v_ref, d_ref, nm_ref, nv_ref):
        gv = g_ref[...]
        nm = B1 * m_ref[...] + (1.0 - B1) * gv
        nv = B2 * v_ref[...] + (1.0 - B2) * jnp.square(gv)
        m_hat = nm / (1.0 - B1 ** STEP)
        v_hat = nv / (1.0 - B2 ** STEP)
        d_ref[...] = -LR * (m_hat / (jnp.sqrt(v_hat) + EPS) + WD * w_ref[...])
        nm_ref[...] = nm
        nv_ref[...] = nv

    blk = pl.BlockSpec((tr, Cc), lambda i: (i, 0))
    return pl.pallas_call(
        body, name=name, grid=(R // tr,), in_specs=[blk] * 4, out_specs=[blk] * 3,
        out_shape=[_sds((R, Cc), F32)] * 3, compiler_params=_cparams(),
    )(w, g, m, v)


SMALL = ("ffn1_norm", "mix_norm", "ffn2_norm", "q_norm_a", "k_norm_a", "q_norm_b", "k_norm_b", "forget_bias",
         "rel_bias_table")
LARGE = ("ffn1_w_in", "ffn1_w_out", "w_in", "w_out", "ffn2_w_in", "ffn2_w_out")
ORDER = ("ffn1_norm", "ffn1_w_in", "ffn1_w_out", "mix_norm", "w_in", "q_norm_a", "k_norm_a", "q_norm_b", "k_norm_b",
         "forget_bias", "rel_bias_table", "w_out", "ffn2_norm", "ffn2_w_in", "ffn2_w_out")


def _pack_small(vals):
    rows = []
    for name in SMALL:
        flat = vals[name].reshape(-1)
        pad = (-flat.shape[0]) % HEAD
        rows.append(jnp.pad(flat, (0, pad)).reshape(-1, HEAD))
    return jnp.concatenate(rows, axis=0)


def _unpack_small(packed, like):
    out, r = {}, 0
    for name in SMALL:
        size = like[name].size
        nrow = -(-size // HEAD)
        out[name] = packed[r:r + nrow].reshape(-1)[:size].reshape(like[name].shape)
        r += nrow
    return out


def kernel(x, ffn1_norm, ffn1_w_in, ffn1_w_out, mix_norm, w_in, q_norm_a, k_norm_a, q_norm_b, k_norm_b, forget_bias, rel_bias_table, w_out, ffn2_norm, ffn2_w_in, ffn2_w_out, loss_target, m_ffn1_norm, m_ffn1_w_in, m_ffn1_w_out, m_mix_norm, m_w_in, m_q_norm_a, m_k_norm_a, m_q_norm_b, m_k_norm_b, m_forget_bias, m_rel_bias_table, m_w_out, m_ffn2_norm, m_ffn2_w_in, m_ffn2_w_out, v_ffn1_norm, v_ffn1_w_in, v_ffn1_w_out, v_mix_norm, v_w_in, v_q_norm_a, v_k_norm_a, v_q_norm_b, v_k_norm_b, v_forget_bias, v_rel_bias_table, v_w_out, v_ffn2_norm, v_ffn2_w_in, v_ffn2_w_out):
    w = dict(ffn1_norm=ffn1_norm, ffn1_w_in=ffn1_w_in, ffn1_w_out=ffn1_w_out, mix_norm=mix_norm, w_in=w_in,
             q_norm_a=q_norm_a, k_norm_a=k_norm_a, q_norm_b=q_norm_b, k_norm_b=k_norm_b, forget_bias=forget_bias,
             rel_bias_table=rel_bias_table, w_out=w_out, ffn2_norm=ffn2_norm, ffn2_w_in=ffn2_w_in, ffn2_w_out=ffn2_w_out)
    m = dict(ffn1_norm=m_ffn1_norm, ffn1_w_in=m_ffn1_w_in, ffn1_w_out=m_ffn1_w_out, mix_norm=m_mix_norm, w_in=m_w_in,
             q_norm_a=m_q_norm_a, k_norm_a=m_k_norm_a, q_norm_b=m_q_norm_b, k_norm_b=m_k_norm_b,
             forget_bias=m_forget_bias, rel_bias_table=m_rel_bias_table, w_out=m_w_out, ffn2_norm=m_ffn2_norm,
             ffn2_w_in=m_ffn2_w_in, ffn2_w_out=m_ffn2_w_out)
    v = dict(ffn1_norm=v_ffn1_norm, ffn1_w_in=v_ffn1_w_in, ffn1_w_out=v_ffn1_w_out, mix_norm=v_mix_norm, w_in=v_w_in,
             q_norm_a=v_q_norm_a, k_norm_a=v_k_norm_a, q_norm_b=v_q_norm_b, k_norm_b=v_k_norm_b,
             forget_bias=v_forget_bias, rel_bias_table=v_rel_bias_table, w_out=v_w_out, ffn2_norm=v_ffn2_norm,
             ffn2_w_in=v_ffn2_w_in, ffn2_w_out=v_ffn2_w_out)
    T, D = x.shape[1], x.shape[2]
    C = D // 2
    H = C // HEAD
    ff_shard = ffn1_w_out.shape[1]

    shards = [ffn1_w_in[0].T.astype(BF16), ffn1_w_out[0].astype(BF16), w_in[0].astype(BF16), w_out[0].astype(BF16),
              ffn2_w_in[0].T.astype(BF16), ffn2_w_out[0].astype(BF16)]
    f1i, f1o, wi, wo, f2i, f2o = _all_gather(shards, "gather_weights")
    wi = jnp.moveaxis(wi, 0, 1).reshape(D, -1)
    wts = dict(ffn1_in=f1i.reshape(2, N_DEV, ff_shard, D), ffn1_out=f1o,
               ffn2_in=f2i.reshape(2, N_DEV, ff_shard, D), ffn2_out=f2o,
               w_a=wi[:, :3 * C], w_b=wi[:, 3 * C:6 * C], w_f=_pad_lanes(wi[:, 6 * C:]), w_o=wo.reshape(2 * C, D))

    small = {name: w[name] for name in SMALL}
    loss_row, grad_x, grads = _local_step(x[0], loss_target[0], small, wts)

    gs = [grads["ffn1_w_in_t"].reshape(N_DEV, 2 * ff_shard, D), grads["ffn1_w_out"],
          jnp.moveaxis(grads["w_in"].reshape(D, N_DEV, -1), 1, 0), grads["w_out"].reshape(N_DEV, -1, D),
          grads["ffn2_w_in_t"].reshape(N_DEV, 2 * ff_shard, D), grads["ffn2_w_out"]]
    xi, yi, ci = lax.axis_index("x"), lax.axis_index("y"), lax.axis_index("c")
    core = jnp.reshape(ci, (1,)).astype(jnp.int32)
    where = jnp.stack([4 * xi + 2 * yi + ci, 2 * xi + yi]).astype(jnp.int32)
    r1 = _exchange_in_chip(gs, "reduce_in_chip")
    ps = [_chip_sum(g, r, core, "chip_sum_" + name) for g, r, name in zip(gs, r1, LARGE)]
    r2 = _exchange_between_chips(ps, "reduce_between_chips")
    g_large = {name: _final_sum(g, ra, rb, where, "final_sum_" + name)
               for g, ra, rb, name in zip(gs, r1, r2, LARGE)}
    g_large["ffn1_w_in"] = g_large["ffn1_w_in"].T
    g_large["ffn2_w_in"] = g_large["ffn2_w_in"].T

    packed = _pack_small(grads)
    nsmall = packed.shape[0]
    packed = jnp.concatenate([packed, loss_row, jnp.zeros(((-nsmall - 1) % 8, HEAD), F32)], axis=0)
    reduced = _all_reduce_small(packed, "reduce_small")
    loss = reduced[nsmall, 0]
    g_small = _unpack_small(reduced[:nsmall], small)

    grad, delta, new_m, new_v = dict(g_small), {}, {}, {}
    for name in LARGE:
        grad[name] = g_large[name][None]
        d, nm, nv = _adamw(w[name][0], g_large[name], m[name][0], v[name][0], "adamw_" + name)
        delta[name], new_m[name], new_v[name] = d[None], nm[None], nv[None]
    d, nm, nv = _adamw(_pack_small(w), reduced[:nsmall], _pack_small(m), _pack_small(v), "adamw_small")
    delta.update(_unpack_small(d, small))
    new_m.update(_unpack_small(nm, small))
    new_v.update(_unpack_small(nv, small))
    return (loss, grad_x[None], *[grad[n] for n in ORDER], *[delta[n] for n in ORDER],
            *[new_m[n] for n in ORDER], *[new_v[n] for n in ORDER])
```

```python
import functools
import math

import numpy as np
import jax
import jax.numpy as jnp
from jax import lax
from jax.experimental import pallas as pl
from jax.experimental.pallas import tpu as pltpu

F32, BF16 = jnp.float32, jnp.bfloat16
HEAD = 128
NSLAB = 16
BLK = 128
DILATIONS = (1, 4, 16)
NUM_BUCKETS, MAX_DISTANCE = 32, 2048
RMS_EPS = 1e-6
NEG = -1e30
SCALE = HEAD ** -0.5
LR, B1, B2, EPS, WD, STEP = 0.001, 0.9, 0.999, 1e-08, 0.01, 10
N_DEV = 8
VMEM_LIMIT_BYTES = 56 << 20
MESH = pl.DeviceIdType.MESH


def _cparams(**kw):
    return pltpu.CompilerParams(vmem_limit_bytes=VMEM_LIMIT_BYTES, **kw)


def _nn(a, b):
    return jnp.dot(a, b, preferred_element_type=F32)


def _nt(a, b):
    return lax.dot_general(a, b, (((1,), (1,)), ((), ())), preferred_element_type=F32)


def _tn(a, b):
    return lax.dot_general(a, b, (((0,), (0,)), ((), ())), preferred_element_type=F32)


def _sds(shape, dtype):
    return jax.ShapeDtypeStruct(shape, dtype)


ANY = pl.BlockSpec(memory_space=pl.ANY)


class _Side:
    def __init__(self, ins, outs, sems, start, finish):
        self.ins, self.outs, self.sems, self.start, self.finish = list(ins), list(outs), list(sems), start, finish


def _call(body, *, name, grid, in_specs, out_specs, out_shape, args, scratch_shapes=(), side=None):
    in_specs, out_specs, out_shape = list(in_specs), list(out_specs), list(out_shape)
    scratch_shapes = list(scratch_shapes)
    if side is None:
        return pl.pallas_call(body, name=name, grid=grid, in_specs=in_specs, out_specs=out_specs, out_shape=out_shape,
                              scratch_shapes=scratch_shapes, compiler_params=_cparams())(*args)
    ni, no, ns = len(args), len(out_shape), len(scratch_shapes)
    si, so = len(side.ins), len(side.outs)

    def fused(*refs):
        h_in, s_in = refs[:ni], refs[ni:ni + si]
        h_out, s_out = refs[ni + si:ni + si + no], refs[ni + si + no:ni + si + no + so]
        h_scr, s_sem = refs[ni + si + no + so:ni + si + no + so + ns], refs[ni + si + no + so + ns:]
        ids = [pl.program_id(k) for k in range(len(grid))]
        first = functools.reduce(jnp.logical_and, [i == 0 for i in ids])
        last = functools.reduce(jnp.logical_and, [i == n - 1 for i, n in zip(ids, grid)])

        @pl.when(first)
        def _():
            side.start(s_in, s_out, s_sem)

        body(*h_in, *h_out, *h_scr)

        @pl.when(last)
        def _():
            side.finish(s_in, s_out, s_sem)

    res = pl.pallas_call(
        fused, name=name, grid=grid, in_specs=in_specs + [ANY] * si, out_specs=out_specs + [ANY] * so,
        out_shape=out_shape + side.outs, scratch_shapes=scratch_shapes + side.sems, compiler_params=_cparams(),
    )(*args, *side.ins)
    return list(res[:no]), list(res[no:])


def _ffn_fwd(x, g, win, wout, tm, name, side=None):
    T, D = x.shape
    nc, tf = wout.shape[0], wout.shape[1]

    def body(x_ref, g_ref, win_ref, wout_ref, y_ref, h_ref, gu_ref):
        j = pl.program_id(1)

        @pl.when(j == 0)
        def _():
            xv = x_ref[...]
            r = lax.rsqrt(jnp.mean(xv * xv, axis=-1, keepdims=True) + RMS_EPS)
            h_ref[...] = (xv * r * g_ref[...]).astype(BF16)
            y_ref[...] = jnp.zeros_like(y_ref)

        hb = h_ref[...]
        gt = _nt(hb, win_ref[0])
        up = _nt(hb, win_ref[1])
        gu_ref[0] = gt.astype(BF16)
        gu_ref[1] = up.astype(BF16)
        act = (gt * jax.nn.sigmoid(gt) * up).astype(BF16)
        y_ref[...] += _nn(act, wout_ref[...])

        @pl.when(j == nc - 1)
        def _():
            y_ref[...] = x_ref[...] + 0.5 * y_ref[...]

    return _call(
        body, name=name, grid=(T // tm, nc), side=side,
        in_specs=[pl.BlockSpec((tm, D), lambda i, j: (i, 0)),
                  pl.BlockSpec((1, D), lambda i, j: (0, 0)),
                  pl.BlockSpec((2, None, tf, D), lambda i, j: (0, j, 0, 0)),
                  pl.BlockSpec((None, tf, D), lambda i, j: (j, 0, 0))],
        out_specs=[pl.BlockSpec((tm, D), lambda i, j: (i, 0)),
                   pl.BlockSpec((tm, D), lambda i, j: (i, 0)),
                   pl.BlockSpec((2, None, tm, tf), lambda i, j: (0, j, i, 0))],
        out_shape=[_sds((T, D), F32), _sds((T, D), BF16), _sds((2, nc, T, tf), BF16)],
        args=(x, g, win, wout))


def _ffn_bwd(dy, gu, win, wout, tm, name, side=None):
    T, D = dy.shape
    nc, tf = wout.shape[0], wout.shape[1]

    def body(dy_ref, gu_ref, win_ref, wout_ref, dh_ref, dgu_ref, act_ref):
        j = pl.program_id(1)

        @pl.when(j == 0)
        def _():
            dh_ref[...] = jnp.zeros_like(dh_ref)

        dyb = (0.5 * dy_ref[...]).astype(BF16)
        dact = _nt(dyb, wout_ref[...])
        gt = gu_ref[0].astype(F32)
        up = gu_ref[1].astype(F32)
        s = jax.nn.sigmoid(gt)
        silu = gt * s
        dgb = (dact * up * (s * (1.0 + gt * (1.0 - s)))).astype(BF16)
        dub = (dact * silu).astype(BF16)
        dgu_ref[0] = dgb
        dgu_ref[1] = dub
        act_ref[...] = (0.5 * silu * up).astype(BF16)
        dh_ref[...] += _nn(dgb, win_ref[0]) + _nn(dub, win_ref[1])

    return _call(
        body, name=name, grid=(T // tm, nc), side=side,
        in_specs=[pl.BlockSpec((tm, D), lambda i, j: (i, 0)),
                  pl.BlockSpec((2, None, tm, tf), lambda i, j: (0, j, i, 0)),
                  pl.BlockSpec((2, None, tf, D), lambda i, j: (0, j, 0, 0)),
                  pl.BlockSpec((None, tf, D), lambda i, j: (j, 0, 0))],
        out_specs=[pl.BlockSpec((tm, D), lambda i, j: (i, 0)),
                   pl.BlockSpec((2, None, tm, tf), lambda i, j: (0, j, i, 0)),
                   pl.BlockSpec((None, tm, tf), lambda i, j: (j, i, 0))],
        out_shape=[_sds((T, D), F32), _sds((2, nc, T, tf), BF16), _sds((nc, T, tf), BF16)],
        args=(dy, gu, win, wout))


def _rms_bwd(x, g, dh, dres, tm, name):
    T, D = x.shape

    def body(x_ref, g_ref, dh_ref, dres_ref, dx_ref, dg_ref):
        @pl.when(pl.program_id(0) == 0)
        def _():
            dg_ref[...] = jnp.zeros_like(dg_ref)

        xv = x_ref[...]
        r = lax.rsqrt(jnp.mean(xv * xv, axis=-1, keepdims=True) + RMS_EPS)
        xhat = xv * r
        dh = dh_ref[...]
        gd = dh * g_ref[...]
        dx_ref[...] = dres_ref[...] + r * (gd - xhat * jnp.mean(gd * xhat, axis=-1, keepdims=True))
        dg_ref[...] += jnp.sum(dh * xhat, axis=0, keepdims=True)

    row = pl.BlockSpec((tm, D), lambda i: (i, 0))
    one = pl.BlockSpec((1, D), lambda i: (0, 0))
    return pl.pallas_call(
        body, name=name, grid=(T // tm,), in_specs=[row, one, row, row], out_specs=[row, one],
        out_shape=[_sds((T, D), F32), _sds((1, D), F32)], compiler_params=_cparams(),
    )(x, g, dh, dres)


def _mm_tn(a, b, *, bm, bn, bt, name, b_slabs=False, side=None):
    nz, T, M = a.shape
    if b_slabs:
        N = b.shape[1] // NSLAB
        assert bt == T // NSLAB
        b_spec = pl.BlockSpec((bt, bn), lambda z, m, n, t: (0, t * (N // bn) + n))
    else:
        N = b.shape[1]
        b_spec = pl.BlockSpec((bt, bn), lambda z, m, n, t: (t, n))
    assert M % bm == 0 and N % bn == 0 and T % bt == 0, (M, bm, N, bn, T, bt)

    def body(a_ref, b_ref, c_ref):
        @pl.when(pl.program_id(3) == 0)
        def _():
            c_ref[...] = jnp.zeros_like(c_ref)

        c_ref[...] += _tn(a_ref[...].astype(BF16), b_ref[...].astype(BF16))

    res = _call(
        body, name=name, grid=(nz, M // bm, N // bn, T // bt), side=side,
        in_specs=[pl.BlockSpec((None, bt, bm), lambda z, m, n, t: (z, t, m)), b_spec],
        out_specs=[pl.BlockSpec((None, bm, bn), lambda z, m, n, t: (z, m, n))],
        out_shape=[_sds((nz, M, N), F32)], args=(a, b))
    return res[0] if side is None else (res[0][0], res[1])


def _proj(x, g, w, gains, modes, *, slabs, tm, normed_dtype, name):
    T, D = x.shape
    N = w.shape[1]
    ntile = len(modes)
    tn = N // ntile
    n16 = T // NSLAB
    if slabs:
        assert tm == n16
        x_in = x.reshape(n16, NSLAB * D)
        x_spec = pl.BlockSpec((tm, D), lambda i, n: (0, i))
        oshape = lambda c: (NSLAB, n16, c)
        ospec = lambda bc, cm: pl.BlockSpec((None, tm, bc), lambda i, n: (i, 0, cm(n)))
    else:
        x_in = x
        x_spec = pl.BlockSpec((tm, D), lambda i, n: (i, 0))
        oshape = lambda c: (T, c)
        ospec = lambda bc, cm: pl.BlockSpec((tm, bc), lambda i, n: (i, cm(n)))

    def body(x_ref, g_ref, w_ref, gains_ref, raw_ref, nrm_ref, h_ref):
        n = pl.program_id(1)

        @pl.when(n == 0)
        def _():
            xv = x_ref[...]
            r = lax.rsqrt(jnp.mean(xv * xv, axis=-1, keepdims=True) + RMS_EPS)
            h_ref[...] = (xv * r * g_ref[...]).astype(BF16)

        y = _nn(h_ref[...], w_ref[...])
        raw_ref[...] = y
        for t, mode in enumerate(modes):
            @pl.when(n == t)
            def _(t=t, mode=mode):
                if not mode:
                    nrm_ref[...] = y.astype(nrm_ref.dtype)
                    return
                gain = gains_ref[t]
                for k in range(tn // HEAD):
                    yk = y[:, k * HEAD:(k + 1) * HEAD]
                    r = lax.rsqrt(jnp.mean(yk * yk, axis=-1, keepdims=True) + RMS_EPS)
                    nrm_ref[:, k * HEAD:(k + 1) * HEAD] = (yk * r * gain).astype(nrm_ref.dtype)

    return pl.pallas_call(
        body, name=name, grid=(T // tm, ntile),
        in_specs=[x_spec, pl.BlockSpec((1, D), lambda i, n: (0, 0)), pl.BlockSpec((D, tn), lambda i, n: (0, n)),
                  pl.BlockSpec((ntile, 1, HEAD), lambda i, n: (0, 0, 0))],
        out_specs=[ospec(tn, lambda n: n), ospec(tn, lambda n: n), ospec(D, lambda n: 0)],
        out_shape=[_sds(oshape(N), F32), _sds(oshape(N), normed_dtype), _sds(oshape(D), BF16)],
        compiler_params=_cparams(),
    )(x_in, g, w, gains)


def _mm(a, w, *, nt, tk, tm, a_layout, out_layout, resid=None, name):
    if a_layout == "slab":
        T, K = a.shape[0] * a.shape[1], a.shape[2]
    else:
        T, K = a.shape
    N = w.shape[0] if nt else w.shape[1]
    n16 = T // NSLAB
    nk = K // tk

    def spec(layout, C, bc, colmap):
        if layout == "nat":
            return pl.BlockSpec((tm, bc), lambda i, k: (i, colmap(k)))
        assert tm == n16
        if layout == "slab":
            return pl.BlockSpec((None, tm, bc), lambda i, k: (i, 0, colmap(k)))
        assert bc == C
        return pl.BlockSpec((tm, C), lambda i, k: (0, i))

    a_in = a.reshape(n16, NSLAB * K) if a_layout == "view" else a
    w_spec = pl.BlockSpec((N, tk), lambda i, k: (0, k)) if nt else pl.BlockSpec((tk, N), lambda i, k: (k, 0))
    o_spec = spec(out_layout, N, N, lambda k: 0)
    oshape = {"nat": (T, N), "slab": (NSLAB, n16, N), "view": (n16, NSLAB * N)}[out_layout]
    has_resid = resid is not None

    def body(*refs):
        a_ref, w_ref = refs[0], refs[1]
        o_ref = refs[-1]
        k = pl.program_id(1)

        @pl.when(k == 0)
        def _():
            o_ref[...] = refs[2][...] if has_resid else jnp.zeros_like(o_ref)

        ab = a_ref[...].astype(BF16)
        o_ref[...] += _nt(ab, w_ref[...]) if nt else _nn(ab, w_ref[...])

    ins = [a_in, w]
    in_specs = [spec(a_layout, K, tk, lambda k: k), w_spec]
    if has_resid:
        ins.append(resid.reshape(n16, NSLAB * N) if out_layout == "view" else resid)
        in_specs.append(o_spec)
    out = pl.pallas_call(
        body, name=name, grid=(T // tm, nk), in_specs=in_specs, out_specs=o_spec,
        out_shape=_sds(oshape, F32), compiler_params=_cparams(),
    )(*ins)
    return out.reshape(T, N) if out_layout == "view" else out


def _log_sigmoid(z):
    return jnp.minimum(z, 0.0) - jnp.log(1.0 + jnp.exp(-jnp.abs(z)))


def _fox_gate_fwd(f_raw, fbias, name):
    T = f_raw.shape[0]
    cb = 256

    def body(f_ref, b_ref, c_ref):
        row = lax.broadcasted_iota(jnp.int32, (cb, cb), 0)
        col = lax.broadcasted_iota(jnp.int32, (cb, cb), 1)
        tri = (col <= row).astype(F32)
        carry = jnp.zeros((1, HEAD), F32)
        for i in range(T // cb):
            lf = _log_sigmoid(f_ref[i * cb:(i + 1) * cb, :] + b_ref[...])
            c = jnp.dot(tri, lf, preferred_element_type=F32, precision=lax.Precision.HIGHEST) + carry
            c_ref[i * cb:(i + 1) * cb, :] = c
            carry = c[cb - 1:cb, :]

    return pl.pallas_call(body, name=name, out_shape=_sds((T, HEAD), F32), compiler_params=_cparams())(f_raw, fbias)


def _fox_gate_bwd(f_raw, fbias, dc, name):
    T = f_raw.shape[0]
    cb = 256

    def body(f_ref, b_ref, dc_ref, df_ref, db_ref):
        row = lax.broadcasted_iota(jnp.int32, (cb, cb), 0)
        col = lax.broadcasted_iota(jnp.int32, (cb, cb), 1)
        tri = (col >= row).astype(F32)
        carry = jnp.zeros((1, HEAD), F32)
        dbias = jnp.zeros((1, HEAD), F32)
        for i in reversed(range(T // cb)):
            dlf = jnp.dot(tri, dc_ref[i * cb:(i + 1) * cb, :], preferred_element_type=F32,
                          precision=lax.Precision.HIGHEST) + carry
            carry = dlf[0:1, :]
            z = f_ref[i * cb:(i + 1) * cb, :] + b_ref[...]
            df = dlf * jax.nn.sigmoid(-z)
            df_ref[i * cb:(i + 1) * cb, :] = df
            dbias = dbias + jnp.sum(df, axis=0, keepdims=True)
        db_ref[...] = dbias

    return pl.pallas_call(body, name=name, out_shape=[_sds((T, HEAD), F32), _sds((1, HEAD), F32)],
                          compiler_params=_cparams())(f_raw, fbias, dc)


def _fox_fwd(qkv, c_col, c_row, tq, name, side=None):
    T = qkv.shape[0]
    H = qkv.shape[1] // (3 * HEAD)
    nq = T // tq

    def body(q_ref, k_ref, v_ref, cq_ref, ck_ref, o_ref, lse_ref, m_s, l_s, acc_s):
        qi, ki = pl.program_id(1), pl.program_id(2)

        @pl.when(ki == 0)
        def _():
            m_s[...] = jnp.full_like(m_s, NEG)
            l_s[...] = jnp.zeros_like(l_s)
            acc_s[...] = jnp.zeros_like(acc_s)

        @pl.when(ki <= qi)
        def _():
            s = _nt(q_ref[...], k_ref[...]) * SCALE + cq_ref[...] - ck_ref[...]
            qpos = qi * tq + lax.broadcasted_iota(jnp.int32, (tq, tq), 0)
            kpos = ki * tq + lax.broadcasted_iota(jnp.int32, (tq, tq), 1)
            s = jnp.where(kpos <= qpos, s, NEG)
            m_new = jnp.maximum(m_s[...], jnp.max(s, axis=-1, keepdims=True))
            alpha = jnp.exp(m_s[...] - m_new)
            p = jnp.exp(s - m_new)
            l_s[...] = alpha * l_s[...] + jnp.sum(p, axis=-1, keepdims=True)
            acc_s[...] = alpha * acc_s[...] + _nn(p.astype(BF16), v_ref[...])
            m_s[...] = m_new

        @pl.when(ki == nq - 1)
        def _():
            o_ref[...] = acc_s[...] / l_s[...]
            lse_ref[...] = m_s[...] + jnp.log(l_s[...])

    kmap = lambda off: (lambda h, qi, ki: (jnp.minimum(ki, qi), off + h))
    return _call(
        body, name=name, grid=(H, nq, nq), side=side,
        in_specs=[pl.BlockSpec((tq, HEAD), lambda h, qi, ki: (qi, h)),
                  pl.BlockSpec((tq, HEAD), kmap(H)),
                  pl.BlockSpec((tq, HEAD), kmap(2 * H)),
                  pl.BlockSpec((None, tq, 1), lambda h, qi, ki: (h, qi, 0)),
                  pl.BlockSpec((None, 1, tq), lambda h, qi, ki: (h, 0, jnp.minimum(ki, qi)))],
        out_specs=[pl.BlockSpec((tq, HEAD), lambda h, qi, ki: (qi, h)),
                   pl.BlockSpec((None, tq, 1), lambda h, qi, ki: (h, qi, 0))],
        out_shape=[_sds((T, H * HEAD), F32), _sds((H, T, 1), F32)],
        scratch_shapes=[pltpu.VMEM((tq, 1), F32), pltpu.VMEM((tq, 1), F32), pltpu.VMEM((tq, HEAD), F32)],
        args=(qkv, qkv, qkv, c_col, c_row))


def _fox_bwd(qkv, c_col, c_row, out, dout, lse, tq, name, side=None):
    T = qkv.shape[0]
    H = qkv.shape[1] // (3 * HEAD)
    nq = T // tq

    def body(q_ref, k_ref, v_ref, cq_ref, ck_ref, o_ref, do_ref, lse_ref, dq_ref, dk_ref, dv_ref, dck_ref, dcq_ref):
        ki, qi = pl.program_id(1), pl.program_id(2)

        @pl.when((ki == 0) & (qi == 0))
        def _():
            dq_ref[...] = jnp.zeros_like(dq_ref)
            dcq_ref[...] = jnp.zeros_like(dcq_ref)

        @pl.when(qi == 0)
        def _():
            dk_ref[...] = jnp.zeros_like(dk_ref)
            dv_ref[...] = jnp.zeros_like(dv_ref)
            dck_ref[...] = jnp.zeros_like(dck_ref)

        @pl.when(qi >= ki)
        def _():
            q, k = q_ref[...], k_ref[...]
            s = _nt(q, k) * SCALE + cq_ref[...] - ck_ref[...]
            qpos = qi * tq + lax.broadcasted_iota(jnp.int32, (tq, tq), 0)
            kpos = ki * tq + lax.broadcasted_iota(jnp.int32, (tq, tq), 1)
            s = jnp.where(kpos <= qpos, s, NEG)
            p = jnp.exp(s - lse_ref[...])
            do = do_ref[...]
            delta = jnp.sum(do * o_ref[...], axis=-1, keepdims=True)
            dob = do.astype(BF16)
            dp = _nt(dob, v_ref[...])
            ds = p * (dp - delta)
            dsb = ds.astype(BF16)
            dv_ref[...] += _tn(p.astype(BF16), dob)
            dk_ref[...] += _tn(dsb, q) * SCALE
            rows = pl.ds(pl.multiple_of(qi * tq, tq), tq)
            dq_ref[rows, :] += _nn(dsb, k) * SCALE
            dck_ref[...] -= jnp.sum(ds, axis=0, keepdims=True)
            dcq_ref[rows, :] += jnp.sum(ds, axis=-1, keepdims=True)

    qmap = lambda h, ki, qi: (jnp.maximum(qi, ki), h)
    return _call(
        body, name=name, grid=(H, nq, nq), side=side,
        in_specs=[pl.BlockSpec((tq, HEAD), qmap),
                  pl.BlockSpec((tq, HEAD), lambda h, ki, qi: (ki, H + h)),
                  pl.BlockSpec((tq, HEAD), lambda h, ki, qi: (ki, 2 * H + h)),
                  pl.BlockSpec((None, tq, 1), lambda h, ki, qi: (h, jnp.maximum(qi, ki), 0)),
                  pl.BlockSpec((None, 1, tq), lambda h, ki, qi: (h, 0, ki)),
                  pl.BlockSpec((tq, HEAD), qmap),
                  pl.BlockSpec((tq, HEAD), qmap),
                  pl.BlockSpec((None, tq, 1), lambda h, ki, qi: (h, jnp.maximum(qi, ki), 0))],
        out_specs=[pl.BlockSpec((T, HEAD), lambda h, ki, qi: (0, h)),
                   pl.BlockSpec((tq, HEAD), lambda h, ki, qi: (ki, h)),
                   pl.BlockSpec((tq, HEAD), lambda h, ki, qi: (ki, h)),
                   pl.BlockSpec((None, 1, tq), lambda h, ki, qi: (h, 0, ki)),
                   pl.BlockSpec((None, T, 1), lambda h, ki, qi: (h, 0, 0))],
        out_shape=[_sds((T, H * HEAD), F32), _sds((T, H * HEAD), F32), _sds((T, H * HEAD), F32), _sds((H, 1, T), F32),
                   _sds((H, T, 1), F32)],
        args=(qkv, qkv, qkv, c_col, c_row, out, dout, lse))


def _t5_bucket(dist):
    max_exact = NUM_BUCKETS // 2
    d = dist.astype(np.float32)
    large = max_exact + (np.log(np.maximum(d, np.float32(1.0)) / np.float32(max_exact))
                         / np.float32(math.log(MAX_DISTANCE / max_exact))
                         * np.float32(NUM_BUCKETS - max_exact)).astype(np.int32)
    large = np.minimum(large, NUM_BUCKETS - 1)
    return np.where(dist < max_exact, dist, large)


def _bucket_maps():
    maps = []
    for d in DILATIONS:
        e = NSLAB // d
        rows = BLK // e
        idx = np.arange(BLK)
        pos = e * (idx % rows) + idx // rows
        qpos = pos[:, None] + BLK
        kpos = np.concatenate([pos, pos + BLK])[None, :]
        delta = qpos - kpos
        band = (delta >= 0) & (delta <= BLK)
        bucket = _t5_bucket(np.clip(delta, 0, None) * d)
        maps.append(np.where(band, bucket, -1).astype(np.int32))
    return np.stack(maps)


def _dil_geometry(T):
    n16 = T // NSLAB
    geo = []
    for d in DILATIONS:
        e = NSLAB // d
        rows = BLK // e
        nblk = n16 // rows
        geo.append((d, e, rows, nblk))
    return geo


def _dil_bias(tab_ref, bkt_ref, bias_s, h):
    for p in range(len(DILATIONS)):
        bk = bkt_ref[p]
        bias = jnp.full((BLK, 2 * BLK), NEG, F32)
        for b in range(NUM_BUCKETS):
            bias = jnp.where(bk == b, tab_ref[b, h], bias)
        bias_s[p] = bias


def _dil_rows(d, e, rows, sub, blk):
    start = pl.multiple_of(blk * rows, rows)
    return [(sub + d * j, pl.ds(start, rows)) for j in range(e)]


def _gather(ref, idx):
    return jnp.concatenate([ref[s, r, :] for s, r in idx], axis=0)


def _scatter(ref, idx, val, rows):
    for j, (s, r) in enumerate(idx):
        ref[s, r, :] = val[j * rows:(j + 1) * rows]


def _scatter_add(ref, idx, val, rows):
    for j, (s, r) in enumerate(idx):
        ref[s, r, :] += val[j * rows:(j + 1) * rows]


def _dil_fwd(qkv, table, name):
    n16 = qkv.shape[1]
    T = NSLAB * n16
    H = qkv.shape[2] // (3 * HEAD)
    geo = _dil_geometry(T)
    bkt = jnp.asarray(_bucket_maps())

    def body(tab_ref, bkt_ref, q_ref, k_ref, v_ref, o_ref, lse_ref, bias_s, m_s, l_s):
        h = pl.program_id(0)
        _dil_bias(tab_ref, bkt_ref, bias_s, h)
        first_mask = lax.broadcasted_iota(jnp.int32, (BLK, 2 * BLK), 1) < BLK

        def block(p, d, e, rows, sub, blk):
            cur = _dil_rows(d, e, rows, sub, blk)
            prev = _dil_rows(d, e, rows, sub, jnp.maximum(blk - 1, 0))
            q = _gather(q_ref, cur).astype(BF16)
            kk = jnp.concatenate([_gather(k_ref, prev), _gather(k_ref, cur)], axis=0).astype(BF16)
            vv = jnp.concatenate([_gather(v_ref, prev), _gather(v_ref, cur)], axis=0).astype(BF16)
            s = _nt(q, kk) * SCALE + bias_s[p]
            s = jnp.where(first_mask & (blk == 0), NEG, s)
            m_blk = jnp.max(s, axis=-1, keepdims=True)
            if p == len(DILATIONS) - 1:
                m_new = m_blk
                pr = jnp.exp(s - m_new)
                l_new = jnp.sum(pr, axis=-1, keepdims=True)
                acc = _nn(pr.astype(BF16), vv)
            else:
                m_old = _gather(m_s, cur)
                m_new = jnp.maximum(m_old, m_blk)
                alpha = jnp.exp(m_old - m_new)
                pr = jnp.exp(s - m_new)
                l_new = alpha * _gather(l_s, cur) + jnp.sum(pr, axis=-1, keepdims=True)
                acc = alpha * _gather(o_ref, cur) + _nn(pr.astype(BF16), vv)
            if p == 0:
                _scatter(o_ref, cur, acc / l_new, rows)
                _scatter(lse_ref, cur, m_new + jnp.log(l_new), rows)
            else:
                _scatter(o_ref, cur, acc, rows)
                _scatter(m_s, cur, m_new, rows)
                _scatter(l_s, cur, l_new, rows)

        for p in reversed(range(len(DILATIONS))):
            d, e, rows, nblk = geo[p]

            def step(i, carry, p=p, d=d, e=e, rows=rows, nblk=nblk):
                block(p, d, e, rows, i // nblk, i % nblk)
                return carry

            lax.fori_loop(0, d * nblk, step, 0)

    head = lambda off: pl.BlockSpec((NSLAB, n16, HEAD), lambda h: (0, 0, off + h))
    return pl.pallas_call(
        body, name=name, grid=(H,),
        in_specs=[pl.BlockSpec(memory_space=pltpu.SMEM), pl.BlockSpec((3, BLK, 2 * BLK), lambda h: (0, 0, 0)),
                  head(0), head(H), head(2 * H)],
        out_specs=[head(0), pl.BlockSpec((None, NSLAB, n16, 1), lambda h: (h, 0, 0, 0))],
        out_shape=[_sds((NSLAB, n16, H * HEAD), F32), _sds((H, NSLAB, n16, 1), F32)],
        scratch_shapes=[pltpu.VMEM((3, BLK, 2 * BLK), F32), pltpu.VMEM((NSLAB, n16, 1), F32),
                        pltpu.VMEM((NSLAB, n16, 1), F32)],
        compiler_params=_cparams(),
    )(table, bkt, qkv, qkv, qkv)


def _dil_bwd(qkv, table, out, dout, lse, name, side=None):
    n16 = qkv.shape[1]
    T = NSLAB * n16
    H = qkv.shape[2] // (3 * HEAD)
    geo = _dil_geometry(T)
    bkt = jnp.asarray(_bucket_maps())

    def body(tab_ref, bkt_ref, q_ref, k_ref, v_ref, o_ref, do_ref, lse_ref,
             dq_ref, dk_ref, dv_ref, dtab_ref, bias_s, dbias_s, delta_s):
        h = pl.program_id(0)
        _dil_bias(tab_ref, bkt_ref, bias_s, h)
        first_mask = lax.broadcasted_iota(jnp.int32, (BLK, 2 * BLK), 1) < BLK
        dbias_s[...] = jnp.zeros_like(dbias_s)
        dq_ref[...] = jnp.zeros_like(dq_ref)
        dk_ref[...] = jnp.zeros_like(dk_ref)
        dv_ref[...] = jnp.zeros_like(dv_ref)
        for r in range(NSLAB):
            delta_s[r] = jnp.sum(do_ref[r] * o_ref[r], axis=-1, keepdims=True)

        def block(p, d, e, rows, sub, blk):
            cur = _dil_rows(d, e, rows, sub, blk)
            prev = _dil_rows(d, e, rows, sub, jnp.maximum(blk - 1, 0))
            q = _gather(q_ref, cur).astype(BF16)
            kk = jnp.concatenate([_gather(k_ref, prev), _gather(k_ref, cur)], axis=0).astype(BF16)
            vv = jnp.concatenate([_gather(v_ref, prev), _gather(v_ref, cur)], axis=0).astype(BF16)
            dob = _gather(do_ref, cur).astype(BF16)
            s = _nt(q, kk) * SCALE + bias_s[p]
            s = jnp.where(first_mask & (blk == 0), NEG, s)
            pr = jnp.exp(s - _gather(lse_ref, cur))
            dp = _nt(dob, vv)
            ds = pr * (dp - _gather(delta_s, cur))
            dsb = ds.astype(BF16)
            dbias_s[p] += ds
            dvv = _tn(pr.astype(BF16), dob)
            dkk = _tn(dsb, q) * SCALE
            _scatter_add(dq_ref, cur, _nn(dsb, kk) * SCALE, rows)
            _scatter_add(dk_ref, prev, dkk[:BLK], rows)
            _scatter_add(dk_ref, cur, dkk[BLK:], rows)
            _scatter_add(dv_ref, prev, dvv[:BLK], rows)
            _scatter_add(dv_ref, cur, dvv[BLK:], rows)

        for p in range(len(DILATIONS)):
            d, e, rows, nblk = geo[p]

            def step(i, carry, p=p, d=d, e=e, rows=rows, nblk=nblk):
                block(p, d, e, rows, i // nblk, i % nblk)
                return carry

            lax.fori_loop(0, d * nblk, step, 0)

        lane = lax.broadcasted_iota(jnp.int32, (1, HEAD), 1)
        row = jnp.zeros((1, HEAD), F32)
        for b in range(NUM_BUCKETS):
            tot = jnp.zeros((1, 1), F32)
            for p in range(len(DILATIONS)):
                hit = jnp.where(bkt_ref[p] == b, dbias_s[p], 0.0)
                tot = tot + jnp.sum(jnp.sum(hit, axis=0, keepdims=True), axis=1, keepdims=True)
            row = jnp.where(lane == b, tot, row)
        dtab_ref[...] = row

    head = lambda off: pl.BlockSpec((NSLAB, n16, HEAD), lambda h: (0, 0, off + h))
    return _call(
        body, name=name, grid=(H,), side=side,
        in_specs=[pl.BlockSpec(memory_space=pltpu.SMEM), pl.BlockSpec((3, BLK, 2 * BLK), lambda h: (0, 0, 0)),
                  head(0), head(H), head(2 * H), head(0), head(0),
                  pl.BlockSpec((None, NSLAB, n16, 1), lambda h: (h, 0, 0, 0))],
        out_specs=[head(0), head(0), head(0), pl.BlockSpec((None, 1, HEAD), lambda h: (h, 0, 0))],
        out_shape=[_sds((NSLAB, n16, H * HEAD), F32)] * 3 + [_sds((H, 1, HEAD), F32)],
        scratch_shapes=[pltpu.VMEM((3, BLK, 2 * BLK), F32), pltpu.VMEM((3, BLK, 2 * BLK), F32),
                        pltpu.VMEM((NSLAB, n16, 1), F32)],
        args=(table, bkt, qkv, qkv, qkv, out, dout, lse))


def _qknorm_bwd(raw, dq, dk, dv, gains, tm, name):
    T, N = raw.shape
    C = N // 3

    def body(raw_ref, dq_ref, dk_ref, dv_ref, gains_ref, dp_ref, dg_ref):
        @pl.when(pl.program_id(0) == 0)
        def _():
            dg_ref[...] = jnp.zeros_like(dg_ref)

        for t, d_ref in enumerate((dq_ref, dk_ref)):
            gain = gains_ref[t]
            dgain = jnp.zeros((1, HEAD), F32)
            for k in range(C // HEAD):
                y = raw_ref[:, t * C + k * HEAD:t * C + (k + 1) * HEAD]
                dn = d_ref[:, k * HEAD:(k + 1) * HEAD]
                r = lax.rsqrt(jnp.mean(y * y, axis=-1, keepdims=True) + RMS_EPS)
                yhat = y * r
                gd = dn * gain
                dy = r * (gd - yhat * jnp.mean(gd * yhat, axis=-1, keepdims=True))
                dp_ref[:, t * C + k * HEAD:t * C + (k + 1) * HEAD] = dy.astype(BF16)
                dgain = dgain + jnp.sum(dn * yhat, axis=0, keepdims=True)
            dg_ref[t] += dgain
        dp_ref[:, 2 * C:] = dv_ref[...].astype(BF16)

    third = pl.BlockSpec((tm, C), lambda i: (i, 0))
    return pl.pallas_call(
        body, name=name, grid=(T // tm,),
        in_specs=[pl.BlockSpec((tm, N), lambda i: (i, 0)), third, third, third,
                  pl.BlockSpec((2, 1, HEAD), lambda i: (0, 0, 0))],
        out_specs=[pl.BlockSpec((tm, N), lambda i: (i, 0)), pl.BlockSpec((2, 1, HEAD), lambda i: (0, 0, 0))],
        out_shape=[_sds((T, N), BF16), _sds((2, 1, HEAD), F32)], compiler_params=_cparams(),
    )(raw, dq, dk, dv, gains)


def _loss_grad(y, target, tm, name):
    T, D = y.shape

    def body(y_ref, t_ref, dy_ref, loss_ref):
        @pl.when(pl.program_id(0) == 0)
        def _():
            loss_ref[...] = jnp.zeros_like(loss_ref)

        err = y_ref[...] - t_ref[...]
        dy_ref[...] = err * (1.0 / D)
        per_tok = jnp.mean(err * err, axis=-1, keepdims=True)
        tot = 0.5 * jnp.sum(per_tok, axis=0, keepdims=True)
        lane = lax.broadcasted_iota(jnp.int32, (1, HEAD), 1)
        loss_ref[...] += jnp.where(lane == 0, tot, 0.0)

    row = pl.BlockSpec((tm, D), lambda i: (i, 0))
    return pl.pallas_call(
        body, name=name, grid=(T // tm,), in_specs=[row, row],
        out_specs=[row, pl.BlockSpec((1, HEAD), lambda i: (0, 0))],
        out_shape=[_sds((T, D), F32), _sds((1, HEAD), F32)], compiler_params=_cparams(),
    )(y, target)


def _pad_lanes(v, width=HEAD):
    return jnp.pad(v, ((0, 0), (0, width - v.shape[1])))


def _local_step(x, target, small, wts, plan=None):
    grads = {}

    def hosted(host, fn, *args, **kw):
        side = plan.before(host, wts, grads) if plan is not None else None
        if side is None:
            return fn(*args, name=host, **kw)
        res, side_res = fn(*args, name=host, side=side, **kw)
        plan.after(host, side_res, wts, grads)
        return res

    T, D = x.shape
    C = D // 2
    H = C // HEAD
    n16 = T // NSLAB
    tm = min(512, T)
    tq = min(512, T)
    bn = min(1024, D)
    g1, gm, g2 = small["ffn1_norm"], small["mix_norm"], small["ffn2_norm"]
    gains_a = jnp.stack([small["q_norm_a"], small["k_norm_a"], jnp.ones_like(small["q_norm_a"])])
    gains_b = jnp.stack([small["q_norm_b"], small["k_norm_b"], jnp.ones_like(small["q_norm_b"])])
    fbias = _pad_lanes(small["forget_bias"])
    table = small["rel_bias_table"]

    x1, h1, gu1 = hosted("ffn1_fwd", _ffn_fwd, x, g1, wts["ffn1_in"], wts["ffn1_out"], tm)
    raw_a, nrm_a, h2a = _proj(x1, gm, wts["w_a"], gains_a, (True, True, False), slabs=True, tm=n16,
                              normed_dtype=F32, name="proj_a")
    raw_b, nrm_b, h2b = _proj(x1, gm, wts["w_b"], gains_b, (True, True, False), slabs=False, tm=tm,
                              normed_dtype=BF16, name="proj_b")
    f_raw, _, _ = _proj(x1, gm, wts["w_f"], gains_b[:1], (False,), slabs=False, tm=tm, normed_dtype=BF16,
                        name="proj_f")
    c = _fox_gate_fwd(f_raw, fbias, "fox_gate_fwd")
    c_heads = c[:, :H].T
    c_col, c_row = c_heads[:, :, None], c_heads[:, None, :]
    out_a, lse_a = _dil_fwd(nrm_a, table, "dil_fwd")
    out_b, lse_b = hosted("fox_fwd", _fox_fwd, nrm_b, c_col, c_row, tq)
    w_o = wts["w_o"]
    x2a = _mm(out_a, w_o[:C], nt=False, tk=C, tm=n16, a_layout="slab", out_layout="view", resid=x1, name="out_a")
    x2 = _mm(out_b, w_o[C:], nt=False, tk=C, tm=tm, a_layout="nat", out_layout="nat", resid=x2a, name="out_b")
    y, h3, gu3 = _ffn_fwd(x2, g2, wts["ffn2_in"], wts["ffn2_out"], tm, "ffn2_fwd")
    dy, loss_row = _loss_grad(y, target, tm, "loss_grad")

    def ffn_backward(tag, xin, g, h, gu, win, wout, dres):
        nc, tf = wout.shape[0], wout.shape[1]
        dh, dgu, acth = hosted(tag + "_bwd", _ffn_bwd, dres, gu, win, wout, tm)
        dxin, dg = _rms_bwd(xin, g, dh, dres, tm, tag + "_rms_bwd")
        grads[tag + "_norm"] = dg
        grads[tag + "_w_in_t"] = hosted(tag + "_dwin", _mm_tn, dgu.reshape(2 * nc, T, tf), h, bm=tf, bn=bn,
                                        bt=min(1024, T))
        grads[tag + "_w_out"] = hosted(tag + "_dwout", _mm_tn, acth, dres, bm=tf, bn=bn, bt=min(1024, T))
        return dxin

    dx2 = ffn_backward("ffn2", x2, g2, h3, gu3, wts["ffn2_in"], wts["ffn2_out"], dy)

    dmix_a = _mm(dx2, w_o[:C], nt=True, tk=D, tm=n16, a_layout="view", out_layout="slab", name="dmix_a")
    dmix_b = _mm(dx2, w_o[C:], nt=True, tk=D, tm=tm, a_layout="nat", out_layout="nat", name="dmix_b")
    dwo_a = _mm_tn(out_a.reshape(1, T, C), dx2.reshape(n16, NSLAB * D), bm=C, bn=bn, bt=n16, b_slabs=True, name="dwo_a")
    dwo_b = _mm_tn(out_b.reshape(1, T, C), dx2, bm=C, bn=bn, bt=tm, name="dwo_b")
    grads["w_out"] = jnp.concatenate([dwo_a[0], dwo_b[0]], axis=0)

    dqa, dka, dva, dtab = hosted("dil_bwd", _dil_bwd, nrm_a, table, out_a, dmix_a, lse_a)
    dqb, dkb, dvb, dck, dcq = hosted("fox_bwd", _fox_bwd, nrm_b, c_col, c_row, out_b, dmix_b, lse_b, tq)
    grads["rel_bias_table"] = dtab[:, 0, :NUM_BUCKETS].T
    dc = _pad_lanes((dck[:, 0, :] + dcq[:, :, 0]).T)
    df, dfb = _fox_gate_bwd(f_raw, fbias, dc, "fox_gate_bwd")
    grads["forget_bias"] = dfb[:, :H]

    flat = lambda a: a.reshape(T, a.shape[-1])
    dproj_a, dgain_a = _qknorm_bwd(flat(raw_a), flat(dqa), flat(dka), flat(dva), gains_a[:2], min(256, T), "qknorm_bwd_a")
    dproj_b, dgain_b = _qknorm_bwd(raw_b, dqb, dkb, dvb, gains_b[:2], min(256, T), "qknorm_bwd_b")
    grads["q_norm_a"], grads["k_norm_a"] = dgain_a[0], dgain_a[1]
    grads["q_norm_b"], grads["k_norm_b"] = dgain_b[0], dgain_b[1]
    dproj_a = dproj_a.reshape(NSLAB, n16, 3 * C)

    dh2 = _mm(dproj_a, wts["w_a"], nt=True, tk=C, tm=n16, a_layout="slab", out_layout="view", name="dh2_a")
    dh2 = _mm(dproj_b, wts["w_b"], nt=True, tk=C, tm=tm, a_layout="nat", out_layout="nat", resid=dh2, name="dh2_b")
    dh2 = _mm(df, wts["w_f"], nt=True, tk=HEAD, tm=tm, a_layout="nat", out_layout="nat", resid=dh2, name="dh2_f")
    dx1, grads["mix_norm"] = _rms_bwd(x1, gm, dh2, dx2, tm, "mix_rms_bwd")
    bt = min(512, T)
    dw_a = _mm_tn(flat(h2a)[None], flat(dproj_a), bm=bn, bn=min(1024, C), bt=bt, name="dw_a")[0]
    dw_b = _mm_tn(h2b[None], dproj_b, bm=bn, bn=min(1024, C), bt=bt, name="dw_b")[0]
    dw_f = _mm_tn(h2b[None], df, bm=bn, bn=HEAD, bt=bt, name="dw_f")[0]
    grads["w_in"] = jnp.concatenate([dw_a, dw_b, dw_f[:, :H]], axis=1)

    grad_x = ffn_backward("ffn1", x, g1, h1, gu1, wts["ffn1_in"], wts["ffn1_out"], dx1)
    return loss_row, grad_x, grads


def _place():
    x, y, c = lax.axis_index("x"), lax.axis_index("y"), lax.axis_index("c")
    other_chips = [(1 - x, y), (x, 1 - y), (1 - x, 1 - y)]
    return x, y, c, other_chips


def _run_side(side, name):
    def body(*refs):
        si, so = len(side.ins), len(side.outs)
        side.start(refs[:si], refs[si:si + so], refs[si + so:])
        side.finish(refs[:si], refs[si:si + so], refs[si + so:])

    return pl.pallas_call(body, name=name, in_specs=[ANY] * len(side.ins), out_specs=[ANY] * len(side.outs),
                          out_shape=side.outs, scratch_shapes=side.sems)(*side.ins)


def _all_gather(shards):
    n = len(shards)

    def plan(ins, outs, sems):
        send_sems, recv_sems, local_sems = sems
        x, y, c, chips = _place()
        me, sibling = (x, y, c), (x, y, 1 - c)

        def copy(a, k, block, to, src=None):
            px, py, pc = block
            dst = outs[a].at[4 * px + 2 * py + pc]
            return pltpu.make_async_remote_copy(
                src_ref=dst if src is None else src, dst_ref=dst, send_sem=send_sems.at[7 * a + k],
                recv_sem=recv_sems.at[7 * a + k], device_id=to, device_id_type=MESH)

        mine = [pltpu.make_async_copy(ins[a], outs[a].at[4 * x + 2 * y + c], local_sems.at[a]) for a in range(n)]
        first = []
        for a in range(n):
            first.append(copy(a, 0, me, sibling, src=ins[a]))
            first += [copy(a, 1 + j, me, (*chip, c), src=ins[a]) for j, chip in enumerate(chips)]
        return copy, mine, first, me, sibling, c, chips

    def start(ins, outs, sems):
        _, mine, first, *_ = plan(ins, outs, sems)
        for cp in mine + first:
            cp.start()

    def finish(ins, outs, sems):
        copy, mine, first, me, sibling, c, chips = plan(ins, outs, sems)
        passed = []
        for a in range(n):
            for j, chip in enumerate(chips):
                copy(a, 1 + j, (*chip, c), me).wait_recv()
                fwd = copy(a, 4 + j, (*chip, c), sibling)
                fwd.start()
                passed.append(fwd)
        for a in range(n):
            copy(a, 0, sibling, me).wait_recv()
            for j, chip in enumerate(chips):
                copy(a, 4 + j, (*chip, 1 - c), me).wait_recv()
        for cp in first + passed:
            cp.wait_send()
        for cp in mine:
            cp.wait()

    return _Side(shards, [_sds((N_DEV,) + s.shape, s.dtype) for s in shards],
                 [pltpu.SemaphoreType.DMA((7 * n,)), pltpu.SemaphoreType.DMA((7 * n,)), pltpu.SemaphoreType.DMA((n,))],
                 start, finish)


def _exchange_in_chip(gs):
    n = len(gs)

    def copies(ins, outs, sems):
        x, y, c, _ = _place()
        return [pltpu.make_async_remote_copy(
            src_ref=ins[a].at[2 * q + 1 - c], dst_ref=outs[a].at[q], send_sem=sems[0].at[4 * a + q],
            recv_sem=sems[1].at[4 * a + q], device_id=(x, y, 1 - c), device_id_type=MESH)
            for a in range(n) for q in range(4)]

    def start(ins, outs, sems):
        for cp in copies(ins, outs, sems):
            cp.start()

    def finish(ins, outs, sems):
        for cp in copies(ins, outs, sems):
            cp.wait()

    return _Side(gs, [_sds((4,) + g.shape[1:], g.dtype) for g in gs],
                 [pltpu.SemaphoreType.DMA((4 * n,)), pltpu.SemaphoreType.DMA((4 * n,))], start, finish)


def _exchange_between_chips(ps):
    n = len(ps)

    def copies(ins, outs, sems):
        x, y, c, chips = _place()
        return [pltpu.make_async_remote_copy(
            src_ref=ins[a].at[2 * cx + cy], dst_ref=outs[a].at[j], send_sem=sems[0].at[3 * a + j],
            recv_sem=sems[1].at[3 * a + j], device_id=(cx, cy, c), device_id_type=MESH)
            for a in range(n) for j, (cx, cy) in enumerate(chips)]

    def start(ins, outs, sems):
        for cp in copies(ins, outs, sems):
            cp.start()

    def finish(ins, outs, sems):
        for cp in copies(ins, outs, sems):
            cp.wait()

    return _Side(ps, [_sds((3,) + p.shape[1:], p.dtype) for p in ps],
                 [pltpu.SemaphoreType.DMA((3 * n,)), pltpu.SemaphoreType.DMA((3 * n,))], start, finish)


def _all_reduce_small(v, name):
    R = v.shape[0]

    def body(v_ref, sum_ref, all_ref, send_sems, recv_sems):
        x, y, c, _ = _place()
        k = 4 * x + 2 * y + c
        all_ref[k] = v_ref[...]
        copies = []
        for rel in range(1, N_DEV):
            fx, fy, fc = (rel >> 2) & 1, (rel >> 1) & 1, rel & 1
            peer = (1 - x if fx else x, 1 - y if fy else y, 1 - c if fc else c)
            copies.append(pltpu.make_async_remote_copy(
                src_ref=v_ref, dst_ref=all_ref.at[k], send_sem=send_sems.at[rel - 1], recv_sem=recv_sems.at[rel - 1],
                device_id=peer, device_id_type=MESH))
        for cp in copies:
            cp.start()
        for rel in range(1, N_DEV):
            fx, fy, fc = (rel >> 2) & 1, (rel >> 1) & 1, rel & 1
            src = 4 * (1 - x if fx else x) + 2 * (1 - y if fy else y) + (1 - c if fc else c)
            pltpu.make_async_remote_copy(
                src_ref=v_ref, dst_ref=all_ref.at[src], send_sem=send_sems.at[rel - 1], recv_sem=recv_sems.at[rel - 1],
                device_id=(x, y, c), device_id_type=MESH).wait_recv()
        for cp in copies:
            cp.wait_send()
        tot = all_ref[0]
        for d in range(1, N_DEV):
            tot = tot + all_ref[d]
        sum_ref[...] = tot

    vm = pl.BlockSpec(memory_space=pltpu.VMEM)
    return pl.pallas_call(
        body, name=name, in_specs=[vm], out_specs=[vm, vm],
        out_shape=[_sds((R, HEAD), F32), _sds((N_DEV, R, HEAD), F32)],
        scratch_shapes=[pltpu.SemaphoreType.DMA((N_DEV - 1,)), pltpu.SemaphoreType.DMA((N_DEV - 1,))],
    )(v)[0]


def _row_tile(rows):
    for cand in (688, 512, 256):
        if rows % cand == 0:
            return cand
    return rows


def _chip_sum(g, r1, core, name):
    _, R, Cc = g.shape
    tr = _row_tile(R)

    def body(core_ref, g_ref, r_ref, p_ref):
        p_ref[...] = (g_ref[...] + r_ref[...]).astype(BF16)

    blk = lambda f: pl.BlockSpec((None, tr, Cc), f)
    return pl.pallas_call(
        body, name=name,
        grid_spec=pltpu.PrefetchScalarGridSpec(
            num_scalar_prefetch=1, grid=(4, R // tr),
            in_specs=[blk(lambda q, i, core: (2 * q + core[0], i, 0)), blk(lambda q, i, core: (q, i, 0))],
            out_specs=blk(lambda q, i, core: (q, i, 0))),
        out_shape=_sds((4, R, Cc), BF16), compiler_params=_cparams(),
    )(core, g, r1)


def _final_sum(g, r1, r2, where, name):
    _, R, Cc = g.shape
    tr = _row_tile(R)

    def body(where_ref, g_ref, r1_ref, r2_ref, o_ref):
        o_ref[...] = ((g_ref[...] + r1_ref[...]) + r2_ref[0].astype(F32)) + (r2_ref[1].astype(F32) + r2_ref[2].astype(F32))

    return pl.pallas_call(
        body, name=name,
        grid_spec=pltpu.PrefetchScalarGridSpec(
            num_scalar_prefetch=1, grid=(R // tr,),
            in_specs=[pl.BlockSpec((None, tr, Cc), lambda i, w: (w[0], i, 0)),
                      pl.BlockSpec((None, tr, Cc), lambda i, w: (w[1], i, 0)),
                      pl.BlockSpec((3, tr, Cc), lambda i, w: (0, i, 0))],
            out_specs=pl.BlockSpec((tr, Cc), lambda i, w: (i, 0))),
        out_shape=_sds((R, Cc), F32), compiler_params=_cparams(),
    )(where, g, r1, r2)


def _adamw(w, g, m, v, name):
    R, Cc = w.shape
    tr = _row_tile(R)

    def body(w_ref, g_ref, m_ref, v_ref, d_ref, nm_ref, nv_ref):
        gv = g_ref[...]
        nm = B1 * m_ref[...] + (1.0 - B1) * gv
        nv = B2 * v_ref[...] + (1.0 - B2) * jnp.square(gv)
        m_hat = nm / (1.0 - B1 ** STEP)
        v_hat = nv / (1.0 - B2 ** STEP)
        d_ref[...] = -LR * (m_hat / (jnp.sqrt(v_hat) + EPS) + WD * w_ref[...])
        nm_ref[...] = nm
        nv_ref[...] = nv

    blk = pl.BlockSpec((tr, Cc), lambda i: (i, 0))
    return pl.pallas_call(
        body, name=name, grid=(R // tr,), in_specs=[blk] * 4, out_specs=[blk] * 3,
        out_shape=[_sds((R, Cc), F32)] * 3, compiler_params=_cparams(),
    )(w, g, m, v)


SMALL = ("ffn1_norm", "mix_norm", "ffn2_norm", "q_norm_a", "k_norm_a", "q_norm_b", "k_norm_b", "forget_bias",
         "rel_bias_table")
LARGE = ("ffn1_w_in", "ffn1_w_out", "w_in", "w_out", "ffn2_w_in", "ffn2_w_out")
ORDER = ("ffn1_norm", "ffn1_w_in", "ffn1_w_out", "mix_norm", "w_in", "q_norm_a", "k_norm_a", "q_norm_b", "k_norm_b",
         "forget_bias", "rel_bias_table", "w_out", "ffn2_norm", "ffn2_w_in", "ffn2_w_out")


def _pack_small(vals):
    rows = []
    for name in SMALL:
        flat = vals[name].reshape(-1)
        pad = (-flat.shape[0]) % HEAD
        rows.append(jnp.pad(flat, (0, pad)).reshape(-1, HEAD))
    return jnp.concatenate(rows, axis=0)


def _unpack_small(packed, like):
    out, r = {}, 0
    for name in SMALL:
        size = like[name].size
        nrow = -(-size // HEAD)
        out[name] = packed[r:r + nrow].reshape(-1)[:size].reshape(like[name].shape)
        r += nrow
    return out


def kernel(x, ffn1_norm, ffn1_w_in, ffn1_w_out, mix_norm, w_in, q_norm_a, k_norm_a, q_norm_b, k_norm_b, forget_bias, rel_bias_table, w_out, ffn2_norm, ffn2_w_in, ffn2_w_out, loss_target, m_ffn1_norm, m_ffn1_w_in, m_ffn1_w_out, m_mix_norm, m_w_in, m_q_norm_a, m_k_norm_a, m_q_norm_b, m_k_norm_b, m_forget_bias, m_rel_bias_table, m_w_out, m_ffn2_norm, m_ffn2_w_in, m_ffn2_w_out, v_ffn1_norm, v_ffn1_w_in, v_ffn1_w_out, v_mix_norm, v_w_in, v_q_norm_a, v_k_norm_a, v_q_norm_b, v_k_norm_b, v_forget_bias, v_rel_bias_table, v_w_out, v_ffn2_norm, v_ffn2_w_in, v_ffn2_w_out):
    w = dict(ffn1_norm=ffn1_norm, ffn1_w_in=ffn1_w_in, ffn1_w_out=ffn1_w_out, mix_norm=mix_norm, w_in=w_in,
             q_norm_a=q_norm_a, k_norm_a=k_norm_a, q_norm_b=q_norm_b, k_norm_b=k_norm_b, forget_bias=forget_bias,
             rel_bias_table=rel_bias_table, w_out=w_out, ffn2_norm=ffn2_norm, ffn2_w_in=ffn2_w_in, ffn2_w_out=ffn2_w_out)
    m = dict(ffn1_norm=m_ffn1_norm, ffn1_w_in=m_ffn1_w_in, ffn1_w_out=m_ffn1_w_out, mix_norm=m_mix_norm, w_in=m_w_in,
             q_norm_a=m_q_norm_a, k_norm_a=m_k_norm_a, q_norm_b=m_q_norm_b, k_norm_b=m_k_norm_b,
             forget_bias=m_forget_bias, rel_bias_table=m_rel_bias_table, w_out=m_w_out, ffn2_norm=m_ffn2_norm,
             ffn2_w_in=m_ffn2_w_in, ffn2_w_out=m_ffn2_w_out)
    v = dict(ffn1_norm=v_ffn1_norm, ffn1_w_in=v_ffn1_w_in, ffn1_w_out=v_ffn1_w_out, mix_norm=v_mix_norm, w_in=v_w_in,
             q_norm_a=v_q_norm_a, k_norm_a=v_k_norm_a, q_norm_b=v_q_norm_b, k_norm_b=v_k_norm_b,
             forget_bias=v_forget_bias, rel_bias_table=v_rel_bias_table, w_out=v_w_out, ffn2_norm=v_ffn2_norm,
             ffn2_w_in=v_ffn2_w_in, ffn2_w_out=v_ffn2_w_out)
    T, D = x.shape[1], x.shape[2]
    C = D // 2
    H = C // HEAD
    ff_shard = ffn1_w_out.shape[1]

    f1i, f1o = _run_side(_all_gather([ffn1_w_in[0].T.astype(BF16), ffn1_w_out[0].astype(BF16)]), "gather_ffn1")
    wts = dict(ffn1_in=f1i.reshape(2, N_DEV, ff_shard, D), ffn1_out=f1o)
    xi, yi, ci = lax.axis_index("x"), lax.axis_index("y"), lax.axis_index("c")
    core = jnp.reshape(ci, (1,)).astype(jnp.int32)
    where = jnp.stack([4 * xi + 2 * yi + ci, 2 * xi + yi]).astype(jnp.int32)
    gs, r1, ps, r2 = {}, {}, {}, {}

    def by_destination(name, grads):
        if name == "w_in":
            gs[name] = jnp.moveaxis(grads["w_in"].reshape(D, N_DEV, -1), 1, 0)
        elif name == "w_out":
            gs[name] = grads["w_out"].reshape(N_DEV, -1, D)
        elif name.endswith("w_in"):
            gs[name] = grads[name + "_t"].reshape(N_DEV, 2 * ff_shard, D)
        else:
            gs[name] = grads[name]
        return gs[name]

    def chip_sums(names):
        for name in names:
            ps[name] = _chip_sum(gs[name], r1[name], core, "chip_sum_" + name)
        return [ps[name] for name in names]

    class Plan:
        carried = {"ffn1_fwd": ("gather", ("w_in", "w_out")), "fox_fwd": ("gather", ("ffn2_w_in", "ffn2_w_out")),
                   "dil_bwd": ("in_chip", ("ffn2_w_in", "ffn2_w_out")), "fox_bwd": ("between", ("ffn2_w_in", "ffn2_w_out")),
                   "ffn1_bwd": ("in_chip", ("w_in", "w_out")), "ffn1_dwin": ("between", ("w_in", "w_out")),
                   "ffn1_dwout": ("in_chip", ("ffn1_w_in",))}

        def before(self, host, wts, grads):
            if host not in self.carried:
                return None
            kind, names = self.carried[host]
            if kind == "gather":
                return _all_gather([(w[n][0].T if n == "ffn2_w_in" else w[n][0]).astype(BF16) for n in names])
            if kind == "in_chip":
                return _exchange_in_chip([by_destination(n, grads) for n in names])
            return _exchange_between_chips(chip_sums(names))

        def after(self, host, res, wts, grads):
            kind, names = self.carried[host]
            if host == "ffn1_fwd":
                wi = jnp.moveaxis(res[0], 0, 1).reshape(D, -1)
                wts.update(w_a=wi[:, :3 * C], w_b=wi[:, 3 * C:6 * C], w_f=_pad_lanes(wi[:, 6 * C:]),
                           w_o=res[1].reshape(2 * C, D))
            elif host == "fox_fwd":
                wts.update(ffn2_in=res[0].reshape(2, N_DEV, ff_shard, D), ffn2_out=res[1])
            else:
                (r1 if kind == "in_chip" else r2).update(zip(names, res))

    small = {name: w[name] for name in SMALL}
    loss_row, grad_x, grads = _local_step(x[0], loss_target[0], small, wts, Plan())

    r1["ffn1_w_out"], = _run_side(_exchange_in_chip([by_destination("ffn1_w_out", grads)]), "reduce_in_chip_tail")
    tail = ("ffn1_w_in", "ffn1_w_out")
    r2.update(zip(tail, _run_side(_exchange_between_chips(chip_sums(tail)), "reduce_between_chips_tail")))
    g_large = {name: _final_sum(gs[name], r1[name], r2[name], where, "final_sum_" + name) for name in LARGE}
    g_large["ffn1_w_in"] = g_large["ffn1_w_in"].T
    g_large["ffn2_w_in"] = g_large["ffn2_w_in"].T

    packed = _pack_small(grads)
    nsmall = packed.shape[0]
    packed = jnp.concatenate([packed, loss_row, jnp.zeros(((-nsmall - 1) % 8, HEAD), F32)], axis=0)
    reduced = _all_reduce_small(packed, "reduce_small")
    loss = reduced[nsmall, 0]
    g_small = _unpack_small(reduced[:nsmall], small)

    grad, delta, new_m, new_v = dict(g_small), {}, {}, {}
    for name in LARGE:
        grad[name] = g_large[name][None]
        d, nm, nv = _adamw(w[name][0], g_large[name], m[name][0], v[name][0], "adamw_" + name)
        delta[name], new_m[name], new_v[name] = d[None], nm[None], nv[None]
    d, nm, nv = _adamw(_pack_small(w), reduced[:nsmall], _pack_small(m), _pack_small(v), "adamw_small")
    delta.update(_unpack_small(d, small))
    new_m.update(_unpack_small(nm, small))
    new_v.update(_unpack_small(nv, small))
    return (loss, grad_x[None], *[grad[n] for n in ORDER], *[delta[n] for n in ORDER],
            *[new_m[n] for n in ORDER], *[new_v[n] for n in ORDER])
```

```python
import functools
import math

import numpy as np
import jax
import jax.numpy as jnp
from jax import lax
from jax.experimental import pallas as pl
from jax.experimental.pallas import tpu as pltpu

F32, BF16 = jnp.float32, jnp.bfloat16
HEAD = 128
NSLAB = 16
BLK = 128
DILATIONS = (1, 4, 16)
NUM_BUCKETS, MAX_DISTANCE = 32, 2048
RMS_EPS = 1e-6
NEG = -1e30
SCALE = HEAD ** -0.5
LR, B1, B2, EPS, WD, STEP = 0.001, 0.9, 0.999, 1e-08, 0.01, 10
N_DEV = 8
VMEM_LIMIT_BYTES = 56 << 20
MESH = pl.DeviceIdType.MESH


def _cparams(**kw):
    return pltpu.CompilerParams(vmem_limit_bytes=VMEM_LIMIT_BYTES, **kw)


def _nn(a, b):
    return jnp.dot(a, b, preferred_element_type=F32)


def _nt(a, b):
    return lax.dot_general(a, b, (((1,), (1,)), ((), ())), preferred_element_type=F32)


def _tn(a, b):
    return lax.dot_general(a, b, (((0,), (0,)), ((), ())), preferred_element_type=F32)


def _sds(shape, dtype):
    return jax.ShapeDtypeStruct(shape, dtype)


ANY = pl.BlockSpec(memory_space=pl.ANY)


class _Side:
    def __init__(self, ins, outs, sems, start, finish):
        self.ins, self.outs, self.sems, self.start, self.finish = list(ins), list(outs), list(sems), start, finish


def _call(body, *, name, grid, in_specs, out_specs, out_shape, args, scratch_shapes=(), side=None):
    in_specs, out_specs, out_shape = list(in_specs), list(out_specs), list(out_shape)
    scratch_shapes = list(scratch_shapes)
    if side is None:
        return pl.pallas_call(body, name=name, grid=grid, in_specs=in_specs, out_specs=out_specs, out_shape=out_shape,
                              scratch_shapes=scratch_shapes, compiler_params=_cparams())(*args)
    ni, no, ns = len(args), len(out_shape), len(scratch_shapes)
    si, so = len(side.ins), len(side.outs)

    def fused(*refs):
        h_in, s_in = refs[:ni], refs[ni:ni + si]
        h_out, s_out = refs[ni + si:ni + si + no], refs[ni + si + no:ni + si + no + so]
        h_scr, s_sem = refs[ni + si + no + so:ni + si + no + so + ns], refs[ni + si + no + so + ns:]
        ids = [pl.program_id(k) for k in range(len(grid))]
        first = functools.reduce(jnp.logical_and, [i == 0 for i in ids])
        last = functools.reduce(jnp.logical_and, [i == n - 1 for i, n in zip(ids, grid)])

        @pl.when(first)
        def _():
            side.start(s_in, s_out, s_sem)

        body(*h_in, *h_out, *h_scr)

        @pl.when(last)
        def _():
            side.finish(s_in, s_out, s_sem)

    res = pl.pallas_call(
        fused, name=name, grid=grid, in_specs=in_specs + [ANY] * si, out_specs=out_specs + [ANY] * so,
        out_shape=out_shape + side.outs, scratch_shapes=scratch_shapes + side.sems, compiler_params=_cparams(),
    )(*args, *side.ins)
    return list(res[:no]), list(res[no:])


def _ffn_fwd(x, g, win, wout, tm, name, side=None):
    T, D = x.shape
    nc, tf = wout.shape[0], wout.shape[1]

    def body(x_ref, g_ref, win_ref, wout_ref, y_ref, h_ref, gu_ref):
        j = pl.program_id(1)

        @pl.when(j == 0)
        def _():
            xv = x_ref[...]
            r = lax.rsqrt(jnp.mean(xv * xv, axis=-1, keepdims=True) + RMS_EPS)
            h_ref[...] = (xv * r * g_ref[...]).astype(BF16)
            y_ref[...] = jnp.zeros_like(y_ref)

        hb = h_ref[...]
        gt = _nt(hb, win_ref[0])
        up = _nt(hb, win_ref[1])
        gu_ref[0] = gt.astype(BF16)
        gu_ref[1] = up.astype(BF16)
        act = (gt * jax.nn.sigmoid(gt) * up).astype(BF16)
        y_ref[...] += _nn(act, wout_ref[...])

        @pl.when(j == nc - 1)
        def _():
            y_ref[...] = x_ref[...] + 0.5 * y_ref[...]

    return _call(
        body, name=name, grid=(T // tm, nc), side=side,
        in_specs=[pl.BlockSpec((tm, D), lambda i, j: (i, 0)),
                  pl.BlockSpec((1, D), lambda i, j: (0, 0)),
                  pl.BlockSpec((2, None, tf, D), lambda i, j: (0, j, 0, 0)),
                  pl.BlockSpec((None, tf, D), lambda i, j: (j, 0, 0))],
        out_specs=[pl.BlockSpec((tm, D), lambda i, j: (i, 0)),
                   pl.BlockSpec((tm, D), lambda i, j: (i, 0)),
                   pl.BlockSpec((2, None, tm, tf), lambda i, j: (0, j, i, 0))],
        out_shape=[_sds((T, D), F32), _sds((T, D), BF16), _sds((2, nc, T, tf), BF16)],
        args=(x, g, win, wout))


def _ffn_bwd(dy, gu, win, wout, tm, name, side=None):
    T, D = dy.shape
    nc, tf = wout.shape[0], wout.shape[1]

    def body(dy_ref, gu_ref, win_ref, wout_ref, dh_ref, dgu_ref, act_ref):
        j = pl.program_id(1)

        @pl.when(j == 0)
        def _():
            dh_ref[...] = jnp.zeros_like(dh_ref)

        dyb = (0.5 * dy_ref[...]).astype(BF16)
        dact = _nt(dyb, wout_ref[...])
        gt = gu_ref[0].astype(F32)
        up = gu_ref[1].astype(F32)
        s = jax.nn.sigmoid(gt)
        silu = gt * s
        dgb = (dact * up * (s * (1.0 + gt * (1.0 - s)))).astype(BF16)
        dub = (dact * silu).astype(BF16)
        dgu_ref[0] = dgb
        dgu_ref[1] = dub
        act_ref[...] = (0.5 * silu * up).astype(BF16)
        dh_ref[...] += _nn(dgb, win_ref[0]) + _nn(dub, win_ref[1])

    return _call(
        body, name=name, grid=(T // tm, nc), side=side,
        in_specs=[pl.BlockSpec((tm, D), lambda i, j: (i, 0)),
                  pl.BlockSpec((2, None, tm, tf), lambda i, j: (0, j, i, 0)),
                  pl.BlockSpec((2, None, tf, D), lambda i, j: (0, j, 0, 0)),
                  pl.BlockSpec((None, tf, D), lambda i, j: (j, 0, 0))],
        out_specs=[pl.BlockSpec((tm, D), lambda i, j: (i, 0)),
                   pl.BlockSpec((2, None, tm, tf), lambda i, j: (0, j, i, 0)),
                   pl.BlockSpec((None, tm, tf), lambda i, j: (j, i, 0))],
        out_shape=[_sds((T, D), F32), _sds((2, nc, T, tf), BF16), _sds((nc, T, tf), BF16)],
        args=(dy, gu, win, wout))


def _rms_bwd(x, g, dh, dres, tm, name):
    T, D = x.shape

    def body(x_ref, g_ref, dh_ref, dres_ref, dx_ref, dg_ref):
        @pl.when(pl.program_id(0) == 0)
        def _():
            dg_ref[...] = jnp.zeros_like(dg_ref)

        xv = x_ref[...]
        r = lax.rsqrt(jnp.mean(xv * xv, axis=-1, keepdims=True) + RMS_EPS)
        xhat = xv * r
        dh = dh_ref[...]
        gd = dh * g_ref[...]
        dx_ref[...] = dres_ref[...] + r * (gd - xhat * jnp.mean(gd * xhat, axis=-1, keepdims=True))
        dg_ref[...] += jnp.sum(dh * xhat, axis=0, keepdims=True)

    row = pl.BlockSpec((tm, D), lambda i: (i, 0))
    one = pl.BlockSpec((1, D), lambda i: (0, 0))
    return pl.pallas_call(
        body, name=name, grid=(T // tm,), in_specs=[row, one, row, row], out_specs=[row, one],
        out_shape=[_sds((T, D), F32), _sds((1, D), F32)], compiler_params=_cparams(),
    )(x, g, dh, dres)


def _mm_tn(a, b, *, bm, bn, bt, name, b_slabs=False, side=None):
    nz, T, M = a.shape
    if b_slabs:
        N = b.shape[1] // NSLAB
        assert bt == T // NSLAB
        b_spec = pl.BlockSpec((bt, bn), lambda z, m, n, t: (0, t * (N // bn) + n))
    else:
        N = b.shape[1]
        b_spec = pl.BlockSpec((bt, bn), lambda z, m, n, t: (t, n))
    assert M % bm == 0 and N % bn == 0 and T % bt == 0, (M, bm, N, bn, T, bt)

    def body(a_ref, b_ref, c_ref):
        @pl.when(pl.program_id(3) == 0)
        def _():
            c_ref[...] = jnp.zeros_like(c_ref)

        c_ref[...] += _tn(a_ref[...].astype(BF16), b_ref[...].astype(BF16))

    res = _call(
        body, name=name, grid=(nz, M // bm, N // bn, T // bt), side=side,
        in_specs=[pl.BlockSpec((None, bt, bm), lambda z, m, n, t: (z, t, m)), b_spec],
        out_specs=[pl.BlockSpec((None, bm, bn), lambda z, m, n, t: (z, m, n))],
        out_shape=[_sds((nz, M, N), F32)], args=(a, b))
    return res[0] if side is None else (res[0][0], res[1])


def _proj(x, g, w, gains, modes, *, slabs, tm, normed_dtype, name):
    T, D = x.shape
    N = w.shape[1]
    ntile = len(modes)
    tn = N // ntile
    n16 = T // NSLAB
    if slabs:
        assert tm == n16
        x_in = x.reshape(n16, NSLAB * D)
        x_spec = pl.BlockSpec((tm, D), lambda i, n: (0, i))
        oshape = lambda c: (NSLAB, n16, c)
        ospec = lambda bc, cm: pl.BlockSpec((None, tm, bc), lambda i, n: (i, 0, cm(n)))
    else:
        x_in = x
        x_spec = pl.BlockSpec((tm, D), lambda i, n: (i, 0))
        oshape = lambda c: (T, c)
        ospec = lambda bc, cm: pl.BlockSpec((tm, bc), lambda i, n: (i, cm(n)))

    def body(x_ref, g_ref, w_ref, gains_ref, raw_ref, nrm_ref, h_ref):
        n = pl.program_id(1)

        @pl.when(n == 0)
        def _():
            xv = x_ref[...]
            r = lax.rsqrt(jnp.mean(xv * xv, axis=-1, keepdims=True) + RMS_EPS)
            h_ref[...] = (xv * r * g_ref[...]).astype(BF16)

        y = _nn(h_ref[...], w_ref[...])
        raw_ref[...] = y
        for t, mode in enumerate(modes):
            @pl.when(n == t)
            def _(t=t, mode=mode):
                if not mode:
                    nrm_ref[...] = y.astype(nrm_ref.dtype)
                    return
                gain = gains_ref[t]
                for k in range(tn // HEAD):
                    yk = y[:, k * HEAD:(k + 1) * HEAD]
                    r = lax.rsqrt(jnp.mean(yk * yk, axis=-1, keepdims=True) + RMS_EPS)
                    nrm_ref[:, k * HEAD:(k + 1) * HEAD] = (yk * r * gain).astype(nrm_ref.dtype)

    return pl.pallas_call(
        body, name=name, grid=(T // tm, ntile),
        in_specs=[x_spec, pl.BlockSpec((1, D), lambda i, n: (0, 0)), pl.BlockSpec((D, tn), lambda i, n: (0, n)),
                  pl.BlockSpec((ntile, 1, HEAD), lambda i, n: (0, 0, 0))],
        out_specs=[ospec(tn, lambda n: n), ospec(tn, lambda n: n), ospec(D, lambda n: 0)],
        out_shape=[_sds(oshape(N), F32), _sds(oshape(N), normed_dtype), _sds(oshape(D), BF16)],
        compiler_params=_cparams(),
    )(x_in, g, w, gains)


def _mm(a, w, *, nt, tk, tm, a_layout, out_layout, resid=None, name):
    if a_layout == "slab":
        T, K = a.shape[0] * a.shape[1], a.shape[2]
    else:
        T, K = a.shape
    N = w.shape[0] if nt else w.shape[1]
    n16 = T // NSLAB
    nk = K // tk

    def spec(layout, C, bc, colmap):
        if layout == "nat":
            return pl.BlockSpec((tm, bc), lambda i, k: (i, colmap(k)))
        assert tm == n16
        if layout == "slab":
            return pl.BlockSpec((None, tm, bc), lambda i, k: (i, 0, colmap(k)))
        assert bc == C
        return pl.BlockSpec((tm, C), lambda i, k: (0, i))

    a_in = a.reshape(n16, NSLAB * K) if a_layout == "view" else a
    w_spec = pl.BlockSpec((N, tk), lambda i, k: (0, k)) if nt else pl.BlockSpec((tk, N), lambda i, k: (k, 0))
    o_spec = spec(out_layout, N, N, lambda k: 0)
    oshape = {"nat": (T, N), "slab": (NSLAB, n16, N), "view": (n16, NSLAB * N)}[out_layout]
    has_resid = resid is not None

    def body(*refs):
        a_ref, w_ref = refs[0], refs[1]
        o_ref = refs[-1]
        k = pl.program_id(1)

        @pl.when(k == 0)
        def _():
            o_ref[...] = refs[2][...] if has_resid else jnp.zeros_like(o_ref)

        ab = a_ref[...].astype(BF16)
        o_ref[...] += _nt(ab, w_ref[...]) if nt else _nn(ab, w_ref[...])

    ins = [a_in, w]
    in_specs = [spec(a_layout, K, tk, lambda k: k), w_spec]
    if has_resid:
        ins.append(resid.reshape(n16, NSLAB * N) if out_layout == "view" else resid)
        in_specs.append(o_spec)
    out = pl.pallas_call(
        body, name=name, grid=(T // tm, nk), in_specs=in_specs, out_specs=o_spec,
        out_shape=_sds(oshape, F32), compiler_params=_cparams(),
    )(*ins)
    return out.reshape(T, N) if out_layout == "view" else out


def _log_sigmoid(z):
    return jnp.minimum(z, 0.0) - jnp.log(1.0 + jnp.exp(-jnp.abs(z)))


def _fox_gate_fwd(f_raw, fbias, name):
    T = f_raw.shape[0]
    cb = 256

    def body(f_ref, b_ref, c_ref):
        row = lax.broadcasted_iota(jnp.int32, (cb, cb), 0)
        col = lax.broadcasted_iota(jnp.int32, (cb, cb), 1)
        tri = (col <= row).astype(F32)
        carry = jnp.zeros((1, HEAD), F32)
        for i in range(T // cb):
            lf = _log_sigmoid(f_ref[i * cb:(i + 1) * cb, :] + b_ref[...])
            c = jnp.dot(tri, lf, preferred_element_type=F32, precision=lax.Precision.HIGHEST) + carry
            c_ref[i * cb:(i + 1) * cb, :] = c
            carry = c[cb - 1:cb, :]

    return pl.pallas_call(body, name=name, out_shape=_sds((T, HEAD), F32), compiler_params=_cparams())(f_raw, fbias)


def _fox_gate_bwd(f_raw, fbias, dc, name):
    T = f_raw.shape[0]
    cb = 256

    def body(f_ref, b_ref, dc_ref, df_ref, db_ref):
        row = lax.broadcasted_iota(jnp.int32, (cb, cb), 0)
        col = lax.broadcasted_iota(jnp.int32, (cb, cb), 1)
        tri = (col >= row).astype(F32)
        carry = jnp.zeros((1, HEAD), F32)
        dbias = jnp.zeros((1, HEAD), F32)
        for i in reversed(range(T // cb)):
            dlf = jnp.dot(tri, dc_ref[i * cb:(i + 1) * cb, :], preferred_element_type=F32,
                          precision=lax.Precision.HIGHEST) + carry
            carry = dlf[0:1, :]
            z = f_ref[i * cb:(i + 1) * cb, :] + b_ref[...]
            df = dlf * jax.nn.sigmoid(-z)
            df_ref[i * cb:(i + 1) * cb, :] = df
            dbias = dbias + jnp.sum(df, axis=0, keepdims=True)
        db_ref[...] = dbias

    return pl.pallas_call(body, name=name, out_shape=[_sds((T, HEAD), F32), _sds((1, HEAD), F32)],
                          compiler_params=_cparams())(f_raw, fbias, dc)


def _fox_fwd(qkv, c_col, c_row, tq, name, side=None):
    T = qkv.shape[0]
    H = qkv.shape[1] // (3 * HEAD)
    nq = T // tq
    c_blocks = c_row.reshape(H, nq, 1, tq)

    def body(q_ref, k_ref, v_ref, cq_ref, ck_ref, o_ref, lse_ref):
        qi = pl.program_id(1)
        q, cq = q_ref[...], cq_ref[...]
        causal = lax.broadcasted_iota(jnp.int32, (tq, tq), 1) <= lax.broadcasted_iota(jnp.int32, (tq, tq), 0)

        def key_block(ki, carry, diagonal):
            m, l, acc = carry
            rows = pl.ds(pl.multiple_of(ki * tq, tq), tq)
            s = _nt(q, k_ref[rows, :]) * SCALE + cq - ck_ref[ki]
            if diagonal:
                s = jnp.where(causal, s, NEG)
            m_new = jnp.maximum(m, jnp.max(s, axis=-1, keepdims=True))
            alpha = jnp.exp(m - m_new)
            p = jnp.exp(s - m_new)
            l = alpha * l + jnp.sum(p, axis=-1, keepdims=True)
            acc = alpha * acc + _nn(p.astype(BF16), v_ref[rows, :])
            return m_new, l, acc

        init = (jnp.full((tq, 1), NEG, F32), jnp.zeros((tq, 1), F32), jnp.zeros((tq, HEAD), F32))
        carry = lax.fori_loop(0, qi, lambda ki, c: key_block(ki, c, False), init)
        m, l, acc = key_block(qi, carry, True)
        o_ref[...] = acc / l
        lse_ref[...] = m + jnp.log(l)

    return _call(
        body, name=name, grid=(H, nq), side=side,
        in_specs=[pl.BlockSpec((tq, HEAD), lambda h, qi: (qi, h)),
                  pl.BlockSpec((T, HEAD), lambda h, qi: (0, H + h)),
                  pl.BlockSpec((T, HEAD), lambda h, qi: (0, 2 * H + h)),
                  pl.BlockSpec((None, tq, 1), lambda h, qi: (h, qi, 0)),
                  pl.BlockSpec((None, nq, 1, tq), lambda h, qi: (h, 0, 0, 0))],
        out_specs=[pl.BlockSpec((tq, HEAD), lambda h, qi: (qi, h)),
                   pl.BlockSpec((None, tq, 1), lambda h, qi: (h, qi, 0))],
        out_shape=[_sds((T, H * HEAD), F32), _sds((H, T, 1), F32)],
        args=(qkv, qkv, qkv, c_col, c_blocks))


def _fox_bwd(qkv, c_col, c_row, out, dout, lse, tq, name, side=None):
    T = qkv.shape[0]
    H = qkv.shape[1] // (3 * HEAD)
    nq = T // tq

    def body(q_ref, k_ref, v_ref, cq_ref, ck_ref, o_ref, do_ref, lse_ref, dq_ref, dk_ref, dv_ref, dck_ref, dcq_ref,
             delta_s):
        ki = pl.program_id(1)

        @pl.when(ki == 0)
        def _():
            dq_ref[...] = jnp.zeros_like(dq_ref)
            dcq_ref[...] = jnp.zeros_like(dcq_ref)
            delta_s[...] = jnp.sum(do_ref[...] * o_ref[...], axis=-1, keepdims=True)

        k, v, ck = k_ref[...], v_ref[...], ck_ref[...]
        causal = lax.broadcasted_iota(jnp.int32, (tq, tq), 1) <= lax.broadcasted_iota(jnp.int32, (tq, tq), 0)

        def query_block(qi, carry, diagonal):
            dk, dv, dck = carry
            rows = pl.ds(pl.multiple_of(qi * tq, tq), tq)
            q = q_ref[rows, :]
            s = _nt(q, k) * SCALE + cq_ref[rows, :] - ck
            if diagonal:
                s = jnp.where(causal, s, NEG)
            p = jnp.exp(s - lse_ref[rows, :])
            dob = do_ref[rows, :].astype(BF16)
            ds = p * (_nt(dob, v) - delta_s[rows, :])
            dsb = ds.astype(BF16)
            dq_ref[rows, :] += _nn(dsb, k) * SCALE
            dcq_ref[rows, :] += jnp.sum(ds, axis=-1, keepdims=True)
            return dk + _tn(dsb, q), dv + _tn(p.astype(BF16), dob), dck - jnp.sum(ds, axis=0, keepdims=True)

        init = (jnp.zeros((tq, HEAD), F32), jnp.zeros((tq, HEAD), F32), jnp.zeros((1, tq), F32))
        carry = query_block(ki, init, True)
        dk, dv, dck = lax.fori_loop(ki + 1, nq, lambda qi, c: query_block(qi, c, False), carry)
        dk_ref[...] = dk * SCALE
        dv_ref[...] = dv
        dck_ref[...] = dck

    head = lambda off: pl.BlockSpec((T, HEAD), lambda h, ki: (0, off + h))
    col = pl.BlockSpec((None, T, 1), lambda h, ki: (h, 0, 0))
    return _call(
        body, name=name, grid=(H, nq), side=side,
        in_specs=[head(0),
                  pl.BlockSpec((tq, HEAD), lambda h, ki: (ki, H + h)),
                  pl.BlockSpec((tq, HEAD), lambda h, ki: (ki, 2 * H + h)),
                  col, pl.BlockSpec((None, 1, tq), lambda h, ki: (h, 0, ki)), head(0), head(0), col],
        out_specs=[head(0),
                   pl.BlockSpec((tq, HEAD), lambda h, ki: (ki, h)),
                   pl.BlockSpec((tq, HEAD), lambda h, ki: (ki, h)),
                   pl.BlockSpec((None, 1, tq), lambda h, ki: (h, 0, ki)), col],
        out_shape=[_sds((T, H * HEAD), F32), _sds((T, H * HEAD), F32), _sds((T, H * HEAD), F32), _sds((H, 1, T), F32),
                   _sds((H, T, 1), F32)],
        scratch_shapes=[pltpu.VMEM((T, 1), F32)],
        args=(qkv, qkv, qkv, c_col, c_row, out, dout, lse))


def _t5_bucket(dist):
    max_exact = NUM_BUCKETS // 2
    d = dist.astype(np.float32)
    large = max_exact + (np.log(np.maximum(d, np.float32(1.0)) / np.float32(max_exact))
                         / np.float32(math.log(MAX_DISTANCE / max_exact))
                         * np.float32(NUM_BUCKETS - max_exact)).astype(np.int32)
    large = np.minimum(large, NUM_BUCKETS - 1)
    return np.where(dist < max_exact, dist, large)


def _bucket_maps():
    maps = []
    for d in DILATIONS:
        e = NSLAB // d
        rows = BLK // e
        idx = np.arange(BLK)
        pos = e * (idx % rows) + idx // rows
        qpos = pos[:, None] + BLK
        kpos = np.concatenate([pos, pos + BLK])[None, :]
        delta = qpos - kpos
        band = (delta >= 0) & (delta <= BLK)
        bucket = _t5_bucket(np.clip(delta, 0, None) * d)
        maps.append(np.where(band, bucket, -1).astype(np.int32))
    return np.stack(maps)


def _dil_geometry(T):
    n16 = T // NSLAB
    geo = []
    for d in DILATIONS:
        e = NSLAB // d
        rows = BLK // e
        nblk = n16 // rows
        geo.append((d, e, rows, nblk))
    return geo


DIL_INTERLEAVE_FWD = {1: 4, 4: 8, 16: 8}
DIL_INTERLEAVE_BWD = {1: 8, 4: 8, 16: 8}


def _dil_interleave(per_step, nblocks):
    while per_step > 1 and (nblocks % per_step or nblocks // per_step < 2):
        per_step -= 1
    return per_step


def _dil_bias(tab_ref, bkt_ref, bias_s, h):
    for p in range(len(DILATIONS)):
        bk = bkt_ref[p]
        bias = jnp.full((BLK, 2 * BLK), NEG, F32)
        for b in range(NUM_BUCKETS):
            bias = jnp.where(bk == b, tab_ref[b, h], bias)
        bias_s[p] = bias


def _dil_rows(d, e, rows, sub, blk):
    start = pl.multiple_of(blk * rows, rows)
    return [(sub + d * j, pl.ds(start, rows)) for j in range(e)]


def _gather(ref, idx):
    return jnp.concatenate([ref[s, r, :] for s, r in idx], axis=0)


def _scatter(ref, idx, val, rows):
    for j, (s, r) in enumerate(idx):
        ref[s, r, :] = val[j * rows:(j + 1) * rows]


def _scatter_add(ref, idx, val, rows):
    for j, (s, r) in enumerate(idx):
        ref[s, r, :] += val[j * rows:(j + 1) * rows]


def _dil_fwd(qkv, table, name):
    n16 = qkv.shape[1]
    T = NSLAB * n16
    H = qkv.shape[2] // (3 * HEAD)
    geo = _dil_geometry(T)
    bkt = jnp.asarray(_bucket_maps())

    def body(tab_ref, bkt_ref, q_ref, k_ref, v_ref, o_ref, lse_ref, bias_s, m_s, l_s):
        h = pl.program_id(0)
        _dil_bias(tab_ref, bkt_ref, bias_s, h)
        first_mask = lax.broadcasted_iota(jnp.int32, (BLK, 2 * BLK), 1) < BLK

        starts = len(DILATIONS) - 1

        def load(p, d, e, rows, sub, blk):
            cur = _dil_rows(d, e, rows, sub, blk)
            prev = _dil_rows(d, e, rows, sub, jnp.maximum(blk - 1, 0))
            q = _gather(q_ref, cur).astype(BF16)
            kk = jnp.concatenate([_gather(k_ref, prev), _gather(k_ref, cur)], axis=0).astype(BF16)
            vv = jnp.concatenate([_gather(v_ref, prev), _gather(v_ref, cur)], axis=0).astype(BF16)
            old = None if p == starts else (_gather(m_s, cur), _gather(l_s, cur), _gather(o_ref, cur))
            return cur, blk, q, kk, vv, old

        def compute(p, blk, q, kk, vv, old):
            s = _nt(q, kk) * SCALE + bias_s[p]
            s = jnp.where(first_mask & (blk == 0), NEG, s)
            m_blk = jnp.max(s, axis=-1, keepdims=True)
            if old is None:
                m_new = m_blk
                pr = jnp.exp(s - m_new)
                l_new = jnp.sum(pr, axis=-1, keepdims=True)
                acc = _nn(pr.astype(BF16), vv)
            else:
                m_old, l_old, acc_old = old
                m_new = jnp.maximum(m_old, m_blk)
                alpha = jnp.exp(m_old - m_new)
                pr = jnp.exp(s - m_new)
                l_new = alpha * l_old + jnp.sum(pr, axis=-1, keepdims=True)
                acc = alpha * acc_old + _nn(pr.astype(BF16), vv)
            if p == 0:
                return acc / l_new, m_new + jnp.log(l_new), None
            return acc, m_new, l_new

        def store(p, rows, cur, acc, m_new, l_new):
            _scatter(o_ref, cur, acc, rows)
            if p == 0:
                _scatter(lse_ref, cur, m_new, rows)
            else:
                _scatter(m_s, cur, m_new, rows)
                _scatter(l_s, cur, l_new, rows)

        for p in reversed(range(len(DILATIONS))):
            d, e, rows, nblk = geo[p]
            per_step = _dil_interleave(DIL_INTERLEAVE_FWD[d], d * nblk)

            def step(i, carry, p=p, d=d, e=e, rows=rows, nblk=nblk, per_step=per_step):
                ids = [i + u * (d * nblk // per_step) for u in range(per_step)]
                loaded = [load(p, d, e, rows, j // nblk, j % nblk) for j in ids]
                done = [(cur, compute(p, blk, q, kk, vv, old)) for cur, blk, q, kk, vv, old in loaded]
                for cur, res in done:
                    store(p, rows, cur, *res)
                return carry

            lax.fori_loop(0, d * nblk // per_step, step, 0)

    head = lambda off: pl.BlockSpec((NSLAB, n16, HEAD), lambda h: (0, 0, off + h))
    return pl.pallas_call(
        body, name=name, grid=(H,),
        in_specs=[pl.BlockSpec(memory_space=pltpu.SMEM), pl.BlockSpec((3, BLK, 2 * BLK), lambda h: (0, 0, 0)),
                  head(0), head(H), head(2 * H)],
        out_specs=[head(0), pl.BlockSpec((None, NSLAB, n16, 1), lambda h: (h, 0, 0, 0))],
        out_shape=[_sds((NSLAB, n16, H * HEAD), F32), _sds((H, NSLAB, n16, 1), F32)],
        scratch_shapes=[pltpu.VMEM((3, BLK, 2 * BLK), F32), pltpu.VMEM((NSLAB, n16, 1), F32),
                        pltpu.VMEM((NSLAB, n16, 1), F32)],
        compiler_params=_cparams(),
    )(table, bkt, qkv, qkv, qkv)


def _dil_bwd(qkv, table, out, dout, lse, name, side=None):
    n16 = qkv.shape[1]
    T = NSLAB * n16
    H = qkv.shape[2] // (3 * HEAD)
    geo = _dil_geometry(T)
    bkt = jnp.asarray(_bucket_maps())

    def body(tab_ref, bkt_ref, q_ref, k_ref, v_ref, o_ref, do_ref, lse_ref,
             dq_ref, dk_ref, dv_ref, dtab_ref, bias_s, dbias_s, delta_s):
        h = pl.program_id(0)
        _dil_bias(tab_ref, bkt_ref, bias_s, h)
        first_mask = lax.broadcasted_iota(jnp.int32, (BLK, 2 * BLK), 1) < BLK
        dbias_s[...] = jnp.zeros_like(dbias_s)
        dq_ref[...] = jnp.zeros_like(dq_ref)
        dk_ref[...] = jnp.zeros_like(dk_ref)
        dv_ref[...] = jnp.zeros_like(dv_ref)
        for r in range(NSLAB):
            delta_s[r] = jnp.sum(do_ref[r] * o_ref[r], axis=-1, keepdims=True)

        def load(d, e, rows, sub, blk):
            cur = _dil_rows(d, e, rows, sub, blk)
            prev = _dil_rows(d, e, rows, sub, jnp.maximum(blk - 1, 0))
            q = _gather(q_ref, cur).astype(BF16)
            kk = jnp.concatenate([_gather(k_ref, prev), _gather(k_ref, cur)], axis=0).astype(BF16)
            vv = jnp.concatenate([_gather(v_ref, prev), _gather(v_ref, cur)], axis=0).astype(BF16)
            dob = _gather(do_ref, cur).astype(BF16)
            return cur, prev, blk, q, kk, vv, dob, _gather(lse_ref, cur), _gather(delta_s, cur)

        def compute(p, blk, q, kk, vv, dob, lse, delta):
            s = _nt(q, kk) * SCALE + bias_s[p]
            s = jnp.where(first_mask & (blk == 0), NEG, s)
            pr = jnp.exp(s - lse)
            ds = pr * (_nt(dob, vv) - delta)
            dsb = ds.astype(BF16)
            return ds, _nn(dsb, kk) * SCALE, _tn(dsb, q) * SCALE, _tn(pr.astype(BF16), dob)

        def store(rows, cur, prev, dq, dkk, dvv):
            _scatter_add(dq_ref, cur, dq, rows)
            _scatter_add(dk_ref, prev, dkk[:BLK], rows)
            _scatter_add(dk_ref, cur, dkk[BLK:], rows)
            _scatter_add(dv_ref, prev, dvv[:BLK], rows)
            _scatter_add(dv_ref, cur, dvv[BLK:], rows)

        for p in range(len(DILATIONS)):
            d, e, rows, nblk = geo[p]
            per_step = _dil_interleave(DIL_INTERLEAVE_BWD[d], d * nblk)

            def step(i, carry, p=p, d=d, e=e, rows=rows, nblk=nblk, per_step=per_step):
                ids = [i + u * (d * nblk // per_step) for u in range(per_step)]
                loaded = [load(d, e, rows, j // nblk, j % nblk) for j in ids]
                done = [(cur, prev, compute(p, *rest)) for cur, prev, *rest in loaded]
                dbias_s[p] += functools.reduce(jnp.add, [res[0] for _, _, res in done])
                for cur, prev, res in done:
                    store(rows, cur, prev, *res[1:])
                return carry

            lax.fori_loop(0, d * nblk // per_step, step, 0)

        lane = lax.broadcasted_iota(jnp.int32, (1, HEAD), 1)
        row = jnp.zeros((1, HEAD), F32)
        for b in range(NUM_BUCKETS):
            tot = jnp.zeros((1, 1), F32)
            for p in range(len(DILATIONS)):
                hit = jnp.where(bkt_ref[p] == b, dbias_s[p], 0.0)
                tot = tot + jnp.sum(jnp.sum(hit, axis=0, keepdims=True), axis=1, keepdims=True)
            row = jnp.where(lane == b, tot, row)
        dtab_ref[...] = row

    head = lambda off: pl.BlockSpec((NSLAB, n16, HEAD), lambda h: (0, 0, off + h))
    return _call(
        body, name=name, grid=(H,), side=side,
        in_specs=[pl.BlockSpec(memory_space=pltpu.SMEM), pl.BlockSpec((3, BLK, 2 * BLK), lambda h: (0, 0, 0)),
                  head(0), head(H), head(2 * H), head(0), head(0),
                  pl.BlockSpec((None, NSLAB, n16, 1), lambda h: (h, 0, 0, 0))],
        out_specs=[head(0), head(0), head(0), pl.BlockSpec((None, 1, HEAD), lambda h: (h, 0, 0))],
        out_shape=[_sds((NSLAB, n16, H * HEAD), F32)] * 3 + [_sds((H, 1, HEAD), F32)],
        scratch_shapes=[pltpu.VMEM((3, BLK, 2 * BLK), F32), pltpu.VMEM((3, BLK, 2 * BLK), F32),
                        pltpu.VMEM((NSLAB, n16, 1), F32)],
        args=(table, bkt, qkv, qkv, qkv, out, dout, lse))


def _qknorm_bwd(raw, dq, dk, dv, gains, tm, name):
    T, N = raw.shape
    C = N // 3

    def body(raw_ref, dq_ref, dk_ref, dv_ref, gains_ref, dp_ref, dg_ref):
        @pl.when(pl.program_id(0) == 0)
        def _():
            dg_ref[...] = jnp.zeros_like(dg_ref)

        for t, d_ref in enumerate((dq_ref, dk_ref)):
            gain = gains_ref[t]
            dgain = jnp.zeros((1, HEAD), F32)
            for k in range(C // HEAD):
                y = raw_ref[:, t * C + k * HEAD:t * C + (k + 1) * HEAD]
                dn = d_ref[:, k * HEAD:(k + 1) * HEAD]
                r = lax.rsqrt(jnp.mean(y * y, axis=-1, keepdims=True) + RMS_EPS)
                yhat = y * r
                gd = dn * gain
                dy = r * (gd - yhat * jnp.mean(gd * yhat, axis=-1, keepdims=True))
                dp_ref[:, t * C + k * HEAD:t * C + (k + 1) * HEAD] = dy.astype(BF16)
                dgain = dgain + jnp.sum(dn * yhat, axis=0, keepdims=True)
            dg_ref[t] += dgain
        dp_ref[:, 2 * C:] = dv_ref[...].astype(BF16)

    third = pl.BlockSpec((tm, C), lambda i: (i, 0))
    return pl.pallas_call(
        body, name=name, grid=(T // tm,),
        in_specs=[pl.BlockSpec((tm, N), lambda i: (i, 0)), third, third, third,
                  pl.BlockSpec((2, 1, HEAD), lambda i: (0, 0, 0))],
        out_specs=[pl.BlockSpec((tm, N), lambda i: (i, 0)), pl.BlockSpec((2, 1, HEAD), lambda i: (0, 0, 0))],
        out_shape=[_sds((T, N), BF16), _sds((2, 1, HEAD), F32)], compiler_params=_cparams(),
    )(raw, dq, dk, dv, gains)


def _loss_grad(y, target, tm, name):
    T, D = y.shape

    def body(y_ref, t_ref, dy_ref, loss_ref):
        @pl.when(pl.program_id(0) == 0)
        def _():
            loss_ref[...] = jnp.zeros_like(loss_ref)

        err = y_ref[...] - t_ref[...]
        dy_ref[...] = err * (1.0 / D)
        per_tok = jnp.mean(err * err, axis=-1, keepdims=True)
        tot = 0.5 * jnp.sum(per_tok, axis=0, keepdims=True)
        lane = lax.broadcasted_iota(jnp.int32, (1, HEAD), 1)
        loss_ref[...] += jnp.where(lane == 0, tot, 0.0)

    row = pl.BlockSpec((tm, D), lambda i: (i, 0))
    return pl.pallas_call(
        body, name=name, grid=(T // tm,), in_specs=[row, row],
        out_specs=[row, pl.BlockSpec((1, HEAD), lambda i: (0, 0))],
        out_shape=[_sds((T, D), F32), _sds((1, HEAD), F32)], compiler_params=_cparams(),
    )(y, target)


def _pad_lanes(v, width=HEAD):
    return jnp.pad(v, ((0, 0), (0, width - v.shape[1])))


def _local_step(x, target, small, wts, plan=None):
    grads = {}

    def hosted(host, fn, *args, **kw):
        side = plan.before(host, wts, grads) if plan is not None else None
        if side is None:
            return fn(*args, name=host, **kw)
        res, side_res = fn(*args, name=host, side=side, **kw)
        plan.after(host, side_res, wts, grads)
        return res

    T, D = x.shape
    C = D // 2
    H = C // HEAD
    n16 = T // NSLAB
    tm = min(512, T)
    tq = min(512, T)
    bn = min(1024, D)
    g1, gm, g2 = small["ffn1_norm"], small["mix_norm"], small["ffn2_norm"]
    gains_a = jnp.stack([small["q_norm_a"], small["k_norm_a"], jnp.ones_like(small["q_norm_a"])])
    gains_b = jnp.stack([small["q_norm_b"], small["k_norm_b"], jnp.ones_like(small["q_norm_b"])])
    fbias = _pad_lanes(small["forget_bias"])
    table = small["rel_bias_table"]

    x1, h1, gu1 = hosted("ffn1_fwd", _ffn_fwd, x, g1, wts["ffn1_in"], wts["ffn1_out"], tm)
    raw_a, nrm_a, h2a = _proj(x1, gm, wts["w_a"], gains_a, (True, True, False), slabs=True, tm=n16,
                              normed_dtype=F32, name="proj_a")
    raw_b, nrm_b, h2b = _proj(x1, gm, wts["w_b"], gains_b, (True, True, False), slabs=False, tm=tm,
                              normed_dtype=BF16, name="proj_b")
    f_raw, _, _ = _proj(x1, gm, wts["w_f"], gains_b[:1], (False,), slabs=False, tm=tm, normed_dtype=BF16,
                        name="proj_f")
    c = _fox_gate_fwd(f_raw, fbias, "fox_gate_fwd")
    c_heads = c[:, :H].T
    c_col, c_row = c_heads[:, :, None], c_heads[:, None, :]
    out_a, lse_a = _dil_fwd(nrm_a, table, "dil_fwd")
    out_b, lse_b = hosted("fox_fwd", _fox_fwd, nrm_b, c_col, c_row, tq)
    w_o = wts["w_o"]
    x2a = _mm(out_a, w_o[:C], nt=False, tk=C, tm=n16, a_layout="slab", out_layout="view", resid=x1, name="out_a")
    x2 = _mm(out_b, w_o[C:], nt=False, tk=C, tm=tm, a_layout="nat", out_layout="nat", resid=x2a, name="out_b")
    y, h3, gu3 = _ffn_fwd(x2, g2, wts["ffn2_in"], wts["ffn2_out"], tm, "ffn2_fwd")
    dy, loss_row = _loss_grad(y, target, tm, "loss_grad")

    def ffn_backward(tag, xin, g, h, gu, win, wout, dres):
        nc, tf = wout.shape[0], wout.shape[1]
        dh, dgu, acth = hosted(tag + "_bwd", _ffn_bwd, dres, gu, win, wout, tm)
        dxin, dg = _rms_bwd(xin, g, dh, dres, tm, tag + "_rms_bwd")
        grads[tag + "_norm"] = dg
        grads[tag + "_w_in_t"] = hosted(tag + "_dwin", _mm_tn, dgu.reshape(2 * nc, T, tf), h, bm=tf, bn=bn,
                                        bt=min(1024, T))
        grads[tag + "_w_out"] = hosted(tag + "_dwout", _mm_tn, acth, dres, bm=tf, bn=bn, bt=min(1024, T))
        return dxin

    dx2 = ffn_backward("ffn2", x2, g2, h3, gu3, wts["ffn2_in"], wts["ffn2_out"], dy)

    dmix_a = _mm(dx2, w_o[:C], nt=True, tk=D, tm=n16, a_layout="view", out_layout="slab", name="dmix_a")
    dmix_b = _mm(dx2, w_o[C:], nt=True, tk=D, tm=tm, a_layout="nat", out_layout="nat", name="dmix_b")
    dwo_a = _mm_tn(out_a.reshape(1, T, C), dx2.reshape(n16, NSLAB * D), bm=C, bn=bn, bt=n16, b_slabs=True, name="dwo_a")
    dwo_b = _mm_tn(out_b.reshape(1, T, C), dx2, bm=C, bn=bn, bt=tm, name="dwo_b")
    grads["w_out"] = jnp.concatenate([dwo_a[0], dwo_b[0]], axis=0)

    dqa, dka, dva, dtab = hosted("dil_bwd", _dil_bwd, nrm_a, table, out_a, dmix_a, lse_a)
    dqb, dkb, dvb, dck, dcq = hosted("fox_bwd", _fox_bwd, nrm_b, c_col, c_row, out_b, dmix_b, lse_b, tq)
    grads["rel_bias_table"] = dtab[:, 0, :NUM_BUCKETS].T
    dc = _pad_lanes((dck[:, 0, :] + dcq[:, :, 0]).T)
    df, dfb = _fox_gate_bwd(f_raw, fbias, dc, "fox_gate_bwd")
    grads["forget_bias"] = dfb[:, :H]

    flat = lambda a: a.reshape(T, a.shape[-1])
    dproj_a, dgain_a = _qknorm_bwd(flat(raw_a), flat(dqa), flat(dka), flat(dva), gains_a[:2], min(256, T), "qknorm_bwd_a")
    dproj_b, dgain_b = _qknorm_bwd(raw_b, dqb, dkb, dvb, gains_b[:2], min(256, T), "qknorm_bwd_b")
    grads["q_norm_a"], grads["k_norm_a"] = dgain_a[0], dgain_a[1]
    grads["q_norm_b"], grads["k_norm_b"] = dgain_b[0], dgain_b[1]
    dproj_a = dproj_a.reshape(NSLAB, n16, 3 * C)

    dh2 = _mm(dproj_a, wts["w_a"], nt=True, tk=C, tm=n16, a_layout="slab", out_layout="view", name="dh2_a")
    dh2 = _mm(dproj_b, wts["w_b"], nt=True, tk=C, tm=tm, a_layout="nat", out_layout="nat", resid=dh2, name="dh2_b")
    dh2 = _mm(df, wts["w_f"], nt=True, tk=HEAD, tm=tm, a_layout="nat", out_layout="nat", resid=dh2, name="dh2_f")
    dx1, grads["mix_norm"] = _rms_bwd(x1, gm, dh2, dx2, tm, "mix_rms_bwd")
    bt = min(512, T)
    dw_a = _mm_tn(flat(h2a)[None], flat(dproj_a), bm=bn, bn=min(1024, C), bt=bt, name="dw_a")[0]
    dw_b = _mm_tn(h2b[None], dproj_b, bm=bn, bn=min(1024, C), bt=bt, name="dw_b")[0]
    dw_f = _mm_tn(h2b[None], df, bm=bn, bn=HEAD, bt=bt, name="dw_f")[0]
    grads["w_in"] = jnp.concatenate([dw_a, dw_b, dw_f[:, :H]], axis=1)

    grad_x = ffn_backward("ffn1", x, g1, h1, gu1, wts["ffn1_in"], wts["ffn1_out"], dx1)
    return loss_row, grad_x, grads


def _place():
    x, y, c = lax.axis_index("x"), lax.axis_index("y"), lax.axis_index("c")
    other_chips = [(1 - x, y), (x, 1 - y), (1 - x, 1 - y)]
    return x, y, c, other_chips


def _run_side(side, name):
    def body(*refs):
        si, so = len(side.ins), len(side.outs)
        side.start(refs[:si], refs[si:si + so], refs[si + so:])
        side.finish(refs[:si], refs[si:si + so], refs[si + so:])

    return pl.pallas_call(body, name=name, in_specs=[ANY] * len(side.ins), out_specs=[ANY] * len(side.outs),
                          out_shape=side.outs, scratch_shapes=side.sems)(*side.ins)


def _all_gather(shards):
    n = len(shards)

    def plan(ins, outs, sems):
        send_sems, recv_sems, local_sems = sems
        x, y, c, chips = _place()
        me, sibling = (x, y, c), (x, y, 1 - c)

        def copy(a, k, block, to, src=None):
            px, py, pc = block
            dst = outs[a].at[4 * px + 2 * py + pc]
            return pltpu.make_async_remote_copy(
                src_ref=dst if src is None else src, dst_ref=dst, send_sem=send_sems.at[7 * a + k],
                recv_sem=recv_sems.at[7 * a + k], device_id=to, device_id_type=MESH)

        mine = [pltpu.make_async_copy(ins[a], outs[a].at[4 * x + 2 * y + c], local_sems.at[a]) for a in range(n)]
        first = []
        for a in range(n):
            first.append(copy(a, 0, me, sibling, src=ins[a]))
            first += [copy(a, 1 + j, me, (*chip, c), src=ins[a]) for j, chip in enumerate(chips)]
        return copy, mine, first, me, sibling, c, chips

    def start(ins, outs, sems):
        _, mine, first, *_ = plan(ins, outs, sems)
        for cp in mine + first:
            cp.start()

    def finish(ins, outs, sems):
        copy, mine, first, me, sibling, c, chips = plan(ins, outs, sems)
        passed = []
        for a in range(n):
            for j, chip in enumerate(chips):
                copy(a, 1 + j, (*chip, c), me).wait_recv()
                fwd = copy(a, 4 + j, (*chip, c), sibling)
                fwd.start()
                passed.append(fwd)
        for a in range(n):
            copy(a, 0, sibling, me).wait_recv()
            for j, chip in enumerate(chips):
                copy(a, 4 + j, (*chip, 1 - c), me).wait_recv()
        for cp in first + passed:
            cp.wait_send()
        for cp in mine:
            cp.wait()

    return _Side(shards, [_sds((N_DEV,) + s.shape, s.dtype) for s in shards],
                 [pltpu.SemaphoreType.DMA((7 * n,)), pltpu.SemaphoreType.DMA((7 * n,)), pltpu.SemaphoreType.DMA((n,))],
                 start, finish)


def _exchange_in_chip(gs):
    n = len(gs)

    def copies(ins, outs, sems):
        x, y, c, _ = _place()
        return [pltpu.make_async_remote_copy(
            src_ref=ins[a].at[2 * q + 1 - c], dst_ref=outs[a].at[q], send_sem=sems[0].at[4 * a + q],
            recv_sem=sems[1].at[4 * a + q], device_id=(x, y, 1 - c), device_id_type=MESH)
            for a in range(n) for q in range(4)]

    def start(ins, outs, sems):
        for cp in copies(ins, outs, sems):
            cp.start()

    def finish(ins, outs, sems):
        for cp in copies(ins, outs, sems):
            cp.wait()

    return _Side(gs, [_sds((4,) + g.shape[1:], g.dtype) for g in gs],
                 [pltpu.SemaphoreType.DMA((4 * n,)), pltpu.SemaphoreType.DMA((4 * n,))], start, finish)


def _exchange_between_chips(ps):
    n = len(ps)

    def copies(ins, outs, sems):
        x, y, c, chips = _place()
        return [pltpu.make_async_remote_copy(
            src_ref=ins[a].at[2 * cx + cy], dst_ref=outs[a].at[j], send_sem=sems[0].at[3 * a + j],
            recv_sem=sems[1].at[3 * a + j], device_id=(cx, cy, c), device_id_type=MESH)
            for a in range(n) for j, (cx, cy) in enumerate(chips)]

    def start(ins, outs, sems):
        for cp in copies(ins, outs, sems):
            cp.start()

    def finish(ins, outs, sems):
        for cp in copies(ins, outs, sems):
            cp.wait()

    return _Side(ps, [_sds((3,) + p.shape[1:], p.dtype) for p in ps],
                 [pltpu.SemaphoreType.DMA((3 * n,)), pltpu.SemaphoreType.DMA((3 * n,))], start, finish)


def _all_reduce_small(v, name):
    R = v.shape[0]

    def body(v_ref, sum_ref, all_ref, send_sems, recv_sems):
        x, y, c, _ = _place()
        k = 4 * x + 2 * y + c
        all_ref[k] = v_ref[...]
        copies = []
        for rel in range(1, N_DEV):
            fx, fy, fc = (rel >> 2) & 1, (rel >> 1) & 1, rel & 1
            peer = (1 - x if fx else x, 1 - y if fy else y, 1 - c if fc else c)
            copies.append(pltpu.make_async_remote_copy(
                src_ref=v_ref, dst_ref=all_ref.at[k], send_sem=send_sems.at[rel - 1], recv_sem=recv_sems.at[rel - 1],
                device_id=peer, device_id_type=MESH))
        for cp in copies:
            cp.start()
        for rel in range(1, N_DEV):
            fx, fy, fc = (rel >> 2) & 1, (rel >> 1) & 1, rel & 1
            src = 4 * (1 - x if fx else x) + 2 * (1 - y if fy else y) + (1 - c if fc else c)
            pltpu.make_async_remote_copy(
                src_ref=v_ref, dst_ref=all_ref.at[src], send_sem=send_sems.at[rel - 1], recv_sem=recv_sems.at[rel - 1],
                device_id=(x, y, c), device_id_type=MESH).wait_recv()
        for cp in copies:
            cp.wait_send()
        tot = all_ref[0]
        for d in range(1, N_DEV):
            tot = tot + all_ref[d]
        sum_ref[...] = tot

    vm = pl.BlockSpec(memory_space=pltpu.VMEM)
    return pl.pallas_call(
        body, name=name, in_specs=[vm], out_specs=[vm, vm],
        out_shape=[_sds((R, HEAD), F32), _sds((N_DEV, R, HEAD), F32)],
        scratch_shapes=[pltpu.SemaphoreType.DMA((N_DEV - 1,)), pltpu.SemaphoreType.DMA((N_DEV - 1,))],
    )(v)[0]


def _row_tile(rows):
    for cand in (688, 512, 256):
        if rows % cand == 0:
            return cand
    return rows


def _chip_sum(g, r1, core, name):
    _, R, Cc = g.shape
    tr = _row_tile(R)

    def body(core_ref, g_ref, r_ref, p_ref):
        p_ref[...] = (g_ref[...] + r_ref[...]).astype(BF16)

    blk = lambda f: pl.BlockSpec((None, tr, Cc), f)
    return pl.pallas_call(
        body, name=name,
        grid_spec=pltpu.PrefetchScalarGridSpec(
            num_scalar_prefetch=1, grid=(4, R // tr),
            in_specs=[blk(lambda q, i, core: (2 * q + core[0], i, 0)), blk(lambda q, i, core: (q, i, 0))],
            out_specs=blk(lambda q, i, core: (q, i, 0))),
        out_shape=_sds((4, R, Cc), BF16), compiler_params=_cparams(),
    )(core, g, r1)


def _final_sum(g, r1, r2, where, name):
    _, R, Cc = g.shape
    tr = _row_tile(R)

    def body(where_ref, g_ref, r1_ref, r2_ref, o_ref):
        o_ref[...] = ((g_ref[...] + r1_ref[...]) + r2_ref[0].astype(F32)) + (r2_ref[1].astype(F32) + r2_ref[2].astype(F32))

    return pl.pallas_call(
        body, name=name,
        grid_spec=pltpu.PrefetchScalarGridSpec(
            num_scalar_prefetch=1, grid=(R // tr,),
            in_specs=[pl.BlockSpec((None, tr, Cc), lambda i, w: (w[0], i, 0)),
                      pl.BlockSpec((None, tr, Cc), lambda i, w: (w[1], i, 0)),
                      pl.BlockSpec((3, tr, Cc), lambda i, w: (0, i, 0))],
            out_specs=pl.BlockSpec((tr, Cc), lambda i, w: (i, 0))),
        out_shape=_sds((R, Cc), F32), compiler_params=_cparams(),
    )(where, g, r1, r2)


def _adamw(w, g, m, v, name):
    R, Cc = w.shape
    tr = _row_tile(R)

    def body(w_ref, g_ref, m_ref, v_ref, d_ref, nm_ref, nv_ref):
        gv = g_ref[...]
        nm = B1 * m_ref[...] + (1.0 - B1) * gv
        nv = B2 * v_ref[...] + (1.0 - B2) * jnp.square(gv)
        m_hat = nm / (1.0 - B1 ** STEP)
        v_hat = nv / (1.0 - B2 ** STEP)
        d_ref[...] = -LR * (m_hat / (jnp.sqrt(v_hat) + EPS) + WD * w_ref[...])
        nm_ref[...] = nm
        nv_ref[...] = nv

    blk = pl.BlockSpec((tr, Cc), lambda i: (i, 0))
    return pl.pallas_call(
        body, name=name, grid=(R // tr,), in_specs=[blk] * 4, out_specs=[blk] * 3,
        out_shape=[_sds((R, Cc), F32)] * 3, compiler_params=_cparams(),
    )(w, g, m, v)


SMALL = ("ffn1_norm", "mix_norm", "ffn2_norm", "q_norm_a", "k_norm_a", "q_norm_b", "k_norm_b", "forget_bias",
         "rel_bias_table")
LARGE = ("ffn1_w_in", "ffn1_w_out", "w_in", "w_out", "ffn2_w_in", "ffn2_w_out")
ORDER = ("ffn1_norm", "ffn1_w_in", "ffn1_w_out", "mix_norm", "w_in", "q_norm_a", "k_norm_a", "q_norm_b", "k_norm_b",
         "forget_bias", "rel_bias_table", "w_out", "ffn2_norm", "ffn2_w_in", "ffn2_w_out")


def _pack_small(vals):
    rows = []
    for name in SMALL:
        flat = vals[name].reshape(-1)
        pad = (-flat.shape[0]) % HEAD
        rows.append(jnp.pad(flat, (0, pad)).reshape(-1, HEAD))
    return jnp.concatenate(rows, axis=0)


def _unpack_small(packed, like):
    out, r = {}, 0
    for name in SMALL:
        size = like[name].size
        nrow = -(-size // HEAD)
        out[name] = packed[r:r + nrow].reshape(-1)[:size].reshape(like[name].shape)
        r += nrow
    return out


def kernel(x, ffn1_norm, ffn1_w_in, ffn1_w_out, mix_norm, w_in, q_norm_a, k_norm_a, q_norm_b, k_norm_b, forget_bias, rel_bias_table, w_out, ffn2_norm, ffn2_w_in, ffn2_w_out, loss_target, m_ffn1_norm, m_ffn1_w_in, m_ffn1_w_out, m_mix_norm, m_w_in, m_q_norm_a, m_k_norm_a, m_q_norm_b, m_k_norm_b, m_forget_bias, m_rel_bias_table, m_w_out, m_ffn2_norm, m_ffn2_w_in, m_ffn2_w_out, v_ffn1_norm, v_ffn1_w_in, v_ffn1_w_out, v_mix_norm, v_w_in, v_q_norm_a, v_k_norm_a, v_q_norm_b, v_k_norm_b, v_forget_bias, v_rel_bias_table, v_w_out, v_ffn2_norm, v_ffn2_w_in, v_ffn2_w_out):
    w = dict(ffn1_norm=ffn1_norm, ffn1_w_in=ffn1_w_in, ffn1_w_out=ffn1_w_out, mix_norm=mix_norm, w_in=w_in,
             q_norm_a=q_norm_a, k_norm_a=k_norm_a, q_norm_b=q_norm_b, k_norm_b=k_norm_b, forget_bias=forget_bias,
             rel_bias_table=rel_bias_table, w_out=w_out, ffn2_norm=ffn2_norm, ffn2_w_in=ffn2_w_in, ffn2_w_out=ffn2_w_out)
    m = dict(ffn1_norm=m_ffn1_norm, ffn1_w_in=m_ffn1_w_in, ffn1_w_out=m_ffn1_w_out, mix_norm=m_mix_norm, w_in=m_w_in,
             q_norm_a=m_q_norm_a, k_norm_a=m_k_norm_a, q_norm_b=m_q_norm_b, k_norm_b=m_k_norm_b,
             forget_bias=m_forget_bias, rel_bias_table=m_rel_bias_table, w_out=m_w_out, ffn2_norm=m_ffn2_norm,
             ffn2_w_in=m_ffn2_w_in, ffn2_w_out=m_ffn2_w_out)
    v = dict(ffn1_norm=v_ffn1_norm, ffn1_w_in=v_ffn1_w_in, ffn1_w_out=v_ffn1_w_out, mix_norm=v_mix_norm, w_in=v_w_in,
             q_norm_a=v_q_norm_a, k_norm_a=v_k_norm_a, q_norm_b=v_q_norm_b, k_norm_b=v_k_norm_b,
             forget_bias=v_forget_bias, rel_bias_table=v_rel_bias_table, w_out=v_w_out, ffn2_norm=v_ffn2_norm,
             ffn2_w_in=v_ffn2_w_in, ffn2_w_out=v_ffn2_w_out)
    T, D = x.shape[1], x.shape[2]
    C = D // 2
    H = C // HEAD
    ff_shard = ffn1_w_out.shape[1]

    f1i, f1o = _run_side(_all_gather([ffn1_w_in[0].T.astype(BF16), ffn1_w_out[0].astype(BF16)]), "gather_ffn1")
    wts = dict(ffn1_in=f1i.reshape(2, N_DEV, ff_shard, D), ffn1_out=f1o)
    xi, yi, ci = lax.axis_index("x"), lax.axis_index("y"), lax.axis_index("c")
    core = jnp.reshape(ci, (1,)).astype(jnp.int32)
    where = jnp.stack([4 * xi + 2 * yi + ci, 2 * xi + yi]).astype(jnp.int32)
    gs, r1, ps, r2 = {}, {}, {}, {}

    def by_destination(name, grads):
        if name == "w_in":
            gs[name] = jnp.moveaxis(grads["w_in"].reshape(D, N_DEV, -1), 1, 0)
        elif name == "w_out":
            gs[name] = grads["w_out"].reshape(N_DEV, -1, D)
        elif name.endswith("w_in"):
            gs[name] = grads[name + "_t"].reshape(N_DEV, 2 * ff_shard, D)
        else:
            gs[name] = grads[name]
        return gs[name]

    def chip_sums(names):
        for name in names:
            ps[name] = _chip_sum(gs[name], r1[name], core, "chip_sum_" + name)
        return [ps[name] for name in names]

    class Plan:
        carried = {"ffn1_fwd": ("gather", ("w_in", "w_out")), "fox_fwd": ("gather", ("ffn2_w_in", "ffn2_w_out")),
                   "dil_bwd": ("in_chip", ("ffn2_w_in", "ffn2_w_out")), "fox_bwd": ("between", ("ffn2_w_in", "ffn2_w_out")),
                   "ffn1_bwd": ("in_chip", ("w_in", "w_out")), "ffn1_dwin": ("between", ("w_in", "w_out")),
                   "ffn1_dwout": ("in_chip", ("ffn1_w_in",))}

        def before(self, host, wts, grads):
            if host not in self.carried:
                return None
            kind, names = self.carried[host]
            if kind == "gather":
                return _all_gather([(w[n][0].T if n == "ffn2_w_in" else w[n][0]).astype(BF16) for n in names])
            if kind == "in_chip":
                return _exchange_in_chip([by_destination(n, grads) for n in names])
            return _exchange_between_chips(chip_sums(names))

        def after(self, host, res, wts, grads):
            kind, names = self.carried[host]
            if host == "ffn1_fwd":
                wi = jnp.moveaxis(res[0], 0, 1).reshape(D, -1)
                wts.update(w_a=wi[:, :3 * C], w_b=wi[:, 3 * C:6 * C], w_f=_pad_lanes(wi[:, 6 * C:]),
                           w_o=res[1].reshape(2 * C, D))
            elif host == "fox_fwd":
                wts.update(ffn2_in=res[0].reshape(2, N_DEV, ff_shard, D), ffn2_out=res[1])
            else:
                (r1 if kind == "in_chip" else r2).update(zip(names, res))

    small = {name: w[name] for name in SMALL}
    loss_row, grad_x, grads = _local_step(x[0], loss_target[0], small, wts, Plan())

    r1["ffn1_w_out"], = _run_side(_exchange_in_chip([by_destination("ffn1_w_out", grads)]), "reduce_in_chip_tail")
    tail = ("ffn1_w_in", "ffn1_w_out")
    r2.update(zip(tail, _run_side(_exchange_between_chips(chip_sums(tail)), "reduce_between_chips_tail")))
    g_large = {name: _final_sum(gs[name], r1[name], r2[name], where, "final_sum_" + name) for name in LARGE}
    g_large["ffn1_w_in"] = g_large["ffn1_w_in"].T
    g_large["ffn2_w_in"] = g_large["ffn2_w_in"].T

    packed = _pack_small(grads)
    nsmall = packed.shape[0]
    packed = jnp.concatenate([packed, loss_row, jnp.zeros(((-nsmall - 1) % 8, HEAD), F32)], axis=0)
    reduced = _all_reduce_small(packed, "reduce_small")
    loss = reduced[nsmall, 0]
    g_small = _unpack_small(reduced[:nsmall], small)

    grad, delta, new_m, new_v = dict(g_small), {}, {}, {}
    for name in LARGE:
        grad[name] = g_large[name][None]
        d, nm, nv = _adamw(w[name][0], g_large[name], m[name][0], v[name][0], "adamw_" + name)
        delta[name], new_m[name], new_v[name] = d[None], nm[None], nv[None]
    d, nm, nv = _adamw(_pack_small(w), reduced[:nsmall], _pack_small(m), _pack_small(v), "adamw_small")
    delta.update(_unpack_small(d, small))
    new_m.update(_unpack_small(nm, small))
    new_v.update(_unpack_small(nv, small))
    return (loss, grad_x[None], *[grad[n] for n in ORDER], *[delta[n] for n in ORDER],
            *[new_m[n] for n in ORDER], *[new_v[n] for n in ORDER])
```

```python
import functools
import math

import numpy as np
import jax
import jax.numpy as jnp
from jax import lax
from jax.experimental import pallas as pl
from jax.experimental.pallas import tpu as pltpu

F32, BF16 = jnp.float32, jnp.bfloat16
HEAD = 128
NSLAB = 16
BLK = 128
DILATIONS = (1, 4, 16)
NUM_BUCKETS, MAX_DISTANCE = 32, 2048
RMS_EPS = 1e-6
NEG = -1e30
SCALE = HEAD ** -0.5
LR, B1, B2, EPS, WD, STEP = 0.001, 0.9, 0.999, 1e-08, 0.01, 10
N_DEV = 8
VMEM_LIMIT_BYTES = 56 << 20
MESH = pl.DeviceIdType.MESH


def _cparams(**kw):
    return pltpu.CompilerParams(vmem_limit_bytes=VMEM_LIMIT_BYTES, **kw)


def _nn(a, b):
    return jnp.dot(a, b, preferred_element_type=F32)


def _nt(a, b):
    return lax.dot_general(a, b, (((1,), (1,)), ((), ())), preferred_element_type=F32)


def _tn(a, b):
    return lax.dot_general(a, b, (((0,), (0,)), ((), ())), preferred_element_type=F32)


def _sds(shape, dtype):
    return jax.ShapeDtypeStruct(shape, dtype)


ANY = pl.BlockSpec(memory_space=pl.ANY)


class _Side:
    def __init__(self, ins, outs, sems, start, finish):
        self.ins, self.outs, self.sems, self.start, self.finish = list(ins), list(outs), list(sems), start, finish


def _call(body, *, name, grid, in_specs, out_specs, out_shape, args, scratch_shapes=(), side=None):
    in_specs, out_specs, out_shape = list(in_specs), list(out_specs), list(out_shape)
    scratch_shapes = list(scratch_shapes)
    if side is None:
        return pl.pallas_call(body, name=name, grid=grid, in_specs=in_specs, out_specs=out_specs, out_shape=out_shape,
                              scratch_shapes=scratch_shapes, compiler_params=_cparams())(*args)
    ni, no, ns = len(args), len(out_shape), len(scratch_shapes)
    si, so = len(side.ins), len(side.outs)

    def fused(*refs):
        h_in, s_in = refs[:ni], refs[ni:ni + si]
        h_out, s_out = refs[ni + si:ni + si + no], refs[ni + si + no:ni + si + no + so]
        h_scr, s_sem = refs[ni + si + no + so:ni + si + no + so + ns], refs[ni + si + no + so + ns:]
        ids = [pl.program_id(k) for k in range(len(grid))]
        first = functools.reduce(jnp.logical_and, [i == 0 for i in ids])
        last = functools.reduce(jnp.logical_and, [i == n - 1 for i, n in zip(ids, grid)])

        @pl.when(first)
        def _():
            side.start(s_in, s_out, s_sem)

        body(*h_in, *h_out, *h_scr)

        @pl.when(last)
        def _():
            side.finish(s_in, s_out, s_sem)

    res = pl.pallas_call(
        fused, name=name, grid=grid, in_specs=in_specs + [ANY] * si, out_specs=out_specs + [ANY] * so,
        out_shape=out_shape + side.outs, scratch_shapes=scratch_shapes + side.sems, compiler_params=_cparams(),
    )(*args, *side.ins)
    return list(res[:no]), list(res[no:])


def _ffn_fwd(x, g, win, wout, tm, name, side=None):
    T, D = x.shape
    nc, tf = wout.shape[0], wout.shape[1]

    def body(x_ref, g_ref, win_ref, wout_ref, y_ref, h_ref, gu_ref):
        j = pl.program_id(1)

        @pl.when(j == 0)
        def _():
            xv = x_ref[...]
            r = lax.rsqrt(jnp.mean(xv * xv, axis=-1, keepdims=True) + RMS_EPS)
            h_ref[...] = (xv * r * g_ref[...]).astype(BF16)
            y_ref[...] = jnp.zeros_like(y_ref)

        hb = h_ref[...]
        gt = _nt(hb, win_ref[0])
        up = _nt(hb, win_ref[1])
        gu_ref[0] = gt.astype(BF16)
        gu_ref[1] = up.astype(BF16)
        act = (gt * jax.nn.sigmoid(gt) * up).astype(BF16)
        y_ref[...] += _nn(act, wout_ref[...])

        @pl.when(j == nc - 1)
        def _():
            y_ref[...] = x_ref[...] + 0.5 * y_ref[...]

    return _call(
        body, name=name, grid=(T // tm, nc), side=side,
        in_specs=[pl.BlockSpec((tm, D), lambda i, j: (i, 0)),
                  pl.BlockSpec((1, D), lambda i, j: (0, 0)),
                  pl.BlockSpec((2, None, tf, D), lambda i, j: (0, j, 0, 0)),
                  pl.BlockSpec((None, tf, D), lambda i, j: (j, 0, 0))],
        out_specs=[pl.BlockSpec((tm, D), lambda i, j: (i, 0)),
                   pl.BlockSpec((tm, D), lambda i, j: (i, 0)),
                   pl.BlockSpec((2, None, tm, tf), lambda i, j: (0, j, i, 0))],
        out_shape=[_sds((T, D), F32), _sds((T, D), BF16), _sds((2, nc, T, tf), BF16)],
        args=(x, g, win, wout))


def _ffn_bwd(dy, gu, win, wout, tm, name, side=None):
    T, D = dy.shape
    nc, tf = wout.shape[0], wout.shape[1]

    def body(dy_ref, gu_ref, win_ref, wout_ref, dh_ref, dgu_ref, act_ref):
        j = pl.program_id(1)

        @pl.when(j == 0)
        def _():
            dh_ref[...] = jnp.zeros_like(dh_ref)

        dyb = (0.5 * dy_ref[...]).astype(BF16)
        dact = _nt(dyb, wout_ref[...])
        gt = gu_ref[0].astype(F32)
        up = gu_ref[1].astype(F32)
        s = jax.nn.sigmoid(gt)
        silu = gt * s
        dgb = (dact * up * (s * (1.0 + gt * (1.0 - s)))).astype(BF16)
        dub = (dact * silu).astype(BF16)
        dgu_ref[0] = dgb
        dgu_ref[1] = dub
        act_ref[...] = (0.5 * silu * up).astype(BF16)
        dh_ref[...] += _nn(dgb, win_ref[0]) + _nn(dub, win_ref[1])

    return _call(
        body, name=name, grid=(T // tm, nc), side=side,
        in_specs=[pl.BlockSpec((tm, D), lambda i, j: (i, 0)),
                  pl.BlockSpec((2, None, tm, tf), lambda i, j: (0, j, i, 0)),
                  pl.BlockSpec((2, None, tf, D), lambda i, j: (0, j, 0, 0)),
                  pl.BlockSpec((None, tf, D), lambda i, j: (j, 0, 0))],
        out_specs=[pl.BlockSpec((tm, D), lambda i, j: (i, 0)),
                   pl.BlockSpec((2, None, tm, tf), lambda i, j: (0, j, i, 0)),
                   pl.BlockSpec((None, tm, tf), lambda i, j: (j, i, 0))],
        out_shape=[_sds((T, D), F32), _sds((2, nc, T, tf), BF16), _sds((nc, T, tf), BF16)],
        args=(dy, gu, win, wout))


def _rms_bwd(x, g, dh, dres, tm, name):
    T, D = x.shape

    def body(x_ref, g_ref, dh_ref, dres_ref, dx_ref, dg_ref):
        @pl.when(pl.program_id(0) == 0)
        def _():
            dg_ref[...] = jnp.zeros_like(dg_ref)

        xv = x_ref[...]
        r = lax.rsqrt(jnp.mean(xv * xv, axis=-1, keepdims=True) + RMS_EPS)
        xhat = xv * r
        dh = dh_ref[...]
        gd = dh * g_ref[...]
        dx_ref[...] = dres_ref[...] + r * (gd - xhat * jnp.mean(gd * xhat, axis=-1, keepdims=True))
        dg_ref[...] += jnp.sum(dh * xhat, axis=0, keepdims=True)

    row = pl.BlockSpec((tm, D), lambda i: (i, 0))
    one = pl.BlockSpec((1, D), lambda i: (0, 0))
    return pl.pallas_call(
        body, name=name, grid=(T // tm,), in_specs=[row, one, row, row], out_specs=[row, one],
        out_shape=[_sds((T, D), F32), _sds((1, D), F32)], compiler_params=_cparams(),
    )(x, g, dh, dres)


def _mm_tn(a, b, *, bm, bn, bt, name, b_slabs=False, side=None, rows=None, m_off=0, into=None):
    nz, T, M = a.shape
    if b_slabs:
        N = b.shape[1] // NSLAB
        assert bt == T // NSLAB
        b_spec = pl.BlockSpec((bt, bn), lambda z, m, n, t: (0, t * (N // bn) + n))
    else:
        N = b.shape[1]
        b_spec = pl.BlockSpec((bt, bn), lambda z, m, n, t: (t, n))
    assert M % bm == 0 and N % bn == 0 and T % bt == 0, (M, bm, N, bn, T, bt)

    def body(a_ref, b_ref, *rest):
        c_ref = rest[-1]

        @pl.when(pl.program_id(3) == 0)
        def _():
            c_ref[...] = jnp.zeros_like(c_ref)

        c_ref[...] += _tn(a_ref[...].astype(BF16), b_ref[...].astype(BF16))

    grid = (nz, M // bm, N // bn, T // bt)
    in_specs = [pl.BlockSpec((None, bt, bm), lambda z, m, n, t: (z, t, m)), b_spec]
    out_spec = pl.BlockSpec((None, bm, bn), lambda z, m, n, t: (z, m + m_off, n))
    out_shape = _sds((nz, M if rows is None else rows, N), F32)
    if into is not None:
        assert side is None and into.shape == out_shape.shape
        return pl.pallas_call(body, name=name, grid=grid, in_specs=in_specs + [ANY], out_specs=out_spec,
                              out_shape=out_shape, input_output_aliases={2: 0}, compiler_params=_cparams())(a, b, into)
    res = _call(body, name=name, grid=grid, side=side, in_specs=in_specs, out_specs=[out_spec],
                out_shape=[out_shape], args=(a, b))
    return res[0] if side is None else (res[0][0], res[1])


def _proj(x, g, wt, gains, modes, *, tn, w_off, slabs, tm, normed_dtype, name):
    T, D = x.shape
    ntile = len(modes)
    N = ntile * tn
    n16 = T // NSLAB
    if slabs:
        assert tm == n16
        x_in = x.reshape(n16, NSLAB * D)
        x_spec = pl.BlockSpec((tm, D), lambda i, n: (0, i))
        oshape = lambda c: (NSLAB, n16, c)
        ospec = lambda bc, cm: pl.BlockSpec((None, tm, bc), lambda i, n: (i, 0, cm(n)))
    else:
        x_in = x
        x_spec = pl.BlockSpec((tm, D), lambda i, n: (i, 0))
        oshape = lambda c: (T, c)
        ospec = lambda bc, cm: pl.BlockSpec((tm, bc), lambda i, n: (i, cm(n)))

    def body(x_ref, g_ref, w_ref, gains_ref, raw_ref, nrm_ref, h_ref):
        n = pl.program_id(1)

        @pl.when(n == 0)
        def _():
            xv = x_ref[...]
            r = lax.rsqrt(jnp.mean(xv * xv, axis=-1, keepdims=True) + RMS_EPS)
            h_ref[...] = (xv * r * g_ref[...]).astype(BF16)

        y = _nt(h_ref[...], w_ref[...])
        raw_ref[...] = y
        for t, mode in enumerate(modes):
            @pl.when(n == t)
            def _(t=t, mode=mode):
                if not mode:
                    nrm_ref[...] = y.astype(nrm_ref.dtype)
                    return
                gain = gains_ref[t]
                for k in range(tn // HEAD):
                    yk = y[:, k * HEAD:(k + 1) * HEAD]
                    r = lax.rsqrt(jnp.mean(yk * yk, axis=-1, keepdims=True) + RMS_EPS)
                    nrm_ref[:, k * HEAD:(k + 1) * HEAD] = (yk * r * gain).astype(nrm_ref.dtype)

    return pl.pallas_call(
        body, name=name, grid=(T // tm, ntile),
        in_specs=[x_spec, pl.BlockSpec((1, D), lambda i, n: (0, 0)),
                  pl.BlockSpec((tn, D), lambda i, n: (n + w_off, 0)),
                  pl.BlockSpec((ntile, 1, HEAD), lambda i, n: (0, 0, 0))],
        out_specs=[ospec(tn, lambda n: n), ospec(tn, lambda n: n), ospec(D, lambda n: 0)],
        out_shape=[_sds(oshape(N), F32), _sds(oshape(N), normed_dtype), _sds(oshape(D), BF16)],
        compiler_params=_cparams(),
    )(x_in, g, wt, gains)


def _mm(a, w, *, nt, tk, tm, a_layout, out_layout, resid=None, name, w_off=0, n_out=None):
    if a_layout == "slab":
        T, K = a.shape[0] * a.shape[1], a.shape[2]
    else:
        T, K = a.shape
    N = (w.shape[0] if nt else w.shape[1]) if n_out is None else n_out
    n16 = T // NSLAB
    nk = K // tk

    def spec(layout, C, bc, colmap):
        if layout == "nat":
            return pl.BlockSpec((tm, bc), lambda i, k: (i, colmap(k)))
        assert tm == n16
        if layout == "slab":
            return pl.BlockSpec((None, tm, bc), lambda i, k: (i, 0, colmap(k)))
        assert bc == C
        return pl.BlockSpec((tm, C), lambda i, k: (0, i))

    a_in = a.reshape(n16, NSLAB * K) if a_layout == "view" else a
    w_spec = (pl.BlockSpec((N, tk), lambda i, k: (w_off, k)) if nt
              else pl.BlockSpec((tk, N), lambda i, k: (k + w_off, 0)))
    o_spec = spec(out_layout, N, N, lambda k: 0)
    oshape = {"nat": (T, N), "slab": (NSLAB, n16, N), "view": (n16, NSLAB * N)}[out_layout]
    has_resid = resid is not None

    def body(*refs):
        a_ref, w_ref = refs[0], refs[1]
        o_ref = refs[-1]
        k = pl.program_id(1)

        @pl.when(k == 0)
        def _():
            o_ref[...] = refs[2][...] if has_resid else jnp.zeros_like(o_ref)

        ab = a_ref[...].astype(BF16)
        o_ref[...] += _nt(ab, w_ref[...]) if nt else _nn(ab, w_ref[...])

    ins = [a_in, w]
    in_specs = [spec(a_layout, K, tk, lambda k: k), w_spec]
    if has_resid:
        ins.append(resid.reshape(n16, NSLAB * N) if out_layout == "view" else resid)
        in_specs.append(o_spec)
    out = pl.pallas_call(
        body, name=name, grid=(T // tm, nk), in_specs=in_specs, out_specs=o_spec,
        out_shape=_sds(oshape, F32), compiler_params=_cparams(),
    )(*ins)
    return out.reshape(T, N) if out_layout == "view" else out


def _log_sigmoid(z):
    return jnp.minimum(z, 0.0) - jnp.log(1.0 + jnp.exp(-jnp.abs(z)))


def _fox_gate_fwd(f_raw, fbias, name):
    T = f_raw.shape[0]
    cb = 256

    def body(f_ref, b_ref, c_ref):
        row = lax.broadcasted_iota(jnp.int32, (cb, cb), 0)
        col = lax.broadcasted_iota(jnp.int32, (cb, cb), 1)
        tri = (col <= row).astype(F32)
        carry = jnp.zeros((1, HEAD), F32)
        for i in range(T // cb):
            lf = _log_sigmoid(f_ref[i * cb:(i + 1) * cb, :] + b_ref[...])
            c = jnp.dot(tri, lf, preferred_element_type=F32, precision=lax.Precision.HIGHEST) + carry
            c_ref[i * cb:(i + 1) * cb, :] = c
            carry = c[cb - 1:cb, :]

    return pl.pallas_call(body, name=name, out_shape=_sds((T, HEAD), F32), compiler_params=_cparams())(f_raw, fbias)


def _fox_gate_bwd(f_raw, fbias, dc, name):
    T = f_raw.shape[0]
    cb = 256

    def body(f_ref, b_ref, dc_ref, df_ref, db_ref):
        row = lax.broadcasted_iota(jnp.int32, (cb, cb), 0)
        col = lax.broadcasted_iota(jnp.int32, (cb, cb), 1)
        tri = (col >= row).astype(F32)
        carry = jnp.zeros((1, HEAD), F32)
        dbias = jnp.zeros((1, HEAD), F32)
        for i in reversed(range(T // cb)):
            dlf = jnp.dot(tri, dc_ref[i * cb:(i + 1) * cb, :], preferred_element_type=F32,
                          precision=lax.Precision.HIGHEST) + carry
            carry = dlf[0:1, :]
            z = f_ref[i * cb:(i + 1) * cb, :] + b_ref[...]
            df = dlf * jax.nn.sigmoid(-z)
            df_ref[i * cb:(i + 1) * cb, :] = df
            dbias = dbias + jnp.sum(df, axis=0, keepdims=True)
        db_ref[...] = dbias

    return pl.pallas_call(body, name=name, out_shape=[_sds((T, HEAD), F32), _sds((1, HEAD), F32)],
                          compiler_params=_cparams())(f_raw, fbias, dc)


def _fox_fwd(qkv, c_col, c_row, tq, name, side=None):
    T = qkv.shape[0]
    H = qkv.shape[1] // (3 * HEAD)
    nq = T // tq
    c_blocks = c_row.reshape(H, nq, 1, tq)

    def body(q_ref, k_ref, v_ref, cq_ref, ck_ref, o_ref, lse_ref):
        qi = pl.program_id(1)
        q, cq = q_ref[...], cq_ref[...]
        causal = lax.broadcasted_iota(jnp.int32, (tq, tq), 1) <= lax.broadcasted_iota(jnp.int32, (tq, tq), 0)

        def key_block(ki, carry, diagonal):
            m, l, acc = carry
            rows = pl.ds(pl.multiple_of(ki * tq, tq), tq)
            s = _nt(q, k_ref[rows, :]) * SCALE + cq - ck_ref[ki]
            if diagonal:
                s = jnp.where(causal, s, NEG)
            m_new = jnp.maximum(m, jnp.max(s, axis=-1, keepdims=True))
            alpha = jnp.exp(m - m_new)
            p = jnp.exp(s - m_new)
            l = alpha * l + jnp.sum(p, axis=-1, keepdims=True)
            acc = alpha * acc + _nn(p.astype(BF16), v_ref[rows, :])
            return m_new, l, acc

        init = (jnp.full((tq, 1), NEG, F32), jnp.zeros((tq, 1), F32), jnp.zeros((tq, HEAD), F32))
        carry = lax.fori_loop(0, qi, lambda ki, c: key_block(ki, c, False), init)
        m, l, acc = key_block(qi, carry, True)
        o_ref[...] = acc / l
        lse_ref[...] = m + jnp.log(l)

    return _call(
        body, name=name, grid=(H, nq), side=side,
        in_specs=[pl.BlockSpec((tq, HEAD), lambda h, qi: (qi, h)),
                  pl.BlockSpec((T, HEAD), lambda h, qi: (0, H + h)),
                  pl.BlockSpec((T, HEAD), lambda h, qi: (0, 2 * H + h)),
                  pl.BlockSpec((None, tq, 1), lambda h, qi: (h, qi, 0)),
                  pl.BlockSpec((None, nq, 1, tq), lambda h, qi: (h, 0, 0, 0))],
        out_specs=[pl.BlockSpec((tq, HEAD), lambda h, qi: (qi, h)),
                   pl.BlockSpec((None, tq, 1), lambda h, qi: (h, qi, 0))],
        out_shape=[_sds((T, H * HEAD), F32), _sds((H, T, 1), F32)],
        args=(qkv, qkv, qkv, c_col, c_blocks))


def _fox_bwd(qkv, c_col, c_row, out, dout, lse, tq, name, side=None):
    T = qkv.shape[0]
    H = qkv.shape[1] // (3 * HEAD)
    nq = T // tq

    def body(q_ref, k_ref, v_ref, cq_ref, ck_ref, o_ref, do_ref, lse_ref, dq_ref, dk_ref, dv_ref, dck_ref, dcq_ref,
             delta_s):
        ki = pl.program_id(1)

        @pl.when(ki == 0)
        def _():
            dq_ref[...] = jnp.zeros_like(dq_ref)
            dcq_ref[...] = jnp.zeros_like(dcq_ref)
            delta_s[...] = jnp.sum(do_ref[...] * o_ref[...], axis=-1, keepdims=True)

        k, v, ck = k_ref[...], v_ref[...], ck_ref[...]
        causal = lax.broadcasted_iota(jnp.int32, (tq, tq), 1) <= lax.broadcasted_iota(jnp.int32, (tq, tq), 0)

        def query_block(qi, carry, diagonal):
            dk, dv, dck = carry
            rows = pl.ds(pl.multiple_of(qi * tq, tq), tq)
            q = q_ref[rows, :]
            s = _nt(q, k) * SCALE + cq_ref[rows, :] - ck
            if diagonal:
                s = jnp.where(causal, s, NEG)
            p = jnp.exp(s - lse_ref[rows, :])
            dob = do_ref[rows, :].astype(BF16)
            ds = p * (_nt(dob, v) - delta_s[rows, :])
            dsb = ds.astype(BF16)
            dq_ref[rows, :] += _nn(dsb, k) * SCALE
            dcq_ref[rows, :] += jnp.sum(ds, axis=-1, keepdims=True)
            return dk + _tn(dsb, q), dv + _tn(p.astype(BF16), dob), dck - jnp.sum(ds, axis=0, keepdims=True)

        init = (jnp.zeros((tq, HEAD), F32), jnp.zeros((tq, HEAD), F32), jnp.zeros((1, tq), F32))
        carry = query_block(ki, init, True)
        dk, dv, dck = lax.fori_loop(ki + 1, nq, lambda qi, c: query_block(qi, c, False), carry)
        dk_ref[...] = dk * SCALE
        dv_ref[...] = dv
        dck_ref[...] = dck

    head = lambda off: pl.BlockSpec((T, HEAD), lambda h, ki: (0, off + h))
    col = pl.BlockSpec((None, T, 1), lambda h, ki: (h, 0, 0))
    return _call(
        body, name=name, grid=(H, nq), side=side,
        in_specs=[head(0),
                  pl.BlockSpec((tq, HEAD), lambda h, ki: (ki, H + h)),
                  pl.BlockSpec((tq, HEAD), lambda h, ki: (ki, 2 * H + h)),
                  col, pl.BlockSpec((None, 1, tq), lambda h, ki: (h, 0, ki)), head(0), head(0), col],
        out_specs=[head(0),
                   pl.BlockSpec((tq, HEAD), lambda h, ki: (ki, h)),
                   pl.BlockSpec((tq, HEAD), lambda h, ki: (ki, h)),
                   pl.BlockSpec((None, 1, tq), lambda h, ki: (h, 0, ki)), col],
        out_shape=[_sds((T, H * HEAD), F32), _sds((T, H * HEAD), F32), _sds((T, H * HEAD), F32), _sds((H, 1, T), F32),
                   _sds((H, T, 1), F32)],
        scratch_shapes=[pltpu.VMEM((T, 1), F32)],
        args=(qkv, qkv, qkv, c_col, c_row, out, dout, lse))


def _t5_bucket(dist):
    max_exact = NUM_BUCKETS // 2
    d = dist.astype(np.float32)
    large = max_exact + (np.log(np.maximum(d, np.float32(1.0)) / np.float32(max_exact))
                         / np.float32(math.log(MAX_DISTANCE / max_exact))
                         * np.float32(NUM_BUCKETS - max_exact)).astype(np.int32)
    large = np.minimum(large, NUM_BUCKETS - 1)
    return np.where(dist < max_exact, dist, large)


def _bucket_maps():
    maps = []
    for d in DILATIONS:
        e = NSLAB // d
        rows = BLK // e
        idx = np.arange(BLK)
        pos = e * (idx % rows) + idx // rows
        qpos = pos[:, None] + BLK
        kpos = np.concatenate([pos, pos + BLK])[None, :]
        delta = qpos - kpos
        band = (delta >= 0) & (delta <= BLK)
        bucket = _t5_bucket(np.clip(delta, 0, None) * d)
        maps.append(np.where(band, bucket, -1).astype(np.int32))
    return np.stack(maps)


def _dil_geometry(T):
    n16 = T // NSLAB
    geo = []
    for d in DILATIONS:
        e = NSLAB // d
        rows = BLK // e
        nblk = n16 // rows
        geo.append((d, e, rows, nblk))
    return geo


DIL_INTERLEAVE_FWD = {1: 4, 4: 8, 16: 8}
DIL_INTERLEAVE_BWD = {1: 8, 4: 8, 16: 8}


def _dil_interleave(per_step, nblocks):
    while per_step > 1 and (nblocks % per_step or nblocks // per_step < 2):
        per_step -= 1
    return per_step


def _dil_bias(tab_ref, bkt_ref, bias_s, h):
    for p in range(len(DILATIONS)):
        bk = bkt_ref[p]
        bias = jnp.full((BLK, 2 * BLK), NEG, F32)
        for b in range(NUM_BUCKETS):
            bias = jnp.where(bk == b, tab_ref[b, h], bias)
        bias_s[p] = bias


def _dil_rows(d, e, rows, sub, blk):
    start = pl.multiple_of(blk * rows, rows)
    return [(sub + d * j, pl.ds(start, rows)) for j in range(e)]


def _gather(ref, idx):
    return jnp.concatenate([ref[s, r, :] for s, r in idx], axis=0)


def _scatter(ref, idx, val, rows):
    for j, (s, r) in enumerate(idx):
        ref[s, r, :] = val[j * rows:(j + 1) * rows]


def _scatter_add(ref, idx, val, rows):
    for j, (s, r) in enumerate(idx):
        ref[s, r, :] += val[j * rows:(j + 1) * rows]


def _dil_fwd(qkv, table, name, side=None):
    n16 = qkv.shape[1]
    T = NSLAB * n16
    H = qkv.shape[2] // (3 * HEAD)
    geo = _dil_geometry(T)
    bkt = jnp.asarray(_bucket_maps())

    def body(tab_ref, bkt_ref, q_ref, k_ref, v_ref, o_ref, lse_ref, bias_s, m_s, l_s):
        h = pl.program_id(0)
        _dil_bias(tab_ref, bkt_ref, bias_s, h)
        first_mask = lax.broadcasted_iota(jnp.int32, (BLK, 2 * BLK), 1) < BLK

        starts = len(DILATIONS) - 1

        def load(p, d, e, rows, sub, blk):
            cur = _dil_rows(d, e, rows, sub, blk)
            prev = _dil_rows(d, e, rows, sub, jnp.maximum(blk - 1, 0))
            q = _gather(q_ref, cur).astype(BF16)
            kk = jnp.concatenate([_gather(k_ref, prev), _gather(k_ref, cur)], axis=0).astype(BF16)
            vv = jnp.concatenate([_gather(v_ref, prev), _gather(v_ref, cur)], axis=0).astype(BF16)
            old = None if p == starts else (_gather(m_s, cur), _gather(l_s, cur), _gather(o_ref, cur))
            return cur, blk, q, kk, vv, old

        def compute(p, blk, q, kk, vv, old):
            s = _nt(q, kk) * SCALE + bias_s[p]
            s = jnp.where(first_mask & (blk == 0), NEG, s)
            m_blk = jnp.max(s, axis=-1, keepdims=True)
            if old is None:
                m_new = m_blk
                pr = jnp.exp(s - m_new)
                l_new = jnp.sum(pr, axis=-1, keepdims=True)
                acc = _nn(pr.astype(BF16), vv)
            else:
                m_old, l_old, acc_old = old
                m_new = jnp.maximum(m_old, m_blk)
                alpha = jnp.exp(m_old - m_new)
                pr = jnp.exp(s - m_new)
                l_new = alpha * l_old + jnp.sum(pr, axis=-1, keepdims=True)
                acc = alpha * acc_old + _nn(pr.astype(BF16), vv)
            if p == 0:
                return acc / l_new, m_new + jnp.log(l_new), None
            return acc, m_new, l_new

        def store(p, rows, cur, acc, m_new, l_new):
            _scatter(o_ref, cur, acc, rows)
            if p == 0:
                _scatter(lse_ref, cur, m_new, rows)
            else:
                _scatter(m_s, cur, m_new, rows)
                _scatter(l_s, cur, l_new, rows)

        for p in reversed(range(len(DILATIONS))):
            d, e, rows, nblk = geo[p]
            per_step = _dil_interleave(DIL_INTERLEAVE_FWD[d], d * nblk)

            def step(i, carry, p=p, d=d, e=e, rows=rows, nblk=nblk, per_step=per_step):
                ids = [i + u * (d * nblk // per_step) for u in range(per_step)]
                loaded = [load(p, d, e, rows, j // nblk, j % nblk) for j in ids]
                done = [(cur, compute(p, blk, q, kk, vv, old)) for cur, blk, q, kk, vv, old in loaded]
                for cur, res in done:
                    store(p, rows, cur, *res)
                return carry

            lax.fori_loop(0, d * nblk // per_step, step, 0)

    head = lambda off: pl.BlockSpec((NSLAB, n16, HEAD), lambda h: (0, 0, off + h))
    return _call(
        body, name=name, grid=(H,), side=side,
        in_specs=[pl.BlockSpec(memory_space=pltpu.SMEM), pl.BlockSpec((3, BLK, 2 * BLK), lambda h: (0, 0, 0)),
                  head(0), head(H), head(2 * H)],
        out_specs=[head(0), pl.BlockSpec((None, NSLAB, n16, 1), lambda h: (h, 0, 0, 0))],
        out_shape=[_sds((NSLAB, n16, H * HEAD), F32), _sds((H, NSLAB, n16, 1), F32)],
        scratch_shapes=[pltpu.VMEM((3, BLK, 2 * BLK), F32), pltpu.VMEM((NSLAB, n16, 1), F32),
                        pltpu.VMEM((NSLAB, n16, 1), F32)],
        args=(table, bkt, qkv, qkv, qkv))


def _dil_bwd(qkv, table, out, dout, lse, name, side=None):
    n16 = qkv.shape[1]
    T = NSLAB * n16
    H = qkv.shape[2] // (3 * HEAD)
    geo = _dil_geometry(T)
    bkt = jnp.asarray(_bucket_maps())

    def body(tab_ref, bkt_ref, q_ref, k_ref, v_ref, o_ref, do_ref, lse_ref,
             dq_ref, dk_ref, dv_ref, dtab_ref, bias_s, dbias_s, delta_s):
        h = pl.program_id(0)
        _dil_bias(tab_ref, bkt_ref, bias_s, h)
        first_mask = lax.broadcasted_iota(jnp.int32, (BLK, 2 * BLK), 1) < BLK
        dbias_s[...] = jnp.zeros_like(dbias_s)
        dq_ref[...] = jnp.zeros_like(dq_ref)
        dk_ref[...] = jnp.zeros_like(dk_ref)
        dv_ref[...] = jnp.zeros_like(dv_ref)
        for r in range(NSLAB):
            delta_s[r] = jnp.sum(do_ref[r] * o_ref[r], axis=-1, keepdims=True)

        def load(d, e, rows, sub, blk):
            cur = _dil_rows(d, e, rows, sub, blk)
            prev = _dil_rows(d, e, rows, sub, jnp.maximum(blk - 1, 0))
            q = _gather(q_ref, cur).astype(BF16)
            kk = jnp.concatenate([_gather(k_ref, prev), _gather(k_ref, cur)], axis=0).astype(BF16)
            vv = jnp.concatenate([_gather(v_ref, prev), _gather(v_ref, cur)], axis=0).astype(BF16)
            dob = _gather(do_ref, cur).astype(BF16)
            return cur, prev, blk, q, kk, vv, dob, _gather(lse_ref, cur), _gather(delta_s, cur)

        def compute(p, blk, q, kk, vv, dob, lse, delta):
            s = _nt(q, kk) * SCALE + bias_s[p]
            s = jnp.where(first_mask & (blk == 0), NEG, s)
            pr = jnp.exp(s - lse)
            ds = pr * (_nt(dob, vv) - delta)
            dsb = ds.astype(BF16)
            return ds, _nn(dsb, kk) * SCALE, _tn(dsb, q) * SCALE, _tn(pr.astype(BF16), dob)

        def store(rows, cur, prev, dq, dkk, dvv):
            _scatter_add(dq_ref, cur, dq, rows)
            _scatter_add(dk_ref, prev, dkk[:BLK], rows)
            _scatter_add(dk_ref, cur, dkk[BLK:], rows)
            _scatter_add(dv_ref, prev, dvv[:BLK], rows)
            _scatter_add(dv_ref, cur, dvv[BLK:], rows)

        for p in range(len(DILATIONS)):
            d, e, rows, nblk = geo[p]
            per_step = _dil_interleave(DIL_INTERLEAVE_BWD[d], d * nblk)

            def step(i, carry, p=p, d=d, e=e, rows=rows, nblk=nblk, per_step=per_step):
                ids = [i + u * (d * nblk // per_step) for u in range(per_step)]
                loaded = [load(d, e, rows, j // nblk, j % nblk) for j in ids]
                done = [(cur, prev, compute(p, *rest)) for cur, prev, *rest in loaded]
                dbias_s[p] += functools.reduce(jnp.add, [res[0] for _, _, res in done])
                for cur, prev, res in done:
                    store(rows, cur, prev, *res[1:])
                return carry

            lax.fori_loop(0, d * nblk // per_step, step, 0)

        lane = lax.broadcasted_iota(jnp.int32, (1, HEAD), 1)
        row = jnp.zeros((1, HEAD), F32)
        for b in range(NUM_BUCKETS):
            tot = jnp.zeros((1, 1), F32)
            for p in range(len(DILATIONS)):
                hit = jnp.where(bkt_ref[p] == b, dbias_s[p], 0.0)
                tot = tot + jnp.sum(jnp.sum(hit, axis=0, keepdims=True), axis=1, keepdims=True)
            row = jnp.where(lane == b, tot, row)
        dtab_ref[...] = row

    head = lambda off: pl.BlockSpec((NSLAB, n16, HEAD), lambda h: (0, 0, off + h))
    return _call(
        body, name=name, grid=(H,), side=side,
        in_specs=[pl.BlockSpec(memory_space=pltpu.SMEM), pl.BlockSpec((3, BLK, 2 * BLK), lambda h: (0, 0, 0)),
                  head(0), head(H), head(2 * H), head(0), head(0),
                  pl.BlockSpec((None, NSLAB, n16, 1), lambda h: (h, 0, 0, 0))],
        out_specs=[head(0), head(0), head(0), pl.BlockSpec((None, 1, HEAD), lambda h: (h, 0, 0))],
        out_shape=[_sds((NSLAB, n16, H * HEAD), F32)] * 3 + [_sds((H, 1, HEAD), F32)],
        scratch_shapes=[pltpu.VMEM((3, BLK, 2 * BLK), F32), pltpu.VMEM((3, BLK, 2 * BLK), F32),
                        pltpu.VMEM((NSLAB, n16, 1), F32)],
        args=(table, bkt, qkv, qkv, qkv, out, dout, lse))


def _qknorm_bwd(raw, dq, dk, dv, gains, tm, name):
    T, N = raw.shape
    C = N // 3

    def body(raw_ref, dq_ref, dk_ref, dv_ref, gains_ref, dp_ref, dg_ref):
        @pl.when(pl.program_id(0) == 0)
        def _():
            dg_ref[...] = jnp.zeros_like(dg_ref)

        for t, d_ref in enumerate((dq_ref, dk_ref)):
            gain = gains_ref[t]
            dgain = jnp.zeros((1, HEAD), F32)
            for k in range(C // HEAD):
                y = raw_ref[:, t * C + k * HEAD:t * C + (k + 1) * HEAD]
                dn = d_ref[:, k * HEAD:(k + 1) * HEAD]
                r = lax.rsqrt(jnp.mean(y * y, axis=-1, keepdims=True) + RMS_EPS)
                yhat = y * r
                gd = dn * gain
                dy = r * (gd - yhat * jnp.mean(gd * yhat, axis=-1, keepdims=True))
                dp_ref[:, t * C + k * HEAD:t * C + (k + 1) * HEAD] = dy.astype(BF16)
                dgain = dgain + jnp.sum(dn * yhat, axis=0, keepdims=True)
            dg_ref[t] += dgain
        dp_ref[:, 2 * C:] = dv_ref[...].astype(BF16)

    third = pl.BlockSpec((tm, C), lambda i: (i, 0))
    return pl.pallas_call(
        body, name=name, grid=(T // tm,),
        in_specs=[pl.BlockSpec((tm, N), lambda i: (i, 0)), third, third, third,
                  pl.BlockSpec((2, 1, HEAD), lambda i: (0, 0, 0))],
        out_specs=[pl.BlockSpec((tm, N), lambda i: (i, 0)), pl.BlockSpec((2, 1, HEAD), lambda i: (0, 0, 0))],
        out_shape=[_sds((T, N), BF16), _sds((2, 1, HEAD), F32)], compiler_params=_cparams(),
    )(raw, dq, dk, dv, gains)


def _loss_grad(y, target, tm, name):
    T, D = y.shape

    def body(y_ref, t_ref, dy_ref, loss_ref):
        @pl.when(pl.program_id(0) == 0)
        def _():
            loss_ref[...] = jnp.zeros_like(loss_ref)

        err = y_ref[...] - t_ref[...]
        dy_ref[...] = err * (1.0 / D)
        per_tok = jnp.mean(err * err, axis=-1, keepdims=True)
        tot = 0.5 * jnp.sum(per_tok, axis=0, keepdims=True)
        lane = lax.broadcasted_iota(jnp.int32, (1, HEAD), 1)
        loss_ref[...] += jnp.where(lane == 0, tot, 0.0)

    row = pl.BlockSpec((tm, D), lambda i: (i, 0))
    return pl.pallas_call(
        body, name=name, grid=(T // tm,), in_specs=[row, row],
        out_specs=[row, pl.BlockSpec((1, HEAD), lambda i: (0, 0))],
        out_shape=[_sds((T, D), F32), _sds((1, HEAD), F32)], compiler_params=_cparams(),
    )(y, target)


def _pad_lanes(v, width=HEAD):
    return jnp.pad(v, ((0, 0), (0, width - v.shape[1])))


def _local_step(x, target, small, wts, plan=None):
    grads = {}

    def hosted(host, fn, *args, **kw):
        side = plan.before(host, wts, grads) if plan is not None else None
        if side is None:
            return fn(*args, name=host, **kw)
        res, side_res = fn(*args, name=host, side=side, **kw)
        plan.after(host, side_res, wts, grads)
        return res

    T, D = x.shape
    C = D // 2
    H = C // HEAD
    n16 = T // NSLAB
    tm = min(512, T)
    tq = min(512, T)
    bn = min(1024, D)
    g1, gm, g2 = small["ffn1_norm"], small["mix_norm"], small["ffn2_norm"]
    gains_a = jnp.stack([small["q_norm_a"], small["k_norm_a"], jnp.ones_like(small["q_norm_a"])])
    gains_b = jnp.stack([small["q_norm_b"], small["k_norm_b"], jnp.ones_like(small["q_norm_b"])])
    fbias = _pad_lanes(small["forget_bias"])
    table = small["rel_bias_table"]

    x1, h1, gu1 = hosted("ffn1_fwd", _ffn_fwd, x, g1, wts["ffn1_in"], wts["ffn1_out"], tm)
    w_in_t, w_f_t, w_o = wts["w_in_t"], wts["w_f_t"], wts["w_o"]
    raw_a, nrm_a, h2a = _proj(x1, gm, w_in_t, gains_a, (True, True, False), tn=C, w_off=0, slabs=True, tm=n16,
                              normed_dtype=F32, name="proj_a")
    raw_b, nrm_b, h2b = _proj(x1, gm, w_in_t, gains_b, (True, True, False), tn=C, w_off=3, slabs=False, tm=tm,
                              normed_dtype=BF16, name="proj_b")
    f_raw, _, _ = _proj(x1, gm, w_f_t, gains_b[:1], (False,), tn=HEAD, w_off=0, slabs=False, tm=tm,
                        normed_dtype=BF16, name="proj_f")
    c = _fox_gate_fwd(f_raw, fbias, "fox_gate_fwd")
    c_heads = c[:, :H].T
    c_col, c_row = c_heads[:, :, None], c_heads[:, None, :]
    out_a, lse_a = hosted("dil_fwd", _dil_fwd, nrm_a, table)
    out_b, lse_b = hosted("fox_fwd", _fox_fwd, nrm_b, c_col, c_row, tq)
    x2a = _mm(out_a, w_o, nt=False, tk=C, tm=n16, a_layout="slab", out_layout="view", resid=x1, name="out_a")
    x2 = _mm(out_b, w_o, nt=False, tk=C, tm=tm, a_layout="nat", out_layout="nat", resid=x2a, w_off=1, name="out_b")
    y, h3, gu3 = _ffn_fwd(x2, g2, wts["ffn2_in"], wts["ffn2_out"], tm, "ffn2_fwd")
    dy, loss_row = _loss_grad(y, target, tm, "loss_grad")

    def ffn_backward(tag, xin, g, h, gu, win, wout, dres):
        nc, tf = wout.shape[0], wout.shape[1]
        dh, dgu, acth = hosted(tag + "_bwd", _ffn_bwd, dres, gu, win, wout, tm)
        dxin, dg = _rms_bwd(xin, g, dh, dres, tm, tag + "_rms_bwd")
        grads[tag + "_norm"] = dg
        grads[tag + "_w_in_t"] = hosted(tag + "_dwin", _mm_tn, dgu.reshape(2 * nc, T, tf), h, bm=tf, bn=bn,
                                        bt=min(1024, T))
        grads[tag + "_w_out"] = hosted(tag + "_dwout", _mm_tn, acth, dres, bm=tf, bn=bn, bt=min(1024, T))
        return dxin

    dx2 = ffn_backward("ffn2", x2, g2, h3, gu3, wts["ffn2_in"], wts["ffn2_out"], dy)

    dmix_a = _mm(dx2, w_o, nt=True, tk=D, tm=n16, a_layout="view", out_layout="slab", n_out=C, name="dmix_a")
    dmix_b = _mm(dx2, w_o, nt=True, tk=D, tm=tm, a_layout="nat", out_layout="nat", n_out=C, w_off=1, name="dmix_b")
    dwo = _mm_tn(out_a.reshape(1, T, C), dx2.reshape(n16, NSLAB * D), bm=C, bn=bn, bt=n16, b_slabs=True,
                 rows=2 * C, name="dwo_a")
    dwo = _mm_tn(out_b.reshape(1, T, C), dx2, bm=C, bn=bn, bt=tm, rows=2 * C, m_off=1, into=dwo, name="dwo_b")
    grads["w_out"] = dwo[0]

    dqa, dka, dva, dtab = hosted("dil_bwd", _dil_bwd, nrm_a, table, out_a, dmix_a, lse_a)
    dqb, dkb, dvb, dck, dcq = hosted("fox_bwd", _fox_bwd, nrm_b, c_col, c_row, out_b, dmix_b, lse_b, tq)
    grads["rel_bias_table"] = dtab[:, 0, :NUM_BUCKETS].T
    dc = _pad_lanes((dck[:, 0, :] + dcq[:, :, 0]).T)
    df, dfb = _fox_gate_bwd(f_raw, fbias, dc, "fox_gate_bwd")
    grads["forget_bias"] = dfb[:, :H]

    flat = lambda a: a.reshape(T, a.shape[-1])
    dproj_a, dgain_a = _qknorm_bwd(flat(raw_a), flat(dqa), flat(dka), flat(dva), gains_a[:2], min(256, T), "qknorm_bwd_a")
    dproj_b, dgain_b = _qknorm_bwd(raw_b, dqb, dkb, dvb, gains_b[:2], min(256, T), "qknorm_bwd_b")
    grads["q_norm_a"], grads["k_norm_a"] = dgain_a[0], dgain_a[1]
    grads["q_norm_b"], grads["k_norm_b"] = dgain_b[0], dgain_b[1]
    dproj_a = dproj_a.reshape(NSLAB, n16, 3 * C)

    dh2 = _mm(dproj_a, w_in_t, nt=False, tk=C, tm=n16, a_layout="slab", out_layout="view", name="dh2_a")
    dh2 = _mm(dproj_b, w_in_t, nt=False, tk=C, tm=tm, a_layout="nat", out_layout="nat", resid=dh2, w_off=3, name="dh2_b")
    dh2 = _mm(df, w_f_t, nt=False, tk=HEAD, tm=tm, a_layout="nat", out_layout="nat", resid=dh2, name="dh2_f")
    dx1, grads["mix_norm"] = _rms_bwd(x1, gm, dh2, dx2, tm, "mix_rms_bwd")
    bt = min(512, T)
    dwt = _mm_tn(flat(dproj_a)[None], flat(h2a), bm=C, bn=bn, bt=bt, rows=6 * C + H, name="dw_a")
    dwt = _mm_tn(dproj_b[None], h2b, bm=C, bn=bn, bt=bt, rows=6 * C + H, m_off=3, into=dwt, name="dw_b")
    dwt = _mm_tn(df[None, :, :H], h2b, bm=H, bn=bn, bt=bt, rows=6 * C + H, m_off=6 * C // H, into=dwt, name="dw_f")
    grads["w_in_t"] = dwt[0]

    grad_x = ffn_backward("ffn1", x, g1, h1, gu1, wts["ffn1_in"], wts["ffn1_out"], dx1)
    return loss_row, grad_x, grads


def _place():
    x, y, c = lax.axis_index("x"), lax.axis_index("y"), lax.axis_index("c")
    other_chips = [(1 - x, y), (x, 1 - y), (1 - x, 1 - y)]
    return x, y, c, other_chips


def _run_side(side, name):
    def body(*refs):
        si, so = len(side.ins), len(side.outs)
        side.start(refs[:si], refs[si:si + so], refs[si + so:])
        side.finish(refs[:si], refs[si:si + so], refs[si + so:])

    return pl.pallas_call(body, name=name, in_specs=[ANY] * len(side.ins), out_specs=[ANY] * len(side.outs),
                          out_shape=side.outs, scratch_shapes=side.sems)(*side.ins)


def _all_gather(shards):
    n = len(shards)

    def plan(ins, outs, sems):
        send_sems, recv_sems, local_sems = sems
        x, y, c, chips = _place()
        me, sibling = (x, y, c), (x, y, 1 - c)

        def copy(a, k, block, to, src=None):
            px, py, pc = block
            dst = outs[a].at[4 * px + 2 * py + pc]
            return pltpu.make_async_remote_copy(
                src_ref=dst if src is None else src, dst_ref=dst, send_sem=send_sems.at[7 * a + k],
                recv_sem=recv_sems.at[7 * a + k], device_id=to, device_id_type=MESH)

        mine = [pltpu.make_async_copy(ins[a], outs[a].at[4 * x + 2 * y + c], local_sems.at[a]) for a in range(n)]
        first = []
        for a in range(n):
            first.append(copy(a, 0, me, sibling, src=ins[a]))
            first += [copy(a, 1 + j, me, (*chip, c), src=ins[a]) for j, chip in enumerate(chips)]
        return copy, mine, first, me, sibling, c, chips

    def start(ins, outs, sems):
        _, mine, first, *_ = plan(ins, outs, sems)
        for cp in mine + first:
            cp.start()

    def finish(ins, outs, sems):
        copy, mine, first, me, sibling, c, chips = plan(ins, outs, sems)
        passed = []
        for a in range(n):
            for j, chip in enumerate(chips):
                copy(a, 1 + j, (*chip, c), me).wait_recv()
                fwd = copy(a, 4 + j, (*chip, c), sibling)
                fwd.start()
                passed.append(fwd)
        for a in range(n):
            copy(a, 0, sibling, me).wait_recv()
            for j, chip in enumerate(chips):
                copy(a, 4 + j, (*chip, 1 - c), me).wait_recv()
        for cp in first + passed:
            cp.wait_send()
        for cp in mine:
            cp.wait()

    return _Side(shards, [_sds((N_DEV,) + s.shape, s.dtype) for s in shards],
                 [pltpu.SemaphoreType.DMA((7 * n,)), pltpu.SemaphoreType.DMA((7 * n,)), pltpu.SemaphoreType.DMA((n,))],
                 start, finish)


def _exchange_in_chip(gs):
    n = len(gs)

    def copies(ins, outs, sems):
        x, y, c, _ = _place()
        return [pltpu.make_async_remote_copy(
            src_ref=ins[a].at[2 * q + 1 - c], dst_ref=outs[a].at[q], send_sem=sems[0].at[4 * a + q],
            recv_sem=sems[1].at[4 * a + q], device_id=(x, y, 1 - c), device_id_type=MESH)
            for a in range(n) for q in range(4)]

    def start(ins, outs, sems):
        for cp in copies(ins, outs, sems):
            cp.start()

    def finish(ins, outs, sems):
        for cp in copies(ins, outs, sems):
            cp.wait()

    return _Side(gs, [_sds((4,) + g.shape[1:], g.dtype) for g in gs],
                 [pltpu.SemaphoreType.DMA((4 * n,)), pltpu.SemaphoreType.DMA((4 * n,))], start, finish)


def _exchange_between_chips(ps):
    n = len(ps)

    def copies(ins, outs, sems):
        x, y, c, chips = _place()
        return [pltpu.make_async_remote_copy(
            src_ref=ins[a].at[2 * cx + cy], dst_ref=outs[a].at[j], send_sem=sems[0].at[3 * a + j],
            recv_sem=sems[1].at[3 * a + j], device_id=(cx, cy, c), device_id_type=MESH)
            for a in range(n) for j, (cx, cy) in enumerate(chips)]

    def start(ins, outs, sems):
        for cp in copies(ins, outs, sems):
            cp.start()

    def finish(ins, outs, sems):
        for cp in copies(ins, outs, sems):
            cp.wait()

    return _Side(ps, [_sds((3,) + p.shape[1:], p.dtype) for p in ps],
                 [pltpu.SemaphoreType.DMA((3 * n,)), pltpu.SemaphoreType.DMA((3 * n,))], start, finish)


def _all_reduce_small(v, name):
    R = v.shape[0]

    def body(v_ref, sum_ref, all_ref, send_sems, recv_sems):
        x, y, c, _ = _place()
        k = 4 * x + 2 * y + c
        all_ref[k] = v_ref[...]
        copies = []
        for rel in range(1, N_DEV):
            fx, fy, fc = (rel >> 2) & 1, (rel >> 1) & 1, rel & 1
            peer = (1 - x if fx else x, 1 - y if fy else y, 1 - c if fc else c)
            copies.append(pltpu.make_async_remote_copy(
                src_ref=v_ref, dst_ref=all_ref.at[k], send_sem=send_sems.at[rel - 1], recv_sem=recv_sems.at[rel - 1],
                device_id=peer, device_id_type=MESH))
        for cp in copies:
            cp.start()
        for rel in range(1, N_DEV):
            fx, fy, fc = (rel >> 2) & 1, (rel >> 1) & 1, rel & 1
            src = 4 * (1 - x if fx else x) + 2 * (1 - y if fy else y) + (1 - c if fc else c)
            pltpu.make_async_remote_copy(
                src_ref=v_ref, dst_ref=all_ref.at[src], send_sem=send_sems.at[rel - 1], recv_sem=recv_sems.at[rel - 1],
                device_id=(x, y, c), device_id_type=MESH).wait_recv()
        for cp in copies:
            cp.wait_send()
        tot = all_ref[0]
        for d in range(1, N_DEV):
            tot = tot + all_ref[d]
        sum_ref[...] = tot

    vm = pl.BlockSpec(memory_space=pltpu.VMEM)
    return pl.pallas_call(
        body, name=name, in_specs=[vm], out_specs=[vm, vm],
        out_shape=[_sds((R, HEAD), F32), _sds((N_DEV, R, HEAD), F32)],
        scratch_shapes=[pltpu.SemaphoreType.DMA((N_DEV - 1,)), pltpu.SemaphoreType.DMA((N_DEV - 1,))],
    )(v)[0]


def _tiles(rows, cols):
    tr = next((cand for cand in (688, 512, 256) if rows % cand == 0), rows)
    tc = 512 if (cols % 512 == 0 and tr * cols * 4 > (2 << 20)) else cols
    return tr, tc


def _chip_sum(g, r1, core, name):
    _, R, Cc = g.shape
    tr, tc = _tiles(R, Cc)

    def body(core_ref, g_ref, r_ref, p_ref):
        p_ref[...] = (g_ref[...] + r_ref[...]).astype(BF16)

    blk = lambda f: pl.BlockSpec((None, tr, tc), f)
    return pl.pallas_call(
        body, name=name,
        grid_spec=pltpu.PrefetchScalarGridSpec(
            num_scalar_prefetch=1, grid=(4, R // tr, Cc // tc),
            in_specs=[blk(lambda q, i, j, core: (2 * q + core[0], i, j)), blk(lambda q, i, j, core: (q, i, j))],
            out_specs=blk(lambda q, i, j, core: (q, i, j))),
        out_shape=_sds((4, R, Cc), BF16), compiler_params=_cparams(),
    )(core, g, r1)


def _final_sum(g, r1, r2, where, name):
    _, R, Cc = g.shape
    tr, tc = _tiles(R, Cc)

    def body(where_ref, g_ref, r1_ref, r2_ref, o_ref):
        o_ref[...] = ((g_ref[...] + r1_ref[...]) + r2_ref[0].astype(F32)) + (r2_ref[1].astype(F32) + r2_ref[2].astype(F32))

    return pl.pallas_call(
        body, name=name,
        grid_spec=pltpu.PrefetchScalarGridSpec(
            num_scalar_prefetch=1, grid=(R // tr, Cc // tc),
            in_specs=[pl.BlockSpec((None, tr, tc), lambda i, j, w: (w[0], i, j)),
                      pl.BlockSpec((None, tr, tc), lambda i, j, w: (w[1], i, j)),
                      pl.BlockSpec((3, tr, tc), lambda i, j, w: (0, i, j))],
            out_specs=pl.BlockSpec((tr, tc), lambda i, j, w: (i, j))),
        out_shape=_sds((R, Cc), F32), compiler_params=_cparams(),
    )(where, g, r1, r2)


def _adamw(w, g, m, v, name):
    R, Cc = w.shape
    tr, tc = _tiles(R, Cc)

    def body(w_ref, g_ref, m_ref, v_ref, d_ref, nm_ref, nv_ref):
        gv = g_ref[...]
        nm = B1 * m_ref[...] + (1.0 - B1) * gv
        nv = B2 * v_ref[...] + (1.0 - B2) * jnp.square(gv)
        m_hat = nm / (1.0 - B1 ** STEP)
        v_hat = nv / (1.0 - B2 ** STEP)
        d_ref[...] = -LR * (m_hat / (jnp.sqrt(v_hat) + EPS) + WD * w_ref[...])
        nm_ref[...] = nm
        nv_ref[...] = nv

    blk = pl.BlockSpec((tr, tc), lambda i, j: (i, j))
    return pl.pallas_call(
        body, name=name, grid=(R // tr, Cc // tc), in_specs=[blk] * 4, out_specs=[blk] * 3,
        out_shape=[_sds((R, Cc), F32)] * 3, compiler_params=_cparams(),
    )(w, g, m, v)


SMALL = ("ffn1_norm", "mix_norm", "ffn2_norm", "q_norm_a", "k_norm_a", "q_norm_b", "k_norm_b", "forget_bias",
         "rel_bias_table")
LARGE = ("ffn1_w_in", "ffn1_w_out", "w_in", "w_out", "ffn2_w_in", "ffn2_w_out")
ORDER = ("ffn1_norm", "ffn1_w_in", "ffn1_w_out", "mix_norm", "w_in", "q_norm_a", "k_norm_a", "q_norm_b", "k_norm_b",
         "forget_bias", "rel_bias_table", "w_out", "ffn2_norm", "ffn2_w_in", "ffn2_w_out")


def _pack_small(vals):
    rows = []
    for name in SMALL:
        flat = vals[name].reshape(-1)
        pad = (-flat.shape[0]) % HEAD
        rows.append(jnp.pad(flat, (0, pad)).reshape(-1, HEAD))
    return jnp.concatenate(rows, axis=0)


def _unpack_small(packed, like):
    out, r = {}, 0
    for name in SMALL:
        size = like[name].size
        nrow = -(-size // HEAD)
        out[name] = packed[r:r + nrow].reshape(-1)[:size].reshape(like[name].shape)
        r += nrow
    return out


def kernel(x, ffn1_norm, ffn1_w_in, ffn1_w_out, mix_norm, w_in, q_norm_a, k_norm_a, q_norm_b, k_norm_b, forget_bias, rel_bias_table, w_out, ffn2_norm, ffn2_w_in, ffn2_w_out, loss_target, m_ffn1_norm, m_ffn1_w_in, m_ffn1_w_out, m_mix_norm, m_w_in, m_q_norm_a, m_k_norm_a, m_q_norm_b, m_k_norm_b, m_forget_bias, m_rel_bias_table, m_w_out, m_ffn2_norm, m_ffn2_w_in, m_ffn2_w_out, v_ffn1_norm, v_ffn1_w_in, v_ffn1_w_out, v_mix_norm, v_w_in, v_q_norm_a, v_k_norm_a, v_q_norm_b, v_k_norm_b, v_forget_bias, v_rel_bias_table, v_w_out, v_ffn2_norm, v_ffn2_w_in, v_ffn2_w_out):
    w = dict(ffn1_norm=ffn1_norm, ffn1_w_in=ffn1_w_in, ffn1_w_out=ffn1_w_out, mix_norm=mix_norm, w_in=w_in,
             q_norm_a=q_norm_a, k_norm_a=k_norm_a, q_norm_b=q_norm_b, k_norm_b=k_norm_b, forget_bias=forget_bias,
             rel_bias_table=rel_bias_table, w_out=w_out, ffn2_norm=ffn2_norm, ffn2_w_in=ffn2_w_in, ffn2_w_out=ffn2_w_out)
    m = dict(ffn1_norm=m_ffn1_norm, ffn1_w_in=m_ffn1_w_in, ffn1_w_out=m_ffn1_w_out, mix_norm=m_mix_norm, w_in=m_w_in,
             q_norm_a=m_q_norm_a, k_norm_a=m_k_norm_a, q_norm_b=m_q_norm_b, k_norm_b=m_k_norm_b,
             forget_bias=m_forget_bias, rel_bias_table=m_rel_bias_table, w_out=m_w_out, ffn2_norm=m_ffn2_norm,
             ffn2_w_in=m_ffn2_w_in, ffn2_w_out=m_ffn2_w_out)
    v = dict(ffn1_norm=v_ffn1_norm, ffn1_w_in=v_ffn1_w_in, ffn1_w_out=v_ffn1_w_out, mix_norm=v_mix_norm, w_in=v_w_in,
             q_norm_a=v_q_norm_a, k_norm_a=v_k_norm_a, q_norm_b=v_q_norm_b, k_norm_b=v_k_norm_b,
             forget_bias=v_forget_bias, rel_bias_table=v_rel_bias_table, w_out=v_w_out, ffn2_norm=v_ffn2_norm,
             ffn2_w_in=v_ffn2_w_in, ffn2_w_out=v_ffn2_w_out)
    T, D = x.shape[1], x.shape[2]
    C = D // 2
    H = C // HEAD
    ff_shard = ffn1_w_out.shape[1]

    f1i, f1o = _run_side(_all_gather([ffn1_w_in[0].T.astype(BF16), ffn1_w_out[0].astype(BF16)]), "gather_ffn1")
    wts = dict(ffn1_in=f1i.reshape(2, N_DEV, ff_shard, D), ffn1_out=f1o)
    xi, yi, ci = lax.axis_index("x"), lax.axis_index("y"), lax.axis_index("c")
    core = jnp.reshape(ci, (1,)).astype(jnp.int32)
    where = jnp.stack([4 * xi + 2 * yi + ci, 2 * xi + yi]).astype(jnp.int32)
    gs, r1, ps, r2 = {}, {}, {}, {}

    def by_destination(name, grads):
        gs[name] = grads[name + "_t" if name.endswith("w_in") else name].reshape(N_DEV, -1, D)
        return gs[name]

    def chip_sums(names):
        for name in names:
            ps[name] = _chip_sum(gs[name], r1[name], core, "chip_sum_" + name)
        return [ps[name] for name in names]

    class Plan:
        carried = {"ffn1_fwd": ("gather", ("w_in", "w_out")),
                   "dil_fwd": ("gather", ("ffn2_w_out",)), "fox_fwd": ("gather", ("ffn2_w_in",)),
                   "dil_bwd": ("in_chip", ("ffn2_w_in", "ffn2_w_out")), "fox_bwd": ("between", ("ffn2_w_in", "ffn2_w_out")),
                   "ffn1_bwd": ("in_chip", ("w_in", "w_out")), "ffn1_dwin": ("between", ("w_in", "w_out")),
                   "ffn1_dwout": ("in_chip", ("ffn1_w_in",))}

        def before(self, host, wts, grads):
            if host not in self.carried:
                return None
            kind, names = self.carried[host]
            if kind == "gather":
                return _all_gather([(w[n][0].T if n.endswith("w_in") else w[n][0]).astype(BF16) for n in names])
            if kind == "in_chip":
                return _exchange_in_chip([by_destination(n, grads) for n in names])
            return _exchange_between_chips(chip_sums(names))

        def after(self, host, res, wts, grads):
            kind, names = self.carried[host]
            if host == "ffn1_fwd":
                w_in_t = res[0].reshape(-1, D)
                wts.update(w_in_t=w_in_t, w_f_t=jnp.pad(w_in_t[6 * C:], ((0, HEAD - H), (0, 0))),
                           w_o=res[1].reshape(2 * C, D))
            elif host == "dil_fwd":
                wts.update(ffn2_out=res[0])
            elif host == "fox_fwd":
                wts.update(ffn2_in=res[0].reshape(2, N_DEV, ff_shard, D))
            else:
                (r1 if kind == "in_chip" else r2).update(zip(names, res))

    small = {name: w[name] for name in SMALL}
    loss_row, grad_x, grads = _local_step(x[0], loss_target[0], small, wts, Plan())

    r1["ffn1_w_out"], = _run_side(_exchange_in_chip([by_destination("ffn1_w_out", grads)]), "reduce_in_chip_tail")
    tail = ("ffn1_w_in", "ffn1_w_out")
    r2.update(zip(tail, _run_side(_exchange_between_chips(chip_sums(tail)), "reduce_between_chips_tail")))
    g_large = {name: _final_sum(gs[name], r1[name], r2[name], where, "final_sum_" + name) for name in LARGE}

    packed = _pack_small(grads)
    nsmall = packed.shape[0]
    packed = jnp.concatenate([packed, loss_row, jnp.zeros(((-nsmall - 1) % 8, HEAD), F32)], axis=0)
    reduced = _all_reduce_small(packed, "reduce_small")
    loss = reduced[nsmall, 0]
    g_small = _unpack_small(reduced[:nsmall], small)

    grad, delta, new_m, new_v = dict(g_small), {}, {}, {}
    for name in LARGE:
        to = (lambda t: t[0].T) if name.endswith("w_in") else (lambda t: t[0])
        back = (lambda t: t.T[None]) if name.endswith("w_in") else (lambda t: t[None])
        d, nm, nv = _adamw(to(w[name]), g_large[name], to(m[name]), to(v[name]), "adamw_" + name)
        grad[name], delta[name], new_m[name], new_v[name] = back(g_large[name]), back(d), back(nm), back(nv)
    d, nm, nv = _adamw(_pack_small(w), reduced[:nsmall], _pack_small(m), _pack_small(v), "adamw_small")
    delta.update(_unpack_small(d, small))
    new_m.update(_unpack_small(nm, small))
    new_v.update(_unpack_small(nv, small))
    return (loss, grad_x[None], *[grad[n] for n in ORDER], *[delta[n] for n in ORDER],
            *[new_m[n] for n in ORDER], *[new_v[n] for n in ORDER])
```

```python
import functools
import math

import numpy as np
import jax
import jax.numpy as jnp
from jax import lax
from jax.experimental import pallas as pl
from jax.experimental.pallas import tpu as pltpu

F32, BF16 = jnp.float32, jnp.bfloat16
HEAD = 128
NSLAB = 16
BLK = 128
DILATIONS = (1, 4, 16)
NUM_BUCKETS, MAX_DISTANCE = 32, 2048
RMS_EPS = 1e-6
NEG = -1e30
SCALE = HEAD ** -0.5
LR, B1, B2, EPS, WD, STEP = 0.001, 0.9, 0.999, 1e-08, 0.01, 10
N_DEV = 8
VMEM_LIMIT_BYTES = 56 << 20
MESH = pl.DeviceIdType.MESH


def _cparams(**kw):
    return pltpu.CompilerParams(vmem_limit_bytes=VMEM_LIMIT_BYTES, **kw)


def _nn(a, b):
    return jnp.dot(a, b, preferred_element_type=F32)


def _nt(a, b):
    return lax.dot_general(a, b, (((1,), (1,)), ((), ())), preferred_element_type=F32)


def _tn(a, b):
    return lax.dot_general(a, b, (((0,), (0,)), ((), ())), preferred_element_type=F32)


def _sds(shape, dtype):
    return jax.ShapeDtypeStruct(shape, dtype)


ANY = pl.BlockSpec(memory_space=pl.ANY)


class _Side:
    def __init__(self, ins, outs, sems, start, finish):
        self.ins, self.outs, self.sems, self.start, self.finish = list(ins), list(outs), list(sems), start, finish


def _call(body, *, name, grid, in_specs, out_specs, out_shape, args, scratch_shapes=(), side=None):
    in_specs, out_specs, out_shape = list(in_specs), list(out_specs), list(out_shape)
    scratch_shapes = list(scratch_shapes)
    if side is None:
        return pl.pallas_call(body, name=name, grid=grid, in_specs=in_specs, out_specs=out_specs, out_shape=out_shape,
                              scratch_shapes=scratch_shapes, compiler_params=_cparams())(*args)
    ni, no, ns = len(args), len(out_shape), len(scratch_shapes)
    si, so = len(side.ins), len(side.outs)

    def fused(*refs):
        h_in, s_in = refs[:ni], refs[ni:ni + si]
        h_out, s_out = refs[ni + si:ni + si + no], refs[ni + si + no:ni + si + no + so]
        h_scr, s_sem = refs[ni + si + no + so:ni + si + no + so + ns], refs[ni + si + no + so + ns:]
        ids = [pl.program_id(k) for k in range(len(grid))]
        first = functools.reduce(jnp.logical_and, [i == 0 for i in ids])
        last = functools.reduce(jnp.logical_and, [i == n - 1 for i, n in zip(ids, grid)])

        @pl.when(first)
        def _():
            side.start(s_in, s_out, s_sem)

        body(*h_in, *h_out, *h_scr)

        @pl.when(last)
        def _():
            side.finish(s_in, s_out, s_sem)

    res = pl.pallas_call(
        fused, name=name, grid=grid, in_specs=in_specs + [ANY] * si, out_specs=out_specs + [ANY] * so,
        out_shape=out_shape + side.outs, scratch_shapes=scratch_shapes + side.sems, compiler_params=_cparams(),
    )(*args, *side.ins)
    return list(res[:no]), list(res[no:])


def _ffn_fwd(x, g, win, wout, tm, name, side=None):
    T, D = x.shape
    nc, tf = wout.shape[0], wout.shape[1]

    def body(x_ref, g_ref, win_ref, wout_ref, y_ref, h_ref, gu_ref):
        j = pl.program_id(1)

        @pl.when(j == 0)
        def _():
            xv = x_ref[...]
            r = lax.rsqrt(jnp.mean(xv * xv, axis=-1, keepdims=True) + RMS_EPS)
            h_ref[...] = (xv * r * g_ref[...]).astype(BF16)
            y_ref[...] = jnp.zeros_like(y_ref)

        hb = h_ref[...]
        gt = _nt(hb, win_ref[0])
        up = _nt(hb, win_ref[1])
        gu_ref[0] = gt.astype(BF16)
        gu_ref[1] = up.astype(BF16)
        act = (gt * jax.nn.sigmoid(gt) * up).astype(BF16)
        y_ref[...] += _nn(act, wout_ref[...])

        @pl.when(j == nc - 1)
        def _():
            y_ref[...] = x_ref[...] + 0.5 * y_ref[...]

    return _call(
        body, name=name, grid=(T // tm, nc), side=side,
        in_specs=[pl.BlockSpec((tm, D), lambda i, j: (i, 0)),
                  pl.BlockSpec((1, D), lambda i, j: (0, 0)),
                  pl.BlockSpec((2, None, tf, D), lambda i, j: (0, j, 0, 0)),
                  pl.BlockSpec((None, tf, D), lambda i, j: (j, 0, 0))],
        out_specs=[pl.BlockSpec((tm, D), lambda i, j: (i, 0)),
                   pl.BlockSpec((tm, D), lambda i, j: (i, 0)),
                   pl.BlockSpec((2, None, tm, tf), lambda i, j: (0, j, i, 0))],
        out_shape=[_sds((T, D), F32), _sds((T, D), BF16), _sds((2, nc, T, tf), BF16)],
        args=(x, g, win, wout))


def _ffn_bwd(dy, gu, win, wout, tm, name, side=None):
    T, D = dy.shape
    nc, tf = wout.shape[0], wout.shape[1]

    def body(dy_ref, gu_ref, win_ref, wout_ref, dh_ref, dgu_ref, act_ref):
        j = pl.program_id(1)

        @pl.when(j == 0)
        def _():
            dh_ref[...] = jnp.zeros_like(dh_ref)

        dyb = (0.5 * dy_ref[...]).astype(BF16)
        dact = _nt(dyb, wout_ref[...])
        gt = gu_ref[0].astype(F32)
        up = gu_ref[1].astype(F32)
        s = jax.nn.sigmoid(gt)
        silu = gt * s
        dgb = (dact * up * (s * (1.0 + gt * (1.0 - s)))).astype(BF16)
        dub = (dact * silu).astype(BF16)
        dgu_ref[0] = dgb
        dgu_ref[1] = dub
        act_ref[...] = (0.5 * silu * up).astype(BF16)
        dh_ref[...] += _nn(dgb, win_ref[0]) + _nn(dub, win_ref[1])

    return _call(
        body, name=name, grid=(T // tm, nc), side=side,
        in_specs=[pl.BlockSpec((tm, D), lambda i, j: (i, 0)),
                  pl.BlockSpec((2, None, tm, tf), lambda i, j: (0, j, i, 0)),
                  pl.BlockSpec((2, None, tf, D), lambda i, j: (0, j, 0, 0)),
                  pl.BlockSpec((None, tf, D), lambda i, j: (j, 0, 0))],
        out_specs=[pl.BlockSpec((tm, D), lambda i, j: (i, 0)),
                   pl.BlockSpec((2, None, tm, tf), lambda i, j: (0, j, i, 0)),
                   pl.BlockSpec((None, tm, tf), lambda i, j: (j, i, 0))],
        out_shape=[_sds((T, D), F32), _sds((2, nc, T, tf), BF16), _sds((nc, T, tf), BF16)],
        args=(dy, gu, win, wout))


def _rms_bwd(x, g, dh, dres, tm, name, side=None):
    T, D = x.shape

    def body(x_ref, g_ref, dh_ref, dres_ref, dx_ref, dg_ref):
        @pl.when(pl.program_id(0) == 0)
        def _():
            dg_ref[...] = jnp.zeros_like(dg_ref)

        xv = x_ref[...]
        r = lax.rsqrt(jnp.mean(xv * xv, axis=-1, keepdims=True) + RMS_EPS)
        xhat = xv * r
        dh = dh_ref[...]
        gd = dh * g_ref[...]
        dx_ref[...] = dres_ref[...] + r * (gd - xhat * jnp.mean(gd * xhat, axis=-1, keepdims=True))
        dg_ref[...] += jnp.sum(dh * xhat, axis=0, keepdims=True)

    row = pl.BlockSpec((tm, D), lambda i: (i, 0))
    one = pl.BlockSpec((1, D), lambda i: (0, 0))
    return _call(body, name=name, grid=(T // tm,), side=side, in_specs=[row, one, row, row], out_specs=[row, one],
                 out_shape=[_sds((T, D), F32), _sds((1, D), F32)], args=(x, g, dh, dres))


def _mm_tn(a, b, *, bm, bn, bt, name, b_slabs=False, side=None, rows=None, m_off=0, into=None):
    nz, T, M = a.shape
    if b_slabs:
        N = b.shape[1] // NSLAB
        assert bt == T // NSLAB
        b_spec = pl.BlockSpec((bt, bn), lambda z, m, n, t: (0, t * (N // bn) + n))
    else:
        N = b.shape[1]
        b_spec = pl.BlockSpec((bt, bn), lambda z, m, n, t: (t, n))
    assert M % bm == 0 and N % bn == 0 and T % bt == 0, (M, bm, N, bn, T, bt)

    def body(a_ref, b_ref, *rest):
        c_ref = rest[-1]

        @pl.when(pl.program_id(3) == 0)
        def _():
            c_ref[...] = jnp.zeros_like(c_ref)

        c_ref[...] += _tn(a_ref[...].astype(BF16), b_ref[...].astype(BF16))

    grid = (nz, M // bm, N // bn, T // bt)
    in_specs = [pl.BlockSpec((None, bt, bm), lambda z, m, n, t: (z, t, m)), b_spec]
    out_spec = pl.BlockSpec((None, bm, bn), lambda z, m, n, t: (z, m + m_off, n))
    out_shape = _sds((nz, M if rows is None else rows, N), F32)
    if into is not None:
        assert side is None and into.shape == out_shape.shape
        return pl.pallas_call(body, name=name, grid=grid, in_specs=in_specs + [ANY], out_specs=out_spec,
                              out_shape=out_shape, input_output_aliases={2: 0}, compiler_params=_cparams())(a, b, into)
    res = _call(body, name=name, grid=grid, side=side, in_specs=in_specs, out_specs=[out_spec],
                out_shape=[out_shape], args=(a, b))
    return res[0] if side is None else (res[0][0], res[1])


def _proj(x, g, wt, gains, modes, *, tn, w_off, slabs, tm, normed_dtype, name):
    T, D = x.shape
    ntile = len(modes)
    N = ntile * tn
    n16 = T // NSLAB
    if slabs:
        assert tm == n16
        x_in = x.reshape(n16, NSLAB * D)
        x_spec = pl.BlockSpec((tm, D), lambda i, n: (0, i))
        oshape = lambda c: (NSLAB, n16, c)
        ospec = lambda bc, cm: pl.BlockSpec((None, tm, bc), lambda i, n: (i, 0, cm(n)))
    else:
        x_in = x
        x_spec = pl.BlockSpec((tm, D), lambda i, n: (i, 0))
        oshape = lambda c: (T, c)
        ospec = lambda bc, cm: pl.BlockSpec((tm, bc), lambda i, n: (i, cm(n)))

    def body(x_ref, g_ref, w_ref, gains_ref, raw_ref, nrm_ref, h_ref):
        n = pl.program_id(1)

        @pl.when(n == 0)
        def _():
            xv = x_ref[...]
            r = lax.rsqrt(jnp.mean(xv * xv, axis=-1, keepdims=True) + RMS_EPS)
            h_ref[...] = (xv * r * g_ref[...]).astype(BF16)

        y = _nt(h_ref[...], w_ref[...])
        raw_ref[...] = y
        for t, mode in enumerate(modes):
            @pl.when(n == t)
            def _(t=t, mode=mode):
                if not mode:
                    nrm_ref[...] = y.astype(nrm_ref.dtype)
                    return
                gain = gains_ref[t]
                for k in range(tn // HEAD):
                    yk = y[:, k * HEAD:(k + 1) * HEAD]
                    r = lax.rsqrt(jnp.mean(yk * yk, axis=-1, keepdims=True) + RMS_EPS)
                    nrm_ref[:, k * HEAD:(k + 1) * HEAD] = (yk * r * gain).astype(nrm_ref.dtype)

    return pl.pallas_call(
        body, name=name, grid=(T // tm, ntile),
        in_specs=[x_spec, pl.BlockSpec((1, D), lambda i, n: (0, 0)),
                  pl.BlockSpec((tn, D), lambda i, n: (n + w_off, 0)),
                  pl.BlockSpec((ntile, 1, HEAD), lambda i, n: (0, 0, 0))],
        out_specs=[ospec(tn, lambda n: n), ospec(tn, lambda n: n), ospec(D, lambda n: 0)],
        out_shape=[_sds(oshape(N), F32), _sds(oshape(N), normed_dtype), _sds(oshape(D), BF16)],
        compiler_params=_cparams(),
    )(x_in, g, wt, gains)


def _mm(a, w, *, nt, tk, tm, a_layout, out_layout, resid=None, name, w_off=0, n_out=None):
    if a_layout == "slab":
        T, K = a.shape[0] * a.shape[1], a.shape[2]
    else:
        T, K = a.shape
    N = (w.shape[0] if nt else w.shape[1]) if n_out is None else n_out
    n16 = T // NSLAB
    nk = K // tk

    def spec(layout, C, bc, colmap):
        if layout == "nat":
            return pl.BlockSpec((tm, bc), lambda i, k: (i, colmap(k)))
        assert tm == n16
        if layout == "slab":
            return pl.BlockSpec((None, tm, bc), lambda i, k: (i, 0, colmap(k)))
        assert bc == C
        return pl.BlockSpec((tm, C), lambda i, k: (0, i))

    a_in = a.reshape(n16, NSLAB * K) if a_layout == "view" else a
    w_spec = (pl.BlockSpec((N, tk), lambda i, k: (w_off, k)) if nt
              else pl.BlockSpec((tk, N), lambda i, k: (k + w_off, 0)))
    o_spec = spec(out_layout, N, N, lambda k: 0)
    oshape = {"nat": (T, N), "slab": (NSLAB, n16, N), "view": (n16, NSLAB * N)}[out_layout]
    has_resid = resid is not None

    def body(*refs):
        a_ref, w_ref = refs[0], refs[1]
        o_ref = refs[-1]
        k = pl.program_id(1)

        @pl.when(k == 0)
        def _():
            o_ref[...] = refs[2][...] if has_resid else jnp.zeros_like(o_ref)

        ab = a_ref[...].astype(BF16)
        o_ref[...] += _nt(ab, w_ref[...]) if nt else _nn(ab, w_ref[...])

    ins = [a_in, w]
    in_specs = [spec(a_layout, K, tk, lambda k: k), w_spec]
    if has_resid:
        ins.append(resid.reshape(n16, NSLAB * N) if out_layout == "view" else resid)
        in_specs.append(o_spec)
    out = pl.pallas_call(
        body, name=name, grid=(T // tm, nk), in_specs=in_specs, out_specs=o_spec,
        out_shape=_sds(oshape, F32), compiler_params=_cparams(),
    )(*ins)
    return out.reshape(T, N) if out_layout == "view" else out


def _log_sigmoid(z):
    return jnp.minimum(z, 0.0) - jnp.log(1.0 + jnp.exp(-jnp.abs(z)))


def _fox_gate_fwd(f_raw, fbias, name):
    T = f_raw.shape[0]
    cb = 256

    def body(f_ref, b_ref, c_ref):
        row = lax.broadcasted_iota(jnp.int32, (cb, cb), 0)
        col = lax.broadcasted_iota(jnp.int32, (cb, cb), 1)
        tri = (col <= row).astype(F32)
        carry = jnp.zeros((1, HEAD), F32)
        for i in range(T // cb):
            lf = _log_sigmoid(f_ref[i * cb:(i + 1) * cb, :] + b_ref[...])
            c = jnp.dot(tri, lf, preferred_element_type=F32, precision=lax.Precision.HIGHEST) + carry
            c_ref[i * cb:(i + 1) * cb, :] = c
            carry = c[cb - 1:cb, :]

    return pl.pallas_call(body, name=name, out_shape=_sds((T, HEAD), F32), compiler_params=_cparams())(f_raw, fbias)


def _fox_gate_bwd(f_raw, fbias, dc, name):
    T = f_raw.shape[0]
    cb = 256

    def body(f_ref, b_ref, dc_ref, df_ref, db_ref):
        row = lax.broadcasted_iota(jnp.int32, (cb, cb), 0)
        col = lax.broadcasted_iota(jnp.int32, (cb, cb), 1)
        tri = (col >= row).astype(F32)
        carry = jnp.zeros((1, HEAD), F32)
        dbias = jnp.zeros((1, HEAD), F32)
        for i in reversed(range(T // cb)):
            dlf = jnp.dot(tri, dc_ref[i * cb:(i + 1) * cb, :], preferred_element_type=F32,
                          precision=lax.Precision.HIGHEST) + carry
            carry = dlf[0:1, :]
            z = f_ref[i * cb:(i + 1) * cb, :] + b_ref[...]
            df = dlf * jax.nn.sigmoid(-z)
            df_ref[i * cb:(i + 1) * cb, :] = df
            dbias = dbias + jnp.sum(df, axis=0, keepdims=True)
        db_ref[...] = dbias

    return pl.pallas_call(body, name=name, out_shape=[_sds((T, HEAD), F32), _sds((1, HEAD), F32)],
                          compiler_params=_cparams())(f_raw, fbias, dc)


def _fox_fwd(qkv, c_col, c_row, tq, name, side=None):
    T = qkv.shape[0]
    H = qkv.shape[1] // (3 * HEAD)
    nq = T // tq
    c_blocks = c_row.reshape(H, nq, 1, tq)

    def body(q_ref, k_ref, v_ref, cq_ref, ck_ref, o_ref, lse_ref):
        qi = pl.program_id(1)
        q, cq = q_ref[...], cq_ref[...]
        causal = lax.broadcasted_iota(jnp.int32, (tq, tq), 1) <= lax.broadcasted_iota(jnp.int32, (tq, tq), 0)

        def key_block(ki, carry, diagonal):
            m, l, acc = carry
            rows = pl.ds(pl.multiple_of(ki * tq, tq), tq)
            s = _nt(q, k_ref[rows, :]) * SCALE + cq - ck_ref[ki]
            if diagonal:
                s = jnp.where(causal, s, NEG)
            m_new = jnp.maximum(m, jnp.max(s, axis=-1, keepdims=True))
            alpha = jnp.exp(m - m_new)
            p = jnp.exp(s - m_new)
            l = alpha * l + jnp.sum(p, axis=-1, keepdims=True)
            acc = alpha * acc + _nn(p.astype(BF16), v_ref[rows, :])
            return m_new, l, acc

        init = (jnp.full((tq, 1), NEG, F32), jnp.zeros((tq, 1), F32), jnp.zeros((tq, HEAD), F32))
        carry = lax.fori_loop(0, qi, lambda ki, c: key_block(ki, c, False), init)
        m, l, acc = key_block(qi, carry, True)
        o_ref[...] = acc / l
        lse_ref[...] = m + jnp.log(l)

    return _call(
        body, name=name, grid=(H, nq), side=side,
        in_specs=[pl.BlockSpec((tq, HEAD), lambda h, qi: (qi, h)),
                  pl.BlockSpec((T, HEAD), lambda h, qi: (0, H + h)),
                  pl.BlockSpec((T, HEAD), lambda h, qi: (0, 2 * H + h)),
                  pl.BlockSpec((None, tq, 1), lambda h, qi: (h, qi, 0)),
                  pl.BlockSpec((None, nq, 1, tq), lambda h, qi: (h, 0, 0, 0))],
        out_specs=[pl.BlockSpec((tq, HEAD), lambda h, qi: (qi, h)),
                   pl.BlockSpec((None, tq, 1), lambda h, qi: (h, qi, 0))],
        out_shape=[_sds((T, H * HEAD), F32), _sds((H, T, 1), F32)],
        args=(qkv, qkv, qkv, c_col, c_blocks))


def _fox_bwd(qkv, c_col, c_row, out, dout, lse, tq, name, side=None):
    T = qkv.shape[0]
    H = qkv.shape[1] // (3 * HEAD)
    nq = T // tq

    def body(q_ref, k_ref, v_ref, cq_ref, ck_ref, o_ref, do_ref, lse_ref, dq_ref, dk_ref, dv_ref, dck_ref, dcq_ref,
             delta_s):
        ki = pl.program_id(1)

        @pl.when(ki == 0)
        def _():
            dq_ref[...] = jnp.zeros_like(dq_ref)
            dcq_ref[...] = jnp.zeros_like(dcq_ref)
            delta_s[...] = jnp.sum(do_ref[...] * o_ref[...], axis=-1, keepdims=True)

        k, v, ck = k_ref[...], v_ref[...], ck_ref[...]
        causal = lax.broadcasted_iota(jnp.int32, (tq, tq), 1) <= lax.broadcasted_iota(jnp.int32, (tq, tq), 0)

        def query_block(qi, carry, diagonal):
            dk, dv, dck = carry
            rows = pl.ds(pl.multiple_of(qi * tq, tq), tq)
            q = q_ref[rows, :]
            s = _nt(q, k) * SCALE + cq_ref[rows, :] - ck
            if diagonal:
                s = jnp.where(causal, s, NEG)
            p = jnp.exp(s - lse_ref[rows, :])
            dob = do_ref[rows, :].astype(BF16)
            ds = p * (_nt(dob, v) - delta_s[rows, :])
            dsb = ds.astype(BF16)
            dq_ref[rows, :] += _nn(dsb, k) * SCALE
            dcq_ref[rows, :] += jnp.sum(ds, axis=-1, keepdims=True)
            return dk + _tn(dsb, q), dv + _tn(p.astype(BF16), dob), dck - jnp.sum(ds, axis=0, keepdims=True)

        init = (jnp.zeros((tq, HEAD), F32), jnp.zeros((tq, HEAD), F32), jnp.zeros((1, tq), F32))
        carry = query_block(ki, init, True)
        dk, dv, dck = lax.fori_loop(ki + 1, nq, lambda qi, c: query_block(qi, c, False), carry)
        dk_ref[...] = dk * SCALE
        dv_ref[...] = dv
        dck_ref[...] = dck

    head = lambda off: pl.BlockSpec((T, HEAD), lambda h, ki: (0, off + h))
    col = pl.BlockSpec((None, T, 1), lambda h, ki: (h, 0, 0))
    return _call(
        body, name=name, grid=(H, nq), side=side,
        in_specs=[head(0),
                  pl.BlockSpec((tq, HEAD), lambda h, ki: (ki, H + h)),
                  pl.BlockSpec((tq, HEAD), lambda h, ki: (ki, 2 * H + h)),
                  col, pl.BlockSpec((None, 1, tq), lambda h, ki: (h, 0, ki)), head(0), head(0), col],
        out_specs=[head(0),
                   pl.BlockSpec((tq, HEAD), lambda h, ki: (ki, h)),
                   pl.BlockSpec((tq, HEAD), lambda h, ki: (ki, h)),
                   pl.BlockSpec((None, 1, tq), lambda h, ki: (h, 0, ki)), col],
        out_shape=[_sds((T, H * HEAD), F32), _sds((T, H * HEAD), F32), _sds((T, H * HEAD), F32), _sds((H, 1, T), F32),
                   _sds((H, T, 1), F32)],
        scratch_shapes=[pltpu.VMEM((T, 1), F32)],
        args=(qkv, qkv, qkv, c_col, c_row, out, dout, lse))


def _t5_bucket(dist):
    max_exact = NUM_BUCKETS // 2
    d = dist.astype(np.float32)
    large = max_exact + (np.log(np.maximum(d, np.float32(1.0)) / np.float32(max_exact))
                         / np.float32(math.log(MAX_DISTANCE / max_exact))
                         * np.float32(NUM_BUCKETS - max_exact)).astype(np.int32)
    large = np.minimum(large, NUM_BUCKETS - 1)
    return np.where(dist < max_exact, dist, large)


def _bucket_maps():
    maps = []
    for d in DILATIONS:
        e = NSLAB // d
        rows = BLK // e
        idx = np.arange(BLK)
        pos = e * (idx % rows) + idx // rows
        qpos = pos[:, None] + BLK
        kpos = np.concatenate([pos, pos + BLK])[None, :]
        delta = qpos - kpos
        band = (delta >= 0) & (delta <= BLK)
        bucket = _t5_bucket(np.clip(delta, 0, None) * d)
        maps.append(np.where(band, bucket, -1).astype(np.int32))
    return np.stack(maps)


def _dil_geometry(T):
    n16 = T // NSLAB
    geo = []
    for d in DILATIONS:
        e = NSLAB // d
        rows = BLK // e
        nblk = n16 // rows
        geo.append((d, e, rows, nblk))
    return geo


DIL_INTERLEAVE_FWD = {1: 4, 4: 8, 16: 8}
DIL_INTERLEAVE_BWD = {1: 8, 4: 8, 16: 8}


def _dil_interleave(per_step, nblocks):
    while per_step > 1 and (nblocks % per_step or nblocks // per_step < 2):
        per_step -= 1
    return per_step


def _dil_bias(tab_ref, bkt_ref, bias_s, h):
    for p in range(len(DILATIONS)):
        bk = bkt_ref[p]
        bias = jnp.full((BLK, 2 * BLK), NEG, F32)
        for b in range(NUM_BUCKETS):
            bias = jnp.where(bk == b, tab_ref[b, h], bias)
        bias_s[p] = bias


def _dil_rows(d, e, rows, sub, blk):
    start = pl.multiple_of(blk * rows, rows)
    return [(sub + d * j, pl.ds(start, rows)) for j in range(e)]


def _gather(ref, idx):
    return jnp.concatenate([ref[s, r, :] for s, r in idx], axis=0)


def _scatter(ref, idx, val, rows):
    for j, (s, r) in enumerate(idx):
        ref[s, r, :] = val[j * rows:(j + 1) * rows]


def _scatter_add(ref, idx, val, rows):
    for j, (s, r) in enumerate(idx):
        ref[s, r, :] += val[j * rows:(j + 1) * rows]


def _dil_fwd(qkv, table, name, side=None):
    n16 = qkv.shape[1]
    T = NSLAB * n16
    H = qkv.shape[2] // (3 * HEAD)
    geo = _dil_geometry(T)
    bkt = jnp.asarray(_bucket_maps())

    def body(tab_ref, bkt_ref, q_ref, k_ref, v_ref, o_ref, lse_ref, bias_s, m_s, l_s):
        h = pl.program_id(0)
        _dil_bias(tab_ref, bkt_ref, bias_s, h)
        first_mask = lax.broadcasted_iota(jnp.int32, (BLK, 2 * BLK), 1) < BLK

        starts = len(DILATIONS) - 1

        def load(p, d, e, rows, sub, blk):
            cur = _dil_rows(d, e, rows, sub, blk)
            prev = _dil_rows(d, e, rows, sub, jnp.maximum(blk - 1, 0))
            q = _gather(q_ref, cur).astype(BF16)
            kk = jnp.concatenate([_gather(k_ref, prev), _gather(k_ref, cur)], axis=0).astype(BF16)
            vv = jnp.concatenate([_gather(v_ref, prev), _gather(v_ref, cur)], axis=0).astype(BF16)
            old = None if p == starts else (_gather(m_s, cur), _gather(l_s, cur), _gather(o_ref, cur))
            return cur, blk, q, kk, vv, old

        def compute(p, blk, q, kk, vv, old):
            s = _nt(q, kk) * SCALE + bias_s[p]
            s = jnp.where(first_mask & (blk == 0), NEG, s)
            m_blk = jnp.max(s, axis=-1, keepdims=True)
            if old is None:
                m_new = m_blk
                pr = jnp.exp(s - m_new)
                l_new = jnp.sum(pr, axis=-1, keepdims=True)
                acc = _nn(pr.astype(BF16), vv)
            else:
                m_old, l_old, acc_old = old
                m_new = jnp.maximum(m_old, m_blk)
                alpha = jnp.exp(m_old - m_new)
                pr = jnp.exp(s - m_new)
                l_new = alpha * l_old + jnp.sum(pr, axis=-1, keepdims=True)
                acc = alpha * acc_old + _nn(pr.astype(BF16), vv)
            if p == 0:
                return acc / l_new, m_new + jnp.log(l_new), None
            return acc, m_new, l_new

        def store(p, rows, cur, acc, m_new, l_new):
            _scatter(o_ref, cur, acc, rows)
            if p == 0:
                _scatter(lse_ref, cur, m_new, rows)
            else:
                _scatter(m_s, cur, m_new, rows)
                _scatter(l_s, cur, l_new, rows)

        for p in reversed(range(len(DILATIONS))):
            d, e, rows, nblk = geo[p]
            per_step = _dil_interleave(DIL_INTERLEAVE_FWD[d], d * nblk)

            def step(i, carry, p=p, d=d, e=e, rows=rows, nblk=nblk, per_step=per_step):
                ids = [i + u * (d * nblk // per_step) for u in range(per_step)]
                loaded = [load(p, d, e, rows, j // nblk, j % nblk) for j in ids]
                done = [(cur, compute(p, blk, q, kk, vv, old)) for cur, blk, q, kk, vv, old in loaded]
                for cur, res in done:
                    store(p, rows, cur, *res)
                return carry

            lax.fori_loop(0, d * nblk // per_step, step, 0)

    head = lambda off: pl.BlockSpec((NSLAB, n16, HEAD), lambda h: (0, 0, off + h))
    return _call(
        body, name=name, grid=(H,), side=side,
        in_specs=[pl.BlockSpec(memory_space=pltpu.SMEM), pl.BlockSpec((3, BLK, 2 * BLK), lambda h: (0, 0, 0)),
                  head(0), head(H), head(2 * H)],
        out_specs=[head(0), pl.BlockSpec((None, NSLAB, n16, 1), lambda h: (h, 0, 0, 0))],
        out_shape=[_sds((NSLAB, n16, H * HEAD), F32), _sds((H, NSLAB, n16, 1), F32)],
        scratch_shapes=[pltpu.VMEM((3, BLK, 2 * BLK), F32), pltpu.VMEM((NSLAB, n16, 1), F32),
                        pltpu.VMEM((NSLAB, n16, 1), F32)],
        args=(table, bkt, qkv, qkv, qkv))


def _dil_bwd(qkv, table, out, dout, lse, name, side=None):
    n16 = qkv.shape[1]
    T = NSLAB * n16
    H = qkv.shape[2] // (3 * HEAD)
    geo = _dil_geometry(T)
    bkt = jnp.asarray(_bucket_maps())

    def body(tab_ref, bkt_ref, q_ref, k_ref, v_ref, o_ref, do_ref, lse_ref,
             dq_ref, dk_ref, dv_ref, dtab_ref, bias_s, dbias_s, delta_s):
        h = pl.program_id(0)
        _dil_bias(tab_ref, bkt_ref, bias_s, h)
        first_mask = lax.broadcasted_iota(jnp.int32, (BLK, 2 * BLK), 1) < BLK
        dbias_s[...] = jnp.zeros_like(dbias_s)
        dq_ref[...] = jnp.zeros_like(dq_ref)
        dk_ref[...] = jnp.zeros_like(dk_ref)
        dv_ref[...] = jnp.zeros_like(dv_ref)
        for r in range(NSLAB):
            delta_s[r] = jnp.sum(do_ref[r] * o_ref[r], axis=-1, keepdims=True)

        def load(d, e, rows, sub, blk):
            cur = _dil_rows(d, e, rows, sub, blk)
            prev = _dil_rows(d, e, rows, sub, jnp.maximum(blk - 1, 0))
            q = _gather(q_ref, cur).astype(BF16)
            kk = jnp.concatenate([_gather(k_ref, prev), _gather(k_ref, cur)], axis=0).astype(BF16)
            vv = jnp.concatenate([_gather(v_ref, prev), _gather(v_ref, cur)], axis=0).astype(BF16)
            dob = _gather(do_ref, cur).astype(BF16)
            return cur, prev, blk, q, kk, vv, dob, _gather(lse_ref, cur), _gather(delta_s, cur)

        def compute(p, blk, q, kk, vv, dob, lse, delta):
            s = _nt(q, kk) * SCALE + bias_s[p]
            s = jnp.where(first_mask & (blk == 0), NEG, s)
            pr = jnp.exp(s - lse)
            ds = pr * (_nt(dob, vv) - delta)
            dsb = ds.astype(BF16)
            return ds, _nn(dsb, kk) * SCALE, _tn(dsb, q) * SCALE, _tn(pr.astype(BF16), dob)

        def store(rows, cur, prev, dq, dkk, dvv):
            _scatter_add(dq_ref, cur, dq, rows)
            _scatter_add(dk_ref, prev, dkk[:BLK], rows)
            _scatter_add(dk_ref, cur, dkk[BLK:], rows)
            _scatter_add(dv_ref, prev, dvv[:BLK], rows)
            _scatter_add(dv_ref, cur, dvv[BLK:], rows)

        for p in range(len(DILATIONS)):
            d, e, rows, nblk = geo[p]
            per_step = _dil_interleave(DIL_INTERLEAVE_BWD[d], d * nblk)

            def step(i, carry, p=p, d=d, e=e, rows=rows, nblk=nblk, per_step=per_step):
                ids = [i + u * (d * nblk // per_step) for u in range(per_step)]
                loaded = [load(d, e, rows, j // nblk, j % nblk) for j in ids]
                done = [(cur, prev, compute(p, *rest)) for cur, prev, *rest in loaded]
                dbias_s[p] += functools.reduce(jnp.add, [res[0] for _, _, res in done])
                for cur, prev, res in done:
                    store(rows, cur, prev, *res[1:])
                return carry

            lax.fori_loop(0, d * nblk // per_step, step, 0)

        lane = lax.broadcasted_iota(jnp.int32, (1, HEAD), 1)
        row = jnp.zeros((1, HEAD), F32)
        for b in range(NUM_BUCKETS):
            tot = jnp.zeros((1, 1), F32)
            for p in range(len(DILATIONS)):
                hit = jnp.where(bkt_ref[p] == b, dbias_s[p], 0.0)
                tot = tot + jnp.sum(jnp.sum(hit, axis=0, keepdims=True), axis=1, keepdims=True)
            row = jnp.where(lane == b, tot, row)
        dtab_ref[...] = row

    head = lambda off: pl.BlockSpec((NSLAB, n16, HEAD), lambda h: (0, 0, off + h))
    return _call(
        body, name=name, grid=(H,), side=side,
        in_specs=[pl.BlockSpec(memory_space=pltpu.SMEM), pl.BlockSpec((3, BLK, 2 * BLK), lambda h: (0, 0, 0)),
                  head(0), head(H), head(2 * H), head(0), head(0),
                  pl.BlockSpec((None, NSLAB, n16, 1), lambda h: (h, 0, 0, 0))],
        out_specs=[head(0), head(0), head(0), pl.BlockSpec((None, 1, HEAD), lambda h: (h, 0, 0))],
        out_shape=[_sds((NSLAB, n16, H * HEAD), F32)] * 3 + [_sds((H, 1, HEAD), F32)],
        scratch_shapes=[pltpu.VMEM((3, BLK, 2 * BLK), F32), pltpu.VMEM((3, BLK, 2 * BLK), F32),
                        pltpu.VMEM((NSLAB, n16, 1), F32)],
        args=(table, bkt, qkv, qkv, qkv, out, dout, lse))


def _qknorm_bwd(raw, dq, dk, dv, gains, tm, name):
    T, N = raw.shape
    C = N // 3

    def body(raw_ref, dq_ref, dk_ref, dv_ref, gains_ref, dp_ref, dg_ref):
        @pl.when(pl.program_id(0) == 0)
        def _():
            dg_ref[...] = jnp.zeros_like(dg_ref)

        for t, d_ref in enumerate((dq_ref, dk_ref)):
            gain = gains_ref[t]
            dgain = jnp.zeros((1, HEAD), F32)
            for k in range(C // HEAD):
                y = raw_ref[:, t * C + k * HEAD:t * C + (k + 1) * HEAD]
                dn = d_ref[:, k * HEAD:(k + 1) * HEAD]
                r = lax.rsqrt(jnp.mean(y * y, axis=-1, keepdims=True) + RMS_EPS)
                yhat = y * r
                gd = dn * gain
                dy = r * (gd - yhat * jnp.mean(gd * yhat, axis=-1, keepdims=True))
                dp_ref[:, t * C + k * HEAD:t * C + (k + 1) * HEAD] = dy.astype(BF16)
                dgain = dgain + jnp.sum(dn * yhat, axis=0, keepdims=True)
            dg_ref[t] += dgain
        dp_ref[:, 2 * C:] = dv_ref[...].astype(BF16)

    third = pl.BlockSpec((tm, C), lambda i: (i, 0))
    return pl.pallas_call(
        body, name=name, grid=(T // tm,),
        in_specs=[pl.BlockSpec((tm, N), lambda i: (i, 0)), third, third, third,
                  pl.BlockSpec((2, 1, HEAD), lambda i: (0, 0, 0))],
        out_specs=[pl.BlockSpec((tm, N), lambda i: (i, 0)), pl.BlockSpec((2, 1, HEAD), lambda i: (0, 0, 0))],
        out_shape=[_sds((T, N), BF16), _sds((2, 1, HEAD), F32)], compiler_params=_cparams(),
    )(raw, dq, dk, dv, gains)


def _loss_grad(y, target, tm, name):
    T, D = y.shape

    def body(y_ref, t_ref, dy_ref, loss_ref):
        @pl.when(pl.program_id(0) == 0)
        def _():
            loss_ref[...] = jnp.zeros_like(loss_ref)

        err = y_ref[...] - t_ref[...]
        dy_ref[...] = err * (1.0 / D)
        per_tok = jnp.mean(err * err, axis=-1, keepdims=True)
        tot = 0.5 * jnp.sum(per_tok, axis=0, keepdims=True)
        lane = lax.broadcasted_iota(jnp.int32, (1, HEAD), 1)
        loss_ref[...] += jnp.where(lane == 0, tot, 0.0)

    row = pl.BlockSpec((tm, D), lambda i: (i, 0))
    return pl.pallas_call(
        body, name=name, grid=(T // tm,), in_specs=[row, row],
        out_specs=[row, pl.BlockSpec((1, HEAD), lambda i: (0, 0))],
        out_shape=[_sds((T, D), F32), _sds((1, HEAD), F32)], compiler_params=_cparams(),
    )(y, target)


def _pad_lanes(v, width=HEAD):
    return jnp.pad(v, ((0, 0), (0, width - v.shape[1])))


def _local_step(x, target, small, wts, plan=None):
    grads = {}

    def hosted(host, fn, *args, **kw):
        side = plan.before(host, wts, grads) if plan is not None else None
        if side is None:
            return fn(*args, name=host, **kw)
        res, side_res = fn(*args, name=host, side=side, **kw)
        plan.after(host, side_res, wts, grads)
        return res

    T, D = x.shape
    C = D // 2
    H = C // HEAD
    n16 = T // NSLAB
    tm = min(512, T)
    tq = min(512, T)
    bn = min(1024, D)
    g1, gm, g2 = small["ffn1_norm"], small["mix_norm"], small["ffn2_norm"]
    gains_a = jnp.stack([small["q_norm_a"], small["k_norm_a"], jnp.ones_like(small["q_norm_a"])])
    gains_b = jnp.stack([small["q_norm_b"], small["k_norm_b"], jnp.ones_like(small["q_norm_b"])])
    fbias = _pad_lanes(small["forget_bias"])
    table = small["rel_bias_table"]

    x1, h1, gu1 = hosted("ffn1_fwd", _ffn_fwd, x, g1, wts["ffn1_in"], wts["ffn1_out"], tm)
    w_in_t, w_f_t, w_o = wts["w_in_t"], wts["w_f_t"], wts["w_o"]
    raw_a, nrm_a, h2a = _proj(x1, gm, w_in_t, gains_a, (True, True, False), tn=C, w_off=0, slabs=True, tm=n16,
                              normed_dtype=F32, name="proj_a")
    raw_b, nrm_b, h2b = _proj(x1, gm, w_in_t, gains_b, (True, True, False), tn=C, w_off=3, slabs=False, tm=tm,
                              normed_dtype=BF16, name="proj_b")
    f_raw, _, _ = _proj(x1, gm, w_f_t, gains_b[:1], (False,), tn=HEAD, w_off=0, slabs=False, tm=tm,
                        normed_dtype=BF16, name="proj_f")
    c = _fox_gate_fwd(f_raw, fbias, "fox_gate_fwd")
    c_heads = c[:, :H].T
    c_col, c_row = c_heads[:, :, None], c_heads[:, None, :]
    out_a, lse_a = hosted("dil_fwd", _dil_fwd, nrm_a, table)
    out_b, lse_b = hosted("fox_fwd", _fox_fwd, nrm_b, c_col, c_row, tq)
    x2a = _mm(out_a, w_o, nt=False, tk=C, tm=n16, a_layout="slab", out_layout="view", resid=x1, name="out_a")
    x2 = _mm(out_b, w_o, nt=False, tk=C, tm=tm, a_layout="nat", out_layout="nat", resid=x2a, w_off=1, name="out_b")
    y, h3, gu3 = _ffn_fwd(x2, g2, wts["ffn2_in"], wts["ffn2_out"], tm, "ffn2_fwd")
    dy, loss_row = _loss_grad(y, target, tm, "loss_grad")

    def ffn_backward(tag, xin, g, h, gu, win, wout, dres):
        nc, tf = wout.shape[0], wout.shape[1]
        dh, dgu, acth = hosted(tag + "_bwd", _ffn_bwd, dres, gu, win, wout, tm)
        grads[tag + "_w_in_t"] = hosted(tag + "_dwin", _mm_tn, dgu.reshape(2 * nc, T, tf), h, bm=tf, bn=bn,
                                        bt=min(4096, T))
        dxin, grads[tag + "_norm"] = hosted(tag + "_rms_bwd", _rms_bwd, xin, g, dh, dres, tm)
        grads[tag + "_w_out"] = hosted(tag + "_dwout", _mm_tn, acth, dres, bm=tf, bn=bn, bt=min(2048, T))
        return dxin

    dx2 = ffn_backward("ffn2", x2, g2, h3, gu3, wts["ffn2_in"], wts["ffn2_out"], dy)

    dmix_a = _mm(dx2, w_o, nt=True, tk=D, tm=n16, a_layout="view", out_layout="slab", n_out=C, name="dmix_a")
    dmix_b = _mm(dx2, w_o, nt=True, tk=D, tm=tm, a_layout="nat", out_layout="nat", n_out=C, w_off=1, name="dmix_b")
    dwo = _mm_tn(out_a.reshape(1, T, C), dx2.reshape(n16, NSLAB * D), bm=C, bn=bn, bt=n16, b_slabs=True,
                 rows=2 * C, name="dwo_a")
    dwo = _mm_tn(out_b.reshape(1, T, C), dx2, bm=C, bn=bn, bt=tm, rows=2 * C, m_off=1, into=dwo, name="dwo_b")
    grads["w_out"] = dwo[0]

    dqa, dka, dva, dtab = hosted("dil_bwd", _dil_bwd, nrm_a, table, out_a, dmix_a, lse_a)
    dqb, dkb, dvb, dck, dcq = hosted("fox_bwd", _fox_bwd, nrm_b, c_col, c_row, out_b, dmix_b, lse_b, tq)
    grads["rel_bias_table"] = dtab[:, 0, :NUM_BUCKETS].T
    dc = _pad_lanes((dck[:, 0, :] + dcq[:, :, 0]).T)
    df, dfb = _fox_gate_bwd(f_raw, fbias, dc, "fox_gate_bwd")
    grads["forget_bias"] = dfb[:, :H]

    flat = lambda a: a.reshape(T, a.shape[-1])
    dproj_a, dgain_a = _qknorm_bwd(flat(raw_a), flat(dqa), flat(dka), flat(dva), gains_a[:2], min(256, T), "qknorm_bwd_a")
    dproj_b, dgain_b = _qknorm_bwd(raw_b, dqb, dkb, dvb, gains_b[:2], min(256, T), "qknorm_bwd_b")
    grads["q_norm_a"], grads["k_norm_a"] = dgain_a[0], dgain_a[1]
    grads["q_norm_b"], grads["k_norm_b"] = dgain_b[0], dgain_b[1]
    dproj_a = dproj_a.reshape(NSLAB, n16, 3 * C)

    dh2 = _mm(dproj_a, w_in_t, nt=False, tk=C, tm=n16, a_layout="slab", out_layout="view", name="dh2_a")
    dh2 = _mm(dproj_b, w_in_t, nt=False, tk=C, tm=tm, a_layout="nat", out_layout="nat", resid=dh2, w_off=3, name="dh2_b")
    dh2 = _mm(df, w_f_t, nt=False, tk=HEAD, tm=tm, a_layout="nat", out_layout="nat", resid=dh2, name="dh2_f")
    dx1, grads["mix_norm"] = _rms_bwd(x1, gm, dh2, dx2, tm, "mix_rms_bwd")
    bt = min(512, T)
    dwt = _mm_tn(flat(dproj_a)[None], flat(h2a), bm=C, bn=bn, bt=bt, rows=6 * C + H, name="dw_a")
    dwt = _mm_tn(dproj_b[None], h2b, bm=C, bn=bn, bt=bt, rows=6 * C + H, m_off=3, into=dwt, name="dw_b")
    dwt = _mm_tn(df[None, :, :H], h2b, bm=H, bn=bn, bt=bt, rows=6 * C + H, m_off=6 * C // H, into=dwt, name="dw_f")
    grads["w_in_t"] = dwt[0]

    grad_x = ffn_backward("ffn1", x, g1, h1, gu1, wts["ffn1_in"], wts["ffn1_out"], dx1)
    return loss_row, grad_x, grads


def _place():
    x, y, c = lax.axis_index("x"), lax.axis_index("y"), lax.axis_index("c")
    other_chips = [(1 - x, y), (x, 1 - y), (1 - x, 1 - y)]
    return x, y, c, other_chips


def _run_side(side, name):
    def body(*refs):
        si, so = len(side.ins), len(side.outs)
        side.start(refs[:si], refs[si:si + so], refs[si + so:])
        side.finish(refs[:si], refs[si:si + so], refs[si + so:])

    return pl.pallas_call(body, name=name, in_specs=[ANY] * len(side.ins), out_specs=[ANY] * len(side.outs),
                          out_shape=side.outs, scratch_shapes=side.sems)(*side.ins)


def _all_gather(shards):
    n = len(shards)

    def plan(ins, outs, sems):
        send_sems, recv_sems, local_sems = sems
        x, y, c, chips = _place()
        me, sibling = (x, y, c), (x, y, 1 - c)

        def copy(a, k, block, to, src=None):
            px, py, pc = block
            dst = outs[a].at[4 * px + 2 * py + pc]
            return pltpu.make_async_remote_copy(
                src_ref=dst if src is None else src, dst_ref=dst, send_sem=send_sems.at[7 * a + k],
                recv_sem=recv_sems.at[7 * a + k], device_id=to, device_id_type=MESH)

        mine = [pltpu.make_async_copy(ins[a], outs[a].at[4 * x + 2 * y + c], local_sems.at[a]) for a in range(n)]
        first = []
        for a in range(n):
            first.append(copy(a, 0, me, sibling, src=ins[a]))
            first += [copy(a, 1 + j, me, (*chip, c), src=ins[a]) for j, chip in enumerate(chips)]
        return copy, mine, first, me, sibling, c, chips

    def start(ins, outs, sems):
        _, mine, first, *_ = plan(ins, outs, sems)
        for cp in mine + first:
            cp.start()

    def finish(ins, outs, sems):
        copy, mine, first, me, sibling, c, chips = plan(ins, outs, sems)
        passed = []
        for a in range(n):
            for j, chip in enumerate(chips):
                copy(a, 1 + j, (*chip, c), me).wait_recv()
                fwd = copy(a, 4 + j, (*chip, c), sibling)
                fwd.start()
                passed.append(fwd)
        for a in range(n):
            copy(a, 0, sibling, me).wait_recv()
            for j, chip in enumerate(chips):
                copy(a, 4 + j, (*chip, 1 - c), me).wait_recv()
        for cp in first + passed:
            cp.wait_send()
        for cp in mine:
            cp.wait()

    return _Side(shards, [_sds((N_DEV,) + s.shape, s.dtype) for s in shards],
                 [pltpu.SemaphoreType.DMA((7 * n,)), pltpu.SemaphoreType.DMA((7 * n,)), pltpu.SemaphoreType.DMA((n,))],
                 start, finish)


def _exchange_in_chip(gs):
    n = len(gs)

    def copies(ins, outs, sems):
        x, y, c, _ = _place()
        return [pltpu.make_async_remote_copy(
            src_ref=ins[a].at[2 * q + 1 - c], dst_ref=outs[a].at[q], send_sem=sems[0].at[4 * a + q],
            recv_sem=sems[1].at[4 * a + q], device_id=(x, y, 1 - c), device_id_type=MESH)
            for a in range(n) for q in range(4)]

    def start(ins, outs, sems):
        for cp in copies(ins, outs, sems):
            cp.start()

    def finish(ins, outs, sems):
        for cp in copies(ins, outs, sems):
            cp.wait()

    return _Side(gs, [_sds((4,) + g.shape[1:], g.dtype) for g in gs],
                 [pltpu.SemaphoreType.DMA((4 * n,)), pltpu.SemaphoreType.DMA((4 * n,))], start, finish)


def _exchange_between_chips(ps):
    n = len(ps)

    def copies(ins, outs, sems):
        x, y, c, chips = _place()
        return [pltpu.make_async_remote_copy(
            src_ref=ins[a].at[2 * cx + cy], dst_ref=outs[a].at[j], send_sem=sems[0].at[3 * a + j],
            recv_sem=sems[1].at[3 * a + j], device_id=(cx, cy, c), device_id_type=MESH)
            for a in range(n) for j, (cx, cy) in enumerate(chips)]

    def start(ins, outs, sems):
        for cp in copies(ins, outs, sems):
            cp.start()

    def finish(ins, outs, sems):
        for cp in copies(ins, outs, sems):
            cp.wait()

    return _Side(ps, [_sds((3,) + p.shape[1:], p.dtype) for p in ps],
                 [pltpu.SemaphoreType.DMA((3 * n,)), pltpu.SemaphoreType.DMA((3 * n,))], start, finish)


def _all_reduce_small(v, name):
    R = v.shape[0]

    def body(v_ref, sum_ref, all_ref, send_sems, recv_sems):
        x, y, c, _ = _place()
        k = 4 * x + 2 * y + c
        all_ref[k] = v_ref[...]
        copies = []
        for rel in range(1, N_DEV):
            fx, fy, fc = (rel >> 2) & 1, (rel >> 1) & 1, rel & 1
            peer = (1 - x if fx else x, 1 - y if fy else y, 1 - c if fc else c)
            copies.append(pltpu.make_async_remote_copy(
                src_ref=v_ref, dst_ref=all_ref.at[k], send_sem=send_sems.at[rel - 1], recv_sem=recv_sems.at[rel - 1],
                device_id=peer, device_id_type=MESH))
        for cp in copies:
            cp.start()
        for rel in range(1, N_DEV):
            fx, fy, fc = (rel >> 2) & 1, (rel >> 1) & 1, rel & 1
            src = 4 * (1 - x if fx else x) + 2 * (1 - y if fy else y) + (1 - c if fc else c)
            pltpu.make_async_remote_copy(
                src_ref=v_ref, dst_ref=all_ref.at[src], send_sem=send_sems.at[rel - 1], recv_sem=recv_sems.at[rel - 1],
                device_id=(x, y, c), device_id_type=MESH).wait_recv()
        for cp in copies:
            cp.wait_send()
        tot = all_ref[0]
        for d in range(1, N_DEV):
            tot = tot + all_ref[d]
        sum_ref[...] = tot

    vm = pl.BlockSpec(memory_space=pltpu.VMEM)
    return pl.pallas_call(
        body, name=name, in_specs=[vm], out_specs=[vm, vm],
        out_shape=[_sds((R, HEAD), F32), _sds((N_DEV, R, HEAD), F32)],
        scratch_shapes=[pltpu.SemaphoreType.DMA((N_DEV - 1,)), pltpu.SemaphoreType.DMA((N_DEV - 1,))],
    )(v)[0]


def _tiles(rows, cols):
    tr = next((cand for cand in (688, 512, 256) if rows % cand == 0), rows)
    tc = 512 if (cols % 512 == 0 and tr * cols * 4 > (2 << 20)) else cols
    return tr, tc


def _chip_sum(g, r1, core, name):
    _, R, Cc = g.shape
    tr, tc = _tiles(R, Cc)

    def body(core_ref, g_ref, r_ref, p_ref):
        p_ref[...] = (g_ref[...] + r_ref[...]).astype(BF16)

    blk = lambda f: pl.BlockSpec((None, tr, tc), f)
    return pl.pallas_call(
        body, name=name,
        grid_spec=pltpu.PrefetchScalarGridSpec(
            num_scalar_prefetch=1, grid=(4, R // tr, Cc // tc),
            in_specs=[blk(lambda q, i, j, core: (2 * q + core[0], i, j)), blk(lambda q, i, j, core: (q, i, j))],
            out_specs=blk(lambda q, i, j, core: (q, i, j))),
        out_shape=_sds((4, R, Cc), BF16), compiler_params=_cparams(),
    )(core, g, r1)


def _final_sum(g, r1, r2, where, name):
    _, R, Cc = g.shape
    tr, tc = _tiles(R, Cc)

    def body(where_ref, g_ref, r1_ref, r2_ref, o_ref):
        o_ref[...] = ((g_ref[...] + r1_ref[...]) + r2_ref[0].astype(F32)) + (r2_ref[1].astype(F32) + r2_ref[2].astype(F32))

    return pl.pallas_call(
        body, name=name,
        grid_spec=pltpu.PrefetchScalarGridSpec(
            num_scalar_prefetch=1, grid=(R // tr, Cc // tc),
            in_specs=[pl.BlockSpec((None, tr, tc), lambda i, j, w: (w[0], i, j)),
                      pl.BlockSpec((None, tr, tc), lambda i, j, w: (w[1], i, j)),
                      pl.BlockSpec((3, tr, tc), lambda i, j, w: (0, i, j))],
            out_specs=pl.BlockSpec((tr, tc), lambda i, j, w: (i, j))),
        out_shape=_sds((R, Cc), F32), compiler_params=_cparams(),
    )(where, g, r1, r2)


def _adamw(w, g, m, v, name):
    R, Cc = w.shape
    tr, tc = _tiles(R, Cc)

    def body(w_ref, g_ref, m_ref, v_ref, d_ref, nm_ref, nv_ref):
        gv = g_ref[...]
        nm = B1 * m_ref[...] + (1.0 - B1) * gv
        nv = B2 * v_ref[...] + (1.0 - B2) * jnp.square(gv)
        m_hat = nm / (1.0 - B1 ** STEP)
        v_hat = nv / (1.0 - B2 ** STEP)
        d_ref[...] = -LR * (m_hat / (jnp.sqrt(v_hat) + EPS) + WD * w_ref[...])
        nm_ref[...] = nm
        nv_ref[...] = nv

    blk = pl.BlockSpec((tr, tc), lambda i, j: (i, j))
    return pl.pallas_call(
        body, name=name, grid=(R // tr, Cc // tc), in_specs=[blk] * 4, out_specs=[blk] * 3,
        out_shape=[_sds((R, Cc), F32)] * 3, compiler_params=_cparams(),
    )(w, g, m, v)


SMALL = ("ffn1_norm", "mix_norm", "ffn2_norm", "q_norm_a", "k_norm_a", "q_norm_b", "k_norm_b", "forget_bias",
         "rel_bias_table")
LARGE = ("ffn1_w_in", "ffn1_w_out", "w_in", "w_out", "ffn2_w_in", "ffn2_w_out")
ORDER = ("ffn1_norm", "ffn1_w_in", "ffn1_w_out", "mix_norm", "w_in", "q_norm_a", "k_norm_a", "q_norm_b", "k_norm_b",
         "forget_bias", "rel_bias_table", "w_out", "ffn2_norm", "ffn2_w_in", "ffn2_w_out")


def _pack_small(vals):
    rows = []
    for name in SMALL:
        flat = vals[name].reshape(-1)
        pad = (-flat.shape[0]) % HEAD
        rows.append(jnp.pad(flat, (0, pad)).reshape(-1, HEAD))
    return jnp.concatenate(rows, axis=0)


def _unpack_small(packed, like):
    out, r = {}, 0
    for name in SMALL:
        size = like[name].size
        nrow = -(-size // HEAD)
        out[name] = packed[r:r + nrow].reshape(-1)[:size].reshape(like[name].shape)
        r += nrow
    return out


def kernel(x, ffn1_norm, ffn1_w_in, ffn1_w_out, mix_norm, w_in, q_norm_a, k_norm_a, q_norm_b, k_norm_b, forget_bias, rel_bias_table, w_out, ffn2_norm, ffn2_w_in, ffn2_w_out, loss_target, m_ffn1_norm, m_ffn1_w_in, m_ffn1_w_out, m_mix_norm, m_w_in, m_q_norm_a, m_k_norm_a, m_q_norm_b, m_k_norm_b, m_forget_bias, m_rel_bias_table, m_w_out, m_ffn2_norm, m_ffn2_w_in, m_ffn2_w_out, v_ffn1_norm, v_ffn1_w_in, v_ffn1_w_out, v_mix_norm, v_w_in, v_q_norm_a, v_k_norm_a, v_q_norm_b, v_k_norm_b, v_forget_bias, v_rel_bias_table, v_w_out, v_ffn2_norm, v_ffn2_w_in, v_ffn2_w_out):
    w = dict(ffn1_norm=ffn1_norm, ffn1_w_in=ffn1_w_in, ffn1_w_out=ffn1_w_out, mix_norm=mix_norm, w_in=w_in,
             q_norm_a=q_norm_a, k_norm_a=k_norm_a, q_norm_b=q_norm_b, k_norm_b=k_norm_b, forget_bias=forget_bias,
             rel_bias_table=rel_bias_table, w_out=w_out, ffn2_norm=ffn2_norm, ffn2_w_in=ffn2_w_in, ffn2_w_out=ffn2_w_out)
    m = dict(ffn1_norm=m_ffn1_norm, ffn1_w_in=m_ffn1_w_in, ffn1_w_out=m_ffn1_w_out, mix_norm=m_mix_norm, w_in=m_w_in,
             q_norm_a=m_q_norm_a, k_norm_a=m_k_norm_a, q_norm_b=m_q_norm_b, k_norm_b=m_k_norm_b,
             forget_bias=m_forget_bias, rel_bias_table=m_rel_bias_table, w_out=m_w_out, ffn2_norm=m_ffn2_norm,
             ffn2_w_in=m_ffn2_w_in, ffn2_w_out=m_ffn2_w_out)
    v = dict(ffn1_norm=v_ffn1_norm, ffn1_w_in=v_ffn1_w_in, ffn1_w_out=v_ffn1_w_out, mix_norm=v_mix_norm, w_in=v_w_in,
             q_norm_a=v_q_norm_a, k_norm_a=v_k_norm_a, q_norm_b=v_q_norm_b, k_norm_b=v_k_norm_b,
             forget_bias=v_forget_bias, rel_bias_table=v_rel_bias_table, w_out=v_w_out, ffn2_norm=v_ffn2_norm,
             ffn2_w_in=v_ffn2_w_in, ffn2_w_out=v_ffn2_w_out)
    T, D = x.shape[1], x.shape[2]
    C = D // 2
    H = C // HEAD
    ff_shard = ffn1_w_out.shape[1]

    f1i, f1o = _run_side(_all_gather([ffn1_w_in[0].T.astype(BF16), ffn1_w_out[0].astype(BF16)]), "gather_ffn1")
    wts = dict(ffn1_in=f1i.reshape(2, N_DEV, ff_shard, D), ffn1_out=f1o)
    xi, yi, ci = lax.axis_index("x"), lax.axis_index("y"), lax.axis_index("c")
    core = jnp.reshape(ci, (1,)).astype(jnp.int32)
    where = jnp.stack([4 * xi + 2 * yi + ci, 2 * xi + yi]).astype(jnp.int32)
    gs, r1, ps, r2 = {}, {}, {}, {}

    def by_destination(name, grads):
        gs[name] = grads[name + "_t" if name.endswith("w_in") else name].reshape(N_DEV, -1, D)
        return gs[name]

    def chip_sums(names):
        for name in names:
            ps[name] = _chip_sum(gs[name], r1[name], core, "chip_sum_" + name)
        return [ps[name] for name in names]

    class Plan:
        carried = {"ffn1_fwd": ("gather", ("w_in", "w_out")),
                   "dil_fwd": ("gather", ("ffn2_w_out",)), "fox_fwd": ("gather", ("ffn2_w_in",)),
                   "dil_bwd": ("in_chip", ("ffn2_w_in", "ffn2_w_out")), "fox_bwd": ("between", ("ffn2_w_in", "ffn2_w_out")),
                   "ffn1_bwd": ("in_chip", ("w_in", "w_out")), "ffn1_dwin": ("between", ("w_in", "w_out")),
                   "ffn1_rms_bwd": ("in_chip", ("ffn1_w_in",)), "ffn1_dwout": ("between", ("ffn1_w_in",))}

        def before(self, host, wts, grads):
            if host not in self.carried:
                return None
            kind, names = self.carried[host]
            if kind == "gather":
                return _all_gather([(w[n][0].T if n.endswith("w_in") else w[n][0]).astype(BF16) for n in names])
            if kind == "in_chip":
                return _exchange_in_chip([by_destination(n, grads) for n in names])
            return _exchange_between_chips(chip_sums(names))

        def after(self, host, res, wts, grads):
            kind, names = self.carried[host]
            if host == "ffn1_fwd":
                w_in_t = res[0].reshape(-1, D)
                wts.update(w_in_t=w_in_t, w_f_t=jnp.pad(w_in_t[6 * C:], ((0, HEAD - H), (0, 0))),
                           w_o=res[1].reshape(2 * C, D))
            elif host == "dil_fwd":
                wts.update(ffn2_out=res[0])
            elif host == "fox_fwd":
                wts.update(ffn2_in=res[0].reshape(2, N_DEV, ff_shard, D))
            else:
                (r1 if kind == "in_chip" else r2).update(zip(names, res))

    small = {name: w[name] for name in SMALL}
    loss_row, grad_x, grads = _local_step(x[0], loss_target[0], small, wts, Plan())

    tail = ("ffn1_w_out",)
    r1[tail[0]], = _run_side(_exchange_in_chip([by_destination(tail[0], grads)]), "reduce_in_chip_tail")
    r2.update(zip(tail, _run_side(_exchange_between_chips(chip_sums(tail)), "reduce_between_chips_tail")))
    g_large = {name: _final_sum(gs[name], r1[name], r2[name], where, "final_sum_" + name) for name in LARGE}

    packed = _pack_small(grads)
    nsmall = packed.shape[0]
    packed = jnp.concatenate([packed, loss_row, jnp.zeros(((-nsmall - 1) % 8, HEAD), F32)], axis=0)
    reduced = _all_reduce_small(packed, "reduce_small")
    loss = reduced[nsmall, 0]
    g_small = _unpack_small(reduced[:nsmall], small)

    grad, delta, new_m, new_v = dict(g_small), {}, {}, {}
    for name in LARGE:
        to = (lambda t: t[0].T) if name.endswith("w_in") else (lambda t: t[0])
        back = (lambda t: t.T[None]) if name.endswith("w_in") else (lambda t: t[None])
        d, nm, nv = _adamw(to(w[name]), g_large[name], to(m[name]), to(v[name]), "adamw_" + name)
        grad[name], delta[name], new_m[name], new_v[name] = back(g_large[name]), back(d), back(nm), back(nv)
    d, nm, nv = _adamw(_pack_small(w), reduced[:nsmall], _pack_small(m), _pack_small(v), "adamw_small")
    delta.update(_unpack_small(d, small))
    new_m.update(_unpack_small(nm, small))
    new_v.update(_unpack_small(nv, small))
    return (loss, grad_x[None], *[grad[n] for n in ORDER], *[delta[n] for n in ORDER],
            *[new_m[n] for n in ORDER], *[new_v[n] for n in ORDER])
```

```python
import functools
import math

import numpy as np
import jax
import jax.numpy as jnp
from jax import lax
from jax.experimental import pallas as pl
from jax.experimental.pallas import tpu as pltpu

F32, BF16 = jnp.float32, jnp.bfloat16
HEAD = 128
NSLAB = 16
BLK = 128
DILATIONS = (1, 4, 16)
NUM_BUCKETS, MAX_DISTANCE = 32, 2048
RMS_EPS = 1e-6
NEG = -1e30
SCALE = HEAD ** -0.5
LR, B1, B2, EPS, WD, STEP = 0.001, 0.9, 0.999, 1e-08, 0.01, 10
N_DEV = 8
VMEM_LIMIT_BYTES = 56 << 20
MESH = pl.DeviceIdType.MESH


def _cparams(**kw):
    return pltpu.CompilerParams(vmem_limit_bytes=VMEM_LIMIT_BYTES, **kw)


def _nn(a, b):
    return jnp.dot(a, b, preferred_element_type=F32)


def _nt(a, b):
    return lax.dot_general(a, b, (((1,), (1,)), ((), ())), preferred_element_type=F32)


def _tn(a, b):
    return lax.dot_general(a, b, (((0,), (0,)), ((), ())), preferred_element_type=F32)


def _sds(shape, dtype):
    return jax.ShapeDtypeStruct(shape, dtype)


ANY = pl.BlockSpec(memory_space=pl.ANY)


class _Side:
    def __init__(self, ins, outs, sems, start, finish):
        self.ins, self.outs, self.sems, self.start, self.finish = list(ins), list(outs), list(sems), start, finish


def _call(body, *, name, grid, in_specs, out_specs, out_shape, args, scratch_shapes=(), side=None):
    in_specs, out_specs, out_shape = list(in_specs), list(out_specs), list(out_shape)
    scratch_shapes = list(scratch_shapes)
    if side is None:
        return pl.pallas_call(body, name=name, grid=grid, in_specs=in_specs, out_specs=out_specs, out_shape=out_shape,
                              scratch_shapes=scratch_shapes, compiler_params=_cparams())(*args)
    ni, no, ns = len(args), len(out_shape), len(scratch_shapes)
    si, so = len(side.ins), len(side.outs)

    def fused(*refs):
        h_in, s_in = refs[:ni], refs[ni:ni + si]
        h_out, s_out = refs[ni + si:ni + si + no], refs[ni + si + no:ni + si + no + so]
        h_scr, s_sem = refs[ni + si + no + so:ni + si + no + so + ns], refs[ni + si + no + so + ns:]
        ids = [pl.program_id(k) for k in range(len(grid))]
        first = functools.reduce(jnp.logical_and, [i == 0 for i in ids])
        last = functools.reduce(jnp.logical_and, [i == n - 1 for i, n in zip(ids, grid)])

        @pl.when(first)
        def _():
            side.start(s_in, s_out, s_sem)

        body(*h_in, *h_out, *h_scr)

        @pl.when(last)
        def _():
            side.finish(s_in, s_out, s_sem)

    res = pl.pallas_call(
        fused, name=name, grid=grid, in_specs=in_specs + [ANY] * si, out_specs=out_specs + [ANY] * so,
        out_shape=out_shape + side.outs, scratch_shapes=scratch_shapes + side.sems, compiler_params=_cparams(),
    )(*args, *side.ins)
    return list(res[:no]), list(res[no:])


def _ffn_fwd(x, g, win, wout, tm, name, side=None):
    T, D = x.shape
    nc, tf = wout.shape[0], wout.shape[1]

    def body(x_ref, g_ref, win_ref, wout_ref, y_ref, h_ref, gu_ref, act_ref):
        j = pl.program_id(1)

        @pl.when(j == 0)
        def _():
            xv = x_ref[...]
            r = lax.rsqrt(jnp.mean(xv * xv, axis=-1, keepdims=True) + RMS_EPS)
            h_ref[...] = (xv * r * g_ref[...]).astype(BF16)
            y_ref[...] = jnp.zeros_like(y_ref)

        hb = h_ref[...]
        gt = _nt(hb, win_ref[0])
        up = _nt(hb, win_ref[1])
        gu_ref[0] = gt.astype(BF16)
        gu_ref[1] = up.astype(BF16)
        act = (gt * jax.nn.sigmoid(gt) * up).astype(BF16)
        act_ref[...] = act
        y_ref[...] += _nn(act, wout_ref[...])

        @pl.when(j == nc - 1)
        def _():
            y_ref[...] = x_ref[...] + 0.5 * y_ref[...]

    return _call(
        body, name=name, grid=(T // tm, nc), side=side,
        in_specs=[pl.BlockSpec((tm, D), lambda i, j: (i, 0)),
                  pl.BlockSpec((1, D), lambda i, j: (0, 0)),
                  pl.BlockSpec((2, None, tf, D), lambda i, j: (0, j, 0, 0)),
                  pl.BlockSpec((None, tf, D), lambda i, j: (j, 0, 0))],
        out_specs=[pl.BlockSpec((tm, D), lambda i, j: (i, 0)),
                   pl.BlockSpec((tm, D), lambda i, j: (i, 0)),
                   pl.BlockSpec((2, None, tm, tf), lambda i, j: (0, j, i, 0)),
                   pl.BlockSpec((None, tm, tf), lambda i, j: (j, i, 0))],
        out_shape=[_sds((T, D), F32), _sds((T, D), BF16), _sds((2, nc, T, tf), BF16), _sds((nc, T, tf), BF16)],
        args=(x, g, win, wout))


def _ffn_bwd(dy, gu, win, wout, tm, name, side=None):
    T, D = dy.shape
    nc, tf = wout.shape[0], wout.shape[1]

    def body(dy_ref, gu_ref, win_ref, wout_ref, dh_ref, dgu_ref, dyb_ref):
        j = pl.program_id(1)

        @pl.when(j == 0)
        def _():
            dh_ref[...] = jnp.zeros_like(dh_ref)
            dyb_ref[...] = (0.5 * dy_ref[...]).astype(BF16)

        dact = _nt(dyb_ref[...], wout_ref[...])
        gt = gu_ref[0].astype(F32)
        up = gu_ref[1].astype(F32)
        s = jax.nn.sigmoid(gt)
        dgb = (dact * up * (s * (1.0 + gt * (1.0 - s)))).astype(BF16)
        dub = (dact * (gt * s)).astype(BF16)
        dgu_ref[0] = dgb
        dgu_ref[1] = dub
        dh_ref[...] += _nn(dgb, win_ref[0]) + _nn(dub, win_ref[1])

    return _call(
        body, name=name, grid=(T // tm, nc), side=side,
        in_specs=[pl.BlockSpec((tm, D), lambda i, j: (i, 0)),
                  pl.BlockSpec((2, None, tm, tf), lambda i, j: (0, j, i, 0)),
                  pl.BlockSpec((2, None, tf, D), lambda i, j: (0, j, 0, 0)),
                  pl.BlockSpec((None, tf, D), lambda i, j: (j, 0, 0))],
        out_specs=[pl.BlockSpec((tm, D), lambda i, j: (i, 0)),
                   pl.BlockSpec((2, None, tm, tf), lambda i, j: (0, j, i, 0)),
                   pl.BlockSpec((tm, D), lambda i, j: (i, 0))],
        out_shape=[_sds((T, D), F32), _sds((2, nc, T, tf), BF16), _sds((T, D), BF16)],
        args=(dy, gu, win, wout))


def _rms_bwd(x, g, dh, dres, tm, name, side=None):
    T, D = x.shape

    def body(x_ref, g_ref, dh_ref, dres_ref, dx_ref, dg_ref):
        @pl.when(pl.program_id(0) == 0)
        def _():
            dg_ref[...] = jnp.zeros_like(dg_ref)

        xv = x_ref[...]
        r = lax.rsqrt(jnp.mean(xv * xv, axis=-1, keepdims=True) + RMS_EPS)
        xhat = xv * r
        dh = dh_ref[...]
        gd = dh * g_ref[...]
        dx_ref[...] = dres_ref[...] + r * (gd - xhat * jnp.mean(gd * xhat, axis=-1, keepdims=True))
        dg_ref[...] += jnp.sum(dh * xhat, axis=0, keepdims=True)

    row = pl.BlockSpec((tm, D), lambda i: (i, 0))
    one = pl.BlockSpec((1, D), lambda i: (0, 0))
    return _call(body, name=name, grid=(T // tm,), side=side, in_specs=[row, one, row, row], out_specs=[row, one],
                 out_shape=[_sds((T, D), F32), _sds((1, D), F32)], args=(x, g, dh, dres))


def _mm_tn(a, b, *, bm, bn, bt, name, b_slabs=False, side=None, rows=None, m_off=0, into=None):
    nz, T, M = a.shape
    if b_slabs:
        N = b.shape[1] // NSLAB
        assert bt == T // NSLAB
        b_spec = pl.BlockSpec((bt, bn), lambda n, z, m, t: (0, t * (N // bn) + n))
    else:
        N = b.shape[1]
        b_spec = pl.BlockSpec((bt, bn), lambda n, z, m, t: (t, n))
    assert M % bm == 0 and N % bn == 0 and T % bt == 0, (M, bm, N, bn, T, bt)

    def body(a_ref, b_ref, *rest):
        c_ref = rest[-1]

        @pl.when(pl.program_id(3) == 0)
        def _():
            c_ref[...] = jnp.zeros_like(c_ref)

        c_ref[...] += _tn(a_ref[...].astype(BF16), b_ref[...].astype(BF16))

    grid = (N // bn, nz, M // bm, T // bt)
    in_specs = [pl.BlockSpec((None, bt, bm), lambda n, z, m, t: (z, t, m)), b_spec]
    out_spec = pl.BlockSpec((None, bm, bn), lambda n, z, m, t: (z, m + m_off, n))
    out_shape = _sds((nz, M if rows is None else rows, N), F32)
    if into is not None:
        assert side is None and into.shape == out_shape.shape
        return pl.pallas_call(body, name=name, grid=grid, in_specs=in_specs + [ANY], out_specs=out_spec,
                              out_shape=out_shape, input_output_aliases={2: 0}, compiler_params=_cparams())(a, b, into)
    res = _call(body, name=name, grid=grid, side=side, in_specs=in_specs, out_specs=[out_spec],
                out_shape=[out_shape], args=(a, b))
    return res[0] if side is None else (res[0][0], res[1])


def _proj(x, g, wt, gains, modes, *, tn, w_off, slabs, tm, normed_dtype, name):
    T, D = x.shape
    ntile = len(modes)
    N = ntile * tn
    n16 = T // NSLAB
    if slabs:
        assert tm == n16
        x_in = x.reshape(n16, NSLAB * D)
        x_spec = pl.BlockSpec((tm, D), lambda i, n: (0, i))
        oshape = lambda c: (NSLAB, n16, c)
        ospec = lambda bc, cm: pl.BlockSpec((None, tm, bc), lambda i, n: (i, 0, cm(n)))
    else:
        x_in = x
        x_spec = pl.BlockSpec((tm, D), lambda i, n: (i, 0))
        oshape = lambda c: (T, c)
        ospec = lambda bc, cm: pl.BlockSpec((tm, bc), lambda i, n: (i, cm(n)))

    def body(x_ref, g_ref, w_ref, gains_ref, raw_ref, nrm_ref, h_ref):
        n = pl.program_id(1)

        @pl.when(n == 0)
        def _():
            xv = x_ref[...]
            r = lax.rsqrt(jnp.mean(xv * xv, axis=-1, keepdims=True) + RMS_EPS)
            h_ref[...] = (xv * r * g_ref[...]).astype(BF16)

        y = _nt(h_ref[...], w_ref[...])
        raw_ref[...] = y
        for t, mode in enumerate(modes):
            @pl.when(n == t)
            def _(t=t, mode=mode):
                if not mode:
                    nrm_ref[...] = y.astype(nrm_ref.dtype)
                    return
                gain = gains_ref[t]
                for k in range(tn // HEAD):
                    yk = y[:, k * HEAD:(k + 1) * HEAD]
                    r = lax.rsqrt(jnp.mean(yk * yk, axis=-1, keepdims=True) + RMS_EPS)
                    nrm_ref[:, k * HEAD:(k + 1) * HEAD] = (yk * r * gain).astype(nrm_ref.dtype)

    return pl.pallas_call(
        body, name=name, grid=(T // tm, ntile),
        in_specs=[x_spec, pl.BlockSpec((1, D), lambda i, n: (0, 0)),
                  pl.BlockSpec((tn, D), lambda i, n: (n + w_off, 0)),
                  pl.BlockSpec((ntile, 1, HEAD), lambda i, n: (0, 0, 0))],
        out_specs=[ospec(tn, lambda n: n), ospec(tn, lambda n: n), ospec(D, lambda n: 0)],
        out_shape=[_sds(oshape(N), F32), _sds(oshape(N), normed_dtype), _sds(oshape(D), BF16)],
        compiler_params=_cparams(),
    )(x_in, g, wt, gains)


def _mm(a, w, *, nt, tk, tm, a_layout, out_layout, resid=None, name, w_off=0, n_out=None):
    if a_layout == "slab":
        T, K = a.shape[0] * a.shape[1], a.shape[2]
    else:
        T, K = a.shape
    N = (w.shape[0] if nt else w.shape[1]) if n_out is None else n_out
    n16 = T // NSLAB
    nk = K // tk

    def spec(layout, C, bc, colmap):
        if layout == "nat":
            return pl.BlockSpec((tm, bc), lambda i, k: (i, colmap(k)))
        assert tm == n16
        if layout == "slab":
            return pl.BlockSpec((None, tm, bc), lambda i, k: (i, 0, colmap(k)))
        assert bc == C
        return pl.BlockSpec((tm, C), lambda i, k: (0, i))

    a_in = a.reshape(n16, NSLAB * K) if a_layout == "view" else a
    w_spec = (pl.BlockSpec((N, tk), lambda i, k: (w_off, k)) if nt
              else pl.BlockSpec((tk, N), lambda i, k: (k + w_off, 0)))
    o_spec = spec(out_layout, N, N, lambda k: 0)
    oshape = {"nat": (T, N), "slab": (NSLAB, n16, N), "view": (n16, NSLAB * N)}[out_layout]
    has_resid = resid is not None

    def body(*refs):
        a_ref, w_ref = refs[0], refs[1]
        o_ref = refs[-1]
        k = pl.program_id(1)

        @pl.when(k == 0)
        def _():
            o_ref[...] = refs[2][...] if has_resid else jnp.zeros_like(o_ref)

        ab = a_ref[...].astype(BF16)
        o_ref[...] += _nt(ab, w_ref[...]) if nt else _nn(ab, w_ref[...])

    ins = [a_in, w]
    in_specs = [spec(a_layout, K, tk, lambda k: k), w_spec]
    if has_resid:
        ins.append(resid.reshape(n16, NSLAB * N) if out_layout == "view" else resid)
        in_specs.append(o_spec)
    out = pl.pallas_call(
        body, name=name, grid=(T // tm, nk), in_specs=in_specs, out_specs=o_spec,
        out_shape=_sds(oshape, F32), compiler_params=_cparams(),
    )(*ins)
    return out.reshape(T, N) if out_layout == "view" else out


def _log_sigmoid(z):
    return jnp.minimum(z, 0.0) - jnp.log(1.0 + jnp.exp(-jnp.abs(z)))


def _fox_gate_fwd(f_raw, fbias, name):
    T = f_raw.shape[0]
    cb = 256

    def body(f_ref, b_ref, c_ref):
        row = lax.broadcasted_iota(jnp.int32, (cb, cb), 0)
        col = lax.broadcasted_iota(jnp.int32, (cb, cb), 1)
        tri = (col <= row).astype(F32)
        carry = jnp.zeros((1, HEAD), F32)
        for i in range(T // cb):
            lf = _log_sigmoid(f_ref[i * cb:(i + 1) * cb, :] + b_ref[...])
            c = jnp.dot(tri, lf, preferred_element_type=F32, precision=lax.Precision.HIGHEST) + carry
            c_ref[i * cb:(i + 1) * cb, :] = c
            carry = c[cb - 1:cb, :]

    return pl.pallas_call(body, name=name, out_shape=_sds((T, HEAD), F32), compiler_params=_cparams())(f_raw, fbias)


def _fox_gate_bwd(f_raw, fbias, dc, name):
    T = f_raw.shape[0]
    cb = 256

    def body(f_ref, b_ref, dc_ref, df_ref, db_ref):
        row = lax.broadcasted_iota(jnp.int32, (cb, cb), 0)
        col = lax.broadcasted_iota(jnp.int32, (cb, cb), 1)
        tri = (col >= row).astype(F32)
        carry = jnp.zeros((1, HEAD), F32)
        dbias = jnp.zeros((1, HEAD), F32)
        for i in reversed(range(T // cb)):
            dlf = jnp.dot(tri, dc_ref[i * cb:(i + 1) * cb, :], preferred_element_type=F32,
                          precision=lax.Precision.HIGHEST) + carry
            carry = dlf[0:1, :]
            z = f_ref[i * cb:(i + 1) * cb, :] + b_ref[...]
            df = dlf * jax.nn.sigmoid(-z)
            df_ref[i * cb:(i + 1) * cb, :] = df
            dbias = dbias + jnp.sum(df, axis=0, keepdims=True)
        db_ref[...] = dbias

    return pl.pallas_call(body, name=name, out_shape=[_sds((T, HEAD), F32), _sds((1, HEAD), F32)],
                          compiler_params=_cparams())(f_raw, fbias, dc)


def _fox_fwd(qkv, c_col, c_row, tq, name, side=None):
    T = qkv.shape[0]
    H = qkv.shape[1] // (3 * HEAD)
    nq = T // tq
    c_blocks = c_row.reshape(H, nq, 1, tq)

    def body(q_ref, k_ref, v_ref, cq_ref, ck_ref, o_ref, lse_ref):
        qi = pl.program_id(1)
        q, cq = q_ref[...], cq_ref[...]
        causal = lax.broadcasted_iota(jnp.int32, (tq, tq), 1) <= lax.broadcasted_iota(jnp.int32, (tq, tq), 0)

        def key_block(ki, carry, diagonal):
            m, l, acc = carry
            rows = pl.ds(pl.multiple_of(ki * tq, tq), tq)
            s = _nt(q, k_ref[rows, :]) * SCALE + cq - ck_ref[ki]
            if diagonal:
                s = jnp.where(causal, s, NEG)
            m_new = jnp.maximum(m, jnp.max(s, axis=-1, keepdims=True))
            alpha = jnp.exp(m - m_new)
            p = jnp.exp(s - m_new)
            l = alpha * l + jnp.sum(p, axis=-1, keepdims=True)
            acc = alpha * acc + _nn(p.astype(BF16), v_ref[rows, :])
            return m_new, l, acc

        init = (jnp.full((tq, 1), NEG, F32), jnp.zeros((tq, 1), F32), jnp.zeros((tq, HEAD), F32))
        carry = lax.fori_loop(0, qi, lambda ki, c: key_block(ki, c, False), init)
        m, l, acc = key_block(qi, carry, True)
        o_ref[...] = acc / l
        lse_ref[...] = m + jnp.log(l)

    return _call(
        body, name=name, grid=(H, nq), side=side,
        in_specs=[pl.BlockSpec((tq, HEAD), lambda h, qi: (qi, h)),
                  pl.BlockSpec((T, HEAD), lambda h, qi: (0, H + h)),
                  pl.BlockSpec((T, HEAD), lambda h, qi: (0, 2 * H + h)),
                  pl.BlockSpec((None, tq, 1), lambda h, qi: (h, qi, 0)),
                  pl.BlockSpec((None, nq, 1, tq), lambda h, qi: (h, 0, 0, 0))],
        out_specs=[pl.BlockSpec((tq, HEAD), lambda h, qi: (qi, h)),
                   pl.BlockSpec((None, tq, 1), lambda h, qi: (h, qi, 0))],
        out_shape=[_sds((T, H * HEAD), F32), _sds((H, T, 1), F32)],
        args=(qkv, qkv, qkv, c_col, c_blocks))


def _fox_bwd(qkv, c_col, c_row, out, dout, lse, tq, name, side=None):
    T = qkv.shape[0]
    H = qkv.shape[1] // (3 * HEAD)
    nq = T // tq

    def body(q_ref, k_ref, v_ref, cq_ref, ck_ref, o_ref, do_ref, lse_ref, dq_ref, dk_ref, dv_ref, dck_ref, dcq_ref,
             delta_s):
        ki = pl.program_id(1)

        @pl.when(ki == 0)
        def _():
            dq_ref[...] = jnp.zeros_like(dq_ref)
            dcq_ref[...] = jnp.zeros_like(dcq_ref)
            delta_s[...] = jnp.sum(do_ref[...] * o_ref[...], axis=-1, keepdims=True)

        k, v, ck = k_ref[...], v_ref[...], ck_ref[...]
        causal = lax.broadcasted_iota(jnp.int32, (tq, tq), 1) <= lax.broadcasted_iota(jnp.int32, (tq, tq), 0)

        def query_block(qi, carry, diagonal):
            dk, dv, dck = carry
            rows = pl.ds(pl.multiple_of(qi * tq, tq), tq)
            q = q_ref[rows, :]
            s = _nt(q, k) * SCALE + cq_ref[rows, :] - ck
            if diagonal:
                s = jnp.where(causal, s, NEG)
            p = jnp.exp(s - lse_ref[rows, :])
            dob = do_ref[rows, :].astype(BF16)
            ds = p * (_nt(dob, v) - delta_s[rows, :])
            dsb = ds.astype(BF16)
            dq_ref[rows, :] += _nn(dsb, k) * SCALE
            dcq_ref[rows, :] += jnp.sum(ds, axis=-1, keepdims=True)
            return dk + _tn(dsb, q), dv + _tn(p.astype(BF16), dob), dck - jnp.sum(ds, axis=0, keepdims=True)

        init = (jnp.zeros((tq, HEAD), F32), jnp.zeros((tq, HEAD), F32), jnp.zeros((1, tq), F32))
        carry = query_block(ki, init, True)
        dk, dv, dck = lax.fori_loop(ki + 1, nq, lambda qi, c: query_block(qi, c, False), carry)
        dk_ref[...] = dk * SCALE
        dv_ref[...] = dv
        dck_ref[...] = dck

    head = lambda off: pl.BlockSpec((T, HEAD), lambda h, ki: (0, off + h))
    col = pl.BlockSpec((None, T, 1), lambda h, ki: (h, 0, 0))
    return _call(
        body, name=name, grid=(H, nq), side=side,
        in_specs=[head(0),
                  pl.BlockSpec((tq, HEAD), lambda h, ki: (ki, H + h)),
                  pl.BlockSpec((tq, HEAD), lambda h, ki: (ki, 2 * H + h)),
                  col, pl.BlockSpec((None, 1, tq), lambda h, ki: (h, 0, ki)), head(0), head(0), col],
        out_specs=[head(0),
                   pl.BlockSpec((tq, HEAD), lambda h, ki: (ki, h)),
                   pl.BlockSpec((tq, HEAD), lambda h, ki: (ki, h)),
                   pl.BlockSpec((None, 1, tq), lambda h, ki: (h, 0, ki)), col],
        out_shape=[_sds((T, H * HEAD), F32), _sds((T, H * HEAD), F32), _sds((T, H * HEAD), F32), _sds((H, 1, T), F32),
                   _sds((H, T, 1), F32)],
        scratch_shapes=[pltpu.VMEM((T, 1), F32)],
        args=(qkv, qkv, qkv, c_col, c_row, out, dout, lse))


def _t5_bucket(dist):
    max_exact = NUM_BUCKETS // 2
    d = dist.astype(np.float32)
    large = max_exact + (np.log(np.maximum(d, np.float32(1.0)) / np.float32(max_exact))
                         / np.float32(math.log(MAX_DISTANCE / max_exact))
                         * np.float32(NUM_BUCKETS - max_exact)).astype(np.int32)
    large = np.minimum(large, NUM_BUCKETS - 1)
    return np.where(dist < max_exact, dist, large)


def _bucket_maps():
    maps = []
    for d in DILATIONS:
        e = NSLAB // d
        rows = BLK // e
        idx = np.arange(BLK)
        pos = e * (idx % rows) + idx // rows
        qpos = pos[:, None] + BLK
        kpos = np.concatenate([pos, pos + BLK])[None, :]
        delta = qpos - kpos
        band = (delta >= 0) & (delta <= BLK)
        bucket = _t5_bucket(np.clip(delta, 0, None) * d)
        maps.append(np.where(band, bucket, -1).astype(np.int32))
    return np.stack(maps)


def _dil_geometry(T):
    n16 = T // NSLAB
    geo = []
    for d in DILATIONS:
        e = NSLAB // d
        rows = BLK // e
        nblk = n16 // rows
        geo.append((d, e, rows, nblk))
    return geo


DIL_INTERLEAVE_FWD = {1: 4, 4: 8, 16: 8}
DIL_INTERLEAVE_BWD = {1: 8, 4: 8, 16: 8}


def _dil_interleave(per_step, nblocks):
    while per_step > 1 and (nblocks % per_step or nblocks // per_step < 2):
        per_step -= 1
    return per_step


def _dil_bias(tab_ref, bkt_ref, bias_s, h):
    for p in range(len(DILATIONS)):
        bk = bkt_ref[p]
        bias = jnp.full((BLK, 2 * BLK), NEG, F32)
        for b in range(NUM_BUCKETS):
            bias = jnp.where(bk == b, tab_ref[b, h], bias)
        bias_s[p] = bias


def _dil_rows(d, e, rows, sub, blk):
    start = pl.multiple_of(blk * rows, rows)
    return [(sub + d * j, pl.ds(start, rows)) for j in range(e)]


def _gather(ref, idx):
    return jnp.concatenate([ref[s, r, :] for s, r in idx], axis=0)


def _scatter(ref, idx, val, rows):
    for j, (s, r) in enumerate(idx):
        ref[s, r, :] = val[j * rows:(j + 1) * rows]


def _scatter_add(ref, idx, val, rows):
    for j, (s, r) in enumerate(idx):
        ref[s, r, :] += val[j * rows:(j + 1) * rows]


def _dil_fwd(qkv, table, name, side=None):
    n16 = qkv.shape[1]
    T = NSLAB * n16
    H = qkv.shape[2] // (3 * HEAD)
    geo = _dil_geometry(T)
    bkt = jnp.asarray(_bucket_maps())

    def body(tab_ref, bkt_ref, q_ref, k_ref, v_ref, o_ref, lse_ref, bias_s, m_s, l_s):
        h = pl.program_id(0)
        _dil_bias(tab_ref, bkt_ref, bias_s, h)
        first_mask = lax.broadcasted_iota(jnp.int32, (BLK, 2 * BLK), 1) < BLK

        starts = len(DILATIONS) - 1

        def load(p, d, e, rows, sub, blk):
            cur = _dil_rows(d, e, rows, sub, blk)
            prev = _dil_rows(d, e, rows, sub, jnp.maximum(blk - 1, 0))
            q = _gather(q_ref, cur).astype(BF16)
            kk = jnp.concatenate([_gather(k_ref, prev), _gather(k_ref, cur)], axis=0).astype(BF16)
            vv = jnp.concatenate([_gather(v_ref, prev), _gather(v_ref, cur)], axis=0).astype(BF16)
            old = None if p == starts else (_gather(m_s, cur), _gather(l_s, cur), _gather(o_ref, cur))
            return cur, blk, q, kk, vv, old

        def compute(p, blk, q, kk, vv, old):
            s = _nt(q, kk) * SCALE + bias_s[p]
            s = jnp.where(first_mask & (blk == 0), NEG, s)
            m_blk = jnp.max(s, axis=-1, keepdims=True)
            if old is None:
                m_new = m_blk
                pr = jnp.exp(s - m_new)
                l_new = jnp.sum(pr, axis=-1, keepdims=True)
                acc = _nn(pr.astype(BF16), vv)
            else:
                m_old, l_old, acc_old = old
                m_new = jnp.maximum(m_old, m_blk)
                alpha = jnp.exp(m_old - m_new)
                pr = jnp.exp(s - m_new)
                l_new = alpha * l_old + jnp.sum(pr, axis=-1, keepdims=True)
                acc = alpha * acc_old + _nn(pr.astype(BF16), vv)
            if p == 0:
                return acc / l_new, m_new + jnp.log(l_new), None
            return acc, m_new, l_new

        def store(p, rows, cur, acc, m_new, l_new):
            _scatter(o_ref, cur, acc, rows)
            if p == 0:
                _scatter(lse_ref, cur, m_new, rows)
            else:
                _scatter(m_s, cur, m_new, rows)
                _scatter(l_s, cur, l_new, rows)

        for p in reversed(range(len(DILATIONS))):
            d, e, rows, nblk = geo[p]
            per_step = _dil_interleave(DIL_INTERLEAVE_FWD[d], d * nblk)

            def step(i, carry, p=p, d=d, e=e, rows=rows, nblk=nblk, per_step=per_step):
                ids = [i + u * (d * nblk // per_step) for u in range(per_step)]
                loaded = [load(p, d, e, rows, j // nblk, j % nblk) for j in ids]
                done = [(cur, compute(p, blk, q, kk, vv, old)) for cur, blk, q, kk, vv, old in loaded]
                for cur, res in done:
                    store(p, rows, cur, *res)
                return carry

            lax.fori_loop(0, d * nblk // per_step, step, 0)

    head = lambda off: pl.BlockSpec((NSLAB, n16, HEAD), lambda h: (0, 0, off + h))
    return _call(
        body, name=name, grid=(H,), side=side,
        in_specs=[pl.BlockSpec(memory_space=pltpu.SMEM), pl.BlockSpec((3, BLK, 2 * BLK), lambda h: (0, 0, 0)),
                  head(0), head(H), head(2 * H)],
        out_specs=[head(0), pl.BlockSpec((None, NSLAB, n16, 1), lambda h: (h, 0, 0, 0))],
        out_shape=[_sds((NSLAB, n16, H * HEAD), F32), _sds((H, NSLAB, n16, 1), F32)],
        scratch_shapes=[pltpu.VMEM((3, BLK, 2 * BLK), F32), pltpu.VMEM((NSLAB, n16, 1), F32),
                        pltpu.VMEM((NSLAB, n16, 1), F32)],
        args=(table, bkt, qkv, qkv, qkv))


def _dil_bwd(qkv, table, out, dout, lse, name, side=None):
    n16 = qkv.shape[1]
    T = NSLAB * n16
    H = qkv.shape[2] // (3 * HEAD)
    geo = _dil_geometry(T)
    bkt = jnp.asarray(_bucket_maps())

    def body(tab_ref, bkt_ref, q_ref, k_ref, v_ref, o_ref, do_ref, lse_ref,
             dq_ref, dk_ref, dv_ref, dtab_ref, bias_s, dbias_s, delta_s):
        h = pl.program_id(0)
        _dil_bias(tab_ref, bkt_ref, bias_s, h)
        first_mask = lax.broadcasted_iota(jnp.int32, (BLK, 2 * BLK), 1) < BLK
        dbias_s[...] = jnp.zeros_like(dbias_s)
        dq_ref[...] = jnp.zeros_like(dq_ref)
        dk_ref[...] = jnp.zeros_like(dk_ref)
        dv_ref[...] = jnp.zeros_like(dv_ref)
        for r in range(NSLAB):
            delta_s[r] = jnp.sum(do_ref[r] * o_ref[r], axis=-1, keepdims=True)

        def load(d, e, rows, sub, blk):
            cur = _dil_rows(d, e, rows, sub, blk)
            prev = _dil_rows(d, e, rows, sub, jnp.maximum(blk - 1, 0))
            q = _gather(q_ref, cur).astype(BF16)
            kk = jnp.concatenate([_gather(k_ref, prev), _gather(k_ref, cur)], axis=0).astype(BF16)
            vv = jnp.concatenate([_gather(v_ref, prev), _gather(v_ref, cur)], axis=0).astype(BF16)
            dob = _gather(do_ref, cur).astype(BF16)
            return cur, prev, blk, q, kk, vv, dob, _gather(lse_ref, cur), _gather(delta_s, cur)

        def compute(p, blk, q, kk, vv, dob, lse, delta):
            s = _nt(q, kk) * SCALE + bias_s[p]
            s = jnp.where(first_mask & (blk == 0), NEG, s)
            pr = jnp.exp(s - lse)
            ds = pr * (_nt(dob, vv) - delta)
            dsb = ds.astype(BF16)
            return ds, _nn(dsb, kk) * SCALE, _tn(dsb, q) * SCALE, _tn(pr.astype(BF16), dob)

        def store(rows, cur, prev, dq, dkk, dvv):
            _scatter_add(dq_ref, cur, dq, rows)
            _scatter_add(dk_ref, prev, dkk[:BLK], rows)
            _scatter_add(dk_ref, cur, dkk[BLK:], rows)
            _scatter_add(dv_ref, prev, dvv[:BLK], rows)
            _scatter_add(dv_ref, cur, dvv[BLK:], rows)

        for p in range(len(DILATIONS)):
            d, e, rows, nblk = geo[p]
            per_step = _dil_interleave(DIL_INTERLEAVE_BWD[d], d * nblk)

            def step(i, carry, p=p, d=d, e=e, rows=rows, nblk=nblk, per_step=per_step):
                ids = [i + u * (d * nblk // per_step) for u in range(per_step)]
                loaded = [load(d, e, rows, j // nblk, j % nblk) for j in ids]
                done = [(cur, prev, compute(p, *rest)) for cur, prev, *rest in loaded]
                dbias_s[p] += functools.reduce(jnp.add, [res[0] for _, _, res in done])
                for cur, prev, res in done:
                    store(rows, cur, prev, *res[1:])
                return carry

            lax.fori_loop(0, d * nblk // per_step, step, 0)

        lane = lax.broadcasted_iota(jnp.int32, (1, HEAD), 1)
        row = jnp.zeros((1, HEAD), F32)
        for b in range(NUM_BUCKETS):
            tot = jnp.zeros((1, 1), F32)
            for p in range(len(DILATIONS)):
                hit = jnp.where(bkt_ref[p] == b, dbias_s[p], 0.0)
                tot = tot + jnp.sum(jnp.sum(hit, axis=0, keepdims=True), axis=1, keepdims=True)
            row = jnp.where(lane == b, tot, row)
        dtab_ref[...] = row

    head = lambda off: pl.BlockSpec((NSLAB, n16, HEAD), lambda h: (0, 0, off + h))
    return _call(
        body, name=name, grid=(H,), side=side,
        in_specs=[pl.BlockSpec(memory_space=pltpu.SMEM), pl.BlockSpec((3, BLK, 2 * BLK), lambda h: (0, 0, 0)),
                  head(0), head(H), head(2 * H), head(0), head(0),
                  pl.BlockSpec((None, NSLAB, n16, 1), lambda h: (h, 0, 0, 0))],
        out_specs=[head(0), head(0), head(0), pl.BlockSpec((None, 1, HEAD), lambda h: (h, 0, 0))],
        out_shape=[_sds((NSLAB, n16, H * HEAD), F32)] * 3 + [_sds((H, 1, HEAD), F32)],
        scratch_shapes=[pltpu.VMEM((3, BLK, 2 * BLK), F32), pltpu.VMEM((3, BLK, 2 * BLK), F32),
                        pltpu.VMEM((NSLAB, n16, 1), F32)],
        args=(table, bkt, qkv, qkv, qkv, out, dout, lse))


def _qknorm_bwd(raw, dq, dk, dv, gains, tm, name):
    T, N = raw.shape
    C = N // 3

    def body(raw_ref, dq_ref, dk_ref, dv_ref, gains_ref, dp_ref, dg_ref):
        @pl.when(pl.program_id(0) == 0)
        def _():
            dg_ref[...] = jnp.zeros_like(dg_ref)

        for t, d_ref in enumerate((dq_ref, dk_ref)):
            gain = gains_ref[t]
            dgain = jnp.zeros((1, HEAD), F32)
            for k in range(C // HEAD):
                y = raw_ref[:, t * C + k * HEAD:t * C + (k + 1) * HEAD]
                dn = d_ref[:, k * HEAD:(k + 1) * HEAD]
                r = lax.rsqrt(jnp.mean(y * y, axis=-1, keepdims=True) + RMS_EPS)
                yhat = y * r
                gd = dn * gain
                dy = r * (gd - yhat * jnp.mean(gd * yhat, axis=-1, keepdims=True))
                dp_ref[:, t * C + k * HEAD:t * C + (k + 1) * HEAD] = dy.astype(BF16)
                dgain = dgain + jnp.sum(dn * yhat, axis=0, keepdims=True)
            dg_ref[t] += dgain
        dp_ref[:, 2 * C:] = dv_ref[...].astype(BF16)

    third = pl.BlockSpec((tm, C), lambda i: (i, 0))
    return pl.pallas_call(
        body, name=name, grid=(T // tm,),
        in_specs=[pl.BlockSpec((tm, N), lambda i: (i, 0)), third, third, third,
                  pl.BlockSpec((2, 1, HEAD), lambda i: (0, 0, 0))],
        out_specs=[pl.BlockSpec((tm, N), lambda i: (i, 0)), pl.BlockSpec((2, 1, HEAD), lambda i: (0, 0, 0))],
        out_shape=[_sds((T, N), BF16), _sds((2, 1, HEAD), F32)], compiler_params=_cparams(),
    )(raw, dq, dk, dv, gains)


def _loss_grad(y, target, tm, name):
    T, D = y.shape

    def body(y_ref, t_ref, dy_ref, loss_ref):
        @pl.when(pl.program_id(0) == 0)
        def _():
            loss_ref[...] = jnp.zeros_like(loss_ref)

        err = y_ref[...] - t_ref[...]
        dy_ref[...] = err * (1.0 / D)
        per_tok = jnp.mean(err * err, axis=-1, keepdims=True)
        tot = 0.5 * jnp.sum(per_tok, axis=0, keepdims=True)
        lane = lax.broadcasted_iota(jnp.int32, (1, HEAD), 1)
        loss_ref[...] += jnp.where(lane == 0, tot, 0.0)

    row = pl.BlockSpec((tm, D), lambda i: (i, 0))
    return pl.pallas_call(
        body, name=name, grid=(T // tm,), in_specs=[row, row],
        out_specs=[row, pl.BlockSpec((1, HEAD), lambda i: (0, 0))],
        out_shape=[_sds((T, D), F32), _sds((1, HEAD), F32)], compiler_params=_cparams(),
    )(y, target)


def _pad_lanes(v, width=HEAD):
    return jnp.pad(v, ((0, 0), (0, width - v.shape[1])))


def _local_step(x, target, small, wts, plan=None):
    grads = {}

    def hosted(host, fn, *args, **kw):
        side = plan.before(host, wts, grads) if plan is not None else None
        if side is None:
            return fn(*args, name=host, **kw)
        res, side_res = fn(*args, name=host, side=side, **kw)
        plan.after(host, side_res, wts, grads)
        return res

    T, D = x.shape
    C = D // 2
    H = C // HEAD
    n16 = T // NSLAB
    tm = min(512, T)
    tq = min(512, T)
    bn = min(1024, D)
    g1, gm, g2 = small["ffn1_norm"], small["mix_norm"], small["ffn2_norm"]
    gains_a = jnp.stack([small["q_norm_a"], small["k_norm_a"], jnp.ones_like(small["q_norm_a"])])
    gains_b = jnp.stack([small["q_norm_b"], small["k_norm_b"], jnp.ones_like(small["q_norm_b"])])
    fbias = _pad_lanes(small["forget_bias"])
    table = small["rel_bias_table"]

    x1, h1, gu1, act1 = hosted("ffn1_fwd", _ffn_fwd, x, g1, wts["ffn1_in"], wts["ffn1_out"], tm)
    w_in_t, w_f_t, w_o = wts["w_in_t"], wts["w_f_t"], wts["w_o"]
    raw_a, nrm_a, h2a = _proj(x1, gm, w_in_t, gains_a, (True, True, False), tn=C, w_off=0, slabs=True, tm=n16,
                              normed_dtype=F32, name="proj_a")
    raw_b, nrm_b, h2b = _proj(x1, gm, w_in_t, gains_b, (True, True, False), tn=C, w_off=3, slabs=False, tm=tm,
                              normed_dtype=BF16, name="proj_b")
    f_raw, _, _ = _proj(x1, gm, w_f_t, gains_b[:1], (False,), tn=HEAD, w_off=0, slabs=False, tm=tm,
                        normed_dtype=BF16, name="proj_f")
    c = _fox_gate_fwd(f_raw, fbias, "fox_gate_fwd")
    c_heads = c[:, :H].T
    c_col, c_row = c_heads[:, :, None], c_heads[:, None, :]
    out_a, lse_a = hosted("dil_fwd", _dil_fwd, nrm_a, table)
    out_b, lse_b = hosted("fox_fwd", _fox_fwd, nrm_b, c_col, c_row, tq)
    x2a = _mm(out_a, w_o, nt=False, tk=C, tm=n16, a_layout="slab", out_layout="view", resid=x1, name="out_a")
    x2 = _mm(out_b, w_o, nt=False, tk=C, tm=tm, a_layout="nat", out_layout="nat", resid=x2a, w_off=1, name="out_b")
    y, h3, gu3, act3 = _ffn_fwd(x2, g2, wts["ffn2_in"], wts["ffn2_out"], tm, "ffn2_fwd")
    dy, loss_row = _loss_grad(y, target, tm, "loss_grad")

    def ffn_backward(tag, xin, g, h, gu, act, win, wout, dres):
        nc, tf = wout.shape[0], wout.shape[1]
        dh, dgu, dyb = hosted(tag + "_bwd", _ffn_bwd, dres, gu, win, wout, tm)
        grads[tag + "_w_in_t"] = hosted(tag + "_dwin", _mm_tn, dgu.reshape(2 * nc, T, tf), h, bm=tf, bn=bn, bt=T)
        dxin, grads[tag + "_norm"] = hosted(tag + "_rms_bwd", _rms_bwd, xin, g, dh, dres, tm)
        grads[tag + "_w_out"] = hosted(tag + "_dwout", _mm_tn, act, dyb, bm=tf, bn=bn, bt=T)
        return dxin

    dx2 = ffn_backward("ffn2", x2, g2, h3, gu3, act3, wts["ffn2_in"], wts["ffn2_out"], dy)

    dmix_a = _mm(dx2, w_o, nt=True, tk=D, tm=n16, a_layout="view", out_layout="slab", n_out=C, name="dmix_a")
    dmix_b = _mm(dx2, w_o, nt=True, tk=D, tm=tm, a_layout="nat", out_layout="nat", n_out=C, w_off=1, name="dmix_b")
    dwo = _mm_tn(out_a.reshape(1, T, C), dx2.reshape(n16, NSLAB * D), bm=C, bn=bn, bt=n16, b_slabs=True,
                 rows=2 * C, name="dwo_a")
    dwo = _mm_tn(out_b.reshape(1, T, C), dx2, bm=C, bn=bn, bt=tm, rows=2 * C, m_off=1, into=dwo, name="dwo_b")
    grads["w_out"] = dwo[0]

    dqa, dka, dva, dtab = hosted("dil_bwd", _dil_bwd, nrm_a, table, out_a, dmix_a, lse_a)
    dqb, dkb, dvb, dck, dcq = hosted("fox_bwd", _fox_bwd, nrm_b, c_col, c_row, out_b, dmix_b, lse_b, tq)
    grads["rel_bias_table"] = dtab[:, 0, :NUM_BUCKETS].T
    dc = _pad_lanes((dck[:, 0, :] + dcq[:, :, 0]).T)
    df, dfb = _fox_gate_bwd(f_raw, fbias, dc, "fox_gate_bwd")
    grads["forget_bias"] = dfb[:, :H]

    flat = lambda a: a.reshape(T, a.shape[-1])
    dproj_a, dgain_a = _qknorm_bwd(flat(raw_a), flat(dqa), flat(dka), flat(dva), gains_a[:2], min(256, T), "qknorm_bwd_a")
    dproj_b, dgain_b = _qknorm_bwd(raw_b, dqb, dkb, dvb, gains_b[:2], min(256, T), "qknorm_bwd_b")
    grads["q_norm_a"], grads["k_norm_a"] = dgain_a[0], dgain_a[1]
    grads["q_norm_b"], grads["k_norm_b"] = dgain_b[0], dgain_b[1]
    dproj_a = dproj_a.reshape(NSLAB, n16, 3 * C)

    dh2 = _mm(dproj_a, w_in_t, nt=False, tk=C, tm=n16, a_layout="slab", out_layout="view", name="dh2_a")
    dh2 = _mm(dproj_b, w_in_t, nt=False, tk=C, tm=tm, a_layout="nat", out_layout="nat", resid=dh2, w_off=3, name="dh2_b")
    dh2 = _mm(df, w_f_t, nt=False, tk=HEAD, tm=tm, a_layout="nat", out_layout="nat", resid=dh2, name="dh2_f")
    dx1, grads["mix_norm"] = _rms_bwd(x1, gm, dh2, dx2, tm, "mix_rms_bwd")
    bt = min(2048, T)
    dwt = _mm_tn(flat(dproj_a)[None], flat(h2a), bm=C, bn=bn, bt=bt, rows=6 * C + H, name="dw_a")
    dwt = _mm_tn(dproj_b[None], h2b, bm=C, bn=bn, bt=bt, rows=6 * C + H, m_off=3, into=dwt, name="dw_b")
    dwt = _mm_tn(df[None, :, :H], h2b, bm=H, bn=bn, bt=bt, rows=6 * C + H, m_off=6 * C // H, into=dwt, name="dw_f")
    grads["w_in_t"] = dwt[0]

    grad_x = ffn_backward("ffn1", x, g1, h1, gu1, act1, wts["ffn1_in"], wts["ffn1_out"], dx1)
    return loss_row, grad_x, grads


def _place():
    x, y, c = lax.axis_index("x"), lax.axis_index("y"), lax.axis_index("c")
    other_chips = [(1 - x, y), (x, 1 - y), (1 - x, 1 - y)]
    return x, y, c, other_chips


def _run_side(side, name):
    def body(*refs):
        si, so = len(side.ins), len(side.outs)
        side.start(refs[:si], refs[si:si + so], refs[si + so:])
        side.finish(refs[:si], refs[si:si + so], refs[si + so:])

    return pl.pallas_call(body, name=name, in_specs=[ANY] * len(side.ins), out_specs=[ANY] * len(side.outs),
                          out_shape=side.outs, scratch_shapes=side.sems)(*side.ins)


def _all_gather(shards):
    n = len(shards)

    def plan(ins, outs, sems):
        send_sems, recv_sems, local_sems = sems
        x, y, c, chips = _place()
        me, sibling = (x, y, c), (x, y, 1 - c)

        def copy(a, k, block, to, src=None):
            px, py, pc = block
            dst = outs[a].at[4 * px + 2 * py + pc]
            return pltpu.make_async_remote_copy(
                src_ref=dst if src is None else src, dst_ref=dst, send_sem=send_sems.at[7 * a + k],
                recv_sem=recv_sems.at[7 * a + k], device_id=to, device_id_type=MESH)

        mine = [pltpu.make_async_copy(ins[a], outs[a].at[4 * x + 2 * y + c], local_sems.at[a]) for a in range(n)]
        first = []
        for a in range(n):
            first.append(copy(a, 0, me, sibling, src=ins[a]))
            first += [copy(a, 1 + j, me, (*chip, c), src=ins[a]) for j, chip in enumerate(chips)]
        return copy, mine, first, me, sibling, c, chips

    def start(ins, outs, sems):
        _, mine, first, *_ = plan(ins, outs, sems)
        for cp in mine + first:
            cp.start()

    def finish(ins, outs, sems):
        copy, mine, first, me, sibling, c, chips = plan(ins, outs, sems)
        passed = []
        for a in range(n):
            for j, chip in enumerate(chips):
                copy(a, 1 + j, (*chip, c), me).wait_recv()
                fwd = copy(a, 4 + j, (*chip, c), sibling)
                fwd.start()
                passed.append(fwd)
        for a in range(n):
            copy(a, 0, sibling, me).wait_recv()
            for j, chip in enumerate(chips):
                copy(a, 4 + j, (*chip, 1 - c), me).wait_recv()
        for cp in first + passed:
            cp.wait_send()
        for cp in mine:
            cp.wait()

    return _Side(shards, [_sds((N_DEV,) + s.shape, s.dtype) for s in shards],
                 [pltpu.SemaphoreType.DMA((7 * n,)), pltpu.SemaphoreType.DMA((7 * n,)), pltpu.SemaphoreType.DMA((n,))],
                 start, finish)


def _exchange_in_chip(gs):
    n = len(gs)

    def copies(ins, outs, sems):
        x, y, c, _ = _place()
        return [pltpu.make_async_remote_copy(
            src_ref=ins[a].at[2 * q + 1 - c], dst_ref=outs[a].at[q], send_sem=sems[0].at[4 * a + q],
            recv_sem=sems[1].at[4 * a + q], device_id=(x, y, 1 - c), device_id_type=MESH)
            for a in range(n) for q in range(4)]

    def start(ins, outs, sems):
        for cp in copies(ins, outs, sems):
            cp.start()

    def finish(ins, outs, sems):
        for cp in copies(ins, outs, sems):
            cp.wait()

    return _Side(gs, [_sds((4,) + g.shape[1:], g.dtype) for g in gs],
                 [pltpu.SemaphoreType.DMA((4 * n,)), pltpu.SemaphoreType.DMA((4 * n,))], start, finish)


def _exchange_between_chips(ps):
    n = len(ps)

    def copies(ins, outs, sems):
        x, y, c, chips = _place()
        return [pltpu.make_async_remote_copy(
            src_ref=ins[a].at[2 * cx + cy], dst_ref=outs[a].at[j], send_sem=sems[0].at[3 * a + j],
            recv_sem=sems[1].at[3 * a + j], device_id=(cx, cy, c), device_id_type=MESH)
            for a in range(n) for j, (cx, cy) in enumerate(chips)]

    def start(ins, outs, sems):
        for cp in copies(ins, outs, sems):
            cp.start()

    def finish(ins, outs, sems):
        for cp in copies(ins, outs, sems):
            cp.wait()

    return _Side(ps, [_sds((3,) + p.shape[1:], p.dtype) for p in ps],
                 [pltpu.SemaphoreType.DMA((3 * n,)), pltpu.SemaphoreType.DMA((3 * n,))], start, finish)


def _all_reduce_small(v, name):
    R = v.shape[0]

    def body(v_ref, sum_ref, all_ref, send_sems, recv_sems):
        x, y, c, _ = _place()
        k = 4 * x + 2 * y + c
        all_ref[k] = v_ref[...]
        copies = []
        for rel in range(1, N_DEV):
            fx, fy, fc = (rel >> 2) & 1, (rel >> 1) & 1, rel & 1
            peer = (1 - x if fx else x, 1 - y if fy else y, 1 - c if fc else c)
            copies.append(pltpu.make_async_remote_copy(
                src_ref=v_ref, dst_ref=all_ref.at[k], send_sem=send_sems.at[rel - 1], recv_sem=recv_sems.at[rel - 1],
                device_id=peer, device_id_type=MESH))
        for cp in copies:
            cp.start()
        for rel in range(1, N_DEV):
            fx, fy, fc = (rel >> 2) & 1, (rel >> 1) & 1, rel & 1
            src = 4 * (1 - x if fx else x) + 2 * (1 - y if fy else y) + (1 - c if fc else c)
            pltpu.make_async_remote_copy(
                src_ref=v_ref, dst_ref=all_ref.at[src], send_sem=send_sems.at[rel - 1], recv_sem=recv_sems.at[rel - 1],
                device_id=(x, y, c), device_id_type=MESH).wait_recv()
        for cp in copies:
            cp.wait_send()
        tot = all_ref[0]
        for d in range(1, N_DEV):
            tot = tot + all_ref[d]
        sum_ref[...] = tot

    vm = pl.BlockSpec(memory_space=pltpu.VMEM)
    return pl.pallas_call(
        body, name=name, in_specs=[vm], out_specs=[vm, vm],
        out_shape=[_sds((R, HEAD), F32), _sds((N_DEV, R, HEAD), F32)],
        scratch_shapes=[pltpu.SemaphoreType.DMA((N_DEV - 1,)), pltpu.SemaphoreType.DMA((N_DEV - 1,))],
    )(v)[0]


def _tiles(rows, cols):
    tr = next((cand for cand in (688, 512, 256) if rows % cand == 0), rows)
    tc = 512 if (cols % 512 == 0 and tr * cols * 4 > (2 << 20)) else cols
    return tr, tc


def _chip_sum(g, r1, core, name):
    _, R, Cc = g.shape
    tr, tc = _tiles(R, Cc)

    def body(core_ref, g_ref, r_ref, p_ref):
        p_ref[...] = (g_ref[...] + r_ref[...]).astype(BF16)

    blk = lambda f: pl.BlockSpec((None, tr, tc), f)
    return pl.pallas_call(
        body, name=name,
        grid_spec=pltpu.PrefetchScalarGridSpec(
            num_scalar_prefetch=1, grid=(4, R // tr, Cc // tc),
            in_specs=[blk(lambda q, i, j, core: (2 * q + core[0], i, j)), blk(lambda q, i, j, core: (q, i, j))],
            out_specs=blk(lambda q, i, j, core: (q, i, j))),
        out_shape=_sds((4, R, Cc), BF16), compiler_params=_cparams(),
    )(core, g, r1)


def _adamw_update(gv, w_ref, m_ref, v_ref, d_ref, nm_ref, nv_ref):
    nm = B1 * m_ref[...] + (1.0 - B1) * gv
    nv = B2 * v_ref[...] + (1.0 - B2) * jnp.square(gv)
    m_hat = nm / (1.0 - B1 ** STEP)
    v_hat = nv / (1.0 - B2 ** STEP)
    d_ref[...] = -LR * (m_hat / (jnp.sqrt(v_hat) + EPS) + WD * w_ref[...])
    nm_ref[...] = nm
    nv_ref[...] = nv


def _reduce_adamw(g, r1, r2, where, w, m, v, name):
    _, R, Cc = g.shape
    tr, tc = _tiles(R, Cc)

    def body(where_ref, g_ref, r1_ref, r2_ref, w_ref, m_ref, v_ref, o_ref, d_ref, nm_ref, nv_ref):
        gv = ((g_ref[...] + r1_ref[...]) + r2_ref[0].astype(F32)) + (r2_ref[1].astype(F32) + r2_ref[2].astype(F32))
        o_ref[...] = gv
        _adamw_update(gv, w_ref, m_ref, v_ref, d_ref, nm_ref, nv_ref)

    blk = pl.BlockSpec((tr, tc), lambda i, j, w: (i, j))
    return pl.pallas_call(
        body, name=name,
        grid_spec=pltpu.PrefetchScalarGridSpec(
            num_scalar_prefetch=1, grid=(R // tr, Cc // tc),
            in_specs=[pl.BlockSpec((None, tr, tc), lambda i, j, w: (w[0], i, j)),
                      pl.BlockSpec((None, tr, tc), lambda i, j, w: (w[1], i, j)),
                      pl.BlockSpec((3, tr, tc), lambda i, j, w: (0, i, j)), blk, blk, blk],
            out_specs=[blk] * 4),
        out_shape=[_sds((R, Cc), F32)] * 4, compiler_params=_cparams(),
    )(where, g, r1, r2, w, m, v)


def _adamw(w, g, m, v, name):
    R, Cc = w.shape
    tr, tc = _tiles(R, Cc)

    def body(w_ref, g_ref, m_ref, v_ref, d_ref, nm_ref, nv_ref):
        _adamw_update(g_ref[...], w_ref, m_ref, v_ref, d_ref, nm_ref, nv_ref)

    blk = pl.BlockSpec((tr, tc), lambda i, j: (i, j))
    return pl.pallas_call(
        body, name=name, grid=(R // tr, Cc // tc), in_specs=[blk] * 4, out_specs=[blk] * 3,
        out_shape=[_sds((R, Cc), F32)] * 3, compiler_params=_cparams(),
    )(w, g, m, v)


SMALL = ("ffn1_norm", "mix_norm", "ffn2_norm", "q_norm_a", "k_norm_a", "q_norm_b", "k_norm_b", "forget_bias",
         "rel_bias_table")
LARGE = ("ffn1_w_in", "ffn1_w_out", "w_in", "w_out", "ffn2_w_in", "ffn2_w_out")
ORDER = ("ffn1_norm", "ffn1_w_in", "ffn1_w_out", "mix_norm", "w_in", "q_norm_a", "k_norm_a", "q_norm_b", "k_norm_b",
         "forget_bias", "rel_bias_table", "w_out", "ffn2_norm", "ffn2_w_in", "ffn2_w_out")


def _pack_small(vals):
    rows = []
    for name in SMALL:
        flat = vals[name].reshape(-1)
        pad = (-flat.shape[0]) % HEAD
        rows.append(jnp.pad(flat, (0, pad)).reshape(-1, HEAD))
    return jnp.concatenate(rows, axis=0)


def _unpack_small(packed, like):
    out, r = {}, 0
    for name in SMALL:
        size = like[name].size
        nrow = -(-size // HEAD)
        out[name] = packed[r:r + nrow].reshape(-1)[:size].reshape(like[name].shape)
        r += nrow
    return out


def kernel(x, ffn1_norm, ffn1_w_in, ffn1_w_out, mix_norm, w_in, q_norm_a, k_norm_a, q_norm_b, k_norm_b, forget_bias, rel_bias_table, w_out, ffn2_norm, ffn2_w_in, ffn2_w_out, loss_target, m_ffn1_norm, m_ffn1_w_in, m_ffn1_w_out, m_mix_norm, m_w_in, m_q_norm_a, m_k_norm_a, m_q_norm_b, m_k_norm_b, m_forget_bias, m_rel_bias_table, m_w_out, m_ffn2_norm, m_ffn2_w_in, m_ffn2_w_out, v_ffn1_norm, v_ffn1_w_in, v_ffn1_w_out, v_mix_norm, v_w_in, v_q_norm_a, v_k_norm_a, v_q_norm_b, v_k_norm_b, v_forget_bias, v_rel_bias_table, v_w_out, v_ffn2_norm, v_ffn2_w_in, v_ffn2_w_out):
    w = dict(ffn1_norm=ffn1_norm, ffn1_w_in=ffn1_w_in, ffn1_w_out=ffn1_w_out, mix_norm=mix_norm, w_in=w_in,
             q_norm_a=q_norm_a, k_norm_a=k_norm_a, q_norm_b=q_norm_b, k_norm_b=k_norm_b, forget_bias=forget_bias,
             rel_bias_table=rel_bias_table, w_out=w_out, ffn2_norm=ffn2_norm, ffn2_w_in=ffn2_w_in, ffn2_w_out=ffn2_w_out)
    m = dict(ffn1_norm=m_ffn1_norm, ffn1_w_in=m_ffn1_w_in, ffn1_w_out=m_ffn1_w_out, mix_norm=m_mix_norm, w_in=m_w_in,
             q_norm_a=m_q_norm_a, k_norm_a=m_k_norm_a, q_norm_b=m_q_norm_b, k_norm_b=m_k_norm_b,
             forget_bias=m_forget_bias, rel_bias_table=m_rel_bias_table, w_out=m_w_out, ffn2_norm=m_ffn2_norm,
             ffn2_w_in=m_ffn2_w_in, ffn2_w_out=m_ffn2_w_out)
    v = dict(ffn1_norm=v_ffn1_norm, ffn1_w_in=v_ffn1_w_in, ffn1_w_out=v_ffn1_w_out, mix_norm=v_mix_norm, w_in=v_w_in,
             q_norm_a=v_q_norm_a, k_norm_a=v_k_norm_a, q_norm_b=v_q_norm_b, k_norm_b=v_k_norm_b,
             forget_bias=v_forget_bias, rel_bias_table=v_rel_bias_table, w_out=v_w_out, ffn2_norm=v_ffn2_norm,
             ffn2_w_in=v_ffn2_w_in, ffn2_w_out=v_ffn2_w_out)
    T, D = x.shape[1], x.shape[2]
    C = D // 2
    H = C // HEAD
    ff_shard = ffn1_w_out.shape[1]

    f1i, f1o = _run_side(_all_gather([ffn1_w_in[0].T.astype(BF16), ffn1_w_out[0].astype(BF16)]), "gather_ffn1")
    wts = dict(ffn1_in=f1i.reshape(2, N_DEV, ff_shard, D), ffn1_out=f1o)
    xi, yi, ci = lax.axis_index("x"), lax.axis_index("y"), lax.axis_index("c")
    core = jnp.reshape(ci, (1,)).astype(jnp.int32)
    where = jnp.stack([4 * xi + 2 * yi + ci, 2 * xi + yi]).astype(jnp.int32)
    gs, r1, ps, r2 = {}, {}, {}, {}

    def by_destination(name, grads):
        gs[name] = grads[name + "_t" if name.endswith("w_in") else name].reshape(N_DEV, -1, D)
        return gs[name]

    def chip_sums(names):
        for name in names:
            ps[name] = _chip_sum(gs[name], r1[name], core, "chip_sum_" + name)
        return [ps[name] for name in names]

    class Plan:
        carried = {"ffn1_fwd": ("gather", ("w_in", "w_out")),
                   "dil_fwd": ("gather", ("ffn2_w_out",)), "fox_fwd": ("gather", ("ffn2_w_in",)),
                   "dil_bwd": ("in_chip", ("ffn2_w_in", "ffn2_w_out")), "fox_bwd": ("between", ("ffn2_w_in", "ffn2_w_out")),
                   "ffn1_bwd": ("in_chip", ("w_in", "w_out")), "ffn1_dwin": ("between", ("w_in", "w_out")),
                   "ffn1_rms_bwd": ("in_chip", ("ffn1_w_in",)), "ffn1_dwout": ("between", ("ffn1_w_in",))}

        def before(self, host, wts, grads):
            if host not in self.carried:
                return None
            kind, names = self.carried[host]
            if kind == "gather":
                return _all_gather([(w[n][0].T if n.endswith("w_in") else w[n][0]).astype(BF16) for n in names])
            if kind == "in_chip":
                return _exchange_in_chip([by_destination(n, grads) for n in names])
            return _exchange_between_chips(chip_sums(names))

        def after(self, host, res, wts, grads):
            kind, names = self.carried[host]
            if host == "ffn1_fwd":
                w_in_t = res[0].reshape(-1, D)
                wts.update(w_in_t=w_in_t, w_f_t=jnp.pad(w_in_t[6 * C:], ((0, HEAD - H), (0, 0))),
                           w_o=res[1].reshape(2 * C, D))
            elif host == "dil_fwd":
                wts.update(ffn2_out=res[0])
            elif host == "fox_fwd":
                wts.update(ffn2_in=res[0].reshape(2, N_DEV, ff_shard, D))
            else:
                (r1 if kind == "in_chip" else r2).update(zip(names, res))

    small = {name: w[name] for name in SMALL}
    loss_row, grad_x, grads = _local_step(x[0], loss_target[0], small, wts, Plan())

    tail = ("ffn1_w_out",)
    r1[tail[0]], = _run_side(_exchange_in_chip([by_destination(tail[0], grads)]), "reduce_in_chip_tail")
    r2.update(zip(tail, _run_side(_exchange_between_chips(chip_sums(tail)), "reduce_between_chips_tail")))

    packed = _pack_small(grads)
    nsmall = packed.shape[0]
    packed = jnp.concatenate([packed, loss_row, jnp.zeros(((-nsmall - 1) % 8, HEAD), F32)], axis=0)
    reduced = _all_reduce_small(packed, "reduce_small")
    loss = reduced[nsmall, 0]
    g_small = _unpack_small(reduced[:nsmall], small)

    grad, delta, new_m, new_v = dict(g_small), {}, {}, {}
    for name in LARGE:
        to = (lambda t: t[0].T) if name.endswith("w_in") else (lambda t: t[0])
        back = (lambda t: t.T[None]) if name.endswith("w_in") else (lambda t: t[None])
        res = _reduce_adamw(gs[name], r1[name], r2[name], where, to(w[name]), to(m[name]), to(v[name]), "adamw_" + name)
        grad[name], delta[name], new_m[name], new_v[name] = (back(t) for t in res)
    d, nm, nv = _adamw(_pack_small(w), reduced[:nsmall], _pack_small(m), _pack_small(v), "adamw_small")
    delta.update(_unpack_small(d, small))
    new_m.update(_unpack_small(nm, small))
    new_v.update(_unpack_small(nv, small))
    return (loss, grad_x[None], *[grad[n] for n in ORDER], *[delta[n] for n in ORDER],
            *[new_m[n] for n in ORDER], *[new_v[n] for n in ORDER])
```

```python
import functools
import math

import numpy as np
import jax
import jax.numpy as jnp
from jax import lax
from jax.experimental import pallas as pl
from jax.experimental.pallas import tpu as pltpu

F32, BF16 = jnp.float32, jnp.bfloat16
HEAD = 128
NSLAB = 16
BLK = 128
DILATIONS = (1, 4, 16)
NUM_BUCKETS, MAX_DISTANCE = 32, 2048
RMS_EPS = 1e-6
NEG = -1e30
SCALE = HEAD ** -0.5
LR, B1, B2, EPS, WD, STEP = 0.001, 0.9, 0.999, 1e-08, 0.01, 10
N_DEV = 8
VMEM_LIMIT_BYTES = 56 << 20
MESH = pl.DeviceIdType.MESH


def _cparams(**kw):
    return pltpu.CompilerParams(vmem_limit_bytes=VMEM_LIMIT_BYTES, **kw)


def _nn(a, b):
    return jnp.dot(a, b, preferred_element_type=F32)


def _nt(a, b):
    return lax.dot_general(a, b, (((1,), (1,)), ((), ())), preferred_element_type=F32)


def _tn(a, b):
    return lax.dot_general(a, b, (((0,), (0,)), ((), ())), preferred_element_type=F32)


def _sds(shape, dtype):
    return jax.ShapeDtypeStruct(shape, dtype)


ANY = pl.BlockSpec(memory_space=pl.ANY)


class _Side:
    def __init__(self, ins, outs, sems, start, finish):
        self.ins, self.outs, self.sems, self.start, self.finish = list(ins), list(outs), list(sems), start, finish


def _call(body, *, name, grid, in_specs, out_specs, out_shape, args, scratch_shapes=(), side=None):
    in_specs, out_specs, out_shape = list(in_specs), list(out_specs), list(out_shape)
    scratch_shapes = list(scratch_shapes)
    if side is None:
        return pl.pallas_call(body, name=name, grid=grid, in_specs=in_specs, out_specs=out_specs, out_shape=out_shape,
                              scratch_shapes=scratch_shapes, compiler_params=_cparams())(*args)
    ni, no, ns = len(args), len(out_shape), len(scratch_shapes)
    si, so = len(side.ins), len(side.outs)

    def fused(*refs):
        h_in, s_in = refs[:ni], refs[ni:ni + si]
        h_out, s_out = refs[ni + si:ni + si + no], refs[ni + si + no:ni + si + no + so]
        h_scr, s_sem = refs[ni + si + no + so:ni + si + no + so + ns], refs[ni + si + no + so + ns:]
        ids = [pl.program_id(k) for k in range(len(grid))]
        first = functools.reduce(jnp.logical_and, [i == 0 for i in ids])
        last = functools.reduce(jnp.logical_and, [i == n - 1 for i, n in zip(ids, grid)])

        @pl.when(first)
        def _():
            side.start(s_in, s_out, s_sem)

        body(*h_in, *h_out, *h_scr)

        @pl.when(last)
        def _():
            side.finish(s_in, s_out, s_sem)

    res = pl.pallas_call(
        fused, name=name, grid=grid, in_specs=in_specs + [ANY] * si, out_specs=out_specs + [ANY] * so,
        out_shape=out_shape + side.outs, scratch_shapes=scratch_shapes + side.sems, compiler_params=_cparams(),
    )(*args, *side.ins)
    return list(res[:no]), list(res[no:])


def _ffn_fwd(x, g, win, wout, tm, name, side=None):
    T, D = x.shape
    nc, tf = wout.shape[0], wout.shape[1]

    def body(x_ref, g_ref, win_ref, wout_ref, y_ref, h_ref, gu_ref, act_ref):
        j = pl.program_id(1)

        @pl.when(j == 0)
        def _():
            xv = x_ref[...]
            r = lax.rsqrt(jnp.mean(xv * xv, axis=-1, keepdims=True) + RMS_EPS)
            h_ref[...] = (xv * r * g_ref[...]).astype(BF16)
            y_ref[...] = jnp.zeros_like(y_ref)

        hb = h_ref[...]
        gt = _nt(win_ref[0], hb)
        up = _nt(win_ref[1], hb)
        gu_ref[0] = gt.astype(BF16)
        gu_ref[1] = up.astype(BF16)
        act = (gt * jax.nn.sigmoid(gt) * up).astype(BF16)
        act_ref[...] = act
        y_ref[...] += _tn(act, wout_ref[...])

        @pl.when(j == nc - 1)
        def _():
            y_ref[...] = x_ref[...] + 0.5 * y_ref[...]

    return _call(
        body, name=name, grid=(T // tm, nc), side=side,
        in_specs=[pl.BlockSpec((tm, D), lambda i, j: (i, 0)),
                  pl.BlockSpec((1, D), lambda i, j: (0, 0)),
                  pl.BlockSpec((2, None, tf, D), lambda i, j: (0, j, 0, 0)),
                  pl.BlockSpec((None, tf, D), lambda i, j: (j, 0, 0))],
        out_specs=[pl.BlockSpec((tm, D), lambda i, j: (i, 0)),
                   pl.BlockSpec((tm, D), lambda i, j: (i, 0)),
                   pl.BlockSpec((2, None, tf, tm), lambda i, j: (0, j, 0, i)),
                   pl.BlockSpec((None, tf, tm), lambda i, j: (j, 0, i))],
        out_shape=[_sds((T, D), F32), _sds((T, D), BF16), _sds((2, nc, tf, T), BF16), _sds((nc, tf, T), BF16)],
        args=(x, g, win, wout))


def _ffn_bwd(dy, gu, win, wout, tm, name, side=None):
    T, D = dy.shape
    nc, tf = wout.shape[0], wout.shape[1]

    def body(dy_ref, gu_ref, win_ref, wout_ref, dh_ref, dgu_ref, dyb_ref):
        j = pl.program_id(1)

        @pl.when(j == 0)
        def _():
            dh_ref[...] = jnp.zeros_like(dh_ref)
            dyb_ref[...] = (0.5 * dy_ref[...]).astype(BF16)

        dact = _nt(wout_ref[...], dyb_ref[...])
        gt = gu_ref[0].astype(F32)
        up = gu_ref[1].astype(F32)
        s = jax.nn.sigmoid(gt)
        dgb = (dact * up * (s * (1.0 + gt * (1.0 - s)))).astype(BF16)
        dub = (dact * (gt * s)).astype(BF16)
        dgu_ref[0] = dgb
        dgu_ref[1] = dub
        dh_ref[...] += _tn(dgb, win_ref[0]) + _tn(dub, win_ref[1])

    return _call(
        body, name=name, grid=(T // tm, nc), side=side,
        in_specs=[pl.BlockSpec((tm, D), lambda i, j: (i, 0)),
                  pl.BlockSpec((2, None, tf, tm), lambda i, j: (0, j, 0, i)),
                  pl.BlockSpec((2, None, tf, D), lambda i, j: (0, j, 0, 0)),
                  pl.BlockSpec((None, tf, D), lambda i, j: (j, 0, 0))],
        out_specs=[pl.BlockSpec((tm, D), lambda i, j: (i, 0)),
                   pl.BlockSpec((2, None, tf, tm), lambda i, j: (0, j, 0, i)),
                   pl.BlockSpec((tm, D), lambda i, j: (i, 0))],
        out_shape=[_sds((T, D), F32), _sds((2, nc, tf, T), BF16), _sds((T, D), BF16)],
        args=(dy, gu, win, wout))


def _rms_bwd(x, g, dh, dres, tm, name, side=None):
    T, D = x.shape

    def body(x_ref, g_ref, dh_ref, dres_ref, dx_ref, dg_ref):
        @pl.when(pl.program_id(0) == 0)
        def _():
            dg_ref[...] = jnp.zeros_like(dg_ref)

        xv = x_ref[...]
        r = lax.rsqrt(jnp.mean(xv * xv, axis=-1, keepdims=True) + RMS_EPS)
        xhat = xv * r
        dh = dh_ref[...]
        gd = dh * g_ref[...]
        dx_ref[...] = dres_ref[...] + r * (gd - xhat * jnp.mean(gd * xhat, axis=-1, keepdims=True))
        dg_ref[...] += jnp.sum(dh * xhat, axis=0, keepdims=True)

    row = pl.BlockSpec((tm, D), lambda i: (i, 0))
    one = pl.BlockSpec((1, D), lambda i: (0, 0))
    return _call(body, name=name, grid=(T // tm,), side=side, in_specs=[row, one, row, row], out_specs=[row, one],
                 out_shape=[_sds((T, D), F32), _sds((1, D), F32)], args=(x, g, dh, dres))


def _mm_tn(a, b, *, bm, bn, bt, name, b_slabs=False, side=None, rows=None, m_off=0, into=None, a_rows=False):
    nz, T, M = (a.shape[0], a.shape[2], a.shape[1]) if a_rows else a.shape
    if b_slabs:
        N = b.shape[1] // NSLAB
        assert bt == T // NSLAB
        b_spec = pl.BlockSpec((bt, bn), lambda n, z, m, t: (0, t * (N // bn) + n))
    else:
        N = b.shape[1]
        b_spec = pl.BlockSpec((bt, bn), lambda n, z, m, t: (t, n))
    assert M % bm == 0 and N % bn == 0 and T % bt == 0, (M, bm, N, bn, T, bt)

    def body(a_ref, b_ref, *rest):
        c_ref = rest[-1]

        @pl.when(pl.program_id(3) == 0)
        def _():
            c_ref[...] = jnp.zeros_like(c_ref)

        ab, bb = a_ref[...].astype(BF16), b_ref[...].astype(BF16)
        c_ref[...] += _nn(ab, bb) if a_rows else _tn(ab, bb)

    grid = (N // bn, nz, M // bm, T // bt)
    a_spec = (pl.BlockSpec((None, bm, bt), lambda n, z, m, t: (z, m, t)) if a_rows
              else pl.BlockSpec((None, bt, bm), lambda n, z, m, t: (z, t, m)))
    in_specs = [a_spec, b_spec]
    out_spec = pl.BlockSpec((None, bm, bn), lambda n, z, m, t: (z, m + m_off, n))
    out_shape = _sds((nz, M if rows is None else rows, N), F32)
    if into is not None:
        assert side is None and into.shape == out_shape.shape
        return pl.pallas_call(body, name=name, grid=grid, in_specs=in_specs + [ANY], out_specs=out_spec,
                              out_shape=out_shape, input_output_aliases={2: 0}, compiler_params=_cparams())(a, b, into)
    res = _call(body, name=name, grid=grid, side=side, in_specs=in_specs, out_specs=[out_spec],
                out_shape=[out_shape], args=(a, b))
    return res[0] if side is None else (res[0][0], res[1])


def _proj(x, g, wt, gains, modes, *, tn, w_off, slabs, tm, normed_dtype, name):
    T, D = x.shape
    ntile = len(modes)
    N = ntile * tn
    n16 = T // NSLAB
    if slabs:
        assert tm == n16
        x_in = x.reshape(n16, NSLAB * D)
        x_spec = pl.BlockSpec((tm, D), lambda i, n: (0, i))
        oshape = lambda c: (NSLAB, n16, c)
        ospec = lambda bc, cm: pl.BlockSpec((None, tm, bc), lambda i, n: (i, 0, cm(n)))
    else:
        x_in = x
        x_spec = pl.BlockSpec((tm, D), lambda i, n: (i, 0))
        oshape = lambda c: (T, c)
        ospec = lambda bc, cm: pl.BlockSpec((tm, bc), lambda i, n: (i, cm(n)))

    def body(x_ref, g_ref, w_ref, gains_ref, raw_ref, nrm_ref, h_ref):
        n = pl.program_id(1)

        @pl.when(n == 0)
        def _():
            xv = x_ref[...]
            r = lax.rsqrt(jnp.mean(xv * xv, axis=-1, keepdims=True) + RMS_EPS)
            h_ref[...] = (xv * r * g_ref[...]).astype(BF16)

        y = _nt(h_ref[...], w_ref[...])
        raw_ref[...] = y
        for t, mode in enumerate(modes):
            @pl.when(n == t)
            def _(t=t, mode=mode):
                if not mode:
                    nrm_ref[...] = y.astype(nrm_ref.dtype)
                    return
                gain = gains_ref[t]
                for k in range(tn // HEAD):
                    yk = y[:, k * HEAD:(k + 1) * HEAD]
                    r = lax.rsqrt(jnp.mean(yk * yk, axis=-1, keepdims=True) + RMS_EPS)
                    nrm_ref[:, k * HEAD:(k + 1) * HEAD] = (yk * r * gain).astype(nrm_ref.dtype)

    return pl.pallas_call(
        body, name=name, grid=(T // tm, ntile),
        in_specs=[x_spec, pl.BlockSpec((1, D), lambda i, n: (0, 0)),
                  pl.BlockSpec((tn, D), lambda i, n: (n + w_off, 0)),
                  pl.BlockSpec((ntile, 1, HEAD), lambda i, n: (0, 0, 0))],
        out_specs=[ospec(tn, lambda n: n), ospec(tn, lambda n: n), ospec(D, lambda n: 0)],
        out_shape=[_sds(oshape(N), F32), _sds(oshape(N), normed_dtype), _sds(oshape(D), BF16)],
        compiler_params=_cparams(),
    )(x_in, g, wt, gains)


def _mm(a, w, *, nt, tk, tm, a_layout, out_layout, resid=None, name, w_off=0, n_out=None):
    if a_layout == "slab":
        T, K = a.shape[0] * a.shape[1], a.shape[2]
    else:
        T, K = a.shape
    N = (w.shape[0] if nt else w.shape[1]) if n_out is None else n_out
    n16 = T // NSLAB
    nk = K // tk

    def spec(layout, C, bc, colmap):
        if layout == "nat":
            return pl.BlockSpec((tm, bc), lambda i, k: (i, colmap(k)))
        assert tm == n16
        if layout == "slab":
            return pl.BlockSpec((None, tm, bc), lambda i, k: (i, 0, colmap(k)))
        assert bc == C
        return pl.BlockSpec((tm, C), lambda i, k: (0, i))

    a_in = a.reshape(n16, NSLAB * K) if a_layout == "view" else a
    w_spec = (pl.BlockSpec((N, tk), lambda i, k: (w_off, k)) if nt
              else pl.BlockSpec((tk, N), lambda i, k: (k + w_off, 0)))
    o_spec = spec(out_layout, N, N, lambda k: 0)
    oshape = {"nat": (T, N), "slab": (NSLAB, n16, N), "view": (n16, NSLAB * N)}[out_layout]
    has_resid = resid is not None

    def body(*refs):
        a_ref, w_ref = refs[0], refs[1]
        o_ref = refs[-1]
        k = pl.program_id(1)

        @pl.when(k == 0)
        def _():
            o_ref[...] = refs[2][...] if has_resid else jnp.zeros_like(o_ref)

        ab = a_ref[...].astype(BF16)
        o_ref[...] += _nt(ab, w_ref[...]) if nt else _nn(ab, w_ref[...])

    ins = [a_in, w]
    in_specs = [spec(a_layout, K, tk, lambda k: k), w_spec]
    if has_resid:
        ins.append(resid.reshape(n16, NSLAB * N) if out_layout == "view" else resid)
        in_specs.append(o_spec)
    out = pl.pallas_call(
        body, name=name, grid=(T // tm, nk), in_specs=in_specs, out_specs=o_spec,
        out_shape=_sds(oshape, F32), compiler_params=_cparams(),
    )(*ins)
    return out.reshape(T, N) if out_layout == "view" else out


def _log_sigmoid(z):
    return jnp.minimum(z, 0.0) - jnp.log(1.0 + jnp.exp(-jnp.abs(z)))


def _fox_gate_fwd(f_raw, fbias, name):
    T = f_raw.shape[0]
    cb = 256

    def body(f_ref, b_ref, c_ref):
        row = lax.broadcasted_iota(jnp.int32, (cb, cb), 0)
        col = lax.broadcasted_iota(jnp.int32, (cb, cb), 1)
        tri = (col <= row).astype(F32)
        carry = jnp.zeros((1, HEAD), F32)
        for i in range(T // cb):
            lf = _log_sigmoid(f_ref[i * cb:(i + 1) * cb, :] + b_ref[...])
            c = jnp.dot(tri, lf, preferred_element_type=F32, precision=lax.Precision.HIGHEST) + carry
            c_ref[i * cb:(i + 1) * cb, :] = c
            carry = c[cb - 1:cb, :]

    return pl.pallas_call(body, name=name, out_shape=_sds((T, HEAD), F32), compiler_params=_cparams())(f_raw, fbias)


def _fox_gate_bwd(f_raw, fbias, dc, name):
    T = f_raw.shape[0]
    cb = 256

    def body(f_ref, b_ref, dc_ref, df_ref, db_ref):
        row = lax.broadcasted_iota(jnp.int32, (cb, cb), 0)
        col = lax.broadcasted_iota(jnp.int32, (cb, cb), 1)
        tri = (col >= row).astype(F32)
        carry = jnp.zeros((1, HEAD), F32)
        dbias = jnp.zeros((1, HEAD), F32)
        for i in reversed(range(T // cb)):
            dlf = jnp.dot(tri, dc_ref[i * cb:(i + 1) * cb, :], preferred_element_type=F32,
                          precision=lax.Precision.HIGHEST) + carry
            carry = dlf[0:1, :]
            z = f_ref[i * cb:(i + 1) * cb, :] + b_ref[...]
            df = dlf * jax.nn.sigmoid(-z)
            df_ref[i * cb:(i + 1) * cb, :] = df
            dbias = dbias + jnp.sum(df, axis=0, keepdims=True)
        db_ref[...] = dbias

    return pl.pallas_call(body, name=name, out_shape=[_sds((T, HEAD), F32), _sds((1, HEAD), F32)],
                          compiler_params=_cparams())(f_raw, fbias, dc)


def _fox_fwd(qkv, c_col, c_row, tq, name, side=None):
    T = qkv.shape[0]
    H = qkv.shape[1] // (3 * HEAD)
    nq = T // tq
    c_blocks = c_row.reshape(H, nq, 1, tq)

    def body(q_ref, k_ref, v_ref, cq_ref, ck_ref, o_ref, lse_ref):
        qi = pl.program_id(1)
        q, cq = q_ref[...], cq_ref[...]
        causal = lax.broadcasted_iota(jnp.int32, (tq, tq), 1) <= lax.broadcasted_iota(jnp.int32, (tq, tq), 0)

        def key_block(ki, carry, diagonal):
            m, l, acc = carry
            rows = pl.ds(pl.multiple_of(ki * tq, tq), tq)
            s = _nt(q, k_ref[rows, :]) * SCALE + cq - ck_ref[ki]
            if diagonal:
                s = jnp.where(causal, s, NEG)
            m_new = jnp.maximum(m, jnp.max(s, axis=-1, keepdims=True))
            alpha = jnp.exp(m - m_new)
            p = jnp.exp(s - m_new)
            l = alpha * l + jnp.sum(p, axis=-1, keepdims=True)
            acc = alpha * acc + _nn(p.astype(BF16), v_ref[rows, :])
            return m_new, l, acc

        init = (jnp.full((tq, 1), NEG, F32), jnp.zeros((tq, 1), F32), jnp.zeros((tq, HEAD), F32))
        carry = lax.fori_loop(0, qi, lambda ki, c: key_block(ki, c, False), init)
        m, l, acc = key_block(qi, carry, True)
        o_ref[...] = acc / l
        lse_ref[...] = m + jnp.log(l)

    return _call(
        body, name=name, grid=(H, nq), side=side,
        in_specs=[pl.BlockSpec((tq, HEAD), lambda h, qi: (qi, h)),
                  pl.BlockSpec((T, HEAD), lambda h, qi: (0, H + h)),
                  pl.BlockSpec((T, HEAD), lambda h, qi: (0, 2 * H + h)),
                  pl.BlockSpec((None, tq, 1), lambda h, qi: (h, qi, 0)),
                  pl.BlockSpec((None, nq, 1, tq), lambda h, qi: (h, 0, 0, 0))],
        out_specs=[pl.BlockSpec((tq, HEAD), lambda h, qi: (qi, h)),
                   pl.BlockSpec((None, tq, 1), lambda h, qi: (h, qi, 0))],
        out_shape=[_sds((T, H * HEAD), F32), _sds((H, T, 1), F32)],
        args=(qkv, qkv, qkv, c_col, c_blocks))


def _fox_bwd(qkv, c_col, c_row, out, dout, lse, tq, name, side=None):
    T = qkv.shape[0]
    H = qkv.shape[1] // (3 * HEAD)
    nq = T // tq

    def body(q_ref, k_ref, v_ref, cq_ref, ck_ref, o_ref, do_ref, lse_ref, dq_ref, dk_ref, dv_ref, dck_ref, dcq_ref,
             delta_s):
        ki = pl.program_id(1)

        @pl.when(ki == 0)
        def _():
            dq_ref[...] = jnp.zeros_like(dq_ref)
            dcq_ref[...] = jnp.zeros_like(dcq_ref)
            delta_s[...] = jnp.sum(do_ref[...] * o_ref[...], axis=-1, keepdims=True)

        k, v, ck = k_ref[...], v_ref[...], ck_ref[...]
        causal = lax.broadcasted_iota(jnp.int32, (tq, tq), 1) <= lax.broadcasted_iota(jnp.int32, (tq, tq), 0)

        def query_block(qi, carry, diagonal):
            dk, dv, dck = carry
            rows = pl.ds(pl.multiple_of(qi * tq, tq), tq)
            q = q_ref[rows, :]
            s = _nt(q, k) * SCALE + cq_ref[rows, :] - ck
            if diagonal:
                s = jnp.where(causal, s, NEG)
            p = jnp.exp(s - lse_ref[rows, :])
            dob = do_ref[rows, :].astype(BF16)
            ds = p * (_nt(dob, v) - delta_s[rows, :])
            dsb = ds.astype(BF16)
            dq_ref[rows, :] += _nn(dsb, k) * SCALE
            dcq_ref[rows, :] += jnp.sum(ds, axis=-1, keepdims=True)
            return dk + _tn(dsb, q), dv + _tn(p.astype(BF16), dob), dck - jnp.sum(ds, axis=0, keepdims=True)

        init = (jnp.zeros((tq, HEAD), F32), jnp.zeros((tq, HEAD), F32), jnp.zeros((1, tq), F32))
        carry = query_block(ki, init, True)
        dk, dv, dck = lax.fori_loop(ki + 1, nq, lambda qi, c: query_block(qi, c, False), carry)
        dk_ref[...] = dk * SCALE
        dv_ref[...] = dv
        dck_ref[...] = dck

    head = lambda off: pl.BlockSpec((T, HEAD), lambda h, ki: (0, off + h))
    col = pl.BlockSpec((None, T, 1), lambda h, ki: (h, 0, 0))
    return _call(
        body, name=name, grid=(H, nq), side=side,
        in_specs=[head(0),
                  pl.BlockSpec((tq, HEAD), lambda h, ki: (ki, H + h)),
                  pl.BlockSpec((tq, HEAD), lambda h, ki: (ki, 2 * H + h)),
                  col, pl.BlockSpec((None, 1, tq), lambda h, ki: (h, 0, ki)), head(0), head(0), col],
        out_specs=[head(0),
                   pl.BlockSpec((tq, HEAD), lambda h, ki: (ki, h)),
                   pl.BlockSpec((tq, HEAD), lambda h, ki: (ki, h)),
                   pl.BlockSpec((None, 1, tq), lambda h, ki: (h, 0, ki)), col],
        out_shape=[_sds((T, H * HEAD), F32), _sds((T, H * HEAD), F32), _sds((T, H * HEAD), F32), _sds((H, 1, T), F32),
                   _sds((H, T, 1), F32)],
        scratch_shapes=[pltpu.VMEM((T, 1), F32)],
        args=(qkv, qkv, qkv, c_col, c_row, out, dout, lse))


def _t5_bucket(dist):
    max_exact = NUM_BUCKETS // 2
    d = dist.astype(np.float32)
    large = max_exact + (np.log(np.maximum(d, np.float32(1.0)) / np.float32(max_exact))
                         / np.float32(math.log(MAX_DISTANCE / max_exact))
                         * np.float32(NUM_BUCKETS - max_exact)).astype(np.int32)
    large = np.minimum(large, NUM_BUCKETS - 1)
    return np.where(dist < max_exact, dist, large)


def _bucket_maps():
    maps = []
    for d in DILATIONS:
        e = NSLAB // d
        rows = BLK // e
        idx = np.arange(BLK)
        pos = e * (idx % rows) + idx // rows
        qpos = pos[:, None] + BLK
        kpos = np.concatenate([pos, pos + BLK])[None, :]
        delta = qpos - kpos
        band = (delta >= 0) & (delta <= BLK)
        bucket = _t5_bucket(np.clip(delta, 0, None) * d)
        maps.append(np.where(band, bucket, -1).astype(np.int32))
    return np.stack(maps)


def _dil_geometry(T):
    n16 = T // NSLAB
    geo = []
    for d in DILATIONS:
        e = NSLAB // d
        rows = BLK // e
        nblk = n16 // rows
        geo.append((d, e, rows, nblk))
    return geo


DIL_INTERLEAVE_FWD = {1: 4, 4: 8, 16: 8}
DIL_INTERLEAVE_BWD = {1: 8, 4: 8, 16: 8}


def _dil_interleave(per_step, nblocks):
    while per_step > 1 and (nblocks % per_step or nblocks // per_step < 2):
        per_step -= 1
    return per_step


def _dil_bias(tab_ref, bkt_ref, bias_s, h):
    for p in range(len(DILATIONS)):
        bk = bkt_ref[p]
        bias = jnp.full((BLK, 2 * BLK), NEG, F32)
        for b in range(NUM_BUCKETS):
            bias = jnp.where(bk == b, tab_ref[b, h], bias)
        bias_s[p] = bias


def _dil_rows(d, e, rows, sub, blk):
    start = pl.multiple_of(blk * rows, rows)
    return [(sub + d * j, pl.ds(start, rows)) for j in range(e)]


def _gather(ref, idx):
    return jnp.concatenate([ref[s, r, :] for s, r in idx], axis=0)


def _scatter(ref, idx, val, rows):
    for j, (s, r) in enumerate(idx):
        ref[s, r, :] = val[j * rows:(j + 1) * rows]


def _scatter_add(ref, idx, val, rows):
    for j, (s, r) in enumerate(idx):
        ref[s, r, :] += val[j * rows:(j + 1) * rows]


def _dil_fwd(qkv, table, name, side=None):
    n16 = qkv.shape[1]
    T = NSLAB * n16
    H = qkv.shape[2] // (3 * HEAD)
    geo = _dil_geometry(T)
    bkt = jnp.asarray(_bucket_maps())

    def body(tab_ref, bkt_ref, q_ref, k_ref, v_ref, o_ref, lse_ref, bias_s, m_s, l_s):
        h = pl.program_id(0)
        _dil_bias(tab_ref, bkt_ref, bias_s, h)
        first_mask = lax.broadcasted_iota(jnp.int32, (BLK, 2 * BLK), 1) < BLK

        starts = len(DILATIONS) - 1

        def load(p, d, e, rows, sub, blk):
            cur = _dil_rows(d, e, rows, sub, blk)
            prev = _dil_rows(d, e, rows, sub, jnp.maximum(blk - 1, 0))
            q = _gather(q_ref, cur).astype(BF16)
            kk = jnp.concatenate([_gather(k_ref, prev), _gather(k_ref, cur)], axis=0).astype(BF16)
            vv = jnp.concatenate([_gather(v_ref, prev), _gather(v_ref, cur)], axis=0).astype(BF16)
            old = None if p == starts else (_gather(m_s, cur), _gather(l_s, cur), _gather(o_ref, cur))
            return cur, blk, q, kk, vv, old

        def compute(p, blk, q, kk, vv, old):
            s = _nt(q, kk) * SCALE + bias_s[p]
            s = jnp.where(first_mask & (blk == 0), NEG, s)
            m_blk = jnp.max(s, axis=-1, keepdims=True)
            if old is None:
                m_new = m_blk
                pr = jnp.exp(s - m_new)
                l_new = jnp.sum(pr, axis=-1, keepdims=True)
                acc = _nn(pr.astype(BF16), vv)
            else:
                m_old, l_old, acc_old = old
                m_new = jnp.maximum(m_old, m_blk)
                alpha = jnp.exp(m_old - m_new)
                pr = jnp.exp(s - m_new)
                l_new = alpha * l_old + jnp.sum(pr, axis=-1, keepdims=True)
                acc = alpha * acc_old + _nn(pr.astype(BF16), vv)
            if p == 0:
                return acc / l_new, m_new + jnp.log(l_new), None
            return acc, m_new, l_new

        def store(p, rows, cur, acc, m_new, l_new):
            _scatter(o_ref, cur, acc, rows)
            if p == 0:
                _scatter(lse_ref, cur, m_new, rows)
            else:
                _scatter(m_s, cur, m_new, rows)
                _scatter(l_s, cur, l_new, rows)

        for p in reversed(range(len(DILATIONS))):
            d, e, rows, nblk = geo[p]
            per_step = _dil_interleave(DIL_INTERLEAVE_FWD[d], d * nblk)

            def step(i, carry, p=p, d=d, e=e, rows=rows, nblk=nblk, per_step=per_step):
                ids = [i + u * (d * nblk // per_step) for u in range(per_step)]
                loaded = [load(p, d, e, rows, j // nblk, j % nblk) for j in ids]
                done = [(cur, compute(p, blk, q, kk, vv, old)) for cur, blk, q, kk, vv, old in loaded]
                for cur, res in done:
                    store(p, rows, cur, *res)
                return carry

            lax.fori_loop(0, d * nblk // per_step, step, 0)

    head = lambda off: pl.BlockSpec((NSLAB, n16, HEAD), lambda h: (0, 0, off + h))
    return _call(
        body, name=name, grid=(H,), side=side,
        in_specs=[pl.BlockSpec(memory_space=pltpu.SMEM), pl.BlockSpec((3, BLK, 2 * BLK), lambda h: (0, 0, 0)),
                  head(0), head(H), head(2 * H)],
        out_specs=[head(0), pl.BlockSpec((None, NSLAB, n16, 1), lambda h: (h, 0, 0, 0))],
        out_shape=[_sds((NSLAB, n16, H * HEAD), F32), _sds((H, NSLAB, n16, 1), F32)],
        scratch_shapes=[pltpu.VMEM((3, BLK, 2 * BLK), F32), pltpu.VMEM((NSLAB, n16, 1), F32),
                        pltpu.VMEM((NSLAB, n16, 1), F32)],
        args=(table, bkt, qkv, qkv, qkv))


def _dil_bwd(qkv, table, out, dout, lse, name, side=None):
    n16 = qkv.shape[1]
    T = NSLAB * n16
    H = qkv.shape[2] // (3 * HEAD)
    geo = _dil_geometry(T)
    bkt = jnp.asarray(_bucket_maps())

    def body(tab_ref, bkt_ref, q_ref, k_ref, v_ref, o_ref, do_ref, lse_ref,
             dq_ref, dk_ref, dv_ref, dtab_ref, bias_s, dbias_s, delta_s):
        h = pl.program_id(0)
        _dil_bias(tab_ref, bkt_ref, bias_s, h)
        first_mask = lax.broadcasted_iota(jnp.int32, (BLK, 2 * BLK), 1) < BLK
        dbias_s[...] = jnp.zeros_like(dbias_s)
        dq_ref[...] = jnp.zeros_like(dq_ref)
        dk_ref[...] = jnp.zeros_like(dk_ref)
        dv_ref[...] = jnp.zeros_like(dv_ref)
        for r in range(NSLAB):
            delta_s[r] = jnp.sum(do_ref[r] * o_ref[r], axis=-1, keepdims=True)

        def load(d, e, rows, sub, blk):
            cur = _dil_rows(d, e, rows, sub, blk)
            prev = _dil_rows(d, e, rows, sub, jnp.maximum(blk - 1, 0))
            q = _gather(q_ref, cur).astype(BF16)
            kk = jnp.concatenate([_gather(k_ref, prev), _gather(k_ref, cur)], axis=0).astype(BF16)
            vv = jnp.concatenate([_gather(v_ref, prev), _gather(v_ref, cur)], axis=0).astype(BF16)
            dob = _gather(do_ref, cur).astype(BF16)
            return cur, prev, blk, q, kk, vv, dob, _gather(lse_ref, cur), _gather(delta_s, cur)

        def compute(p, blk, q, kk, vv, dob, lse, delta):
            s = _nt(q, kk) * SCALE + bias_s[p]
            s = jnp.where(first_mask & (blk == 0), NEG, s)
            pr = jnp.exp(s - lse)
            ds = pr * (_nt(dob, vv) - delta)
            dsb = ds.astype(BF16)
            return ds, _nn(dsb, kk) * SCALE, _tn(dsb, q) * SCALE, _tn(pr.astype(BF16), dob)

        def store(rows, cur, prev, dq, dkk, dvv):
            _scatter_add(dq_ref, cur, dq, rows)
            _scatter_add(dk_ref, prev, dkk[:BLK], rows)
            _scatter_add(dk_ref, cur, dkk[BLK:], rows)
            _scatter_add(dv_ref, prev, dvv[:BLK], rows)
            _scatter_add(dv_ref, cur, dvv[BLK:], rows)

        for p in range(len(DILATIONS)):
            d, e, rows, nblk = geo[p]
            per_step = _dil_interleave(DIL_INTERLEAVE_BWD[d], d * nblk)

            def step(i, carry, p=p, d=d, e=e, rows=rows, nblk=nblk, per_step=per_step):
                ids = [i + u * (d * nblk // per_step) for u in range(per_step)]
                loaded = [load(d, e, rows, j // nblk, j % nblk) for j in ids]
                done = [(cur, prev, compute(p, *rest)) for cur, prev, *rest in loaded]
                dbias_s[p] += functools.reduce(jnp.add, [res[0] for _, _, res in done])
                for cur, prev, res in done:
                    store(rows, cur, prev, *res[1:])
                return carry

            lax.fori_loop(0, d * nblk // per_step, step, 0)

        lane = lax.broadcasted_iota(jnp.int32, (1, HEAD), 1)
        row = jnp.zeros((1, HEAD), F32)
        for b in range(NUM_BUCKETS):
            tot = jnp.zeros((1, 1), F32)
            for p in range(len(DILATIONS)):
                hit = jnp.where(bkt_ref[p] == b, dbias_s[p], 0.0)
                tot = tot + jnp.sum(jnp.sum(hit, axis=0, keepdims=True), axis=1, keepdims=True)
            row = jnp.where(lane == b, tot, row)
        dtab_ref[...] = row

    head = lambda off: pl.BlockSpec((NSLAB, n16, HEAD), lambda h: (0, 0, off + h))
    return _call(
        body, name=name, grid=(H,), side=side,
        in_specs=[pl.BlockSpec(memory_space=pltpu.SMEM), pl.BlockSpec((3, BLK, 2 * BLK), lambda h: (0, 0, 0)),
                  head(0), head(H), head(2 * H), head(0), head(0),
                  pl.BlockSpec((None, NSLAB, n16, 1), lambda h: (h, 0, 0, 0))],
        out_specs=[head(0), head(0), head(0), pl.BlockSpec((None, 1, HEAD), lambda h: (h, 0, 0))],
        out_shape=[_sds((NSLAB, n16, H * HEAD), F32)] * 3 + [_sds((H, 1, HEAD), F32)],
        scratch_shapes=[pltpu.VMEM((3, BLK, 2 * BLK), F32), pltpu.VMEM((3, BLK, 2 * BLK), F32),
                        pltpu.VMEM((NSLAB, n16, 1), F32)],
        args=(table, bkt, qkv, qkv, qkv, out, dout, lse))


def _qknorm_bwd(raw, dq, dk, dv, gains, tm, name):
    T, N = raw.shape
    C = N // 3

    def body(raw_ref, dq_ref, dk_ref, dv_ref, gains_ref, dp_ref, dg_ref):
        @pl.when(pl.program_id(0) == 0)
        def _():
            dg_ref[...] = jnp.zeros_like(dg_ref)

        for t, d_ref in enumerate((dq_ref, dk_ref)):
            gain = gains_ref[t]
            dgain = jnp.zeros((1, HEAD), F32)
            for k in range(C // HEAD):
                y = raw_ref[:, t * C + k * HEAD:t * C + (k + 1) * HEAD]
                dn = d_ref[:, k * HEAD:(k + 1) * HEAD]
                r = lax.rsqrt(jnp.mean(y * y, axis=-1, keepdims=True) + RMS_EPS)
                yhat = y * r
                gd = dn * gain
                dy = r * (gd - yhat * jnp.mean(gd * yhat, axis=-1, keepdims=True))
                dp_ref[:, t * C + k * HEAD:t * C + (k + 1) * HEAD] = dy.astype(BF16)
                dgain = dgain + jnp.sum(dn * yhat, axis=0, keepdims=True)
            dg_ref[t] += dgain
        dp_ref[:, 2 * C:] = dv_ref[...].astype(BF16)

    third = pl.BlockSpec((tm, C), lambda i: (i, 0))
    return pl.pallas_call(
        body, name=name, grid=(T // tm,),
        in_specs=[pl.BlockSpec((tm, N), lambda i: (i, 0)), third, third, third,
                  pl.BlockSpec((2, 1, HEAD), lambda i: (0, 0, 0))],
        out_specs=[pl.BlockSpec((tm, N), lambda i: (i, 0)), pl.BlockSpec((2, 1, HEAD), lambda i: (0, 0, 0))],
        out_shape=[_sds((T, N), BF16), _sds((2, 1, HEAD), F32)], compiler_params=_cparams(),
    )(raw, dq, dk, dv, gains)


def _loss_grad(y, target, tm, name):
    T, D = y.shape

    def body(y_ref, t_ref, dy_ref, loss_ref):
        @pl.when(pl.program_id(0) == 0)
        def _():
            loss_ref[...] = jnp.zeros_like(loss_ref)

        err = y_ref[...] - t_ref[...]
        dy_ref[...] = err * (1.0 / D)
        per_tok = jnp.mean(err * err, axis=-1, keepdims=True)
        tot = 0.5 * jnp.sum(per_tok, axis=0, keepdims=True)
        lane = lax.broadcasted_iota(jnp.int32, (1, HEAD), 1)
        loss_ref[...] += jnp.where(lane == 0, tot, 0.0)

    row = pl.BlockSpec((tm, D), lambda i: (i, 0))
    return pl.pallas_call(
        body, name=name, grid=(T // tm,), in_specs=[row, row],
        out_specs=[row, pl.BlockSpec((1, HEAD), lambda i: (0, 0))],
        out_shape=[_sds((T, D), F32), _sds((1, HEAD), F32)], compiler_params=_cparams(),
    )(y, target)


def _pad_lanes(v, width=HEAD):
    return jnp.pad(v, ((0, 0), (0, width - v.shape[1])))


def _local_step(x, target, small, wts, plan=None):
    grads = {}

    def hosted(host, fn, *args, **kw):
        side = plan.before(host, wts, grads) if plan is not None else None
        if side is None:
            return fn(*args, name=host, **kw)
        res, side_res = fn(*args, name=host, side=side, **kw)
        plan.after(host, side_res, wts, grads)
        return res

    T, D = x.shape
    C = D // 2
    H = C // HEAD
    n16 = T // NSLAB
    tm = min(512, T)
    tq = min(512, T)
    bn = min(1024, D)
    g1, gm, g2 = small["ffn1_norm"], small["mix_norm"], small["ffn2_norm"]
    gains_a = jnp.stack([small["q_norm_a"], small["k_norm_a"], jnp.ones_like(small["q_norm_a"])])
    gains_b = jnp.stack([small["q_norm_b"], small["k_norm_b"], jnp.ones_like(small["q_norm_b"])])
    fbias = _pad_lanes(small["forget_bias"])
    table = small["rel_bias_table"]

    x1, h1, gu1, act1 = hosted("ffn1_fwd", _ffn_fwd, x, g1, wts["ffn1_in"], wts["ffn1_out"], tm)
    w_in_t, w_f_t, w_o = wts["w_in_t"], wts["w_f_t"], wts["w_o"]
    raw_a, nrm_a, h2a = _proj(x1, gm, w_in_t, gains_a, (True, True, False), tn=C, w_off=0, slabs=True, tm=n16,
                              normed_dtype=F32, name="proj_a")
    raw_b, nrm_b, h2b = _proj(x1, gm, w_in_t, gains_b, (True, True, False), tn=C, w_off=3, slabs=False, tm=tm,
                              normed_dtype=BF16, name="proj_b")
    f_raw, _, _ = _proj(x1, gm, w_f_t, gains_b[:1], (False,), tn=HEAD, w_off=0, slabs=False, tm=tm,
                        normed_dtype=BF16, name="proj_f")
    c = _fox_gate_fwd(f_raw, fbias, "fox_gate_fwd")
    c_heads = c[:, :H].T
    c_col, c_row = c_heads[:, :, None], c_heads[:, None, :]
    out_a, lse_a = hosted("dil_fwd", _dil_fwd, nrm_a, table)
    out_b, lse_b = hosted("fox_fwd", _fox_fwd, nrm_b, c_col, c_row, tq)
    x2a = _mm(out_a, w_o, nt=False, tk=C, tm=n16, a_layout="slab", out_layout="view", resid=x1, name="out_a")
    x2 = _mm(out_b, w_o, nt=False, tk=C, tm=tm, a_layout="nat", out_layout="nat", resid=x2a, w_off=1, name="out_b")
    y, h3, gu3, act3 = _ffn_fwd(x2, g2, wts["ffn2_in"], wts["ffn2_out"], tm, "ffn2_fwd")
    dy, loss_row = _loss_grad(y, target, tm, "loss_grad")

    def ffn_backward(tag, xin, g, h, gu, act, win, wout, dres):
        nc, tf = wout.shape[0], wout.shape[1]
        dh, dgu, dyb = hosted(tag + "_bwd", _ffn_bwd, dres, gu, win, wout, tm)
        grads[tag + "_w_in_t"] = hosted(tag + "_dwin", _mm_tn, dgu.reshape(2 * nc, tf, T), h, bm=tf, bn=bn, bt=T,
                                        a_rows=True)
        dxin, grads[tag + "_norm"] = hosted(tag + "_rms_bwd", _rms_bwd, xin, g, dh, dres, tm)
        grads[tag + "_w_out"] = hosted(tag + "_dwout", _mm_tn, act, dyb, bm=tf, bn=bn, bt=T, a_rows=True)
        return dxin

    dx2 = ffn_backward("ffn2", x2, g2, h3, gu3, act3, wts["ffn2_in"], wts["ffn2_out"], dy)

    dmix_a = _mm(dx2, w_o, nt=True, tk=D, tm=n16, a_layout="view", out_layout="slab", n_out=C, name="dmix_a")
    dmix_b = _mm(dx2, w_o, nt=True, tk=D, tm=tm, a_layout="nat", out_layout="nat", n_out=C, w_off=1, name="dmix_b")
    dwo = _mm_tn(out_a.reshape(1, T, C), dx2.reshape(n16, NSLAB * D), bm=C, bn=bn, bt=n16, b_slabs=True,
                 rows=2 * C, name="dwo_a")
    dwo = _mm_tn(out_b.reshape(1, T, C), dx2, bm=C, bn=bn, bt=tm, rows=2 * C, m_off=1, into=dwo, name="dwo_b")
    grads["w_out"] = dwo[0]

    dqa, dka, dva, dtab = hosted("dil_bwd", _dil_bwd, nrm_a, table, out_a, dmix_a, lse_a)
    dqb, dkb, dvb, dck, dcq = hosted("fox_bwd", _fox_bwd, nrm_b, c_col, c_row, out_b, dmix_b, lse_b, tq)
    grads["rel_bias_table"] = dtab[:, 0, :NUM_BUCKETS].T
    dc = _pad_lanes((dck[:, 0, :] + dcq[:, :, 0]).T)
    df, dfb = _fox_gate_bwd(f_raw, fbias, dc, "fox_gate_bwd")
    grads["forget_bias"] = dfb[:, :H]

    flat = lambda a: a.reshape(T, a.shape[-1])
    dproj_a, dgain_a = _qknorm_bwd(flat(raw_a), flat(dqa), flat(dka), flat(dva), gains_a[:2], min(256, T), "qknorm_bwd_a")
    dproj_b, dgain_b = _qknorm_bwd(raw_b, dqb, dkb, dvb, gains_b[:2], min(256, T), "qknorm_bwd_b")
    grads["q_norm_a"], grads["k_norm_a"] = dgain_a[0], dgain_a[1]
    grads["q_norm_b"], grads["k_norm_b"] = dgain_b[0], dgain_b[1]
    dproj_a = dproj_a.reshape(NSLAB, n16, 3 * C)

    dh2 = _mm(dproj_a, w_in_t, nt=False, tk=C, tm=n16, a_layout="slab", out_layout="view", name="dh2_a")
    dh2 = _mm(dproj_b, w_in_t, nt=False, tk=C, tm=tm, a_layout="nat", out_layout="nat", resid=dh2, w_off=3, name="dh2_b")
    dh2 = _mm(df, w_f_t, nt=False, tk=HEAD, tm=tm, a_layout="nat", out_layout="nat", resid=dh2, name="dh2_f")
    dx1, grads["mix_norm"] = _rms_bwd(x1, gm, dh2, dx2, tm, "mix_rms_bwd")
    bt = min(2048, T)
    dwt = _mm_tn(flat(dproj_a)[None], flat(h2a), bm=C, bn=bn, bt=bt, rows=6 * C + H, name="dw_a")
    dwt = _mm_tn(dproj_b[None], h2b, bm=C, bn=bn, bt=bt, rows=6 * C + H, m_off=3, into=dwt, name="dw_b")
    dwt = _mm_tn(df[None, :, :H], h2b, bm=H, bn=bn, bt=bt, rows=6 * C + H, m_off=6 * C // H, into=dwt, name="dw_f")
    grads["w_in_t"] = dwt[0]

    grad_x = ffn_backward("ffn1", x, g1, h1, gu1, act1, wts["ffn1_in"], wts["ffn1_out"], dx1)
    return loss_row, grad_x, grads


def _place():
    x, y, c = lax.axis_index("x"), lax.axis_index("y"), lax.axis_index("c")
    other_chips = [(1 - x, y), (x, 1 - y), (1 - x, 1 - y)]
    return x, y, c, other_chips


def _run_side(side, name):
    def body(*refs):
        si, so = len(side.ins), len(side.outs)
        side.start(refs[:si], refs[si:si + so], refs[si + so:])
        side.finish(refs[:si], refs[si:si + so], refs[si + so:])

    return pl.pallas_call(body, name=name, in_specs=[ANY] * len(side.ins), out_specs=[ANY] * len(side.outs),
                          out_shape=side.outs, scratch_shapes=side.sems)(*side.ins)


def _all_gather(shards):
    n = len(shards)

    def plan(ins, outs, sems):
        send_sems, recv_sems, local_sems = sems
        x, y, c, chips = _place()
        me, sibling = (x, y, c), (x, y, 1 - c)

        def copy(a, k, block, to, src=None):
            px, py, pc = block
            dst = outs[a].at[4 * px + 2 * py + pc]
            return pltpu.make_async_remote_copy(
                src_ref=dst if src is None else src, dst_ref=dst, send_sem=send_sems.at[7 * a + k],
                recv_sem=recv_sems.at[7 * a + k], device_id=to, device_id_type=MESH)

        mine = [pltpu.make_async_copy(ins[a], outs[a].at[4 * x + 2 * y + c], local_sems.at[a]) for a in range(n)]
        first = []
        for a in range(n):
            first.append(copy(a, 0, me, sibling, src=ins[a]))
            first += [copy(a, 1 + j, me, (*chip, c), src=ins[a]) for j, chip in enumerate(chips)]
        return copy, mine, first, me, sibling, c, chips

    def start(ins, outs, sems):
        _, mine, first, *_ = plan(ins, outs, sems)
        for cp in mine + first:
            cp.start()

    def finish(ins, outs, sems):
        copy, mine, first, me, sibling, c, chips = plan(ins, outs, sems)
        passed = []
        for a in range(n):
            for j, chip in enumerate(chips):
                copy(a, 1 + j, (*chip, c), me).wait_recv()
                fwd = copy(a, 4 + j, (*chip, c), sibling)
                fwd.start()
                passed.append(fwd)
        for a in range(n):
            copy(a, 0, sibling, me).wait_recv()
            for j, chip in enumerate(chips):
                copy(a, 4 + j, (*chip, 1 - c), me).wait_recv()
        for cp in first + passed:
            cp.wait_send()
        for cp in mine:
            cp.wait()

    return _Side(shards, [_sds((N_DEV,) + s.shape, s.dtype) for s in shards],
                 [pltpu.SemaphoreType.DMA((7 * n,)), pltpu.SemaphoreType.DMA((7 * n,)), pltpu.SemaphoreType.DMA((n,))],
                 start, finish)


def _exchange_in_chip(gs):
    n = len(gs)

    def copies(ins, outs, sems):
        x, y, c, _ = _place()
        return [pltpu.make_async_remote_copy(
            src_ref=ins[a].at[2 * q + 1 - c], dst_ref=outs[a].at[q], send_sem=sems[0].at[4 * a + q],
            recv_sem=sems[1].at[4 * a + q], device_id=(x, y, 1 - c), device_id_type=MESH)
            for a in range(n) for q in range(4)]

    def start(ins, outs, sems):
        for cp in copies(ins, outs, sems):
            cp.start()

    def finish(ins, outs, sems):
        for cp in copies(ins, outs, sems):
            cp.wait()

    return _Side(gs, [_sds((4,) + g.shape[1:], g.dtype) for g in gs],
                 [pltpu.SemaphoreType.DMA((4 * n,)), pltpu.SemaphoreType.DMA((4 * n,))], start, finish)


def _exchange_between_chips(ps):
    n = len(ps)

    def copies(ins, outs, sems):
        x, y, c, chips = _place()
        return [pltpu.make_async_remote_copy(
            src_ref=ins[a].at[2 * cx + cy], dst_ref=outs[a].at[j], send_sem=sems[0].at[3 * a + j],
            recv_sem=sems[1].at[3 * a + j], device_id=(cx, cy, c), device_id_type=MESH)
            for a in range(n) for j, (cx, cy) in enumerate(chips)]

    def start(ins, outs, sems):
        for cp in copies(ins, outs, sems):
            cp.start()

    def finish(ins, outs, sems):
        for cp in copies(ins, outs, sems):
            cp.wait()

    return _Side(ps, [_sds((3,) + p.shape[1:], p.dtype) for p in ps],
                 [pltpu.SemaphoreType.DMA((3 * n,)), pltpu.SemaphoreType.DMA((3 * n,))], start, finish)


def _all_reduce_small(v, name):
    R = v.shape[0]

    def body(v_ref, sum_ref, all_ref, send_sems, recv_sems):
        x, y, c, _ = _place()
        k = 4 * x + 2 * y + c
        all_ref[k] = v_ref[...]
        copies = []
        for rel in range(1, N_DEV):
            fx, fy, fc = (rel >> 2) & 1, (rel >> 1) & 1, rel & 1
            peer = (1 - x if fx else x, 1 - y if fy else y, 1 - c if fc else c)
            copies.append(pltpu.make_async_remote_copy(
                src_ref=v_ref, dst_ref=all_ref.at[k], send_sem=send_sems.at[rel - 1], recv_sem=recv_sems.at[rel - 1],
                device_id=peer, device_id_type=MESH))
        for cp in copies:
            cp.start()
        for rel in range(1, N_DEV):
            fx, fy, fc = (rel >> 2) & 1, (rel >> 1) & 1, rel & 1
            src = 4 * (1 - x if fx else x) + 2 * (1 - y if fy else y) + (1 - c if fc else c)
            pltpu.make_async_remote_copy(
                src_ref=v_ref, dst_ref=all_ref.at[src], send_sem=send_sems.at[rel - 1], recv_sem=recv_sems.at[rel - 1],
                device_id=(x, y, c), device_id_type=MESH).wait_recv()
        for cp in copies:
            cp.wait_send()
        tot = all_ref[0]
        for d in range(1, N_DEV):
            tot = tot + all_ref[d]
        sum_ref[...] = tot

    vm = pl.BlockSpec(memory_space=pltpu.VMEM)
    return pl.pallas_call(
        body, name=name, in_specs=[vm], out_specs=[vm, vm],
        out_shape=[_sds((R, HEAD), F32), _sds((N_DEV, R, HEAD), F32)],
        scratch_shapes=[pltpu.SemaphoreType.DMA((N_DEV - 1,)), pltpu.SemaphoreType.DMA((N_DEV - 1,))],
    )(v)[0]


def _tiles(rows, cols):
    tr = next((cand for cand in (688, 512, 256) if rows % cand == 0), rows)
    tc = 512 if (cols % 512 == 0 and tr * cols * 4 > (2 << 20)) else cols
    return tr, tc


def _chip_sum(g, r1, core, name):
    _, R, Cc = g.shape
    tr, tc = _tiles(R, Cc)

    def body(core_ref, g_ref, r_ref, p_ref):
        p_ref[...] = (g_ref[...] + r_ref[...]).astype(BF16)

    blk = lambda f: pl.BlockSpec((None, tr, tc), f)
    return pl.pallas_call(
        body, name=name,
        grid_spec=pltpu.PrefetchScalarGridSpec(
            num_scalar_prefetch=1, grid=(4, R // tr, Cc // tc),
            in_specs=[blk(lambda q, i, j, core: (2 * q + core[0], i, j)), blk(lambda q, i, j, core: (q, i, j))],
            out_specs=blk(lambda q, i, j, core: (q, i, j))),
        out_shape=_sds((4, R, Cc), BF16), compiler_params=_cparams(),
    )(core, g, r1)


def _adamw_update(gv, w_ref, m_ref, v_ref, d_ref, nm_ref, nv_ref):
    nm = B1 * m_ref[...] + (1.0 - B1) * gv
    nv = B2 * v_ref[...] + (1.0 - B2) * jnp.square(gv)
    m_hat = nm / (1.0 - B1 ** STEP)
    v_hat = nv / (1.0 - B2 ** STEP)
    d_ref[...] = -LR * (m_hat / (jnp.sqrt(v_hat) + EPS) + WD * w_ref[...])
    nm_ref[...] = nm
    nv_ref[...] = nv


def _reduce_adamw(g, r1, r2, where, w, m, v, name):
    _, R, Cc = g.shape
    tr, tc = _tiles(R, Cc)

    def body(where_ref, g_ref, r1_ref, r2_ref, w_ref, m_ref, v_ref, o_ref, d_ref, nm_ref, nv_ref):
        gv = ((g_ref[...] + r1_ref[...]) + r2_ref[0].astype(F32)) + (r2_ref[1].astype(F32) + r2_ref[2].astype(F32))
        o_ref[...] = gv
        _adamw_update(gv, w_ref, m_ref, v_ref, d_ref, nm_ref, nv_ref)

    blk = pl.BlockSpec((tr, tc), lambda i, j, w: (i, j))
    return pl.pallas_call(
        body, name=name,
        grid_spec=pltpu.PrefetchScalarGridSpec(
            num_scalar_prefetch=1, grid=(R // tr, Cc // tc),
            in_specs=[pl.BlockSpec((None, tr, tc), lambda i, j, w: (w[0], i, j)),
                      pl.BlockSpec((None, tr, tc), lambda i, j, w: (w[1], i, j)),
                      pl.BlockSpec((3, tr, tc), lambda i, j, w: (0, i, j)), blk, blk, blk],
            out_specs=[blk] * 4),
        out_shape=[_sds((R, Cc), F32)] * 4, compiler_params=_cparams(),
    )(where, g, r1, r2, w, m, v)


def _adamw(w, g, m, v, name):
    R, Cc = w.shape
    tr, tc = _tiles(R, Cc)

    def body(w_ref, g_ref, m_ref, v_ref, d_ref, nm_ref, nv_ref):
        _adamw_update(g_ref[...], w_ref, m_ref, v_ref, d_ref, nm_ref, nv_ref)

    blk = pl.BlockSpec((tr, tc), lambda i, j: (i, j))
    return pl.pallas_call(
        body, name=name, grid=(R // tr, Cc // tc), in_specs=[blk] * 4, out_specs=[blk] * 3,
        out_shape=[_sds((R, Cc), F32)] * 3, compiler_params=_cparams(),
    )(w, g, m, v)


SMALL = ("ffn1_norm", "mix_norm", "ffn2_norm", "q_norm_a", "k_norm_a", "q_norm_b", "k_norm_b", "forget_bias",
         "rel_bias_table")
LARGE = ("ffn1_w_in", "ffn1_w_out", "w_in", "w_out", "ffn2_w_in", "ffn2_w_out")
ORDER = ("ffn1_norm", "ffn1_w_in", "ffn1_w_out", "mix_norm", "w_in", "q_norm_a", "k_norm_a", "q_norm_b", "k_norm_b",
         "forget_bias", "rel_bias_table", "w_out", "ffn2_norm", "ffn2_w_in", "ffn2_w_out")


def _pack_small(vals):
    rows = []
    for name in SMALL:
        flat = vals[name].reshape(-1)
        pad = (-flat.shape[0]) % HEAD
        rows.append(jnp.pad(flat, (0, pad)).reshape(-1, HEAD))
    return jnp.concatenate(rows, axis=0)


def _unpack_small(packed, like):
    out, r = {}, 0
    for name in SMALL:
        size = like[name].size
        nrow = -(-size // HEAD)
        out[name] = packed[r:r + nrow].reshape(-1)[:size].reshape(like[name].shape)
        r += nrow
    return out


def kernel(x, ffn1_norm, ffn1_w_in, ffn1_w_out, mix_norm, w_in, q_norm_a, k_norm_a, q_norm_b, k_norm_b, forget_bias, rel_bias_table, w_out, ffn2_norm, ffn2_w_in, ffn2_w_out, loss_target, m_ffn1_norm, m_ffn1_w_in, m_ffn1_w_out, m_mix_norm, m_w_in, m_q_norm_a, m_k_norm_a, m_q_norm_b, m_k_norm_b, m_forget_bias, m_rel_bias_table, m_w_out, m_ffn2_norm, m_ffn2_w_in, m_ffn2_w_out, v_ffn1_norm, v_ffn1_w_in, v_ffn1_w_out, v_mix_norm, v_w_in, v_q_norm_a, v_k_norm_a, v_q_norm_b, v_k_norm_b, v_forget_bias, v_rel_bias_table, v_w_out, v_ffn2_norm, v_ffn2_w_in, v_ffn2_w_out):
    w = dict(ffn1_norm=ffn1_norm, ffn1_w_in=ffn1_w_in, ffn1_w_out=ffn1_w_out, mix_norm=mix_norm, w_in=w_in,
             q_norm_a=q_norm_a, k_norm_a=k_norm_a, q_norm_b=q_norm_b, k_norm_b=k_norm_b, forget_bias=forget_bias,
             rel_bias_table=rel_bias_table, w_out=w_out, ffn2_norm=ffn2_norm, ffn2_w_in=ffn2_w_in, ffn2_w_out=ffn2_w_out)
    m = dict(ffn1_norm=m_ffn1_norm, ffn1_w_in=m_ffn1_w_in, ffn1_w_out=m_ffn1_w_out, mix_norm=m_mix_norm, w_in=m_w_in,
             q_norm_a=m_q_norm_a, k_norm_a=m_k_norm_a, q_norm_b=m_q_norm_b, k_norm_b=m_k_norm_b,
             forget_bias=m_forget_bias, rel_bias_table=m_rel_bias_table, w_out=m_w_out, ffn2_norm=m_ffn2_norm,
             ffn2_w_in=m_ffn2_w_in, ffn2_w_out=m_ffn2_w_out)
    v = dict(ffn1_norm=v_ffn1_norm, ffn1_w_in=v_ffn1_w_in, ffn1_w_out=v_ffn1_w_out, mix_norm=v_mix_norm, w_in=v_w_in,
             q_norm_a=v_q_norm_a, k_norm_a=v_k_norm_a, q_norm_b=v_q_norm_b, k_norm_b=v_k_norm_b,
             forget_bias=v_forget_bias, rel_bias_table=v_rel_bias_table, w_out=v_w_out, ffn2_norm=v_ffn2_norm,
             ffn2_w_in=v_ffn2_w_in, ffn2_w_out=v_ffn2_w_out)
    T, D = x.shape[1], x.shape[2]
    C = D // 2
    H = C // HEAD
    ff_shard = ffn1_w_out.shape[1]

    f1i, f1o = _run_side(_all_gather([ffn1_w_in[0].T.astype(BF16), ffn1_w_out[0].astype(BF16)]), "gather_ffn1")
    wts = dict(ffn1_in=f1i.reshape(2, N_DEV, ff_shard, D), ffn1_out=f1o)
    xi, yi, ci = lax.axis_index("x"), lax.axis_index("y"), lax.axis_index("c")
    core = jnp.reshape(ci, (1,)).astype(jnp.int32)
    where = jnp.stack([4 * xi + 2 * yi + ci, 2 * xi + yi]).astype(jnp.int32)
    gs, r1, ps, r2 = {}, {}, {}, {}

    def by_destination(name, grads):
        gs[name] = grads[name + "_t" if name.endswith("w_in") else name].reshape(N_DEV, -1, D)
        return gs[name]

    def chip_sums(names):
        for name in names:
            ps[name] = _chip_sum(gs[name], r1[name], core, "chip_sum_" + name)
        return [ps[name] for name in names]

    class Plan:
        carried = {"ffn1_fwd": ("gather", ("w_in", "w_out")),
                   "dil_fwd": ("gather", ("ffn2_w_out",)), "fox_fwd": ("gather", ("ffn2_w_in",)),
                   "dil_bwd": ("in_chip", ("ffn2_w_in", "ffn2_w_out")), "fox_bwd": ("between", ("ffn2_w_in", "ffn2_w_out")),
                   "ffn1_bwd": ("in_chip", ("w_in", "w_out")), "ffn1_dwin": ("between", ("w_in", "w_out")),
                   "ffn1_rms_bwd": ("in_chip", ("ffn1_w_in",)), "ffn1_dwout": ("between", ("ffn1_w_in",))}

        def before(self, host, wts, grads):
            if host not in self.carried:
                return None
            kind, names = self.carried[host]
            if kind == "gather":
                return _all_gather([(w[n][0].T if n.endswith("w_in") else w[n][0]).astype(BF16) for n in names])
            if kind == "in_chip":
                return _exchange_in_chip([by_destination(n, grads) for n in names])
            return _exchange_between_chips(chip_sums(names))

        def after(self, host, res, wts, grads):
            kind, names = self.carried[host]
            if host == "ffn1_fwd":
                w_in_t = res[0].reshape(-1, D)
                wts.update(w_in_t=w_in_t, w_f_t=jnp.pad(w_in_t[6 * C:], ((0, HEAD - H), (0, 0))),
                           w_o=res[1].reshape(2 * C, D))
            elif host == "dil_fwd":
                wts.update(ffn2_out=res[0])
            elif host == "fox_fwd":
                wts.update(ffn2_in=res[0].reshape(2, N_DEV, ff_shard, D))
            else:
                (r1 if kind == "in_chip" else r2).update(zip(names, res))

    small = {name: w[name] for name in SMALL}
    loss_row, grad_x, grads = _local_step(x[0], loss_target[0], small, wts, Plan())

    tail = ("ffn1_w_out",)
    r1[tail[0]], = _run_side(_exchange_in_chip([by_destination(tail[0], grads)]), "reduce_in_chip_tail")
    r2.update(zip(tail, _run_side(_exchange_between_chips(chip_sums(tail)), "reduce_between_chips_tail")))

    packed = _pack_small(grads)
    nsmall = packed.shape[0]
    packed = jnp.concatenate([packed, loss_row, jnp.zeros(((-nsmall - 1) % 8, HEAD), F32)], axis=0)
    reduced = _all_reduce_small(packed, "reduce_small")
    loss = reduced[nsmall, 0]
    g_small = _unpack_small(reduced[:nsmall], small)

    grad, delta, new_m, new_v = dict(g_small), {}, {}, {}
    for name in LARGE:
        to = (lambda t: t[0].T) if name.endswith("w_in") else (lambda t: t[0])
        back = (lambda t: t.T[None]) if name.endswith("w_in") else (lambda t: t[None])
        res = _reduce_adamw(gs[name], r1[name], r2[name], where, to(w[name]), to(m[name]), to(v[name]), "adamw_" + name)
        grad[name], delta[name], new_m[name], new_v[name] = (back(t) for t in res)
    d, nm, nv = _adamw(_pack_small(w), reduced[:nsmall], _pack_small(m), _pack_small(v), "adamw_small")
    delta.update(_unpack_small(d, small))
    new_m.update(_unpack_small(nm, small))
    new_v.update(_unpack_small(nv, small))
    return (loss, grad_x[None], *[grad[n] for n in ORDER], *[delta[n] for n in ORDER],
            *[new_m[n] for n in ORDER], *[new_v[n] for n in ORDER])
```

```python
import functools
import math

import numpy as np
import jax
import jax.numpy as jnp
from jax import lax
from jax.experimental import pallas as pl
from jax.experimental.pallas import tpu as pltpu

F32, BF16 = jnp.float32, jnp.bfloat16
HEAD = 128
NSLAB = 16
BLK = 128
DILATIONS = (1, 4, 16)
NUM_BUCKETS, MAX_DISTANCE = 32, 2048
RMS_EPS = 1e-6
NEG = -1e30
SCALE = HEAD ** -0.5
LR, B1, B2, EPS, WD, STEP = 0.001, 0.9, 0.999, 1e-08, 0.01, 10
N_DEV = 8
VMEM_LIMIT_BYTES = 56 << 20
MESH = pl.DeviceIdType.MESH


def _cparams(**kw):
    return pltpu.CompilerParams(vmem_limit_bytes=VMEM_LIMIT_BYTES, **kw)


def _nn(a, b):
    return jnp.dot(a, b, preferred_element_type=F32)


def _nt(a, b):
    return lax.dot_general(a, b, (((1,), (1,)), ((), ())), preferred_element_type=F32)


def _tn(a, b):
    return lax.dot_general(a, b, (((0,), (0,)), ((), ())), preferred_element_type=F32)


def _sds(shape, dtype):
    return jax.ShapeDtypeStruct(shape, dtype)


ANY = pl.BlockSpec(memory_space=pl.ANY)


class _Side:
    def __init__(self, ins, outs, sems, start, finish):
        self.ins, self.outs, self.sems, self.start, self.finish = list(ins), list(outs), list(sems), start, finish


def _call(body, *, name, grid, in_specs, out_specs, out_shape, args, scratch_shapes=(), side=None):
    in_specs, out_specs, out_shape = list(in_specs), list(out_specs), list(out_shape)
    scratch_shapes = list(scratch_shapes)
    if side is None:
        return pl.pallas_call(body, name=name, grid=grid, in_specs=in_specs, out_specs=out_specs, out_shape=out_shape,
                              scratch_shapes=scratch_shapes, compiler_params=_cparams())(*args)
    ni, no, ns = len(args), len(out_shape), len(scratch_shapes)
    si, so = len(side.ins), len(side.outs)

    def fused(*refs):
        h_in, s_in = refs[:ni], refs[ni:ni + si]
        h_out, s_out = refs[ni + si:ni + si + no], refs[ni + si + no:ni + si + no + so]
        h_scr, s_sem = refs[ni + si + no + so:ni + si + no + so + ns], refs[ni + si + no + so + ns:]
        ids = [pl.program_id(k) for k in range(len(grid))]
        first = functools.reduce(jnp.logical_and, [i == 0 for i in ids])
        last = functools.reduce(jnp.logical_and, [i == n - 1 for i, n in zip(ids, grid)])

        @pl.when(first)
        def _():
            side.start(s_in, s_out, s_sem)

        body(*h_in, *h_out, *h_scr)

        @pl.when(last)
        def _():
            side.finish(s_in, s_out, s_sem)

    res = pl.pallas_call(
        fused, name=name, grid=grid, in_specs=in_specs + [ANY] * si, out_specs=out_specs + [ANY] * so,
        out_shape=out_shape + side.outs, scratch_shapes=scratch_shapes + side.sems, compiler_params=_cparams(),
    )(*args, *side.ins)
    return list(res[:no]), list(res[no:])


def _ffn_fwd(x, g, win, wout, tm, name, side=None):
    T, D = x.shape
    nc, tf = wout.shape[0], wout.shape[1]

    def body(x_ref, g_ref, win_ref, wout_ref, y_ref, h_ref, gu_ref, act_ref):
        j = pl.program_id(1)

        @pl.when(j == 0)
        def _():
            xv = x_ref[...]
            r = lax.rsqrt(jnp.mean(xv * xv, axis=-1, keepdims=True) + RMS_EPS)
            h_ref[...] = (xv * r * g_ref[...]).astype(BF16)
            y_ref[...] = jnp.zeros_like(y_ref)

        hb = h_ref[...]
        gt = _nt(win_ref[0], hb)
        up = _nt(win_ref[1], hb)
        gu_ref[0] = gt.astype(BF16)
        gu_ref[1] = up.astype(BF16)
        act = (gt * jax.nn.sigmoid(gt) * up).astype(BF16)
        act_ref[...] = act
        y_ref[...] += _tn(act, wout_ref[...])

        @pl.when(j == nc - 1)
        def _():
            y_ref[...] = x_ref[...] + 0.5 * y_ref[...]

    return _call(
        body, name=name, grid=(T // tm, nc), side=side,
        in_specs=[pl.BlockSpec((tm, D), lambda i, j: (i, 0)),
                  pl.BlockSpec((1, D), lambda i, j: (0, 0)),
                  pl.BlockSpec((2, None, tf, D), lambda i, j: (0, j, 0, 0)),
                  pl.BlockSpec((None, tf, D), lambda i, j: (j, 0, 0))],
        out_specs=[pl.BlockSpec((tm, D), lambda i, j: (i, 0)),
                   pl.BlockSpec((tm, D), lambda i, j: (i, 0)),
                   pl.BlockSpec((2, None, tf, tm), lambda i, j: (0, j, 0, i)),
                   pl.BlockSpec((None, tf, tm), lambda i, j: (j, 0, i))],
        out_shape=[_sds((T, D), F32), _sds((T, D), BF16), _sds((2, nc, tf, T), BF16), _sds((nc, tf, T), BF16)],
        args=(x, g, win, wout))


def _ffn_bwd(dy, gu, win, wout, tm, name, side=None):
    T, D = dy.shape
    nc, tf = wout.shape[0], wout.shape[1]

    def body(dy_ref, gu_ref, win_ref, wout_ref, dh_ref, dgu_ref, dyb_ref):
        j = pl.program_id(1)

        @pl.when(j == 0)
        def _():
            dh_ref[...] = jnp.zeros_like(dh_ref)
            dyb_ref[...] = (0.5 * dy_ref[...]).astype(BF16)

        dact = _nt(wout_ref[...], dyb_ref[...])
        gt = gu_ref[0].astype(F32)
        up = gu_ref[1].astype(F32)
        s = jax.nn.sigmoid(gt)
        dgb = (dact * up * (s * (1.0 + gt * (1.0 - s)))).astype(BF16)
        dub = (dact * (gt * s)).astype(BF16)
        dgu_ref[0] = dgb
        dgu_ref[1] = dub
        dh_ref[...] += _tn(dgb, win_ref[0]) + _tn(dub, win_ref[1])

    return _call(
        body, name=name, grid=(T // tm, nc), side=side,
        in_specs=[pl.BlockSpec((tm, D), lambda i, j: (i, 0)),
                  pl.BlockSpec((2, None, tf, tm), lambda i, j: (0, j, 0, i)),
                  pl.BlockSpec((2, None, tf, D), lambda i, j: (0, j, 0, 0)),
                  pl.BlockSpec((None, tf, D), lambda i, j: (j, 0, 0))],
        out_specs=[pl.BlockSpec((tm, D), lambda i, j: (i, 0)),
                   pl.BlockSpec((2, None, tf, tm), lambda i, j: (0, j, 0, i)),
                   pl.BlockSpec((tm, D), lambda i, j: (i, 0))],
        out_shape=[_sds((T, D), F32), _sds((2, nc, tf, T), BF16), _sds((T, D), BF16)],
        args=(dy, gu, win, wout))


def _rms_bwd(x, g, dh, dres, tm, name, side=None):
    T, D = x.shape

    def body(x_ref, g_ref, dh_ref, dres_ref, dx_ref, dg_ref):
        @pl.when(pl.program_id(0) == 0)
        def _():
            dg_ref[...] = jnp.zeros_like(dg_ref)

        xv = x_ref[...]
        r = lax.rsqrt(jnp.mean(xv * xv, axis=-1, keepdims=True) + RMS_EPS)
        xhat = xv * r
        dh = dh_ref[...]
        gd = dh * g_ref[...]
        dx_ref[...] = dres_ref[...] + r * (gd - xhat * jnp.mean(gd * xhat, axis=-1, keepdims=True))
        dg_ref[...] += jnp.sum(dh * xhat, axis=0, keepdims=True)

    row = pl.BlockSpec((tm, D), lambda i: (i, 0))
    one = pl.BlockSpec((1, D), lambda i: (0, 0))
    return _call(body, name=name, grid=(T // tm,), side=side, in_specs=[row, one, row, row], out_specs=[row, one],
                 out_shape=[_sds((T, D), F32), _sds((1, D), F32)], args=(x, g, dh, dres))


def _mm_tn(a, b, *, bm, bn, bt, name, b_slabs=False, side=None, rows=None, m_off=0, into=None, a_rows=False):
    nz, T, M = (a.shape[0], a.shape[2], a.shape[1]) if a_rows else a.shape
    if b_slabs:
        N = b.shape[1] // NSLAB
        assert bt == T // NSLAB
        b_spec = pl.BlockSpec((bt, bn), lambda n, z, m, t: (0, t * (N // bn) + n))
    else:
        N = b.shape[1]
        b_spec = pl.BlockSpec((bt, bn), lambda n, z, m, t: (t, n))
    assert M % bm == 0 and N % bn == 0 and T % bt == 0, (M, bm, N, bn, T, bt)

    def body(a_ref, b_ref, *rest):
        c_ref = rest[-1]

        @pl.when(pl.program_id(3) == 0)
        def _():
            c_ref[...] = jnp.zeros_like(c_ref)

        ab, bb = a_ref[...].astype(BF16), b_ref[...].astype(BF16)
        c_ref[...] += _nn(ab, bb) if a_rows else _tn(ab, bb)

    grid = (N // bn, nz, M // bm, T // bt)
    a_spec = (pl.BlockSpec((None, bm, bt), lambda n, z, m, t: (z, m, t)) if a_rows
              else pl.BlockSpec((None, bt, bm), lambda n, z, m, t: (z, t, m)))
    in_specs = [a_spec, b_spec]
    out_spec = pl.BlockSpec((None, bm, bn), lambda n, z, m, t: (z, m + m_off, n))
    out_shape = _sds((nz, M if rows is None else rows, N), F32)
    if into is not None:
        assert side is None and into.shape == out_shape.shape
        return pl.pallas_call(body, name=name, grid=grid, in_specs=in_specs + [ANY], out_specs=out_spec,
                              out_shape=out_shape, input_output_aliases={2: 0}, compiler_params=_cparams())(a, b, into)
    res = _call(body, name=name, grid=grid, side=side, in_specs=in_specs, out_specs=[out_spec],
                out_shape=[out_shape], args=(a, b))
    return res[0] if side is None else (res[0][0], res[1])


def _proj(x, g, wt, gains, modes, *, tn, w_off, slabs, tm, normed_dtype, name):
    T, D = x.shape
    ntile = len(modes)
    N = ntile * tn
    n16 = T // NSLAB
    if slabs:
        assert tm == n16
        x_in = x.reshape(n16, NSLAB * D)
        x_spec = pl.BlockSpec((tm, D), lambda i, n: (0, i))
        oshape = lambda c: (NSLAB, n16, c)
        ospec = lambda bc, cm: pl.BlockSpec((None, tm, bc), lambda i, n: (i, 0, cm(n)))
    else:
        x_in = x
        x_spec = pl.BlockSpec((tm, D), lambda i, n: (i, 0))
        oshape = lambda c: (T, c)
        ospec = lambda bc, cm: pl.BlockSpec((tm, bc), lambda i, n: (i, cm(n)))

    def body(x_ref, g_ref, w_ref, gains_ref, raw_ref, nrm_ref, h_ref):
        n = pl.program_id(1)

        @pl.when(n == 0)
        def _():
            xv = x_ref[...]
            r = lax.rsqrt(jnp.mean(xv * xv, axis=-1, keepdims=True) + RMS_EPS)
            h_ref[...] = (xv * r * g_ref[...]).astype(BF16)

        y = _nt(h_ref[...], w_ref[...])
        raw_ref[...] = y
        for t, mode in enumerate(modes):
            @pl.when(n == t)
            def _(t=t, mode=mode):
                if not mode:
                    nrm_ref[...] = y.astype(nrm_ref.dtype)
                    return
                gain = gains_ref[t]
                for k in range(tn // HEAD):
                    yk = y[:, k * HEAD:(k + 1) * HEAD]
                    r = lax.rsqrt(jnp.mean(yk * yk, axis=-1, keepdims=True) + RMS_EPS)
                    nrm_ref[:, k * HEAD:(k + 1) * HEAD] = (yk * r * gain).astype(nrm_ref.dtype)

    return pl.pallas_call(
        body, name=name, grid=(T // tm, ntile),
        in_specs=[x_spec, pl.BlockSpec((1, D), lambda i, n: (0, 0)),
                  pl.BlockSpec((tn, D), lambda i, n: (n + w_off, 0)),
                  pl.BlockSpec((ntile, 1, HEAD), lambda i, n: (0, 0, 0))],
        out_specs=[ospec(tn, lambda n: n), ospec(tn, lambda n: n), ospec(D, lambda n: 0)],
        out_shape=[_sds(oshape(N), F32), _sds(oshape(N), normed_dtype), _sds(oshape(D), BF16)],
        compiler_params=_cparams(),
    )(x_in, g, wt, gains)


def _mm(a, w, *, nt, tk, tm, a_layout, out_layout, resid=None, name, w_off=0, n_out=None):
    if a_layout == "slab":
        T, K = a.shape[0] * a.shape[1], a.shape[2]
    else:
        T, K = a.shape
    N = (w.shape[0] if nt else w.shape[1]) if n_out is None else n_out
    n16 = T // NSLAB
    nk = K // tk

    def spec(layout, C, bc, colmap):
        if layout == "nat":
            return pl.BlockSpec((tm, bc), lambda i, k: (i, colmap(k)))
        assert tm == n16
        if layout == "slab":
            return pl.BlockSpec((None, tm, bc), lambda i, k: (i, 0, colmap(k)))
        assert bc == C
        return pl.BlockSpec((tm, C), lambda i, k: (0, i))

    a_in = a.reshape(n16, NSLAB * K) if a_layout == "view" else a
    w_spec = (pl.BlockSpec((N, tk), lambda i, k: (w_off, k)) if nt
              else pl.BlockSpec((tk, N), lambda i, k: (k + w_off, 0)))
    o_spec = spec(out_layout, N, N, lambda k: 0)
    oshape = {"nat": (T, N), "slab": (NSLAB, n16, N), "view": (n16, NSLAB * N)}[out_layout]
    has_resid = resid is not None

    def body(*refs):
        a_ref, w_ref = refs[0], refs[1]
        o_ref = refs[-1]
        k = pl.program_id(1)

        @pl.when(k == 0)
        def _():
            o_ref[...] = refs[2][...] if has_resid else jnp.zeros_like(o_ref)

        ab = a_ref[...].astype(BF16)
        o_ref[...] += _nt(ab, w_ref[...]) if nt else _nn(ab, w_ref[...])

    ins = [a_in, w]
    in_specs = [spec(a_layout, K, tk, lambda k: k), w_spec]
    if has_resid:
        ins.append(resid.reshape(n16, NSLAB * N) if out_layout == "view" else resid)
        in_specs.append(o_spec)
    out = pl.pallas_call(
        body, name=name, grid=(T // tm, nk), in_specs=in_specs, out_specs=o_spec,
        out_shape=_sds(oshape, F32), compiler_params=_cparams(),
    )(*ins)
    return out.reshape(T, N) if out_layout == "view" else out


def _log_sigmoid(z):
    return jnp.minimum(z, 0.0) - jnp.log(1.0 + jnp.exp(-jnp.abs(z)))


def _fox_gate_fwd(f_raw, fbias, name):
    T = f_raw.shape[0]
    cb = 256

    def body(f_ref, b_ref, c_ref):
        row = lax.broadcasted_iota(jnp.int32, (cb, cb), 0)
        col = lax.broadcasted_iota(jnp.int32, (cb, cb), 1)
        tri = (col <= row).astype(F32)
        carry = jnp.zeros((1, HEAD), F32)
        for i in range(T // cb):
            lf = _log_sigmoid(f_ref[i * cb:(i + 1) * cb, :] + b_ref[...])
            c = jnp.dot(tri, lf, preferred_element_type=F32, precision=lax.Precision.HIGHEST) + carry
            c_ref[i * cb:(i + 1) * cb, :] = c
            carry = c[cb - 1:cb, :]

    return pl.pallas_call(body, name=name, out_shape=_sds((T, HEAD), F32), compiler_params=_cparams())(f_raw, fbias)


def _fox_gate_bwd(f_raw, fbias, dc, name):
    T = f_raw.shape[0]
    cb = 256

    def body(f_ref, b_ref, dc_ref, df_ref, db_ref):
        row = lax.broadcasted_iota(jnp.int32, (cb, cb), 0)
        col = lax.broadcasted_iota(jnp.int32, (cb, cb), 1)
        tri = (col >= row).astype(F32)
        carry = jnp.zeros((1, HEAD), F32)
        dbias = jnp.zeros((1, HEAD), F32)
        for i in reversed(range(T // cb)):
            dlf = jnp.dot(tri, dc_ref[i * cb:(i + 1) * cb, :], preferred_element_type=F32,
                          precision=lax.Precision.HIGHEST) + carry
            carry = dlf[0:1, :]
            z = f_ref[i * cb:(i + 1) * cb, :] + b_ref[...]
            df = dlf * jax.nn.sigmoid(-z)
            df_ref[i * cb:(i + 1) * cb, :] = df
            dbias = dbias + jnp.sum(df, axis=0, keepdims=True)
        db_ref[...] = dbias

    return pl.pallas_call(body, name=name, out_shape=[_sds((T, HEAD), F32), _sds((1, HEAD), F32)],
                          compiler_params=_cparams())(f_raw, fbias, dc)


def _fox_fwd(qkv, c_col, c_row, tq, name, side=None):
    T = qkv.shape[0]
    H = qkv.shape[1] // (3 * HEAD)
    nq = T // tq
    c_blocks = c_row.reshape(H, nq, 1, tq)

    def body(q_ref, k_ref, v_ref, cq_ref, ck_ref, o_ref, lse_ref):
        qi = pl.program_id(1)
        q, cq = q_ref[...], cq_ref[...]
        causal = lax.broadcasted_iota(jnp.int32, (tq, tq), 1) <= lax.broadcasted_iota(jnp.int32, (tq, tq), 0)

        def key_block(ki, carry, diagonal):
            m, l, acc = carry
            rows = pl.ds(pl.multiple_of(ki * tq, tq), tq)
            s = _nt(q, k_ref[rows, :]) * SCALE + cq - ck_ref[ki]
            if diagonal:
                s = jnp.where(causal, s, NEG)
            m_new = jnp.maximum(m, jnp.max(s, axis=-1, keepdims=True))
            alpha = jnp.exp(m - m_new)
            p = jnp.exp(s - m_new)
            l = alpha * l + jnp.sum(p, axis=-1, keepdims=True)
            acc = alpha * acc + _nn(p.astype(BF16), v_ref[rows, :])
            return m_new, l, acc

        init = (jnp.full((tq, 1), NEG, F32), jnp.zeros((tq, 1), F32), jnp.zeros((tq, HEAD), F32))
        carry = lax.fori_loop(0, qi, lambda ki, c: key_block(ki, c, False), init)
        m, l, acc = key_block(qi, carry, True)
        o_ref[...] = acc / l
        lse_ref[...] = m + jnp.log(l)

    return _call(
        body, name=name, grid=(H, nq), side=side,
        in_specs=[pl.BlockSpec((tq, HEAD), lambda h, qi: (qi, h)),
                  pl.BlockSpec((T, HEAD), lambda h, qi: (0, H + h)),
                  pl.BlockSpec((T, HEAD), lambda h, qi: (0, 2 * H + h)),
                  pl.BlockSpec((None, tq, 1), lambda h, qi: (h, qi, 0)),
                  pl.BlockSpec((None, nq, 1, tq), lambda h, qi: (h, 0, 0, 0))],
        out_specs=[pl.BlockSpec((tq, HEAD), lambda h, qi: (qi, h)),
                   pl.BlockSpec((None, tq, 1), lambda h, qi: (h, qi, 0))],
        out_shape=[_sds((T, H * HEAD), F32), _sds((H, T, 1), F32)],
        args=(qkv, qkv, qkv, c_col, c_blocks))


def _fox_bwd(qkv, c_col, c_row, out, dout, lse, tq, name, side=None):
    T = qkv.shape[0]
    H = qkv.shape[1] // (3 * HEAD)
    nq = T // tq

    def body(q_ref, k_ref, v_ref, cq_ref, ck_ref, o_ref, do_ref, lse_ref, dq_ref, dk_ref, dv_ref, dck_ref, dcq_ref,
             delta_s):
        ki = pl.program_id(1)

        @pl.when(ki == 0)
        def _():
            dq_ref[...] = jnp.zeros_like(dq_ref)
            dcq_ref[...] = jnp.zeros_like(dcq_ref)
            delta_s[...] = jnp.sum(do_ref[...] * o_ref[...], axis=-1, keepdims=True)

        k, v, ck = k_ref[...], v_ref[...], ck_ref[...]
        causal = lax.broadcasted_iota(jnp.int32, (tq, tq), 1) <= lax.broadcasted_iota(jnp.int32, (tq, tq), 0)

        def query_block(qi, carry, diagonal):
            dk, dv, dck = carry
            rows = pl.ds(pl.multiple_of(qi * tq, tq), tq)
            q = q_ref[rows, :]
            s = _nt(q, k) * SCALE + cq_ref[rows, :] - ck
            if diagonal:
                s = jnp.where(causal, s, NEG)
            p = jnp.exp(s - lse_ref[rows, :])
            dob = do_ref[rows, :].astype(BF16)
            ds = p * (_nt(dob, v) - delta_s[rows, :])
            dsb = ds.astype(BF16)
            dq_ref[rows, :] += _nn(dsb, k) * SCALE
            dcq_ref[rows, :] += jnp.sum(ds, axis=-1, keepdims=True)
            return dk + _tn(dsb, q), dv + _tn(p.astype(BF16), dob), dck - jnp.sum(ds, axis=0, keepdims=True)

        init = (jnp.zeros((tq, HEAD), F32), jnp.zeros((tq, HEAD), F32), jnp.zeros((1, tq), F32))
        carry = query_block(ki, init, True)
        dk, dv, dck = lax.fori_loop(ki + 1, nq, lambda qi, c: query_block(qi, c, False), carry)
        dk_ref[...] = dk * SCALE
        dv_ref[...] = dv
        dck_ref[...] = dck

    head = lambda off: pl.BlockSpec((T, HEAD), lambda h, ki: (0, off + h))
    col = pl.BlockSpec((None, T, 1), lambda h, ki: (h, 0, 0))
    return _call(
        body, name=name, grid=(H, nq), side=side,
        in_specs=[head(0),
                  pl.BlockSpec((tq, HEAD), lambda h, ki: (ki, H + h)),
                  pl.BlockSpec((tq, HEAD), lambda h, ki: (ki, 2 * H + h)),
                  col, pl.BlockSpec((None, 1, tq), lambda h, ki: (h, 0, ki)), head(0), head(0), col],
        out_specs=[head(0),
                   pl.BlockSpec((tq, HEAD), lambda h, ki: (ki, h)),
                   pl.BlockSpec((tq, HEAD), lambda h, ki: (ki, h)),
                   pl.BlockSpec((None, 1, tq), lambda h, ki: (h, 0, ki)), col],
        out_shape=[_sds((T, H * HEAD), F32), _sds((T, H * HEAD), F32), _sds((T, H * HEAD), F32), _sds((H, 1, T), F32),
                   _sds((H, T, 1), F32)],
        scratch_shapes=[pltpu.VMEM((T, 1), F32)],
        args=(qkv, qkv, qkv, c_col, c_row, out, dout, lse))


def _t5_bucket(dist):
    max_exact = NUM_BUCKETS // 2
    d = dist.astype(np.float32)
    large = max_exact + (np.log(np.maximum(d, np.float32(1.0)) / np.float32(max_exact))
                         / np.float32(math.log(MAX_DISTANCE / max_exact))
                         * np.float32(NUM_BUCKETS - max_exact)).astype(np.int32)
    large = np.minimum(large, NUM_BUCKETS - 1)
    return np.where(dist < max_exact, dist, large)


def _bucket_maps():
    maps = []
    for d in DILATIONS:
        e = NSLAB // d
        rows = BLK // e
        idx = np.arange(BLK)
        pos = e * (idx % rows) + idx // rows
        qpos = pos[:, None] + BLK
        kpos = np.concatenate([pos, pos + BLK])[None, :]
        delta = qpos - kpos
        band = (delta >= 0) & (delta <= BLK)
        bucket = _t5_bucket(np.clip(delta, 0, None) * d)
        maps.append(np.where(band, bucket, -1).astype(np.int32))
    return np.stack(maps)


def _dil_geometry(T):
    n16 = T // NSLAB
    geo = []
    for d in DILATIONS:
        e = NSLAB // d
        rows = BLK // e
        nblk = n16 // rows
        geo.append((d, e, rows, nblk))
    return geo


DIL_INTERLEAVE_FWD = {1: 4, 4: 8, 16: 8}
DIL_INTERLEAVE_BWD = {1: 8, 4: 8, 16: 8}


def _dil_interleave(per_step, nblocks):
    while per_step > 1 and (nblocks % per_step or nblocks // per_step < 2):
        per_step -= 1
    return per_step


def _dil_bias(tab_ref, bkt_ref, bias_s, h):
    for p in range(len(DILATIONS)):
        bk = bkt_ref[p]
        bias = jnp.full((BLK, 2 * BLK), NEG, F32)
        for b in range(NUM_BUCKETS):
            bias = jnp.where(bk == b, tab_ref[b, h], bias)
        bias_s[p] = bias


def _dil_rows(d, e, rows, sub, blk):
    start = pl.multiple_of(blk * rows, rows)
    return [(sub + d * j, pl.ds(start, rows)) for j in range(e)]


def _gather(ref, idx):
    return jnp.concatenate([ref[s, r, :] for s, r in idx], axis=0)


def _scatter(ref, idx, val, rows):
    for j, (s, r) in enumerate(idx):
        ref[s, r, :] = val[j * rows:(j + 1) * rows]


def _scatter_add(ref, idx, val, rows):
    for j, (s, r) in enumerate(idx):
        ref[s, r, :] += val[j * rows:(j + 1) * rows]


def _dil_fwd(qkv, table, name, side=None):
    n16 = qkv.shape[1]
    T = NSLAB * n16
    H = qkv.shape[2] // (3 * HEAD)
    geo = _dil_geometry(T)
    bkt = jnp.asarray(_bucket_maps())

    def body(tab_ref, bkt_ref, q_ref, k_ref, v_ref, o_ref, lse_ref, bias_s, m_s, l_s):
        h = pl.program_id(0)
        _dil_bias(tab_ref, bkt_ref, bias_s, h)
        first_mask = lax.broadcasted_iota(jnp.int32, (BLK, 2 * BLK), 1) < BLK

        starts = len(DILATIONS) - 1

        def load(p, d, e, rows, sub, blk):
            cur = _dil_rows(d, e, rows, sub, blk)
            prev = _dil_rows(d, e, rows, sub, jnp.maximum(blk - 1, 0))
            q = _gather(q_ref, cur).astype(BF16)
            kk = jnp.concatenate([_gather(k_ref, prev), _gather(k_ref, cur)], axis=0).astype(BF16)
            vv = jnp.concatenate([_gather(v_ref, prev), _gather(v_ref, cur)], axis=0).astype(BF16)
            old = None if p == starts else (_gather(m_s, cur), _gather(l_s, cur), _gather(o_ref, cur))
            return cur, blk, q, kk, vv, old

        def compute(p, blk, q, kk, vv, old):
            s = _nt(q, kk) * SCALE + bias_s[p]
            s = jnp.where(first_mask & (blk == 0), NEG, s)
            m_blk = jnp.max(s, axis=-1, keepdims=True)
            if old is None:
                m_new = m_blk
                pr = jnp.exp(s - m_new)
                l_new = jnp.sum(pr, axis=-1, keepdims=True)
                acc = _nn(pr.astype(BF16), vv)
            else:
                m_old, l_old, acc_old = old
                m_new = jnp.maximum(m_old, m_blk)
                alpha = jnp.exp(m_old - m_new)
                pr = jnp.exp(s - m_new)
                l_new = alpha * l_old + jnp.sum(pr, axis=-1, keepdims=True)
                acc = alpha * acc_old + _nn(pr.astype(BF16), vv)
            if p == 0:
                return acc / l_new, m_new + jnp.log(l_new), None
            return acc, m_new, l_new

        def store(p, rows, cur, acc, m_new, l_new):
            _scatter(o_ref, cur, acc, rows)
            if p == 0:
                _scatter(lse_ref, cur, m_new, rows)
            else:
                _scatter(m_s, cur, m_new, rows)
                _scatter(l_s, cur, l_new, rows)

        for p in reversed(range(len(DILATIONS))):
            d, e, rows, nblk = geo[p]
            per_step = _dil_interleave(DIL_INTERLEAVE_FWD[d], d * nblk)

            def step(i, carry, p=p, d=d, e=e, rows=rows, nblk=nblk, per_step=per_step):
                ids = [i + u * (d * nblk // per_step) for u in range(per_step)]
                loaded = [load(p, d, e, rows, j // nblk, j % nblk) for j in ids]
                done = [(cur, compute(p, blk, q, kk, vv, old)) for cur, blk, q, kk, vv, old in loaded]
                for cur, res in done:
                    store(p, rows, cur, *res)
                return carry

            lax.fori_loop(0, d * nblk // per_step, step, 0)

    head = lambda off: pl.BlockSpec((NSLAB, n16, HEAD), lambda h: (0, 0, off + h))
    return _call(
        body, name=name, grid=(H,), side=side,
        in_specs=[pl.BlockSpec(memory_space=pltpu.SMEM), pl.BlockSpec((3, BLK, 2 * BLK), lambda h: (0, 0, 0)),
                  head(0), head(H), head(2 * H)],
        out_specs=[head(0), pl.BlockSpec((None, NSLAB, n16, 1), lambda h: (h, 0, 0, 0))],
        out_shape=[_sds((NSLAB, n16, H * HEAD), F32), _sds((H, NSLAB, n16, 1), F32)],
        scratch_shapes=[pltpu.VMEM((3, BLK, 2 * BLK), F32), pltpu.VMEM((NSLAB, n16, 1), F32),
                        pltpu.VMEM((NSLAB, n16, 1), F32)],
        args=(table, bkt, qkv, qkv, qkv))


def _dil_bwd(qkv, table, out, dout, lse, name, side=None):
    n16 = qkv.shape[1]
    T = NSLAB * n16
    H = qkv.shape[2] // (3 * HEAD)
    geo = _dil_geometry(T)
    bkt = jnp.asarray(_bucket_maps())

    def body(tab_ref, bkt_ref, q_ref, k_ref, v_ref, o_ref, do_ref, lse_ref,
             dq_ref, dk_ref, dv_ref, dtab_ref, bias_s, dbias_s, delta_s):
        h = pl.program_id(0)
        _dil_bias(tab_ref, bkt_ref, bias_s, h)
        first_mask = lax.broadcasted_iota(jnp.int32, (BLK, 2 * BLK), 1) < BLK
        dbias_s[...] = jnp.zeros_like(dbias_s)
        dq_ref[...] = jnp.zeros_like(dq_ref)
        dk_ref[...] = jnp.zeros_like(dk_ref)
        dv_ref[...] = jnp.zeros_like(dv_ref)
        for r in range(NSLAB):
            delta_s[r] = jnp.sum(do_ref[r] * o_ref[r], axis=-1, keepdims=True)

        def load(d, e, rows, sub, blk):
            cur = _dil_rows(d, e, rows, sub, blk)
            prev = _dil_rows(d, e, rows, sub, jnp.maximum(blk - 1, 0))
            q = _gather(q_ref, cur).astype(BF16)
            kk = jnp.concatenate([_gather(k_ref, prev), _gather(k_ref, cur)], axis=0).astype(BF16)
            vv = jnp.concatenate([_gather(v_ref, prev), _gather(v_ref, cur)], axis=0).astype(BF16)
            dob = _gather(do_ref, cur).astype(BF16)
            return cur, prev, blk, q, kk, vv, dob, _gather(lse_ref, cur), _gather(delta_s, cur)

        def compute(p, blk, q, kk, vv, dob, lse, delta):
            s = _nt(q, kk) * SCALE + bias_s[p]
            s = jnp.where(first_mask & (blk == 0), NEG, s)
            pr = jnp.exp(s - lse)
            ds = pr * (_nt(dob, vv) - delta)
            dsb = ds.astype(BF16)
            return ds, _nn(dsb, kk) * SCALE, _tn(dsb, q) * SCALE, _tn(pr.astype(BF16), dob)

        def store(rows, cur, prev, dq, dkk, dvv):
            _scatter_add(dq_ref, cur, dq, rows)
            _scatter_add(dk_ref, prev, dkk[:BLK], rows)
            _scatter_add(dk_ref, cur, dkk[BLK:], rows)
            _scatter_add(dv_ref, prev, dvv[:BLK], rows)
            _scatter_add(dv_ref, cur, dvv[BLK:], rows)

        for p in range(len(DILATIONS)):
            d, e, rows, nblk = geo[p]
            per_step = _dil_interleave(DIL_INTERLEAVE_BWD[d], d * nblk)

            def step(i, carry, p=p, d=d, e=e, rows=rows, nblk=nblk, per_step=per_step):
                ids = [i + u * (d * nblk // per_step) for u in range(per_step)]
                loaded = [load(d, e, rows, j // nblk, j % nblk) for j in ids]
                done = [(cur, prev, compute(p, *rest)) for cur, prev, *rest in loaded]
                dbias_s[p] += functools.reduce(jnp.add, [res[0] for _, _, res in done])
                for cur, prev, res in done:
                    store(rows, cur, prev, *res[1:])
                return carry

            lax.fori_loop(0, d * nblk // per_step, step, 0)

        lane = lax.broadcasted_iota(jnp.int32, (1, HEAD), 1)
        row = jnp.zeros((1, HEAD), F32)
        for b in range(NUM_BUCKETS):
            tot = jnp.zeros((1, 1), F32)
            for p in range(len(DILATIONS)):
                hit = jnp.where(bkt_ref[p] == b, dbias_s[p], 0.0)
                tot = tot + jnp.sum(jnp.sum(hit, axis=0, keepdims=True), axis=1, keepdims=True)
            row = jnp.where(lane == b, tot, row)
        dtab_ref[...] = row

    head = lambda off: pl.BlockSpec((NSLAB, n16, HEAD), lambda h: (0, 0, off + h))
    return _call(
        body, name=name, grid=(H,), side=side,
        in_specs=[pl.BlockSpec(memory_space=pltpu.SMEM), pl.BlockSpec((3, BLK, 2 * BLK), lambda h: (0, 0, 0)),
                  head(0), head(H), head(2 * H), head(0), head(0),
                  pl.BlockSpec((None, NSLAB, n16, 1), lambda h: (h, 0, 0, 0))],
        out_specs=[head(0), head(0), head(0), pl.BlockSpec((None, 1, HEAD), lambda h: (h, 0, 0))],
        out_shape=[_sds((NSLAB, n16, H * HEAD), F32)] * 3 + [_sds((H, 1, HEAD), F32)],
        scratch_shapes=[pltpu.VMEM((3, BLK, 2 * BLK), F32), pltpu.VMEM((3, BLK, 2 * BLK), F32),
                        pltpu.VMEM((NSLAB, n16, 1), F32)],
        args=(table, bkt, qkv, qkv, qkv, out, dout, lse))


def _qknorm_bwd(raw, dq, dk, dv, gains, tm, name):
    T, N = raw.shape
    C = N // 3

    def body(raw_ref, dq_ref, dk_ref, dv_ref, gains_ref, dp_ref, dg_ref):
        @pl.when(pl.program_id(0) == 0)
        def _():
            dg_ref[...] = jnp.zeros_like(dg_ref)

        for t, d_ref in enumerate((dq_ref, dk_ref)):
            gain = gains_ref[t]
            dgain = jnp.zeros((1, HEAD), F32)
            for k in range(C // HEAD):
                y = raw_ref[:, t * C + k * HEAD:t * C + (k + 1) * HEAD]
                dn = d_ref[:, k * HEAD:(k + 1) * HEAD]
                r = lax.rsqrt(jnp.mean(y * y, axis=-1, keepdims=True) + RMS_EPS)
                yhat = y * r
                gd = dn * gain
                dy = r * (gd - yhat * jnp.mean(gd * yhat, axis=-1, keepdims=True))
                dp_ref[:, t * C + k * HEAD:t * C + (k + 1) * HEAD] = dy.astype(BF16)
                dgain = dgain + jnp.sum(dn * yhat, axis=0, keepdims=True)
            dg_ref[t] += dgain
        dp_ref[:, 2 * C:] = dv_ref[...].astype(BF16)

    third = pl.BlockSpec((tm, C), lambda i: (i, 0))
    return pl.pallas_call(
        body, name=name, grid=(T // tm,),
        in_specs=[pl.BlockSpec((tm, N), lambda i: (i, 0)), third, third, third,
                  pl.BlockSpec((2, 1, HEAD), lambda i: (0, 0, 0))],
        out_specs=[pl.BlockSpec((tm, N), lambda i: (i, 0)), pl.BlockSpec((2, 1, HEAD), lambda i: (0, 0, 0))],
        out_shape=[_sds((T, N), BF16), _sds((2, 1, HEAD), F32)], compiler_params=_cparams(),
    )(raw, dq, dk, dv, gains)


def _loss_grad(y, target, tm, name):
    T, D = y.shape

    def body(y_ref, t_ref, dy_ref, loss_ref):
        @pl.when(pl.program_id(0) == 0)
        def _():
            loss_ref[...] = jnp.zeros_like(loss_ref)

        err = y_ref[...] - t_ref[...]
        dy_ref[...] = err * (1.0 / D)
        per_tok = jnp.mean(err * err, axis=-1, keepdims=True)
        tot = 0.5 * jnp.sum(per_tok, axis=0, keepdims=True)
        lane = lax.broadcasted_iota(jnp.int32, (1, HEAD), 1)
        loss_ref[...] += jnp.where(lane == 0, tot, 0.0)

    row = pl.BlockSpec((tm, D), lambda i: (i, 0))
    return pl.pallas_call(
        body, name=name, grid=(T // tm,), in_specs=[row, row],
        out_specs=[row, pl.BlockSpec((1, HEAD), lambda i: (0, 0))],
        out_shape=[_sds((T, D), F32), _sds((1, HEAD), F32)], compiler_params=_cparams(),
    )(y, target)


def _pad_lanes(v, width=HEAD):
    return jnp.pad(v, ((0, 0), (0, width - v.shape[1])))


def _local_step(x, target, small, wts, plan=None):
    grads = {}

    def hosted(host, fn, *args, **kw):
        side = plan.before(host, wts, grads) if plan is not None else None
        if side is None:
            return fn(*args, name=host, **kw)
        res, side_res = fn(*args, name=host, side=side, **kw)
        plan.after(host, side_res, wts, grads)
        return res

    T, D = x.shape
    C = D // 2
    H = C // HEAD
    n16 = T // NSLAB
    tm = min(512, T)
    tq = min(512, T)
    bn = min(1024, D)
    g1, gm, g2 = small["ffn1_norm"], small["mix_norm"], small["ffn2_norm"]
    gains_a = jnp.stack([small["q_norm_a"], small["k_norm_a"], jnp.ones_like(small["q_norm_a"])])
    gains_b = jnp.stack([small["q_norm_b"], small["k_norm_b"], jnp.ones_like(small["q_norm_b"])])
    fbias = _pad_lanes(small["forget_bias"])
    table = small["rel_bias_table"]

    x1, h1, gu1, act1 = hosted("ffn1_fwd", _ffn_fwd, x, g1, wts["ffn1_in"], wts["ffn1_out"], tm)
    w_in_t, w_f_t, w_o = wts["w_in_t"], wts["w_f_t"], wts["w_o"]
    raw_a, nrm_a, h2a = _proj(x1, gm, w_in_t, gains_a, (True, True, False), tn=C, w_off=0, slabs=True, tm=n16,
                              normed_dtype=F32, name="proj_a")
    raw_b, nrm_b, h2b = _proj(x1, gm, w_in_t, gains_b, (True, True, False), tn=C, w_off=3, slabs=False, tm=tm,
                              normed_dtype=BF16, name="proj_b")
    f_raw, _, _ = _proj(x1, gm, w_f_t, gains_b[:1], (False,), tn=HEAD, w_off=0, slabs=False, tm=tm,
                        normed_dtype=BF16, name="proj_f")
    c = _fox_gate_fwd(f_raw, fbias, "fox_gate_fwd")
    c_heads = c[:, :H].T
    c_col, c_row = c_heads[:, :, None], c_heads[:, None, :]
    out_a, lse_a = hosted("dil_fwd", _dil_fwd, nrm_a, table)
    out_b, lse_b = hosted("fox_fwd", _fox_fwd, nrm_b, c_col, c_row, tq)
    x2a = _mm(out_a, w_o, nt=False, tk=C, tm=n16, a_layout="slab", out_layout="view", resid=x1, name="out_a")
    x2 = _mm(out_b, w_o, nt=False, tk=C, tm=tm, a_layout="nat", out_layout="nat", resid=x2a, w_off=1, name="out_b")
    y, h3, gu3, act3 = _ffn_fwd(x2, g2, wts["ffn2_in"], wts["ffn2_out"], tm, "ffn2_fwd")
    dy, loss_row = _loss_grad(y, target, tm, "loss_grad")

    def ffn_backward(tag, xin, g, h, gu, act, win, wout, dres):
        nc, tf = wout.shape[0], wout.shape[1]
        dh, dgu, dyb = hosted(tag + "_bwd", _ffn_bwd, dres, gu, win, wout, tm)
        grads[tag + "_w_in_t"] = hosted(tag + "_dwin", _mm_tn, dgu.reshape(2 * nc, tf, T), h, bm=tf, bn=bn, bt=T,
                                        a_rows=True)
        dxin, grads[tag + "_norm"] = hosted(tag + "_rms_bwd", _rms_bwd, xin, g, dh, dres, tm)
        grads[tag + "_w_out"] = hosted(tag + "_dwout", _mm_tn, act, dyb, bm=tf, bn=bn, bt=T, a_rows=True)
        return dxin

    dx2 = ffn_backward("ffn2", x2, g2, h3, gu3, act3, wts["ffn2_in"], wts["ffn2_out"], dy)

    dmix_a = _mm(dx2, w_o, nt=True, tk=D, tm=n16, a_layout="view", out_layout="slab", n_out=C, name="dmix_a")
    dmix_b = _mm(dx2, w_o, nt=True, tk=D, tm=tm, a_layout="nat", out_layout="nat", n_out=C, w_off=1, name="dmix_b")
    dwo = _mm_tn(out_a.reshape(1, T, C), dx2.reshape(n16, NSLAB * D), bm=C, bn=bn, bt=n16, b_slabs=True,
                 rows=2 * C, name="dwo_a")
    dwo = _mm_tn(out_b.reshape(1, T, C), dx2, bm=C, bn=bn, bt=tm, rows=2 * C, m_off=1, into=dwo, name="dwo_b")
    grads["w_out"] = dwo[0]

    dqa, dka, dva, dtab = hosted("dil_bwd", _dil_bwd, nrm_a, table, out_a, dmix_a, lse_a)
    dqb, dkb, dvb, dck, dcq = hosted("fox_bwd", _fox_bwd, nrm_b, c_col, c_row, out_b, dmix_b, lse_b, tq)
    grads["rel_bias_table"] = dtab[:, 0, :NUM_BUCKETS].T
    dc = _pad_lanes((dck[:, 0, :] + dcq[:, :, 0]).T)
    df, dfb = _fox_gate_bwd(f_raw, fbias, dc, "fox_gate_bwd")
    grads["forget_bias"] = dfb[:, :H]

    flat = lambda a: a.reshape(T, a.shape[-1])
    dproj_a, dgain_a = _qknorm_bwd(flat(raw_a), flat(dqa), flat(dka), flat(dva), gains_a[:2], min(256, T), "qknorm_bwd_a")
    dproj_b, dgain_b = _qknorm_bwd(raw_b, dqb, dkb, dvb, gains_b[:2], min(256, T), "qknorm_bwd_b")
    grads["q_norm_a"], grads["k_norm_a"] = dgain_a[0], dgain_a[1]
    grads["q_norm_b"], grads["k_norm_b"] = dgain_b[0], dgain_b[1]
    dproj_a = dproj_a.reshape(NSLAB, n16, 3 * C)

    dh2 = _mm(dproj_a, w_in_t, nt=False, tk=C, tm=n16, a_layout="slab", out_layout="view", name="dh2_a")
    dh2 = _mm(dproj_b, w_in_t, nt=False, tk=C, tm=tm, a_layout="nat", out_layout="nat", resid=dh2, w_off=3, name="dh2_b")
    dh2 = _mm(df, w_f_t, nt=False, tk=HEAD, tm=tm, a_layout="nat", out_layout="nat", resid=dh2, name="dh2_f")
    dx1, grads["mix_norm"] = _rms_bwd(x1, gm, dh2, dx2, tm, "mix_rms_bwd")
    bt = min(2048, T)
    dwt = _mm_tn(flat(dproj_a)[None], flat(h2a), bm=C, bn=bn, bt=bt, rows=6 * C + H, name="dw_a")
    dwt = _mm_tn(dproj_b[None], h2b, bm=C, bn=bn, bt=bt, rows=6 * C + H, m_off=3, into=dwt, name="dw_b")
    dwt = _mm_tn(df[None, :, :H], h2b, bm=H, bn=bn, bt=bt, rows=6 * C + H, m_off=6 * C // H, into=dwt, name="dw_f")
    grads["w_in_t"] = dwt[0]

    grad_x = ffn_backward("ffn1", x, g1, h1, gu1, act1, wts["ffn1_in"], wts["ffn1_out"], dx1)
    return loss_row, grad_x, grads


def _place():
    x, y, c = lax.axis_index("x"), lax.axis_index("y"), lax.axis_index("c")
    other_chips = [(1 - x, y), (x, 1 - y), (1 - x, 1 - y)]
    return x, y, c, other_chips


def _run_side(side, name):
    def body(*refs):
        si, so = len(side.ins), len(side.outs)
        side.start(refs[:si], refs[si:si + so], refs[si + so:])
        side.finish(refs[:si], refs[si:si + so], refs[si + so:])

    return pl.pallas_call(body, name=name, in_specs=[ANY] * len(side.ins), out_specs=[ANY] * len(side.outs),
                          out_shape=side.outs, scratch_shapes=side.sems)(*side.ins)


def _all_gather(shards):
    n = len(shards)

    def plan(ins, outs, sems):
        send_sems, recv_sems, local_sems = sems
        x, y, c, chips = _place()
        me, sibling = (x, y, c), (x, y, 1 - c)

        def copy(a, k, block, to, src=None):
            px, py, pc = block
            dst = outs[a].at[4 * px + 2 * py + pc]
            return pltpu.make_async_remote_copy(
                src_ref=dst if src is None else src, dst_ref=dst, send_sem=send_sems.at[7 * a + k],
                recv_sem=recv_sems.at[7 * a + k], device_id=to, device_id_type=MESH)

        mine = [pltpu.make_async_copy(ins[a], outs[a].at[4 * x + 2 * y + c], local_sems.at[a]) for a in range(n)]
        first = []
        for a in range(n):
            first.append(copy(a, 0, me, sibling, src=ins[a]))
            first += [copy(a, 1 + j, me, (*chip, c), src=ins[a]) for j, chip in enumerate(chips)]
        return copy, mine, first, me, sibling, c, chips

    def start(ins, outs, sems):
        _, mine, first, *_ = plan(ins, outs, sems)
        for cp in mine + first:
            cp.start()

    def finish(ins, outs, sems):
        copy, mine, first, me, sibling, c, chips = plan(ins, outs, sems)
        passed = []
        for a in range(n):
            for j, chip in enumerate(chips):
                copy(a, 1 + j, (*chip, c), me).wait_recv()
                fwd = copy(a, 4 + j, (*chip, c), sibling)
                fwd.start()
                passed.append(fwd)
        for a in range(n):
            copy(a, 0, sibling, me).wait_recv()
            for j, chip in enumerate(chips):
                copy(a, 4 + j, (*chip, 1 - c), me).wait_recv()
        for cp in first + passed:
            cp.wait_send()
        for cp in mine:
            cp.wait()

    return _Side(shards, [_sds((N_DEV,) + s.shape, s.dtype) for s in shards],
                 [pltpu.SemaphoreType.DMA((7 * n,)), pltpu.SemaphoreType.DMA((7 * n,)), pltpu.SemaphoreType.DMA((n,))],
                 start, finish)


def _all_gather_relayed(shards):
    n = len(shards)
    halves = [-(-(s.shape[0] // 2) // 16) * 16 for s in shards]

    def body_parts(ins, outs, sems):
        send_sems, recv_sems, local_sems = sems
        x, y, c, _ = _place()
        me, sib, xn, yn, dg = (x, y, c), (x, y, 1 - c), (1 - x, y, c), (x, 1 - y, c), (1 - x, 1 - y, c)

        def rows(a, block, part):
            px, py, pc = block
            whole = outs[a].at[4 * px + 2 * py + pc]
            if part is None:
                return whole
            return whole.at[pl.ds(0, halves[a])] if part == 0 else whole.at[pl.ds(halves[a], shards[a].shape[0] - halves[a])]

        def copy(a, k, block, part, to, src=None):
            dst = rows(a, block, part)
            return pltpu.make_async_remote_copy(
                src_ref=dst if src is None else src, dst_ref=dst, send_sem=send_sems.at[9 * a + k],
                recv_sem=recv_sems.at[9 * a + k], device_id=to, device_id_type=MESH)

        flip = lambda dev: (dev[0], dev[1], 1 - dev[2])
        mine = [pltpu.make_async_copy(ins[a], rows(a, me, None), local_sems.at[a]) for a in range(n)]
        own = [[copy(a, 0, me, None, sib, src=ins[a]), copy(a, 1, me, None, xn, src=ins[a]),
                copy(a, 2, me, None, yn, src=ins[a])] for a in range(n)]
        relays = lambda a: [(1, [copy(a, 3, xn, 0, yn), copy(a, 5, xn, None, sib)]),
                            (2, [copy(a, 4, yn, 1, xn), copy(a, 6, yn, None, sib)]),
                            (3, [copy(a, 7, dg, 0, sib)]), (4, [copy(a, 8, dg, 1, sib)])]
        lands = {0: (sib, None), 1: (xn, None), 2: (yn, None), 3: (dg, 0), 4: (dg, 1), 5: (flip(xn), None),
                 6: (flip(yn), None), 7: (flip(dg), 0), 8: (flip(dg), 1)}
        arrival = lambda a, k: copy(a, k, lands[k][0], lands[k][1], me)
        return mine, own, relays, arrival

    def start(ins, outs, sems):
        mine, own, _, _ = body_parts(ins, outs, sems)
        for cp in mine + [cp for per in own for cp in per]:
            cp.start()

    def finish(ins, outs, sems):
        mine, own, relays, arrival = body_parts(ins, outs, sems)
        sent = [cp for per in own for cp in per]
        relays = [relays(a) for a in range(n)]
        for stage in range(4):
            for a in range(n):
                after, passes = relays[a][stage]
                arrival(a, after).wait_recv()
                for cp in passes:
                    cp.start()
                sent += passes
        for a in range(n):
            for k in (0, 5, 6, 7, 8):
                arrival(a, k).wait_recv()
        for cp in sent:
            cp.wait_send()
        for cp in mine:
            cp.wait()

    return _Side(shards, [_sds((N_DEV,) + s.shape, s.dtype) for s in shards],
                 [pltpu.SemaphoreType.DMA((9 * n,)), pltpu.SemaphoreType.DMA((9 * n,)), pltpu.SemaphoreType.DMA((n,))],
                 start, finish)


def _exchange_in_chip(gs):
    n = len(gs)

    def copies(ins, outs, sems):
        x, y, c, _ = _place()
        return [pltpu.make_async_remote_copy(
            src_ref=ins[a].at[2 * q + 1 - c], dst_ref=outs[a].at[q], send_sem=sems[0].at[4 * a + q],
            recv_sem=sems[1].at[4 * a + q], device_id=(x, y, 1 - c), device_id_type=MESH)
            for a in range(n) for q in range(4)]

    def start(ins, outs, sems):
        for cp in copies(ins, outs, sems):
            cp.start()

    def finish(ins, outs, sems):
        for cp in copies(ins, outs, sems):
            cp.wait()

    return _Side(gs, [_sds((4,) + g.shape[1:], g.dtype) for g in gs],
                 [pltpu.SemaphoreType.DMA((4 * n,)), pltpu.SemaphoreType.DMA((4 * n,))], start, finish)


def _exchange_between_chips(ps):
    n = len(ps)

    def copies(ins, outs, sems):
        x, y, c, chips = _place()
        return [pltpu.make_async_remote_copy(
            src_ref=ins[a].at[2 * cx + cy], dst_ref=outs[a].at[j], send_sem=sems[0].at[3 * a + j],
            recv_sem=sems[1].at[3 * a + j], device_id=(cx, cy, c), device_id_type=MESH)
            for a in range(n) for j, (cx, cy) in enumerate(chips)]

    def start(ins, outs, sems):
        for cp in copies(ins, outs, sems):
            cp.start()

    def finish(ins, outs, sems):
        for cp in copies(ins, outs, sems):
            cp.wait()

    return _Side(ps, [_sds((3,) + p.shape[1:], p.dtype) for p in ps],
                 [pltpu.SemaphoreType.DMA((3 * n,)), pltpu.SemaphoreType.DMA((3 * n,))], start, finish)


def _all_reduce_small(v, name):
    R = v.shape[0]

    def body(v_ref, sum_ref, all_ref, send_sems, recv_sems):
        x, y, c, _ = _place()
        k = 4 * x + 2 * y + c
        all_ref[k] = v_ref[...]
        copies = []
        for rel in range(1, N_DEV):
            fx, fy, fc = (rel >> 2) & 1, (rel >> 1) & 1, rel & 1
            peer = (1 - x if fx else x, 1 - y if fy else y, 1 - c if fc else c)
            copies.append(pltpu.make_async_remote_copy(
                src_ref=v_ref, dst_ref=all_ref.at[k], send_sem=send_sems.at[rel - 1], recv_sem=recv_sems.at[rel - 1],
                device_id=peer, device_id_type=MESH))
        for cp in copies:
            cp.start()
        for rel in range(1, N_DEV):
            fx, fy, fc = (rel >> 2) & 1, (rel >> 1) & 1, rel & 1
            src = 4 * (1 - x if fx else x) + 2 * (1 - y if fy else y) + (1 - c if fc else c)
            pltpu.make_async_remote_copy(
                src_ref=v_ref, dst_ref=all_ref.at[src], send_sem=send_sems.at[rel - 1], recv_sem=recv_sems.at[rel - 1],
                device_id=(x, y, c), device_id_type=MESH).wait_recv()
        for cp in copies:
            cp.wait_send()
        tot = all_ref[0]
        for d in range(1, N_DEV):
            tot = tot + all_ref[d]
        sum_ref[...] = tot

    vm = pl.BlockSpec(memory_space=pltpu.VMEM)
    return pl.pallas_call(
        body, name=name, in_specs=[vm], out_specs=[vm, vm],
        out_shape=[_sds((R, HEAD), F32), _sds((N_DEV, R, HEAD), F32)],
        scratch_shapes=[pltpu.SemaphoreType.DMA((N_DEV - 1,)), pltpu.SemaphoreType.DMA((N_DEV - 1,))],
    )(v)[0]


def _tiles(rows, cols):
    tr = next((cand for cand in (688, 512, 256) if rows % cand == 0), rows)
    tc = 512 if (cols % 512 == 0 and tr * cols * 4 > (2 << 20)) else cols
    return tr, tc


def _chip_sum(g, r1, core, name):
    _, R, Cc = g.shape
    tr, tc = _tiles(R, Cc)

    def body(core_ref, g_ref, r_ref, p_ref):
        p_ref[...] = (g_ref[...] + r_ref[...]).astype(BF16)

    blk = lambda f: pl.BlockSpec((None, tr, tc), f)
    return pl.pallas_call(
        body, name=name,
        grid_spec=pltpu.PrefetchScalarGridSpec(
            num_scalar_prefetch=1, grid=(4, R // tr, Cc // tc),
            in_specs=[blk(lambda q, i, j, core: (2 * q + core[0], i, j)), blk(lambda q, i, j, core: (q, i, j))],
            out_specs=blk(lambda q, i, j, core: (q, i, j))),
        out_shape=_sds((4, R, Cc), BF16), compiler_params=_cparams(),
    )(core, g, r1)


def _adamw_update(gv, w_ref, m_ref, v_ref, d_ref, nm_ref, nv_ref):
    nm = B1 * m_ref[...] + (1.0 - B1) * gv
    nv = B2 * v_ref[...] + (1.0 - B2) * jnp.square(gv)
    m_hat = nm / (1.0 - B1 ** STEP)
    v_hat = nv / (1.0 - B2 ** STEP)
    d_ref[...] = -LR * (m_hat / (jnp.sqrt(v_hat) + EPS) + WD * w_ref[...])
    nm_ref[...] = nm
    nv_ref[...] = nv


def _reduce_adamw(g, r1, r2, where, w, m, v, name):
    _, R, Cc = g.shape
    tr, tc = _tiles(R, Cc)

    def body(where_ref, g_ref, r1_ref, r2_ref, w_ref, m_ref, v_ref, o_ref, d_ref, nm_ref, nv_ref):
        gv = ((g_ref[...] + r1_ref[...]) + r2_ref[0].astype(F32)) + (r2_ref[1].astype(F32) + r2_ref[2].astype(F32))
        o_ref[...] = gv
        _adamw_update(gv, w_ref, m_ref, v_ref, d_ref, nm_ref, nv_ref)

    blk = pl.BlockSpec((tr, tc), lambda i, j, w: (i, j))
    return pl.pallas_call(
        body, name=name,
        grid_spec=pltpu.PrefetchScalarGridSpec(
            num_scalar_prefetch=1, grid=(R // tr, Cc // tc),
            in_specs=[pl.BlockSpec((None, tr, tc), lambda i, j, w: (w[0], i, j)),
                      pl.BlockSpec((None, tr, tc), lambda i, j, w: (w[1], i, j)),
                      pl.BlockSpec((3, tr, tc), lambda i, j, w: (0, i, j)), blk, blk, blk],
            out_specs=[blk] * 4),
        out_shape=[_sds((R, Cc), F32)] * 4, compiler_params=_cparams(),
    )(where, g, r1, r2, w, m, v)


def _adamw(w, g, m, v, name):
    R, Cc = w.shape
    tr, tc = _tiles(R, Cc)

    def body(w_ref, g_ref, m_ref, v_ref, d_ref, nm_ref, nv_ref):
        _adamw_update(g_ref[...], w_ref, m_ref, v_ref, d_ref, nm_ref, nv_ref)

    blk = pl.BlockSpec((tr, tc), lambda i, j: (i, j))
    return pl.pallas_call(
        body, name=name, grid=(R // tr, Cc // tc), in_specs=[blk] * 4, out_specs=[blk] * 3,
        out_shape=[_sds((R, Cc), F32)] * 3, compiler_params=_cparams(),
    )(w, g, m, v)


SMALL = ("ffn1_norm", "mix_norm", "ffn2_norm", "q_norm_a", "k_norm_a", "q_norm_b", "k_norm_b", "forget_bias",
         "rel_bias_table")
LARGE = ("ffn1_w_in", "ffn1_w_out", "w_in", "w_out", "ffn2_w_in", "ffn2_w_out")
ORDER = ("ffn1_norm", "ffn1_w_in", "ffn1_w_out", "mix_norm", "w_in", "q_norm_a", "k_norm_a", "q_norm_b", "k_norm_b",
         "forget_bias", "rel_bias_table", "w_out", "ffn2_norm", "ffn2_w_in", "ffn2_w_out")


def _pack_small(vals):
    rows = []
    for name in SMALL:
        flat = vals[name].reshape(-1)
        pad = (-flat.shape[0]) % HEAD
        rows.append(jnp.pad(flat, (0, pad)).reshape(-1, HEAD))
    return jnp.concatenate(rows, axis=0)


def _unpack_small(packed, like):
    out, r = {}, 0
    for name in SMALL:
        size = like[name].size
        nrow = -(-size // HEAD)
        out[name] = packed[r:r + nrow].reshape(-1)[:size].reshape(like[name].shape)
        r += nrow
    return out


def kernel(x, ffn1_norm, ffn1_w_in, ffn1_w_out, mix_norm, w_in, q_norm_a, k_norm_a, q_norm_b, k_norm_b, forget_bias, rel_bias_table, w_out, ffn2_norm, ffn2_w_in, ffn2_w_out, loss_target, m_ffn1_norm, m_ffn1_w_in, m_ffn1_w_out, m_mix_norm, m_w_in, m_q_norm_a, m_k_norm_a, m_q_norm_b, m_k_norm_b, m_forget_bias, m_rel_bias_table, m_w_out, m_ffn2_norm, m_ffn2_w_in, m_ffn2_w_out, v_ffn1_norm, v_ffn1_w_in, v_ffn1_w_out, v_mix_norm, v_w_in, v_q_norm_a, v_k_norm_a, v_q_norm_b, v_k_norm_b, v_forget_bias, v_rel_bias_table, v_w_out, v_ffn2_norm, v_ffn2_w_in, v_ffn2_w_out):
    w = dict(ffn1_norm=ffn1_norm, ffn1_w_in=ffn1_w_in, ffn1_w_out=ffn1_w_out, mix_norm=mix_norm, w_in=w_in,
             q_norm_a=q_norm_a, k_norm_a=k_norm_a, q_norm_b=q_norm_b, k_norm_b=k_norm_b, forget_bias=forget_bias,
             rel_bias_table=rel_bias_table, w_out=w_out, ffn2_norm=ffn2_norm, ffn2_w_in=ffn2_w_in, ffn2_w_out=ffn2_w_out)
    m = dict(ffn1_norm=m_ffn1_norm, ffn1_w_in=m_ffn1_w_in, ffn1_w_out=m_ffn1_w_out, mix_norm=m_mix_norm, w_in=m_w_in,
             q_norm_a=m_q_norm_a, k_norm_a=m_k_norm_a, q_norm_b=m_q_norm_b, k_norm_b=m_k_norm_b,
             forget_bias=m_forget_bias, rel_bias_table=m_rel_bias_table, w_out=m_w_out, ffn2_norm=m_ffn2_norm,
             ffn2_w_in=m_ffn2_w_in, ffn2_w_out=m_ffn2_w_out)
    v = dict(ffn1_norm=v_ffn1_norm, ffn1_w_in=v_ffn1_w_in, ffn1_w_out=v_ffn1_w_out, mix_norm=v_mix_norm, w_in=v_w_in,
             q_norm_a=v_q_norm_a, k_norm_a=v_k_norm_a, q_norm_b=v_q_norm_b, k_norm_b=v_k_norm_b,
             forget_bias=v_forget_bias, rel_bias_table=v_rel_bias_table, w_out=v_w_out, ffn2_norm=v_ffn2_norm,
             ffn2_w_in=v_ffn2_w_in, ffn2_w_out=v_ffn2_w_out)
    T, D = x.shape[1], x.shape[2]
    C = D // 2
    H = C // HEAD
    ff_shard = ffn1_w_out.shape[1]

    f1i, f1o = _run_side(_all_gather_relayed([ffn1_w_in[0].T.astype(BF16), ffn1_w_out[0].astype(BF16)]), "gather_ffn1")
    wts = dict(ffn1_in=f1i.reshape(2, N_DEV, ff_shard, D), ffn1_out=f1o)
    xi, yi, ci = lax.axis_index("x"), lax.axis_index("y"), lax.axis_index("c")
    core = jnp.reshape(ci, (1,)).astype(jnp.int32)
    where = jnp.stack([4 * xi + 2 * yi + ci, 2 * xi + yi]).astype(jnp.int32)
    gs, r1, ps, r2 = {}, {}, {}, {}

    def by_destination(name, grads):
        gs[name] = grads[name + "_t" if name.endswith("w_in") else name].reshape(N_DEV, -1, D)
        return gs[name]

    def chip_sums(names):
        for name in names:
            ps[name] = _chip_sum(gs[name], r1[name], core, "chip_sum_" + name)
        return [ps[name] for name in names]

    class Plan:
        carried = {"ffn1_fwd": ("gather", ("w_in", "w_out")),
                   "dil_fwd": ("gather", ("ffn2_w_out",)), "fox_fwd": ("gather", ("ffn2_w_in",)),
                   "dil_bwd": ("in_chip", ("ffn2_w_in", "ffn2_w_out")), "fox_bwd": ("between", ("ffn2_w_in", "ffn2_w_out")),
                   "ffn1_bwd": ("in_chip", ("w_in", "w_out")), "ffn1_dwin": ("between", ("w_in", "w_out")),
                   "ffn1_rms_bwd": ("in_chip", ("ffn1_w_in",)), "ffn1_dwout": ("between", ("ffn1_w_in",))}

        def before(self, host, wts, grads):
            if host not in self.carried:
                return None
            kind, names = self.carried[host]
            if kind == "gather":
                return _all_gather([(w[n][0].T if n.endswith("w_in") else w[n][0]).astype(BF16) for n in names])
            if kind == "in_chip":
                return _exchange_in_chip([by_destination(n, grads) for n in names])
            return _exchange_between_chips(chip_sums(names))

        def after(self, host, res, wts, grads):
            kind, names = self.carried[host]
            if host == "ffn1_fwd":
                w_in_t = res[0].reshape(-1, D)
                wts.update(w_in_t=w_in_t, w_f_t=jnp.pad(w_in_t[6 * C:], ((0, HEAD - H), (0, 0))),
                           w_o=res[1].reshape(2 * C, D))
            elif host == "dil_fwd":
                wts.update(ffn2_out=res[0])
            elif host == "fox_fwd":
                wts.update(ffn2_in=res[0].reshape(2, N_DEV, ff_shard, D))
            else:
                (r1 if kind == "in_chip" else r2).update(zip(names, res))

    small = {name: w[name] for name in SMALL}
    loss_row, grad_x, grads = _local_step(x[0], loss_target[0], small, wts, Plan())

    tail = ("ffn1_w_out",)
    r1[tail[0]], = _run_side(_exchange_in_chip([by_destination(tail[0], grads)]), "reduce_in_chip_tail")
    r2.update(zip(tail, _run_side(_exchange_between_chips(chip_sums(tail)), "reduce_between_chips_tail")))

    packed = _pack_small(grads)
    nsmall = packed.shape[0]
    packed = jnp.concatenate([packed, loss_row, jnp.zeros(((-nsmall - 1) % 8, HEAD), F32)], axis=0)
    reduced = _all_reduce_small(packed, "reduce_small")
    loss = reduced[nsmall, 0]
    g_small = _unpack_small(reduced[:nsmall], small)

    grad, delta, new_m, new_v = dict(g_small), {}, {}, {}
    for name in LARGE:
        to = (lambda t: t[0].T) if name.endswith("w_in") else (lambda t: t[0])
        back = (lambda t: t.T[None]) if name.endswith("w_in") else (lambda t: t[None])
        res = _reduce_adamw(gs[name], r1[name], r2[name], where, to(w[name]), to(m[name]), to(v[name]), "adamw_" + name)
        grad[name], delta[name], new_m[name], new_v[name] = (back(t) for t in res)
    d, nm, nv = _adamw(_pack_small(w), reduced[:nsmall], _pack_small(m), _pack_small(v), "adamw_small")
    delta.update(_unpack_small(d, small))
    new_m.update(_unpack_small(nm, small))
    new_v.update(_unpack_small(nv, small))
    return (loss, grad_x[None], *[grad[n] for n in ORDER], *[delta[n] for n in ORDER],
            *[new_m[n] for n in ORDER], *[new_v[n] for n in ORDER])
```

```python
import functools
import math

import numpy as np
import jax
import jax.numpy as jnp
from jax import lax
from jax.experimental import pallas as pl
from jax.experimental.pallas import tpu as pltpu

F32, BF16 = jnp.float32, jnp.bfloat16
HEAD = 128
NSLAB = 16
BLK = 128
DILATIONS = (1, 4, 16)
NUM_BUCKETS, MAX_DISTANCE = 32, 2048
RMS_EPS = 1e-6
NEG = -1e30
SCALE = HEAD ** -0.5
LR, B1, B2, EPS, WD, STEP = 0.001, 0.9, 0.999, 1e-08, 0.01, 10
N_DEV = 8
VMEM_LIMIT_BYTES = 56 << 20
MESH = pl.DeviceIdType.MESH


def _cparams(**kw):
    return pltpu.CompilerParams(vmem_limit_bytes=VMEM_LIMIT_BYTES, **kw)


def _nn(a, b):
    return jnp.dot(a, b, preferred_element_type=F32)


def _nt(a, b):
    return lax.dot_general(a, b, (((1,), (1,)), ((), ())), preferred_element_type=F32)


def _tn(a, b):
    return lax.dot_general(a, b, (((0,), (0,)), ((), ())), preferred_element_type=F32)


def _sds(shape, dtype):
    return jax.ShapeDtypeStruct(shape, dtype)


ANY = pl.BlockSpec(memory_space=pl.ANY)


class _Side:
    def __init__(self, ins, outs, sems, start, finish):
        self.ins, self.outs, self.sems, self.start, self.finish = list(ins), list(outs), list(sems), start, finish


def _call(body, *, name, grid, in_specs, out_specs, out_shape, args, scratch_shapes=(), side=None):
    in_specs, out_specs, out_shape = list(in_specs), list(out_specs), list(out_shape)
    scratch_shapes = list(scratch_shapes)
    if side is None:
        return pl.pallas_call(body, name=name, grid=grid, in_specs=in_specs, out_specs=out_specs, out_shape=out_shape,
                              scratch_shapes=scratch_shapes, compiler_params=_cparams())(*args)
    ni, no, ns = len(args), len(out_shape), len(scratch_shapes)
    si, so = len(side.ins), len(side.outs)

    def fused(*refs):
        h_in, s_in = refs[:ni], refs[ni:ni + si]
        h_out, s_out = refs[ni + si:ni + si + no], refs[ni + si + no:ni + si + no + so]
        h_scr, s_sem = refs[ni + si + no + so:ni + si + no + so + ns], refs[ni + si + no + so + ns:]
        ids = [pl.program_id(k) for k in range(len(grid))]
        first = functools.reduce(jnp.logical_and, [i == 0 for i in ids])
        last = functools.reduce(jnp.logical_and, [i == n - 1 for i, n in zip(ids, grid)])

        @pl.when(first)
        def _():
            side.start(s_in, s_out, s_sem)

        body(*h_in, *h_out, *h_scr)

        @pl.when(last)
        def _():
            side.finish(s_in, s_out, s_sem)

    res = pl.pallas_call(
        fused, name=name, grid=grid, in_specs=in_specs + [ANY] * si, out_specs=out_specs + [ANY] * so,
        out_shape=out_shape + side.outs, scratch_shapes=scratch_shapes + side.sems, compiler_params=_cparams(),
    )(*args, *side.ins)
    return list(res[:no]), list(res[no:])


def _ffn_fwd(x, g, win, wout, tm, name, side=None):
    T, D = x.shape
    nc, tf = wout.shape[0], wout.shape[1]

    def body(x_ref, g_ref, win_ref, wout_ref, y_ref, h_ref, gu_ref, act_ref):
        j = pl.program_id(1)

        @pl.when(j == 0)
        def _():
            xv = x_ref[...]
            r = lax.rsqrt(jnp.mean(xv * xv, axis=-1, keepdims=True) + RMS_EPS)
            h_ref[...] = (xv * r * g_ref[...]).astype(BF16)
            y_ref[...] = jnp.zeros_like(y_ref)

        hb = h_ref[...]
        gt = _nt(win_ref[0], hb)
        up = _nt(win_ref[1], hb)
        gu_ref[0] = gt.astype(BF16)
        gu_ref[1] = up.astype(BF16)
        act = (gt * jax.nn.sigmoid(gt) * up).astype(BF16)
        act_ref[...] = act
        y_ref[...] += _tn(act, wout_ref[...])

        @pl.when(j == nc - 1)
        def _():
            y_ref[...] = x_ref[...] + 0.5 * y_ref[...]

    return _call(
        body, name=name, grid=(T // tm, nc), side=side,
        in_specs=[pl.BlockSpec((tm, D), lambda i, j: (i, 0)),
                  pl.BlockSpec((1, D), lambda i, j: (0, 0)),
                  pl.BlockSpec((2, None, tf, D), lambda i, j: (0, j, 0, 0)),
                  pl.BlockSpec((None, tf, D), lambda i, j: (j, 0, 0))],
        out_specs=[pl.BlockSpec((tm, D), lambda i, j: (i, 0)),
                   pl.BlockSpec((tm, D), lambda i, j: (i, 0)),
                   pl.BlockSpec((2, None, tf, tm), lambda i, j: (0, j, 0, i)),
                   pl.BlockSpec((None, tf, tm), lambda i, j: (j, 0, i))],
        out_shape=[_sds((T, D), F32), _sds((T, D), BF16), _sds((2, nc, tf, T), BF16), _sds((nc, tf, T), BF16)],
        args=(x, g, win, wout))


def _ffn_bwd(dy, gu, win, wout, tm, name, side=None):
    T, D = dy.shape
    nc, tf = wout.shape[0], wout.shape[1]

    def body(dy_ref, gu_ref, win_ref, wout_ref, dh_ref, dgu_ref, dyb_ref):
        j = pl.program_id(1)

        @pl.when(j == 0)
        def _():
            dh_ref[...] = jnp.zeros_like(dh_ref)
            dyb_ref[...] = (0.5 * dy_ref[...]).astype(BF16)

        dact = _nt(wout_ref[...], dyb_ref[...])
        gt = gu_ref[0].astype(F32)
        up = gu_ref[1].astype(F32)
        s = jax.nn.sigmoid(gt)
        dgb = (dact * up * (s * (1.0 + gt * (1.0 - s)))).astype(BF16)
        dub = (dact * (gt * s)).astype(BF16)
        dgu_ref[0] = dgb
        dgu_ref[1] = dub
        dh_ref[...] += _tn(dgb, win_ref[0]) + _tn(dub, win_ref[1])

    return _call(
        body, name=name, grid=(T // tm, nc), side=side,
        in_specs=[pl.BlockSpec((tm, D), lambda i, j: (i, 0)),
                  pl.BlockSpec((2, None, tf, tm), lambda i, j: (0, j, 0, i)),
                  pl.BlockSpec((2, None, tf, D), lambda i, j: (0, j, 0, 0)),
                  pl.BlockSpec((None, tf, D), lambda i, j: (j, 0, 0))],
        out_specs=[pl.BlockSpec((tm, D), lambda i, j: (i, 0)),
                   pl.BlockSpec((2, None, tf, tm), lambda i, j: (0, j, 0, i)),
                   pl.BlockSpec((tm, D), lambda i, j: (i, 0))],
        out_shape=[_sds((T, D), F32), _sds((2, nc, tf, T), BF16), _sds((T, D), BF16)],
        args=(dy, gu, win, wout))


def _rms_bwd(x, g, dh, dres, tm, name, side=None):
    T, D = x.shape

    def body(x_ref, g_ref, dh_ref, dres_ref, dx_ref, dg_ref):
        @pl.when(pl.program_id(0) == 0)
        def _():
            dg_ref[...] = jnp.zeros_like(dg_ref)

        xv = x_ref[...]
        r = lax.rsqrt(jnp.mean(xv * xv, axis=-1, keepdims=True) + RMS_EPS)
        xhat = xv * r
        dh = dh_ref[...]
        gd = dh * g_ref[...]
        dx_ref[...] = dres_ref[...] + r * (gd - xhat * jnp.mean(gd * xhat, axis=-1, keepdims=True))
        dg_ref[...] += jnp.sum(dh * xhat, axis=0, keepdims=True)

    row = pl.BlockSpec((tm, D), lambda i: (i, 0))
    one = pl.BlockSpec((1, D), lambda i: (0, 0))
    return _call(body, name=name, grid=(T // tm,), side=side, in_specs=[row, one, row, row], out_specs=[row, one],
                 out_shape=[_sds((T, D), F32), _sds((1, D), F32)], args=(x, g, dh, dres))


def _mm_tn(a, b, *, bm, bn, bt, name, b_slabs=False, side=None, rows=None, m_off=0, into=None, a_rows=False):
    nz, T, M = (a.shape[0], a.shape[2], a.shape[1]) if a_rows else a.shape
    if b_slabs:
        N = b.shape[1] // NSLAB
        assert bt == T // NSLAB
        b_spec = pl.BlockSpec((bt, bn), lambda n, z, m, t: (0, t * (N // bn) + n))
    else:
        N = b.shape[1]
        b_spec = pl.BlockSpec((bt, bn), lambda n, z, m, t: (t, n))
    assert M % bm == 0 and N % bn == 0 and T % bt == 0, (M, bm, N, bn, T, bt)

    def body(a_ref, b_ref, *rest):
        c_ref = rest[-1]

        @pl.when(pl.program_id(3) == 0)
        def _():
            c_ref[...] = jnp.zeros_like(c_ref)

        ab, bb = a_ref[...].astype(BF16), b_ref[...].astype(BF16)
        c_ref[...] += _nn(ab, bb) if a_rows else _tn(ab, bb)

    grid = (N // bn, nz, M // bm, T // bt)
    a_spec = (pl.BlockSpec((None, bm, bt), lambda n, z, m, t: (z, m, t)) if a_rows
              else pl.BlockSpec((None, bt, bm), lambda n, z, m, t: (z, t, m)))
    in_specs = [a_spec, b_spec]
    out_spec = pl.BlockSpec((None, bm, bn), lambda n, z, m, t: (z, m + m_off, n))
    out_shape = _sds((nz, M if rows is None else rows, N), F32)
    if into is not None:
        assert side is None and into.shape == out_shape.shape
        return pl.pallas_call(body, name=name, grid=grid, in_specs=in_specs + [ANY], out_specs=out_spec,
                              out_shape=out_shape, input_output_aliases={2: 0}, compiler_params=_cparams())(a, b, into)
    res = _call(body, name=name, grid=grid, side=side, in_specs=in_specs, out_specs=[out_spec],
                out_shape=[out_shape], args=(a, b))
    return res[0] if side is None else (res[0][0], res[1])


def _tok_spec(layout, tm, n16, C, bc, colmap):
    if layout == "nat":
        return pl.BlockSpec((tm, bc), lambda i, k: (i, colmap(k)))
    assert tm % n16 == 0
    if layout == "slab":
        return pl.BlockSpec((tm // n16, n16, bc), lambda i, k: (i, 0, colmap(k)))
    assert bc == C
    return pl.BlockSpec((n16, (tm // n16) * C), lambda i, k: (0, i))


def _tok_load(ref, layout, sp):
    if layout == "nat":
        return ref[...]
    if layout == "slab":
        return ref[...].reshape(-1, ref.shape[-1])
    c = ref.shape[1] // sp
    return jnp.concatenate([ref[:, s * c:(s + 1) * c] for s in range(sp)], axis=0)


def _tok_store(ref, layout, sp, val, cols=None, accumulate=False):
    def put(idx, v):
        if accumulate:
            ref[idx] += v
        else:
            ref[idx] = v

    lanes = slice(None) if cols is None else slice(cols[0], cols[0] + cols[1])
    if layout == "nat":
        put((slice(None), lanes), val)
    elif layout == "slab":
        put((slice(None), slice(None), lanes), val.reshape(sp, ref.shape[1], val.shape[-1]))
    else:
        assert cols is None
        c, n = ref.shape[1] // sp, ref.shape[0]
        for s in range(sp):
            put((slice(None), slice(s * c, (s + 1) * c)), val[s * n:(s + 1) * n])


def _proj(x, g, wt, gains, modes, *, tn, w_off, slabs, tm, normed_dtype, name):
    T, D = x.shape
    ntile = len(modes)
    N = ntile * tn
    n16 = T // NSLAB
    in_layout, out_layout = ("view", "slab") if slabs else ("nat", "nat")
    sp = tm // n16
    x_in = x.reshape(n16, NSLAB * D) if slabs else x
    x_spec = _tok_spec(in_layout, tm, n16, D, D, lambda n: 0)
    oshape = lambda c: (NSLAB, n16, c) if slabs else (T, c)
    ospec = lambda bc, cm: _tok_spec(out_layout, tm, n16, None, bc, cm)

    def body(x_ref, g_ref, w_ref, gains_ref, raw_ref, nrm_ref, h_ref):
        n = pl.program_id(1)

        @pl.when(n == 0)
        def _():
            xv = _tok_load(x_ref, in_layout, sp)
            r = lax.rsqrt(jnp.mean(xv * xv, axis=-1, keepdims=True) + RMS_EPS)
            _tok_store(h_ref, out_layout, sp, (xv * r * g_ref[...]).astype(BF16))

        y = _nt(_tok_load(h_ref, out_layout, sp), w_ref[...])
        _tok_store(raw_ref, out_layout, sp, y)
        for t, mode in enumerate(modes):
            @pl.when(n == t)
            def _(t=t, mode=mode):
                if not mode:
                    _tok_store(nrm_ref, out_layout, sp, y.astype(nrm_ref.dtype))
                    return
                gain = gains_ref[t]
                for k in range(tn // HEAD):
                    yk = y[:, k * HEAD:(k + 1) * HEAD]
                    r = lax.rsqrt(jnp.mean(yk * yk, axis=-1, keepdims=True) + RMS_EPS)
                    _tok_store(nrm_ref, out_layout, sp, (yk * r * gain).astype(nrm_ref.dtype), cols=(k * HEAD, HEAD))

    return pl.pallas_call(
        body, name=name, grid=(T // tm, ntile),
        in_specs=[x_spec, pl.BlockSpec((1, D), lambda i, n: (0, 0)),
                  pl.BlockSpec((tn, D), lambda i, n: (n + w_off, 0)),
                  pl.BlockSpec((ntile, 1, HEAD), lambda i, n: (0, 0, 0))],
        out_specs=[ospec(tn, lambda n: n), ospec(tn, lambda n: n), ospec(D, lambda n: 0)],
        out_shape=[_sds(oshape(N), F32), _sds(oshape(N), normed_dtype), _sds(oshape(D), BF16)],
        compiler_params=_cparams(),
    )(x_in, g, wt, gains)


def _mm(a, w, *, nt, tk, tm, a_layout, out_layout, resid=None, name, w_off=0, n_out=None):
    if a_layout == "slab":
        T, K = a.shape[0] * a.shape[1], a.shape[2]
    else:
        T, K = a.shape
    N = (w.shape[0] if nt else w.shape[1]) if n_out is None else n_out
    n16 = T // NSLAB
    nk = K // tk
    sp = tm // n16
    a_in = a.reshape(n16, NSLAB * K) if a_layout == "view" else a
    w_spec = (pl.BlockSpec((N, tk), lambda i, k: (w_off, k)) if nt
              else pl.BlockSpec((tk, N), lambda i, k: (k + w_off, 0)))
    o_spec = _tok_spec(out_layout, tm, n16, N, N, lambda k: 0)
    oshape = {"nat": (T, N), "slab": (NSLAB, n16, N), "view": (n16, NSLAB * N)}[out_layout]
    has_resid = resid is not None

    def body(*refs):
        a_ref, w_ref = refs[0], refs[1]
        o_ref = refs[-1]
        k = pl.program_id(1)

        @pl.when(k == 0)
        def _():
            o_ref[...] = refs[2][...] if has_resid else jnp.zeros_like(o_ref)

        ab = _tok_load(a_ref, a_layout, sp).astype(BF16)
        _tok_store(o_ref, out_layout, sp, _nt(ab, w_ref[...]) if nt else _nn(ab, w_ref[...]), accumulate=True)

    ins = [a_in, w]
    in_specs = [_tok_spec(a_layout, tm, n16, K, tk, lambda k: k), w_spec]
    if has_resid:
        ins.append(resid.reshape(n16, NSLAB * N) if out_layout == "view" else resid)
        in_specs.append(o_spec)
    out = pl.pallas_call(
        body, name=name, grid=(T // tm, nk), in_specs=in_specs, out_specs=o_spec,
        out_shape=_sds(oshape, F32), compiler_params=_cparams(),
    )(*ins)
    return out.reshape(T, N) if out_layout == "view" else out


def _log_sigmoid(z):
    return jnp.minimum(z, 0.0) - jnp.log(1.0 + jnp.exp(-jnp.abs(z)))


def _fox_gate_fwd(f_raw, fbias, name):
    T = f_raw.shape[0]
    cb = 256

    def body(f_ref, b_ref, c_ref):
        row = lax.broadcasted_iota(jnp.int32, (cb, cb), 0)
        col = lax.broadcasted_iota(jnp.int32, (cb, cb), 1)
        tri = (col <= row).astype(F32)
        carry = jnp.zeros((1, HEAD), F32)
        for i in range(T // cb):
            lf = _log_sigmoid(f_ref[i * cb:(i + 1) * cb, :] + b_ref[...])
            c = jnp.dot(tri, lf, preferred_element_type=F32, precision=lax.Precision.HIGHEST) + carry
            c_ref[i * cb:(i + 1) * cb, :] = c
            carry = c[cb - 1:cb, :]

    return pl.pallas_call(body, name=name, out_shape=_sds((T, HEAD), F32), compiler_params=_cparams())(f_raw, fbias)


def _fox_gate_bwd(f_raw, fbias, dc, name):
    T = f_raw.shape[0]
    cb = 256

    def body(f_ref, b_ref, dc_ref, df_ref, db_ref):
        row = lax.broadcasted_iota(jnp.int32, (cb, cb), 0)
        col = lax.broadcasted_iota(jnp.int32, (cb, cb), 1)
        tri = (col >= row).astype(F32)
        carry = jnp.zeros((1, HEAD), F32)
        dbias = jnp.zeros((1, HEAD), F32)
        for i in reversed(range(T // cb)):
            dlf = jnp.dot(tri, dc_ref[i * cb:(i + 1) * cb, :], preferred_element_type=F32,
                          precision=lax.Precision.HIGHEST) + carry
            carry = dlf[0:1, :]
            z = f_ref[i * cb:(i + 1) * cb, :] + b_ref[...]
            df = dlf * jax.nn.sigmoid(-z)
            df_ref[i * cb:(i + 1) * cb, :] = df
            dbias = dbias + jnp.sum(df, axis=0, keepdims=True)
        db_ref[...] = dbias

    return pl.pallas_call(body, name=name, out_shape=[_sds((T, HEAD), F32), _sds((1, HEAD), F32)],
                          compiler_params=_cparams())(f_raw, fbias, dc)


def _fox_fwd(qkv, c_col, c_row, tq, name, side=None):
    T = qkv.shape[0]
    H = qkv.shape[1] // (3 * HEAD)
    nq = T // tq
    c_blocks = c_row.reshape(H, nq, 1, tq)

    def body(q_ref, k_ref, v_ref, cq_ref, ck_ref, o_ref, lse_ref):
        qi = pl.program_id(1)
        q, cq = q_ref[...], cq_ref[...]
        causal = lax.broadcasted_iota(jnp.int32, (tq, tq), 1) <= lax.broadcasted_iota(jnp.int32, (tq, tq), 0)

        def key_block(ki, carry, diagonal):
            m, l, acc = carry
            rows = pl.ds(pl.multiple_of(ki * tq, tq), tq)
            s = _nt(q, k_ref[rows, :]) * SCALE + cq - ck_ref[ki]
            if diagonal:
                s = jnp.where(causal, s, NEG)
            m_new = jnp.maximum(m, jnp.max(s, axis=-1, keepdims=True))
            alpha = jnp.exp(m - m_new)
            p = jnp.exp(s - m_new)
            l = alpha * l + jnp.sum(p, axis=-1, keepdims=True)
            acc = alpha * acc + _nn(p.astype(BF16), v_ref[rows, :])
            return m_new, l, acc

        init = (jnp.full((tq, 1), NEG, F32), jnp.zeros((tq, 1), F32), jnp.zeros((tq, HEAD), F32))
        carry = lax.fori_loop(0, qi, lambda ki, c: key_block(ki, c, False), init)
        m, l, acc = key_block(qi, carry, True)
        o_ref[...] = acc / l
        lse_ref[...] = m + jnp.log(l)

    return _call(
        body, name=name, grid=(H, nq), side=side,
        in_specs=[pl.BlockSpec((tq, HEAD), lambda h, qi: (qi, h)),
                  pl.BlockSpec((T, HEAD), lambda h, qi: (0, H + h)),
                  pl.BlockSpec((T, HEAD), lambda h, qi: (0, 2 * H + h)),
                  pl.BlockSpec((None, tq, 1), lambda h, qi: (h, qi, 0)),
                  pl.BlockSpec((None, nq, 1, tq), lambda h, qi: (h, 0, 0, 0))],
        out_specs=[pl.BlockSpec((tq, HEAD), lambda h, qi: (qi, h)),
                   pl.BlockSpec((None, tq, 1), lambda h, qi: (h, qi, 0))],
        out_shape=[_sds((T, H * HEAD), F32), _sds((H, T, 1), F32)],
        args=(qkv, qkv, qkv, c_col, c_blocks))


def _fox_bwd(qkv, c_col, c_row, out, dout, lse, tq, name, side=None):
    T = qkv.shape[0]
    H = qkv.shape[1] // (3 * HEAD)
    nq = T // tq

    def body(q_ref, k_ref, v_ref, cq_ref, ck_ref, o_ref, do_ref, lse_ref, dq_ref, dk_ref, dv_ref, dck_ref, dcq_ref,
             delta_s):
        ki = pl.program_id(1)

        @pl.when(ki == 0)
        def _():
            dq_ref[...] = jnp.zeros_like(dq_ref)
            dcq_ref[...] = jnp.zeros_like(dcq_ref)
            delta_s[...] = jnp.sum(do_ref[...] * o_ref[...], axis=-1, keepdims=True)

        k, v, ck = k_ref[...], v_ref[...], ck_ref[...]
        causal = lax.broadcasted_iota(jnp.int32, (tq, tq), 1) <= lax.broadcasted_iota(jnp.int32, (tq, tq), 0)

        def query_block(qi, carry, diagonal):
            dk, dv, dck = carry
            rows = pl.ds(pl.multiple_of(qi * tq, tq), tq)
            q = q_ref[rows, :]
            s = _nt(q, k) * SCALE + cq_ref[rows, :] - ck
            if diagonal:
                s = jnp.where(causal, s, NEG)
            p = jnp.exp(s - lse_ref[rows, :])
            dob = do_ref[rows, :].astype(BF16)
            ds = p * (_nt(dob, v) - delta_s[rows, :])
            dsb = ds.astype(BF16)
            dq_ref[rows, :] += _nn(dsb, k) * SCALE
            dcq_ref[rows, :] += jnp.sum(ds, axis=-1, keepdims=True)
            return dk + _tn(dsb, q), dv + _tn(p.astype(BF16), dob), dck - jnp.sum(ds, axis=0, keepdims=True)

        init = (jnp.zeros((tq, HEAD), F32), jnp.zeros((tq, HEAD), F32), jnp.zeros((1, tq), F32))
        carry = query_block(ki, init, True)
        dk, dv, dck = lax.fori_loop(ki + 1, nq, lambda qi, c: query_block(qi, c, False), carry)
        dk_ref[...] = dk * SCALE
        dv_ref[...] = dv
        dck_ref[...] = dck

    head = lambda off: pl.BlockSpec((T, HEAD), lambda h, ki: (0, off + h))
    col = pl.BlockSpec((None, T, 1), lambda h, ki: (h, 0, 0))
    return _call(
        body, name=name, grid=(H, nq), side=side,
        in_specs=[head(0),
                  pl.BlockSpec((tq, HEAD), lambda h, ki: (ki, H + h)),
                  pl.BlockSpec((tq, HEAD), lambda h, ki: (ki, 2 * H + h)),
                  col, pl.BlockSpec((None, 1, tq), lambda h, ki: (h, 0, ki)), head(0), head(0), col],
        out_specs=[head(0),
                   pl.BlockSpec((tq, HEAD), lambda h, ki: (ki, h)),
                   pl.BlockSpec((tq, HEAD), lambda h, ki: (ki, h)),
                   pl.BlockSpec((None, 1, tq), lambda h, ki: (h, 0, ki)), col],
        out_shape=[_sds((T, H * HEAD), F32), _sds((T, H * HEAD), F32), _sds((T, H * HEAD), F32), _sds((H, 1, T), F32),
                   _sds((H, T, 1), F32)],
        scratch_shapes=[pltpu.VMEM((T, 1), F32)],
        args=(qkv, qkv, qkv, c_col, c_row, out, dout, lse))


def _t5_bucket(dist):
    max_exact = NUM_BUCKETS // 2
    d = dist.astype(np.float32)
    large = max_exact + (np.log(np.maximum(d, np.float32(1.0)) / np.float32(max_exact))
                         / np.float32(math.log(MAX_DISTANCE / max_exact))
                         * np.float32(NUM_BUCKETS - max_exact)).astype(np.int32)
    large = np.minimum(large, NUM_BUCKETS - 1)
    return np.where(dist < max_exact, dist, large)


def _bucket_maps():
    maps = []
    for d in DILATIONS:
        e = NSLAB // d
        rows = BLK // e
        idx = np.arange(BLK)
        pos = e * (idx % rows) + idx // rows
        qpos = pos[:, None] + BLK
        kpos = np.concatenate([pos, pos + BLK])[None, :]
        delta = qpos - kpos
        band = (delta >= 0) & (delta <= BLK)
        bucket = _t5_bucket(np.clip(delta, 0, None) * d)
        maps.append(np.where(band, bucket, -1).astype(np.int32))
    return np.stack(maps)


def _dil_geometry(T):
    n16 = T // NSLAB
    geo = []
    for d in DILATIONS:
        e = NSLAB // d
        rows = BLK // e
        nblk = n16 // rows
        geo.append((d, e, rows, nblk))
    return geo


DIL_INTERLEAVE_FWD = {1: 4, 4: 8, 16: 8}
DIL_INTERLEAVE_BWD = {1: 8, 4: 8, 16: 8}


def _dil_interleave(per_step, nblocks):
    while per_step > 1 and (nblocks % per_step or nblocks // per_step < 2):
        per_step -= 1
    return per_step


def _dil_bias(tab_ref, bkt_ref, bias_s, h):
    for p in range(len(DILATIONS)):
        bk = bkt_ref[p]
        bias = jnp.full((BLK, 2 * BLK), NEG, F32)
        for b in range(NUM_BUCKETS):
            bias = jnp.where(bk == b, tab_ref[b, h], bias)
        bias_s[p] = bias


def _dil_rows(d, e, rows, sub, blk):
    start = pl.multiple_of(blk * rows, rows)
    return [(sub + d * j, pl.ds(start, rows)) for j in range(e)]


def _gather(ref, idx):
    return jnp.concatenate([ref[s, r, :] for s, r in idx], axis=0)


def _scatter(ref, idx, val, rows):
    for j, (s, r) in enumerate(idx):
        ref[s, r, :] = val[j * rows:(j + 1) * rows]


def _scatter_add(ref, idx, val, rows):
    for j, (s, r) in enumerate(idx):
        ref[s, r, :] += val[j * rows:(j + 1) * rows]


def _dil_fwd(qkv, table, name, side=None):
    n16 = qkv.shape[1]
    T = NSLAB * n16
    H = qkv.shape[2] // (3 * HEAD)
    geo = _dil_geometry(T)
    bkt = jnp.asarray(_bucket_maps())

    def body(tab_ref, bkt_ref, q_ref, k_ref, v_ref, o_ref, lse_ref, bias_s, m_s, l_s):
        h = pl.program_id(0)
        _dil_bias(tab_ref, bkt_ref, bias_s, h)
        first_mask = lax.broadcasted_iota(jnp.int32, (BLK, 2 * BLK), 1) < BLK

        starts = len(DILATIONS) - 1

        def load(p, d, e, rows, sub, blk):
            cur = _dil_rows(d, e, rows, sub, blk)
            prev = _dil_rows(d, e, rows, sub, jnp.maximum(blk - 1, 0))
            q = _gather(q_ref, cur).astype(BF16)
            kk = jnp.concatenate([_gather(k_ref, prev), _gather(k_ref, cur)], axis=0).astype(BF16)
            vv = jnp.concatenate([_gather(v_ref, prev), _gather(v_ref, cur)], axis=0).astype(BF16)
            old = None if p == starts else (_gather(m_s, cur), _gather(l_s, cur), _gather(o_ref, cur))
            return cur, blk, q, kk, vv, old

        def compute(p, blk, q, kk, vv, old):
            s = _nt(q, kk) * SCALE + bias_s[p]
            s = jnp.where(first_mask & (blk == 0), NEG, s)
            m_blk = jnp.max(s, axis=-1, keepdims=True)
            if old is None:
                m_new = m_blk
                pr = jnp.exp(s - m_new)
                l_new = jnp.sum(pr, axis=-1, keepdims=True)
                acc = _nn(pr.astype(BF16), vv)
            else:
                m_old, l_old, acc_old = old
                m_new = jnp.maximum(m_old, m_blk)
                alpha = jnp.exp(m_old - m_new)
                pr = jnp.exp(s - m_new)
                l_new = alpha * l_old + jnp.sum(pr, axis=-1, keepdims=True)
                acc = alpha * acc_old + _nn(pr.astype(BF16), vv)
            if p == 0:
                return acc / l_new, m_new + jnp.log(l_new), None
            return acc, m_new, l_new

        def store(p, rows, cur, acc, m_new, l_new):
            _scatter(o_ref, cur, acc, rows)
            if p == 0:
                _scatter(lse_ref, cur, m_new, rows)
            else:
                _scatter(m_s, cur, m_new, rows)
                _scatter(l_s, cur, l_new, rows)

        for p in reversed(range(len(DILATIONS))):
            d, e, rows, nblk = geo[p]
            per_step = _dil_interleave(DIL_INTERLEAVE_FWD[d], d * nblk)

            def step(i, carry, p=p, d=d, e=e, rows=rows, nblk=nblk, per_step=per_step):
                ids = [i + u * (d * nblk // per_step) for u in range(per_step)]
                loaded = [load(p, d, e, rows, j // nblk, j % nblk) for j in ids]
                done = [(cur, compute(p, blk, q, kk, vv, old)) for cur, blk, q, kk, vv, old in loaded]
                for cur, res in done:
                    store(p, rows, cur, *res)
                return carry

            lax.fori_loop(0, d * nblk // per_step, step, 0)

    head = lambda off: pl.BlockSpec((NSLAB, n16, HEAD), lambda h: (0, 0, off + h))
    return _call(
        body, name=name, grid=(H,), side=side,
        in_specs=[pl.BlockSpec(memory_space=pltpu.SMEM), pl.BlockSpec((3, BLK, 2 * BLK), lambda h: (0, 0, 0)),
                  head(0), head(H), head(2 * H)],
        out_specs=[head(0), pl.BlockSpec((None, NSLAB, n16, 1), lambda h: (h, 0, 0, 0))],
        out_shape=[_sds((NSLAB, n16, H * HEAD), F32), _sds((H, NSLAB, n16, 1), F32)],
        scratch_shapes=[pltpu.VMEM((3, BLK, 2 * BLK), F32), pltpu.VMEM((NSLAB, n16, 1), F32),
                        pltpu.VMEM((NSLAB, n16, 1), F32)],
        args=(table, bkt, qkv, qkv, qkv))


def _dil_bwd(qkv, table, out, dout, lse, name, side=None):
    n16 = qkv.shape[1]
    T = NSLAB * n16
    H = qkv.shape[2] // (3 * HEAD)
    geo = _dil_geometry(T)
    bkt = jnp.asarray(_bucket_maps())

    def body(tab_ref, bkt_ref, q_ref, k_ref, v_ref, o_ref, do_ref, lse_ref,
             dq_ref, dk_ref, dv_ref, dtab_ref, bias_s, dbias_s, delta_s):
        h = pl.program_id(0)
        _dil_bias(tab_ref, bkt_ref, bias_s, h)
        first_mask = lax.broadcasted_iota(jnp.int32, (BLK, 2 * BLK), 1) < BLK
        dbias_s[...] = jnp.zeros_like(dbias_s)
        dq_ref[...] = jnp.zeros_like(dq_ref)
        dk_ref[...] = jnp.zeros_like(dk_ref)
        dv_ref[...] = jnp.zeros_like(dv_ref)
        for r in range(NSLAB):
            delta_s[r] = jnp.sum(do_ref[r] * o_ref[r], axis=-1, keepdims=True)

        def load(d, e, rows, sub, blk):
            cur = _dil_rows(d, e, rows, sub, blk)
            prev = _dil_rows(d, e, rows, sub, jnp.maximum(blk - 1, 0))
            q = _gather(q_ref, cur).astype(BF16)
            kk = jnp.concatenate([_gather(k_ref, prev), _gather(k_ref, cur)], axis=0).astype(BF16)
            vv = jnp.concatenate([_gather(v_ref, prev), _gather(v_ref, cur)], axis=0).astype(BF16)
            dob = _gather(do_ref, cur).astype(BF16)
            return cur, prev, blk, q, kk, vv, dob, _gather(lse_ref, cur), _gather(delta_s, cur)

        def compute(p, blk, q, kk, vv, dob, lse, delta):
            s = _nt(q, kk) * SCALE + bias_s[p]
            s = jnp.where(first_mask & (blk == 0), NEG, s)
            pr = jnp.exp(s - lse)
            ds = pr * (_nt(dob, vv) - delta)
            dsb = ds.astype(BF16)
            return ds, _nn(dsb, kk) * SCALE, _tn(dsb, q) * SCALE, _tn(pr.astype(BF16), dob)

        def store(rows, cur, prev, dq, dkk, dvv):
            _scatter_add(dq_ref, cur, dq, rows)
            _scatter_add(dk_ref, prev, dkk[:BLK], rows)
            _scatter_add(dk_ref, cur, dkk[BLK:], rows)
            _scatter_add(dv_ref, prev, dvv[:BLK], rows)
            _scatter_add(dv_ref, cur, dvv[BLK:], rows)

        for p in range(len(DILATIONS)):
            d, e, rows, nblk = geo[p]
            per_step = _dil_interleave(DIL_INTERLEAVE_BWD[d], d * nblk)

            def step(i, carry, p=p, d=d, e=e, rows=rows, nblk=nblk, per_step=per_step):
                ids = [i + u * (d * nblk // per_step) for u in range(per_step)]
                loaded = [load(d, e, rows, j // nblk, j % nblk) for j in ids]
                done = [(cur, prev, compute(p, *rest)) for cur, prev, *rest in loaded]
                dbias_s[p] += functools.reduce(jnp.add, [res[0] for _, _, res in done])
                for cur, prev, res in done:
                    store(rows, cur, prev, *res[1:])
                return carry

            lax.fori_loop(0, d * nblk // per_step, step, 0)

        lane = lax.broadcasted_iota(jnp.int32, (1, HEAD), 1)
        row = jnp.zeros((1, HEAD), F32)
        for b in range(NUM_BUCKETS):
            tot = jnp.zeros((1, 1), F32)
            for p in range(len(DILATIONS)):
                hit = jnp.where(bkt_ref[p] == b, dbias_s[p], 0.0)
                tot = tot + jnp.sum(jnp.sum(hit, axis=0, keepdims=True), axis=1, keepdims=True)
            row = jnp.where(lane == b, tot, row)
        dtab_ref[...] = row

    head = lambda off: pl.BlockSpec((NSLAB, n16, HEAD), lambda h: (0, 0, off + h))
    return _call(
        body, name=name, grid=(H,), side=side,
        in_specs=[pl.BlockSpec(memory_space=pltpu.SMEM), pl.BlockSpec((3, BLK, 2 * BLK), lambda h: (0, 0, 0)),
                  head(0), head(H), head(2 * H), head(0), head(0),
                  pl.BlockSpec((None, NSLAB, n16, 1), lambda h: (h, 0, 0, 0))],
        out_specs=[head(0), head(0), head(0), pl.BlockSpec((None, 1, HEAD), lambda h: (h, 0, 0))],
        out_shape=[_sds((NSLAB, n16, H * HEAD), F32)] * 3 + [_sds((H, 1, HEAD), F32)],
        scratch_shapes=[pltpu.VMEM((3, BLK, 2 * BLK), F32), pltpu.VMEM((3, BLK, 2 * BLK), F32),
                        pltpu.VMEM((NSLAB, n16, 1), F32)],
        args=(table, bkt, qkv, qkv, qkv, out, dout, lse))


def _qknorm_bwd(raw, dq, dk, dv, gains, tm, name):
    T, N = raw.shape
    C = N // 3

    def body(raw_ref, dq_ref, dk_ref, dv_ref, gains_ref, dp_ref, dg_ref):
        @pl.when(pl.program_id(0) == 0)
        def _():
            dg_ref[...] = jnp.zeros_like(dg_ref)

        for t, d_ref in enumerate((dq_ref, dk_ref)):
            gain = gains_ref[t]
            dgain = jnp.zeros((1, HEAD), F32)
            for k in range(C // HEAD):
                y = raw_ref[:, t * C + k * HEAD:t * C + (k + 1) * HEAD]
                dn = d_ref[:, k * HEAD:(k + 1) * HEAD]
                r = lax.rsqrt(jnp.mean(y * y, axis=-1, keepdims=True) + RMS_EPS)
                yhat = y * r
                gd = dn * gain
                dy = r * (gd - yhat * jnp.mean(gd * yhat, axis=-1, keepdims=True))
                dp_ref[:, t * C + k * HEAD:t * C + (k + 1) * HEAD] = dy.astype(BF16)
                dgain = dgain + jnp.sum(dn * yhat, axis=0, keepdims=True)
            dg_ref[t] += dgain
        dp_ref[:, 2 * C:] = dv_ref[...].astype(BF16)

    third = pl.BlockSpec((tm, C), lambda i: (i, 0))
    return pl.pallas_call(
        body, name=name, grid=(T // tm,),
        in_specs=[pl.BlockSpec((tm, N), lambda i: (i, 0)), third, third, third,
                  pl.BlockSpec((2, 1, HEAD), lambda i: (0, 0, 0))],
        out_specs=[pl.BlockSpec((tm, N), lambda i: (i, 0)), pl.BlockSpec((2, 1, HEAD), lambda i: (0, 0, 0))],
        out_shape=[_sds((T, N), BF16), _sds((2, 1, HEAD), F32)], compiler_params=_cparams(),
    )(raw, dq, dk, dv, gains)


def _loss_grad(y, target, tm, name):
    T, D = y.shape

    def body(y_ref, t_ref, dy_ref, loss_ref):
        @pl.when(pl.program_id(0) == 0)
        def _():
            loss_ref[...] = jnp.zeros_like(loss_ref)

        err = y_ref[...] - t_ref[...]
        dy_ref[...] = err * (1.0 / D)
        per_tok = jnp.mean(err * err, axis=-1, keepdims=True)
        tot = 0.5 * jnp.sum(per_tok, axis=0, keepdims=True)
        lane = lax.broadcasted_iota(jnp.int32, (1, HEAD), 1)
        loss_ref[...] += jnp.where(lane == 0, tot, 0.0)

    row = pl.BlockSpec((tm, D), lambda i: (i, 0))
    return pl.pallas_call(
        body, name=name, grid=(T // tm,), in_specs=[row, row],
        out_specs=[row, pl.BlockSpec((1, HEAD), lambda i: (0, 0))],
        out_shape=[_sds((T, D), F32), _sds((1, HEAD), F32)], compiler_params=_cparams(),
    )(y, target)


def _pad_lanes(v, width=HEAD):
    return jnp.pad(v, ((0, 0), (0, width - v.shape[1])))


def _local_step(x, target, small, wts, plan=None):
    grads = {}

    def hosted(host, fn, *args, **kw):
        side = plan.before(host, wts, grads) if plan is not None else None
        if side is None:
            return fn(*args, name=host, **kw)
        res, side_res = fn(*args, name=host, side=side, **kw)
        plan.after(host, side_res, wts, grads)
        return res

    T, D = x.shape
    C = D // 2
    H = C // HEAD
    n16 = T // NSLAB
    tm = min(512, T)
    tmm = min(1024, T)
    tms = 2 * n16
    tq = min(512, T)
    bn = min(1024, D)
    g1, gm, g2 = small["ffn1_norm"], small["mix_norm"], small["ffn2_norm"]
    gains_a = jnp.stack([small["q_norm_a"], small["k_norm_a"], jnp.ones_like(small["q_norm_a"])])
    gains_b = jnp.stack([small["q_norm_b"], small["k_norm_b"], jnp.ones_like(small["q_norm_b"])])
    fbias = _pad_lanes(small["forget_bias"])
    table = small["rel_bias_table"]

    x1, h1, gu1, act1 = hosted("ffn1_fwd", _ffn_fwd, x, g1, wts["ffn1_in"], wts["ffn1_out"], tm)
    w_in_t, w_f_t, w_o = wts["w_in_t"], wts["w_f_t"], wts["w_o"]
    raw_a, nrm_a, h2a = _proj(x1, gm, w_in_t, gains_a, (True, True, False), tn=C, w_off=0, slabs=True, tm=tms,
                              normed_dtype=F32, name="proj_a")
    raw_b, nrm_b, h2b = _proj(x1, gm, w_in_t, gains_b, (True, True, False), tn=C, w_off=3, slabs=False, tm=tmm,
                              normed_dtype=BF16, name="proj_b")
    f_raw, _, _ = _proj(x1, gm, w_f_t, gains_b[:1], (False,), tn=HEAD, w_off=0, slabs=False, tm=tm,
                        normed_dtype=BF16, name="proj_f")
    c = _fox_gate_fwd(f_raw, fbias, "fox_gate_fwd")
    c_heads = c[:, :H].T
    c_col, c_row = c_heads[:, :, None], c_heads[:, None, :]
    out_a, lse_a = hosted("dil_fwd", _dil_fwd, nrm_a, table)
    out_b, lse_b = hosted("fox_fwd", _fox_fwd, nrm_b, c_col, c_row, tq)
    x2a = _mm(out_a, w_o, nt=False, tk=C, tm=tms, a_layout="slab", out_layout="view", resid=x1, name="out_a")
    x2 = _mm(out_b, w_o, nt=False, tk=C, tm=tmm, a_layout="nat", out_layout="nat", resid=x2a, w_off=1, name="out_b")
    y, h3, gu3, act3 = _ffn_fwd(x2, g2, wts["ffn2_in"], wts["ffn2_out"], tm, "ffn2_fwd")
    dy, loss_row = _loss_grad(y, target, tm, "loss_grad")

    def ffn_backward(tag, xin, g, h, gu, act, win, wout, dres):
        nc, tf = wout.shape[0], wout.shape[1]
        dh, dgu, dyb = hosted(tag + "_bwd", _ffn_bwd, dres, gu, win, wout, tm)
        grads[tag + "_w_in_t"] = hosted(tag + "_dwin", _mm_tn, dgu.reshape(2 * nc, tf, T), h, bm=tf, bn=bn, bt=T,
                                        a_rows=True)
        dxin, grads[tag + "_norm"] = hosted(tag + "_rms_bwd", _rms_bwd, xin, g, dh, dres, tm)
        grads[tag + "_w_out"] = hosted(tag + "_dwout", _mm_tn, act, dyb, bm=tf, bn=bn, bt=T, a_rows=True)
        return dxin

    dx2 = ffn_backward("ffn2", x2, g2, h3, gu3, act3, wts["ffn2_in"], wts["ffn2_out"], dy)

    dmix_a = _mm(dx2, w_o, nt=True, tk=D, tm=tms, a_layout="view", out_layout="slab", n_out=C, name="dmix_a")
    dmix_b = _mm(dx2, w_o, nt=True, tk=D, tm=tmm, a_layout="nat", out_layout="nat", n_out=C, w_off=1, name="dmix_b")
    dwo = _mm_tn(out_a.reshape(1, T, C), dx2.reshape(n16, NSLAB * D), bm=C, bn=bn, bt=n16, b_slabs=True,
                 rows=2 * C, name="dwo_a")
    dwo = _mm_tn(out_b.reshape(1, T, C), dx2, bm=C, bn=bn, bt=tm, rows=2 * C, m_off=1, into=dwo, name="dwo_b")
    grads["w_out"] = dwo[0]

    dqa, dka, dva, dtab = hosted("dil_bwd", _dil_bwd, nrm_a, table, out_a, dmix_a, lse_a)
    dqb, dkb, dvb, dck, dcq = hosted("fox_bwd", _fox_bwd, nrm_b, c_col, c_row, out_b, dmix_b, lse_b, tq)
    grads["rel_bias_table"] = dtab[:, 0, :NUM_BUCKETS].T
    dc = _pad_lanes((dck[:, 0, :] + dcq[:, :, 0]).T)
    df, dfb = _fox_gate_bwd(f_raw, fbias, dc, "fox_gate_bwd")
    grads["forget_bias"] = dfb[:, :H]

    flat = lambda a: a.reshape(T, a.shape[-1])
    dproj_a, dgain_a = _qknorm_bwd(flat(raw_a), flat(dqa), flat(dka), flat(dva), gains_a[:2], min(256, T), "qknorm_bwd_a")
    dproj_b, dgain_b = _qknorm_bwd(raw_b, dqb, dkb, dvb, gains_b[:2], min(256, T), "qknorm_bwd_b")
    grads["q_norm_a"], grads["k_norm_a"] = dgain_a[0], dgain_a[1]
    grads["q_norm_b"], grads["k_norm_b"] = dgain_b[0], dgain_b[1]
    dproj_a = dproj_a.reshape(NSLAB, n16, 3 * C)

    dh2 = _mm(dproj_a, w_in_t, nt=False, tk=C, tm=tms, a_layout="slab", out_layout="view", name="dh2_a")
    dh2 = _mm(dproj_b, w_in_t, nt=False, tk=C, tm=tmm, a_layout="nat", out_layout="nat", resid=dh2, w_off=3, name="dh2_b")
    dh2 = _mm(df, w_f_t, nt=False, tk=HEAD, tm=tmm, a_layout="nat", out_layout="nat", resid=dh2, name="dh2_f")
    dx1, grads["mix_norm"] = _rms_bwd(x1, gm, dh2, dx2, tm, "mix_rms_bwd")
    bt = min(2048, T)
    dwt = _mm_tn(flat(dproj_a)[None], flat(h2a), bm=C, bn=bn, bt=bt, rows=6 * C + H, name="dw_a")
    dwt = _mm_tn(dproj_b[None], h2b, bm=C, bn=bn, bt=bt, rows=6 * C + H, m_off=3, into=dwt, name="dw_b")
    dwt = _mm_tn(df[None, :, :H], h2b, bm=H, bn=bn, bt=bt, rows=6 * C + H, m_off=6 * C // H, into=dwt, name="dw_f")
    grads["w_in_t"] = dwt[0]

    grad_x = ffn_backward("ffn1", x, g1, h1, gu1, act1, wts["ffn1_in"], wts["ffn1_out"], dx1)
    return loss_row, grad_x, grads


def _place():
    x, y, c = lax.axis_index("x"), lax.axis_index("y"), lax.axis_index("c")
    other_chips = [(1 - x, y), (x, 1 - y), (1 - x, 1 - y)]
    return x, y, c, other_chips


def _run_side(side, name):
    def body(*refs):
        si, so = len(side.ins), len(side.outs)
        side.start(refs[:si], refs[si:si + so], refs[si + so:])
        side.finish(refs[:si], refs[si:si + so], refs[si + so:])

    return pl.pallas_call(body, name=name, in_specs=[ANY] * len(side.ins), out_specs=[ANY] * len(side.outs),
                          out_shape=side.outs, scratch_shapes=side.sems)(*side.ins)


def _all_gather(shards):
    n = len(shards)

    def plan(ins, outs, sems):
        send_sems, recv_sems, local_sems = sems
        x, y, c, chips = _place()
        me, sibling = (x, y, c), (x, y, 1 - c)

        def copy(a, k, block, to, src=None):
            px, py, pc = block
            dst = outs[a].at[4 * px + 2 * py + pc]
            return pltpu.make_async_remote_copy(
                src_ref=dst if src is None else src, dst_ref=dst, send_sem=send_sems.at[7 * a + k],
                recv_sem=recv_sems.at[7 * a + k], device_id=to, device_id_type=MESH)

        mine = [pltpu.make_async_copy(ins[a], outs[a].at[4 * x + 2 * y + c], local_sems.at[a]) for a in range(n)]
        first = []
        for a in range(n):
            first.append(copy(a, 0, me, sibling, src=ins[a]))
            first += [copy(a, 1 + j, me, (*chip, c), src=ins[a]) for j, chip in enumerate(chips)]
        return copy, mine, first, me, sibling, c, chips

    def start(ins, outs, sems):
        _, mine, first, *_ = plan(ins, outs, sems)
        for cp in mine + first:
            cp.start()

    def finish(ins, outs, sems):
        copy, mine, first, me, sibling, c, chips = plan(ins, outs, sems)
        passed = []
        for a in range(n):
            for j, chip in enumerate(chips):
                copy(a, 1 + j, (*chip, c), me).wait_recv()
                fwd = copy(a, 4 + j, (*chip, c), sibling)
                fwd.start()
                passed.append(fwd)
        for a in range(n):
            copy(a, 0, sibling, me).wait_recv()
            for j, chip in enumerate(chips):
                copy(a, 4 + j, (*chip, 1 - c), me).wait_recv()
        for cp in first + passed:
            cp.wait_send()
        for cp in mine:
            cp.wait()

    return _Side(shards, [_sds((N_DEV,) + s.shape, s.dtype) for s in shards],
                 [pltpu.SemaphoreType.DMA((7 * n,)), pltpu.SemaphoreType.DMA((7 * n,)), pltpu.SemaphoreType.DMA((n,))],
                 start, finish)


def _all_gather_relayed(shards):
    n = len(shards)
    halves = [-(-(s.shape[0] // 2) // 16) * 16 for s in shards]

    def body_parts(ins, outs, sems):
        send_sems, recv_sems, local_sems = sems
        x, y, c, _ = _place()
        me, sib, xn, yn, dg = (x, y, c), (x, y, 1 - c), (1 - x, y, c), (x, 1 - y, c), (1 - x, 1 - y, c)

        def rows(a, block, part):
            px, py, pc = block
            whole = outs[a].at[4 * px + 2 * py + pc]
            if part is None:
                return whole
            return whole.at[pl.ds(0, halves[a])] if part == 0 else whole.at[pl.ds(halves[a], shards[a].shape[0] - halves[a])]

        def copy(a, k, block, part, to, src=None):
            dst = rows(a, block, part)
            return pltpu.make_async_remote_copy(
                src_ref=dst if src is None else src, dst_ref=dst, send_sem=send_sems.at[9 * a + k],
                recv_sem=recv_sems.at[9 * a + k], device_id=to, device_id_type=MESH)

        flip = lambda dev: (dev[0], dev[1], 1 - dev[2])
        mine = [pltpu.make_async_copy(ins[a], rows(a, me, None), local_sems.at[a]) for a in range(n)]
        own = [[copy(a, 0, me, None, sib, src=ins[a]), copy(a, 1, me, None, xn, src=ins[a]),
                copy(a, 2, me, None, yn, src=ins[a])] for a in range(n)]
        relays = lambda a: [(1, [copy(a, 3, xn, 0, yn), copy(a, 5, xn, None, sib)]),
                            (2, [copy(a, 4, yn, 1, xn), copy(a, 6, yn, None, sib)]),
                            (3, [copy(a, 7, dg, 0, sib)]), (4, [copy(a, 8, dg, 1, sib)])]
        lands = {0: (sib, None), 1: (xn, None), 2: (yn, None), 3: (dg, 0), 4: (dg, 1), 5: (flip(xn), None),
                 6: (flip(yn), None), 7: (flip(dg), 0), 8: (flip(dg), 1)}
        arrival = lambda a, k: copy(a, k, lands[k][0], lands[k][1], me)
        return mine, own, relays, arrival

    def start(ins, outs, sems):
        mine, own, _, _ = body_parts(ins, outs, sems)
        for cp in mine + [cp for per in own for cp in per]:
            cp.start()

    def finish(ins, outs, sems):
        mine, own, relays, arrival = body_parts(ins, outs, sems)
        sent = [cp for per in own for cp in per]
        relays = [relays(a) for a in range(n)]
        for stage in range(4):
            for a in range(n):
                after, passes = relays[a][stage]
                arrival(a, after).wait_recv()
                for cp in passes:
                    cp.start()
                sent += passes
        for a in range(n):
            for k in (0, 5, 6, 7, 8):
                arrival(a, k).wait_recv()
        for cp in sent:
            cp.wait_send()
        for cp in mine:
            cp.wait()

    return _Side(shards, [_sds((N_DEV,) + s.shape, s.dtype) for s in shards],
                 [pltpu.SemaphoreType.DMA((9 * n,)), pltpu.SemaphoreType.DMA((9 * n,)), pltpu.SemaphoreType.DMA((n,))],
                 start, finish)


def _exchange_in_chip(gs):
    n = len(gs)

    def copies(ins, outs, sems):
        x, y, c, _ = _place()
        return [pltpu.make_async_remote_copy(
            src_ref=ins[a].at[2 * q + 1 - c], dst_ref=outs[a].at[q], send_sem=sems[0].at[4 * a + q],
            recv_sem=sems[1].at[4 * a + q], device_id=(x, y, 1 - c), device_id_type=MESH)
            for a in range(n) for q in range(4)]

    def start(ins, outs, sems):
        for cp in copies(ins, outs, sems):
            cp.start()

    def finish(ins, outs, sems):
        for cp in copies(ins, outs, sems):
            cp.wait()

    return _Side(gs, [_sds((4,) + g.shape[1:], g.dtype) for g in gs],
                 [pltpu.SemaphoreType.DMA((4 * n,)), pltpu.SemaphoreType.DMA((4 * n,))], start, finish)


def _exchange_between_chips(ps):
    n = len(ps)

    def copies(ins, outs, sems):
        x, y, c, chips = _place()
        return [pltpu.make_async_remote_copy(
            src_ref=ins[a].at[2 * cx + cy], dst_ref=outs[a].at[j], send_sem=sems[0].at[3 * a + j],
            recv_sem=sems[1].at[3 * a + j], device_id=(cx, cy, c), device_id_type=MESH)
            for a in range(n) for j, (cx, cy) in enumerate(chips)]

    def start(ins, outs, sems):
        for cp in copies(ins, outs, sems):
            cp.start()

    def finish(ins, outs, sems):
        for cp in copies(ins, outs, sems):
            cp.wait()

    return _Side(ps, [_sds((3,) + p.shape[1:], p.dtype) for p in ps],
                 [pltpu.SemaphoreType.DMA((3 * n,)), pltpu.SemaphoreType.DMA((3 * n,))], start, finish)


def _all_reduce_small(v, name):
    R = v.shape[0]

    def body(v_ref, sum_ref, all_ref, send_sems, recv_sems):
        x, y, c, _ = _place()
        k = 4 * x + 2 * y + c
        all_ref[k] = v_ref[...]
        copies = []
        for rel in range(1, N_DEV):
            fx, fy, fc = (rel >> 2) & 1, (rel >> 1) & 1, rel & 1
            peer = (1 - x if fx else x, 1 - y if fy else y, 1 - c if fc else c)
            copies.append(pltpu.make_async_remote_copy(
                src_ref=v_ref, dst_ref=all_ref.at[k], send_sem=send_sems.at[rel - 1], recv_sem=recv_sems.at[rel - 1],
                device_id=peer, device_id_type=MESH))
        for cp in copies:
            cp.start()
        for rel in range(1, N_DEV):
            fx, fy, fc = (rel >> 2) & 1, (rel >> 1) & 1, rel & 1
            src = 4 * (1 - x if fx else x) + 2 * (1 - y if fy else y) + (1 - c if fc else c)
            pltpu.make_async_remote_copy(
                src_ref=v_ref, dst_ref=all_ref.at[src], send_sem=send_sems.at[rel - 1], recv_sem=recv_sems.at[rel - 1],
                device_id=(x, y, c), device_id_type=MESH).wait_recv()
        for cp in copies:
            cp.wait_send()
        tot = all_ref[0]
        for d in range(1, N_DEV):
            tot = tot + all_ref[d]
        sum_ref[...] = tot

    vm = pl.BlockSpec(memory_space=pltpu.VMEM)
    return pl.pallas_call(
        body, name=name, in_specs=[vm], out_specs=[vm, vm],
        out_shape=[_sds((R, HEAD), F32), _sds((N_DEV, R, HEAD), F32)],
        scratch_shapes=[pltpu.SemaphoreType.DMA((N_DEV - 1,)), pltpu.SemaphoreType.DMA((N_DEV - 1,))],
    )(v)[0]


def _tiles(rows, cols):
    tr = next((cand for cand in (688, 512, 256) if rows % cand == 0), rows)
    tc = 512 if (cols % 512 == 0 and tr * cols * 4 > (2 << 20)) else cols
    return tr, tc


def _chip_sum(g, r1, core, name):
    _, R, Cc = g.shape
    tr, tc = _tiles(R, Cc)

    def body(core_ref, g_ref, r_ref, p_ref):
        p_ref[...] = (g_ref[...] + r_ref[...]).astype(BF16)

    blk = lambda f: pl.BlockSpec((None, tr, tc), f)
    return pl.pallas_call(
        body, name=name,
        grid_spec=pltpu.PrefetchScalarGridSpec(
            num_scalar_prefetch=1, grid=(4, R // tr, Cc // tc),
            in_specs=[blk(lambda q, i, j, core: (2 * q + core[0], i, j)), blk(lambda q, i, j, core: (q, i, j))],
            out_specs=blk(lambda q, i, j, core: (q, i, j))),
        out_shape=_sds((4, R, Cc), BF16), compiler_params=_cparams(),
    )(core, g, r1)


def _adamw_update(gv, w_ref, m_ref, v_ref, d_ref, nm_ref, nv_ref):
    nm = B1 * m_ref[...] + (1.0 - B1) * gv
    nv = B2 * v_ref[...] + (1.0 - B2) * jnp.square(gv)
    m_hat = nm / (1.0 - B1 ** STEP)
    v_hat = nv / (1.0 - B2 ** STEP)
    d_ref[...] = -LR * (m_hat / (jnp.sqrt(v_hat) + EPS) + WD * w_ref[...])
    nm_ref[...] = nm
    nv_ref[...] = nv


def _reduce_adamw(g, r1, r2, where, w, m, v, name):
    _, R, Cc = g.shape
    tr, tc = _tiles(R, Cc)

    def body(where_ref, g_ref, r1_ref, r2_ref, w_ref, m_ref, v_ref, o_ref, d_ref, nm_ref, nv_ref):
        gv = ((g_ref[...] + r1_ref[...]) + r2_ref[0].astype(F32)) + (r2_ref[1].astype(F32) + r2_ref[2].astype(F32))
        o_ref[...] = gv
        _adamw_update(gv, w_ref, m_ref, v_ref, d_ref, nm_ref, nv_ref)

    blk = pl.BlockSpec((tr, tc), lambda i, j, w: (i, j))
    return pl.pallas_call(
        body, name=name,
        grid_spec=pltpu.PrefetchScalarGridSpec(
            num_scalar_prefetch=1, grid=(R // tr, Cc // tc),
            in_specs=[pl.BlockSpec((None, tr, tc), lambda i, j, w: (w[0], i, j)),
                      pl.BlockSpec((None, tr, tc), lambda i, j, w: (w[1], i, j)),
                      pl.BlockSpec((3, tr, tc), lambda i, j, w: (0, i, j)), blk, blk, blk],
            out_specs=[blk] * 4),
        out_shape=[_sds((R, Cc), F32)] * 4, compiler_params=_cparams(),
    )(where, g, r1, r2, w, m, v)


def _adamw(w, g, m, v, name):
    R, Cc = w.shape
    tr, tc = _tiles(R, Cc)

    def body(w_ref, g_ref, m_ref, v_ref, d_ref, nm_ref, nv_ref):
        _adamw_update(g_ref[...], w_ref, m_ref, v_ref, d_ref, nm_ref, nv_ref)

    blk = pl.BlockSpec((tr, tc), lambda i, j: (i, j))
    return pl.pallas_call(
        body, name=name, grid=(R // tr, Cc // tc), in_specs=[blk] * 4, out_specs=[blk] * 3,
        out_shape=[_sds((R, Cc), F32)] * 3, compiler_params=_cparams(),
    )(w, g, m, v)


SMALL = ("ffn1_norm", "mix_norm", "ffn2_norm", "q_norm_a", "k_norm_a", "q_norm_b", "k_norm_b", "forget_bias",
         "rel_bias_table")
LARGE = ("ffn1_w_in", "ffn1_w_out", "w_in", "w_out", "ffn2_w_in", "ffn2_w_out")
ORDER = ("ffn1_norm", "ffn1_w_in", "ffn1_w_out", "mix_norm", "w_in", "q_norm_a", "k_norm_a", "q_norm_b", "k_norm_b",
         "forget_bias", "rel_bias_table", "w_out", "ffn2_norm", "ffn2_w_in", "ffn2_w_out")


def _pack_small(vals):
    rows = []
    for name in SMALL:
        flat = vals[name].reshape(-1)
        pad = (-flat.shape[0]) % HEAD
        rows.append(jnp.pad(flat, (0, pad)).reshape(-1, HEAD))
    return jnp.concatenate(rows, axis=0)


def _unpack_small(packed, like):
    out, r = {}, 0
    for name in SMALL:
        size = like[name].size
        nrow = -(-size // HEAD)
        out[name] = packed[r:r + nrow].reshape(-1)[:size].reshape(like[name].shape)
        r += nrow
    return out


def kernel(x, ffn1_norm, ffn1_w_in, ffn1_w_out, mix_norm, w_in, q_norm_a, k_norm_a, q_norm_b, k_norm_b, forget_bias, rel_bias_table, w_out, ffn2_norm, ffn2_w_in, ffn2_w_out, loss_target, m_ffn1_norm, m_ffn1_w_in, m_ffn1_w_out, m_mix_norm, m_w_in, m_q_norm_a, m_k_norm_a, m_q_norm_b, m_k_norm_b, m_forget_bias, m_rel_bias_table, m_w_out, m_ffn2_norm, m_ffn2_w_in, m_ffn2_w_out, v_ffn1_norm, v_ffn1_w_in, v_ffn1_w_out, v_mix_norm, v_w_in, v_q_norm_a, v_k_norm_a, v_q_norm_b, v_k_norm_b, v_forget_bias, v_rel_bias_table, v_w_out, v_ffn2_norm, v_ffn2_w_in, v_ffn2_w_out):
    w = dict(ffn1_norm=ffn1_norm, ffn1_w_in=ffn1_w_in, ffn1_w_out=ffn1_w_out, mix_norm=mix_norm, w_in=w_in,
             q_norm_a=q_norm_a, k_norm_a=k_norm_a, q_norm_b=q_norm_b, k_norm_b=k_norm_b, forget_bias=forget_bias,
             rel_bias_table=rel_bias_table, w_out=w_out, ffn2_norm=ffn2_norm, ffn2_w_in=ffn2_w_in, ffn2_w_out=ffn2_w_out)
    m = dict(ffn1_norm=m_ffn1_norm, ffn1_w_in=m_ffn1_w_in, ffn1_w_out=m_ffn1_w_out, mix_norm=m_mix_norm, w_in=m_w_in,
             q_norm_a=m_q_norm_a, k_norm_a=m_k_norm_a, q_norm_b=m_q_norm_b, k_norm_b=m_k_norm_b,
             forget_bias=m_forget_bias, rel_bias_table=m_rel_bias_table, w_out=m_w_out, ffn2_norm=m_ffn2_norm,
             ffn2_w_in=m_ffn2_w_in, ffn2_w_out=m_ffn2_w_out)
    v = dict(ffn1_norm=v_ffn1_norm, ffn1_w_in=v_ffn1_w_in, ffn1_w_out=v_ffn1_w_out, mix_norm=v_mix_norm, w_in=v_w_in,
             q_norm_a=v_q_norm_a, k_norm_a=v_k_norm_a, q_norm_b=v_q_norm_b, k_norm_b=v_k_norm_b,
             forget_bias=v_forget_bias, rel_bias_table=v_rel_bias_table, w_out=v_w_out, ffn2_norm=v_ffn2_norm,
             ffn2_w_in=v_ffn2_w_in, ffn2_w_out=v_ffn2_w_out)
    T, D = x.shape[1], x.shape[2]
    C = D // 2
    H = C // HEAD
    ff_shard = ffn1_w_out.shape[1]

    f1i, f1o = _run_side(_all_gather_relayed([ffn1_w_in[0].T.astype(BF16), ffn1_w_out[0].astype(BF16)]), "gather_ffn1")
    wts = dict(ffn1_in=f1i.reshape(2, N_DEV, ff_shard, D), ffn1_out=f1o)
    xi, yi, ci = lax.axis_index("x"), lax.axis_index("y"), lax.axis_index("c")
    core = jnp.reshape(ci, (1,)).astype(jnp.int32)
    where = jnp.stack([4 * xi + 2 * yi + ci, 2 * xi + yi]).astype(jnp.int32)
    gs, r1, ps, r2 = {}, {}, {}, {}

    def by_destination(name, grads):
        gs[name] = grads[name + "_t" if name.endswith("w_in") else name].reshape(N_DEV, -1, D)
        return gs[name]

    def chip_sums(names):
        for name in names:
            ps[name] = _chip_sum(gs[name], r1[name], core, "chip_sum_" + name)
        return [ps[name] for name in names]

    class Plan:
        carried = {"ffn1_fwd": ("gather", ("w_in", "w_out")),
                   "dil_fwd": ("gather", ("ffn2_w_out",)), "fox_fwd": ("gather", ("ffn2_w_in",)),
                   "dil_bwd": ("in_chip", ("ffn2_w_in", "ffn2_w_out")), "fox_bwd": ("between", ("ffn2_w_in", "ffn2_w_out")),
                   "ffn1_bwd": ("in_chip", ("w_in", "w_out")), "ffn1_dwin": ("between", ("w_in", "w_out")),
                   "ffn1_rms_bwd": ("in_chip", ("ffn1_w_in",)), "ffn1_dwout": ("between", ("ffn1_w_in",))}

        def before(self, host, wts, grads):
            if host not in self.carried:
                return None
            kind, names = self.carried[host]
            if kind == "gather":
                return _all_gather([(w[n][0].T if n.endswith("w_in") else w[n][0]).astype(BF16) for n in names])
            if kind == "in_chip":
                return _exchange_in_chip([by_destination(n, grads) for n in names])
            return _exchange_between_chips(chip_sums(names))

        def after(self, host, res, wts, grads):
            kind, names = self.carried[host]
            if host == "ffn1_fwd":
                w_in_t = res[0].reshape(-1, D)
                wts.update(w_in_t=w_in_t, w_f_t=jnp.pad(w_in_t[6 * C:], ((0, HEAD - H), (0, 0))),
                           w_o=res[1].reshape(2 * C, D))
            elif host == "dil_fwd":
                wts.update(ffn2_out=res[0])
            elif host == "fox_fwd":
                wts.update(ffn2_in=res[0].reshape(2, N_DEV, ff_shard, D))
            else:
                (r1 if kind == "in_chip" else r2).update(zip(names, res))

    small = {name: w[name] for name in SMALL}
    loss_row, grad_x, grads = _local_step(x[0], loss_target[0], small, wts, Plan())

    tail = ("ffn1_w_out",)
    r1[tail[0]], = _run_side(_exchange_in_chip([by_destination(tail[0], grads)]), "reduce_in_chip_tail")
    r2.update(zip(tail, _run_side(_exchange_between_chips(chip_sums(tail)), "reduce_between_chips_tail")))

    packed = _pack_small(grads)
    nsmall = packed.shape[0]
    packed = jnp.concatenate([packed, loss_row, jnp.zeros(((-nsmall - 1) % 8, HEAD), F32)], axis=0)
    reduced = _all_reduce_small(packed, "reduce_small")
    loss = reduced[nsmall, 0]
    g_small = _unpack_small(reduced[:nsmall], small)

    grad, delta, new_m, new_v = dict(g_small), {}, {}, {}
    for name in LARGE:
        to = (lambda t: t[0].T) if name.endswith("w_in") else (lambda t: t[0])
        back = (lambda t: t.T[None]) if name.endswith("w_in") else (lambda t: t[None])
        res = _reduce_adamw(gs[name], r1[name], r2[name], where, to(w[name]), to(m[name]), to(v[name]), "adamw_" + name)
        grad[name], delta[name], new_m[name], new_v[name] = (back(t) for t in res)
    d, nm, nv = _adamw(_pack_small(w), reduced[:nsmall], _pack_small(m), _pack_small(v), "adamw_small")
    delta.update(_unpack_small(d, small))
    new_m.update(_unpack_small(nm, small))
    new_v.update(_unpack_small(nv, small))
    return (loss, grad_x[None], *[grad[n] for n in ORDER], *[delta[n] for n in ORDER],
            *[new_m[n] for n in ORDER], *[new_v[n] for n in ORDER])
```

```python
import functools
import math

import numpy as np
import jax
import jax.numpy as jnp
from jax import lax
from jax.experimental import pallas as pl
from jax.experimental.pallas import tpu as pltpu

F32, BF16 = jnp.float32, jnp.bfloat16
HEAD = 128
NSLAB = 16
BLK = 128
DILATIONS = (1, 4, 16)
NUM_BUCKETS, MAX_DISTANCE = 32, 2048
RMS_EPS = 1e-6
NEG = -1e30
SCALE = HEAD ** -0.5
LR, B1, B2, EPS, WD, STEP = 0.001, 0.9, 0.999, 1e-08, 0.01, 10
N_DEV = 8
VMEM_LIMIT_BYTES = 56 << 20
MESH = pl.DeviceIdType.MESH


def _cparams(**kw):
    return pltpu.CompilerParams(vmem_limit_bytes=VMEM_LIMIT_BYTES, **kw)


def _nn(a, b):
    return jnp.dot(a, b, preferred_element_type=F32)


def _nt(a, b):
    return lax.dot_general(a, b, (((1,), (1,)), ((), ())), preferred_element_type=F32)


def _tn(a, b):
    return lax.dot_general(a, b, (((0,), (0,)), ((), ())), preferred_element_type=F32)


def _sds(shape, dtype):
    return jax.ShapeDtypeStruct(shape, dtype)


ANY = pl.BlockSpec(memory_space=pl.ANY)


class _Side:
    def __init__(self, ins, outs, sems, start, finish):
        self.ins, self.outs, self.sems, self.start, self.finish = list(ins), list(outs), list(sems), start, finish


def _call(body, *, name, grid, in_specs, out_specs, out_shape, args, scratch_shapes=(), side=None):
    in_specs, out_specs, out_shape = list(in_specs), list(out_specs), list(out_shape)
    scratch_shapes = list(scratch_shapes)
    if side is None:
        return pl.pallas_call(body, name=name, grid=grid, in_specs=in_specs, out_specs=out_specs, out_shape=out_shape,
                              scratch_shapes=scratch_shapes, compiler_params=_cparams())(*args)
    ni, no, ns = len(args), len(out_shape), len(scratch_shapes)
    si, so = len(side.ins), len(side.outs)

    def fused(*refs):
        h_in, s_in = refs[:ni], refs[ni:ni + si]
        h_out, s_out = refs[ni + si:ni + si + no], refs[ni + si + no:ni + si + no + so]
        h_scr, s_sem = refs[ni + si + no + so:ni + si + no + so + ns], refs[ni + si + no + so + ns:]
        ids = [pl.program_id(k) for k in range(len(grid))]
        first = functools.reduce(jnp.logical_and, [i == 0 for i in ids])
        last = functools.reduce(jnp.logical_and, [i == n - 1 for i, n in zip(ids, grid)])

        @pl.when(first)
        def _():
            side.start(s_in, s_out, s_sem)

        body(*h_in, *h_out, *h_scr)

        @pl.when(last)
        def _():
            side.finish(s_in, s_out, s_sem)

    res = pl.pallas_call(
        fused, name=name, grid=grid, in_specs=in_specs + [ANY] * si, out_specs=out_specs + [ANY] * so,
        out_shape=out_shape + side.outs, scratch_shapes=scratch_shapes + side.sems, compiler_params=_cparams(),
    )(*args, *side.ins)
    return list(res[:no]), list(res[no:])


def _ffn_fwd(x, g, win, wout, tm, name, side=None):
    T, D = x.shape
    nc, tf = wout.shape[0], wout.shape[1]

    def body(x_ref, g_ref, win_ref, wout_ref, y_ref, h_ref, gu_ref, act_ref):
        j = pl.program_id(1)

        @pl.when(j == 0)
        def _():
            xv = x_ref[...]
            r = lax.rsqrt(jnp.mean(xv * xv, axis=-1, keepdims=True) + RMS_EPS)
            h_ref[...] = (xv * r * g_ref[...]).astype(BF16)
            y_ref[...] = jnp.zeros_like(y_ref)

        hb = h_ref[...]
        gt = _nt(win_ref[0], hb)
        up = _nt(win_ref[1], hb)
        gu_ref[0] = gt.astype(BF16)
        gu_ref[1] = up.astype(BF16)
        act = (gt * jax.nn.sigmoid(gt) * up).astype(BF16)
        act_ref[...] = act
        y_ref[...] += _tn(act, wout_ref[...])

        @pl.when(j == nc - 1)
        def _():
            y_ref[...] = x_ref[...] + 0.5 * y_ref[...]

    return _call(
        body, name=name, grid=(T // tm, nc), side=side,
        in_specs=[pl.BlockSpec((tm, D), lambda i, j: (i, 0)),
                  pl.BlockSpec((1, D), lambda i, j: (0, 0)),
                  pl.BlockSpec((2, None, tf, D), lambda i, j: (0, j, 0, 0)),
                  pl.BlockSpec((None, tf, D), lambda i, j: (j, 0, 0))],
        out_specs=[pl.BlockSpec((tm, D), lambda i, j: (i, 0)),
                   pl.BlockSpec((tm, D), lambda i, j: (i, 0)),
                   pl.BlockSpec((2, None, tf, tm), lambda i, j: (0, j, 0, i)),
                   pl.BlockSpec((None, tf, tm), lambda i, j: (j, 0, i))],
        out_shape=[_sds((T, D), F32), _sds((T, D), BF16), _sds((2, nc, tf, T), BF16), _sds((nc, tf, T), BF16)],
        args=(x, g, win, wout))


def _ffn_bwd(dy, gu, win, wout, tm, name, side=None):
    T, D = dy.shape
    nc, tf = wout.shape[0], wout.shape[1]

    def body(dy_ref, gu_ref, win_ref, wout_ref, dh_ref, dgu_ref, dyb_ref):
        j = pl.program_id(1)

        @pl.when(j == 0)
        def _():
            dh_ref[...] = jnp.zeros_like(dh_ref)
            dyb_ref[...] = (0.5 * dy_ref[...]).astype(BF16)

        dact = _nt(wout_ref[...], dyb_ref[...])
        gt = gu_ref[0].astype(F32)
        up = gu_ref[1].astype(F32)
        s = jax.nn.sigmoid(gt)
        dgb = (dact * up * (s * (1.0 + gt * (1.0 - s)))).astype(BF16)
        dub = (dact * (gt * s)).astype(BF16)
        dgu_ref[0] = dgb
        dgu_ref[1] = dub
        dh_ref[...] += _tn(dgb, win_ref[0]) + _tn(dub, win_ref[1])

    return _call(
        body, name=name, grid=(T // tm, nc), side=side,
        in_specs=[pl.BlockSpec((tm, D), lambda i, j: (i, 0)),
                  pl.BlockSpec((2, None, tf, tm), lambda i, j: (0, j, 0, i)),
                  pl.BlockSpec((2, None, tf, D), lambda i, j: (0, j, 0, 0)),
                  pl.BlockSpec((None, tf, D), lambda i, j: (j, 0, 0))],
        out_specs=[pl.BlockSpec((tm, D), lambda i, j: (i, 0)),
                   pl.BlockSpec((2, None, tf, tm), lambda i, j: (0, j, 0, i)),
                   pl.BlockSpec((tm, D), lambda i, j: (i, 0))],
        out_shape=[_sds((T, D), F32), _sds((2, nc, tf, T), BF16), _sds((T, D), BF16)],
        args=(dy, gu, win, wout))


def _rms_bwd(x, g, dh, dres, tm, name, side=None):
    T, D = x.shape

    def body(x_ref, g_ref, dh_ref, dres_ref, dx_ref, dg_ref):
        @pl.when(pl.program_id(0) == 0)
        def _():
            dg_ref[...] = jnp.zeros_like(dg_ref)

        xv = x_ref[...]
        r = lax.rsqrt(jnp.mean(xv * xv, axis=-1, keepdims=True) + RMS_EPS)
        xhat = xv * r
        dh = dh_ref[...]
        gd = dh * g_ref[...]
        dx_ref[...] = dres_ref[...] + r * (gd - xhat * jnp.mean(gd * xhat, axis=-1, keepdims=True))
        dg_ref[...] += jnp.sum(dh * xhat, axis=0, keepdims=True)

    row = pl.BlockSpec((tm, D), lambda i: (i, 0))
    one = pl.BlockSpec((1, D), lambda i: (0, 0))
    return _call(body, name=name, grid=(T // tm,), side=side, in_specs=[row, one, row, row], out_specs=[row, one],
                 out_shape=[_sds((T, D), F32), _sds((1, D), F32)], args=(x, g, dh, dres))


def _mm_tn(a, b, *, bm, bn, bt, name, b_slabs=False, side=None, rows=None, m_off=0, into=None, a_rows=False,
           twin=False):
    nz, T, M = (a.shape[0], a.shape[2], a.shape[1]) if a_rows else a.shape
    if b_slabs:
        N = b.shape[1] // NSLAB
        assert bt == T // NSLAB
        b_spec = pl.BlockSpec((bt, bn), lambda n, z, m, t: (0, t * (N // bn) + n))
    else:
        N = b.shape[1]
        b_spec = pl.BlockSpec((bt, bn), lambda n, z, m, t: (t, n))
    assert M % bm == 0 and N % bn == 0 and T % bt == 0, (M, bm, N, bn, T, bt)

    def body(a_ref, b_ref, *rest):
        c_ref = rest[-2] if twin else rest[-1]

        @pl.when(pl.program_id(3) == 0)
        def _():
            c_ref[...] = jnp.zeros_like(c_ref)

        ab, bb = a_ref[...].astype(BF16), b_ref[...].astype(BF16)
        c_ref[...] += _nn(ab, bb) if a_rows else _tn(ab, bb)
        if twin:
            @pl.when(pl.program_id(3) == T // bt - 1)
            def _():
                rest[-1][...] = c_ref[...].astype(BF16)

    grid = (N // bn, nz, M // bm, T // bt)
    a_spec = (pl.BlockSpec((None, bm, bt), lambda n, z, m, t: (z, m, t)) if a_rows
              else pl.BlockSpec((None, bt, bm), lambda n, z, m, t: (z, t, m)))
    in_specs = [a_spec, b_spec]
    out_spec = pl.BlockSpec((None, bm, bn), lambda n, z, m, t: (z, m + m_off, n))
    out_shape = _sds((nz, M if rows is None else rows, N), F32)
    if into is not None:
        assert side is None and not twin and into.shape == out_shape.shape
        return pl.pallas_call(body, name=name, grid=grid, in_specs=in_specs + [ANY], out_specs=out_spec,
                              out_shape=out_shape, input_output_aliases={2: 0}, compiler_params=_cparams())(a, b, into)
    outs = [out_shape] + ([_sds(out_shape.shape, BF16)] if twin else [])
    res = _call(body, name=name, grid=grid, side=side, in_specs=in_specs, out_specs=[out_spec] * len(outs),
                out_shape=outs, args=(a, b))
    mine = res if side is None else res[0]
    mine = tuple(mine) if twin else mine[0]
    return mine if side is None else (mine, res[1])


def _tok_spec(layout, tm, n16, C, bc, colmap):
    if layout == "nat":
        return pl.BlockSpec((tm, bc), lambda i, k: (i, colmap(k)))
    assert tm % n16 == 0
    if layout == "slab":
        return pl.BlockSpec((tm // n16, n16, bc), lambda i, k: (i, 0, colmap(k)))
    assert bc == C
    return pl.BlockSpec((n16, (tm // n16) * C), lambda i, k: (0, i))


def _tok_load(ref, layout, sp):
    if layout == "nat":
        return ref[...]
    if layout == "slab":
        return ref[...].reshape(-1, ref.shape[-1])
    c = ref.shape[1] // sp
    return jnp.concatenate([ref[:, s * c:(s + 1) * c] for s in range(sp)], axis=0)


def _tok_store(ref, layout, sp, val, cols=None, accumulate=False):
    def put(idx, v):
        if accumulate:
            ref[idx] += v
        else:
            ref[idx] = v

    lanes = slice(None) if cols is None else slice(cols[0], cols[0] + cols[1])
    if layout == "nat":
        put((slice(None), lanes), val)
    elif layout == "slab":
        put((slice(None), slice(None), lanes), val.reshape(sp, ref.shape[1], val.shape[-1]))
    else:
        assert cols is None
        c, n = ref.shape[1] // sp, ref.shape[0]
        for s in range(sp):
            put((slice(None), slice(s * c, (s + 1) * c)), val[s * n:(s + 1) * n])


def _proj(x, g, wt, gains, modes, *, tn, w_off, slabs, tm, normed_dtype, name):
    T, D = x.shape
    ntile = len(modes)
    N = ntile * tn
    n16 = T // NSLAB
    in_layout, out_layout = ("view", "slab") if slabs else ("nat", "nat")
    sp = tm // n16
    x_in = x.reshape(n16, NSLAB * D) if slabs else x
    x_spec = _tok_spec(in_layout, tm, n16, D, D, lambda n: 0)
    oshape = lambda c: (NSLAB, n16, c) if slabs else (T, c)
    ospec = lambda bc, cm: _tok_spec(out_layout, tm, n16, None, bc, cm)

    def body(x_ref, g_ref, w_ref, gains_ref, raw_ref, nrm_ref, h_ref):
        n = pl.program_id(1)

        @pl.when(n == 0)
        def _():
            xv = _tok_load(x_ref, in_layout, sp)
            r = lax.rsqrt(jnp.mean(xv * xv, axis=-1, keepdims=True) + RMS_EPS)
            _tok_store(h_ref, out_layout, sp, (xv * r * g_ref[...]).astype(BF16))

        y = _nt(_tok_load(h_ref, out_layout, sp), w_ref[...])
        _tok_store(raw_ref, out_layout, sp, y)
        for t, mode in enumerate(modes):
            @pl.when(n == t)
            def _(t=t, mode=mode):
                if not mode:
                    _tok_store(nrm_ref, out_layout, sp, y.astype(nrm_ref.dtype))
                    return
                gain = gains_ref[t]
                for k in range(tn // HEAD):
                    yk = y[:, k * HEAD:(k + 1) * HEAD]
                    r = lax.rsqrt(jnp.mean(yk * yk, axis=-1, keepdims=True) + RMS_EPS)
                    _tok_store(nrm_ref, out_layout, sp, (yk * r * gain).astype(nrm_ref.dtype), cols=(k * HEAD, HEAD))

    return pl.pallas_call(
        body, name=name, grid=(T // tm, ntile),
        in_specs=[x_spec, pl.BlockSpec((1, D), lambda i, n: (0, 0)),
                  pl.BlockSpec((tn, D), lambda i, n: (n + w_off, 0)),
                  pl.BlockSpec((ntile, 1, HEAD), lambda i, n: (0, 0, 0))],
        out_specs=[ospec(tn, lambda n: n), ospec(tn, lambda n: n), ospec(D, lambda n: 0)],
        out_shape=[_sds(oshape(N), F32), _sds(oshape(N), normed_dtype), _sds(oshape(D), BF16)],
        compiler_params=_cparams(),
    )(x_in, g, wt, gains)


def _mm(a, w, *, nt, tk, tm, a_layout, out_layout, resid=None, name, w_off=0, n_out=None):
    if a_layout == "slab":
        T, K = a.shape[0] * a.shape[1], a.shape[2]
    else:
        T, K = a.shape
    N = (w.shape[0] if nt else w.shape[1]) if n_out is None else n_out
    n16 = T // NSLAB
    nk = K // tk
    sp = tm // n16
    a_in = a.reshape(n16, NSLAB * K) if a_layout == "view" else a
    w_spec = (pl.BlockSpec((N, tk), lambda i, k: (w_off, k)) if nt
              else pl.BlockSpec((tk, N), lambda i, k: (k + w_off, 0)))
    o_spec = _tok_spec(out_layout, tm, n16, N, N, lambda k: 0)
    oshape = {"nat": (T, N), "slab": (NSLAB, n16, N), "view": (n16, NSLAB * N)}[out_layout]
    has_resid = resid is not None

    def body(*refs):
        a_ref, w_ref = refs[0], refs[1]
        o_ref = refs[-1]
        k = pl.program_id(1)

        @pl.when(k == 0)
        def _():
            o_ref[...] = refs[2][...] if has_resid else jnp.zeros_like(o_ref)

        ab = _tok_load(a_ref, a_layout, sp).astype(BF16)
        _tok_store(o_ref, out_layout, sp, _nt(ab, w_ref[...]) if nt else _nn(ab, w_ref[...]), accumulate=True)

    ins = [a_in, w]
    in_specs = [_tok_spec(a_layout, tm, n16, K, tk, lambda k: k), w_spec]
    if has_resid:
        ins.append(resid.reshape(n16, NSLAB * N) if out_layout == "view" else resid)
        in_specs.append(o_spec)
    out = pl.pallas_call(
        body, name=name, grid=(T // tm, nk), in_specs=in_specs, out_specs=o_spec,
        out_shape=_sds(oshape, F32), compiler_params=_cparams(),
    )(*ins)
    return out.reshape(T, N) if out_layout == "view" else out


def _log_sigmoid(z):
    return jnp.minimum(z, 0.0) - jnp.log(1.0 + jnp.exp(-jnp.abs(z)))


def _fox_gate_fwd(f_raw, fbias, name):
    T = f_raw.shape[0]
    cb = 256

    def body(f_ref, b_ref, c_ref):
        row = lax.broadcasted_iota(jnp.int32, (cb, cb), 0)
        col = lax.broadcasted_iota(jnp.int32, (cb, cb), 1)
        tri = (col <= row).astype(F32)
        carry = jnp.zeros((1, HEAD), F32)
        for i in range(T // cb):
            lf = _log_sigmoid(f_ref[i * cb:(i + 1) * cb, :] + b_ref[...])
            c = jnp.dot(tri, lf, preferred_element_type=F32, precision=lax.Precision.HIGHEST) + carry
            c_ref[i * cb:(i + 1) * cb, :] = c
            carry = c[cb - 1:cb, :]

    return pl.pallas_call(body, name=name, out_shape=_sds((T, HEAD), F32), compiler_params=_cparams())(f_raw, fbias)


def _fox_gate_bwd(f_raw, fbias, dc, name):
    T = f_raw.shape[0]
    cb = 256

    def body(f_ref, b_ref, dc_ref, df_ref, db_ref):
        row = lax.broadcasted_iota(jnp.int32, (cb, cb), 0)
        col = lax.broadcasted_iota(jnp.int32, (cb, cb), 1)
        tri = (col >= row).astype(F32)
        carry = jnp.zeros((1, HEAD), F32)
        dbias = jnp.zeros((1, HEAD), F32)
        for i in reversed(range(T // cb)):
            dlf = jnp.dot(tri, dc_ref[i * cb:(i + 1) * cb, :], preferred_element_type=F32,
                          precision=lax.Precision.HIGHEST) + carry
            carry = dlf[0:1, :]
            z = f_ref[i * cb:(i + 1) * cb, :] + b_ref[...]
            df = dlf * jax.nn.sigmoid(-z)
            df_ref[i * cb:(i + 1) * cb, :] = df
            dbias = dbias + jnp.sum(df, axis=0, keepdims=True)
        db_ref[...] = dbias

    return pl.pallas_call(body, name=name, out_shape=[_sds((T, HEAD), F32), _sds((1, HEAD), F32)],
                          compiler_params=_cparams())(f_raw, fbias, dc)


def _fox_fwd(qkv, c_col, c_row, tq, name, side=None):
    T = qkv.shape[0]
    H = qkv.shape[1] // (3 * HEAD)
    nq = T // tq
    c_blocks = c_row.reshape(H, nq, 1, tq)

    def body(q_ref, k_ref, v_ref, cq_ref, ck_ref, o_ref, lse_ref):
        qi = pl.program_id(1)
        q, cq = q_ref[...], cq_ref[...]
        causal = lax.broadcasted_iota(jnp.int32, (tq, tq), 1) <= lax.broadcasted_iota(jnp.int32, (tq, tq), 0)

        def key_block(ki, carry, diagonal):
            m, l, acc = carry
            rows = pl.ds(pl.multiple_of(ki * tq, tq), tq)
            s = _nt(q, k_ref[rows, :]) * SCALE + cq - ck_ref[ki]
            if diagonal:
                s = jnp.where(causal, s, NEG)
            m_new = jnp.maximum(m, jnp.max(s, axis=-1, keepdims=True))
            alpha = jnp.exp(m - m_new)
            p = jnp.exp(s - m_new)
            l = alpha * l + jnp.sum(p, axis=-1, keepdims=True)
            acc = alpha * acc + _nn(p.astype(BF16), v_ref[rows, :])
            return m_new, l, acc

        init = (jnp.full((tq, 1), NEG, F32), jnp.zeros((tq, 1), F32), jnp.zeros((tq, HEAD), F32))
        carry = lax.fori_loop(0, qi, lambda ki, c: key_block(ki, c, False), init)
        m, l, acc = key_block(qi, carry, True)
        o_ref[...] = acc / l
        lse_ref[...] = m + jnp.log(l)

    return _call(
        body, name=name, grid=(H, nq), side=side,
        in_specs=[pl.BlockSpec((tq, HEAD), lambda h, qi: (qi, h)),
                  pl.BlockSpec((T, HEAD), lambda h, qi: (0, H + h)),
                  pl.BlockSpec((T, HEAD), lambda h, qi: (0, 2 * H + h)),
                  pl.BlockSpec((None, tq, 1), lambda h, qi: (h, qi, 0)),
                  pl.BlockSpec((None, nq, 1, tq), lambda h, qi: (h, 0, 0, 0))],
        out_specs=[pl.BlockSpec((tq, HEAD), lambda h, qi: (qi, h)),
                   pl.BlockSpec((None, tq, 1), lambda h, qi: (h, qi, 0))],
        out_shape=[_sds((T, H * HEAD), F32), _sds((H, T, 1), F32)],
        args=(qkv, qkv, qkv, c_col, c_blocks))


def _fox_bwd(qkv, c_col, c_row, out, dout, lse, tq, name, side=None):
    T = qkv.shape[0]
    H = qkv.shape[1] // (3 * HEAD)
    nq = T // tq

    def body(q_ref, k_ref, v_ref, cq_ref, ck_ref, o_ref, do_ref, lse_ref, dq_ref, dk_ref, dv_ref, dck_ref, dcq_ref,
             delta_s):
        ki = pl.program_id(1)

        @pl.when(ki == 0)
        def _():
            dq_ref[...] = jnp.zeros_like(dq_ref)
            dcq_ref[...] = jnp.zeros_like(dcq_ref)
            delta_s[...] = jnp.sum(do_ref[...] * o_ref[...], axis=-1, keepdims=True)

        k, v, ck = k_ref[...], v_ref[...], ck_ref[...]
        causal = lax.broadcasted_iota(jnp.int32, (tq, tq), 1) <= lax.broadcasted_iota(jnp.int32, (tq, tq), 0)

        def query_block(qi, carry, diagonal):
            dk, dv, dck = carry
            rows = pl.ds(pl.multiple_of(qi * tq, tq), tq)
            q = q_ref[rows, :]
            s = _nt(q, k) * SCALE + cq_ref[rows, :] - ck
            if diagonal:
                s = jnp.where(causal, s, NEG)
            p = jnp.exp(s - lse_ref[rows, :])
            dob = do_ref[rows, :].astype(BF16)
            ds = p * (_nt(dob, v) - delta_s[rows, :])
            dsb = ds.astype(BF16)
            dq_ref[rows, :] += _nn(dsb, k) * SCALE
            dcq_ref[rows, :] += jnp.sum(ds, axis=-1, keepdims=True)
            return dk + _tn(dsb, q), dv + _tn(p.astype(BF16), dob), dck - jnp.sum(ds, axis=0, keepdims=True)

        init = (jnp.zeros((tq, HEAD), F32), jnp.zeros((tq, HEAD), F32), jnp.zeros((1, tq), F32))
        carry = query_block(ki, init, True)
        dk, dv, dck = lax.fori_loop(ki + 1, nq, lambda qi, c: query_block(qi, c, False), carry)
        dk_ref[...] = dk * SCALE
        dv_ref[...] = dv
        dck_ref[...] = dck

    head = lambda off: pl.BlockSpec((T, HEAD), lambda h, ki: (0, off + h))
    col = pl.BlockSpec((None, T, 1), lambda h, ki: (h, 0, 0))
    return _call(
        body, name=name, grid=(H, nq), side=side,
        in_specs=[head(0),
                  pl.BlockSpec((tq, HEAD), lambda h, ki: (ki, H + h)),
                  pl.BlockSpec((tq, HEAD), lambda h, ki: (ki, 2 * H + h)),
                  col, pl.BlockSpec((None, 1, tq), lambda h, ki: (h, 0, ki)), head(0), head(0), col],
        out_specs=[head(0),
                   pl.BlockSpec((tq, HEAD), lambda h, ki: (ki, h)),
                   pl.BlockSpec((tq, HEAD), lambda h, ki: (ki, h)),
                   pl.BlockSpec((None, 1, tq), lambda h, ki: (h, 0, ki)), col],
        out_shape=[_sds((T, H * HEAD), F32), _sds((T, H * HEAD), F32), _sds((T, H * HEAD), F32), _sds((H, 1, T), F32),
                   _sds((H, T, 1), F32)],
        scratch_shapes=[pltpu.VMEM((T, 1), F32)],
        args=(qkv, qkv, qkv, c_col, c_row, out, dout, lse))


def _t5_bucket(dist):
    max_exact = NUM_BUCKETS // 2
    d = dist.astype(np.float32)
    large = max_exact + (np.log(np.maximum(d, np.float32(1.0)) / np.float32(max_exact))
                         / np.float32(math.log(MAX_DISTANCE / max_exact))
                         * np.float32(NUM_BUCKETS - max_exact)).astype(np.int32)
    large = np.minimum(large, NUM_BUCKETS - 1)
    return np.where(dist < max_exact, dist, large)


def _bucket_maps():
    maps = []
    for d in DILATIONS:
        e = NSLAB // d
        rows = BLK // e
        idx = np.arange(BLK)
        pos = e * (idx % rows) + idx // rows
        qpos = pos[:, None] + BLK
        kpos = np.concatenate([pos, pos + BLK])[None, :]
        delta = qpos - kpos
        band = (delta >= 0) & (delta <= BLK)
        bucket = _t5_bucket(np.clip(delta, 0, None) * d)
        maps.append(np.where(band, bucket, -1).astype(np.int32))
    return np.stack(maps)


def _dil_geometry(T):
    n16 = T // NSLAB
    geo = []
    for d in DILATIONS:
        e = NSLAB // d
        rows = BLK // e
        nblk = n16 // rows
        geo.append((d, e, rows, nblk))
    return geo


DIL_INTERLEAVE_FWD = {1: 4, 4: 8, 16: 8}
DIL_INTERLEAVE_BWD = {1: 8, 4: 8, 16: 8}


def _dil_interleave(per_step, nblocks):
    while per_step > 1 and (nblocks % per_step or nblocks // per_step < 2):
        per_step -= 1
    return per_step


def _dil_bias(tab_ref, bkt_ref, bias_s, h):
    for p in range(len(DILATIONS)):
        bk = bkt_ref[p]
        bias = jnp.full((BLK, 2 * BLK), NEG, F32)
        for b in range(NUM_BUCKETS):
            bias = jnp.where(bk == b, tab_ref[b, h], bias)
        bias_s[p] = bias


def _dil_rows(d, e, rows, sub, blk):
    start = pl.multiple_of(blk * rows, rows)
    return [(sub + d * j, pl.ds(start, rows)) for j in range(e)]


def _gather(ref, idx):
    return jnp.concatenate([ref[s, r, :] for s, r in idx], axis=0)


def _scatter(ref, idx, val, rows):
    for j, (s, r) in enumerate(idx):
        ref[s, r, :] = val[j * rows:(j + 1) * rows]


def _scatter_add(ref, idx, val, rows):
    for j, (s, r) in enumerate(idx):
        ref[s, r, :] += val[j * rows:(j + 1) * rows]


def _dil_fwd(qkv, table, name, side=None):
    n16 = qkv.shape[1]
    T = NSLAB * n16
    H = qkv.shape[2] // (3 * HEAD)
    geo = _dil_geometry(T)
    bkt = jnp.asarray(_bucket_maps())

    def body(tab_ref, bkt_ref, q_ref, k_ref, v_ref, o_ref, lse_ref, bias_s, m_s, l_s):
        h = pl.program_id(0)
        _dil_bias(tab_ref, bkt_ref, bias_s, h)
        first_mask = lax.broadcasted_iota(jnp.int32, (BLK, 2 * BLK), 1) < BLK

        starts = len(DILATIONS) - 1

        def load(p, d, e, rows, sub, blk):
            cur = _dil_rows(d, e, rows, sub, blk)
            prev = _dil_rows(d, e, rows, sub, jnp.maximum(blk - 1, 0))
            q = _gather(q_ref, cur).astype(BF16)
            kk = jnp.concatenate([_gather(k_ref, prev), _gather(k_ref, cur)], axis=0).astype(BF16)
            vv = jnp.concatenate([_gather(v_ref, prev), _gather(v_ref, cur)], axis=0).astype(BF16)
            old = None if p == starts else (_gather(m_s, cur), _gather(l_s, cur), _gather(o_ref, cur))
            return cur, blk, q, kk, vv, old

        def compute(p, blk, q, kk, vv, old):
            s = _nt(q, kk) * SCALE + bias_s[p]
            s = jnp.where(first_mask & (blk == 0), NEG, s)
            m_blk = jnp.max(s, axis=-1, keepdims=True)
            if old is None:
                m_new = m_blk
                pr = jnp.exp(s - m_new)
                l_new = jnp.sum(pr, axis=-1, keepdims=True)
                acc = _nn(pr.astype(BF16), vv)
            else:
                m_old, l_old, acc_old = old
                m_new = jnp.maximum(m_old, m_blk)
                alpha = jnp.exp(m_old - m_new)
                pr = jnp.exp(s - m_new)
                l_new = alpha * l_old + jnp.sum(pr, axis=-1, keepdims=True)
                acc = alpha * acc_old + _nn(pr.astype(BF16), vv)
            if p == 0:
                return acc / l_new, m_new + jnp.log(l_new), None
            return acc, m_new, l_new

        def store(p, rows, cur, acc, m_new, l_new):
            _scatter(o_ref, cur, acc, rows)
            if p == 0:
                _scatter(lse_ref, cur, m_new, rows)
            else:
                _scatter(m_s, cur, m_new, rows)
                _scatter(l_s, cur, l_new, rows)

        for p in reversed(range(len(DILATIONS))):
            d, e, rows, nblk = geo[p]
            per_step = _dil_interleave(DIL_INTERLEAVE_FWD[d], d * nblk)

            def step(i, carry, p=p, d=d, e=e, rows=rows, nblk=nblk, per_step=per_step):
                ids = [i + u * (d * nblk // per_step) for u in range(per_step)]
                loaded = [load(p, d, e, rows, j // nblk, j % nblk) for j in ids]
                done = [(cur, compute(p, blk, q, kk, vv, old)) for cur, blk, q, kk, vv, old in loaded]
                for cur, res in done:
                    store(p, rows, cur, *res)
                return carry

            lax.fori_loop(0, d * nblk // per_step, step, 0)

    head = lambda off: pl.BlockSpec((NSLAB, n16, HEAD), lambda h: (0, 0, off + h))
    return _call(
        body, name=name, grid=(H,), side=side,
        in_specs=[pl.BlockSpec(memory_space=pltpu.SMEM), pl.BlockSpec((3, BLK, 2 * BLK), lambda h: (0, 0, 0)),
                  head(0), head(H), head(2 * H)],
        out_specs=[head(0), pl.BlockSpec((None, NSLAB, n16, 1), lambda h: (h, 0, 0, 0))],
        out_shape=[_sds((NSLAB, n16, H * HEAD), F32), _sds((H, NSLAB, n16, 1), F32)],
        scratch_shapes=[pltpu.VMEM((3, BLK, 2 * BLK), F32), pltpu.VMEM((NSLAB, n16, 1), F32),
                        pltpu.VMEM((NSLAB, n16, 1), F32)],
        args=(table, bkt, qkv, qkv, qkv))


def _dil_bwd(qkv, table, out, dout, lse, name, side=None):
    n16 = qkv.shape[1]
    T = NSLAB * n16
    H = qkv.shape[2] // (3 * HEAD)
    geo = _dil_geometry(T)
    bkt = jnp.asarray(_bucket_maps())

    def body(tab_ref, bkt_ref, q_ref, k_ref, v_ref, o_ref, do_ref, lse_ref,
             dq_ref, dk_ref, dv_ref, dtab_ref, bias_s, dbias_s, delta_s):
        h = pl.program_id(0)
        _dil_bias(tab_ref, bkt_ref, bias_s, h)
        first_mask = lax.broadcasted_iota(jnp.int32, (BLK, 2 * BLK), 1) < BLK
        dbias_s[...] = jnp.zeros_like(dbias_s)
        dq_ref[...] = jnp.zeros_like(dq_ref)
        dk_ref[...] = jnp.zeros_like(dk_ref)
        dv_ref[...] = jnp.zeros_like(dv_ref)
        for r in range(NSLAB):
            delta_s[r] = jnp.sum(do_ref[r] * o_ref[r], axis=-1, keepdims=True)

        def load(d, e, rows, sub, blk):
            cur = _dil_rows(d, e, rows, sub, blk)
            prev = _dil_rows(d, e, rows, sub, jnp.maximum(blk - 1, 0))
            q = _gather(q_ref, cur).astype(BF16)
            kk = jnp.concatenate([_gather(k_ref, prev), _gather(k_ref, cur)], axis=0).astype(BF16)
            vv = jnp.concatenate([_gather(v_ref, prev), _gather(v_ref, cur)], axis=0).astype(BF16)
            dob = _gather(do_ref, cur).astype(BF16)
            return cur, prev, blk, q, kk, vv, dob, _gather(lse_ref, cur), _gather(delta_s, cur)

        def compute(p, blk, q, kk, vv, dob, lse, delta):
            s = _nt(q, kk) * SCALE + bias_s[p]
            s = jnp.where(first_mask & (blk == 0), NEG, s)
            pr = jnp.exp(s - lse)
            ds = pr * (_nt(dob, vv) - delta)
            dsb = ds.astype(BF16)
            return ds, _nn(dsb, kk) * SCALE, _tn(dsb, q) * SCALE, _tn(pr.astype(BF16), dob)

        def store(rows, cur, prev, dq, dkk, dvv):
            _scatter_add(dq_ref, cur, dq, rows)
            _scatter_add(dk_ref, prev, dkk[:BLK], rows)
            _scatter_add(dk_ref, cur, dkk[BLK:], rows)
            _scatter_add(dv_ref, prev, dvv[:BLK], rows)
            _scatter_add(dv_ref, cur, dvv[BLK:], rows)

        for p in range(len(DILATIONS)):
            d, e, rows, nblk = geo[p]
            per_step = _dil_interleave(DIL_INTERLEAVE_BWD[d], d * nblk)

            def step(i, carry, p=p, d=d, e=e, rows=rows, nblk=nblk, per_step=per_step):
                ids = [i + u * (d * nblk // per_step) for u in range(per_step)]
                loaded = [load(d, e, rows, j // nblk, j % nblk) for j in ids]
                done = [(cur, prev, compute(p, *rest)) for cur, prev, *rest in loaded]
                dbias_s[p] += functools.reduce(jnp.add, [res[0] for _, _, res in done])
                for cur, prev, res in done:
                    store(rows, cur, prev, *res[1:])
                return carry

            lax.fori_loop(0, d * nblk // per_step, step, 0)

        lane = lax.broadcasted_iota(jnp.int32, (1, HEAD), 1)
        row = jnp.zeros((1, HEAD), F32)
        for b in range(NUM_BUCKETS):
            tot = jnp.zeros((1, 1), F32)
            for p in range(len(DILATIONS)):
                hit = jnp.where(bkt_ref[p] == b, dbias_s[p], 0.0)
                tot = tot + jnp.sum(jnp.sum(hit, axis=0, keepdims=True), axis=1, keepdims=True)
            row = jnp.where(lane == b, tot, row)
        dtab_ref[...] = row

    head = lambda off: pl.BlockSpec((NSLAB, n16, HEAD), lambda h: (0, 0, off + h))
    return _call(
        body, name=name, grid=(H,), side=side,
        in_specs=[pl.BlockSpec(memory_space=pltpu.SMEM), pl.BlockSpec((3, BLK, 2 * BLK), lambda h: (0, 0, 0)),
                  head(0), head(H), head(2 * H), head(0), head(0),
                  pl.BlockSpec((None, NSLAB, n16, 1), lambda h: (h, 0, 0, 0))],
        out_specs=[head(0), head(0), head(0), pl.BlockSpec((None, 1, HEAD), lambda h: (h, 0, 0))],
        out_shape=[_sds((NSLAB, n16, H * HEAD), F32)] * 3 + [_sds((H, 1, HEAD), F32)],
        scratch_shapes=[pltpu.VMEM((3, BLK, 2 * BLK), F32), pltpu.VMEM((3, BLK, 2 * BLK), F32),
                        pltpu.VMEM((NSLAB, n16, 1), F32)],
        args=(table, bkt, qkv, qkv, qkv, out, dout, lse))


def _qknorm_bwd(raw, dq, dk, dv, gains, tm, name):
    T, N = raw.shape
    C = N // 3

    def body(raw_ref, dq_ref, dk_ref, dv_ref, gains_ref, dp_ref, dg_ref):
        @pl.when(pl.program_id(0) == 0)
        def _():
            dg_ref[...] = jnp.zeros_like(dg_ref)

        for t, d_ref in enumerate((dq_ref, dk_ref)):
            gain = gains_ref[t]
            dgain = jnp.zeros((1, HEAD), F32)
            for k in range(C // HEAD):
                y = raw_ref[:, t * C + k * HEAD:t * C + (k + 1) * HEAD]
                dn = d_ref[:, k * HEAD:(k + 1) * HEAD]
                r = lax.rsqrt(jnp.mean(y * y, axis=-1, keepdims=True) + RMS_EPS)
                yhat = y * r
                gd = dn * gain
                dy = r * (gd - yhat * jnp.mean(gd * yhat, axis=-1, keepdims=True))
                dp_ref[:, t * C + k * HEAD:t * C + (k + 1) * HEAD] = dy.astype(BF16)
                dgain = dgain + jnp.sum(dn * yhat, axis=0, keepdims=True)
            dg_ref[t] += dgain
        dp_ref[:, 2 * C:] = dv_ref[...].astype(BF16)

    third = pl.BlockSpec((tm, C), lambda i: (i, 0))
    return pl.pallas_call(
        body, name=name, grid=(T // tm,),
        in_specs=[pl.BlockSpec((tm, N), lambda i: (i, 0)), third, third, third,
                  pl.BlockSpec((2, 1, HEAD), lambda i: (0, 0, 0))],
        out_specs=[pl.BlockSpec((tm, N), lambda i: (i, 0)), pl.BlockSpec((2, 1, HEAD), lambda i: (0, 0, 0))],
        out_shape=[_sds((T, N), BF16), _sds((2, 1, HEAD), F32)], compiler_params=_cparams(),
    )(raw, dq, dk, dv, gains)


def _loss_grad(y, target, tm, name):
    T, D = y.shape

    def body(y_ref, t_ref, dy_ref, loss_ref):
        @pl.when(pl.program_id(0) == 0)
        def _():
            loss_ref[...] = jnp.zeros_like(loss_ref)

        err = y_ref[...] - t_ref[...]
        dy_ref[...] = err * (1.0 / D)
        per_tok = jnp.mean(err * err, axis=-1, keepdims=True)
        tot = 0.5 * jnp.sum(per_tok, axis=0, keepdims=True)
        lane = lax.broadcasted_iota(jnp.int32, (1, HEAD), 1)
        loss_ref[...] += jnp.where(lane == 0, tot, 0.0)

    row = pl.BlockSpec((tm, D), lambda i: (i, 0))
    return pl.pallas_call(
        body, name=name, grid=(T // tm,), in_specs=[row, row],
        out_specs=[row, pl.BlockSpec((1, HEAD), lambda i: (0, 0))],
        out_shape=[_sds((T, D), F32), _sds((1, HEAD), F32)], compiler_params=_cparams(),
    )(y, target)


def _pad_lanes(v, width=HEAD):
    return jnp.pad(v, ((0, 0), (0, width - v.shape[1])))


def _local_step(x, target, small, wts, plan=None):
    grads = {}

    def hosted(host, fn, *args, **kw):
        side = plan.before(host, wts, grads) if plan is not None else None
        if side is None:
            return fn(*args, name=host, **kw)
        res, side_res = fn(*args, name=host, side=side, **kw)
        plan.after(host, side_res, wts, grads)
        return res

    T, D = x.shape
    C = D // 2
    H = C // HEAD
    n16 = T // NSLAB
    tm = min(512, T)
    tmm = min(1024, T)
    tms = 4 * n16
    tq = min(512, T)
    bn = min(1024, D)
    g1, gm, g2 = small["ffn1_norm"], small["mix_norm"], small["ffn2_norm"]
    gains_a = jnp.stack([small["q_norm_a"], small["k_norm_a"], jnp.ones_like(small["q_norm_a"])])
    gains_b = jnp.stack([small["q_norm_b"], small["k_norm_b"], jnp.ones_like(small["q_norm_b"])])
    fbias = _pad_lanes(small["forget_bias"])
    table = small["rel_bias_table"]

    x1, h1, gu1, act1 = hosted("ffn1_fwd", _ffn_fwd, x, g1, wts["ffn1_in"], wts["ffn1_out"], tm)
    w_in_t, w_f_t, w_o = wts["w_in_t"], wts["w_f_t"], wts["w_o"]
    raw_a, nrm_a, h2a = _proj(x1, gm, w_in_t, gains_a, (True, True, False), tn=C, w_off=0, slabs=True, tm=tms,
                              normed_dtype=F32, name="proj_a")
    raw_b, nrm_b, h2b = _proj(x1, gm, w_in_t, gains_b, (True, True, False), tn=C, w_off=3, slabs=False, tm=tmm,
                              normed_dtype=BF16, name="proj_b")
    f_raw, _, _ = _proj(x1, gm, w_f_t, gains_b[:1], (False,), tn=HEAD, w_off=0, slabs=False, tm=tm,
                        normed_dtype=BF16, name="proj_f")
    c = _fox_gate_fwd(f_raw, fbias, "fox_gate_fwd")
    c_heads = c[:, :H].T
    c_col, c_row = c_heads[:, :, None], c_heads[:, None, :]
    out_a, lse_a = hosted("dil_fwd", _dil_fwd, nrm_a, table)
    out_b, lse_b = hosted("fox_fwd", _fox_fwd, nrm_b, c_col, c_row, tq)
    x2a = _mm(out_a, w_o, nt=False, tk=C, tm=tms, a_layout="slab", out_layout="view", resid=x1, name="out_a")
    x2 = _mm(out_b, w_o, nt=False, tk=C, tm=tmm, a_layout="nat", out_layout="nat", resid=x2a, w_off=1, name="out_b")
    y, h3, gu3, act3 = _ffn_fwd(x2, g2, wts["ffn2_in"], wts["ffn2_out"], tm, "ffn2_fwd")
    dy, loss_row = _loss_grad(y, target, tm, "loss_grad")

    def ffn_backward(tag, xin, g, h, gu, act, win, wout, dres):
        nc, tf = wout.shape[0], wout.shape[1]
        dh, dgu, dyb = hosted(tag + "_bwd", _ffn_bwd, dres, gu, win, wout, tm)
        grads[tag + "_w_in_t"], grads[tag + "_w_in_t_bf16"] = hosted(
            tag + "_dwin", _mm_tn, dgu.reshape(2 * nc, tf, T), h, bm=tf, bn=bn, bt=T, a_rows=True, twin=True)
        dxin, grads[tag + "_norm"] = hosted(tag + "_rms_bwd", _rms_bwd, xin, g, dh, dres, tm)
        grads[tag + "_w_out"], grads[tag + "_w_out_bf16"] = hosted(
            tag + "_dwout", _mm_tn, act, dyb, bm=tf, bn=bn, bt=T, a_rows=True, twin=True)
        return dxin

    dx2 = ffn_backward("ffn2", x2, g2, h3, gu3, act3, wts["ffn2_in"], wts["ffn2_out"], dy)

    dmix_a = _mm(dx2, w_o, nt=True, tk=D, tm=tms, a_layout="view", out_layout="slab", n_out=C, name="dmix_a")
    dmix_b = _mm(dx2, w_o, nt=True, tk=D, tm=tmm, a_layout="nat", out_layout="nat", n_out=C, w_off=1, name="dmix_b")
    dwo = _mm_tn(out_a.reshape(1, T, C), dx2.reshape(n16, NSLAB * D), bm=C, bn=bn, bt=n16, b_slabs=True,
                 rows=2 * C, name="dwo_a")
    dwo = _mm_tn(out_b.reshape(1, T, C), dx2, bm=C, bn=bn, bt=tm, rows=2 * C, m_off=1, into=dwo, name="dwo_b")
    grads["w_out"] = dwo[0]

    dqa, dka, dva, dtab = hosted("dil_bwd", _dil_bwd, nrm_a, table, out_a, dmix_a, lse_a)
    dqb, dkb, dvb, dck, dcq = hosted("fox_bwd", _fox_bwd, nrm_b, c_col, c_row, out_b, dmix_b, lse_b, tq)
    grads["rel_bias_table"] = dtab[:, 0, :NUM_BUCKETS].T
    dc = _pad_lanes((dck[:, 0, :] + dcq[:, :, 0]).T)
    df, dfb = _fox_gate_bwd(f_raw, fbias, dc, "fox_gate_bwd")
    grads["forget_bias"] = dfb[:, :H]

    flat = lambda a: a.reshape(T, a.shape[-1])
    dproj_a, dgain_a = _qknorm_bwd(flat(raw_a), flat(dqa), flat(dka), flat(dva), gains_a[:2], min(256, T), "qknorm_bwd_a")
    dproj_b, dgain_b = _qknorm_bwd(raw_b, dqb, dkb, dvb, gains_b[:2], min(256, T), "qknorm_bwd_b")
    grads["q_norm_a"], grads["k_norm_a"] = dgain_a[0], dgain_a[1]
    grads["q_norm_b"], grads["k_norm_b"] = dgain_b[0], dgain_b[1]
    dproj_a = dproj_a.reshape(NSLAB, n16, 3 * C)

    dh2 = _mm(dproj_a, w_in_t, nt=False, tk=C, tm=tms, a_layout="slab", out_layout="view", name="dh2_a")
    dh2 = _mm(dproj_b, w_in_t, nt=False, tk=C, tm=tmm, a_layout="nat", out_layout="nat", resid=dh2, w_off=3, name="dh2_b")
    dh2 = _mm(df, w_f_t, nt=False, tk=HEAD, tm=tmm, a_layout="nat", out_layout="nat", resid=dh2, name="dh2_f")
    dx1, grads["mix_norm"] = _rms_bwd(x1, gm, dh2, dx2, tm, "mix_rms_bwd")
    bt = min(2048, T)
    dwt = _mm_tn(flat(dproj_a)[None], flat(h2a), bm=C, bn=bn, bt=bt, rows=6 * C + H, name="dw_a")
    dwt = _mm_tn(dproj_b[None], h2b, bm=C, bn=bn, bt=bt, rows=6 * C + H, m_off=3, into=dwt, name="dw_b")
    dwt = _mm_tn(df[None, :, :H], h2b, bm=H, bn=bn, bt=bt, rows=6 * C + H, m_off=6 * C // H, into=dwt, name="dw_f")
    grads["w_in_t"] = dwt[0]

    grad_x = ffn_backward("ffn1", x, g1, h1, gu1, act1, wts["ffn1_in"], wts["ffn1_out"], dx1)
    return loss_row, grad_x, grads


def _place():
    x, y, c = lax.axis_index("x"), lax.axis_index("y"), lax.axis_index("c")
    other_chips = [(1 - x, y), (x, 1 - y), (1 - x, 1 - y)]
    return x, y, c, other_chips


def _run_side(side, name):
    def body(*refs):
        si, so = len(side.ins), len(side.outs)
        side.start(refs[:si], refs[si:si + so], refs[si + so:])
        side.finish(refs[:si], refs[si:si + so], refs[si + so:])

    return pl.pallas_call(body, name=name, in_specs=[ANY] * len(side.ins), out_specs=[ANY] * len(side.outs),
                          out_shape=side.outs, scratch_shapes=side.sems)(*side.ins)


def _all_gather(shards):
    n = len(shards)

    def plan(ins, outs, sems):
        send_sems, recv_sems, local_sems = sems
        x, y, c, chips = _place()
        me, sibling = (x, y, c), (x, y, 1 - c)

        def copy(a, k, block, to, src=None):
            px, py, pc = block
            dst = outs[a].at[4 * px + 2 * py + pc]
            return pltpu.make_async_remote_copy(
                src_ref=dst if src is None else src, dst_ref=dst, send_sem=send_sems.at[7 * a + k],
                recv_sem=recv_sems.at[7 * a + k], device_id=to, device_id_type=MESH)

        mine = [pltpu.make_async_copy(ins[a], outs[a].at[4 * x + 2 * y + c], local_sems.at[a]) for a in range(n)]
        first = []
        for a in range(n):
            first.append(copy(a, 0, me, sibling, src=ins[a]))
            first += [copy(a, 1 + j, me, (*chip, c), src=ins[a]) for j, chip in enumerate(chips)]
        return copy, mine, first, me, sibling, c, chips

    def start(ins, outs, sems):
        _, mine, first, *_ = plan(ins, outs, sems)
        for cp in mine + first:
            cp.start()

    def finish(ins, outs, sems):
        copy, mine, first, me, sibling, c, chips = plan(ins, outs, sems)
        passed = []
        for a in range(n):
            for j, chip in enumerate(chips):
                copy(a, 1 + j, (*chip, c), me).wait_recv()
                fwd = copy(a, 4 + j, (*chip, c), sibling)
                fwd.start()
                passed.append(fwd)
        for a in range(n):
            copy(a, 0, sibling, me).wait_recv()
            for j, chip in enumerate(chips):
                copy(a, 4 + j, (*chip, 1 - c), me).wait_recv()
        for cp in first + passed:
            cp.wait_send()
        for cp in mine:
            cp.wait()

    return _Side(shards, [_sds((N_DEV,) + s.shape, s.dtype) for s in shards],
                 [pltpu.SemaphoreType.DMA((7 * n,)), pltpu.SemaphoreType.DMA((7 * n,)), pltpu.SemaphoreType.DMA((n,))],
                 start, finish)


def _all_gather_relayed(shards):
    n = len(shards)
    halves = [-(-(s.shape[0] // 2) // 16) * 16 for s in shards]

    def body_parts(ins, outs, sems):
        send_sems, recv_sems, local_sems = sems
        x, y, c, _ = _place()
        me, sib, xn, yn, dg = (x, y, c), (x, y, 1 - c), (1 - x, y, c), (x, 1 - y, c), (1 - x, 1 - y, c)

        def rows(a, block, part):
            px, py, pc = block
            whole = outs[a].at[4 * px + 2 * py + pc]
            if part is None:
                return whole
            return whole.at[pl.ds(0, halves[a])] if part == 0 else whole.at[pl.ds(halves[a], shards[a].shape[0] - halves[a])]

        def copy(a, k, block, part, to, src=None):
            dst = rows(a, block, part)
            return pltpu.make_async_remote_copy(
                src_ref=dst if src is None else src, dst_ref=dst, send_sem=send_sems.at[9 * a + k],
                recv_sem=recv_sems.at[9 * a + k], device_id=to, device_id_type=MESH)

        flip = lambda dev: (dev[0], dev[1], 1 - dev[2])
        mine = [pltpu.make_async_copy(ins[a], rows(a, me, None), local_sems.at[a]) for a in range(n)]
        own = [[copy(a, 0, me, None, sib, src=ins[a]), copy(a, 1, me, None, xn, src=ins[a]),
                copy(a, 2, me, None, yn, src=ins[a])] for a in range(n)]
        relays = lambda a: [(1, [copy(a, 3, xn, 0, yn), copy(a, 5, xn, None, sib)]),
                            (2, [copy(a, 4, yn, 1, xn), copy(a, 6, yn, None, sib)]),
                            (3, [copy(a, 7, dg, 0, sib)]), (4, [copy(a, 8, dg, 1, sib)])]
        lands = {0: (sib, None), 1: (xn, None), 2: (yn, None), 3: (dg, 0), 4: (dg, 1), 5: (flip(xn), None),
                 6: (flip(yn), None), 7: (flip(dg), 0), 8: (flip(dg), 1)}
        arrival = lambda a, k: copy(a, k, lands[k][0], lands[k][1], me)
        return mine, own, relays, arrival

    def start(ins, outs, sems):
        mine, own, _, _ = body_parts(ins, outs, sems)
        for cp in mine + [cp for per in own for cp in per]:
            cp.start()

    def finish(ins, outs, sems):
        mine, own, relays, arrival = body_parts(ins, outs, sems)
        sent = [cp for per in own for cp in per]
        relays = [relays(a) for a in range(n)]
        for stage in range(4):
            for a in range(n):
                after, passes = relays[a][stage]
                arrival(a, after).wait_recv()
                for cp in passes:
                    cp.start()
                sent += passes
        for a in range(n):
            for k in (0, 5, 6, 7, 8):
                arrival(a, k).wait_recv()
        for cp in sent:
            cp.wait_send()
        for cp in mine:
            cp.wait()

    return _Side(shards, [_sds((N_DEV,) + s.shape, s.dtype) for s in shards],
                 [pltpu.SemaphoreType.DMA((9 * n,)), pltpu.SemaphoreType.DMA((9 * n,)), pltpu.SemaphoreType.DMA((n,))],
                 start, finish)


def _exchange_in_chip(gs):
    n = len(gs)

    def copies(ins, outs, sems):
        x, y, c, _ = _place()
        return [pltpu.make_async_remote_copy(
            src_ref=ins[a].at[2 * q + 1 - c], dst_ref=outs[a].at[q], send_sem=sems[0].at[4 * a + q],
            recv_sem=sems[1].at[4 * a + q], device_id=(x, y, 1 - c), device_id_type=MESH)
            for a in range(n) for q in range(4)]

    def start(ins, outs, sems):
        for cp in copies(ins, outs, sems):
            cp.start()

    def finish(ins, outs, sems):
        for cp in copies(ins, outs, sems):
            cp.wait()

    return _Side(gs, [_sds((4,) + g.shape[1:], g.dtype) for g in gs],
                 [pltpu.SemaphoreType.DMA((4 * n,)), pltpu.SemaphoreType.DMA((4 * n,))], start, finish)


def _exchange_between_chips(ps):
    n = len(ps)

    def copies(ins, outs, sems):
        x, y, c, chips = _place()
        return [pltpu.make_async_remote_copy(
            src_ref=ins[a].at[2 * cx + cy], dst_ref=outs[a].at[j], send_sem=sems[0].at[3 * a + j],
            recv_sem=sems[1].at[3 * a + j], device_id=(cx, cy, c), device_id_type=MESH)
            for a in range(n) for j, (cx, cy) in enumerate(chips)]

    def start(ins, outs, sems):
        for cp in copies(ins, outs, sems):
            cp.start()

    def finish(ins, outs, sems):
        for cp in copies(ins, outs, sems):
            cp.wait()

    return _Side(ps, [_sds((3,) + p.shape[1:], p.dtype) for p in ps],
                 [pltpu.SemaphoreType.DMA((3 * n,)), pltpu.SemaphoreType.DMA((3 * n,))], start, finish)


def _all_reduce_small(v, name):
    R = v.shape[0]

    def body(v_ref, sum_ref, all_ref, send_sems, recv_sems):
        x, y, c, _ = _place()
        k = 4 * x + 2 * y + c
        all_ref[k] = v_ref[...]
        copies = []
        for rel in range(1, N_DEV):
            fx, fy, fc = (rel >> 2) & 1, (rel >> 1) & 1, rel & 1
            peer = (1 - x if fx else x, 1 - y if fy else y, 1 - c if fc else c)
            copies.append(pltpu.make_async_remote_copy(
                src_ref=v_ref, dst_ref=all_ref.at[k], send_sem=send_sems.at[rel - 1], recv_sem=recv_sems.at[rel - 1],
                device_id=peer, device_id_type=MESH))
        for cp in copies:
            cp.start()
        for rel in range(1, N_DEV):
            fx, fy, fc = (rel >> 2) & 1, (rel >> 1) & 1, rel & 1
            src = 4 * (1 - x if fx else x) + 2 * (1 - y if fy else y) + (1 - c if fc else c)
            pltpu.make_async_remote_copy(
                src_ref=v_ref, dst_ref=all_ref.at[src], send_sem=send_sems.at[rel - 1], recv_sem=recv_sems.at[rel - 1],
                device_id=(x, y, c), device_id_type=MESH).wait_recv()
        for cp in copies:
            cp.wait_send()
        tot = all_ref[0]
        for d in range(1, N_DEV):
            tot = tot + all_ref[d]
        sum_ref[...] = tot

    vm = pl.BlockSpec(memory_space=pltpu.VMEM)
    return pl.pallas_call(
        body, name=name, in_specs=[vm], out_specs=[vm, vm],
        out_shape=[_sds((R, HEAD), F32), _sds((N_DEV, R, HEAD), F32)],
        scratch_shapes=[pltpu.SemaphoreType.DMA((N_DEV - 1,)), pltpu.SemaphoreType.DMA((N_DEV - 1,))],
    )(v)[0]


def _tiles(rows, cols):
    tr = next((cand for cand in (688, 512, 256) if rows % cand == 0), rows)
    tc = 512 if (cols % 512 == 0 and tr * cols * 4 > (2 << 20)) else cols
    return tr, tc


def _chip_sum(g, r1, core, name):
    _, R, Cc = g.shape
    tr, tc = _tiles(R, Cc)

    def body(core_ref, g_ref, r_ref, p_ref):
        p_ref[...] = (g_ref[...] + r_ref[...]).astype(BF16)

    blk = lambda f: pl.BlockSpec((None, tr, tc), f)
    return pl.pallas_call(
        body, name=name,
        grid_spec=pltpu.PrefetchScalarGridSpec(
            num_scalar_prefetch=1, grid=(4, R // tr, Cc // tc),
            in_specs=[blk(lambda q, i, j, core: (2 * q + core[0], i, j)), blk(lambda q, i, j, core: (q, i, j))],
            out_specs=blk(lambda q, i, j, core: (q, i, j))),
        out_shape=_sds((4, R, Cc), BF16), compiler_params=_cparams(),
    )(core, g, r1)


def _adamw_update(gv, w_ref, m_ref, v_ref, d_ref, nm_ref, nv_ref):
    nm = B1 * m_ref[...] + (1.0 - B1) * gv
    nv = B2 * v_ref[...] + (1.0 - B2) * jnp.square(gv)
    m_hat = nm / (1.0 - B1 ** STEP)
    v_hat = nv / (1.0 - B2 ** STEP)
    d_ref[...] = -LR * (m_hat / (jnp.sqrt(v_hat) + EPS) + WD * w_ref[...])
    nm_ref[...] = nm
    nv_ref[...] = nv


def _reduce_adamw(g, r1, r2, where, w, m, v, name):
    _, R, Cc = g.shape
    tr, tc = _tiles(R, Cc)

    def body(where_ref, g_ref, r1_ref, r2_ref, w_ref, m_ref, v_ref, o_ref, d_ref, nm_ref, nv_ref):
        gv = ((g_ref[...] + r1_ref[...]) + r2_ref[0].astype(F32)) + (r2_ref[1].astype(F32) + r2_ref[2].astype(F32))
        o_ref[...] = gv
        _adamw_update(gv, w_ref, m_ref, v_ref, d_ref, nm_ref, nv_ref)

    blk = pl.BlockSpec((tr, tc), lambda i, j, w: (i, j))
    return pl.pallas_call(
        body, name=name,
        grid_spec=pltpu.PrefetchScalarGridSpec(
            num_scalar_prefetch=1, grid=(R // tr, Cc // tc),
            in_specs=[pl.BlockSpec((None, tr, tc), lambda i, j, w: (w[0], i, j)),
                      pl.BlockSpec((None, tr, tc), lambda i, j, w: (w[1], i, j)),
                      pl.BlockSpec((3, tr, tc), lambda i, j, w: (0, i, j)), blk, blk, blk],
            out_specs=[blk] * 4),
        out_shape=[_sds((R, Cc), F32)] * 4, compiler_params=_cparams(),
    )(where, g, r1, r2, w, m, v)


def _adamw(w, g, m, v, name):
    R, Cc = w.shape
    tr, tc = _tiles(R, Cc)

    def body(w_ref, g_ref, m_ref, v_ref, d_ref, nm_ref, nv_ref):
        _adamw_update(g_ref[...], w_ref, m_ref, v_ref, d_ref, nm_ref, nv_ref)

    blk = pl.BlockSpec((tr, tc), lambda i, j: (i, j))
    return pl.pallas_call(
        body, name=name, grid=(R // tr, Cc // tc), in_specs=[blk] * 4, out_specs=[blk] * 3,
        out_shape=[_sds((R, Cc), F32)] * 3, compiler_params=_cparams(),
    )(w, g, m, v)


SMALL = ("ffn1_norm", "mix_norm", "ffn2_norm", "q_norm_a", "k_norm_a", "q_norm_b", "k_norm_b", "forget_bias",
         "rel_bias_table")
LARGE = ("ffn1_w_in", "ffn1_w_out", "w_in", "w_out", "ffn2_w_in", "ffn2_w_out")
ORDER = ("ffn1_norm", "ffn1_w_in", "ffn1_w_out", "mix_norm", "w_in", "q_norm_a", "k_norm_a", "q_norm_b", "k_norm_b",
         "forget_bias", "rel_bias_table", "w_out", "ffn2_norm", "ffn2_w_in", "ffn2_w_out")


def _pack_small(vals):
    rows = []
    for name in SMALL:
        flat = vals[name].reshape(-1)
        pad = (-flat.shape[0]) % HEAD
        rows.append(jnp.pad(flat, (0, pad)).reshape(-1, HEAD))
    return jnp.concatenate(rows, axis=0)


def _unpack_small(packed, like):
    out, r = {}, 0
    for name in SMALL:
        size = like[name].size
        nrow = -(-size // HEAD)
        out[name] = packed[r:r + nrow].reshape(-1)[:size].reshape(like[name].shape)
        r += nrow
    return out


def kernel(x, ffn1_norm, ffn1_w_in, ffn1_w_out, mix_norm, w_in, q_norm_a, k_norm_a, q_norm_b, k_norm_b, forget_bias, rel_bias_table, w_out, ffn2_norm, ffn2_w_in, ffn2_w_out, loss_target, m_ffn1_norm, m_ffn1_w_in, m_ffn1_w_out, m_mix_norm, m_w_in, m_q_norm_a, m_k_norm_a, m_q_norm_b, m_k_norm_b, m_forget_bias, m_rel_bias_table, m_w_out, m_ffn2_norm, m_ffn2_w_in, m_ffn2_w_out, v_ffn1_norm, v_ffn1_w_in, v_ffn1_w_out, v_mix_norm, v_w_in, v_q_norm_a, v_k_norm_a, v_q_norm_b, v_k_norm_b, v_forget_bias, v_rel_bias_table, v_w_out, v_ffn2_norm, v_ffn2_w_in, v_ffn2_w_out):
    w = dict(ffn1_norm=ffn1_norm, ffn1_w_in=ffn1_w_in, ffn1_w_out=ffn1_w_out, mix_norm=mix_norm, w_in=w_in,
             q_norm_a=q_norm_a, k_norm_a=k_norm_a, q_norm_b=q_norm_b, k_norm_b=k_norm_b, forget_bias=forget_bias,
             rel_bias_table=rel_bias_table, w_out=w_out, ffn2_norm=ffn2_norm, ffn2_w_in=ffn2_w_in, ffn2_w_out=ffn2_w_out)
    m = dict(ffn1_norm=m_ffn1_norm, ffn1_w_in=m_ffn1_w_in, ffn1_w_out=m_ffn1_w_out, mix_norm=m_mix_norm, w_in=m_w_in,
             q_norm_a=m_q_norm_a, k_norm_a=m_k_norm_a, q_norm_b=m_q_norm_b, k_norm_b=m_k_norm_b,
             forget_bias=m_forget_bias, rel_bias_table=m_rel_bias_table, w_out=m_w_out, ffn2_norm=m_ffn2_norm,
             ffn2_w_in=m_ffn2_w_in, ffn2_w_out=m_ffn2_w_out)
    v = dict(ffn1_norm=v_ffn1_norm, ffn1_w_in=v_ffn1_w_in, ffn1_w_out=v_ffn1_w_out, mix_norm=v_mix_norm, w_in=v_w_in,
             q_norm_a=v_q_norm_a, k_norm_a=v_k_norm_a, q_norm_b=v_q_norm_b, k_norm_b=v_k_norm_b,
             forget_bias=v_forget_bias, rel_bias_table=v_rel_bias_table, w_out=v_w_out, ffn2_norm=v_ffn2_norm,
             ffn2_w_in=v_ffn2_w_in, ffn2_w_out=v_ffn2_w_out)
    T, D = x.shape[1], x.shape[2]
    C = D // 2
    H = C // HEAD
    ff_shard = ffn1_w_out.shape[1]

    f1i, f1o = _run_side(_all_gather_relayed([ffn1_w_in[0].T.astype(BF16), ffn1_w_out[0].astype(BF16)]), "gather_ffn1")
    wts = dict(ffn1_in=f1i.reshape(2, N_DEV, ff_shard, D), ffn1_out=f1o)
    xi, yi, ci = lax.axis_index("x"), lax.axis_index("y"), lax.axis_index("c")
    core = jnp.reshape(ci, (1,)).astype(jnp.int32)
    where = jnp.stack([4 * xi + 2 * yi + ci, 2 * xi + yi]).astype(jnp.int32)
    gs, r1, ps, r2 = {}, {}, {}, {}

    def by_destination(name, grads):
        key = name + "_t" if name.endswith("w_in") else name
        gs[name] = grads[key].reshape(N_DEV, -1, D)
        return grads.get(key + "_bf16", grads[key]).reshape(N_DEV, -1, D)

    def chip_sums(names):
        for name in names:
            ps[name] = _chip_sum(gs[name], r1[name], core, "chip_sum_" + name)
        return [ps[name] for name in names]

    class Plan:
        carried = {"ffn1_fwd": ("gather", ("w_in", "w_out")),
                   "dil_fwd": ("gather", ("ffn2_w_out",)), "fox_fwd": ("gather", ("ffn2_w_in",)),
                   "dil_bwd": ("in_chip", ("ffn2_w_in", "ffn2_w_out")), "fox_bwd": ("between", ("ffn2_w_in", "ffn2_w_out")),
                   "ffn1_bwd": ("in_chip", ("w_in", "w_out")), "ffn1_dwin": ("between", ("w_in", "w_out")),
                   "ffn1_rms_bwd": ("in_chip", ("ffn1_w_in",)), "ffn1_dwout": ("between", ("ffn1_w_in",))}

        def before(self, host, wts, grads):
            if host not in self.carried:
                return None
            kind, names = self.carried[host]
            if kind == "gather":
                return _all_gather([(w[n][0].T if n.endswith("w_in") else w[n][0]).astype(BF16) for n in names])
            if kind == "in_chip":
                return _exchange_in_chip([by_destination(n, grads) for n in names])
            return _exchange_between_chips(chip_sums(names))

        def after(self, host, res, wts, grads):
            kind, names = self.carried[host]
            if host == "ffn1_fwd":
                w_in_t = res[0].reshape(-1, D)
                wts.update(w_in_t=w_in_t, w_f_t=jnp.pad(w_in_t[6 * C:], ((0, HEAD - H), (0, 0))),
                           w_o=res[1].reshape(2 * C, D))
            elif host == "dil_fwd":
                wts.update(ffn2_out=res[0])
            elif host == "fox_fwd":
                wts.update(ffn2_in=res[0].reshape(2, N_DEV, ff_shard, D))
            else:
                (r1 if kind == "in_chip" else r2).update(zip(names, res))

    small = {name: w[name] for name in SMALL}
    loss_row, grad_x, grads = _local_step(x[0], loss_target[0], small, wts, Plan())

    tail = ("ffn1_w_out",)
    r1[tail[0]], = _run_side(_exchange_in_chip([by_destination(tail[0], grads)]), "reduce_in_chip_tail")
    r2.update(zip(tail, _run_side(_exchange_between_chips(chip_sums(tail)), "reduce_between_chips_tail")))

    packed = _pack_small(grads)
    nsmall = packed.shape[0]
    packed = jnp.concatenate([packed, loss_row, jnp.zeros(((-nsmall - 1) % 8, HEAD), F32)], axis=0)
    reduced = _all_reduce_small(packed, "reduce_small")
    loss = reduced[nsmall, 0]
    g_small = _unpack_small(reduced[:nsmall], small)

    grad, delta, new_m, new_v = dict(g_small), {}, {}, {}
    for name in LARGE:
        to = (lambda t: t[0].T) if name.endswith("w_in") else (lambda t: t[0])
        back = (lambda t: t.T[None]) if name.endswith("w_in") else (lambda t: t[None])
        res = _reduce_adamw(gs[name], r1[name], r2[name], where, to(w[name]), to(m[name]), to(v[name]), "adamw_" + name)
        grad[name], delta[name], new_m[name], new_v[name] = (back(t) for t in res)
    d, nm, nv = _adamw(_pack_small(w), reduced[:nsmall], _pack_small(m), _pack_small(v), "adamw_small")
    delta.update(_unpack_small(d, small))
    new_m.update(_unpack_small(nm, small))
    new_v.update(_unpack_small(nv, small))
    return (loss, grad_x[None], *[grad[n] for n in ORDER], *[delta[n] for n in ORDER],
            *[new_m[n] for n in ORDER], *[new_v[n] for n in ORDER])
```

```python
import functools
import math

import numpy as np
import jax
import jax.numpy as jnp
from jax import lax
from jax.experimental import pallas as pl
from jax.experimental.pallas import tpu as pltpu

F32, BF16 = jnp.float32, jnp.bfloat16
HEAD = 128
NSLAB = 16
BLK = 128
DILATIONS = (1, 4, 16)
NUM_BUCKETS, MAX_DISTANCE = 32, 2048
RMS_EPS = 1e-6
NEG = -1e30
SCALE = HEAD ** -0.5
LR, B1, B2, EPS, WD, STEP = 0.001, 0.9, 0.999, 1e-08, 0.01, 10
N_DEV = 8
VMEM_LIMIT_BYTES = 56 << 20
MESH = pl.DeviceIdType.MESH


def _cparams(**kw):
    return pltpu.CompilerParams(vmem_limit_bytes=VMEM_LIMIT_BYTES, **kw)


def _nn(a, b):
    return jnp.dot(a, b, preferred_element_type=F32)


def _nt(a, b):
    return lax.dot_general(a, b, (((1,), (1,)), ((), ())), preferred_element_type=F32)


def _tn(a, b):
    return lax.dot_general(a, b, (((0,), (0,)), ((), ())), preferred_element_type=F32)


def _sds(shape, dtype):
    return jax.ShapeDtypeStruct(shape, dtype)


ANY = pl.BlockSpec(memory_space=pl.ANY)


class _Side:
    def __init__(self, ins, outs, sems, start, finish):
        self.ins, self.outs, self.sems, self.start, self.finish = list(ins), list(outs), list(sems), start, finish


def _call(body, *, name, grid, in_specs, out_specs, out_shape, args, scratch_shapes=(), side=None):
    in_specs, out_specs, out_shape = list(in_specs), list(out_specs), list(out_shape)
    scratch_shapes = list(scratch_shapes)
    if side is None:
        return pl.pallas_call(body, name=name, grid=grid, in_specs=in_specs, out_specs=out_specs, out_shape=out_shape,
                              scratch_shapes=scratch_shapes, compiler_params=_cparams())(*args)
    ni, no, ns = len(args), len(out_shape), len(scratch_shapes)
    si, so = len(side.ins), len(side.outs)

    def fused(*refs):
        h_in, s_in = refs[:ni], refs[ni:ni + si]
        h_out, s_out = refs[ni + si:ni + si + no], refs[ni + si + no:ni + si + no + so]
        h_scr, s_sem = refs[ni + si + no + so:ni + si + no + so + ns], refs[ni + si + no + so + ns:]
        ids = [pl.program_id(k) for k in range(len(grid))]
        first = functools.reduce(jnp.logical_and, [i == 0 for i in ids])
        last = functools.reduce(jnp.logical_and, [i == n - 1 for i, n in zip(ids, grid)])

        @pl.when(first)
        def _():
            side.start(s_in, s_out, s_sem)

        body(*h_in, *h_out, *h_scr)

        @pl.when(last)
        def _():
            side.finish(s_in, s_out, s_sem)

    res = pl.pallas_call(
        fused, name=name, grid=grid, in_specs=in_specs + [ANY] * si, out_specs=out_specs + [ANY] * so,
        out_shape=out_shape + side.outs, scratch_shapes=scratch_shapes + side.sems, compiler_params=_cparams(),
    )(*args, *side.ins)
    return list(res[:no]), list(res[no:])


def _ffn_fwd(x, g, win, wout, tm, name, side=None):
    T, D = x.shape
    nc, tf = win.shape[1], win.shape[2]
    down = wout is not None

    def body(x_ref, g_ref, win_ref, *refs):
        h_ref, gu_ref, act_ref = refs[-3:]
        j = pl.program_id(1)

        @pl.when(j == 0)
        def _():
            xv = x_ref[...]
            r = lax.rsqrt(jnp.mean(xv * xv, axis=-1, keepdims=True) + RMS_EPS)
            h_ref[...] = (xv * r * g_ref[...]).astype(BF16)
            if down:
                refs[1][...] = jnp.zeros_like(refs[1])

        hb = h_ref[...]
        gt = _nt(win_ref[0], hb)
        up = _nt(win_ref[1], hb)
        gu_ref[0] = gt.astype(BF16)
        gu_ref[1] = up.astype(BF16)
        act = (gt * jax.nn.sigmoid(gt) * up).astype(BF16)
        act_ref[...] = act
        if down:
            wout_ref, y_ref = refs[0], refs[1]
            y_ref[...] += _tn(act, wout_ref[...])

            @pl.when(j == nc - 1)
            def _():
                y_ref[...] = x_ref[...] + 0.5 * y_ref[...]

    row = pl.BlockSpec((tm, D), lambda i, j: (i, 0))
    return _call(
        body, name=name, grid=(T // tm, nc), side=side,
        in_specs=[row, pl.BlockSpec((1, D), lambda i, j: (0, 0)),
                  pl.BlockSpec((2, None, tf, D), lambda i, j: (0, j, 0, 0))]
        + ([pl.BlockSpec((None, tf, D), lambda i, j: (j, 0, 0))] if down else []),
        out_specs=([row] if down else [])
        + [row, pl.BlockSpec((2, None, tf, tm), lambda i, j: (0, j, 0, i)),
           pl.BlockSpec((None, tf, tm), lambda i, j: (j, 0, i))],
        out_shape=([_sds((T, D), F32)] if down else [])
        + [_sds((T, D), BF16), _sds((2, nc, tf, T), BF16), _sds((nc, tf, T), BF16)],
        args=(x, g, win) + ((wout,) if down else ()))


def _ffn_down(x, act, wout, tm, name, side=None):
    T, D = x.shape
    nc, tf = wout.shape[0], wout.shape[1]

    def body(x_ref, act_ref, wout_ref, y_ref):
        j = pl.program_id(1)

        @pl.when(j == 0)
        def _():
            y_ref[...] = jnp.zeros_like(y_ref)

        y_ref[...] += _tn(act_ref[...], wout_ref[...])

        @pl.when(j == nc - 1)
        def _():
            y_ref[...] = x_ref[...] + 0.5 * y_ref[...]

    row = pl.BlockSpec((tm, D), lambda i, j: (i, 0))
    res = _call(
        body, name=name, grid=(T // tm, nc), side=side,
        in_specs=[row, pl.BlockSpec((None, tf, tm), lambda i, j: (j, 0, i)),
                  pl.BlockSpec((None, tf, D), lambda i, j: (j, 0, 0))],
        out_specs=[row], out_shape=[_sds((T, D), F32)], args=(x, act, wout))
    return res[0] if side is None else (res[0][0], res[1])


def _ffn_bwd(dy, gu, win, wout, tm, name, side=None):
    T, D = dy.shape
    nc, tf = wout.shape[0], wout.shape[1]

    def body(dy_ref, gu_ref, win_ref, wout_ref, dh_ref, dgu_ref, dyb_ref):
        j = pl.program_id(1)

        @pl.when(j == 0)
        def _():
            dh_ref[...] = jnp.zeros_like(dh_ref)
            dyb_ref[...] = (0.5 * dy_ref[...]).astype(BF16)

        dact = _nt(wout_ref[...], dyb_ref[...])
        gt = gu_ref[0].astype(F32)
        up = gu_ref[1].astype(F32)
        s = jax.nn.sigmoid(gt)
        dgb = (dact * up * (s * (1.0 + gt * (1.0 - s)))).astype(BF16)
        dub = (dact * (gt * s)).astype(BF16)
        dgu_ref[0] = dgb
        dgu_ref[1] = dub
        dh_ref[...] += _tn(dgb, win_ref[0]) + _tn(dub, win_ref[1])

    return _call(
        body, name=name, grid=(T // tm, nc), side=side,
        in_specs=[pl.BlockSpec((tm, D), lambda i, j: (i, 0)),
                  pl.BlockSpec((2, None, tf, tm), lambda i, j: (0, j, 0, i)),
                  pl.BlockSpec((2, None, tf, D), lambda i, j: (0, j, 0, 0)),
                  pl.BlockSpec((None, tf, D), lambda i, j: (j, 0, 0))],
        out_specs=[pl.BlockSpec((tm, D), lambda i, j: (i, 0)),
                   pl.BlockSpec((2, None, tf, tm), lambda i, j: (0, j, 0, i)),
                   pl.BlockSpec((tm, D), lambda i, j: (i, 0))],
        out_shape=[_sds((T, D), F32), _sds((2, nc, tf, T), BF16), _sds((T, D), BF16)],
        args=(dy, gu, win, wout))


def _rms_bwd(x, g, dh, dres, tm, name, side=None):
    T, D = x.shape

    def body(x_ref, g_ref, dh_ref, dres_ref, dx_ref, dg_ref):
        @pl.when(pl.program_id(0) == 0)
        def _():
            dg_ref[...] = jnp.zeros_like(dg_ref)

        xv = x_ref[...]
        r = lax.rsqrt(jnp.mean(xv * xv, axis=-1, keepdims=True) + RMS_EPS)
        xhat = xv * r
        dh = dh_ref[...]
        gd = dh * g_ref[...]
        dx_ref[...] = dres_ref[...] + r * (gd - xhat * jnp.mean(gd * xhat, axis=-1, keepdims=True))
        dg_ref[...] += jnp.sum(dh * xhat, axis=0, keepdims=True)

    row = pl.BlockSpec((tm, D), lambda i: (i, 0))
    one = pl.BlockSpec((1, D), lambda i: (0, 0))
    return _call(body, name=name, grid=(T // tm,), side=side, in_specs=[row, one, row, row], out_specs=[row, one],
                 out_shape=[_sds((T, D), F32), _sds((1, D), F32)], args=(x, g, dh, dres))


def _mm_tn(a, b, *, bm, bn, bt, name, b_slabs=False, side=None, rows=None, m_off=0, into=None, a_rows=False,
           twin=False):
    nz, T, M = (a.shape[0], a.shape[2], a.shape[1]) if a_rows else a.shape
    if b_slabs:
        N = b.shape[1] // NSLAB
        assert bt == T // NSLAB
        b_spec = pl.BlockSpec((bt, bn), lambda n, z, m, t: (0, t * (N // bn) + n))
    else:
        N = b.shape[1]
        b_spec = pl.BlockSpec((bt, bn), lambda n, z, m, t: (t, n))
    assert M % bm == 0 and N % bn == 0 and T % bt == 0, (M, bm, N, bn, T, bt)

    def body(a_ref, b_ref, *rest):
        c_ref = rest[-2] if twin else rest[-1]

        @pl.when(pl.program_id(3) == 0)
        def _():
            c_ref[...] = jnp.zeros_like(c_ref)

        ab, bb = a_ref[...].astype(BF16), b_ref[...].astype(BF16)
        c_ref[...] += _nn(ab, bb) if a_rows else _tn(ab, bb)
        if twin:
            @pl.when(pl.program_id(3) == T // bt - 1)
            def _():
                rest[-1][...] = c_ref[...].astype(BF16)

    grid = (N // bn, nz, M // bm, T // bt)
    a_spec = (pl.BlockSpec((None, bm, bt), lambda n, z, m, t: (z, m, t)) if a_rows
              else pl.BlockSpec((None, bt, bm), lambda n, z, m, t: (z, t, m)))
    in_specs = [a_spec, b_spec]
    out_spec = pl.BlockSpec((None, bm, bn), lambda n, z, m, t: (z, m + m_off, n))
    out_shape = _sds((nz, M if rows is None else rows, N), F32)
    if into is not None:
        assert side is None and not twin and into.shape == out_shape.shape
        return pl.pallas_call(body, name=name, grid=grid, in_specs=in_specs + [ANY], out_specs=out_spec,
                              out_shape=out_shape, input_output_aliases={2: 0}, compiler_params=_cparams())(a, b, into)
    outs = [out_shape] + ([_sds(out_shape.shape, BF16)] if twin else [])
    res = _call(body, name=name, grid=grid, side=side, in_specs=in_specs, out_specs=[out_spec] * len(outs),
                out_shape=outs, args=(a, b))
    mine = res if side is None else res[0]
    mine = tuple(mine) if twin else mine[0]
    return mine if side is None else (mine, res[1])


def _tok_spec(layout, tm, n16, C, bc, colmap):
    if layout == "nat":
        return pl.BlockSpec((tm, bc), lambda i, k: (i, colmap(k)))
    assert tm % n16 == 0
    if layout == "slab":
        return pl.BlockSpec((tm // n16, n16, bc), lambda i, k: (i, 0, colmap(k)))
    assert bc == C
    return pl.BlockSpec((n16, (tm // n16) * C), lambda i, k: (0, i))


def _tok_load(ref, layout, sp):
    if layout == "nat":
        return ref[...]
    if layout == "slab":
        return ref[...].reshape(-1, ref.shape[-1])
    c = ref.shape[1] // sp
    return jnp.concatenate([ref[:, s * c:(s + 1) * c] for s in range(sp)], axis=0)


def _tok_store(ref, layout, sp, val, cols=None, accumulate=False):
    def put(idx, v):
        if accumulate:
            ref[idx] += v
        else:
            ref[idx] = v

    lanes = slice(None) if cols is None else slice(cols[0], cols[0] + cols[1])
    if layout == "nat":
        put((slice(None), lanes), val)
    elif layout == "slab":
        put((slice(None), slice(None), lanes), val.reshape(sp, ref.shape[1], val.shape[-1]))
    else:
        assert cols is None
        c, n = ref.shape[1] // sp, ref.shape[0]
        for s in range(sp):
            put((slice(None), slice(s * c, (s + 1) * c)), val[s * n:(s + 1) * n])


def _proj(x, g, wt, gains, modes, *, tn, w_off, slabs, tm, normed_dtype, name):
    T, D = x.shape
    ntile = len(modes)
    N = ntile * tn
    n16 = T // NSLAB
    in_layout, out_layout = ("view", "slab") if slabs else ("nat", "nat")
    sp = tm // n16
    x_in = x.reshape(n16, NSLAB * D) if slabs else x
    x_spec = _tok_spec(in_layout, tm, n16, D, D, lambda n: 0)
    oshape = lambda c: (NSLAB, n16, c) if slabs else (T, c)
    ospec = lambda bc, cm: _tok_spec(out_layout, tm, n16, None, bc, cm)

    def body(x_ref, g_ref, w_ref, gains_ref, raw_ref, nrm_ref, h_ref):
        n = pl.program_id(1)

        @pl.when(n == 0)
        def _():
            xv = _tok_load(x_ref, in_layout, sp)
            r = lax.rsqrt(jnp.mean(xv * xv, axis=-1, keepdims=True) + RMS_EPS)
            _tok_store(h_ref, out_layout, sp, (xv * r * g_ref[...]).astype(BF16))

        y = _nt(_tok_load(h_ref, out_layout, sp), w_ref[...])
        _tok_store(raw_ref, out_layout, sp, y)
        for t, mode in enumerate(modes):
            @pl.when(n == t)
            def _(t=t, mode=mode):
                if not mode:
                    _tok_store(nrm_ref, out_layout, sp, y.astype(nrm_ref.dtype))
                    return
                gain = gains_ref[t]
                for k in range(tn // HEAD):
                    yk = y[:, k * HEAD:(k + 1) * HEAD]
                    r = lax.rsqrt(jnp.mean(yk * yk, axis=-1, keepdims=True) + RMS_EPS)
                    _tok_store(nrm_ref, out_layout, sp, (yk * r * gain).astype(nrm_ref.dtype), cols=(k * HEAD, HEAD))

    return pl.pallas_call(
        body, name=name, grid=(T // tm, ntile),
        in_specs=[x_spec, pl.BlockSpec((1, D), lambda i, n: (0, 0)),
                  pl.BlockSpec((tn, D), lambda i, n: (n + w_off, 0)),
                  pl.BlockSpec((ntile, 1, HEAD), lambda i, n: (0, 0, 0))],
        out_specs=[ospec(tn, lambda n: n), ospec(tn, lambda n: n), ospec(D, lambda n: 0)],
        out_shape=[_sds(oshape(N), F32), _sds(oshape(N), normed_dtype), _sds(oshape(D), BF16)],
        compiler_params=_cparams(),
    )(x_in, g, wt, gains)


def _mm(a, w, *, nt, tk, tm, a_layout, out_layout, resid=None, name, w_off=0, n_out=None):
    if a_layout == "slab":
        T, K = a.shape[0] * a.shape[1], a.shape[2]
    else:
        T, K = a.shape
    N = (w.shape[0] if nt else w.shape[1]) if n_out is None else n_out
    n16 = T // NSLAB
    nk = K // tk
    sp = tm // n16
    a_in = a.reshape(n16, NSLAB * K) if a_layout == "view" else a
    w_spec = (pl.BlockSpec((N, tk), lambda i, k: (w_off, k)) if nt
              else pl.BlockSpec((tk, N), lambda i, k: (k + w_off, 0)))
    o_spec = _tok_spec(out_layout, tm, n16, N, N, lambda k: 0)
    oshape = {"nat": (T, N), "slab": (NSLAB, n16, N), "view": (n16, NSLAB * N)}[out_layout]
    has_resid = resid is not None

    def body(*refs):
        a_ref, w_ref = refs[0], refs[1]
        o_ref = refs[-1]
        k = pl.program_id(1)

        @pl.when(k == 0)
        def _():
            o_ref[...] = refs[2][...] if has_resid else jnp.zeros_like(o_ref)

        ab = _tok_load(a_ref, a_layout, sp).astype(BF16)
        _tok_store(o_ref, out_layout, sp, _nt(ab, w_ref[...]) if nt else _nn(ab, w_ref[...]), accumulate=True)

    ins = [a_in, w]
    in_specs = [_tok_spec(a_layout, tm, n16, K, tk, lambda k: k), w_spec]
    if has_resid:
        ins.append(resid.reshape(n16, NSLAB * N) if out_layout == "view" else resid)
        in_specs.append(o_spec)
    out = pl.pallas_call(
        body, name=name, grid=(T // tm, nk), in_specs=in_specs, out_specs=o_spec,
        out_shape=_sds(oshape, F32), compiler_params=_cparams(),
    )(*ins)
    return out.reshape(T, N) if out_layout == "view" else out


def _log_sigmoid(z):
    return jnp.minimum(z, 0.0) - jnp.log(1.0 + jnp.exp(-jnp.abs(z)))


def _fox_gate_fwd(f_raw, fbias, name):
    T = f_raw.shape[0]
    cb = 256

    def body(f_ref, b_ref, c_ref):
        row = lax.broadcasted_iota(jnp.int32, (cb, cb), 0)
        col = lax.broadcasted_iota(jnp.int32, (cb, cb), 1)
        tri = (col <= row).astype(F32)
        carry = jnp.zeros((1, HEAD), F32)
        for i in range(T // cb):
            lf = _log_sigmoid(f_ref[i * cb:(i + 1) * cb, :] + b_ref[...])
            c = jnp.dot(tri, lf, preferred_element_type=F32, precision=lax.Precision.HIGHEST) + carry
            c_ref[i * cb:(i + 1) * cb, :] = c
            carry = c[cb - 1:cb, :]

    return pl.pallas_call(body, name=name, out_shape=_sds((T, HEAD), F32), compiler_params=_cparams())(f_raw, fbias)


def _fox_gate_bwd(f_raw, fbias, dc, name):
    T = f_raw.shape[0]
    cb = 256

    def body(f_ref, b_ref, dc_ref, df_ref, db_ref):
        row = lax.broadcasted_iota(jnp.int32, (cb, cb), 0)
        col = lax.broadcasted_iota(jnp.int32, (cb, cb), 1)
        tri = (col >= row).astype(F32)
        carry = jnp.zeros((1, HEAD), F32)
        dbias = jnp.zeros((1, HEAD), F32)
        for i in reversed(range(T // cb)):
            dlf = jnp.dot(tri, dc_ref[i * cb:(i + 1) * cb, :], preferred_element_type=F32,
                          precision=lax.Precision.HIGHEST) + carry
            carry = dlf[0:1, :]
            z = f_ref[i * cb:(i + 1) * cb, :] + b_ref[...]
            df = dlf * jax.nn.sigmoid(-z)
            df_ref[i * cb:(i + 1) * cb, :] = df
            dbias = dbias + jnp.sum(df, axis=0, keepdims=True)
        db_ref[...] = dbias

    return pl.pallas_call(body, name=name, out_shape=[_sds((T, HEAD), F32), _sds((1, HEAD), F32)],
                          compiler_params=_cparams())(f_raw, fbias, dc)


def _fox_fwd(qkv, c_col, c_row, tq, name, side=None):
    T = qkv.shape[0]
    H = qkv.shape[1] // (3 * HEAD)
    nq = T // tq
    c_blocks = c_row.reshape(H, nq, 1, tq)

    def body(q_ref, k_ref, v_ref, cq_ref, ck_ref, o_ref, lse_ref):
        qi = pl.program_id(1)
        q, cq = q_ref[...], cq_ref[...]
        causal = lax.broadcasted_iota(jnp.int32, (tq, tq), 1) <= lax.broadcasted_iota(jnp.int32, (tq, tq), 0)

        def key_block(ki, carry, diagonal):
            m, l, acc = carry
            rows = pl.ds(pl.multiple_of(ki * tq, tq), tq)
            s = _nt(q, k_ref[rows, :]) * SCALE + cq - ck_ref[ki]
            if diagonal:
                s = jnp.where(causal, s, NEG)
            m_new = jnp.maximum(m, jnp.max(s, axis=-1, keepdims=True))
            alpha = jnp.exp(m - m_new)
            p = jnp.exp(s - m_new)
            l = alpha * l + jnp.sum(p, axis=-1, keepdims=True)
            acc = alpha * acc + _nn(p.astype(BF16), v_ref[rows, :])
            return m_new, l, acc

        init = (jnp.full((tq, 1), NEG, F32), jnp.zeros((tq, 1), F32), jnp.zeros((tq, HEAD), F32))
        carry = lax.fori_loop(0, qi, lambda ki, c: key_block(ki, c, False), init)
        m, l, acc = key_block(qi, carry, True)
        o_ref[...] = acc / l
        lse_ref[...] = m + jnp.log(l)

    return _call(
        body, name=name, grid=(H, nq), side=side,
        in_specs=[pl.BlockSpec((tq, HEAD), lambda h, qi: (qi, h)),
                  pl.BlockSpec((T, HEAD), lambda h, qi: (0, H + h)),
                  pl.BlockSpec((T, HEAD), lambda h, qi: (0, 2 * H + h)),
                  pl.BlockSpec((None, tq, 1), lambda h, qi: (h, qi, 0)),
                  pl.BlockSpec((None, nq, 1, tq), lambda h, qi: (h, 0, 0, 0))],
        out_specs=[pl.BlockSpec((tq, HEAD), lambda h, qi: (qi, h)),
                   pl.BlockSpec((None, tq, 1), lambda h, qi: (h, qi, 0))],
        out_shape=[_sds((T, H * HEAD), F32), _sds((H, T, 1), F32)],
        args=(qkv, qkv, qkv, c_col, c_blocks))


def _fox_bwd(qkv, c_col, c_row, out, dout, lse, tq, name, side=None):
    T = qkv.shape[0]
    H = qkv.shape[1] // (3 * HEAD)
    nq = T // tq

    def body(q_ref, k_ref, v_ref, cq_ref, ck_ref, o_ref, do_ref, lse_ref, dq_ref, dk_ref, dv_ref, dck_ref, dcq_ref,
             delta_s):
        ki = pl.program_id(1)

        @pl.when(ki == 0)
        def _():
            dq_ref[...] = jnp.zeros_like(dq_ref)
            dcq_ref[...] = jnp.zeros_like(dcq_ref)
            delta_s[...] = jnp.sum(do_ref[...] * o_ref[...], axis=-1, keepdims=True)

        k, v, ck = k_ref[...], v_ref[...], ck_ref[...]
        causal = lax.broadcasted_iota(jnp.int32, (tq, tq), 1) <= lax.broadcasted_iota(jnp.int32, (tq, tq), 0)

        def query_block(qi, carry, diagonal):
            dk, dv, dck = carry
            rows = pl.ds(pl.multiple_of(qi * tq, tq), tq)
            q = q_ref[rows, :]
            s = _nt(q, k) * SCALE + cq_ref[rows, :] - ck
            if diagonal:
                s = jnp.where(causal, s, NEG)
            p = jnp.exp(s - lse_ref[rows, :])
            dob = do_ref[rows, :].astype(BF16)
            ds = p * (_nt(dob, v) - delta_s[rows, :])
            dsb = ds.astype(BF16)
            dq_ref[rows, :] += _nn(dsb, k) * SCALE
            dcq_ref[rows, :] += jnp.sum(ds, axis=-1, keepdims=True)
            return dk + _tn(dsb, q), dv + _tn(p.astype(BF16), dob), dck - jnp.sum(ds, axis=0, keepdims=True)

        init = (jnp.zeros((tq, HEAD), F32), jnp.zeros((tq, HEAD), F32), jnp.zeros((1, tq), F32))
        carry = query_block(ki, init, True)
        dk, dv, dck = lax.fori_loop(ki + 1, nq, lambda qi, c: query_block(qi, c, False), carry)
        dk_ref[...] = dk * SCALE
        dv_ref[...] = dv
        dck_ref[...] = dck

    head = lambda off: pl.BlockSpec((T, HEAD), lambda h, ki: (0, off + h))
    col = pl.BlockSpec((None, T, 1), lambda h, ki: (h, 0, 0))
    return _call(
        body, name=name, grid=(H, nq), side=side,
        in_specs=[head(0),
                  pl.BlockSpec((tq, HEAD), lambda h, ki: (ki, H + h)),
                  pl.BlockSpec((tq, HEAD), lambda h, ki: (ki, 2 * H + h)),
                  col, pl.BlockSpec((None, 1, tq), lambda h, ki: (h, 0, ki)), head(0), head(0), col],
        out_specs=[head(0),
                   pl.BlockSpec((tq, HEAD), lambda h, ki: (ki, h)),
                   pl.BlockSpec((tq, HEAD), lambda h, ki: (ki, h)),
                   pl.BlockSpec((None, 1, tq), lambda h, ki: (h, 0, ki)), col],
        out_shape=[_sds((T, H * HEAD), F32), _sds((T, H * HEAD), F32), _sds((T, H * HEAD), F32), _sds((H, 1, T), F32),
                   _sds((H, T, 1), F32)],
        scratch_shapes=[pltpu.VMEM((T, 1), F32)],
        args=(qkv, qkv, qkv, c_col, c_row, out, dout, lse))


def _t5_bucket(dist):
    max_exact = NUM_BUCKETS // 2
    d = dist.astype(np.float32)
    large = max_exact + (np.log(np.maximum(d, np.float32(1.0)) / np.float32(max_exact))
                         / np.float32(math.log(MAX_DISTANCE / max_exact))
                         * np.float32(NUM_BUCKETS - max_exact)).astype(np.int32)
    large = np.minimum(large, NUM_BUCKETS - 1)
    return np.where(dist < max_exact, dist, large)


def _bucket_maps():
    maps = []
    for d in DILATIONS:
        e = NSLAB // d
        rows = BLK // e
        idx = np.arange(BLK)
        pos = e * (idx % rows) + idx // rows
        qpos = pos[:, None] + BLK
        kpos = np.concatenate([pos, pos + BLK])[None, :]
        delta = qpos - kpos
        band = (delta >= 0) & (delta <= BLK)
        bucket = _t5_bucket(np.clip(delta, 0, None) * d)
        maps.append(np.where(band, bucket, -1).astype(np.int32))
    return np.stack(maps)


def _dil_geometry(T):
    n16 = T // NSLAB
    geo = []
    for d in DILATIONS:
        e = NSLAB // d
        rows = BLK // e
        nblk = n16 // rows
        geo.append((d, e, rows, nblk))
    return geo


DIL_INTERLEAVE_FWD = {1: 4, 4: 8, 16: 8}
DIL_INTERLEAVE_BWD = {1: 8, 4: 8, 16: 8}


def _dil_interleave(per_step, nblocks):
    while per_step > 1 and (nblocks % per_step or nblocks // per_step < 2):
        per_step -= 1
    return per_step


def _dil_bias(tab_ref, bkt_ref, bias_s, h):
    for p in range(len(DILATIONS)):
        bk = bkt_ref[p]
        bias = jnp.full((BLK, 2 * BLK), NEG, F32)
        for b in range(NUM_BUCKETS):
            bias = jnp.where(bk == b, tab_ref[b, h], bias)
        bias_s[p] = bias


def _dil_rows(d, e, rows, sub, blk):
    start = pl.multiple_of(blk * rows, rows)
    return [(sub + d * j, pl.ds(start, rows)) for j in range(e)]


def _gather(ref, idx):
    return jnp.concatenate([ref[s, r, :] for s, r in idx], axis=0)


def _scatter(ref, idx, val, rows):
    for j, (s, r) in enumerate(idx):
        ref[s, r, :] = val[j * rows:(j + 1) * rows]


def _scatter_add(ref, idx, val, rows):
    for j, (s, r) in enumerate(idx):
        ref[s, r, :] += val[j * rows:(j + 1) * rows]


def _dil_fwd(qkv, table, name, side=None):
    n16 = qkv.shape[1]
    T = NSLAB * n16
    H = qkv.shape[2] // (3 * HEAD)
    geo = _dil_geometry(T)
    bkt = jnp.asarray(_bucket_maps())

    def body(tab_ref, bkt_ref, q_ref, k_ref, v_ref, o_ref, lse_ref, bias_s, m_s, l_s):
        h = pl.program_id(0)
        _dil_bias(tab_ref, bkt_ref, bias_s, h)
        first_mask = lax.broadcasted_iota(jnp.int32, (BLK, 2 * BLK), 1) < BLK

        starts = len(DILATIONS) - 1

        def load(p, d, e, rows, sub, blk):
            cur = _dil_rows(d, e, rows, sub, blk)
            prev = _dil_rows(d, e, rows, sub, jnp.maximum(blk - 1, 0))
            q = _gather(q_ref, cur).astype(BF16)
            kk = jnp.concatenate([_gather(k_ref, prev), _gather(k_ref, cur)], axis=0).astype(BF16)
            vv = jnp.concatenate([_gather(v_ref, prev), _gather(v_ref, cur)], axis=0).astype(BF16)
            old = None if p == starts else (_gather(m_s, cur), _gather(l_s, cur), _gather(o_ref, cur))
            return cur, blk, q, kk, vv, old

        def compute(p, blk, q, kk, vv, old):
            s = _nt(q, kk) * SCALE + bias_s[p]
            s = jnp.where(first_mask & (blk == 0), NEG, s)
            m_blk = jnp.max(s, axis=-1, keepdims=True)
            if old is None:
                m_new = m_blk
                pr = jnp.exp(s - m_new)
                l_new = jnp.sum(pr, axis=-1, keepdims=True)
                acc = _nn(pr.astype(BF16), vv)
            else:
                m_old, l_old, acc_old = old
                m_new = jnp.maximum(m_old, m_blk)
                alpha = jnp.exp(m_old - m_new)
                pr = jnp.exp(s - m_new)
                l_new = alpha * l_old + jnp.sum(pr, axis=-1, keepdims=True)
                acc = alpha * acc_old + _nn(pr.astype(BF16), vv)
            if p == 0:
                return acc / l_new, m_new + jnp.log(l_new), None
            return acc, m_new, l_new

        def store(p, rows, cur, acc, m_new, l_new):
            _scatter(o_ref, cur, acc, rows)
            if p == 0:
                _scatter(lse_ref, cur, m_new, rows)
            else:
                _scatter(m_s, cur, m_new, rows)
                _scatter(l_s, cur, l_new, rows)

        for p in reversed(range(len(DILATIONS))):
            d, e, rows, nblk = geo[p]
            per_step = _dil_interleave(DIL_INTERLEAVE_FWD[d], d * nblk)

            def step(i, carry, p=p, d=d, e=e, rows=rows, nblk=nblk, per_step=per_step):
                ids = [i + u * (d * nblk // per_step) for u in range(per_step)]
                loaded = [load(p, d, e, rows, j // nblk, j % nblk) for j in ids]
                done = [(cur, compute(p, blk, q, kk, vv, old)) for cur, blk, q, kk, vv, old in loaded]
                for cur, res in done:
                    store(p, rows, cur, *res)
                return carry

            lax.fori_loop(0, d * nblk // per_step, step, 0)

    head = lambda off: pl.BlockSpec((NSLAB, n16, HEAD), lambda h: (0, 0, off + h))
    return _call(
        body, name=name, grid=(H,), side=side,
        in_specs=[pl.BlockSpec(memory_space=pltpu.SMEM), pl.BlockSpec((3, BLK, 2 * BLK), lambda h: (0, 0, 0)),
                  head(0), head(H), head(2 * H)],
        out_specs=[head(0), pl.BlockSpec((None, NSLAB, n16, 1), lambda h: (h, 0, 0, 0))],
        out_shape=[_sds((NSLAB, n16, H * HEAD), F32), _sds((H, NSLAB, n16, 1), F32)],
        scratch_shapes=[pltpu.VMEM((3, BLK, 2 * BLK), F32), pltpu.VMEM((NSLAB, n16, 1), F32),
                        pltpu.VMEM((NSLAB, n16, 1), F32)],
        args=(table, bkt, qkv, qkv, qkv))


def _dil_bwd(qkv, table, out, dout, lse, name, side=None):
    n16 = qkv.shape[1]
    T = NSLAB * n16
    H = qkv.shape[2] // (3 * HEAD)
    geo = _dil_geometry(T)
    bkt = jnp.asarray(_bucket_maps())

    def body(tab_ref, bkt_ref, q_ref, k_ref, v_ref, o_ref, do_ref, lse_ref,
             dq_ref, dk_ref, dv_ref, dtab_ref, bias_s, dbias_s, delta_s):
        h = pl.program_id(0)
        _dil_bias(tab_ref, bkt_ref, bias_s, h)
        first_mask = lax.broadcasted_iota(jnp.int32, (BLK, 2 * BLK), 1) < BLK
        dbias_s[...] = jnp.zeros_like(dbias_s)
        dq_ref[...] = jnp.zeros_like(dq_ref)
        dk_ref[...] = jnp.zeros_like(dk_ref)
        dv_ref[...] = jnp.zeros_like(dv_ref)
        for r in range(NSLAB):
            delta_s[r] = jnp.sum(do_ref[r] * o_ref[r], axis=-1, keepdims=True)

        def load(d, e, rows, sub, blk):
            cur = _dil_rows(d, e, rows, sub, blk)
            prev = _dil_rows(d, e, rows, sub, jnp.maximum(blk - 1, 0))
            q = _gather(q_ref, cur).astype(BF16)
            kk = jnp.concatenate([_gather(k_ref, prev), _gather(k_ref, cur)], axis=0).astype(BF16)
            vv = jnp.concatenate([_gather(v_ref, prev), _gather(v_ref, cur)], axis=0).astype(BF16)
            dob = _gather(do_ref, cur).astype(BF16)
            return cur, prev, blk, q, kk, vv, dob, _gather(lse_ref, cur), _gather(delta_s, cur)

        def compute(p, blk, q, kk, vv, dob, lse, delta):
            s = _nt(q, kk) * SCALE + bias_s[p]
            s = jnp.where(first_mask & (blk == 0), NEG, s)
            pr = jnp.exp(s - lse)
            ds = pr * (_nt(dob, vv) - delta)
            dsb = ds.astype(BF16)
            return ds, _nn(dsb, kk) * SCALE, _tn(dsb, q) * SCALE, _tn(pr.astype(BF16), dob)

        def store(rows, cur, prev, dq, dkk, dvv):
            _scatter_add(dq_ref, cur, dq, rows)
            _scatter_add(dk_ref, prev, dkk[:BLK], rows)
            _scatter_add(dk_ref, cur, dkk[BLK:], rows)
            _scatter_add(dv_ref, prev, dvv[:BLK], rows)
            _scatter_add(dv_ref, cur, dvv[BLK:], rows)

        for p in range(len(DILATIONS)):
            d, e, rows, nblk = geo[p]
            per_step = _dil_interleave(DIL_INTERLEAVE_BWD[d], d * nblk)

            def step(i, carry, p=p, d=d, e=e, rows=rows, nblk=nblk, per_step=per_step):
                ids = [i + u * (d * nblk // per_step) for u in range(per_step)]
                loaded = [load(d, e, rows, j // nblk, j % nblk) for j in ids]
                done = [(cur, prev, compute(p, *rest)) for cur, prev, *rest in loaded]
                dbias_s[p] += functools.reduce(jnp.add, [res[0] for _, _, res in done])
                for cur, prev, res in done:
                    store(rows, cur, prev, *res[1:])
                return carry

            lax.fori_loop(0, d * nblk // per_step, step, 0)

        lane = lax.broadcasted_iota(jnp.int32, (1, HEAD), 1)
        row = jnp.zeros((1, HEAD), F32)
        for b in range(NUM_BUCKETS):
            tot = jnp.zeros((1, 1), F32)
            for p in range(len(DILATIONS)):
                hit = jnp.where(bkt_ref[p] == b, dbias_s[p], 0.0)
                tot = tot + jnp.sum(jnp.sum(hit, axis=0, keepdims=True), axis=1, keepdims=True)
            row = jnp.where(lane == b, tot, row)
        dtab_ref[...] = row

    head = lambda off: pl.BlockSpec((NSLAB, n16, HEAD), lambda h: (0, 0, off + h))
    return _call(
        body, name=name, grid=(H,), side=side,
        in_specs=[pl.BlockSpec(memory_space=pltpu.SMEM), pl.BlockSpec((3, BLK, 2 * BLK), lambda h: (0, 0, 0)),
                  head(0), head(H), head(2 * H), head(0), head(0),
                  pl.BlockSpec((None, NSLAB, n16, 1), lambda h: (h, 0, 0, 0))],
        out_specs=[head(0), head(0), head(0), pl.BlockSpec((None, 1, HEAD), lambda h: (h, 0, 0))],
        out_shape=[_sds((NSLAB, n16, H * HEAD), F32)] * 3 + [_sds((H, 1, HEAD), F32)],
        scratch_shapes=[pltpu.VMEM((3, BLK, 2 * BLK), F32), pltpu.VMEM((3, BLK, 2 * BLK), F32),
                        pltpu.VMEM((NSLAB, n16, 1), F32)],
        args=(table, bkt, qkv, qkv, qkv, out, dout, lse))


def _qknorm_bwd(raw, dq, dk, dv, gains, tm, name):
    T, N = raw.shape
    C = N // 3

    def body(raw_ref, dq_ref, dk_ref, dv_ref, gains_ref, dp_ref, dg_ref):
        @pl.when(pl.program_id(0) == 0)
        def _():
            dg_ref[...] = jnp.zeros_like(dg_ref)

        for t, d_ref in enumerate((dq_ref, dk_ref)):
            gain = gains_ref[t]
            dgain = jnp.zeros((1, HEAD), F32)
            for k in range(C // HEAD):
                y = raw_ref[:, t * C + k * HEAD:t * C + (k + 1) * HEAD]
                dn = d_ref[:, k * HEAD:(k + 1) * HEAD]
                r = lax.rsqrt(jnp.mean(y * y, axis=-1, keepdims=True) + RMS_EPS)
                yhat = y * r
                gd = dn * gain
                dy = r * (gd - yhat * jnp.mean(gd * yhat, axis=-1, keepdims=True))
                dp_ref[:, t * C + k * HEAD:t * C + (k + 1) * HEAD] = dy.astype(BF16)
                dgain = dgain + jnp.sum(dn * yhat, axis=0, keepdims=True)
            dg_ref[t] += dgain
        dp_ref[:, 2 * C:] = dv_ref[...].astype(BF16)

    third = pl.BlockSpec((tm, C), lambda i: (i, 0))
    return pl.pallas_call(
        body, name=name, grid=(T // tm,),
        in_specs=[pl.BlockSpec((tm, N), lambda i: (i, 0)), third, third, third,
                  pl.BlockSpec((2, 1, HEAD), lambda i: (0, 0, 0))],
        out_specs=[pl.BlockSpec((tm, N), lambda i: (i, 0)), pl.BlockSpec((2, 1, HEAD), lambda i: (0, 0, 0))],
        out_shape=[_sds((T, N), BF16), _sds((2, 1, HEAD), F32)], compiler_params=_cparams(),
    )(raw, dq, dk, dv, gains)


def _loss_grad(y, target, tm, name):
    T, D = y.shape

    def body(y_ref, t_ref, dy_ref, loss_ref):
        @pl.when(pl.program_id(0) == 0)
        def _():
            loss_ref[...] = jnp.zeros_like(loss_ref)

        err = y_ref[...] - t_ref[...]
        dy_ref[...] = err * (1.0 / D)
        per_tok = jnp.mean(err * err, axis=-1, keepdims=True)
        tot = 0.5 * jnp.sum(per_tok, axis=0, keepdims=True)
        lane = lax.broadcasted_iota(jnp.int32, (1, HEAD), 1)
        loss_ref[...] += jnp.where(lane == 0, tot, 0.0)

    row = pl.BlockSpec((tm, D), lambda i: (i, 0))
    return pl.pallas_call(
        body, name=name, grid=(T // tm,), in_specs=[row, row],
        out_specs=[row, pl.BlockSpec((1, HEAD), lambda i: (0, 0))],
        out_shape=[_sds((T, D), F32), _sds((1, HEAD), F32)], compiler_params=_cparams(),
    )(y, target)


def _pad_lanes(v, width=HEAD):
    return jnp.pad(v, ((0, 0), (0, width - v.shape[1])))


def _local_step(x, target, small, wts, plan=None):
    grads = {}

    def hosted(host, fn, *args, **kw):
        side = plan.before(host, wts, grads) if plan is not None else None
        if side is None:
            return fn(*args, name=host, **kw)
        res, side_res = fn(*args, name=host, side=side, **kw)
        plan.after(host, side_res, wts, grads)
        return res

    T, D = x.shape
    C = D // 2
    H = C // HEAD
    n16 = T // NSLAB
    tm = min(512, T)
    tmm = min(1024, T)
    tms = 4 * n16
    tq = min(512, T)
    bn = min(1024, D)
    g1, gm, g2 = small["ffn1_norm"], small["mix_norm"], small["ffn2_norm"]
    gains_a = jnp.stack([small["q_norm_a"], small["k_norm_a"], jnp.ones_like(small["q_norm_a"])])
    gains_b = jnp.stack([small["q_norm_b"], small["k_norm_b"], jnp.ones_like(small["q_norm_b"])])
    fbias = _pad_lanes(small["forget_bias"])
    table = small["rel_bias_table"]

    h1, gu1, act1 = hosted("ffn1_up", _ffn_fwd, x, g1, wts["ffn1_in"], None, tm)
    x1 = hosted("ffn1_down", _ffn_down, x, act1, wts["ffn1_out"], tm)
    w_in_t, w_f_t, w_o = wts["w_in_t"], wts["w_f_t"], wts["w_o"]
    raw_a, nrm_a, h2a = _proj(x1, gm, w_in_t, gains_a, (True, True, False), tn=C, w_off=0, slabs=True, tm=tms,
                              normed_dtype=F32, name="proj_a")
    raw_b, nrm_b, h2b = _proj(x1, gm, w_in_t, gains_b, (True, True, False), tn=C, w_off=3, slabs=False, tm=tmm,
                              normed_dtype=BF16, name="proj_b")
    f_raw, _, _ = _proj(x1, gm, w_f_t, gains_b[:1], (False,), tn=HEAD, w_off=0, slabs=False, tm=tm,
                        normed_dtype=BF16, name="proj_f")
    c = _fox_gate_fwd(f_raw, fbias, "fox_gate_fwd")
    c_heads = c[:, :H].T
    c_col, c_row = c_heads[:, :, None], c_heads[:, None, :]
    out_a, lse_a = hosted("dil_fwd", _dil_fwd, nrm_a, table)
    out_b, lse_b = hosted("fox_fwd", _fox_fwd, nrm_b, c_col, c_row, tq)
    x2a = _mm(out_a, w_o, nt=False, tk=C, tm=tms, a_layout="slab", out_layout="view", resid=x1, name="out_a")
    x2 = _mm(out_b, w_o, nt=False, tk=C, tm=tmm, a_layout="nat", out_layout="nat", resid=x2a, w_off=1, name="out_b")
    y, h3, gu3, act3 = _ffn_fwd(x2, g2, wts["ffn2_in"], wts["ffn2_out"], tm, "ffn2_fwd")
    dy, loss_row = _loss_grad(y, target, tm, "loss_grad")

    def ffn_backward(tag, xin, g, h, gu, act, win, wout, dres):
        nc, tf = wout.shape[0], wout.shape[1]
        dh, dgu, dyb = hosted(tag + "_bwd", _ffn_bwd, dres, gu, win, wout, tm)
        grads[tag + "_w_in_t"], grads[tag + "_w_in_t_bf16"] = hosted(
            tag + "_dwin", _mm_tn, dgu.reshape(2 * nc, tf, T), h, bm=tf, bn=bn, bt=T, a_rows=True, twin=True)
        dxin, grads[tag + "_norm"] = hosted(tag + "_rms_bwd", _rms_bwd, xin, g, dh, dres, tm)
        grads[tag + "_w_out"], grads[tag + "_w_out_bf16"] = hosted(
            tag + "_dwout", _mm_tn, act, dyb, bm=tf, bn=bn, bt=T, a_rows=True, twin=True)
        return dxin

    dx2 = ffn_backward("ffn2", x2, g2, h3, gu3, act3, wts["ffn2_in"], wts["ffn2_out"], dy)

    dmix_a = _mm(dx2, w_o, nt=True, tk=D, tm=tms, a_layout="view", out_layout="slab", n_out=C, name="dmix_a")
    dmix_b = _mm(dx2, w_o, nt=True, tk=D, tm=tmm, a_layout="nat", out_layout="nat", n_out=C, w_off=1, name="dmix_b")
    dwo = _mm_tn(out_a.reshape(1, T, C), dx2.reshape(n16, NSLAB * D), bm=C, bn=bn, bt=n16, b_slabs=True,
                 rows=2 * C, name="dwo_a")
    dwo = _mm_tn(out_b.reshape(1, T, C), dx2, bm=C, bn=bn, bt=tm, rows=2 * C, m_off=1, into=dwo, name="dwo_b")
    grads["w_out"] = dwo[0]

    dqa, dka, dva, dtab = hosted("dil_bwd", _dil_bwd, nrm_a, table, out_a, dmix_a, lse_a)
    dqb, dkb, dvb, dck, dcq = hosted("fox_bwd", _fox_bwd, nrm_b, c_col, c_row, out_b, dmix_b, lse_b, tq)
    grads["rel_bias_table"] = dtab[:, 0, :NUM_BUCKETS].T
    dc = _pad_lanes((dck[:, 0, :] + dcq[:, :, 0]).T)
    df, dfb = _fox_gate_bwd(f_raw, fbias, dc, "fox_gate_bwd")
    grads["forget_bias"] = dfb[:, :H]

    flat = lambda a: a.reshape(T, a.shape[-1])
    dproj_a, dgain_a = _qknorm_bwd(flat(raw_a), flat(dqa), flat(dka), flat(dva), gains_a[:2], min(256, T), "qknorm_bwd_a")
    dproj_b, dgain_b = _qknorm_bwd(raw_b, dqb, dkb, dvb, gains_b[:2], min(256, T), "qknorm_bwd_b")
    grads["q_norm_a"], grads["k_norm_a"] = dgain_a[0], dgain_a[1]
    grads["q_norm_b"], grads["k_norm_b"] = dgain_b[0], dgain_b[1]
    dproj_a = dproj_a.reshape(NSLAB, n16, 3 * C)

    dh2 = _mm(dproj_a, w_in_t, nt=False, tk=C, tm=tms, a_layout="slab", out_layout="view", name="dh2_a")
    dh2 = _mm(dproj_b, w_in_t, nt=False, tk=C, tm=tmm, a_layout="nat", out_layout="nat", resid=dh2, w_off=3, name="dh2_b")
    dh2 = _mm(df, w_f_t, nt=False, tk=HEAD, tm=tmm, a_layout="nat", out_layout="nat", resid=dh2, name="dh2_f")
    dx1, grads["mix_norm"] = _rms_bwd(x1, gm, dh2, dx2, tm, "mix_rms_bwd")
    bt = min(2048, T)
    dwt = _mm_tn(flat(dproj_a)[None], flat(h2a), bm=C, bn=bn, bt=bt, rows=6 * C + H, name="dw_a")
    dwt = _mm_tn(dproj_b[None], h2b, bm=C, bn=bn, bt=bt, rows=6 * C + H, m_off=3, into=dwt, name="dw_b")
    dwt = _mm_tn(df[None, :, :H], h2b, bm=H, bn=bn, bt=bt, rows=6 * C + H, m_off=6 * C // H, into=dwt, name="dw_f")
    grads["w_in_t"] = dwt[0]

    grad_x = ffn_backward("ffn1", x, g1, h1, gu1, act1, wts["ffn1_in"], wts["ffn1_out"], dx1)
    return loss_row, grad_x, grads


def _place():
    x, y, c = lax.axis_index("x"), lax.axis_index("y"), lax.axis_index("c")
    other_chips = [(1 - x, y), (x, 1 - y), (1 - x, 1 - y)]
    return x, y, c, other_chips


def _run_side(side, name):
    def body(*refs):
        si, so = len(side.ins), len(side.outs)
        side.start(refs[:si], refs[si:si + so], refs[si + so:])
        side.finish(refs[:si], refs[si:si + so], refs[si + so:])

    return pl.pallas_call(body, name=name, in_specs=[ANY] * len(side.ins), out_specs=[ANY] * len(side.outs),
                          out_shape=side.outs, scratch_shapes=side.sems)(*side.ins)


def _all_gather(shards):
    n = len(shards)

    def plan(ins, outs, sems):
        send_sems, recv_sems, local_sems = sems
        x, y, c, chips = _place()
        me, sibling = (x, y, c), (x, y, 1 - c)

        def copy(a, k, block, to, src=None):
            px, py, pc = block
            dst = outs[a].at[4 * px + 2 * py + pc]
            return pltpu.make_async_remote_copy(
                src_ref=dst if src is None else src, dst_ref=dst, send_sem=send_sems.at[7 * a + k],
                recv_sem=recv_sems.at[7 * a + k], device_id=to, device_id_type=MESH)

        mine = [pltpu.make_async_copy(ins[a], outs[a].at[4 * x + 2 * y + c], local_sems.at[a]) for a in range(n)]
        first = []
        for a in range(n):
            first.append(copy(a, 0, me, sibling, src=ins[a]))
            first += [copy(a, 1 + j, me, (*chip, c), src=ins[a]) for j, chip in enumerate(chips)]
        return copy, mine, first, me, sibling, c, chips

    def start(ins, outs, sems):
        _, mine, first, *_ = plan(ins, outs, sems)
        for cp in mine + first:
            cp.start()

    def finish(ins, outs, sems):
        copy, mine, first, me, sibling, c, chips = plan(ins, outs, sems)
        passed = []
        for a in range(n):
            for j, chip in enumerate(chips):
                copy(a, 1 + j, (*chip, c), me).wait_recv()
                fwd = copy(a, 4 + j, (*chip, c), sibling)
                fwd.start()
                passed.append(fwd)
        for a in range(n):
            copy(a, 0, sibling, me).wait_recv()
            for j, chip in enumerate(chips):
                copy(a, 4 + j, (*chip, 1 - c), me).wait_recv()
        for cp in first + passed:
            cp.wait_send()
        for cp in mine:
            cp.wait()

    return _Side(shards, [_sds((N_DEV,) + s.shape, s.dtype) for s in shards],
                 [pltpu.SemaphoreType.DMA((7 * n,)), pltpu.SemaphoreType.DMA((7 * n,)), pltpu.SemaphoreType.DMA((n,))],
                 start, finish)


def _all_gather_relayed(shards):
    n = len(shards)
    halves = [-(-(s.shape[0] // 2) // 16) * 16 for s in shards]

    def body_parts(ins, outs, sems):
        send_sems, recv_sems, local_sems = sems
        x, y, c, _ = _place()
        me, sib, xn, yn, dg = (x, y, c), (x, y, 1 - c), (1 - x, y, c), (x, 1 - y, c), (1 - x, 1 - y, c)

        def rows(a, block, part):
            px, py, pc = block
            whole = outs[a].at[4 * px + 2 * py + pc]
            if part is None:
                return whole
            return whole.at[pl.ds(0, halves[a])] if part == 0 else whole.at[pl.ds(halves[a], shards[a].shape[0] - halves[a])]

        def copy(a, k, block, part, to, src=None):
            dst = rows(a, block, part)
            return pltpu.make_async_remote_copy(
                src_ref=dst if src is None else src, dst_ref=dst, send_sem=send_sems.at[9 * a + k],
                recv_sem=recv_sems.at[9 * a + k], device_id=to, device_id_type=MESH)

        flip = lambda dev: (dev[0], dev[1], 1 - dev[2])
        mine = [pltpu.make_async_copy(ins[a], rows(a, me, None), local_sems.at[a]) for a in range(n)]
        own = [[copy(a, 0, me, None, sib, src=ins[a]), copy(a, 1, me, None, xn, src=ins[a]),
                copy(a, 2, me, None, yn, src=ins[a])] for a in range(n)]
        relays = lambda a: [(1, [copy(a, 3, xn, 0, yn), copy(a, 5, xn, None, sib)]),
                            (2, [copy(a, 4, yn, 1, xn), copy(a, 6, yn, None, sib)]),
                            (3, [copy(a, 7, dg, 0, sib)]), (4, [copy(a, 8, dg, 1, sib)])]
        lands = {0: (sib, None), 1: (xn, None), 2: (yn, None), 3: (dg, 0), 4: (dg, 1), 5: (flip(xn), None),
                 6: (flip(yn), None), 7: (flip(dg), 0), 8: (flip(dg), 1)}
        arrival = lambda a, k: copy(a, k, lands[k][0], lands[k][1], me)
        return mine, own, relays, arrival

    def start(ins, outs, sems):
        mine, own, _, _ = body_parts(ins, outs, sems)
        for cp in mine + [cp for per in own for cp in per]:
            cp.start()

    def finish(ins, outs, sems):
        mine, own, relays, arrival = body_parts(ins, outs, sems)
        sent = [cp for per in own for cp in per]
        relays = [relays(a) for a in range(n)]
        for stage in range(4):
            for a in range(n):
                after, passes = relays[a][stage]
                arrival(a, after).wait_recv()
                for cp in passes:
                    cp.start()
                sent += passes
        for a in range(n):
            for k in (0, 5, 6, 7, 8):
                arrival(a, k).wait_recv()
        for cp in sent:
            cp.wait_send()
        for cp in mine:
            cp.wait()

    return _Side(shards, [_sds((N_DEV,) + s.shape, s.dtype) for s in shards],
                 [pltpu.SemaphoreType.DMA((9 * n,)), pltpu.SemaphoreType.DMA((9 * n,)), pltpu.SemaphoreType.DMA((n,))],
                 start, finish)


def _exchange_in_chip(gs):
    n = len(gs)

    def copies(ins, outs, sems):
        x, y, c, _ = _place()
        return [pltpu.make_async_remote_copy(
            src_ref=ins[a].at[2 * q + 1 - c], dst_ref=outs[a].at[q], send_sem=sems[0].at[4 * a + q],
            recv_sem=sems[1].at[4 * a + q], device_id=(x, y, 1 - c), device_id_type=MESH)
            for a in range(n) for q in range(4)]

    def start(ins, outs, sems):
        for cp in copies(ins, outs, sems):
            cp.start()

    def finish(ins, outs, sems):
        for cp in copies(ins, outs, sems):
            cp.wait()

    return _Side(gs, [_sds((4,) + g.shape[1:], g.dtype) for g in gs],
                 [pltpu.SemaphoreType.DMA((4 * n,)), pltpu.SemaphoreType.DMA((4 * n,))], start, finish)


def _exchange_between_chips(ps):
    n = len(ps)

    def copies(ins, outs, sems):
        x, y, c, chips = _place()
        return [pltpu.make_async_remote_copy(
            src_ref=ins[a].at[2 * cx + cy], dst_ref=outs[a].at[j], send_sem=sems[0].at[3 * a + j],
            recv_sem=sems[1].at[3 * a + j], device_id=(cx, cy, c), device_id_type=MESH)
            for a in range(n) for j, (cx, cy) in enumerate(chips)]

    def start(ins, outs, sems):
        for cp in copies(ins, outs, sems):
            cp.start()

    def finish(ins, outs, sems):
        for cp in copies(ins, outs, sems):
            cp.wait()

    return _Side(ps, [_sds((3,) + p.shape[1:], p.dtype) for p in ps],
                 [pltpu.SemaphoreType.DMA((3 * n,)), pltpu.SemaphoreType.DMA((3 * n,))], start, finish)


def _all_reduce_small(v, name):
    R = v.shape[0]

    def body(v_ref, sum_ref, all_ref, send_sems, recv_sems):
        x, y, c, _ = _place()
        k = 4 * x + 2 * y + c
        all_ref[k] = v_ref[...]
        copies = []
        for rel in range(1, N_DEV):
            fx, fy, fc = (rel >> 2) & 1, (rel >> 1) & 1, rel & 1
            peer = (1 - x if fx else x, 1 - y if fy else y, 1 - c if fc else c)
            copies.append(pltpu.make_async_remote_copy(
                src_ref=v_ref, dst_ref=all_ref.at[k], send_sem=send_sems.at[rel - 1], recv_sem=recv_sems.at[rel - 1],
                device_id=peer, device_id_type=MESH))
        for cp in copies:
            cp.start()
        for rel in range(1, N_DEV):
            fx, fy, fc = (rel >> 2) & 1, (rel >> 1) & 1, rel & 1
            src = 4 * (1 - x if fx else x) + 2 * (1 - y if fy else y) + (1 - c if fc else c)
            pltpu.make_async_remote_copy(
                src_ref=v_ref, dst_ref=all_ref.at[src], send_sem=send_sems.at[rel - 1], recv_sem=recv_sems.at[rel - 1],
                device_id=(x, y, c), device_id_type=MESH).wait_recv()
        for cp in copies:
            cp.wait_send()
        tot = all_ref[0]
        for d in range(1, N_DEV):
            tot = tot + all_ref[d]
        sum_ref[...] = tot

    vm = pl.BlockSpec(memory_space=pltpu.VMEM)
    return pl.pallas_call(
        body, name=name, in_specs=[vm], out_specs=[vm, vm],
        out_shape=[_sds((R, HEAD), F32), _sds((N_DEV, R, HEAD), F32)],
        scratch_shapes=[pltpu.SemaphoreType.DMA((N_DEV - 1,)), pltpu.SemaphoreType.DMA((N_DEV - 1,))],
    )(v)[0]


def _tiles(rows, cols):
    tr = next((cand for cand in (688, 512, 256) if rows % cand == 0), rows)
    tc = 512 if (cols % 512 == 0 and tr * cols * 4 > (2 << 20)) else cols
    return tr, tc


def _chip_sum(g, r1, core, name):
    _, R, Cc = g.shape
    tr, tc = _tiles(R, Cc)

    def body(core_ref, g_ref, r_ref, p_ref):
        p_ref[...] = (g_ref[...] + r_ref[...]).astype(BF16)

    blk = lambda f: pl.BlockSpec((None, tr, tc), f)
    return pl.pallas_call(
        body, name=name,
        grid_spec=pltpu.PrefetchScalarGridSpec(
            num_scalar_prefetch=1, grid=(4, R // tr, Cc // tc),
            in_specs=[blk(lambda q, i, j, core: (2 * q + core[0], i, j)), blk(lambda q, i, j, core: (q, i, j))],
            out_specs=blk(lambda q, i, j, core: (q, i, j))),
        out_shape=_sds((4, R, Cc), BF16), compiler_params=_cparams(),
    )(core, g, r1)


def _adamw_update(gv, w_ref, m_ref, v_ref, d_ref, nm_ref, nv_ref):
    nm = B1 * m_ref[...] + (1.0 - B1) * gv
    nv = B2 * v_ref[...] + (1.0 - B2) * jnp.square(gv)
    m_hat = nm / (1.0 - B1 ** STEP)
    v_hat = nv / (1.0 - B2 ** STEP)
    d_ref[...] = -LR * (m_hat / (jnp.sqrt(v_hat) + EPS) + WD * w_ref[...])
    nm_ref[...] = nm
    nv_ref[...] = nv


def _reduce_adamw(g, r1, r2, where, w, m, v, name):
    _, R, Cc = g.shape
    tr, tc = _tiles(R, Cc)

    def body(where_ref, g_ref, r1_ref, r2_ref, w_ref, m_ref, v_ref, o_ref, d_ref, nm_ref, nv_ref):
        gv = ((g_ref[...] + r1_ref[...]) + r2_ref[0].astype(F32)) + (r2_ref[1].astype(F32) + r2_ref[2].astype(F32))
        o_ref[...] = gv
        _adamw_update(gv, w_ref, m_ref, v_ref, d_ref, nm_ref, nv_ref)

    blk = pl.BlockSpec((tr, tc), lambda i, j, w: (i, j))
    return pl.pallas_call(
        body, name=name,
        grid_spec=pltpu.PrefetchScalarGridSpec(
            num_scalar_prefetch=1, grid=(R // tr, Cc // tc),
            in_specs=[pl.BlockSpec((None, tr, tc), lambda i, j, w: (w[0], i, j)),
                      pl.BlockSpec((None, tr, tc), lambda i, j, w: (w[1], i, j)),
                      pl.BlockSpec((3, tr, tc), lambda i, j, w: (0, i, j)), blk, blk, blk],
            out_specs=[blk] * 4),
        out_shape=[_sds((R, Cc), F32)] * 4, compiler_params=_cparams(),
    )(where, g, r1, r2, w, m, v)


def _adamw(w, g, m, v, name):
    R, Cc = w.shape
    tr, tc = _tiles(R, Cc)

    def body(w_ref, g_ref, m_ref, v_ref, d_ref, nm_ref, nv_ref):
        _adamw_update(g_ref[...], w_ref, m_ref, v_ref, d_ref, nm_ref, nv_ref)

    blk = pl.BlockSpec((tr, tc), lambda i, j: (i, j))
    return pl.pallas_call(
        body, name=name, grid=(R // tr, Cc // tc), in_specs=[blk] * 4, out_specs=[blk] * 3,
        out_shape=[_sds((R, Cc), F32)] * 3, compiler_params=_cparams(),
    )(w, g, m, v)


SMALL = ("ffn1_norm", "mix_norm", "ffn2_norm", "q_norm_a", "k_norm_a", "q_norm_b", "k_norm_b", "forget_bias",
         "rel_bias_table")
LARGE = ("ffn1_w_in", "ffn1_w_out", "w_in", "w_out", "ffn2_w_in", "ffn2_w_out")
ORDER = ("ffn1_norm", "ffn1_w_in", "ffn1_w_out", "mix_norm", "w_in", "q_norm_a", "k_norm_a", "q_norm_b", "k_norm_b",
         "forget_bias", "rel_bias_table", "w_out", "ffn2_norm", "ffn2_w_in", "ffn2_w_out")


def _pack_small(vals):
    rows = []
    for name in SMALL:
        flat = vals[name].reshape(-1)
        pad = (-flat.shape[0]) % HEAD
        rows.append(jnp.pad(flat, (0, pad)).reshape(-1, HEAD))
    return jnp.concatenate(rows, axis=0)


def _unpack_small(packed, like):
    out, r = {}, 0
    for name in SMALL:
        size = like[name].size
        nrow = -(-size // HEAD)
        out[name] = packed[r:r + nrow].reshape(-1)[:size].reshape(like[name].shape)
        r += nrow
    return out


def kernel(x, ffn1_norm, ffn1_w_in, ffn1_w_out, mix_norm, w_in, q_norm_a, k_norm_a, q_norm_b, k_norm_b, forget_bias, rel_bias_table, w_out, ffn2_norm, ffn2_w_in, ffn2_w_out, loss_target, m_ffn1_norm, m_ffn1_w_in, m_ffn1_w_out, m_mix_norm, m_w_in, m_q_norm_a, m_k_norm_a, m_q_norm_b, m_k_norm_b, m_forget_bias, m_rel_bias_table, m_w_out, m_ffn2_norm, m_ffn2_w_in, m_ffn2_w_out, v_ffn1_norm, v_ffn1_w_in, v_ffn1_w_out, v_mix_norm, v_w_in, v_q_norm_a, v_k_norm_a, v_q_norm_b, v_k_norm_b, v_forget_bias, v_rel_bias_table, v_w_out, v_ffn2_norm, v_ffn2_w_in, v_ffn2_w_out):
    w = dict(ffn1_norm=ffn1_norm, ffn1_w_in=ffn1_w_in, ffn1_w_out=ffn1_w_out, mix_norm=mix_norm, w_in=w_in,
             q_norm_a=q_norm_a, k_norm_a=k_norm_a, q_norm_b=q_norm_b, k_norm_b=k_norm_b, forget_bias=forget_bias,
             rel_bias_table=rel_bias_table, w_out=w_out, ffn2_norm=ffn2_norm, ffn2_w_in=ffn2_w_in, ffn2_w_out=ffn2_w_out)
    m = dict(ffn1_norm=m_ffn1_norm, ffn1_w_in=m_ffn1_w_in, ffn1_w_out=m_ffn1_w_out, mix_norm=m_mix_norm, w_in=m_w_in,
             q_norm_a=m_q_norm_a, k_norm_a=m_k_norm_a, q_norm_b=m_q_norm_b, k_norm_b=m_k_norm_b,
             forget_bias=m_forget_bias, rel_bias_table=m_rel_bias_table, w_out=m_w_out, ffn2_norm=m_ffn2_norm,
             ffn2_w_in=m_ffn2_w_in, ffn2_w_out=m_ffn2_w_out)
    v = dict(ffn1_norm=v_ffn1_norm, ffn1_w_in=v_ffn1_w_in, ffn1_w_out=v_ffn1_w_out, mix_norm=v_mix_norm, w_in=v_w_in,
             q_norm_a=v_q_norm_a, k_norm_a=v_k_norm_a, q_norm_b=v_q_norm_b, k_norm_b=v_k_norm_b,
             forget_bias=v_forget_bias, rel_bias_table=v_rel_bias_table, w_out=v_w_out, ffn2_norm=v_ffn2_norm,
             ffn2_w_in=v_ffn2_w_in, ffn2_w_out=v_ffn2_w_out)
    T, D = x.shape[1], x.shape[2]
    C = D // 2
    H = C // HEAD
    ff_shard = ffn1_w_out.shape[1]

    f1i, = _run_side(_all_gather_relayed([ffn1_w_in[0].T.astype(BF16)]), "gather_ffn1")
    wts = dict(ffn1_in=f1i.reshape(2, N_DEV, ff_shard, D))
    xi, yi, ci = lax.axis_index("x"), lax.axis_index("y"), lax.axis_index("c")
    core = jnp.reshape(ci, (1,)).astype(jnp.int32)
    where = jnp.stack([4 * xi + 2 * yi + ci, 2 * xi + yi]).astype(jnp.int32)
    gs, r1, ps, r2 = {}, {}, {}, {}

    def by_destination(name, grads):
        key = name + "_t" if name.endswith("w_in") else name
        gs[name] = grads[key].reshape(N_DEV, -1, D)
        return grads.get(key + "_bf16", grads[key]).reshape(N_DEV, -1, D)

    def chip_sums(names):
        for name in names:
            ps[name] = _chip_sum(gs[name], r1[name], core, "chip_sum_" + name)
        return [ps[name] for name in names]

    class Plan:
        carried = {"ffn1_up": ("gather", ("ffn1_w_out", "w_out")), "ffn1_down": ("gather", ("w_in",)),
                   "dil_fwd": ("gather", ("ffn2_w_out",)), "fox_fwd": ("gather", ("ffn2_w_in",)),
                   "dil_bwd": ("in_chip", ("ffn2_w_in", "ffn2_w_out")), "fox_bwd": ("between", ("ffn2_w_in", "ffn2_w_out")),
                   "ffn1_bwd": ("in_chip", ("w_in", "w_out")), "ffn1_dwin": ("between", ("w_in", "w_out")),
                   "ffn1_rms_bwd": ("in_chip", ("ffn1_w_in",)), "ffn1_dwout": ("between", ("ffn1_w_in",))}

        def before(self, host, wts, grads):
            if host not in self.carried:
                return None
            kind, names = self.carried[host]
            if kind == "gather":
                return _all_gather([(w[n][0].T if n.endswith("w_in") else w[n][0]).astype(BF16) for n in names])
            if kind == "in_chip":
                return _exchange_in_chip([by_destination(n, grads) for n in names])
            return _exchange_between_chips(chip_sums(names))

        def after(self, host, res, wts, grads):
            kind, names = self.carried[host]
            if host == "ffn1_up":
                wts.update(ffn1_out=res[0], w_o=res[1].reshape(2 * C, D))
            elif host == "ffn1_down":
                w_in_t = res[0].reshape(-1, D)
                wts.update(w_in_t=w_in_t, w_f_t=jnp.pad(w_in_t[6 * C:], ((0, HEAD - H), (0, 0))))
            elif host == "dil_fwd":
                wts.update(ffn2_out=res[0])
            elif host == "fox_fwd":
                wts.update(ffn2_in=res[0].reshape(2, N_DEV, ff_shard, D))
            else:
                (r1 if kind == "in_chip" else r2).update(zip(names, res))

    small = {name: w[name] for name in SMALL}
    loss_row, grad_x, grads = _local_step(x[0], loss_target[0], small, wts, Plan())

    tail = ("ffn1_w_out",)
    r1[tail[0]], = _run_side(_exchange_in_chip([by_destination(tail[0], grads)]), "reduce_in_chip_tail")
    r2.update(zip(tail, _run_side(_exchange_between_chips(chip_sums(tail)), "reduce_between_chips_tail")))

    packed = _pack_small(grads)
    nsmall = packed.shape[0]
    packed = jnp.concatenate([packed, loss_row, jnp.zeros(((-nsmall - 1) % 8, HEAD), F32)], axis=0)
    reduced = _all_reduce_small(packed, "reduce_small")
    loss = reduced[nsmall, 0]
    g_small = _unpack_small(reduced[:nsmall], small)

    grad, delta, new_m, new_v = dict(g_small), {}, {}, {}
    for name in LARGE:
        to = (lambda t: t[0].T) if name.endswith("w_in") else (lambda t: t[0])
        back = (lambda t: t.T[None]) if name.endswith("w_in") else (lambda t: t[None])
        res = _reduce_adamw(gs[name], r1[name], r2[name], where, to(w[name]), to(m[name]), to(v[name]), "adamw_" + name)
        grad[name], delta[name], new_m[name], new_v[name] = (back(t) for t in res)
    d, nm, nv = _adamw(_pack_small(w), reduced[:nsmall], _pack_small(m), _pack_small(v), "adamw_small")
    delta.update(_unpack_small(d, small))
    new_m.update(_unpack_small(nm, small))
    new_v.update(_unpack_small(nv, small))
    return (loss, grad_x[None], *[grad[n] for n in ORDER], *[delta[n] for n in ORDER],
            *[new_m[n] for n in ORDER], *[new_v[n] for n in ORDER])
```

```python
import functools
import math

import numpy as np
import jax
import jax.numpy as jnp
from jax import lax
from jax.experimental import pallas as pl
from jax.experimental.pallas import tpu as pltpu

F32, BF16 = jnp.float32, jnp.bfloat16
HEAD = 128
NSLAB = 16
BLK = 128
DILATIONS = (1, 4, 16)
NUM_BUCKETS, MAX_DISTANCE = 32, 2048
RMS_EPS = 1e-6
NEG = -1e30
SCALE = HEAD ** -0.5
LR, B1, B2, EPS, WD, STEP = 0.001, 0.9, 0.999, 1e-08, 0.01, 10
N_DEV = 8
VMEM_LIMIT_BYTES = 56 << 20
MESH = pl.DeviceIdType.MESH


def _cparams(**kw):
    return pltpu.CompilerParams(vmem_limit_bytes=VMEM_LIMIT_BYTES, **kw)


def _nn(a, b):
    return jnp.dot(a, b, preferred_element_type=F32)


def _nt(a, b):
    return lax.dot_general(a, b, (((1,), (1,)), ((), ())), preferred_element_type=F32)


def _tn(a, b):
    return lax.dot_general(a, b, (((0,), (0,)), ((), ())), preferred_element_type=F32)


def _sds(shape, dtype):
    return jax.ShapeDtypeStruct(shape, dtype)


ANY = pl.BlockSpec(memory_space=pl.ANY)


class _Side:
    def __init__(self, ins, outs, sems, start, finish):
        self.ins, self.outs, self.sems, self.start, self.finish = list(ins), list(outs), list(sems), start, finish


def _call(body, *, name, grid, in_specs, out_specs, out_shape, args, scratch_shapes=(), side=None):
    in_specs, out_specs, out_shape = list(in_specs), list(out_specs), list(out_shape)
    scratch_shapes = list(scratch_shapes)
    if side is None:
        return pl.pallas_call(body, name=name, grid=grid, in_specs=in_specs, out_specs=out_specs, out_shape=out_shape,
                              scratch_shapes=scratch_shapes, compiler_params=_cparams())(*args)
    ni, no, ns = len(args), len(out_shape), len(scratch_shapes)
    si, so = len(side.ins), len(side.outs)

    def fused(*refs):
        h_in, s_in = refs[:ni], refs[ni:ni + si]
        h_out, s_out = refs[ni + si:ni + si + no], refs[ni + si + no:ni + si + no + so]
        h_scr, s_sem = refs[ni + si + no + so:ni + si + no + so + ns], refs[ni + si + no + so + ns:]
        ids = [pl.program_id(k) for k in range(len(grid))]
        first = functools.reduce(jnp.logical_and, [i == 0 for i in ids])
        last = functools.reduce(jnp.logical_and, [i == n - 1 for i, n in zip(ids, grid)])

        @pl.when(first)
        def _():
            side.start(s_in, s_out, s_sem)

        body(*h_in, *h_out, *h_scr)

        @pl.when(last)
        def _():
            side.finish(s_in, s_out, s_sem)

    res = pl.pallas_call(
        fused, name=name, grid=grid, in_specs=in_specs + [ANY] * si, out_specs=out_specs + [ANY] * so,
        out_shape=out_shape + side.outs, scratch_shapes=scratch_shapes + side.sems, compiler_params=_cparams(),
    )(*args, *side.ins)
    return list(res[:no]), list(res[no:])


def _ffn_fwd(x, g, win, wout, tm, name, side=None, target=None):
    T, D = x.shape
    nc, tf = win.shape[1], win.shape[2]
    down, loss = wout is not None, target is not None
    assert down or not loss

    def body(x_ref, g_ref, win_ref, *refs):
        ins, outs = refs[:down + loss], refs[down + loss:]
        h_ref, gu_ref, act_ref = outs[down:down + 3]
        i, j = pl.program_id(0), pl.program_id(1)

        @pl.when(j == 0)
        def _():
            xv = x_ref[...]
            r = lax.rsqrt(jnp.mean(xv * xv, axis=-1, keepdims=True) + RMS_EPS)
            h_ref[...] = (xv * r * g_ref[...]).astype(BF16)
            if down:
                outs[0][...] = jnp.zeros_like(outs[0])

        if loss:
            @pl.when((i == 0) & (j == 0))
            def _():
                outs[-1][...] = jnp.zeros_like(outs[-1])

        hb = h_ref[...]
        gt = _nt(win_ref[0], hb)
        up = _nt(win_ref[1], hb)
        gu_ref[0] = gt.astype(BF16)
        gu_ref[1] = up.astype(BF16)
        act = (gt * jax.nn.sigmoid(gt) * up).astype(BF16)
        act_ref[...] = act
        if down:
            y_ref = outs[0]
            y_ref[...] += _tn(act, ins[0][...])

            @pl.when(j == nc - 1)
            def _():
                y = x_ref[...] + 0.5 * y_ref[...]
                if not loss:
                    y_ref[...] = y
                    return
                err = y - ins[1][...]
                y_ref[...] = err * (1.0 / D)
                tot = 0.5 * jnp.sum(jnp.mean(err * err, axis=-1, keepdims=True), axis=0, keepdims=True)
                lane = lax.broadcasted_iota(jnp.int32, (1, HEAD), 1)
                outs[-1][...] += jnp.where(lane == 0, tot, 0.0)

    row = pl.BlockSpec((tm, D), lambda i, j: (i, 0))
    return _call(
        body, name=name, grid=(T // tm, nc), side=side,
        in_specs=[row, pl.BlockSpec((1, D), lambda i, j: (0, 0)),
                  pl.BlockSpec((2, None, tf, D), lambda i, j: (0, j, 0, 0))]
        + ([pl.BlockSpec((None, tf, D), lambda i, j: (j, 0, 0))] if down else []) + ([row] if loss else []),
        out_specs=([row] if down else [])
        + [row, pl.BlockSpec((2, None, tf, tm), lambda i, j: (0, j, 0, i)),
           pl.BlockSpec((None, tf, tm), lambda i, j: (j, 0, i))]
        + ([pl.BlockSpec((1, HEAD), lambda i, j: (0, 0))] if loss else []),
        out_shape=([_sds((T, D), F32)] if down else [])
        + [_sds((T, D), BF16), _sds((2, nc, tf, T), BF16), _sds((nc, tf, T), BF16)]
        + ([_sds((1, HEAD), F32)] if loss else []),
        args=(x, g, win) + ((wout,) if down else ()) + ((target,) if loss else ()))


def _ffn_down(x, act, wout, tm, name, side=None):
    T, D = x.shape
    nc, tf = wout.shape[0], wout.shape[1]

    def body(x_ref, act_ref, wout_ref, y_ref):
        j = pl.program_id(1)

        @pl.when(j == 0)
        def _():
            y_ref[...] = jnp.zeros_like(y_ref)

        y_ref[...] += _tn(act_ref[...], wout_ref[...])

        @pl.when(j == nc - 1)
        def _():
            y_ref[...] = x_ref[...] + 0.5 * y_ref[...]

    row = pl.BlockSpec((tm, D), lambda i, j: (i, 0))
    res = _call(
        body, name=name, grid=(T // tm, nc), side=side,
        in_specs=[row, pl.BlockSpec((None, tf, tm), lambda i, j: (j, 0, i)),
                  pl.BlockSpec((None, tf, D), lambda i, j: (j, 0, 0))],
        out_specs=[row], out_shape=[_sds((T, D), F32)], args=(x, act, wout))
    return res[0] if side is None else (res[0][0], res[1])


def _ffn_bwd(dy, gu, win, wout, tm, name, side=None):
    T, D = dy.shape
    nc, tf = wout.shape[0], wout.shape[1]

    def body(dy_ref, gu_ref, win_ref, wout_ref, dh_ref, dgu_ref, dyb_ref):
        j = pl.program_id(1)

        @pl.when(j == 0)
        def _():
            dh_ref[...] = jnp.zeros_like(dh_ref)
            dyb_ref[...] = (0.5 * dy_ref[...]).astype(BF16)

        dact = _nt(wout_ref[...], dyb_ref[...])
        gt = gu_ref[0].astype(F32)
        up = gu_ref[1].astype(F32)
        s = jax.nn.sigmoid(gt)
        dgb = (dact * up * (s * (1.0 + gt * (1.0 - s)))).astype(BF16)
        dub = (dact * (gt * s)).astype(BF16)
        dgu_ref[0] = dgb
        dgu_ref[1] = dub
        dh_ref[...] += _tn(dgb, win_ref[0]) + _tn(dub, win_ref[1])

    return _call(
        body, name=name, grid=(T // tm, nc), side=side,
        in_specs=[pl.BlockSpec((tm, D), lambda i, j: (i, 0)),
                  pl.BlockSpec((2, None, tf, tm), lambda i, j: (0, j, 0, i)),
                  pl.BlockSpec((2, None, tf, D), lambda i, j: (0, j, 0, 0)),
                  pl.BlockSpec((None, tf, D), lambda i, j: (j, 0, 0))],
        out_specs=[pl.BlockSpec((tm, D), lambda i, j: (i, 0)),
                   pl.BlockSpec((2, None, tf, tm), lambda i, j: (0, j, 0, i)),
                   pl.BlockSpec((tm, D), lambda i, j: (i, 0))],
        out_shape=[_sds((T, D), F32), _sds((2, nc, tf, T), BF16), _sds((T, D), BF16)],
        args=(dy, gu, win, wout))


def _rms_bwd(x, g, dh, dres, tm, name, side=None):
    T, D = x.shape

    def body(x_ref, g_ref, dh_ref, dres_ref, dx_ref, dg_ref):
        @pl.when(pl.program_id(0) == 0)
        def _():
            dg_ref[...] = jnp.zeros_like(dg_ref)

        xv = x_ref[...]
        r = lax.rsqrt(jnp.mean(xv * xv, axis=-1, keepdims=True) + RMS_EPS)
        xhat = xv * r
        dh = dh_ref[...]
        gd = dh * g_ref[...]
        dx_ref[...] = dres_ref[...] + r * (gd - xhat * jnp.mean(gd * xhat, axis=-1, keepdims=True))
        dg_ref[...] += jnp.sum(dh * xhat, axis=0, keepdims=True)

    row = pl.BlockSpec((tm, D), lambda i: (i, 0))
    one = pl.BlockSpec((1, D), lambda i: (0, 0))
    return _call(body, name=name, grid=(T // tm,), side=side, in_specs=[row, one, row, row], out_specs=[row, one],
                 out_shape=[_sds((T, D), F32), _sds((1, D), F32)], args=(x, g, dh, dres))


def _mm_tn(a, b, *, bm, bn, bt, name, b_slabs=False, side=None, rows=None, m_off=0, into=None, a_rows=False,
           twin=False):
    nz, T, M = (a.shape[0], a.shape[2], a.shape[1]) if a_rows else a.shape
    if b_slabs:
        N = b.shape[1] // NSLAB
        assert bt == T // NSLAB
        b_spec = pl.BlockSpec((bt, bn), lambda n, z, m, t: (0, t * (N // bn) + n))
    else:
        N = b.shape[1]
        b_spec = pl.BlockSpec((bt, bn), lambda n, z, m, t: (t, n))
    assert M % bm == 0 and N % bn == 0 and T % bt == 0, (M, bm, N, bn, T, bt)

    def body(a_ref, b_ref, *rest):
        c_ref = rest[-2] if twin else rest[-1]

        @pl.when(pl.program_id(3) == 0)
        def _():
            c_ref[...] = jnp.zeros_like(c_ref)

        ab, bb = a_ref[...].astype(BF16), b_ref[...].astype(BF16)
        c_ref[...] += _nn(ab, bb) if a_rows else _tn(ab, bb)
        if twin:
            @pl.when(pl.program_id(3) == T // bt - 1)
            def _():
                rest[-1][...] = c_ref[...].astype(BF16)

    grid = (N // bn, nz, M // bm, T // bt)
    a_spec = (pl.BlockSpec((None, bm, bt), lambda n, z, m, t: (z, m, t)) if a_rows
              else pl.BlockSpec((None, bt, bm), lambda n, z, m, t: (z, t, m)))
    in_specs = [a_spec, b_spec]
    out_spec = pl.BlockSpec((None, bm, bn), lambda n, z, m, t: (z, m + m_off, n))
    out_shape = _sds((nz, M if rows is None else rows, N), F32)
    if into is not None:
        assert side is None and not twin and into.shape == out_shape.shape
        return pl.pallas_call(body, name=name, grid=grid, in_specs=in_specs + [ANY], out_specs=out_spec,
                              out_shape=out_shape, input_output_aliases={2: 0}, compiler_params=_cparams())(a, b, into)
    outs = [out_shape] + ([_sds(out_shape.shape, BF16)] if twin else [])
    res = _call(body, name=name, grid=grid, side=side, in_specs=in_specs, out_specs=[out_spec] * len(outs),
                out_shape=outs, args=(a, b))
    mine = res if side is None else res[0]
    mine = tuple(mine) if twin else mine[0]
    return mine if side is None else (mine, res[1])


def _tok_spec(layout, tm, n16, C, bc, colmap):
    if layout == "nat":
        return pl.BlockSpec((tm, bc), lambda i, k: (i, colmap(k)))
    assert tm % n16 == 0
    if layout == "slab":
        return pl.BlockSpec((tm // n16, n16, bc), lambda i, k: (i, 0, colmap(k)))
    assert bc == C
    return pl.BlockSpec((n16, (tm // n16) * C), lambda i, k: (0, i))


def _tok_load(ref, layout, sp):
    if layout == "nat":
        return ref[...]
    if layout == "slab":
        return ref[...].reshape(-1, ref.shape[-1])
    c = ref.shape[1] // sp
    return jnp.concatenate([ref[:, s * c:(s + 1) * c] for s in range(sp)], axis=0)


def _tok_store(ref, layout, sp, val, cols=None, accumulate=False):
    def put(idx, v):
        if accumulate:
            ref[idx] += v
        else:
            ref[idx] = v

    lanes = slice(None) if cols is None else slice(cols[0], cols[0] + cols[1])
    if layout == "nat":
        put((slice(None), lanes), val)
    elif layout == "slab":
        put((slice(None), slice(None), lanes), val.reshape(sp, ref.shape[1], val.shape[-1]))
    else:
        assert cols is None
        c, n = ref.shape[1] // sp, ref.shape[0]
        for s in range(sp):
            put((slice(None), slice(s * c, (s + 1) * c)), val[s * n:(s + 1) * n])


def _proj(x, g, wt, gains, modes, *, tn, w_off, slabs, tm, normed_dtype, name):
    T, D = x.shape
    ntile = len(modes)
    N = ntile * tn
    n16 = T // NSLAB
    in_layout, out_layout = ("view", "slab") if slabs else ("nat", "nat")
    sp = tm // n16
    x_in = x.reshape(n16, NSLAB * D) if slabs else x
    x_spec = _tok_spec(in_layout, tm, n16, D, D, lambda n: 0)
    oshape = lambda c: (NSLAB, n16, c) if slabs else (T, c)
    ospec = lambda bc, cm: _tok_spec(out_layout, tm, n16, None, bc, cm)

    def body(x_ref, g_ref, w_ref, gains_ref, raw_ref, nrm_ref, h_ref):
        n = pl.program_id(1)

        @pl.when(n == 0)
        def _():
            xv = _tok_load(x_ref, in_layout, sp)
            r = lax.rsqrt(jnp.mean(xv * xv, axis=-1, keepdims=True) + RMS_EPS)
            _tok_store(h_ref, out_layout, sp, (xv * r * g_ref[...]).astype(BF16))

        y = _nt(_tok_load(h_ref, out_layout, sp), w_ref[...])
        _tok_store(raw_ref, out_layout, sp, y)
        for t, mode in enumerate(modes):
            @pl.when(n == t)
            def _(t=t, mode=mode):
                if not mode:
                    _tok_store(nrm_ref, out_layout, sp, y.astype(nrm_ref.dtype))
                    return
                gain = gains_ref[t]
                for k in range(tn // HEAD):
                    yk = y[:, k * HEAD:(k + 1) * HEAD]
                    r = lax.rsqrt(jnp.mean(yk * yk, axis=-1, keepdims=True) + RMS_EPS)
                    _tok_store(nrm_ref, out_layout, sp, (yk * r * gain).astype(nrm_ref.dtype), cols=(k * HEAD, HEAD))

    return pl.pallas_call(
        body, name=name, grid=(T // tm, ntile),
        in_specs=[x_spec, pl.BlockSpec((1, D), lambda i, n: (0, 0)),
                  pl.BlockSpec((tn, D), lambda i, n: (n + w_off, 0)),
                  pl.BlockSpec((ntile, 1, HEAD), lambda i, n: (0, 0, 0))],
        out_specs=[ospec(tn, lambda n: n), ospec(tn, lambda n: n), ospec(D, lambda n: 0)],
        out_shape=[_sds(oshape(N), F32), _sds(oshape(N), normed_dtype), _sds(oshape(D), BF16)],
        compiler_params=_cparams(),
    )(x_in, g, wt, gains)


def _mm(a, w, *, nt, tk, tm, a_layout, out_layout, resid=None, name, w_off=0, n_out=None):
    if a_layout == "slab":
        T, K = a.shape[0] * a.shape[1], a.shape[2]
    else:
        T, K = a.shape
    N = (w.shape[0] if nt else w.shape[1]) if n_out is None else n_out
    n16 = T // NSLAB
    nk = K // tk
    sp = tm // n16
    a_in = a.reshape(n16, NSLAB * K) if a_layout == "view" else a
    w_spec = (pl.BlockSpec((N, tk), lambda i, k: (w_off, k)) if nt
              else pl.BlockSpec((tk, N), lambda i, k: (k + w_off, 0)))
    o_spec = _tok_spec(out_layout, tm, n16, N, N, lambda k: 0)
    oshape = {"nat": (T, N), "slab": (NSLAB, n16, N), "view": (n16, NSLAB * N)}[out_layout]
    has_resid = resid is not None

    def body(*refs):
        a_ref, w_ref = refs[0], refs[1]
        o_ref = refs[-1]
        k = pl.program_id(1)

        @pl.when(k == 0)
        def _():
            o_ref[...] = refs[2][...] if has_resid else jnp.zeros_like(o_ref)

        ab = _tok_load(a_ref, a_layout, sp).astype(BF16)
        _tok_store(o_ref, out_layout, sp, _nt(ab, w_ref[...]) if nt else _nn(ab, w_ref[...]), accumulate=True)

    ins = [a_in, w]
    in_specs = [_tok_spec(a_layout, tm, n16, K, tk, lambda k: k), w_spec]
    if has_resid:
        ins.append(resid.reshape(n16, NSLAB * N) if out_layout == "view" else resid)
        in_specs.append(o_spec)
    out = pl.pallas_call(
        body, name=name, grid=(T // tm, nk), in_specs=in_specs, out_specs=o_spec,
        out_shape=_sds(oshape, F32), compiler_params=_cparams(),
    )(*ins)
    return out.reshape(T, N) if out_layout == "view" else out


def _log_sigmoid(z):
    return jnp.minimum(z, 0.0) - jnp.log(1.0 + jnp.exp(-jnp.abs(z)))


def _fox_gate_fwd(f_raw, fbias, name):
    T = f_raw.shape[0]
    cb = 256

    def body(f_ref, b_ref, c_ref):
        row = lax.broadcasted_iota(jnp.int32, (cb, cb), 0)
        col = lax.broadcasted_iota(jnp.int32, (cb, cb), 1)
        tri = (col <= row).astype(F32)
        carry = jnp.zeros((1, HEAD), F32)
        for i in range(T // cb):
            lf = _log_sigmoid(f_ref[i * cb:(i + 1) * cb, :] + b_ref[...])
            c = jnp.dot(tri, lf, preferred_element_type=F32, precision=lax.Precision.HIGHEST) + carry
            c_ref[i * cb:(i + 1) * cb, :] = c
            carry = c[cb - 1:cb, :]

    return pl.pallas_call(body, name=name, out_shape=_sds((T, HEAD), F32), compiler_params=_cparams())(f_raw, fbias)


def _fox_gate_bwd(f_raw, fbias, dc, name):
    T = f_raw.shape[0]
    cb = 256

    def body(f_ref, b_ref, dc_ref, df_ref, db_ref):
        row = lax.broadcasted_iota(jnp.int32, (cb, cb), 0)
        col = lax.broadcasted_iota(jnp.int32, (cb, cb), 1)
        tri = (col >= row).astype(F32)
        carry = jnp.zeros((1, HEAD), F32)
        dbias = jnp.zeros((1, HEAD), F32)
        for i in reversed(range(T // cb)):
            dlf = jnp.dot(tri, dc_ref[i * cb:(i + 1) * cb, :], preferred_element_type=F32,
                          precision=lax.Precision.HIGHEST) + carry
            carry = dlf[0:1, :]
            z = f_ref[i * cb:(i + 1) * cb, :] + b_ref[...]
            df = dlf * jax.nn.sigmoid(-z)
            df_ref[i * cb:(i + 1) * cb, :] = df
            dbias = dbias + jnp.sum(df, axis=0, keepdims=True)
        db_ref[...] = dbias

    return pl.pallas_call(body, name=name, out_shape=[_sds((T, HEAD), F32), _sds((1, HEAD), F32)],
                          compiler_params=_cparams())(f_raw, fbias, dc)


def _fox_fwd(qkv, c_col, c_row, tq, name, side=None):
    T = qkv.shape[0]
    H = qkv.shape[1] // (3 * HEAD)
    nq = T // tq
    c_blocks = c_row.reshape(H, nq, 1, tq)

    def body(q_ref, k_ref, v_ref, cq_ref, ck_ref, o_ref, lse_ref):
        qi = pl.program_id(1)
        q, cq = q_ref[...], cq_ref[...]
        causal = lax.broadcasted_iota(jnp.int32, (tq, tq), 1) <= lax.broadcasted_iota(jnp.int32, (tq, tq), 0)

        def key_block(ki, carry, diagonal):
            m, l, acc = carry
            rows = pl.ds(pl.multiple_of(ki * tq, tq), tq)
            s = _nt(q, k_ref[rows, :]) * SCALE + cq - ck_ref[ki]
            if diagonal:
                s = jnp.where(causal, s, NEG)
            m_new = jnp.maximum(m, jnp.max(s, axis=-1, keepdims=True))
            alpha = jnp.exp(m - m_new)
            p = jnp.exp(s - m_new)
            l = alpha * l + jnp.sum(p, axis=-1, keepdims=True)
            acc = alpha * acc + _nn(p.astype(BF16), v_ref[rows, :])
            return m_new, l, acc

        init = (jnp.full((tq, 1), NEG, F32), jnp.zeros((tq, 1), F32), jnp.zeros((tq, HEAD), F32))
        carry = lax.fori_loop(0, qi, lambda ki, c: key_block(ki, c, False), init)
        m, l, acc = key_block(qi, carry, True)
        o_ref[...] = acc / l
        lse_ref[...] = m + jnp.log(l)

    return _call(
        body, name=name, grid=(H, nq), side=side,
        in_specs=[pl.BlockSpec((tq, HEAD), lambda h, qi: (qi, h)),
                  pl.BlockSpec((T, HEAD), lambda h, qi: (0, H + h)),
                  pl.BlockSpec((T, HEAD), lambda h, qi: (0, 2 * H + h)),
                  pl.BlockSpec((None, tq, 1), lambda h, qi: (h, qi, 0)),
                  pl.BlockSpec((None, nq, 1, tq), lambda h, qi: (h, 0, 0, 0))],
        out_specs=[pl.BlockSpec((tq, HEAD), lambda h, qi: (qi, h)),
                   pl.BlockSpec((None, tq, 1), lambda h, qi: (h, qi, 0))],
        out_shape=[_sds((T, H * HEAD), F32), _sds((H, T, 1), F32)],
        args=(qkv, qkv, qkv, c_col, c_blocks))


def _fox_bwd(qkv, c_col, c_row, out, dout, lse, tq, name, side=None):
    T = qkv.shape[0]
    H = qkv.shape[1] // (3 * HEAD)
    nq = T // tq

    def body(q_ref, k_ref, v_ref, cq_ref, ck_ref, o_ref, do_ref, lse_ref, dq_ref, dk_ref, dv_ref, dck_ref, dcq_ref,
             delta_s):
        ki = pl.program_id(1)

        @pl.when(ki == 0)
        def _():
            dq_ref[...] = jnp.zeros_like(dq_ref)
            dcq_ref[...] = jnp.zeros_like(dcq_ref)
            delta_s[...] = jnp.sum(do_ref[...] * o_ref[...], axis=-1, keepdims=True)

        k, v, ck = k_ref[...], v_ref[...], ck_ref[...]
        causal = lax.broadcasted_iota(jnp.int32, (tq, tq), 1) <= lax.broadcasted_iota(jnp.int32, (tq, tq), 0)

        def query_block(qi, carry, diagonal):
            dk, dv, dck = carry
            rows = pl.ds(pl.multiple_of(qi * tq, tq), tq)
            q = q_ref[rows, :]
            s = _nt(q, k) * SCALE + cq_ref[rows, :] - ck
            if diagonal:
                s = jnp.where(causal, s, NEG)
            p = jnp.exp(s - lse_ref[rows, :])
            dob = do_ref[rows, :].astype(BF16)
            ds = p * (_nt(dob, v) - delta_s[rows, :])
            dsb = ds.astype(BF16)
            dq_ref[rows, :] += _nn(dsb, k) * SCALE
            dcq_ref[rows, :] += jnp.sum(ds, axis=-1, keepdims=True)
            return dk + _tn(dsb, q), dv + _tn(p.astype(BF16), dob), dck - jnp.sum(ds, axis=0, keepdims=True)

        init = (jnp.zeros((tq, HEAD), F32), jnp.zeros((tq, HEAD), F32), jnp.zeros((1, tq), F32))
        carry = query_block(ki, init, True)
        dk, dv, dck = lax.fori_loop(ki + 1, nq, lambda qi, c: query_block(qi, c, False), carry)
        dk_ref[...] = dk * SCALE
        dv_ref[...] = dv
        dck_ref[...] = dck

    head = lambda off: pl.BlockSpec((T, HEAD), lambda h, ki: (0, off + h))
    col = pl.BlockSpec((None, T, 1), lambda h, ki: (h, 0, 0))
    return _call(
        body, name=name, grid=(H, nq), side=side,
        in_specs=[head(0),
                  pl.BlockSpec((tq, HEAD), lambda h, ki: (ki, H + h)),
                  pl.BlockSpec((tq, HEAD), lambda h, ki: (ki, 2 * H + h)),
                  col, pl.BlockSpec((None, 1, tq), lambda h, ki: (h, 0, ki)), head(0), head(0), col],
        out_specs=[head(0),
                   pl.BlockSpec((tq, HEAD), lambda h, ki: (ki, h)),
                   pl.BlockSpec((tq, HEAD), lambda h, ki: (ki, h)),
                   pl.BlockSpec((None, 1, tq), lambda h, ki: (h, 0, ki)), col],
        out_shape=[_sds((T, H * HEAD), F32), _sds((T, H * HEAD), F32), _sds((T, H * HEAD), F32), _sds((H, 1, T), F32),
                   _sds((H, T, 1), F32)],
        scratch_shapes=[pltpu.VMEM((T, 1), F32)],
        args=(qkv, qkv, qkv, c_col, c_row, out, dout, lse))


def _t5_bucket(dist):
    max_exact = NUM_BUCKETS // 2
    d = dist.astype(np.float32)
    large = max_exact + (np.log(np.maximum(d, np.float32(1.0)) / np.float32(max_exact))
                         / np.float32(math.log(MAX_DISTANCE / max_exact))
                         * np.float32(NUM_BUCKETS - max_exact)).astype(np.int32)
    large = np.minimum(large, NUM_BUCKETS - 1)
    return np.where(dist < max_exact, dist, large)


def _bucket_maps():
    maps = []
    for d in DILATIONS:
        e = NSLAB // d
        rows = BLK // e
        idx = np.arange(BLK)
        pos = e * (idx % rows) + idx // rows
        qpos = pos[:, None] + BLK
        kpos = np.concatenate([pos, pos + BLK])[None, :]
        delta = qpos - kpos
        band = (delta >= 0) & (delta <= BLK)
        bucket = _t5_bucket(np.clip(delta, 0, None) * d)
        maps.append(np.where(band, bucket, -1).astype(np.int32))
    return np.stack(maps)


def _dil_geometry(T):
    n16 = T // NSLAB
    geo = []
    for d in DILATIONS:
        e = NSLAB // d
        rows = BLK // e
        nblk = n16 // rows
        geo.append((d, e, rows, nblk))
    return geo


DIL_INTERLEAVE_FWD = {1: 4, 4: 8, 16: 8}
DIL_INTERLEAVE_BWD = {1: 8, 4: 8, 16: 8}


def _dil_interleave(per_step, nblocks):
    while per_step > 1 and (nblocks % per_step or nblocks // per_step < 2):
        per_step -= 1
    return per_step


def _dil_bias(tab_ref, bkt_ref, bias_s, h):
    for p in range(len(DILATIONS)):
        bk = bkt_ref[p]
        bias = jnp.full((BLK, 2 * BLK), NEG, F32)
        for b in range(NUM_BUCKETS):
            bias = jnp.where(bk == b, tab_ref[b, h], bias)
        bias_s[p] = bias


def _dil_rows(d, e, rows, sub, blk):
    start = pl.multiple_of(blk * rows, rows)
    return [(sub + d * j, pl.ds(start, rows)) for j in range(e)]


def _gather(ref, idx):
    return jnp.concatenate([ref[s, r, :] for s, r in idx], axis=0)


def _scatter(ref, idx, val, rows):
    for j, (s, r) in enumerate(idx):
        ref[s, r, :] = val[j * rows:(j + 1) * rows]


def _scatter_add(ref, idx, val, rows):
    for j, (s, r) in enumerate(idx):
        ref[s, r, :] += val[j * rows:(j + 1) * rows]


def _dil_fwd(qkv, table, name, side=None):
    n16 = qkv.shape[1]
    T = NSLAB * n16
    H = qkv.shape[2] // (3 * HEAD)
    geo = _dil_geometry(T)
    bkt = jnp.asarray(_bucket_maps())

    def body(tab_ref, bkt_ref, q_ref, k_ref, v_ref, o_ref, lse_ref, bias_s, m_s, l_s):
        h = pl.program_id(0)
        _dil_bias(tab_ref, bkt_ref, bias_s, h)
        first_mask = lax.broadcasted_iota(jnp.int32, (BLK, 2 * BLK), 1) < BLK

        starts = len(DILATIONS) - 1

        def load(p, d, e, rows, sub, blk):
            cur = _dil_rows(d, e, rows, sub, blk)
            prev = _dil_rows(d, e, rows, sub, jnp.maximum(blk - 1, 0))
            q = _gather(q_ref, cur).astype(BF16)
            kk = jnp.concatenate([_gather(k_ref, prev), _gather(k_ref, cur)], axis=0).astype(BF16)
            vv = jnp.concatenate([_gather(v_ref, prev), _gather(v_ref, cur)], axis=0).astype(BF16)
            old = None if p == starts else (_gather(m_s, cur), _gather(l_s, cur), _gather(o_ref, cur))
            return cur, blk, q, kk, vv, old

        def compute(p, blk, q, kk, vv, old):
            s = _nt(q, kk) * SCALE + bias_s[p]
            s = jnp.where(first_mask & (blk == 0), NEG, s)
            m_blk = jnp.max(s, axis=-1, keepdims=True)
            if old is None:
                m_new = m_blk
                pr = jnp.exp(s - m_new)
                l_new = jnp.sum(pr, axis=-1, keepdims=True)
                acc = _nn(pr.astype(BF16), vv)
            else:
                m_old, l_old, acc_old = old
                m_new = jnp.maximum(m_old, m_blk)
                alpha = jnp.exp(m_old - m_new)
                pr = jnp.exp(s - m_new)
                l_new = alpha * l_old + jnp.sum(pr, axis=-1, keepdims=True)
                acc = alpha * acc_old + _nn(pr.astype(BF16), vv)
            if p == 0:
                return acc / l_new, m_new + jnp.log(l_new), None
            return acc, m_new, l_new

        def store(p, rows, cur, acc, m_new, l_new):
            _scatter(o_ref, cur, acc, rows)
            if p == 0:
                _scatter(lse_ref, cur, m_new, rows)
            else:
                _scatter(m_s, cur, m_new, rows)
                _scatter(l_s, cur, l_new, rows)

        for p in reversed(range(len(DILATIONS))):
            d, e, rows, nblk = geo[p]
            per_step = _dil_interleave(DIL_INTERLEAVE_FWD[d], d * nblk)

            def step(i, carry, p=p, d=d, e=e, rows=rows, nblk=nblk, per_step=per_step):
                ids = [i + u * (d * nblk // per_step) for u in range(per_step)]
                loaded = [load(p, d, e, rows, j // nblk, j % nblk) for j in ids]
                done = [(cur, compute(p, blk, q, kk, vv, old)) for cur, blk, q, kk, vv, old in loaded]
                for cur, res in done:
                    store(p, rows, cur, *res)
                return carry

            lax.fori_loop(0, d * nblk // per_step, step, 0)

    head = lambda off: pl.BlockSpec((NSLAB, n16, HEAD), lambda h: (0, 0, off + h))
    return _call(
        body, name=name, grid=(H,), side=side,
        in_specs=[pl.BlockSpec(memory_space=pltpu.SMEM), pl.BlockSpec((3, BLK, 2 * BLK), lambda h: (0, 0, 0)),
                  head(0), head(H), head(2 * H)],
        out_specs=[head(0), pl.BlockSpec((None, NSLAB, n16, 1), lambda h: (h, 0, 0, 0))],
        out_shape=[_sds((NSLAB, n16, H * HEAD), F32), _sds((H, NSLAB, n16, 1), F32)],
        scratch_shapes=[pltpu.VMEM((3, BLK, 2 * BLK), F32), pltpu.VMEM((NSLAB, n16, 1), F32),
                        pltpu.VMEM((NSLAB, n16, 1), F32)],
        args=(table, bkt, qkv, qkv, qkv))


def _dil_bwd(qkv, table, out, dout, lse, name, side=None):
    n16 = qkv.shape[1]
    T = NSLAB * n16
    H = qkv.shape[2] // (3 * HEAD)
    geo = _dil_geometry(T)
    bkt = jnp.asarray(_bucket_maps())

    def body(tab_ref, bkt_ref, q_ref, k_ref, v_ref, o_ref, do_ref, lse_ref,
             dq_ref, dk_ref, dv_ref, dtab_ref, bias_s, dbias_s, delta_s):
        h = pl.program_id(0)
        _dil_bias(tab_ref, bkt_ref, bias_s, h)
        first_mask = lax.broadcasted_iota(jnp.int32, (BLK, 2 * BLK), 1) < BLK
        dbias_s[...] = jnp.zeros_like(dbias_s)
        dq_ref[...] = jnp.zeros_like(dq_ref)
        dk_ref[...] = jnp.zeros_like(dk_ref)
        dv_ref[...] = jnp.zeros_like(dv_ref)
        for r in range(NSLAB):
            delta_s[r] = jnp.sum(do_ref[r] * o_ref[r], axis=-1, keepdims=True)

        def load(d, e, rows, sub, blk):
            cur = _dil_rows(d, e, rows, sub, blk)
            prev = _dil_rows(d, e, rows, sub, jnp.maximum(blk - 1, 0))
            q = _gather(q_ref, cur).astype(BF16)
            kk = jnp.concatenate([_gather(k_ref, prev), _gather(k_ref, cur)], axis=0).astype(BF16)
            vv = jnp.concatenate([_gather(v_ref, prev), _gather(v_ref, cur)], axis=0).astype(BF16)
            dob = _gather(do_ref, cur).astype(BF16)
            return cur, prev, blk, q, kk, vv, dob, _gather(lse_ref, cur), _gather(delta_s, cur)

        def compute(p, blk, q, kk, vv, dob, lse, delta):
            s = _nt(q, kk) * SCALE + bias_s[p]
            s = jnp.where(first_mask & (blk == 0), NEG, s)
            pr = jnp.exp(s - lse)
            ds = pr * (_nt(dob, vv) - delta)
            dsb = ds.astype(BF16)
            return ds, _nn(dsb, kk) * SCALE, _tn(dsb, q) * SCALE, _tn(pr.astype(BF16), dob)

        def store(rows, cur, prev, dq, dkk, dvv):
            _scatter_add(dq_ref, cur, dq, rows)
            _scatter_add(dk_ref, prev, dkk[:BLK], rows)
            _scatter_add(dk_ref, cur, dkk[BLK:], rows)
            _scatter_add(dv_ref, prev, dvv[:BLK], rows)
            _scatter_add(dv_ref, cur, dvv[BLK:], rows)

        for p in range(len(DILATIONS)):
            d, e, rows, nblk = geo[p]
            per_step = _dil_interleave(DIL_INTERLEAVE_BWD[d], d * nblk)

            def step(i, carry, p=p, d=d, e=e, rows=rows, nblk=nblk, per_step=per_step):
                ids = [i + u * (d * nblk // per_step) for u in range(per_step)]
                loaded = [load(d, e, rows, j // nblk, j % nblk) for j in ids]
                done = [(cur, prev, compute(p, *rest)) for cur, prev, *rest in loaded]
                dbias_s[p] += functools.reduce(jnp.add, [res[0] for _, _, res in done])
                for cur, prev, res in done:
                    store(rows, cur, prev, *res[1:])
                return carry

            lax.fori_loop(0, d * nblk // per_step, step, 0)

        lane = lax.broadcasted_iota(jnp.int32, (1, HEAD), 1)
        row = jnp.zeros((1, HEAD), F32)
        for b in range(NUM_BUCKETS):
            tot = jnp.zeros((1, 1), F32)
            for p in range(len(DILATIONS)):
                hit = jnp.where(bkt_ref[p] == b, dbias_s[p], 0.0)
                tot = tot + jnp.sum(jnp.sum(hit, axis=0, keepdims=True), axis=1, keepdims=True)
            row = jnp.where(lane == b, tot, row)
        dtab_ref[...] = row

    head = lambda off: pl.BlockSpec((NSLAB, n16, HEAD), lambda h: (0, 0, off + h))
    return _call(
        body, name=name, grid=(H,), side=side,
        in_specs=[pl.BlockSpec(memory_space=pltpu.SMEM), pl.BlockSpec((3, BLK, 2 * BLK), lambda h: (0, 0, 0)),
                  head(0), head(H), head(2 * H), head(0), head(0),
                  pl.BlockSpec((None, NSLAB, n16, 1), lambda h: (h, 0, 0, 0))],
        out_specs=[head(0), head(0), head(0), pl.BlockSpec((None, 1, HEAD), lambda h: (h, 0, 0))],
        out_shape=[_sds((NSLAB, n16, H * HEAD), F32)] * 3 + [_sds((H, 1, HEAD), F32)],
        scratch_shapes=[pltpu.VMEM((3, BLK, 2 * BLK), F32), pltpu.VMEM((3, BLK, 2 * BLK), F32),
                        pltpu.VMEM((NSLAB, n16, 1), F32)],
        args=(table, bkt, qkv, qkv, qkv, out, dout, lse))


def _qknorm_bwd(raw, dq, dk, dv, gains, tm, name):
    T, N = raw.shape
    C = N // 3

    def body(raw_ref, dq_ref, dk_ref, dv_ref, gains_ref, dp_ref, dg_ref):
        @pl.when(pl.program_id(0) == 0)
        def _():
            dg_ref[...] = jnp.zeros_like(dg_ref)

        for t, d_ref in enumerate((dq_ref, dk_ref)):
            gain = gains_ref[t]
            dgain = jnp.zeros((1, HEAD), F32)
            for k in range(C // HEAD):
                y = raw_ref[:, t * C + k * HEAD:t * C + (k + 1) * HEAD]
                dn = d_ref[:, k * HEAD:(k + 1) * HEAD]
                r = lax.rsqrt(jnp.mean(y * y, axis=-1, keepdims=True) + RMS_EPS)
                yhat = y * r
                gd = dn * gain
                dy = r * (gd - yhat * jnp.mean(gd * yhat, axis=-1, keepdims=True))
                dp_ref[:, t * C + k * HEAD:t * C + (k + 1) * HEAD] = dy.astype(BF16)
                dgain = dgain + jnp.sum(dn * yhat, axis=0, keepdims=True)
            dg_ref[t] += dgain
        dp_ref[:, 2 * C:] = dv_ref[...].astype(BF16)

    third = pl.BlockSpec((tm, C), lambda i: (i, 0))
    return pl.pallas_call(
        body, name=name, grid=(T // tm,),
        in_specs=[pl.BlockSpec((tm, N), lambda i: (i, 0)), third, third, third,
                  pl.BlockSpec((2, 1, HEAD), lambda i: (0, 0, 0))],
        out_specs=[pl.BlockSpec((tm, N), lambda i: (i, 0)), pl.BlockSpec((2, 1, HEAD), lambda i: (0, 0, 0))],
        out_shape=[_sds((T, N), BF16), _sds((2, 1, HEAD), F32)], compiler_params=_cparams(),
    )(raw, dq, dk, dv, gains)


def _pad_lanes(v, width=HEAD):
    return jnp.pad(v, ((0, 0), (0, width - v.shape[1])))


def _local_step(x, target, small, wts, plan=None):
    grads = {}

    def hosted(host, fn, *args, **kw):
        side = plan.before(host, wts, grads) if plan is not None else None
        if side is None:
            return fn(*args, name=host, **kw)
        res, side_res = fn(*args, name=host, side=side, **kw)
        plan.after(host, side_res, wts, grads)
        return res

    T, D = x.shape
    C = D // 2
    H = C // HEAD
    n16 = T // NSLAB
    tm = min(512, T)
    tmm = min(1024, T)
    tms = 4 * n16
    tq = min(512, T)
    bn = min(1024, D)
    g1, gm, g2 = small["ffn1_norm"], small["mix_norm"], small["ffn2_norm"]
    gains_a = jnp.stack([small["q_norm_a"], small["k_norm_a"], jnp.ones_like(small["q_norm_a"])])
    gains_b = jnp.stack([small["q_norm_b"], small["k_norm_b"], jnp.ones_like(small["q_norm_b"])])
    fbias = _pad_lanes(small["forget_bias"])
    table = small["rel_bias_table"]

    h1, gu1, act1 = hosted("ffn1_up", _ffn_fwd, x, g1, wts["ffn1_in"], None, tm)
    x1 = hosted("ffn1_down", _ffn_down, x, act1, wts["ffn1_out"], tm)
    w_in_t, w_f_t, w_o = wts["w_in_t"], wts["w_f_t"], wts["w_o"]
    raw_a, nrm_a, h2a = _proj(x1, gm, w_in_t, gains_a, (True, True, False), tn=C, w_off=0, slabs=True, tm=tms,
                              normed_dtype=F32, name="proj_a")
    raw_b, nrm_b, h2b = _proj(x1, gm, w_in_t, gains_b, (True, True, False), tn=C, w_off=3, slabs=False, tm=tmm,
                              normed_dtype=BF16, name="proj_b")
    f_raw, _, _ = _proj(x1, gm, w_f_t, gains_b[:1], (False,), tn=HEAD, w_off=0, slabs=False, tm=tm,
                        normed_dtype=BF16, name="proj_f")
    c = _fox_gate_fwd(f_raw, fbias, "fox_gate_fwd")
    c_heads = c[:, :H].T
    c_col, c_row = c_heads[:, :, None], c_heads[:, None, :]
    out_a, lse_a = hosted("dil_fwd", _dil_fwd, nrm_a, table)
    out_b, lse_b = hosted("fox_fwd", _fox_fwd, nrm_b, c_col, c_row, tq)
    x2a = _mm(out_a, w_o, nt=False, tk=C, tm=tms, a_layout="slab", out_layout="view", resid=x1, name="out_a")
    x2 = _mm(out_b, w_o, nt=False, tk=C, tm=tmm, a_layout="nat", out_layout="nat", resid=x2a, w_off=1, name="out_b")
    dy, h3, gu3, act3, loss_row = _ffn_fwd(x2, g2, wts["ffn2_in"], wts["ffn2_out"], tm, "ffn2_fwd", target=target)

    def ffn_backward(tag, xin, g, h, gu, act, win, wout, dres):
        nc, tf = wout.shape[0], wout.shape[1]
        dh, dgu, dyb = hosted(tag + "_bwd", _ffn_bwd, dres, gu, win, wout, tm)
        grads[tag + "_w_in_t"], grads[tag + "_w_in_t_bf16"] = hosted(
            tag + "_dwin", _mm_tn, dgu.reshape(2 * nc, tf, T), h, bm=tf, bn=bn, bt=T, a_rows=True, twin=True)
        dxin, grads[tag + "_norm"] = hosted(tag + "_rms_bwd", _rms_bwd, xin, g, dh, dres, tm)
        grads[tag + "_w_out"], grads[tag + "_w_out_bf16"] = hosted(
            tag + "_dwout", _mm_tn, act, dyb, bm=tf, bn=bn, bt=T, a_rows=True, twin=True)
        return dxin

    dx2 = ffn_backward("ffn2", x2, g2, h3, gu3, act3, wts["ffn2_in"], wts["ffn2_out"], dy)

    dmix_a = _mm(dx2, w_o, nt=True, tk=D, tm=tms, a_layout="view", out_layout="slab", n_out=C, name="dmix_a")
    dmix_b = _mm(dx2, w_o, nt=True, tk=D, tm=tmm, a_layout="nat", out_layout="nat", n_out=C, w_off=1, name="dmix_b")
    dwo = _mm_tn(out_a.reshape(1, T, C), dx2.reshape(n16, NSLAB * D), bm=C, bn=bn, bt=n16, b_slabs=True,
                 rows=2 * C, name="dwo_a")
    dwo = _mm_tn(out_b.reshape(1, T, C), dx2, bm=C, bn=bn, bt=tm, rows=2 * C, m_off=1, into=dwo, name="dwo_b")
    grads["w_out"] = dwo[0]

    dqa, dka, dva, dtab = hosted("dil_bwd", _dil_bwd, nrm_a, table, out_a, dmix_a, lse_a)
    dqb, dkb, dvb, dck, dcq = hosted("fox_bwd", _fox_bwd, nrm_b, c_col, c_row, out_b, dmix_b, lse_b, tq)
    grads["rel_bias_table"] = dtab[:, 0, :NUM_BUCKETS].T
    dc = _pad_lanes((dck[:, 0, :] + dcq[:, :, 0]).T)
    df, dfb = _fox_gate_bwd(f_raw, fbias, dc, "fox_gate_bwd")
    grads["forget_bias"] = dfb[:, :H]

    flat = lambda a: a.reshape(T, a.shape[-1])
    dproj_a, dgain_a = _qknorm_bwd(flat(raw_a), flat(dqa), flat(dka), flat(dva), gains_a[:2], min(256, T), "qknorm_bwd_a")
    dproj_b, dgain_b = _qknorm_bwd(raw_b, dqb, dkb, dvb, gains_b[:2], min(256, T), "qknorm_bwd_b")
    grads["q_norm_a"], grads["k_norm_a"] = dgain_a[0], dgain_a[1]
    grads["q_norm_b"], grads["k_norm_b"] = dgain_b[0], dgain_b[1]
    dproj_a = dproj_a.reshape(NSLAB, n16, 3 * C)

    dh2 = _mm(dproj_a, w_in_t, nt=False, tk=C, tm=tms, a_layout="slab", out_layout="view", name="dh2_a")
    dh2 = _mm(dproj_b, w_in_t, nt=False, tk=C, tm=tmm, a_layout="nat", out_layout="nat", resid=dh2, w_off=3, name="dh2_b")
    dh2 = _mm(df, w_f_t, nt=False, tk=HEAD, tm=tmm, a_layout="nat", out_layout="nat", resid=dh2, name="dh2_f")
    dx1, grads["mix_norm"] = _rms_bwd(x1, gm, dh2, dx2, tm, "mix_rms_bwd")
    bt = min(2048, T)
    dwt = _mm_tn(flat(dproj_a)[None], flat(h2a), bm=C, bn=bn, bt=bt, rows=6 * C + H, name="dw_a")
    dwt = _mm_tn(dproj_b[None], h2b, bm=C, bn=bn, bt=bt, rows=6 * C + H, m_off=3, into=dwt, name="dw_b")
    dwt = _mm_tn(df[None, :, :H], h2b, bm=H, bn=bn, bt=bt, rows=6 * C + H, m_off=6 * C // H, into=dwt, name="dw_f")
    grads["w_in_t"] = dwt[0]

    grad_x = ffn_backward("ffn1", x, g1, h1, gu1, act1, wts["ffn1_in"], wts["ffn1_out"], dx1)
    return loss_row, grad_x, grads


def _place():
    x, y, c = lax.axis_index("x"), lax.axis_index("y"), lax.axis_index("c")
    other_chips = [(1 - x, y), (x, 1 - y), (1 - x, 1 - y)]
    return x, y, c, other_chips


def _run_side(side, name):
    def body(*refs):
        si, so = len(side.ins), len(side.outs)
        side.start(refs[:si], refs[si:si + so], refs[si + so:])
        side.finish(refs[:si], refs[si:si + so], refs[si + so:])

    return pl.pallas_call(body, name=name, in_specs=[ANY] * len(side.ins), out_specs=[ANY] * len(side.outs),
                          out_shape=side.outs, scratch_shapes=side.sems)(*side.ins)


def _all_gather(shards):
    n = len(shards)

    def plan(ins, outs, sems):
        send_sems, recv_sems, local_sems = sems
        x, y, c, chips = _place()
        me, sibling = (x, y, c), (x, y, 1 - c)

        def copy(a, k, block, to, src=None):
            px, py, pc = block
            dst = outs[a].at[4 * px + 2 * py + pc]
            return pltpu.make_async_remote_copy(
                src_ref=dst if src is None else src, dst_ref=dst, send_sem=send_sems.at[7 * a + k],
                recv_sem=recv_sems.at[7 * a + k], device_id=to, device_id_type=MESH)

        mine = [pltpu.make_async_copy(ins[a], outs[a].at[4 * x + 2 * y + c], local_sems.at[a]) for a in range(n)]
        first = []
        for a in range(n):
            first.append(copy(a, 0, me, sibling, src=ins[a]))
            first += [copy(a, 1 + j, me, (*chip, c), src=ins[a]) for j, chip in enumerate(chips)]
        return copy, mine, first, me, sibling, c, chips

    def start(ins, outs, sems):
        _, mine, first, *_ = plan(ins, outs, sems)
        for cp in mine + first:
            cp.start()

    def finish(ins, outs, sems):
        copy, mine, first, me, sibling, c, chips = plan(ins, outs, sems)
        passed = []
        for a in range(n):
            for j, chip in enumerate(chips):
                copy(a, 1 + j, (*chip, c), me).wait_recv()
                fwd = copy(a, 4 + j, (*chip, c), sibling)
                fwd.start()
                passed.append(fwd)
        for a in range(n):
            copy(a, 0, sibling, me).wait_recv()
            for j, chip in enumerate(chips):
                copy(a, 4 + j, (*chip, 1 - c), me).wait_recv()
        for cp in first + passed:
            cp.wait_send()
        for cp in mine:
            cp.wait()

    return _Side(shards, [_sds((N_DEV,) + s.shape, s.dtype) for s in shards],
                 [pltpu.SemaphoreType.DMA((7 * n,)), pltpu.SemaphoreType.DMA((7 * n,)), pltpu.SemaphoreType.DMA((n,))],
                 start, finish)


def _all_gather_relayed(shards):
    n = len(shards)
    halves = [-(-(s.shape[0] // 2) // 16) * 16 for s in shards]

    def body_parts(ins, outs, sems):
        send_sems, recv_sems, local_sems = sems
        x, y, c, _ = _place()
        me, sib, xn, yn, dg = (x, y, c), (x, y, 1 - c), (1 - x, y, c), (x, 1 - y, c), (1 - x, 1 - y, c)

        def rows(a, block, part):
            px, py, pc = block
            whole = outs[a].at[4 * px + 2 * py + pc]
            if part is None:
                return whole
            return whole.at[pl.ds(0, halves[a])] if part == 0 else whole.at[pl.ds(halves[a], shards[a].shape[0] - halves[a])]

        def copy(a, k, block, part, to, src=None):
            dst = rows(a, block, part)
            return pltpu.make_async_remote_copy(
                src_ref=dst if src is None else src, dst_ref=dst, send_sem=send_sems.at[9 * a + k],
                recv_sem=recv_sems.at[9 * a + k], device_id=to, device_id_type=MESH)

        flip = lambda dev: (dev[0], dev[1], 1 - dev[2])
        mine = [pltpu.make_async_copy(ins[a], rows(a, me, None), local_sems.at[a]) for a in range(n)]
        own = [[copy(a, 0, me, None, sib, src=ins[a]), copy(a, 1, me, None, xn, src=ins[a]),
                copy(a, 2, me, None, yn, src=ins[a])] for a in range(n)]
        relays = lambda a: [(1, [copy(a, 3, xn, 0, yn), copy(a, 5, xn, None, sib)]),
                            (2, [copy(a, 4, yn, 1, xn), copy(a, 6, yn, None, sib)]),
                            (3, [copy(a, 7, dg, 0, sib)]), (4, [copy(a, 8, dg, 1, sib)])]
        lands = {0: (sib, None), 1: (xn, None), 2: (yn, None), 3: (dg, 0), 4: (dg, 1), 5: (flip(xn), None),
                 6: (flip(yn), None), 7: (flip(dg), 0), 8: (flip(dg), 1)}
        arrival = lambda a, k: copy(a, k, lands[k][0], lands[k][1], me)
        return mine, own, relays, arrival

    def start(ins, outs, sems):
        mine, own, _, _ = body_parts(ins, outs, sems)
        for cp in mine + [cp for per in own for cp in per]:
            cp.start()

    def finish(ins, outs, sems):
        mine, own, relays, arrival = body_parts(ins, outs, sems)
        sent = [cp for per in own for cp in per]
        relays = [relays(a) for a in range(n)]
        for stage in range(4):
            for a in range(n):
                after, passes = relays[a][stage]
                arrival(a, after).wait_recv()
                for cp in passes:
                    cp.start()
                sent += passes
        for a in range(n):
            for k in (0, 5, 6, 7, 8):
                arrival(a, k).wait_recv()
        for cp in sent:
            cp.wait_send()
        for cp in mine:
            cp.wait()

    return _Side(shards, [_sds((N_DEV,) + s.shape, s.dtype) for s in shards],
                 [pltpu.SemaphoreType.DMA((9 * n,)), pltpu.SemaphoreType.DMA((9 * n,)), pltpu.SemaphoreType.DMA((n,))],
                 start, finish)


def _exchange_in_chip(gs):
    n = len(gs)

    def copies(ins, outs, sems):
        x, y, c, _ = _place()
        return [pltpu.make_async_remote_copy(
            src_ref=ins[a].at[2 * q + 1 - c], dst_ref=outs[a].at[q], send_sem=sems[0].at[4 * a + q],
            recv_sem=sems[1].at[4 * a + q], device_id=(x, y, 1 - c), device_id_type=MESH)
            for a in range(n) for q in range(4)]

    def start(ins, outs, sems):
        for cp in copies(ins, outs, sems):
            cp.start()

    def finish(ins, outs, sems):
        for cp in copies(ins, outs, sems):
            cp.wait()

    return _Side(gs, [_sds((4,) + g.shape[1:], g.dtype) for g in gs],
                 [pltpu.SemaphoreType.DMA((4 * n,)), pltpu.SemaphoreType.DMA((4 * n,))], start, finish)


def _exchange_between_chips(ps):
    n = len(ps)

    def copies(ins, outs, sems):
        x, y, c, chips = _place()
        return [pltpu.make_async_remote_copy(
            src_ref=ins[a].at[2 * cx + cy], dst_ref=outs[a].at[j], send_sem=sems[0].at[3 * a + j],
            recv_sem=sems[1].at[3 * a + j], device_id=(cx, cy, c), device_id_type=MESH)
            for a in range(n) for j, (cx, cy) in enumerate(chips)]

    def start(ins, outs, sems):
        for cp in copies(ins, outs, sems):
            cp.start()

    def finish(ins, outs, sems):
        for cp in copies(ins, outs, sems):
            cp.wait()

    return _Side(ps, [_sds((3,) + p.shape[1:], p.dtype) for p in ps],
                 [pltpu.SemaphoreType.DMA((3 * n,)), pltpu.SemaphoreType.DMA((3 * n,))], start, finish)


def _all_reduce_small(v, name):
    R = v.shape[0]

    def body(v_ref, sum_ref, all_ref, send_sems, recv_sems):
        x, y, c, _ = _place()
        k = 4 * x + 2 * y + c
        all_ref[k] = v_ref[...]
        copies = []
        for rel in range(1, N_DEV):
            fx, fy, fc = (rel >> 2) & 1, (rel >> 1) & 1, rel & 1
            peer = (1 - x if fx else x, 1 - y if fy else y, 1 - c if fc else c)
            copies.append(pltpu.make_async_remote_copy(
                src_ref=v_ref, dst_ref=all_ref.at[k], send_sem=send_sems.at[rel - 1], recv_sem=recv_sems.at[rel - 1],
                device_id=peer, device_id_type=MESH))
        for cp in copies:
            cp.start()
        for rel in range(1, N_DEV):
            fx, fy, fc = (rel >> 2) & 1, (rel >> 1) & 1, rel & 1
            src = 4 * (1 - x if fx else x) + 2 * (1 - y if fy else y) + (1 - c if fc else c)
            pltpu.make_async_remote_copy(
                src_ref=v_ref, dst_ref=all_ref.at[src], send_sem=send_sems.at[rel - 1], recv_sem=recv_sems.at[rel - 1],
                device_id=(x, y, c), device_id_type=MESH).wait_recv()
        for cp in copies:
            cp.wait_send()
        tot = all_ref[0]
        for d in range(1, N_DEV):
            tot = tot + all_ref[d]
        sum_ref[...] = tot

    vm = pl.BlockSpec(memory_space=pltpu.VMEM)
    return pl.pallas_call(
        body, name=name, in_specs=[vm], out_specs=[vm, vm],
        out_shape=[_sds((R, HEAD), F32), _sds((N_DEV, R, HEAD), F32)],
        scratch_shapes=[pltpu.SemaphoreType.DMA((N_DEV - 1,)), pltpu.SemaphoreType.DMA((N_DEV - 1,))],
    )(v)[0]


def _tiles(rows, cols):
    tr = next((cand for cand in (688, 512, 256) if rows % cand == 0), rows)
    tc = 512 if (cols % 512 == 0 and tr * cols * 4 > (2 << 20)) else cols
    return tr, tc


def _chip_sum(g, r1, core, name):
    _, R, Cc = g.shape
    tr, tc = _tiles(R, Cc)

    def body(core_ref, g_ref, r_ref, p_ref):
        p_ref[...] = (g_ref[...] + r_ref[...]).astype(BF16)

    blk = lambda f: pl.BlockSpec((None, tr, tc), f)
    return pl.pallas_call(
        body, name=name,
        grid_spec=pltpu.PrefetchScalarGridSpec(
            num_scalar_prefetch=1, grid=(4, R // tr, Cc // tc),
            in_specs=[blk(lambda q, i, j, core: (2 * q + core[0], i, j)), blk(lambda q, i, j, core: (q, i, j))],
            out_specs=blk(lambda q, i, j, core: (q, i, j))),
        out_shape=_sds((4, R, Cc), BF16), compiler_params=_cparams(),
    )(core, g, r1)


def _adamw_update(gv, w_ref, m_ref, v_ref, d_ref, nm_ref, nv_ref):
    nm = B1 * m_ref[...] + (1.0 - B1) * gv
    nv = B2 * v_ref[...] + (1.0 - B2) * jnp.square(gv)
    m_hat = nm / (1.0 - B1 ** STEP)
    v_hat = nv / (1.0 - B2 ** STEP)
    d_ref[...] = -LR * (m_hat / (jnp.sqrt(v_hat) + EPS) + WD * w_ref[...])
    nm_ref[...] = nm
    nv_ref[...] = nv


def _reduce_adamw(g, r1, r2, where, w, m, v, name):
    _, R, Cc = g.shape
    tr, tc = _tiles(R, Cc)

    def body(where_ref, g_ref, r1_ref, r2_ref, w_ref, m_ref, v_ref, o_ref, d_ref, nm_ref, nv_ref):
        gv = ((g_ref[...] + r1_ref[...]) + r2_ref[0].astype(F32)) + (r2_ref[1].astype(F32) + r2_ref[2].astype(F32))
        o_ref[...] = gv
        _adamw_update(gv, w_ref, m_ref, v_ref, d_ref, nm_ref, nv_ref)

    blk = pl.BlockSpec((tr, tc), lambda i, j, w: (i, j))
    return pl.pallas_call(
        body, name=name,
        grid_spec=pltpu.PrefetchScalarGridSpec(
            num_scalar_prefetch=1, grid=(R // tr, Cc // tc),
            in_specs=[pl.BlockSpec((None, tr, tc), lambda i, j, w: (w[0], i, j)),
                      pl.BlockSpec((None, tr, tc), lambda i, j, w: (w[1], i, j)),
                      pl.BlockSpec((3, tr, tc), lambda i, j, w: (0, i, j)), blk, blk, blk],
            out_specs=[blk] * 4),
        out_shape=[_sds((R, Cc), F32)] * 4, compiler_params=_cparams(),
    )(where, g, r1, r2, w, m, v)


def _adamw(w, g, m, v, name):
    R, Cc = w.shape
    tr, tc = _tiles(R, Cc)

    def body(w_ref, g_ref, m_ref, v_ref, d_ref, nm_ref, nv_ref):
        _adamw_update(g_ref[...], w_ref, m_ref, v_ref, d_ref, nm_ref, nv_ref)

    blk = pl.BlockSpec((tr, tc), lambda i, j: (i, j))
    return pl.pallas_call(
        body, name=name, grid=(R // tr, Cc // tc), in_specs=[blk] * 4, out_specs=[blk] * 3,
        out_shape=[_sds((R, Cc), F32)] * 3, compiler_params=_cparams(),
    )(w, g, m, v)


SMALL = ("ffn1_norm", "mix_norm", "ffn2_norm", "q_norm_a", "k_norm_a", "q_norm_b", "k_norm_b", "forget_bias",
         "rel_bias_table")
LARGE = ("ffn1_w_in", "ffn1_w_out", "w_in", "w_out", "ffn2_w_in", "ffn2_w_out")
ORDER = ("ffn1_norm", "ffn1_w_in", "ffn1_w_out", "mix_norm", "w_in", "q_norm_a", "k_norm_a", "q_norm_b", "k_norm_b",
         "forget_bias", "rel_bias_table", "w_out", "ffn2_norm", "ffn2_w_in", "ffn2_w_out")


def _pack_small(vals):
    rows = []
    for name in SMALL:
        flat = vals[name].reshape(-1)
        pad = (-flat.shape[0]) % HEAD
        rows.append(jnp.pad(flat, (0, pad)).reshape(-1, HEAD))
    return jnp.concatenate(rows, axis=0)


def _unpack_small(packed, like):
    out, r = {}, 0
    for name in SMALL:
        size = like[name].size
        nrow = -(-size // HEAD)
        out[name] = packed[r:r + nrow].reshape(-1)[:size].reshape(like[name].shape)
        r += nrow
    return out


def kernel(x, ffn1_norm, ffn1_w_in, ffn1_w_out, mix_norm, w_in, q_norm_a, k_norm_a, q_norm_b, k_norm_b, forget_bias, rel_bias_table, w_out, ffn2_norm, ffn2_w_in, ffn2_w_out, loss_target, m_ffn1_norm, m_ffn1_w_in, m_ffn1_w_out, m_mix_norm, m_w_in, m_q_norm_a, m_k_norm_a, m_q_norm_b, m_k_norm_b, m_forget_bias, m_rel_bias_table, m_w_out, m_ffn2_norm, m_ffn2_w_in, m_ffn2_w_out, v_ffn1_norm, v_ffn1_w_in, v_ffn1_w_out, v_mix_norm, v_w_in, v_q_norm_a, v_k_norm_a, v_q_norm_b, v_k_norm_b, v_forget_bias, v_rel_bias_table, v_w_out, v_ffn2_norm, v_ffn2_w_in, v_ffn2_w_out):
    w = dict(ffn1_norm=ffn1_norm, ffn1_w_in=ffn1_w_in, ffn1_w_out=ffn1_w_out, mix_norm=mix_norm, w_in=w_in,
             q_norm_a=q_norm_a, k_norm_a=k_norm_a, q_norm_b=q_norm_b, k_norm_b=k_norm_b, forget_bias=forget_bias,
             rel_bias_table=rel_bias_table, w_out=w_out, ffn2_norm=ffn2_norm, ffn2_w_in=ffn2_w_in, ffn2_w_out=ffn2_w_out)
    m = dict(ffn1_norm=m_ffn1_norm, ffn1_w_in=m_ffn1_w_in, ffn1_w_out=m_ffn1_w_out, mix_norm=m_mix_norm, w_in=m_w_in,
             q_norm_a=m_q_norm_a, k_norm_a=m_k_norm_a, q_norm_b=m_q_norm_b, k_norm_b=m_k_norm_b,
             forget_bias=m_forget_bias, rel_bias_table=m_rel_bias_table, w_out=m_w_out, ffn2_norm=m_ffn2_norm,
             ffn2_w_in=m_ffn2_w_in, ffn2_w_out=m_ffn2_w_out)
    v = dict(ffn1_norm=v_ffn1_norm, ffn1_w_in=v_ffn1_w_in, ffn1_w_out=v_ffn1_w_out, mix_norm=v_mix_norm, w_in=v_w_in,
             q_norm_a=v_q_norm_a, k_norm_a=v_k_norm_a, q_norm_b=v_q_norm_b, k_norm_b=v_k_norm_b,
             forget_bias=v_forget_bias, rel_bias_table=v_rel_bias_table, w_out=v_w_out, ffn2_norm=v_ffn2_norm,
             ffn2_w_in=v_ffn2_w_in, ffn2_w_out=v_ffn2_w_out)
    T, D = x.shape[1], x.shape[2]
    C = D // 2
    H = C // HEAD
    ff_shard = ffn1_w_out.shape[1]

    f1i, = _run_side(_all_gather_relayed([ffn1_w_in[0].T.astype(BF16)]), "gather_ffn1")
    wts = dict(ffn1_in=f1i.reshape(2, N_DEV, ff_shard, D))
    xi, yi, ci = lax.axis_index("x"), lax.axis_index("y"), lax.axis_index("c")
    core = jnp.reshape(ci, (1,)).astype(jnp.int32)
    where = jnp.stack([4 * xi + 2 * yi + ci, 2 * xi + yi]).astype(jnp.int32)
    gs, r1, ps, r2 = {}, {}, {}, {}

    def shard(name):
        name, _, part = name.partition(":")
        s = (w[name][0].T if name.endswith("w_in") else w[name][0]).astype(BF16)
        first = -(-(s.shape[0] // 2) // 16) * 16
        return {"": s, "first": s[:first], "rest": s[first:]}[part]

    def by_destination(name, grads):
        key = name + "_t" if name.endswith("w_in") else name
        gs[name] = grads[key].reshape(N_DEV, -1, D)
        return grads.get(key + "_bf16", grads[key]).reshape(N_DEV, -1, D)

    def chip_sums(names):
        for name in names:
            ps[name] = _chip_sum(gs[name], r1[name], core, "chip_sum_" + name)
        return [ps[name] for name in names]

    class Plan:
        carried = {"ffn1_up": ("gather", ("ffn1_w_out", "w_in:first")), "ffn1_down": ("gather", ("w_in:rest", "w_out")),
                   "dil_fwd": ("gather", ("ffn2_w_out",)), "fox_fwd": ("gather", ("ffn2_w_in",)),
                   "dil_bwd": ("in_chip", ("ffn2_w_in", "ffn2_w_out")), "fox_bwd": ("between", ("ffn2_w_in", "ffn2_w_out")),
                   "ffn1_bwd": ("in_chip", ("w_in", "w_out")), "ffn1_dwin": ("between", ("w_in", "w_out")),
                   "ffn1_rms_bwd": ("in_chip", ("ffn1_w_in",)), "ffn1_dwout": ("between", ("ffn1_w_in",))}

        def before(self, host, wts, grads):
            if host not in self.carried:
                return None
            kind, names = self.carried[host]
            if kind == "gather":
                return _all_gather([shard(n) for n in names])
            if kind == "in_chip":
                return _exchange_in_chip([by_destination(n, grads) for n in names])
            return _exchange_between_chips(chip_sums(names))

        def after(self, host, res, wts, grads):
            kind, names = self.carried[host]
            if host == "ffn1_up":
                wts.update(ffn1_out=res[0], w_in_first_rows=res[1])
            elif host == "ffn1_down":
                w_in_t = jnp.concatenate([wts.pop("w_in_first_rows"), res[0]], axis=1).reshape(-1, D)
                wts.update(w_in_t=w_in_t, w_f_t=jnp.pad(w_in_t[6 * C:], ((0, HEAD - H), (0, 0))),
                           w_o=res[1].reshape(2 * C, D))
            elif host == "dil_fwd":
                wts.update(ffn2_out=res[0])
            elif host == "fox_fwd":
                wts.update(ffn2_in=res[0].reshape(2, N_DEV, ff_shard, D))
            else:
                (r1 if kind == "in_chip" else r2).update(zip(names, res))

    small = {name: w[name] for name in SMALL}
    loss_row, grad_x, grads = _local_step(x[0], loss_target[0], small, wts, Plan())

    tail = ("ffn1_w_out",)
    r1[tail[0]], = _run_side(_exchange_in_chip([by_destination(tail[0], grads)]), "reduce_in_chip_tail")
    r2.update(zip(tail, _run_side(_exchange_between_chips(chip_sums(tail)), "reduce_between_chips_tail")))

    packed = _pack_small(grads)
    nsmall = packed.shape[0]
    packed = jnp.concatenate([packed, loss_row, jnp.zeros(((-nsmall - 1) % 8, HEAD), F32)], axis=0)
    reduced = _all_reduce_small(packed, "reduce_small")
    loss = reduced[nsmall, 0]
    g_small = _unpack_small(reduced[:nsmall], small)

    grad, delta, new_m, new_v = dict(g_small), {}, {}, {}
    for name in LARGE:
        to = (lambda t: t[0].T) if name.endswith("w_in") else (lambda t: t[0])
        back = (lambda t: t.T[None]) if name.endswith("w_in") else (lambda t: t[None])
        res = _reduce_adamw(gs[name], r1[name], r2[name], where, to(w[name]), to(m[name]), to(v[name]), "adamw_" + name)
        grad[name], delta[name], new_m[name], new_v[name] = (back(t) for t in res)
    d, nm, nv = _adamw(_pack_small(w), reduced[:nsmall], _pack_small(m), _pack_small(v), "adamw_small")
    delta.update(_unpack_small(d, small))
    new_m.update(_unpack_small(nm, small))
    new_v.update(_unpack_small(nv, small))
    return (loss, grad_x[None], *[grad[n] for n in ORDER], *[delta[n] for n in ORDER],
            *[new_m[n] for n in ORDER], *[new_v[n] for n in ORDER])
```

```python
import functools
import math

import numpy as np
import jax
import jax.numpy as jnp
from jax import lax
from jax.experimental import pallas as pl
from jax.experimental.pallas import tpu as pltpu

F32, BF16 = jnp.float32, jnp.bfloat16
HEAD = 128
NSLAB = 16
BLK = 128
DILATIONS = (1, 4, 16)
NUM_BUCKETS, MAX_DISTANCE = 32, 2048
RMS_EPS = 1e-6
NEG = -1e30
SCALE = HEAD ** -0.5
LR, B1, B2, EPS, WD, STEP = 0.001, 0.9, 0.999, 1e-08, 0.01, 10
N_DEV = 8
VMEM_LIMIT_BYTES = 56 << 20
MESH = pl.DeviceIdType.MESH


def _cparams(**kw):
    return pltpu.CompilerParams(vmem_limit_bytes=VMEM_LIMIT_BYTES, **kw)


def _nn(a, b):
    return jnp.dot(a, b, preferred_element_type=F32)


def _nt(a, b):
    return lax.dot_general(a, b, (((1,), (1,)), ((), ())), preferred_element_type=F32)


def _tn(a, b):
    return lax.dot_general(a, b, (((0,), (0,)), ((), ())), preferred_element_type=F32)


def _sds(shape, dtype):
    return jax.ShapeDtypeStruct(shape, dtype)


ANY = pl.BlockSpec(memory_space=pl.ANY)


class _Side:
    def __init__(self, ins, outs, sems, start, finish):
        self.ins, self.outs, self.sems, self.start, self.finish = list(ins), list(outs), list(sems), start, finish


def _call(body, *, name, grid, in_specs, out_specs, out_shape, args, scratch_shapes=(), side=None):
    in_specs, out_specs, out_shape = list(in_specs), list(out_specs), list(out_shape)
    scratch_shapes = list(scratch_shapes)
    if side is None:
        return pl.pallas_call(body, name=name, grid=grid, in_specs=in_specs, out_specs=out_specs, out_shape=out_shape,
                              scratch_shapes=scratch_shapes, compiler_params=_cparams())(*args)
    ni, no, ns = len(args), len(out_shape), len(scratch_shapes)
    si, so = len(side.ins), len(side.outs)

    def fused(*refs):
        h_in, s_in = refs[:ni], refs[ni:ni + si]
        h_out, s_out = refs[ni + si:ni + si + no], refs[ni + si + no:ni + si + no + so]
        h_scr, s_sem = refs[ni + si + no + so:ni + si + no + so + ns], refs[ni + si + no + so + ns:]
        ids = [pl.program_id(k) for k in range(len(grid))]
        first = functools.reduce(jnp.logical_and, [i == 0 for i in ids])
        last = functools.reduce(jnp.logical_and, [i == n - 1 for i, n in zip(ids, grid)])

        @pl.when(first)
        def _():
            side.start(s_in, s_out, s_sem)

        body(*h_in, *h_out, *h_scr)

        @pl.when(last)
        def _():
            side.finish(s_in, s_out, s_sem)

    res = pl.pallas_call(
        fused, name=name, grid=grid, in_specs=in_specs + [ANY] * si, out_specs=out_specs + [ANY] * so,
        out_shape=out_shape + side.outs, scratch_shapes=scratch_shapes + side.sems, compiler_params=_cparams(),
    )(*args, *side.ins)
    return list(res[:no]), list(res[no:])


def _ffn_fwd(x, g, win, wout, tm, name, side=None, target=None):
    T, D = x.shape
    nc, tf = win.shape[1], win.shape[2]
    down, loss = wout is not None, target is not None
    assert down or not loss

    def body(x_ref, g_ref, win_ref, *refs):
        ins, outs = refs[:down + loss], refs[down + loss:]
        h_ref, gu_ref, act_ref = outs[down:down + 3]
        i, j = pl.program_id(0), pl.program_id(1)

        @pl.when(j == 0)
        def _():
            xv = x_ref[...]
            r = lax.rsqrt(jnp.mean(xv * xv, axis=-1, keepdims=True) + RMS_EPS)
            h_ref[...] = (xv * r * g_ref[...]).astype(BF16)
            if down:
                outs[0][...] = jnp.zeros_like(outs[0])

        if loss:
            @pl.when((i == 0) & (j == 0))
            def _():
                outs[-1][...] = jnp.zeros_like(outs[-1])

        hb = h_ref[...]
        gt = _nt(win_ref[0], hb)
        up = _nt(win_ref[1], hb)
        gu_ref[0] = gt.astype(BF16)
        gu_ref[1] = up.astype(BF16)
        act = (gt * jax.nn.sigmoid(gt) * up).astype(BF16)
        act_ref[...] = act
        if down:
            y_ref = outs[0]
            y_ref[...] += _tn(act, ins[0][...])

            @pl.when(j == nc - 1)
            def _():
                y = x_ref[...] + 0.5 * y_ref[...]
                if not loss:
                    y_ref[...] = y
                    return
                err = y - ins[1][...]
                y_ref[...] = err * (1.0 / D)
                tot = 0.5 * jnp.sum(jnp.mean(err * err, axis=-1, keepdims=True), axis=0, keepdims=True)
                lane = lax.broadcasted_iota(jnp.int32, (1, HEAD), 1)
                outs[-1][...] += jnp.where(lane == 0, tot, 0.0)

    row = pl.BlockSpec((tm, D), lambda i, j: (i, 0))
    return _call(
        body, name=name, grid=(T // tm, nc), side=side,
        in_specs=[row, pl.BlockSpec((1, D), lambda i, j: (0, 0)),
                  pl.BlockSpec((2, None, tf, D), lambda i, j: (0, j, 0, 0))]
        + ([pl.BlockSpec((None, tf, D), lambda i, j: (j, 0, 0))] if down else []) + ([row] if loss else []),
        out_specs=([row] if down else [])
        + [row, pl.BlockSpec((2, None, tf, tm), lambda i, j: (0, j, 0, i)),
           pl.BlockSpec((None, tf, tm), lambda i, j: (j, 0, i))]
        + ([pl.BlockSpec((1, HEAD), lambda i, j: (0, 0))] if loss else []),
        out_shape=([_sds((T, D), F32)] if down else [])
        + [_sds((T, D), BF16), _sds((2, nc, tf, T), BF16), _sds((nc, tf, T), BF16)]
        + ([_sds((1, HEAD), F32)] if loss else []),
        args=(x, g, win) + ((wout,) if down else ()) + ((target,) if loss else ()))


def _ffn_down(x, act, wout, tm, name, side=None):
    T, D = x.shape
    nc, tf = wout.shape[0], wout.shape[1]

    def body(x_ref, act_ref, wout_ref, y_ref):
        j = pl.program_id(1)

        @pl.when(j == 0)
        def _():
            y_ref[...] = jnp.zeros_like(y_ref)

        y_ref[...] += _tn(act_ref[...], wout_ref[...])

        @pl.when(j == nc - 1)
        def _():
            y_ref[...] = x_ref[...] + 0.5 * y_ref[...]

    row = pl.BlockSpec((tm, D), lambda i, j: (i, 0))
    res = _call(
        body, name=name, grid=(T // tm, nc), side=side,
        in_specs=[row, pl.BlockSpec((None, tf, tm), lambda i, j: (j, 0, i)),
                  pl.BlockSpec((None, tf, D), lambda i, j: (j, 0, 0))],
        out_specs=[row], out_shape=[_sds((T, D), F32)], args=(x, act, wout))
    return res[0] if side is None else (res[0][0], res[1])


def _ffn_bwd(dy, gu, win, wout, tm, name, side=None):
    T, D = dy.shape
    nc, tf = wout.shape[0], wout.shape[1]

    def body(dy_ref, gu_ref, win_ref, wout_ref, dh_ref, dgu_ref, dyb_ref):
        j = pl.program_id(1)

        @pl.when(j == 0)
        def _():
            dh_ref[...] = jnp.zeros_like(dh_ref)
            dyb_ref[...] = (0.5 * dy_ref[...]).astype(BF16)

        dact = _nt(wout_ref[...], dyb_ref[...])
        gt = gu_ref[0].astype(F32)
        up = gu_ref[1].astype(F32)
        s = jax.nn.sigmoid(gt)
        dgb = (dact * up * (s * (1.0 + gt * (1.0 - s)))).astype(BF16)
        dub = (dact * (gt * s)).astype(BF16)
        dgu_ref[0] = dgb
        dgu_ref[1] = dub
        dh_ref[...] += _tn(dgb, win_ref[0]) + _tn(dub, win_ref[1])

    return _call(
        body, name=name, grid=(T // tm, nc), side=side,
        in_specs=[pl.BlockSpec((tm, D), lambda i, j: (i, 0)),
                  pl.BlockSpec((2, None, tf, tm), lambda i, j: (0, j, 0, i)),
                  pl.BlockSpec((2, None, tf, D), lambda i, j: (0, j, 0, 0)),
                  pl.BlockSpec((None, tf, D), lambda i, j: (j, 0, 0))],
        out_specs=[pl.BlockSpec((tm, D), lambda i, j: (i, 0)),
                   pl.BlockSpec((2, None, tf, tm), lambda i, j: (0, j, 0, i)),
                   pl.BlockSpec((tm, D), lambda i, j: (i, 0))],
        out_shape=[_sds((T, D), F32), _sds((2, nc, tf, T), BF16), _sds((T, D), BF16)],
        args=(dy, gu, win, wout))


def _rms_bwd(x, g, dh, dres, tm, name, side=None):
    T, D = x.shape

    def body(x_ref, g_ref, dh_ref, dres_ref, dx_ref, dg_ref):
        @pl.when(pl.program_id(0) == 0)
        def _():
            dg_ref[...] = jnp.zeros_like(dg_ref)

        xv = x_ref[...]
        r = lax.rsqrt(jnp.mean(xv * xv, axis=-1, keepdims=True) + RMS_EPS)
        xhat = xv * r
        dh = dh_ref[...]
        gd = dh * g_ref[...]
        dx_ref[...] = dres_ref[...] + r * (gd - xhat * jnp.mean(gd * xhat, axis=-1, keepdims=True))
        dg_ref[...] += jnp.sum(dh * xhat, axis=0, keepdims=True)

    row = pl.BlockSpec((tm, D), lambda i: (i, 0))
    one = pl.BlockSpec((1, D), lambda i: (0, 0))
    return _call(body, name=name, grid=(T // tm,), side=side, in_specs=[row, one, row, row], out_specs=[row, one],
                 out_shape=[_sds((T, D), F32), _sds((1, D), F32)], args=(x, g, dh, dres))


def _mm_tn(a, b, *, bm, bn, bt, name, b_slabs=False, side=None, rows=None, m_off=0, into=None, a_rows=False,
           twin=False):
    nz, T, M = (a.shape[0], a.shape[2], a.shape[1]) if a_rows else a.shape
    if b_slabs:
        N = b.shape[1] // NSLAB
        assert bt == T // NSLAB
        b_spec = pl.BlockSpec((bt, bn), lambda n, z, m, t: (0, t * (N // bn) + n))
    else:
        N = b.shape[1]
        b_spec = pl.BlockSpec((bt, bn), lambda n, z, m, t: (t, n))
    assert M % bm == 0 and N % bn == 0 and T % bt == 0, (M, bm, N, bn, T, bt)

    def body(a_ref, b_ref, *rest):
        c_ref = rest[-2] if twin else rest[-1]

        @pl.when(pl.program_id(3) == 0)
        def _():
            c_ref[...] = jnp.zeros_like(c_ref)

        ab, bb = a_ref[...].astype(BF16), b_ref[...].astype(BF16)
        c_ref[...] += _nn(ab, bb) if a_rows else _tn(ab, bb)
        if twin:
            @pl.when(pl.program_id(3) == T // bt - 1)
            def _():
                rest[-1][...] = c_ref[...].astype(BF16)

    grid = (N // bn, nz, M // bm, T // bt)
    a_spec = (pl.BlockSpec((None, bm, bt), lambda n, z, m, t: (z, m, t)) if a_rows
              else pl.BlockSpec((None, bt, bm), lambda n, z, m, t: (z, t, m)))
    in_specs = [a_spec, b_spec]
    out_spec = pl.BlockSpec((None, bm, bn), lambda n, z, m, t: (z, m + m_off, n))
    out_shape = _sds((nz, M if rows is None else rows, N), F32)
    if into is not None:
        assert side is None and not twin and into.shape == out_shape.shape
        return pl.pallas_call(body, name=name, grid=grid, in_specs=in_specs + [ANY], out_specs=out_spec,
                              out_shape=out_shape, input_output_aliases={2: 0}, compiler_params=_cparams())(a, b, into)
    outs = [out_shape] + ([_sds(out_shape.shape, BF16)] if twin else [])
    res = _call(body, name=name, grid=grid, side=side, in_specs=in_specs, out_specs=[out_spec] * len(outs),
                out_shape=outs, args=(a, b))
    mine = res if side is None else res[0]
    mine = tuple(mine) if twin else mine[0]
    return mine if side is None else (mine, res[1])


def _tok_spec(layout, tm, n16, C, bc, colmap):
    if layout == "nat":
        return pl.BlockSpec((tm, bc), lambda i, k: (i, colmap(k)))
    assert tm % n16 == 0
    if layout == "slab":
        return pl.BlockSpec((tm // n16, n16, bc), lambda i, k: (i, 0, colmap(k)))
    assert bc == C
    return pl.BlockSpec((n16, (tm // n16) * C), lambda i, k: (0, i))


def _tok_load(ref, layout, sp):
    if layout == "nat":
        return ref[...]
    if layout == "slab":
        return ref[...].reshape(-1, ref.shape[-1])
    c = ref.shape[1] // sp
    return jnp.concatenate([ref[:, s * c:(s + 1) * c] for s in range(sp)], axis=0)


def _tok_store(ref, layout, sp, val, cols=None, accumulate=False):
    def put(idx, v):
        if accumulate:
            ref[idx] += v
        else:
            ref[idx] = v

    lanes = slice(None) if cols is None else slice(cols[0], cols[0] + cols[1])
    if layout == "nat":
        put((slice(None), lanes), val)
    elif layout == "slab":
        put((slice(None), slice(None), lanes), val.reshape(sp, ref.shape[1], val.shape[-1]))
    else:
        assert cols is None
        c, n = ref.shape[1] // sp, ref.shape[0]
        for s in range(sp):
            put((slice(None), slice(s * c, (s + 1) * c)), val[s * n:(s + 1) * n])


def _proj(x, g, wt, gains, modes, *, tn, w_off, slabs, tm, normed_dtype, name):
    T, D = x.shape
    ntile = len(modes)
    N = ntile * tn
    n16 = T // NSLAB
    in_layout, out_layout = ("view", "slab") if slabs else ("nat", "nat")
    sp = tm // n16
    x_in = x.reshape(n16, NSLAB * D) if slabs else x
    x_spec = _tok_spec(in_layout, tm, n16, D, D, lambda n: 0)
    oshape = lambda c: (NSLAB, n16, c) if slabs else (T, c)
    ospec = lambda bc, cm: _tok_spec(out_layout, tm, n16, None, bc, cm)

    def body(x_ref, g_ref, w_ref, gains_ref, raw_ref, nrm_ref, h_ref):
        n = pl.program_id(1)

        @pl.when(n == 0)
        def _():
            xv = _tok_load(x_ref, in_layout, sp)
            r = lax.rsqrt(jnp.mean(xv * xv, axis=-1, keepdims=True) + RMS_EPS)
            _tok_store(h_ref, out_layout, sp, (xv * r * g_ref[...]).astype(BF16))

        y = _nt(_tok_load(h_ref, out_layout, sp), w_ref[...])
        _tok_store(raw_ref, out_layout, sp, y)
        for t, mode in enumerate(modes):
            @pl.when(n == t)
            def _(t=t, mode=mode):
                if not mode:
                    _tok_store(nrm_ref, out_layout, sp, y.astype(nrm_ref.dtype))
                    return
                gain = gains_ref[t]
                for k in range(tn // HEAD):
                    yk = y[:, k * HEAD:(k + 1) * HEAD]
                    r = lax.rsqrt(jnp.mean(yk * yk, axis=-1, keepdims=True) + RMS_EPS)
                    _tok_store(nrm_ref, out_layout, sp, (yk * r * gain).astype(nrm_ref.dtype), cols=(k * HEAD, HEAD))

    return pl.pallas_call(
        body, name=name, grid=(T // tm, ntile),
        in_specs=[x_spec, pl.BlockSpec((1, D), lambda i, n: (0, 0)),
                  pl.BlockSpec((tn, D), lambda i, n: (n + w_off, 0)),
                  pl.BlockSpec((ntile, 1, HEAD), lambda i, n: (0, 0, 0))],
        out_specs=[ospec(tn, lambda n: n), ospec(tn, lambda n: n), ospec(D, lambda n: 0)],
        out_shape=[_sds(oshape(N), F32), _sds(oshape(N), normed_dtype), _sds(oshape(D), BF16)],
        compiler_params=_cparams(),
    )(x_in, g, wt, gains)


def _mm(a, w, *, nt, tk, tm, a_layout, out_layout, resid=None, name, w_off=0, n_out=None):
    if a_layout == "slab":
        T, K = a.shape[0] * a.shape[1], a.shape[2]
    else:
        T, K = a.shape
    N = (w.shape[0] if nt else w.shape[1]) if n_out is None else n_out
    n16 = T // NSLAB
    nk = K // tk
    sp = tm // n16
    a_in = a.reshape(n16, NSLAB * K) if a_layout == "view" else a
    w_spec = (pl.BlockSpec((N, tk), lambda i, k: (w_off, k)) if nt
              else pl.BlockSpec((tk, N), lambda i, k: (k + w_off, 0)))
    o_spec = _tok_spec(out_layout, tm, n16, N, N, lambda k: 0)
    oshape = {"nat": (T, N), "slab": (NSLAB, n16, N), "view": (n16, NSLAB * N)}[out_layout]
    has_resid = resid is not None

    def body(*refs):
        a_ref, w_ref = refs[0], refs[1]
        o_ref = refs[-1]
        k = pl.program_id(1)

        @pl.when(k == 0)
        def _():
            o_ref[...] = refs[2][...] if has_resid else jnp.zeros_like(o_ref)

        ab = _tok_load(a_ref, a_layout, sp).astype(BF16)
        _tok_store(o_ref, out_layout, sp, _nt(ab, w_ref[...]) if nt else _nn(ab, w_ref[...]), accumulate=True)

    ins = [a_in, w]
    in_specs = [_tok_spec(a_layout, tm, n16, K, tk, lambda k: k), w_spec]
    if has_resid:
        ins.append(resid.reshape(n16, NSLAB * N) if out_layout == "view" else resid)
        in_specs.append(o_spec)
    out = pl.pallas_call(
        body, name=name, grid=(T // tm, nk), in_specs=in_specs, out_specs=o_spec,
        out_shape=_sds(oshape, F32), compiler_params=_cparams(),
    )(*ins)
    return out.reshape(T, N) if out_layout == "view" else out


def _log_sigmoid(z):
    return jnp.minimum(z, 0.0) - jnp.log(1.0 + jnp.exp(-jnp.abs(z)))


def _fox_gate_fwd(f_raw, fbias, name):
    T = f_raw.shape[0]
    cb = 256

    def body(f_ref, b_ref, c_ref):
        row = lax.broadcasted_iota(jnp.int32, (cb, cb), 0)
        col = lax.broadcasted_iota(jnp.int32, (cb, cb), 1)
        tri = (col <= row).astype(F32)
        carry = jnp.zeros((1, HEAD), F32)
        for i in range(T // cb):
            lf = _log_sigmoid(f_ref[i * cb:(i + 1) * cb, :] + b_ref[...])
            c = jnp.dot(tri, lf, preferred_element_type=F32, precision=lax.Precision.HIGHEST) + carry
            c_ref[i * cb:(i + 1) * cb, :] = c
            carry = c[cb - 1:cb, :]

    return pl.pallas_call(body, name=name, out_shape=_sds((T, HEAD), F32), compiler_params=_cparams())(f_raw, fbias)


def _fox_gate_bwd(f_raw, fbias, dc, name):
    T = f_raw.shape[0]
    cb = 256

    def body(f_ref, b_ref, dc_ref, df_ref, db_ref):
        row = lax.broadcasted_iota(jnp.int32, (cb, cb), 0)
        col = lax.broadcasted_iota(jnp.int32, (cb, cb), 1)
        tri = (col >= row).astype(F32)
        carry = jnp.zeros((1, HEAD), F32)
        dbias = jnp.zeros((1, HEAD), F32)
        for i in reversed(range(T // cb)):
            dlf = jnp.dot(tri, dc_ref[i * cb:(i + 1) * cb, :], preferred_element_type=F32,
                          precision=lax.Precision.HIGHEST) + carry
            carry = dlf[0:1, :]
            z = f_ref[i * cb:(i + 1) * cb, :] + b_ref[...]
            df = dlf * jax.nn.sigmoid(-z)
            df_ref[i * cb:(i + 1) * cb, :] = df
            dbias = dbias + jnp.sum(df, axis=0, keepdims=True)
        db_ref[...] = dbias

    return pl.pallas_call(body, name=name, out_shape=[_sds((T, HEAD), F32), _sds((1, HEAD), F32)],
                          compiler_params=_cparams())(f_raw, fbias, dc)


def _fox_fwd(qkv, c_col, c_row, tq, name, side=None):
    T = qkv.shape[0]
    H = qkv.shape[1] // (3 * HEAD)
    nq = T // tq
    c_blocks = c_row.reshape(H, nq, 1, tq)

    def body(q_ref, k_ref, v_ref, cq_ref, ck_ref, o_ref, lse_ref):
        qi = pl.program_id(1)
        q, cq = q_ref[...], cq_ref[...]
        causal = lax.broadcasted_iota(jnp.int32, (tq, tq), 1) <= lax.broadcasted_iota(jnp.int32, (tq, tq), 0)

        def key_block(ki, carry, diagonal):
            m, l, acc = carry
            rows = pl.ds(pl.multiple_of(ki * tq, tq), tq)
            s = _nt(q, k_ref[rows, :]) * SCALE + cq - ck_ref[ki]
            if diagonal:
                s = jnp.where(causal, s, NEG)
            m_new = jnp.maximum(m, jnp.max(s, axis=-1, keepdims=True))
            alpha = jnp.exp(m - m_new)
            p = jnp.exp(s - m_new)
            l = alpha * l + jnp.sum(p, axis=-1, keepdims=True)
            acc = alpha * acc + _nn(p.astype(BF16), v_ref[rows, :])
            return m_new, l, acc

        init = (jnp.full((tq, 1), NEG, F32), jnp.zeros((tq, 1), F32), jnp.zeros((tq, HEAD), F32))
        carry = lax.fori_loop(0, qi, lambda ki, c: key_block(ki, c, False), init)
        m, l, acc = key_block(qi, carry, True)
        o_ref[...] = acc / l
        lse_ref[...] = m + jnp.log(l)

    return _call(
        body, name=name, grid=(H, nq), side=side,
        in_specs=[pl.BlockSpec((tq, HEAD), lambda h, qi: (qi, h)),
                  pl.BlockSpec((T, HEAD), lambda h, qi: (0, H + h)),
                  pl.BlockSpec((T, HEAD), lambda h, qi: (0, 2 * H + h)),
                  pl.BlockSpec((None, tq, 1), lambda h, qi: (h, qi, 0)),
                  pl.BlockSpec((None, nq, 1, tq), lambda h, qi: (h, 0, 0, 0))],
        out_specs=[pl.BlockSpec((tq, HEAD), lambda h, qi: (qi, h)),
                   pl.BlockSpec((None, tq, 1), lambda h, qi: (h, qi, 0))],
        out_shape=[_sds((T, H * HEAD), F32), _sds((H, T, 1), F32)],
        args=(qkv, qkv, qkv, c_col, c_blocks))


def _fox_bwd(qkv, c_col, c_row, out, dout, lse, tq, name, side=None):
    T = qkv.shape[0]
    H = qkv.shape[1] // (3 * HEAD)
    nq = T // tq

    def body(q_ref, k_ref, v_ref, cq_ref, ck_ref, o_ref, do_ref, lse_ref, dq_ref, dk_ref, dv_ref, dck_ref, dcq_ref,
             delta_s):
        ki = pl.program_id(1)

        @pl.when(ki == 0)
        def _():
            dq_ref[...] = jnp.zeros_like(dq_ref)
            dcq_ref[...] = jnp.zeros_like(dcq_ref)
            delta_s[...] = jnp.sum(do_ref[...] * o_ref[...], axis=-1, keepdims=True)

        k, v, ck = k_ref[...], v_ref[...], ck_ref[...]
        causal = lax.broadcasted_iota(jnp.int32, (tq, tq), 1) <= lax.broadcasted_iota(jnp.int32, (tq, tq), 0)

        def query_block(qi, carry, diagonal):
            dk, dv, dck = carry
            rows = pl.ds(pl.multiple_of(qi * tq, tq), tq)
            q = q_ref[rows, :]
            s = _nt(q, k) * SCALE + cq_ref[rows, :] - ck
            if diagonal:
                s = jnp.where(causal, s, NEG)
            p = jnp.exp(s - lse_ref[rows, :])
            dob = do_ref[rows, :].astype(BF16)
            ds = p * (_nt(dob, v) - delta_s[rows, :])
            dsb = ds.astype(BF16)
            dq_ref[rows, :] += _nn(dsb, k) * SCALE
            dcq_ref[rows, :] += jnp.sum(ds, axis=-1, keepdims=True)
            return dk + _tn(dsb, q), dv + _tn(p.astype(BF16), dob), dck - jnp.sum(ds, axis=0, keepdims=True)

        init = (jnp.zeros((tq, HEAD), F32), jnp.zeros((tq, HEAD), F32), jnp.zeros((1, tq), F32))
        carry = query_block(ki, init, True)
        dk, dv, dck = lax.fori_loop(ki + 1, nq, lambda qi, c: query_block(qi, c, False), carry)
        dk_ref[...] = dk * SCALE
        dv_ref[...] = dv
        dck_ref[...] = dck

    head = lambda off: pl.BlockSpec((T, HEAD), lambda h, ki: (0, off + h))
    col = pl.BlockSpec((None, T, 1), lambda h, ki: (h, 0, 0))
    return _call(
        body, name=name, grid=(H, nq), side=side,
        in_specs=[head(0),
                  pl.BlockSpec((tq, HEAD), lambda h, ki: (ki, H + h)),
                  pl.BlockSpec((tq, HEAD), lambda h, ki: (ki, 2 * H + h)),
                  col, pl.BlockSpec((None, 1, tq), lambda h, ki: (h, 0, ki)), head(0), head(0), col],
        out_specs=[head(0),
                   pl.BlockSpec((tq, HEAD), lambda h, ki: (ki, h)),
                   pl.BlockSpec((tq, HEAD), lambda h, ki: (ki, h)),
                   pl.BlockSpec((None, 1, tq), lambda h, ki: (h, 0, ki)), col],
        out_shape=[_sds((T, H * HEAD), F32), _sds((T, H * HEAD), F32), _sds((T, H * HEAD), F32), _sds((H, 1, T), F32),
                   _sds((H, T, 1), F32)],
        scratch_shapes=[pltpu.VMEM((T, 1), F32)],
        args=(qkv, qkv, qkv, c_col, c_row, out, dout, lse))


def _t5_bucket(dist):
    max_exact = NUM_BUCKETS // 2
    d = dist.astype(np.float32)
    large = max_exact + (np.log(np.maximum(d, np.float32(1.0)) / np.float32(max_exact))
                         / np.float32(math.log(MAX_DISTANCE / max_exact))
                         * np.float32(NUM_BUCKETS - max_exact)).astype(np.int32)
    large = np.minimum(large, NUM_BUCKETS - 1)
    return np.where(dist < max_exact, dist, large)


def _bucket_maps():
    maps = []
    for d in DILATIONS:
        e = NSLAB // d
        rows = BLK // e
        idx = np.arange(BLK)
        pos = e * (idx % rows) + idx // rows
        qpos = pos[:, None] + BLK
        kpos = np.concatenate([pos, pos + BLK])[None, :]
        delta = qpos - kpos
        band = (delta >= 0) & (delta <= BLK)
        bucket = _t5_bucket(np.clip(delta, 0, None) * d)
        maps.append(np.where(band, bucket, -1).astype(np.int32))
    return np.stack(maps)


def _dil_geometry(T):
    n16 = T // NSLAB
    geo = []
    for d in DILATIONS:
        e = NSLAB // d
        rows = BLK // e
        nblk = n16 // rows
        geo.append((d, e, rows, nblk))
    return geo


DIL_INTERLEAVE_FWD = {1: 4, 4: 8, 16: 8}
DIL_INTERLEAVE_BWD = {1: 8, 4: 8, 16: 8}


def _dil_interleave(per_step, nblocks):
    while per_step > 1 and (nblocks % per_step or nblocks // per_step < 2):
        per_step -= 1
    return per_step


def _dil_bias(tab_ref, bkt_ref, bias_s, h):
    for p in range(len(DILATIONS)):
        bk = bkt_ref[p]
        bias = jnp.full((BLK, 2 * BLK), NEG, F32)
        for b in range(NUM_BUCKETS):
            bias = jnp.where(bk == b, tab_ref[b, h], bias)
        bias_s[p] = bias


def _dil_rows(d, e, rows, sub, blk):
    start = pl.multiple_of(blk * rows, rows)
    return [(sub + d * j, pl.ds(start, rows)) for j in range(e)]


def _gather(ref, idx):
    return jnp.concatenate([ref[s, r, :] for s, r in idx], axis=0)


def _scatter(ref, idx, val, rows):
    for j, (s, r) in enumerate(idx):
        ref[s, r, :] = val[j * rows:(j + 1) * rows]


def _scatter_add(ref, idx, val, rows):
    for j, (s, r) in enumerate(idx):
        ref[s, r, :] += val[j * rows:(j + 1) * rows]


def _dil_fwd(qkv, table, name, side=None):
    n16 = qkv.shape[1]
    T = NSLAB * n16
    H = qkv.shape[2] // (3 * HEAD)
    geo = _dil_geometry(T)
    bkt = jnp.asarray(_bucket_maps())

    def body(tab_ref, bkt_ref, q_ref, k_ref, v_ref, o_ref, lse_ref, bias_s, m_s, l_s):
        h = pl.program_id(0)
        _dil_bias(tab_ref, bkt_ref, bias_s, h)
        first_mask = lax.broadcasted_iota(jnp.int32, (BLK, 2 * BLK), 1) < BLK

        starts = len(DILATIONS) - 1

        def load(p, d, e, rows, sub, blk):
            cur = _dil_rows(d, e, rows, sub, blk)
            prev = _dil_rows(d, e, rows, sub, jnp.maximum(blk - 1, 0))
            q = _gather(q_ref, cur).astype(BF16)
            kk = jnp.concatenate([_gather(k_ref, prev), _gather(k_ref, cur)], axis=0).astype(BF16)
            vv = jnp.concatenate([_gather(v_ref, prev), _gather(v_ref, cur)], axis=0).astype(BF16)
            old = None if p == starts else (_gather(m_s, cur), _gather(l_s, cur), _gather(o_ref, cur))
            return cur, blk, q, kk, vv, old

        def compute(p, blk, q, kk, vv, old):
            s = _nt(q, kk) * SCALE + bias_s[p]
            s = jnp.where(first_mask & (blk == 0), NEG, s)
            m_blk = jnp.max(s, axis=-1, keepdims=True)
            if old is None:
                m_new = m_blk
                pr = jnp.exp(s - m_new)
                l_new = jnp.sum(pr, axis=-1, keepdims=True)
                acc = _nn(pr.astype(BF16), vv)
            else:
                m_old, l_old, acc_old = old
                m_new = jnp.maximum(m_old, m_blk)
                alpha = jnp.exp(m_old - m_new)
                pr = jnp.exp(s - m_new)
                l_new = alpha * l_old + jnp.sum(pr, axis=-1, keepdims=True)
                acc = alpha * acc_old + _nn(pr.astype(BF16), vv)
            if p == 0:
                return acc / l_new, m_new + jnp.log(l_new), None
            return acc, m_new, l_new

        def store(p, rows, cur, acc, m_new, l_new):
            _scatter(o_ref, cur, acc, rows)
            if p == 0:
                _scatter(lse_ref, cur, m_new, rows)
            else:
                _scatter(m_s, cur, m_new, rows)
                _scatter(l_s, cur, l_new, rows)

        for p in reversed(range(len(DILATIONS))):
            d, e, rows, nblk = geo[p]
            per_step = _dil_interleave(DIL_INTERLEAVE_FWD[d], d * nblk)

            def step(i, carry, p=p, d=d, e=e, rows=rows, nblk=nblk, per_step=per_step):
                ids = [i + u * (d * nblk // per_step) for u in range(per_step)]
                loaded = [load(p, d, e, rows, j // nblk, j % nblk) for j in ids]
                done = [(cur, compute(p, blk, q, kk, vv, old)) for cur, blk, q, kk, vv, old in loaded]
                for cur, res in done:
                    store(p, rows, cur, *res)
                return carry

            lax.fori_loop(0, d * nblk // per_step, step, 0)

    head = lambda off: pl.BlockSpec((NSLAB, n16, HEAD), lambda h: (0, 0, off + h))
    return _call(
        body, name=name, grid=(H,), side=side,
        in_specs=[pl.BlockSpec(memory_space=pltpu.SMEM), pl.BlockSpec((3, BLK, 2 * BLK), lambda h: (0, 0, 0)),
                  head(0), head(H), head(2 * H)],
        out_specs=[head(0), pl.BlockSpec((None, NSLAB, n16, 1), lambda h: (h, 0, 0, 0))],
        out_shape=[_sds((NSLAB, n16, H * HEAD), F32), _sds((H, NSLAB, n16, 1), F32)],
        scratch_shapes=[pltpu.VMEM((3, BLK, 2 * BLK), F32), pltpu.VMEM((NSLAB, n16, 1), F32),
                        pltpu.VMEM((NSLAB, n16, 1), F32)],
        args=(table, bkt, qkv, qkv, qkv))


def _dil_bwd(qkv, table, out, dout, lse, name, side=None):
    n16 = qkv.shape[1]
    T = NSLAB * n16
    H = qkv.shape[2] // (3 * HEAD)
    geo = _dil_geometry(T)
    bkt = jnp.asarray(_bucket_maps())

    def body(tab_ref, bkt_ref, q_ref, k_ref, v_ref, o_ref, do_ref, lse_ref,
             dq_ref, dk_ref, dv_ref, dtab_ref, bias_s, dbias_s, delta_s):
        h = pl.program_id(0)
        _dil_bias(tab_ref, bkt_ref, bias_s, h)
        first_mask = lax.broadcasted_iota(jnp.int32, (BLK, 2 * BLK), 1) < BLK
        dbias_s[...] = jnp.zeros_like(dbias_s)
        dq_ref[...] = jnp.zeros_like(dq_ref)
        dk_ref[...] = jnp.zeros_like(dk_ref)
        dv_ref[...] = jnp.zeros_like(dv_ref)
        for r in range(NSLAB):
            delta_s[r] = jnp.sum(do_ref[r] * o_ref[r], axis=-1, keepdims=True)

        def load(d, e, rows, sub, blk):
            cur = _dil_rows(d, e, rows, sub, blk)
            prev = _dil_rows(d, e, rows, sub, jnp.maximum(blk - 1, 0))
            q = _gather(q_ref, cur).astype(BF16)
            kk = jnp.concatenate([_gather(k_ref, prev), _gather(k_ref, cur)], axis=0).astype(BF16)
            vv = jnp.concatenate([_gather(v_ref, prev), _gather(v_ref, cur)], axis=0).astype(BF16)
            dob = _gather(do_ref, cur).astype(BF16)
            return cur, prev, blk, q, kk, vv, dob, _gather(lse_ref, cur), _gather(delta_s, cur)

        def compute(p, blk, q, kk, vv, dob, lse, delta):
            s = _nt(q, kk) * SCALE + bias_s[p]
            s = jnp.where(first_mask & (blk == 0), NEG, s)
            pr = jnp.exp(s - lse)
            ds = pr * (_nt(dob, vv) - delta)
            dsb = ds.astype(BF16)
            return ds, _nn(dsb, kk) * SCALE, _tn(dsb, q) * SCALE, _tn(pr.astype(BF16), dob)

        def store(rows, cur, prev, dq, dkk, dvv):
            _scatter_add(dq_ref, cur, dq, rows)
            _scatter_add(dk_ref, prev, dkk[:BLK], rows)
            _scatter_add(dk_ref, cur, dkk[BLK:], rows)
            _scatter_add(dv_ref, prev, dvv[:BLK], rows)
            _scatter_add(dv_ref, cur, dvv[BLK:], rows)

        for p in range(len(DILATIONS)):
            d, e, rows, nblk = geo[p]
            per_step = _dil_interleave(DIL_INTERLEAVE_BWD[d], d * nblk)

            def step(i, carry, p=p, d=d, e=e, rows=rows, nblk=nblk, per_step=per_step):
                ids = [i + u * (d * nblk // per_step) for u in range(per_step)]
                loaded = [load(d, e, rows, j // nblk, j % nblk) for j in ids]
                done = [(cur, prev, compute(p, *rest)) for cur, prev, *rest in loaded]
                dbias_s[p] += functools.reduce(jnp.add, [res[0] for _, _, res in done])
                for cur, prev, res in done:
                    store(rows, cur, prev, *res[1:])
                return carry

            lax.fori_loop(0, d * nblk // per_step, step, 0)

        lane = lax.broadcasted_iota(jnp.int32, (1, HEAD), 1)
        row = jnp.zeros((1, HEAD), F32)
        for b in range(NUM_BUCKETS):
            tot = jnp.zeros((1, 1), F32)
            for p in range(len(DILATIONS)):
                hit = jnp.where(bkt_ref[p] == b, dbias_s[p], 0.0)
                tot = tot + jnp.sum(jnp.sum(hit, axis=0, keepdims=True), axis=1, keepdims=True)
            row = jnp.where(lane == b, tot, row)
        dtab_ref[...] = row

    head = lambda off: pl.BlockSpec((NSLAB, n16, HEAD), lambda h: (0, 0, off + h))
    return _call(
        body, name=name, grid=(H,), side=side,
        in_specs=[pl.BlockSpec(memory_space=pltpu.SMEM), pl.BlockSpec((3, BLK, 2 * BLK), lambda h: (0, 0, 0)),
                  head(0), head(H), head(2 * H), head(0), head(0),
                  pl.BlockSpec((None, NSLAB, n16, 1), lambda h: (h, 0, 0, 0))],
        out_specs=[head(0), head(0), head(0), pl.BlockSpec((None, 1, HEAD), lambda h: (h, 0, 0))],
        out_shape=[_sds((NSLAB, n16, H * HEAD), F32)] * 3 + [_sds((H, 1, HEAD), F32)],
        scratch_shapes=[pltpu.VMEM((3, BLK, 2 * BLK), F32), pltpu.VMEM((3, BLK, 2 * BLK), F32),
                        pltpu.VMEM((NSLAB, n16, 1), F32)],
        args=(table, bkt, qkv, qkv, qkv, out, dout, lse))


def _qknorm_bwd(raw, dq, dk, dv, gains, tm, name):
    T, N = raw.shape
    C = N // 3

    def body(raw_ref, dq_ref, dk_ref, dv_ref, gains_ref, dp_ref, dg_ref):
        @pl.when(pl.program_id(0) == 0)
        def _():
            dg_ref[...] = jnp.zeros_like(dg_ref)

        for t, d_ref in enumerate((dq_ref, dk_ref)):
            gain = gains_ref[t]
            dgain = jnp.zeros((1, HEAD), F32)
            for k in range(C // HEAD):
                y = raw_ref[:, t * C + k * HEAD:t * C + (k + 1) * HEAD]
                dn = d_ref[:, k * HEAD:(k + 1) * HEAD]
                r = lax.rsqrt(jnp.mean(y * y, axis=-1, keepdims=True) + RMS_EPS)
                yhat = y * r
                gd = dn * gain
                dy = r * (gd - yhat * jnp.mean(gd * yhat, axis=-1, keepdims=True))
                dp_ref[:, t * C + k * HEAD:t * C + (k + 1) * HEAD] = dy.astype(BF16)
                dgain = dgain + jnp.sum(dn * yhat, axis=0, keepdims=True)
            dg_ref[t] += dgain
        dp_ref[:, 2 * C:] = dv_ref[...].astype(BF16)

    third = pl.BlockSpec((tm, C), lambda i: (i, 0))
    return pl.pallas_call(
        body, name=name, grid=(T // tm,),
        in_specs=[pl.BlockSpec((tm, N), lambda i: (i, 0)), third, third, third,
                  pl.BlockSpec((2, 1, HEAD), lambda i: (0, 0, 0))],
        out_specs=[pl.BlockSpec((tm, N), lambda i: (i, 0)), pl.BlockSpec((2, 1, HEAD), lambda i: (0, 0, 0))],
        out_shape=[_sds((T, N), BF16), _sds((2, 1, HEAD), F32)], compiler_params=_cparams(),
    )(raw, dq, dk, dv, gains)


def _pad_lanes(v, width=HEAD):
    return jnp.pad(v, ((0, 0), (0, width - v.shape[1])))


def _local_step(x, target, small, wts, plan=None):
    grads = {}

    def hosted(host, fn, *args, **kw):
        side = plan.before(host, wts, grads) if plan is not None else None
        if side is None:
            return fn(*args, name=host, **kw)
        res, side_res = fn(*args, name=host, side=side, **kw)
        plan.after(host, side_res, wts, grads)
        return res

    T, D = x.shape
    C = D // 2
    H = C // HEAD
    n16 = T // NSLAB
    tm = min(512, T)
    tmm = min(1024, T)
    tms = 4 * n16
    tq = min(512, T)
    bn = min(1024, D)
    g1, gm, g2 = small["ffn1_norm"], small["mix_norm"], small["ffn2_norm"]
    gains_a = jnp.stack([small["q_norm_a"], small["k_norm_a"], jnp.ones_like(small["q_norm_a"])])
    gains_b = jnp.stack([small["q_norm_b"], small["k_norm_b"], jnp.ones_like(small["q_norm_b"])])
    fbias = _pad_lanes(small["forget_bias"])
    table = small["rel_bias_table"]

    h1, gu1, act1 = hosted("ffn1_up", _ffn_fwd, x, g1, wts["ffn1_in"], None, tmm)
    x1 = hosted("ffn1_down", _ffn_down, x, act1, wts["ffn1_out"], tmm)
    w_in_t, w_f_t, w_o = wts["w_in_t"], wts["w_f_t"], wts["w_o"]
    raw_a, nrm_a, h2a = _proj(x1, gm, w_in_t, gains_a, (True, True, False), tn=C, w_off=0, slabs=True, tm=tms,
                              normed_dtype=F32, name="proj_a")
    raw_b, nrm_b, h2b = _proj(x1, gm, w_in_t, gains_b, (True, True, False), tn=C, w_off=3, slabs=False, tm=tmm,
                              normed_dtype=BF16, name="proj_b")
    f_raw, _, _ = _proj(x1, gm, w_f_t, gains_b[:1], (False,), tn=HEAD, w_off=0, slabs=False, tm=tm,
                        normed_dtype=BF16, name="proj_f")
    c = _fox_gate_fwd(f_raw, fbias, "fox_gate_fwd")
    c_heads = c[:, :H].T
    c_col, c_row = c_heads[:, :, None], c_heads[:, None, :]
    out_a, lse_a = hosted("dil_fwd", _dil_fwd, nrm_a, table)
    out_b, lse_b = hosted("fox_fwd", _fox_fwd, nrm_b, c_col, c_row, tq)
    x2a = _mm(out_a, w_o, nt=False, tk=C, tm=tms, a_layout="slab", out_layout="view", resid=x1, name="out_a")
    x2 = _mm(out_b, w_o, nt=False, tk=C, tm=tmm, a_layout="nat", out_layout="nat", resid=x2a, w_off=1, name="out_b")
    dy, h3, gu3, act3, loss_row = _ffn_fwd(x2, g2, wts["ffn2_in"], wts["ffn2_out"], tm, "ffn2_fwd", target=target)

    def ffn_backward(tag, xin, g, h, gu, act, win, wout, dres):
        nc, tf = wout.shape[0], wout.shape[1]
        dh, dgu, dyb = hosted(tag + "_bwd", _ffn_bwd, dres, gu, win, wout, tm)
        grads[tag + "_w_in_t"], grads[tag + "_w_in_t_bf16"] = hosted(
            tag + "_dwin", _mm_tn, dgu.reshape(2 * nc, tf, T), h, bm=tf, bn=bn, bt=T, a_rows=True, twin=True)
        dxin, grads[tag + "_norm"] = hosted(tag + "_rms_bwd", _rms_bwd, xin, g, dh, dres, tm)
        grads[tag + "_w_out"], grads[tag + "_w_out_bf16"] = hosted(
            tag + "_dwout", _mm_tn, act, dyb, bm=tf, bn=bn, bt=T, a_rows=True, twin=True)
        return dxin

    dx2 = ffn_backward("ffn2", x2, g2, h3, gu3, act3, wts["ffn2_in"], wts["ffn2_out"], dy)

    dmix_a = _mm(dx2, w_o, nt=True, tk=D, tm=tms, a_layout="view", out_layout="slab", n_out=C, name="dmix_a")
    dmix_b = _mm(dx2, w_o, nt=True, tk=D, tm=tmm, a_layout="nat", out_layout="nat", n_out=C, w_off=1, name="dmix_b")
    dwo = _mm_tn(out_a.reshape(1, T, C), dx2.reshape(n16, NSLAB * D), bm=C, bn=bn, bt=n16, b_slabs=True,
                 rows=2 * C, name="dwo_a")
    dwo = _mm_tn(out_b.reshape(1, T, C), dx2, bm=C, bn=bn, bt=tm, rows=2 * C, m_off=1, into=dwo, name="dwo_b")
    grads["w_out"] = dwo[0]

    dqa, dka, dva, dtab = hosted("dil_bwd", _dil_bwd, nrm_a, table, out_a, dmix_a, lse_a)
    dqb, dkb, dvb, dck, dcq = hosted("fox_bwd", _fox_bwd, nrm_b, c_col, c_row, out_b, dmix_b, lse_b, tq)
    grads["rel_bias_table"] = dtab[:, 0, :NUM_BUCKETS].T
    dc = _pad_lanes((dck[:, 0, :] + dcq[:, :, 0]).T)
    df, dfb = _fox_gate_bwd(f_raw, fbias, dc, "fox_gate_bwd")
    grads["forget_bias"] = dfb[:, :H]

    flat = lambda a: a.reshape(T, a.shape[-1])
    dproj_a, dgain_a = _qknorm_bwd(flat(raw_a), flat(dqa), flat(dka), flat(dva), gains_a[:2], min(256, T), "qknorm_bwd_a")
    dproj_b, dgain_b = _qknorm_bwd(raw_b, dqb, dkb, dvb, gains_b[:2], min(256, T), "qknorm_bwd_b")
    grads["q_norm_a"], grads["k_norm_a"] = dgain_a[0], dgain_a[1]
    grads["q_norm_b"], grads["k_norm_b"] = dgain_b[0], dgain_b[1]
    dproj_a = dproj_a.reshape(NSLAB, n16, 3 * C)

    dh2 = _mm(dproj_a, w_in_t, nt=False, tk=C, tm=tms, a_layout="slab", out_layout="view", name="dh2_a")
    dh2 = _mm(dproj_b, w_in_t, nt=False, tk=C, tm=tmm, a_layout="nat", out_layout="nat", resid=dh2, w_off=3, name="dh2_b")
    dh2 = _mm(df, w_f_t, nt=False, tk=HEAD, tm=tmm, a_layout="nat", out_layout="nat", resid=dh2, name="dh2_f")
    dx1, grads["mix_norm"] = _rms_bwd(x1, gm, dh2, dx2, tm, "mix_rms_bwd")
    bt = min(2048, T)
    dwt = _mm_tn(flat(dproj_a)[None], flat(h2a), bm=C, bn=bn, bt=bt, rows=6 * C + H, name="dw_a")
    dwt = _mm_tn(dproj_b[None], h2b, bm=C, bn=bn, bt=bt, rows=6 * C + H, m_off=3, into=dwt, name="dw_b")
    dwt = _mm_tn(df[None, :, :H], h2b, bm=H, bn=bn, bt=bt, rows=6 * C + H, m_off=6 * C // H, into=dwt, name="dw_f")
    grads["w_in_t"] = dwt[0]

    grad_x = ffn_backward("ffn1", x, g1, h1, gu1, act1, wts["ffn1_in"], wts["ffn1_out"], dx1)
    return loss_row, grad_x, grads


def _place():
    x, y, c = lax.axis_index("x"), lax.axis_index("y"), lax.axis_index("c")
    other_chips = [(1 - x, y), (x, 1 - y), (1 - x, 1 - y)]
    return x, y, c, other_chips


def _run_side(side, name):
    def body(*refs):
        si, so = len(side.ins), len(side.outs)
        side.start(refs[:si], refs[si:si + so], refs[si + so:])
        side.finish(refs[:si], refs[si:si + so], refs[si + so:])

    return pl.pallas_call(body, name=name, in_specs=[ANY] * len(side.ins), out_specs=[ANY] * len(side.outs),
                          out_shape=side.outs, scratch_shapes=side.sems)(*side.ins)


def _all_gather(shards):
    n = len(shards)

    def plan(ins, outs, sems):
        send_sems, recv_sems, local_sems = sems
        x, y, c, chips = _place()
        me, sibling = (x, y, c), (x, y, 1 - c)

        def copy(a, k, block, to, src=None):
            px, py, pc = block
            dst = outs[a].at[4 * px + 2 * py + pc]
            return pltpu.make_async_remote_copy(
                src_ref=dst if src is None else src, dst_ref=dst, send_sem=send_sems.at[7 * a + k],
                recv_sem=recv_sems.at[7 * a + k], device_id=to, device_id_type=MESH)

        mine = [pltpu.make_async_copy(ins[a], outs[a].at[4 * x + 2 * y + c], local_sems.at[a]) for a in range(n)]
        first = []
        for a in range(n):
            first.append(copy(a, 0, me, sibling, src=ins[a]))
            first += [copy(a, 1 + j, me, (*chip, c), src=ins[a]) for j, chip in enumerate(chips)]
        return copy, mine, first, me, sibling, c, chips

    def start(ins, outs, sems):
        _, mine, first, *_ = plan(ins, outs, sems)
        for cp in mine + first:
            cp.start()

    def finish(ins, outs, sems):
        copy, mine, first, me, sibling, c, chips = plan(ins, outs, sems)
        passed = []
        for a in range(n):
            for j, chip in enumerate(chips):
                copy(a, 1 + j, (*chip, c), me).wait_recv()
                fwd = copy(a, 4 + j, (*chip, c), sibling)
                fwd.start()
                passed.append(fwd)
        for a in range(n):
            copy(a, 0, sibling, me).wait_recv()
            for j, chip in enumerate(chips):
                copy(a, 4 + j, (*chip, 1 - c), me).wait_recv()
        for cp in first + passed:
            cp.wait_send()
        for cp in mine:
            cp.wait()

    return _Side(shards, [_sds((N_DEV,) + s.shape, s.dtype) for s in shards],
                 [pltpu.SemaphoreType.DMA((7 * n,)), pltpu.SemaphoreType.DMA((7 * n,)), pltpu.SemaphoreType.DMA((n,))],
                 start, finish)


def _all_gather_relayed(shards):
    n = len(shards)
    halves = [-(-(s.shape[0] // 2) // 16) * 16 for s in shards]

    def body_parts(ins, outs, sems):
        send_sems, recv_sems, local_sems = sems
        x, y, c, _ = _place()
        me, sib, xn, yn, dg = (x, y, c), (x, y, 1 - c), (1 - x, y, c), (x, 1 - y, c), (1 - x, 1 - y, c)

        def rows(a, block, part):
            px, py, pc = block
            whole = outs[a].at[4 * px + 2 * py + pc]
            if part is None:
                return whole
            return whole.at[pl.ds(0, halves[a])] if part == 0 else whole.at[pl.ds(halves[a], shards[a].shape[0] - halves[a])]

        def copy(a, k, block, part, to, src=None):
            dst = rows(a, block, part)
            return pltpu.make_async_remote_copy(
                src_ref=dst if src is None else src, dst_ref=dst, send_sem=send_sems.at[9 * a + k],
                recv_sem=recv_sems.at[9 * a + k], device_id=to, device_id_type=MESH)

        flip = lambda dev: (dev[0], dev[1], 1 - dev[2])
        mine = [pltpu.make_async_copy(ins[a], rows(a, me, None), local_sems.at[a]) for a in range(n)]
        own = [[copy(a, 0, me, None, sib, src=ins[a]), copy(a, 1, me, None, xn, src=ins[a]),
                copy(a, 2, me, None, yn, src=ins[a])] for a in range(n)]
        relays = lambda a: [(1, [copy(a, 3, xn, 0, yn), copy(a, 5, xn, None, sib)]),
                            (2, [copy(a, 4, yn, 1, xn), copy(a, 6, yn, None, sib)]),
                            (3, [copy(a, 7, dg, 0, sib)]), (4, [copy(a, 8, dg, 1, sib)])]
        lands = {0: (sib, None), 1: (xn, None), 2: (yn, None), 3: (dg, 0), 4: (dg, 1), 5: (flip(xn), None),
                 6: (flip(yn), None), 7: (flip(dg), 0), 8: (flip(dg), 1)}
        arrival = lambda a, k: copy(a, k, lands[k][0], lands[k][1], me)
        return mine, own, relays, arrival

    def start(ins, outs, sems):
        mine, own, _, _ = body_parts(ins, outs, sems)
        for cp in mine + [cp for per in own for cp in per]:
            cp.start()

    def finish(ins, outs, sems):
        mine, own, relays, arrival = body_parts(ins, outs, sems)
        sent = [cp for per in own for cp in per]
        relays = [relays(a) for a in range(n)]
        for stage in range(4):
            for a in range(n):
                after, passes = relays[a][stage]
                arrival(a, after).wait_recv()
                for cp in passes:
                    cp.start()
                sent += passes
        for a in range(n):
            for k in (0, 5, 6, 7, 8):
                arrival(a, k).wait_recv()
        for cp in sent:
            cp.wait_send()
        for cp in mine:
            cp.wait()

    return _Side(shards, [_sds((N_DEV,) + s.shape, s.dtype) for s in shards],
                 [pltpu.SemaphoreType.DMA((9 * n,)), pltpu.SemaphoreType.DMA((9 * n,)), pltpu.SemaphoreType.DMA((n,))],
                 start, finish)


def _exchange_in_chip(gs):
    n = len(gs)

    def copies(ins, outs, sems):
        x, y, c, _ = _place()
        return [pltpu.make_async_remote_copy(
            src_ref=ins[a].at[2 * q + 1 - c], dst_ref=outs[a].at[q], send_sem=sems[0].at[4 * a + q],
            recv_sem=sems[1].at[4 * a + q], device_id=(x, y, 1 - c), device_id_type=MESH)
            for a in range(n) for q in range(4)]

    def start(ins, outs, sems):
        for cp in copies(ins, outs, sems):
            cp.start()

    def finish(ins, outs, sems):
        for cp in copies(ins, outs, sems):
            cp.wait()

    return _Side(gs, [_sds((4,) + g.shape[1:], g.dtype) for g in gs],
                 [pltpu.SemaphoreType.DMA((4 * n,)), pltpu.SemaphoreType.DMA((4 * n,))], start, finish)


def _exchange_between_chips(ps):
    n = len(ps)

    def copies(ins, outs, sems):
        x, y, c, chips = _place()
        return [pltpu.make_async_remote_copy(
            src_ref=ins[a].at[2 * cx + cy], dst_ref=outs[a].at[j], send_sem=sems[0].at[3 * a + j],
            recv_sem=sems[1].at[3 * a + j], device_id=(cx, cy, c), device_id_type=MESH)
            for a in range(n) for j, (cx, cy) in enumerate(chips)]

    def start(ins, outs, sems):
        for cp in copies(ins, outs, sems):
            cp.start()

    def finish(ins, outs, sems):
        for cp in copies(ins, outs, sems):
            cp.wait()

    return _Side(ps, [_sds((3,) + p.shape[1:], p.dtype) for p in ps],
                 [pltpu.SemaphoreType.DMA((3 * n,)), pltpu.SemaphoreType.DMA((3 * n,))], start, finish)


def _all_reduce_small(v, name):
    R = v.shape[0]

    def body(v_ref, sum_ref, all_ref, send_sems, recv_sems):
        x, y, c, _ = _place()
        k = 4 * x + 2 * y + c
        all_ref[k] = v_ref[...]
        copies = []
        for rel in range(1, N_DEV):
            fx, fy, fc = (rel >> 2) & 1, (rel >> 1) & 1, rel & 1
            peer = (1 - x if fx else x, 1 - y if fy else y, 1 - c if fc else c)
            copies.append(pltpu.make_async_remote_copy(
                src_ref=v_ref, dst_ref=all_ref.at[k], send_sem=send_sems.at[rel - 1], recv_sem=recv_sems.at[rel - 1],
                device_id=peer, device_id_type=MESH))
        for cp in copies:
            cp.start()
        for rel in range(1, N_DEV):
            fx, fy, fc = (rel >> 2) & 1, (rel >> 1) & 1, rel & 1
            src = 4 * (1 - x if fx else x) + 2 * (1 - y if fy else y) + (1 - c if fc else c)
            pltpu.make_async_remote_copy(
                src_ref=v_ref, dst_ref=all_ref.at[src], send_sem=send_sems.at[rel - 1], recv_sem=recv_sems.at[rel - 1],
                device_id=(x, y, c), device_id_type=MESH).wait_recv()
        for cp in copies:
            cp.wait_send()
        tot = all_ref[0]
        for d in range(1, N_DEV):
            tot = tot + all_ref[d]
        sum_ref[...] = tot

    vm = pl.BlockSpec(memory_space=pltpu.VMEM)
    return pl.pallas_call(
        body, name=name, in_specs=[vm], out_specs=[vm, vm],
        out_shape=[_sds((R, HEAD), F32), _sds((N_DEV, R, HEAD), F32)],
        scratch_shapes=[pltpu.SemaphoreType.DMA((N_DEV - 1,)), pltpu.SemaphoreType.DMA((N_DEV - 1,))],
    )(v)[0]


def _tiles(rows, cols):
    tr = next((cand for cand in (688, 512, 256) if rows % cand == 0), rows)
    tc = 512 if (cols % 512 == 0 and tr * cols * 4 > (2 << 20)) else cols
    return tr, tc


def _chip_sum(g, r1, core, name):
    _, R, Cc = g.shape
    tr, tc = _tiles(R, Cc)

    def body(core_ref, g_ref, r_ref, p_ref):
        p_ref[...] = (g_ref[...] + r_ref[...]).astype(BF16)

    blk = lambda f: pl.BlockSpec((None, tr, tc), f)
    return pl.pallas_call(
        body, name=name,
        grid_spec=pltpu.PrefetchScalarGridSpec(
            num_scalar_prefetch=1, grid=(4, R // tr, Cc // tc),
            in_specs=[blk(lambda q, i, j, core: (2 * q + core[0], i, j)), blk(lambda q, i, j, core: (q, i, j))],
            out_specs=blk(lambda q, i, j, core: (q, i, j))),
        out_shape=_sds((4, R, Cc), BF16), compiler_params=_cparams(),
    )(core, g, r1)


def _adamw_update(gv, w_ref, m_ref, v_ref, d_ref, nm_ref, nv_ref):
    nm = B1 * m_ref[...] + (1.0 - B1) * gv
    nv = B2 * v_ref[...] + (1.0 - B2) * jnp.square(gv)
    m_hat = nm / (1.0 - B1 ** STEP)
    v_hat = nv / (1.0 - B2 ** STEP)
    d_ref[...] = -LR * (m_hat / (jnp.sqrt(v_hat) + EPS) + WD * w_ref[...])
    nm_ref[...] = nm
    nv_ref[...] = nv


def _reduce_adamw(g, r1, r2, where, w, m, v, name):
    _, R, Cc = g.shape
    tr, tc = _tiles(R, Cc)

    def body(where_ref, g_ref, r1_ref, r2_ref, w_ref, m_ref, v_ref, o_ref, d_ref, nm_ref, nv_ref):
        gv = ((g_ref[...] + r1_ref[...]) + r2_ref[0].astype(F32)) + (r2_ref[1].astype(F32) + r2_ref[2].astype(F32))
        o_ref[...] = gv
        _adamw_update(gv, w_ref, m_ref, v_ref, d_ref, nm_ref, nv_ref)

    blk = pl.BlockSpec((tr, tc), lambda i, j, w: (i, j))
    return pl.pallas_call(
        body, name=name,
        grid_spec=pltpu.PrefetchScalarGridSpec(
            num_scalar_prefetch=1, grid=(R // tr, Cc // tc),
            in_specs=[pl.BlockSpec((None, tr, tc), lambda i, j, w: (w[0], i, j)),
                      pl.BlockSpec((None, tr, tc), lambda i, j, w: (w[1], i, j)),
                      pl.BlockSpec((3, tr, tc), lambda i, j, w: (0, i, j)), blk, blk, blk],
            out_specs=[blk] * 4),
        out_shape=[_sds((R, Cc), F32)] * 4, compiler_params=_cparams(),
    )(where, g, r1, r2, w, m, v)


def _adamw(w, g, m, v, name):
    R, Cc = w.shape
    tr, tc = _tiles(R, Cc)

    def body(w_ref, g_ref, m_ref, v_ref, d_ref, nm_ref, nv_ref):
        _adamw_update(g_ref[...], w_ref, m_ref, v_ref, d_ref, nm_ref, nv_ref)

    blk = pl.BlockSpec((tr, tc), lambda i, j: (i, j))
    return pl.pallas_call(
        body, name=name, grid=(R // tr, Cc // tc), in_specs=[blk] * 4, out_specs=[blk] * 3,
        out_shape=[_sds((R, Cc), F32)] * 3, compiler_params=_cparams(),
    )(w, g, m, v)


SMALL = ("ffn1_norm", "mix_norm", "ffn2_norm", "q_norm_a", "k_norm_a", "q_norm_b", "k_norm_b", "forget_bias",
         "rel_bias_table")
LARGE = ("ffn1_w_in", "ffn1_w_out", "w_in", "w_out", "ffn2_w_in", "ffn2_w_out")
ORDER = ("ffn1_norm", "ffn1_w_in", "ffn1_w_out", "mix_norm", "w_in", "q_norm_a", "k_norm_a", "q_norm_b", "k_norm_b",
         "forget_bias", "rel_bias_table", "w_out", "ffn2_norm", "ffn2_w_in", "ffn2_w_out")


def _pack_small(vals):
    rows = []
    for name in SMALL:
        flat = vals[name].reshape(-1)
        pad = (-flat.shape[0]) % HEAD
        rows.append(jnp.pad(flat, (0, pad)).reshape(-1, HEAD))
    return jnp.concatenate(rows, axis=0)


def _unpack_small(packed, like):
    out, r = {}, 0
    for name in SMALL:
        size = like[name].size
        nrow = -(-size // HEAD)
        out[name] = packed[r:r + nrow].reshape(-1)[:size].reshape(like[name].shape)
        r += nrow
    return out


def kernel(x, ffn1_norm, ffn1_w_in, ffn1_w_out, mix_norm, w_in, q_norm_a, k_norm_a, q_norm_b, k_norm_b, forget_bias, rel_bias_table, w_out, ffn2_norm, ffn2_w_in, ffn2_w_out, loss_target, m_ffn1_norm, m_ffn1_w_in, m_ffn1_w_out, m_mix_norm, m_w_in, m_q_norm_a, m_k_norm_a, m_q_norm_b, m_k_norm_b, m_forget_bias, m_rel_bias_table, m_w_out, m_ffn2_norm, m_ffn2_w_in, m_ffn2_w_out, v_ffn1_norm, v_ffn1_w_in, v_ffn1_w_out, v_mix_norm, v_w_in, v_q_norm_a, v_k_norm_a, v_q_norm_b, v_k_norm_b, v_forget_bias, v_rel_bias_table, v_w_out, v_ffn2_norm, v_ffn2_w_in, v_ffn2_w_out):
    w = dict(ffn1_norm=ffn1_norm, ffn1_w_in=ffn1_w_in, ffn1_w_out=ffn1_w_out, mix_norm=mix_norm, w_in=w_in,
             q_norm_a=q_norm_a, k_norm_a=k_norm_a, q_norm_b=q_norm_b, k_norm_b=k_norm_b, forget_bias=forget_bias,
             rel_bias_table=rel_bias_table, w_out=w_out, ffn2_norm=ffn2_norm, ffn2_w_in=ffn2_w_in, ffn2_w_out=ffn2_w_out)
    m = dict(ffn1_norm=m_ffn1_norm, ffn1_w_in=m_ffn1_w_in, ffn1_w_out=m_ffn1_w_out, mix_norm=m_mix_norm, w_in=m_w_in,
             q_norm_a=m_q_norm_a, k_norm_a=m_k_norm_a, q_norm_b=m_q_norm_b, k_norm_b=m_k_norm_b,
             forget_bias=m_forget_bias, rel_bias_table=m_rel_bias_table, w_out=m_w_out, ffn2_norm=m_ffn2_norm,
             ffn2_w_in=m_ffn2_w_in, ffn2_w_out=m_ffn2_w_out)
    v = dict(ffn1_norm=v_ffn1_norm, ffn1_w_in=v_ffn1_w_in, ffn1_w_out=v_ffn1_w_out, mix_norm=v_mix_norm, w_in=v_w_in,
             q_norm_a=v_q_norm_a, k_norm_a=v_k_norm_a, q_norm_b=v_q_norm_b, k_norm_b=v_k_norm_b,
             forget_bias=v_forget_bias, rel_bias_table=v_rel_bias_table, w_out=v_w_out, ffn2_norm=v_ffn2_norm,
             ffn2_w_in=v_ffn2_w_in, ffn2_w_out=v_ffn2_w_out)
    T, D = x.shape[1], x.shape[2]
    C = D // 2
    H = C // HEAD
    ff_shard = ffn1_w_out.shape[1]

    f1i, = _run_side(_all_gather_relayed([ffn1_w_in[0].T.astype(BF16)]), "gather_ffn1")
    wts = dict(ffn1_in=f1i.reshape(2, N_DEV, ff_shard, D))
    xi, yi, ci = lax.axis_index("x"), lax.axis_index("y"), lax.axis_index("c")
    core = jnp.reshape(ci, (1,)).astype(jnp.int32)
    where = jnp.stack([4 * xi + 2 * yi + ci, 2 * xi + yi]).astype(jnp.int32)
    gs, r1, ps, r2 = {}, {}, {}, {}

    def shard(name):
        name, _, part = name.partition(":")
        s = (w[name][0].T if name.endswith("w_in") else w[name][0]).astype(BF16)
        first = -(-(s.shape[0] // 2) // 16) * 16
        return {"": s, "first": s[:first], "rest": s[first:]}[part]

    def by_destination(name, grads):
        key = name + "_t" if name.endswith("w_in") else name
        gs[name] = grads[key].reshape(N_DEV, -1, D)
        return grads.get(key + "_bf16", grads[key]).reshape(N_DEV, -1, D)

    def chip_sums(names):
        for name in names:
            ps[name] = _chip_sum(gs[name], r1[name], core, "chip_sum_" + name)
        return [ps[name] for name in names]

    class Plan:
        carried = {"ffn1_up": ("gather", ("ffn1_w_out", "w_in:first")), "ffn1_down": ("gather", ("w_in:rest", "w_out")),
                   "dil_fwd": ("gather", ("ffn2_w_out",)), "fox_fwd": ("gather", ("ffn2_w_in",)),
                   "dil_bwd": ("in_chip", ("ffn2_w_in", "ffn2_w_out")), "fox_bwd": ("between", ("ffn2_w_in", "ffn2_w_out")),
                   "ffn1_bwd": ("in_chip", ("w_in", "w_out")), "ffn1_dwin": ("between", ("w_in", "w_out")),
                   "ffn1_rms_bwd": ("in_chip", ("ffn1_w_in",)), "ffn1_dwout": ("between", ("ffn1_w_in",))}

        def before(self, host, wts, grads):
            if host not in self.carried:
                return None
            kind, names = self.carried[host]
            if kind == "gather":
                return _all_gather([shard(n) for n in names])
            if kind == "in_chip":
                return _exchange_in_chip([by_destination(n, grads) for n in names])
            return _exchange_between_chips(chip_sums(names))

        def after(self, host, res, wts, grads):
            kind, names = self.carried[host]
            if host == "ffn1_up":
                wts.update(ffn1_out=res[0], w_in_first_rows=res[1])
            elif host == "ffn1_down":
                w_in_t = jnp.concatenate([wts.pop("w_in_first_rows"), res[0]], axis=1).reshape(-1, D)
                wts.update(w_in_t=w_in_t, w_f_t=jnp.pad(w_in_t[6 * C:], ((0, HEAD - H), (0, 0))),
                           w_o=res[1].reshape(2 * C, D))
            elif host == "dil_fwd":
                wts.update(ffn2_out=res[0])
            elif host == "fox_fwd":
                wts.update(ffn2_in=res[0].reshape(2, N_DEV, ff_shard, D))
            else:
                (r1 if kind == "in_chip" else r2).update(zip(names, res))

    small = {name: w[name] for name in SMALL}
    loss_row, grad_x, grads = _local_step(x[0], loss_target[0], small, wts, Plan())

    tail = ("ffn1_w_out",)
    r1[tail[0]], = _run_side(_exchange_in_chip([by_destination(tail[0], grads)]), "reduce_in_chip_tail")
    r2.update(zip(tail, _run_side(_exchange_between_chips(chip_sums(tail)), "reduce_between_chips_tail")))

    packed = _pack_small(grads)
    nsmall = packed.shape[0]
    packed = jnp.concatenate([packed, loss_row, jnp.zeros(((-nsmall - 1) % 8, HEAD), F32)], axis=0)
    reduced = _all_reduce_small(packed, "reduce_small")
    loss = reduced[nsmall, 0]
    g_small = _unpack_small(reduced[:nsmall], small)

    grad, delta, new_m, new_v = dict(g_small), {}, {}, {}
    for name in LARGE:
        to = (lambda t: t[0].T) if name.endswith("w_in") else (lambda t: t[0])
        back = (lambda t: t.T[None]) if name.endswith("w_in") else (lambda t: t[None])
        res = _reduce_adamw(gs[name], r1[name], r2[name], where, to(w[name]), to(m[name]), to(v[name]), "adamw_" + name)
        grad[name], delta[name], new_m[name], new_v[name] = (back(t) for t in res)
    d, nm, nv = _adamw(_pack_small(w), reduced[:nsmall], _pack_small(m), _pack_small(v), "adamw_small")
    delta.update(_unpack_small(d, small))
    new_m.update(_unpack_small(nm, small))
    new_v.update(_unpack_small(nv, small))
    return (loss, grad_x[None], *[grad[n] for n in ORDER], *[delta[n] for n in ORDER],
            *[new_m[n] for n in ORDER], *[new_v[n] for n in ORDER])
```

```python
import functools
import math

import numpy as np
import jax
import jax.numpy as jnp
from jax import lax
from jax.experimental import pallas as pl
from jax.experimental.pallas import tpu as pltpu

F32, BF16 = jnp.float32, jnp.bfloat16
HEAD = 128
NSLAB = 16
BLK = 128
DILATIONS = (1, 4, 16)
NUM_BUCKETS, MAX_DISTANCE = 32, 2048
RMS_EPS = 1e-6
NEG = -1e30
SCALE = HEAD ** -0.5
LR, B1, B2, EPS, WD, STEP = 0.001, 0.9, 0.999, 1e-08, 0.01, 10
N_DEV = 8
VMEM_LIMIT_BYTES = 56 << 20
MESH = pl.DeviceIdType.MESH


def _cparams(**kw):
    return pltpu.CompilerParams(vmem_limit_bytes=VMEM_LIMIT_BYTES, **kw)


def _nn(a, b):
    return jnp.dot(a, b, preferred_element_type=F32)


def _nt(a, b):
    return lax.dot_general(a, b, (((1,), (1,)), ((), ())), preferred_element_type=F32)


def _tn(a, b):
    return lax.dot_general(a, b, (((0,), (0,)), ((), ())), preferred_element_type=F32)


def _sds(shape, dtype):
    return jax.ShapeDtypeStruct(shape, dtype)


ANY = pl.BlockSpec(memory_space=pl.ANY)


class _Side:
    def __init__(self, ins, outs, sems, start, finish):
        self.ins, self.outs, self.sems, self.start, self.finish = list(ins), list(outs), list(sems), start, finish


def _call(body, *, name, grid, in_specs, out_specs, out_shape, args, scratch_shapes=(), side=None):
    in_specs, out_specs, out_shape = list(in_specs), list(out_specs), list(out_shape)
    scratch_shapes = list(scratch_shapes)
    if side is None:
        return pl.pallas_call(body, name=name, grid=grid, in_specs=in_specs, out_specs=out_specs, out_shape=out_shape,
                              scratch_shapes=scratch_shapes, compiler_params=_cparams())(*args)
    ni, no, ns = len(args), len(out_shape), len(scratch_shapes)
    si, so = len(side.ins), len(side.outs)

    def fused(*refs):
        h_in, s_in = refs[:ni], refs[ni:ni + si]
        h_out, s_out = refs[ni + si:ni + si + no], refs[ni + si + no:ni + si + no + so]
        h_scr, s_sem = refs[ni + si + no + so:ni + si + no + so + ns], refs[ni + si + no + so + ns:]
        ids = [pl.program_id(k) for k in range(len(grid))]
        first = functools.reduce(jnp.logical_and, [i == 0 for i in ids])
        last = functools.reduce(jnp.logical_and, [i == n - 1 for i, n in zip(ids, grid)])

        @pl.when(first)
        def _():
            side.start(s_in, s_out, s_sem)

        body(*h_in, *h_out, *h_scr)

        @pl.when(last)
        def _():
            side.finish(s_in, s_out, s_sem)

    res = pl.pallas_call(
        fused, name=name, grid=grid, in_specs=in_specs + [ANY] * si, out_specs=out_specs + [ANY] * so,
        out_shape=out_shape + side.outs, scratch_shapes=scratch_shapes + side.sems, compiler_params=_cparams(),
    )(*args, *side.ins)
    return list(res[:no]), list(res[no:])


def _ffn_fwd(x, g, win, wout, tm, name, side=None, target=None):
    T, D = x.shape
    nc, tf = win.shape[1], win.shape[2]
    down, loss = wout is not None, target is not None
    assert down or not loss

    def body(x_ref, g_ref, win_ref, *refs):
        ins, outs = refs[:down + loss], refs[down + loss:]
        h_ref, gu_ref, act_ref = outs[down:down + 3]
        i, j = pl.program_id(0), pl.program_id(1)

        @pl.when(j == 0)
        def _():
            xv = x_ref[...]
            r = lax.rsqrt(jnp.mean(xv * xv, axis=-1, keepdims=True) + RMS_EPS)
            h_ref[...] = (xv * r * g_ref[...]).astype(BF16)
            if down:
                outs[0][...] = jnp.zeros_like(outs[0])

        if loss:
            @pl.when((i == 0) & (j == 0))
            def _():
                outs[-1][...] = jnp.zeros_like(outs[-1])

        hb = h_ref[...]
        gt = _nt(win_ref[0], hb)
        up = _nt(win_ref[1], hb)
        gu_ref[0] = gt.astype(BF16)
        gu_ref[1] = up.astype(BF16)
        act = (gt * jax.nn.sigmoid(gt) * up).astype(BF16)
        act_ref[...] = act
        if down:
            y_ref = outs[0]
            y_ref[...] += _tn(act, ins[0][...])

            @pl.when(j == nc - 1)
            def _():
                y = x_ref[...] + 0.5 * y_ref[...]
                if not loss:
                    y_ref[...] = y
                    return
                err = y - ins[1][...]
                y_ref[...] = err * (1.0 / D)
                tot = 0.5 * jnp.sum(jnp.mean(err * err, axis=-1, keepdims=True), axis=0, keepdims=True)
                lane = lax.broadcasted_iota(jnp.int32, (1, HEAD), 1)
                outs[-1][...] += jnp.where(lane == 0, tot, 0.0)

    row = pl.BlockSpec((tm, D), lambda i, j: (i, 0))
    return _call(
        body, name=name, grid=(T // tm, nc), side=side,
        in_specs=[row, pl.BlockSpec((1, D), lambda i, j: (0, 0)),
                  pl.BlockSpec((2, None, tf, D), lambda i, j: (0, j, 0, 0))]
        + ([pl.BlockSpec((None, tf, D), lambda i, j: (j, 0, 0))] if down else []) + ([row] if loss else []),
        out_specs=([row] if down else [])
        + [row, pl.BlockSpec((2, None, tf, tm), lambda i, j: (0, j, 0, i)),
           pl.BlockSpec((None, tf, tm), lambda i, j: (j, 0, i))]
        + ([pl.BlockSpec((1, HEAD), lambda i, j: (0, 0))] if loss else []),
        out_shape=([_sds((T, D), F32)] if down else [])
        + [_sds((T, D), BF16), _sds((2, nc, tf, T), BF16), _sds((nc, tf, T), BF16)]
        + ([_sds((1, HEAD), F32)] if loss else []),
        args=(x, g, win) + ((wout,) if down else ()) + ((target,) if loss else ()))


def _ffn_down(x, act, wout, tm, name, side=None):
    T, D = x.shape
    nc, tf = wout.shape[0], wout.shape[1]

    def body(x_ref, act_ref, wout_ref, y_ref):
        j = pl.program_id(1)

        @pl.when(j == 0)
        def _():
            y_ref[...] = jnp.zeros_like(y_ref)

        y_ref[...] += _tn(act_ref[...], wout_ref[...])

        @pl.when(j == nc - 1)
        def _():
            y_ref[...] = x_ref[...] + 0.5 * y_ref[...]

    row = pl.BlockSpec((tm, D), lambda i, j: (i, 0))
    res = _call(
        body, name=name, grid=(T // tm, nc), side=side,
        in_specs=[row, pl.BlockSpec((None, tf, tm), lambda i, j: (j, 0, i)),
                  pl.BlockSpec((None, tf, D), lambda i, j: (j, 0, 0))],
        out_specs=[row], out_shape=[_sds((T, D), F32)], args=(x, act, wout))
    return res[0] if side is None else (res[0][0], res[1])


def _ffn_bwd(dy, gu, win, wout, tm, name, side=None):
    T, D = dy.shape
    nc, tf = wout.shape[0], wout.shape[1]

    def body(dy_ref, gu_ref, win_ref, wout_ref, dh_ref, dgu_ref, dyb_ref):
        j = pl.program_id(1)

        @pl.when(j == 0)
        def _():
            dh_ref[...] = jnp.zeros_like(dh_ref)
            dyb_ref[...] = (0.5 * dy_ref[...]).astype(BF16)

        dact = _nt(wout_ref[...], dyb_ref[...])
        gt = gu_ref[0].astype(F32)
        up = gu_ref[1].astype(F32)
        s = jax.nn.sigmoid(gt)
        dgb = (dact * up * (s * (1.0 + gt * (1.0 - s)))).astype(BF16)
        dub = (dact * (gt * s)).astype(BF16)
        dgu_ref[0] = dgb
        dgu_ref[1] = dub
        dh_ref[...] += _tn(dgb, win_ref[0]) + _tn(dub, win_ref[1])

    return _call(
        body, name=name, grid=(T // tm, nc), side=side,
        in_specs=[pl.BlockSpec((tm, D), lambda i, j: (i, 0)),
                  pl.BlockSpec((2, None, tf, tm), lambda i, j: (0, j, 0, i)),
                  pl.BlockSpec((2, None, tf, D), lambda i, j: (0, j, 0, 0)),
                  pl.BlockSpec((None, tf, D), lambda i, j: (j, 0, 0))],
        out_specs=[pl.BlockSpec((tm, D), lambda i, j: (i, 0)),
                   pl.BlockSpec((2, None, tf, tm), lambda i, j: (0, j, 0, i)),
                   pl.BlockSpec((tm, D), lambda i, j: (i, 0))],
        out_shape=[_sds((T, D), F32), _sds((2, nc, tf, T), BF16), _sds((T, D), BF16)],
        args=(dy, gu, win, wout))


def _rms_bwd(x, g, dh, dres, tm, name, side=None):
    T, D = x.shape

    def body(x_ref, g_ref, dh_ref, dres_ref, dx_ref, dg_ref):
        @pl.when(pl.program_id(0) == 0)
        def _():
            dg_ref[...] = jnp.zeros_like(dg_ref)

        xv = x_ref[...]
        r = lax.rsqrt(jnp.mean(xv * xv, axis=-1, keepdims=True) + RMS_EPS)
        xhat = xv * r
        dh = dh_ref[...]
        gd = dh * g_ref[...]
        dx_ref[...] = dres_ref[...] + r * (gd - xhat * jnp.mean(gd * xhat, axis=-1, keepdims=True))
        dg_ref[...] += jnp.sum(dh * xhat, axis=0, keepdims=True)

    row = pl.BlockSpec((tm, D), lambda i: (i, 0))
    one = pl.BlockSpec((1, D), lambda i: (0, 0))
    return _call(body, name=name, grid=(T // tm,), side=side, in_specs=[row, one, row, row], out_specs=[row, one],
                 out_shape=[_sds((T, D), F32), _sds((1, D), F32)], args=(x, g, dh, dres))


def _mm_tn(a, b, *, bm, bn, bt, name, b_slabs=False, side=None, rows=None, m_off=0, into=None, a_rows=False,
           twin=False):
    nz, T, M = (a.shape[0], a.shape[2], a.shape[1]) if a_rows else a.shape
    if b_slabs:
        N = b.shape[1] // NSLAB
        assert bt == T // NSLAB
        b_spec = pl.BlockSpec((bt, bn), lambda n, z, m, t: (0, t * (N // bn) + n))
    else:
        N = b.shape[1]
        b_spec = pl.BlockSpec((bt, bn), lambda n, z, m, t: (t, n))
    assert M % bm == 0 and N % bn == 0 and T % bt == 0, (M, bm, N, bn, T, bt)

    def body(a_ref, b_ref, *rest):
        c_ref = rest[-2] if twin else rest[-1]

        @pl.when(pl.program_id(3) == 0)
        def _():
            c_ref[...] = jnp.zeros_like(c_ref)

        ab, bb = a_ref[...].astype(BF16), b_ref[...].astype(BF16)
        c_ref[...] += _nn(ab, bb) if a_rows else _tn(ab, bb)
        if twin:
            @pl.when(pl.program_id(3) == T // bt - 1)
            def _():
                rest[-1][...] = c_ref[...].astype(BF16)

    grid = (N // bn, nz, M // bm, T // bt)
    a_spec = (pl.BlockSpec((None, bm, bt), lambda n, z, m, t: (z, m, t)) if a_rows
              else pl.BlockSpec((None, bt, bm), lambda n, z, m, t: (z, t, m)))
    in_specs = [a_spec, b_spec]
    out_spec = pl.BlockSpec((None, bm, bn), lambda n, z, m, t: (z, m + m_off, n))
    out_shape = _sds((nz, M if rows is None else rows, N), F32)
    if into is not None:
        assert side is None and not twin and into.shape == out_shape.shape
        return pl.pallas_call(body, name=name, grid=grid, in_specs=in_specs + [ANY], out_specs=out_spec,
                              out_shape=out_shape, input_output_aliases={2: 0}, compiler_params=_cparams())(a, b, into)
    outs = [out_shape] + ([_sds(out_shape.shape, BF16)] if twin else [])
    res = _call(body, name=name, grid=grid, side=side, in_specs=in_specs, out_specs=[out_spec] * len(outs),
                out_shape=outs, args=(a, b))
    mine = res if side is None else res[0]
    mine = tuple(mine) if twin else mine[0]
    return mine if side is None else (mine, res[1])


def _tok_spec(layout, tm, n16, C, bc, colmap):
    if layout == "nat":
        return pl.BlockSpec((tm, bc), lambda i, k: (i, colmap(k)))
    assert tm % n16 == 0
    if layout == "slab":
        return pl.BlockSpec((tm // n16, n16, bc), lambda i, k: (i, 0, colmap(k)))
    assert bc == C
    return pl.BlockSpec((n16, (tm // n16) * C), lambda i, k: (0, i))


def _tok_load(ref, layout, sp):
    if layout == "nat":
        return ref[...]
    if layout == "slab":
        return ref[...].reshape(-1, ref.shape[-1])
    c = ref.shape[1] // sp
    return jnp.concatenate([ref[:, s * c:(s + 1) * c] for s in range(sp)], axis=0)


def _tok_store(ref, layout, sp, val, cols=None, accumulate=False):
    def put(idx, v):
        if accumulate:
            ref[idx] += v
        else:
            ref[idx] = v

    lanes = slice(None) if cols is None else slice(cols[0], cols[0] + cols[1])
    if layout == "nat":
        put((slice(None), lanes), val)
    elif layout == "slab":
        put((slice(None), slice(None), lanes), val.reshape(sp, ref.shape[1], val.shape[-1]))
    else:
        assert cols is None
        c, n = ref.shape[1] // sp, ref.shape[0]
        for s in range(sp):
            put((slice(None), slice(s * c, (s + 1) * c)), val[s * n:(s + 1) * n])


def _proj(x, g, wt, gains, modes, *, tn, w_off, slabs, tm, normed_dtype, name):
    T, D = x.shape
    ntile = len(modes)
    N = ntile * tn
    n16 = T // NSLAB
    in_layout, out_layout = ("view", "slab") if slabs else ("nat", "nat")
    sp = tm // n16
    x_in = x.reshape(n16, NSLAB * D) if slabs else x
    x_spec = _tok_spec(in_layout, tm, n16, D, D, lambda n: 0)
    oshape = lambda c: (NSLAB, n16, c) if slabs else (T, c)
    ospec = lambda bc, cm: _tok_spec(out_layout, tm, n16, None, bc, cm)

    def body(x_ref, g_ref, w_ref, gains_ref, raw_ref, nrm_ref, h_ref):
        n = pl.program_id(1)

        @pl.when(n == 0)
        def _():
            xv = _tok_load(x_ref, in_layout, sp)
            r = lax.rsqrt(jnp.mean(xv * xv, axis=-1, keepdims=True) + RMS_EPS)
            _tok_store(h_ref, out_layout, sp, (xv * r * g_ref[...]).astype(BF16))

        y = _nt(_tok_load(h_ref, out_layout, sp), w_ref[...])
        _tok_store(raw_ref, out_layout, sp, y)
        for t, mode in enumerate(modes):
            @pl.when(n == t)
            def _(t=t, mode=mode):
                if not mode:
                    _tok_store(nrm_ref, out_layout, sp, y.astype(nrm_ref.dtype))
                    return
                gain = gains_ref[t]
                for k in range(tn // HEAD):
                    yk = y[:, k * HEAD:(k + 1) * HEAD]
                    r = lax.rsqrt(jnp.mean(yk * yk, axis=-1, keepdims=True) + RMS_EPS)
                    _tok_store(nrm_ref, out_layout, sp, (yk * r * gain).astype(nrm_ref.dtype), cols=(k * HEAD, HEAD))

    return pl.pallas_call(
        body, name=name, grid=(T // tm, ntile),
        in_specs=[x_spec, pl.BlockSpec((1, D), lambda i, n: (0, 0)),
                  pl.BlockSpec((tn, D), lambda i, n: (n + w_off, 0)),
                  pl.BlockSpec((ntile, 1, HEAD), lambda i, n: (0, 0, 0))],
        out_specs=[ospec(tn, lambda n: n), ospec(tn, lambda n: n), ospec(D, lambda n: 0)],
        out_shape=[_sds(oshape(N), F32), _sds(oshape(N), normed_dtype), _sds(oshape(D), BF16)],
        compiler_params=_cparams(),
    )(x_in, g, wt, gains)


def _mm(a, w, *, nt, tk, tm, a_layout, out_layout, resid=None, name, w_off=0, n_out=None):
    if a_layout == "slab":
        T, K = a.shape[0] * a.shape[1], a.shape[2]
    else:
        T, K = a.shape
    N = (w.shape[0] if nt else w.shape[1]) if n_out is None else n_out
    n16 = T // NSLAB
    nk = K // tk
    sp = tm // n16
    a_in = a.reshape(n16, NSLAB * K) if a_layout == "view" else a
    w_spec = (pl.BlockSpec((N, tk), lambda i, k: (w_off, k)) if nt
              else pl.BlockSpec((tk, N), lambda i, k: (k + w_off, 0)))
    o_spec = _tok_spec(out_layout, tm, n16, N, N, lambda k: 0)
    oshape = {"nat": (T, N), "slab": (NSLAB, n16, N), "view": (n16, NSLAB * N)}[out_layout]
    has_resid = resid is not None

    def body(*refs):
        a_ref, w_ref = refs[0], refs[1]
        o_ref = refs[-1]
        k = pl.program_id(1)

        @pl.when(k == 0)
        def _():
            o_ref[...] = refs[2][...] if has_resid else jnp.zeros_like(o_ref)

        ab = _tok_load(a_ref, a_layout, sp).astype(BF16)
        _tok_store(o_ref, out_layout, sp, _nt(ab, w_ref[...]) if nt else _nn(ab, w_ref[...]), accumulate=True)

    ins = [a_in, w]
    in_specs = [_tok_spec(a_layout, tm, n16, K, tk, lambda k: k), w_spec]
    if has_resid:
        ins.append(resid.reshape(n16, NSLAB * N) if out_layout == "view" else resid)
        in_specs.append(o_spec)
    out = pl.pallas_call(
        body, name=name, grid=(T // tm, nk), in_specs=in_specs, out_specs=o_spec,
        out_shape=_sds(oshape, F32), compiler_params=_cparams(),
    )(*ins)
    return out.reshape(T, N) if out_layout == "view" else out


def _log_sigmoid(z):
    return jnp.minimum(z, 0.0) - jnp.log(1.0 + jnp.exp(-jnp.abs(z)))


def _fox_gate_fwd(f_raw, fbias, name):
    T = f_raw.shape[0]
    cb = 256

    def body(f_ref, b_ref, c_ref):
        row = lax.broadcasted_iota(jnp.int32, (cb, cb), 0)
        col = lax.broadcasted_iota(jnp.int32, (cb, cb), 1)
        tri = (col <= row).astype(F32)
        carry = jnp.zeros((1, HEAD), F32)
        for i in range(T // cb):
            lf = _log_sigmoid(f_ref[i * cb:(i + 1) * cb, :] + b_ref[...])
            c = jnp.dot(tri, lf, preferred_element_type=F32, precision=lax.Precision.HIGHEST) + carry
            c_ref[i * cb:(i + 1) * cb, :] = c
            carry = c[cb - 1:cb, :]

    return pl.pallas_call(body, name=name, out_shape=_sds((T, HEAD), F32), compiler_params=_cparams())(f_raw, fbias)


def _fox_gate_bwd(f_raw, fbias, dc, name):
    T = f_raw.shape[0]
    cb = 256

    def body(f_ref, b_ref, dc_ref, df_ref, db_ref):
        row = lax.broadcasted_iota(jnp.int32, (cb, cb), 0)
        col = lax.broadcasted_iota(jnp.int32, (cb, cb), 1)
        tri = (col >= row).astype(F32)
        carry = jnp.zeros((1, HEAD), F32)
        dbias = jnp.zeros((1, HEAD), F32)
        for i in reversed(range(T // cb)):
            dlf = jnp.dot(tri, dc_ref[i * cb:(i + 1) * cb, :], preferred_element_type=F32,
                          precision=lax.Precision.HIGHEST) + carry
            carry = dlf[0:1, :]
            z = f_ref[i * cb:(i + 1) * cb, :] + b_ref[...]
            df = dlf * jax.nn.sigmoid(-z)
            df_ref[i * cb:(i + 1) * cb, :] = df
            dbias = dbias + jnp.sum(df, axis=0, keepdims=True)
        db_ref[...] = dbias

    return pl.pallas_call(body, name=name, out_shape=[_sds((T, HEAD), F32), _sds((1, HEAD), F32)],
                          compiler_params=_cparams())(f_raw, fbias, dc)


def _fox_fwd(qkv, c_col, c_row, tq, name, side=None):
    T = qkv.shape[0]
    H = qkv.shape[1] // (3 * HEAD)
    nq = T // tq
    c_blocks = c_row.reshape(H, nq, 1, tq)

    def body(q_ref, k_ref, v_ref, cq_ref, ck_ref, o_ref, lse_ref):
        qi = pl.program_id(1)
        q, cq = q_ref[...], cq_ref[...]
        causal = lax.broadcasted_iota(jnp.int32, (tq, tq), 1) <= lax.broadcasted_iota(jnp.int32, (tq, tq), 0)

        def key_block(ki, carry, diagonal):
            m, l, acc = carry
            rows = pl.ds(pl.multiple_of(ki * tq, tq), tq)
            s = _nt(q, k_ref[rows, :]) * SCALE + cq - ck_ref[ki]
            if diagonal:
                s = jnp.where(causal, s, NEG)
            m_new = jnp.maximum(m, jnp.max(s, axis=-1, keepdims=True))
            alpha = jnp.exp(m - m_new)
            p = jnp.exp(s - m_new)
            l = alpha * l + jnp.sum(p, axis=-1, keepdims=True)
            acc = alpha * acc + _nn(p.astype(BF16), v_ref[rows, :])
            return m_new, l, acc

        init = (jnp.full((tq, 1), NEG, F32), jnp.zeros((tq, 1), F32), jnp.zeros((tq, HEAD), F32))
        carry = lax.fori_loop(0, qi, lambda ki, c: key_block(ki, c, False), init)
        m, l, acc = key_block(qi, carry, True)
        o_ref[...] = acc / l
        lse_ref[...] = m + jnp.log(l)

    return _call(
        body, name=name, grid=(H, nq), side=side,
        in_specs=[pl.BlockSpec((tq, HEAD), lambda h, qi: (qi, h)),
                  pl.BlockSpec((T, HEAD), lambda h, qi: (0, H + h)),
                  pl.BlockSpec((T, HEAD), lambda h, qi: (0, 2 * H + h)),
                  pl.BlockSpec((None, tq, 1), lambda h, qi: (h, qi, 0)),
                  pl.BlockSpec((None, nq, 1, tq), lambda h, qi: (h, 0, 0, 0))],
        out_specs=[pl.BlockSpec((tq, HEAD), lambda h, qi: (qi, h)),
                   pl.BlockSpec((None, tq, 1), lambda h, qi: (h, qi, 0))],
        out_shape=[_sds((T, H * HEAD), F32), _sds((H, T, 1), F32)],
        args=(qkv, qkv, qkv, c_col, c_blocks))


def _fox_bwd(qkv, c_col, c_row, out, dout, lse, tq, name, side=None):
    T = qkv.shape[0]
    H = qkv.shape[1] // (3 * HEAD)
    nq = T // tq

    def body(q_ref, k_ref, v_ref, cq_ref, ck_ref, o_ref, do_ref, lse_ref, dq_ref, dk_ref, dv_ref, dck_ref, dcq_ref,
             delta_s):
        ki = pl.program_id(1)

        @pl.when(ki == 0)
        def _():
            dq_ref[...] = jnp.zeros_like(dq_ref)
            dcq_ref[...] = jnp.zeros_like(dcq_ref)
            delta_s[...] = jnp.sum(do_ref[...] * o_ref[...], axis=-1, keepdims=True)

        k, v, ck = k_ref[...], v_ref[...], ck_ref[...]
        causal = lax.broadcasted_iota(jnp.int32, (tq, tq), 1) <= lax.broadcasted_iota(jnp.int32, (tq, tq), 0)
        ones = jnp.ones((8, tq), BF16)

        def query_block(qi, carry, diagonal):
            dk, dv, dck = carry
            rows = pl.ds(pl.multiple_of(qi * tq, tq), tq)
            q = q_ref[rows, :]
            s = _nt(q, k) * SCALE + cq_ref[rows, :] - ck
            if diagonal:
                s = jnp.where(causal, s, NEG)
            p = jnp.exp(s - lse_ref[rows, :])
            dob = do_ref[rows, :].astype(BF16)
            ds = p * (_nt(dob, v) - delta_s[rows, :])
            dsb = ds.astype(BF16)
            dq_ref[rows, :] += _nn(dsb, k) * SCALE
            dcq_ref[qi] += _nt(ones, dsb)[0:1]
            return dk + _tn(dsb, q), dv + _tn(p.astype(BF16), dob), dck - _nn(ones, dsb)[0:1]

        init = (jnp.zeros((tq, HEAD), F32), jnp.zeros((tq, HEAD), F32), jnp.zeros((1, tq), F32))
        carry = query_block(ki, init, True)
        dk, dv, dck = lax.fori_loop(ki + 1, nq, lambda qi, c: query_block(qi, c, False), carry)
        dk_ref[...] = dk * SCALE
        dv_ref[...] = dv
        dck_ref[...] = dck

    head = lambda off: pl.BlockSpec((T, HEAD), lambda h, ki: (0, off + h))
    col = pl.BlockSpec((None, T, 1), lambda h, ki: (h, 0, 0))
    return _call(
        body, name=name, grid=(H, nq), side=side,
        in_specs=[head(0),
                  pl.BlockSpec((tq, HEAD), lambda h, ki: (ki, H + h)),
                  pl.BlockSpec((tq, HEAD), lambda h, ki: (ki, 2 * H + h)),
                  col, pl.BlockSpec((None, 1, tq), lambda h, ki: (h, 0, ki)), head(0), head(0), col],
        out_specs=[head(0),
                   pl.BlockSpec((tq, HEAD), lambda h, ki: (ki, h)),
                   pl.BlockSpec((tq, HEAD), lambda h, ki: (ki, h)),
                   pl.BlockSpec((None, 1, tq), lambda h, ki: (h, 0, ki)),
                   pl.BlockSpec((None, nq, 1, tq), lambda h, ki: (h, 0, 0, 0))],
        out_shape=[_sds((T, H * HEAD), F32), _sds((T, H * HEAD), F32), _sds((T, H * HEAD), F32), _sds((H, 1, T), F32),
                   _sds((H, nq, 1, tq), F32)],
        scratch_shapes=[pltpu.VMEM((T, 1), F32)],
        args=(qkv, qkv, qkv, c_col, c_row, out, dout, lse))


def _t5_bucket(dist):
    max_exact = NUM_BUCKETS // 2
    d = dist.astype(np.float32)
    large = max_exact + (np.log(np.maximum(d, np.float32(1.0)) / np.float32(max_exact))
                         / np.float32(math.log(MAX_DISTANCE / max_exact))
                         * np.float32(NUM_BUCKETS - max_exact)).astype(np.int32)
    large = np.minimum(large, NUM_BUCKETS - 1)
    return np.where(dist < max_exact, dist, large)


def _bucket_maps():
    maps = []
    for d in DILATIONS:
        e = NSLAB // d
        rows = BLK // e
        idx = np.arange(BLK)
        pos = e * (idx % rows) + idx // rows
        qpos = pos[:, None] + BLK
        kpos = np.concatenate([pos, pos + BLK])[None, :]
        delta = qpos - kpos
        band = (delta >= 0) & (delta <= BLK)
        bucket = _t5_bucket(np.clip(delta, 0, None) * d)
        maps.append(np.where(band, bucket, -1).astype(np.int32))
    return np.stack(maps)


def _dil_geometry(T):
    n16 = T // NSLAB
    geo = []
    for d in DILATIONS:
        e = NSLAB // d
        rows = BLK // e
        nblk = n16 // rows
        geo.append((d, e, rows, nblk))
    return geo


DIL_INTERLEAVE_FWD = {1: 4, 4: 8, 16: 8}
DIL_INTERLEAVE_BWD = {1: 8, 4: 8, 16: 8}


def _dil_interleave(per_step, nblocks):
    while per_step > 1 and (nblocks % per_step or nblocks // per_step < 2):
        per_step -= 1
    return per_step


def _dil_bias(tab_ref, bkt_ref, bias_s, h):
    for p in range(len(DILATIONS)):
        bk = bkt_ref[p]
        bias = jnp.full((BLK, 2 * BLK), NEG, F32)
        for b in range(NUM_BUCKETS):
            bias = jnp.where(bk == b, tab_ref[b, h], bias)
        bias_s[p] = bias


def _dil_rows(d, e, rows, sub, blk):
    start = pl.multiple_of(blk * rows, rows)
    return [(sub + d * j, pl.ds(start, rows)) for j in range(e)]


def _gather(ref, idx):
    return jnp.concatenate([ref[s, r, :] for s, r in idx], axis=0)


def _scatter(ref, idx, val, rows):
    for j, (s, r) in enumerate(idx):
        ref[s, r, :] = val[j * rows:(j + 1) * rows]


def _scatter_add(ref, idx, val, rows):
    for j, (s, r) in enumerate(idx):
        ref[s, r, :] += val[j * rows:(j + 1) * rows]


def _dil_fwd(qkv, table, name, side=None):
    n16 = qkv.shape[1]
    T = NSLAB * n16
    H = qkv.shape[2] // (3 * HEAD)
    geo = _dil_geometry(T)
    bkt = jnp.asarray(_bucket_maps())

    def body(tab_ref, bkt_ref, q_ref, k_ref, v_ref, o_ref, lse_ref, bias_s, m_s, l_s):
        h = pl.program_id(0)
        _dil_bias(tab_ref, bkt_ref, bias_s, h)
        first_mask = lax.broadcasted_iota(jnp.int32, (BLK, 2 * BLK), 1) < BLK

        starts = len(DILATIONS) - 1

        def load(p, d, e, rows, sub, blk):
            cur = _dil_rows(d, e, rows, sub, blk)
            prev = _dil_rows(d, e, rows, sub, jnp.maximum(blk - 1, 0))
            q = _gather(q_ref, cur).astype(BF16)
            kk = jnp.concatenate([_gather(k_ref, prev), _gather(k_ref, cur)], axis=0).astype(BF16)
            vv = jnp.concatenate([_gather(v_ref, prev), _gather(v_ref, cur)], axis=0).astype(BF16)
            old = None if p == starts else (_gather(m_s, cur), _gather(l_s, cur), _gather(o_ref, cur))
            return cur, blk, q, kk, vv, old

        def compute(p, blk, q, kk, vv, old):
            s = _nt(q, kk) * SCALE + bias_s[p]
            s = jnp.where(first_mask & (blk == 0), NEG, s)
            m_blk = jnp.max(s, axis=-1, keepdims=True)
            if old is None:
                m_new = m_blk
                pr = jnp.exp(s - m_new)
                l_new = jnp.sum(pr, axis=-1, keepdims=True)
                acc = _nn(pr.astype(BF16), vv)
            else:
                m_old, l_old, acc_old = old
                m_new = jnp.maximum(m_old, m_blk)
                alpha = jnp.exp(m_old - m_new)
                pr = jnp.exp(s - m_new)
                l_new = alpha * l_old + jnp.sum(pr, axis=-1, keepdims=True)
                acc = alpha * acc_old + _nn(pr.astype(BF16), vv)
            if p == 0:
                return acc / l_new, m_new + jnp.log(l_new), None
            return acc, m_new, l_new

        def store(p, rows, cur, acc, m_new, l_new):
            _scatter(o_ref, cur, acc, rows)
            if p == 0:
                _scatter(lse_ref, cur, m_new, rows)
            else:
                _scatter(m_s, cur, m_new, rows)
                _scatter(l_s, cur, l_new, rows)

        for p in reversed(range(len(DILATIONS))):
            d, e, rows, nblk = geo[p]
            per_step = _dil_interleave(DIL_INTERLEAVE_FWD[d], d * nblk)

            def step(i, carry, p=p, d=d, e=e, rows=rows, nblk=nblk, per_step=per_step):
                ids = [i + u * (d * nblk // per_step) for u in range(per_step)]
                loaded = [load(p, d, e, rows, j // nblk, j % nblk) for j in ids]
                done = [(cur, compute(p, blk, q, kk, vv, old)) for cur, blk, q, kk, vv, old in loaded]
                for cur, res in done:
                    store(p, rows, cur, *res)
                return carry

            lax.fori_loop(0, d * nblk // per_step, step, 0)

    head = lambda off: pl.BlockSpec((NSLAB, n16, HEAD), lambda h: (0, 0, off + h))
    return _call(
        body, name=name, grid=(H,), side=side,
        in_specs=[pl.BlockSpec(memory_space=pltpu.SMEM), pl.BlockSpec((3, BLK, 2 * BLK), lambda h: (0, 0, 0)),
                  head(0), head(H), head(2 * H)],
        out_specs=[head(0), pl.BlockSpec((None, NSLAB, n16, 1), lambda h: (h, 0, 0, 0))],
        out_shape=[_sds((NSLAB, n16, H * HEAD), F32), _sds((H, NSLAB, n16, 1), F32)],
        scratch_shapes=[pltpu.VMEM((3, BLK, 2 * BLK), F32), pltpu.VMEM((NSLAB, n16, 1), F32),
                        pltpu.VMEM((NSLAB, n16, 1), F32)],
        args=(table, bkt, qkv, qkv, qkv))


def _dil_bwd(qkv, table, out, dout, lse, name, side=None):
    n16 = qkv.shape[1]
    T = NSLAB * n16
    H = qkv.shape[2] // (3 * HEAD)
    geo = _dil_geometry(T)
    bkt = jnp.asarray(_bucket_maps())

    def body(tab_ref, bkt_ref, q_ref, k_ref, v_ref, o_ref, do_ref, lse_ref,
             dq_ref, dk_ref, dv_ref, dtab_ref, bias_s, dbias_s, delta_s):
        h = pl.program_id(0)
        _dil_bias(tab_ref, bkt_ref, bias_s, h)
        first_mask = lax.broadcasted_iota(jnp.int32, (BLK, 2 * BLK), 1) < BLK
        dbias_s[...] = jnp.zeros_like(dbias_s)
        dq_ref[...] = jnp.zeros_like(dq_ref)
        dk_ref[...] = jnp.zeros_like(dk_ref)
        dv_ref[...] = jnp.zeros_like(dv_ref)
        for r in range(NSLAB):
            delta_s[r] = jnp.sum(do_ref[r] * o_ref[r], axis=-1, keepdims=True)

        def load(d, e, rows, sub, blk):
            cur = _dil_rows(d, e, rows, sub, blk)
            prev = _dil_rows(d, e, rows, sub, jnp.maximum(blk - 1, 0))
            q = _gather(q_ref, cur).astype(BF16)
            kk = jnp.concatenate([_gather(k_ref, prev), _gather(k_ref, cur)], axis=0).astype(BF16)
            vv = jnp.concatenate([_gather(v_ref, prev), _gather(v_ref, cur)], axis=0).astype(BF16)
            dob = _gather(do_ref, cur).astype(BF16)
            return cur, prev, blk, q, kk, vv, dob, _gather(lse_ref, cur), _gather(delta_s, cur)

        def compute(p, blk, q, kk, vv, dob, lse, delta):
            s = _nt(q, kk) * SCALE + bias_s[p]
            s = jnp.where(first_mask & (blk == 0), NEG, s)
            pr = jnp.exp(s - lse)
            ds = pr * (_nt(dob, vv) - delta)
            dsb = ds.astype(BF16)
            return ds, _nn(dsb, kk) * SCALE, _tn(dsb, q) * SCALE, _tn(pr.astype(BF16), dob)

        def store(rows, cur, prev, dq, dkk, dvv):
            _scatter_add(dq_ref, cur, dq, rows)
            _scatter_add(dk_ref, prev, dkk[:BLK], rows)
            _scatter_add(dk_ref, cur, dkk[BLK:], rows)
            _scatter_add(dv_ref, prev, dvv[:BLK], rows)
            _scatter_add(dv_ref, cur, dvv[BLK:], rows)

        for p in range(len(DILATIONS)):
            d, e, rows, nblk = geo[p]
            per_step = _dil_interleave(DIL_INTERLEAVE_BWD[d], d * nblk)

            def step(i, carry, p=p, d=d, e=e, rows=rows, nblk=nblk, per_step=per_step):
                ids = [i + u * (d * nblk // per_step) for u in range(per_step)]
                loaded = [load(d, e, rows, j // nblk, j % nblk) for j in ids]
                done = [(cur, prev, compute(p, *rest)) for cur, prev, *rest in loaded]
                dbias_s[p] += functools.reduce(jnp.add, [res[0] for _, _, res in done])
                for cur, prev, res in done:
                    store(rows, cur, prev, *res[1:])
                return carry

            lax.fori_loop(0, d * nblk // per_step, step, 0)

        lane = lax.broadcasted_iota(jnp.int32, (1, HEAD), 1)
        row = jnp.zeros((1, HEAD), F32)
        for b in range(NUM_BUCKETS):
            tot = jnp.zeros((1, 1), F32)
            for p in range(len(DILATIONS)):
                hit = jnp.where(bkt_ref[p] == b, dbias_s[p], 0.0)
                tot = tot + jnp.sum(jnp.sum(hit, axis=0, keepdims=True), axis=1, keepdims=True)
            row = jnp.where(lane == b, tot, row)
        dtab_ref[...] = row

    head = lambda off: pl.BlockSpec((NSLAB, n16, HEAD), lambda h: (0, 0, off + h))
    return _call(
        body, name=name, grid=(H,), side=side,
        in_specs=[pl.BlockSpec(memory_space=pltpu.SMEM), pl.BlockSpec((3, BLK, 2 * BLK), lambda h: (0, 0, 0)),
                  head(0), head(H), head(2 * H), head(0), head(0),
                  pl.BlockSpec((None, NSLAB, n16, 1), lambda h: (h, 0, 0, 0))],
        out_specs=[head(0), head(0), head(0), pl.BlockSpec((None, 1, HEAD), lambda h: (h, 0, 0))],
        out_shape=[_sds((NSLAB, n16, H * HEAD), F32)] * 3 + [_sds((H, 1, HEAD), F32)],
        scratch_shapes=[pltpu.VMEM((3, BLK, 2 * BLK), F32), pltpu.VMEM((3, BLK, 2 * BLK), F32),
                        pltpu.VMEM((NSLAB, n16, 1), F32)],
        args=(table, bkt, qkv, qkv, qkv, out, dout, lse))


def _qknorm_bwd(raw, dq, dk, dv, gains, tm, name):
    T, N = raw.shape
    C = N // 3

    def body(raw_ref, dq_ref, dk_ref, dv_ref, gains_ref, dp_ref, dg_ref):
        @pl.when(pl.program_id(0) == 0)
        def _():
            dg_ref[...] = jnp.zeros_like(dg_ref)

        for t, d_ref in enumerate((dq_ref, dk_ref)):
            gain = gains_ref[t]
            dgain = jnp.zeros((1, HEAD), F32)
            for k in range(C // HEAD):
                y = raw_ref[:, t * C + k * HEAD:t * C + (k + 1) * HEAD]
                dn = d_ref[:, k * HEAD:(k + 1) * HEAD]
                r = lax.rsqrt(jnp.mean(y * y, axis=-1, keepdims=True) + RMS_EPS)
                yhat = y * r
                gd = dn * gain
                dy = r * (gd - yhat * jnp.mean(gd * yhat, axis=-1, keepdims=True))
                dp_ref[:, t * C + k * HEAD:t * C + (k + 1) * HEAD] = dy.astype(BF16)
                dgain = dgain + jnp.sum(dn * yhat, axis=0, keepdims=True)
            dg_ref[t] += dgain
        dp_ref[:, 2 * C:] = dv_ref[...].astype(BF16)

    third = pl.BlockSpec((tm, C), lambda i: (i, 0))
    return pl.pallas_call(
        body, name=name, grid=(T // tm,),
        in_specs=[pl.BlockSpec((tm, N), lambda i: (i, 0)), third, third, third,
                  pl.BlockSpec((2, 1, HEAD), lambda i: (0, 0, 0))],
        out_specs=[pl.BlockSpec((tm, N), lambda i: (i, 0)), pl.BlockSpec((2, 1, HEAD), lambda i: (0, 0, 0))],
        out_shape=[_sds((T, N), BF16), _sds((2, 1, HEAD), F32)], compiler_params=_cparams(),
    )(raw, dq, dk, dv, gains)


def _pad_lanes(v, width=HEAD):
    return jnp.pad(v, ((0, 0), (0, width - v.shape[1])))


def _local_step(x, target, small, wts, plan=None):
    grads = {}

    def hosted(host, fn, *args, **kw):
        side = plan.before(host, wts, grads) if plan is not None else None
        if side is None:
            return fn(*args, name=host, **kw)
        res, side_res = fn(*args, name=host, side=side, **kw)
        plan.after(host, side_res, wts, grads)
        return res

    T, D = x.shape
    C = D // 2
    H = C // HEAD
    n16 = T // NSLAB
    tm = min(512, T)
    tmm = min(1024, T)
    tms = 4 * n16
    tq = min(512, T)
    bn = min(1024, D)
    g1, gm, g2 = small["ffn1_norm"], small["mix_norm"], small["ffn2_norm"]
    gains_a = jnp.stack([small["q_norm_a"], small["k_norm_a"], jnp.ones_like(small["q_norm_a"])])
    gains_b = jnp.stack([small["q_norm_b"], small["k_norm_b"], jnp.ones_like(small["q_norm_b"])])
    fbias = _pad_lanes(small["forget_bias"])
    table = small["rel_bias_table"]

    h1, gu1, act1 = hosted("ffn1_up", _ffn_fwd, x, g1, wts["ffn1_in"], None, tmm)
    x1 = hosted("ffn1_down", _ffn_down, x, act1, wts["ffn1_out"], tmm)
    w_in_t, w_f_t, w_o = wts["w_in_t"], wts["w_f_t"], wts["w_o"]
    raw_a, nrm_a, h2a = _proj(x1, gm, w_in_t, gains_a, (True, True, False), tn=C, w_off=0, slabs=True, tm=tms,
                              normed_dtype=F32, name="proj_a")
    raw_b, nrm_b, h2b = _proj(x1, gm, w_in_t, gains_b, (True, True, False), tn=C, w_off=3, slabs=False, tm=tmm,
                              normed_dtype=BF16, name="proj_b")
    f_raw, _, _ = _proj(x1, gm, w_f_t, gains_b[:1], (False,), tn=HEAD, w_off=0, slabs=False, tm=tm,
                        normed_dtype=BF16, name="proj_f")
    c = _fox_gate_fwd(f_raw, fbias, "fox_gate_fwd")
    c_heads = c[:, :H].T
    c_col, c_row = c_heads[:, :, None], c_heads[:, None, :]
    out_a, lse_a = hosted("dil_fwd", _dil_fwd, nrm_a, table)
    out_b, lse_b = hosted("fox_fwd", _fox_fwd, nrm_b, c_col, c_row, tq)
    x2a = _mm(out_a, w_o, nt=False, tk=C, tm=tms, a_layout="slab", out_layout="view", resid=x1, name="out_a")
    x2 = _mm(out_b, w_o, nt=False, tk=C, tm=tmm, a_layout="nat", out_layout="nat", resid=x2a, w_off=1, name="out_b")
    dy, h3, gu3, act3, loss_row = _ffn_fwd(x2, g2, wts["ffn2_in"], wts["ffn2_out"], tm, "ffn2_fwd", target=target)

    def ffn_backward(tag, xin, g, h, gu, act, win, wout, dres):
        nc, tf = wout.shape[0], wout.shape[1]
        dh, dgu, dyb = hosted(tag + "_bwd", _ffn_bwd, dres, gu, win, wout, tm)
        grads[tag + "_w_in_t"], grads[tag + "_w_in_t_bf16"] = hosted(
            tag + "_dwin", _mm_tn, dgu.reshape(2 * nc, tf, T), h, bm=tf, bn=bn, bt=T, a_rows=True, twin=True)
        dxin, grads[tag + "_norm"] = hosted(tag + "_rms_bwd", _rms_bwd, xin, g, dh, dres, tm)
        grads[tag + "_w_out"], grads[tag + "_w_out_bf16"] = hosted(
            tag + "_dwout", _mm_tn, act, dyb, bm=tf, bn=bn, bt=T, a_rows=True, twin=True)
        return dxin

    dx2 = ffn_backward("ffn2", x2, g2, h3, gu3, act3, wts["ffn2_in"], wts["ffn2_out"], dy)

    dmix_a = _mm(dx2, w_o, nt=True, tk=D, tm=tms, a_layout="view", out_layout="slab", n_out=C, name="dmix_a")
    dmix_b = _mm(dx2, w_o, nt=True, tk=D, tm=tmm, a_layout="nat", out_layout="nat", n_out=C, w_off=1, name="dmix_b")
    dwo = _mm_tn(out_a.reshape(1, T, C), dx2.reshape(n16, NSLAB * D), bm=C, bn=bn, bt=n16, b_slabs=True,
                 rows=2 * C, name="dwo_a")
    dwo = _mm_tn(out_b.reshape(1, T, C), dx2, bm=C, bn=bn, bt=tm, rows=2 * C, m_off=1, into=dwo, name="dwo_b")
    grads["w_out"] = dwo[0]

    dqa, dka, dva, dtab = hosted("dil_bwd", _dil_bwd, nrm_a, table, out_a, dmix_a, lse_a)
    dqb, dkb, dvb, dck, dcq = hosted("fox_bwd", _fox_bwd, nrm_b, c_col, c_row, out_b, dmix_b, lse_b, tq)
    grads["rel_bias_table"] = dtab[:, 0, :NUM_BUCKETS].T
    dc = _pad_lanes((dck[:, 0, :] + dcq.reshape(H, T)).T)
    df, dfb = _fox_gate_bwd(f_raw, fbias, dc, "fox_gate_bwd")
    grads["forget_bias"] = dfb[:, :H]

    flat = lambda a: a.reshape(T, a.shape[-1])
    dproj_a, dgain_a = _qknorm_bwd(flat(raw_a), flat(dqa), flat(dka), flat(dva), gains_a[:2], min(256, T), "qknorm_bwd_a")
    dproj_b, dgain_b = _qknorm_bwd(raw_b, dqb, dkb, dvb, gains_b[:2], min(256, T), "qknorm_bwd_b")
    grads["q_norm_a"], grads["k_norm_a"] = dgain_a[0], dgain_a[1]
    grads["q_norm_b"], grads["k_norm_b"] = dgain_b[0], dgain_b[1]
    dproj_a = dproj_a.reshape(NSLAB, n16, 3 * C)

    dh2 = _mm(dproj_a, w_in_t, nt=False, tk=C, tm=tms, a_layout="slab", out_layout="view", name="dh2_a")
    dh2 = _mm(dproj_b, w_in_t, nt=False, tk=C, tm=tmm, a_layout="nat", out_layout="nat", resid=dh2, w_off=3, name="dh2_b")
    dh2 = _mm(df, w_f_t, nt=False, tk=HEAD, tm=tmm, a_layout="nat", out_layout="nat", resid=dh2, name="dh2_f")
    dx1, grads["mix_norm"] = _rms_bwd(x1, gm, dh2, dx2, tm, "mix_rms_bwd")
    bt = min(2048, T)
    dwt = _mm_tn(flat(dproj_a)[None], flat(h2a), bm=C, bn=bn, bt=bt, rows=6 * C + H, name="dw_a")
    dwt = _mm_tn(dproj_b[None], h2b, bm=C, bn=bn, bt=bt, rows=6 * C + H, m_off=3, into=dwt, name="dw_b")
    dwt = _mm_tn(df[None, :, :H], h2b, bm=H, bn=bn, bt=bt, rows=6 * C + H, m_off=6 * C // H, into=dwt, name="dw_f")
    grads["w_in_t"] = dwt[0]

    grad_x = ffn_backward("ffn1", x, g1, h1, gu1, act1, wts["ffn1_in"], wts["ffn1_out"], dx1)
    return loss_row, grad_x, grads


def _place():
    x, y, c = lax.axis_index("x"), lax.axis_index("y"), lax.axis_index("c")
    other_chips = [(1 - x, y), (x, 1 - y), (1 - x, 1 - y)]
    return x, y, c, other_chips


def _run_side(side, name):
    def body(*refs):
        si, so = len(side.ins), len(side.outs)
        side.start(refs[:si], refs[si:si + so], refs[si + so:])
        side.finish(refs[:si], refs[si:si + so], refs[si + so:])

    return pl.pallas_call(body, name=name, in_specs=[ANY] * len(side.ins), out_specs=[ANY] * len(side.outs),
                          out_shape=side.outs, scratch_shapes=side.sems)(*side.ins)


def _all_gather(shards):
    n = len(shards)

    def plan(ins, outs, sems):
        send_sems, recv_sems, local_sems = sems
        x, y, c, chips = _place()
        me, sibling = (x, y, c), (x, y, 1 - c)

        def copy(a, k, block, to, src=None):
            px, py, pc = block
            dst = outs[a].at[4 * px + 2 * py + pc]
            return pltpu.make_async_remote_copy(
                src_ref=dst if src is None else src, dst_ref=dst, send_sem=send_sems.at[7 * a + k],
                recv_sem=recv_sems.at[7 * a + k], device_id=to, device_id_type=MESH)

        mine = [pltpu.make_async_copy(ins[a], outs[a].at[4 * x + 2 * y + c], local_sems.at[a]) for a in range(n)]
        first = []
        for a in range(n):
            first.append(copy(a, 0, me, sibling, src=ins[a]))
            first += [copy(a, 1 + j, me, (*chip, c), src=ins[a]) for j, chip in enumerate(chips)]
        return copy, mine, first, me, sibling, c, chips

    def start(ins, outs, sems):
        _, mine, first, *_ = plan(ins, outs, sems)
        for cp in mine + first:
            cp.start()

    def finish(ins, outs, sems):
        copy, mine, first, me, sibling, c, chips = plan(ins, outs, sems)
        passed = []
        for a in range(n):
            for j, chip in enumerate(chips):
                copy(a, 1 + j, (*chip, c), me).wait_recv()
                fwd = copy(a, 4 + j, (*chip, c), sibling)
                fwd.start()
                passed.append(fwd)
        for a in range(n):
            copy(a, 0, sibling, me).wait_recv()
            for j, chip in enumerate(chips):
                copy(a, 4 + j, (*chip, 1 - c), me).wait_recv()
        for cp in first + passed:
            cp.wait_send()
        for cp in mine:
            cp.wait()

    return _Side(shards, [_sds((N_DEV,) + s.shape, s.dtype) for s in shards],
                 [pltpu.SemaphoreType.DMA((7 * n,)), pltpu.SemaphoreType.DMA((7 * n,)), pltpu.SemaphoreType.DMA((n,))],
                 start, finish)


def _all_gather_relayed(shards):
    n = len(shards)
    halves = [-(-(s.shape[0] // 2) // 16) * 16 for s in shards]

    def body_parts(ins, outs, sems):
        send_sems, recv_sems, local_sems = sems
        x, y, c, _ = _place()
        me, sib, xn, yn, dg = (x, y, c), (x, y, 1 - c), (1 - x, y, c), (x, 1 - y, c), (1 - x, 1 - y, c)

        def rows(a, block, part):
            px, py, pc = block
            whole = outs[a].at[4 * px + 2 * py + pc]
            if part is None:
                return whole
            return whole.at[pl.ds(0, halves[a])] if part == 0 else whole.at[pl.ds(halves[a], shards[a].shape[0] - halves[a])]

        def copy(a, k, block, part, to, src=None):
            dst = rows(a, block, part)
            return pltpu.make_async_remote_copy(
                src_ref=dst if src is None else src, dst_ref=dst, send_sem=send_sems.at[9 * a + k],
                recv_sem=recv_sems.at[9 * a + k], device_id=to, device_id_type=MESH)

        flip = lambda dev: (dev[0], dev[1], 1 - dev[2])
        mine = [pltpu.make_async_copy(ins[a], rows(a, me, None), local_sems.at[a]) for a in range(n)]
        own = [[copy(a, 0, me, None, sib, src=ins[a]), copy(a, 1, me, None, xn, src=ins[a]),
                copy(a, 2, me, None, yn, src=ins[a])] for a in range(n)]
        relays = lambda a: [(1, [copy(a, 3, xn, 0, yn), copy(a, 5, xn, None, sib)]),
                            (2, [copy(a, 4, yn, 1, xn), copy(a, 6, yn, None, sib)]),
                            (3, [copy(a, 7, dg, 0, sib)]), (4, [copy(a, 8, dg, 1, sib)])]
        lands = {0: (sib, None), 1: (xn, None), 2: (yn, None), 3: (dg, 0), 4: (dg, 1), 5: (flip(xn), None),
                 6: (flip(yn), None), 7: (flip(dg), 0), 8: (flip(dg), 1)}
        arrival = lambda a, k: copy(a, k, lands[k][0], lands[k][1], me)
        return mine, own, relays, arrival

    def start(ins, outs, sems):
        mine, own, _, _ = body_parts(ins, outs, sems)
        for cp in mine + [cp for per in own for cp in per]:
            cp.start()

    def finish(ins, outs, sems):
        mine, own, relays, arrival = body_parts(ins, outs, sems)
        sent = [cp for per in own for cp in per]
        relays = [relays(a) for a in range(n)]
        for stage in range(4):
            for a in range(n):
                after, passes = relays[a][stage]
                arrival(a, after).wait_recv()
                for cp in passes:
                    cp.start()
                sent += passes
        for a in range(n):
            for k in (0, 5, 6, 7, 8):
                arrival(a, k).wait_recv()
        for cp in sent:
            cp.wait_send()
        for cp in mine:
            cp.wait()

    return _Side(shards, [_sds((N_DEV,) + s.shape, s.dtype) for s in shards],
                 [pltpu.SemaphoreType.DMA((9 * n,)), pltpu.SemaphoreType.DMA((9 * n,)), pltpu.SemaphoreType.DMA((n,))],
                 start, finish)


def _exchange_in_chip(gs):
    n = len(gs)

    def copies(ins, outs, sems):
        x, y, c, _ = _place()
        return [pltpu.make_async_remote_copy(
            src_ref=ins[a].at[2 * q + 1 - c], dst_ref=outs[a].at[q], send_sem=sems[0].at[4 * a + q],
            recv_sem=sems[1].at[4 * a + q], device_id=(x, y, 1 - c), device_id_type=MESH)
            for a in range(n) for q in range(4)]

    def start(ins, outs, sems):
        for cp in copies(ins, outs, sems):
            cp.start()

    def finish(ins, outs, sems):
        for cp in copies(ins, outs, sems):
            cp.wait()

    return _Side(gs, [_sds((4,) + g.shape[1:], g.dtype) for g in gs],
                 [pltpu.SemaphoreType.DMA((4 * n,)), pltpu.SemaphoreType.DMA((4 * n,))], start, finish)


def _exchange_between_chips(ps):
    n = len(ps)

    def copies(ins, outs, sems):
        x, y, c, chips = _place()
        return [pltpu.make_async_remote_copy(
            src_ref=ins[a].at[2 * cx + cy], dst_ref=outs[a].at[j], send_sem=sems[0].at[3 * a + j],
            recv_sem=sems[1].at[3 * a + j], device_id=(cx, cy, c), device_id_type=MESH)
            for a in range(n) for j, (cx, cy) in enumerate(chips)]

    def start(ins, outs, sems):
        for cp in copies(ins, outs, sems):
            cp.start()

    def finish(ins, outs, sems):
        for cp in copies(ins, outs, sems):
            cp.wait()

    return _Side(ps, [_sds((3,) + p.shape[1:], p.dtype) for p in ps],
                 [pltpu.SemaphoreType.DMA((3 * n,)), pltpu.SemaphoreType.DMA((3 * n,))], start, finish)


def _all_reduce_small(v, name):
    R = v.shape[0]

    def body(v_ref, sum_ref, all_ref, send_sems, recv_sems):
        x, y, c, _ = _place()
        k = 4 * x + 2 * y + c
        all_ref[k] = v_ref[...]
        copies = []
        for rel in range(1, N_DEV):
            fx, fy, fc = (rel >> 2) & 1, (rel >> 1) & 1, rel & 1
            peer = (1 - x if fx else x, 1 - y if fy else y, 1 - c if fc else c)
            copies.append(pltpu.make_async_remote_copy(
                src_ref=v_ref, dst_ref=all_ref.at[k], send_sem=send_sems.at[rel - 1], recv_sem=recv_sems.at[rel - 1],
                device_id=peer, device_id_type=MESH))
        for cp in copies:
            cp.start()
        for rel in range(1, N_DEV):
            fx, fy, fc = (rel >> 2) & 1, (rel >> 1) & 1, rel & 1
            src = 4 * (1 - x if fx else x) + 2 * (1 - y if fy else y) + (1 - c if fc else c)
            pltpu.make_async_remote_copy(
                src_ref=v_ref, dst_ref=all_ref.at[src], send_sem=send_sems.at[rel - 1], recv_sem=recv_sems.at[rel - 1],
                device_id=(x, y, c), device_id_type=MESH).wait_recv()
        for cp in copies:
            cp.wait_send()
        tot = all_ref[0]
        for d in range(1, N_DEV):
            tot = tot + all_ref[d]
        sum_ref[...] = tot

    vm = pl.BlockSpec(memory_space=pltpu.VMEM)
    return pl.pallas_call(
        body, name=name, in_specs=[vm], out_specs=[vm, vm],
        out_shape=[_sds((R, HEAD), F32), _sds((N_DEV, R, HEAD), F32)],
        scratch_shapes=[pltpu.SemaphoreType.DMA((N_DEV - 1,)), pltpu.SemaphoreType.DMA((N_DEV - 1,))],
    )(v)[0]


def _tiles(rows, cols):
    tr = next((cand for cand in (688, 512, 256) if rows % cand == 0), rows)
    tc = 512 if (cols % 512 == 0 and tr * cols * 4 > (2 << 20)) else cols
    return tr, tc


def _chip_sum(g, r1, core, name):
    _, R, Cc = g.shape
    tr, tc = _tiles(R, Cc)

    def body(core_ref, g_ref, r_ref, p_ref):
        p_ref[...] = (g_ref[...] + r_ref[...]).astype(BF16)

    blk = lambda f: pl.BlockSpec((None, tr, tc), f)
    return pl.pallas_call(
        body, name=name,
        grid_spec=pltpu.PrefetchScalarGridSpec(
            num_scalar_prefetch=1, grid=(4, R // tr, Cc // tc),
            in_specs=[blk(lambda q, i, j, core: (2 * q + core[0], i, j)), blk(lambda q, i, j, core: (q, i, j))],
            out_specs=blk(lambda q, i, j, core: (q, i, j))),
        out_shape=_sds((4, R, Cc), BF16), compiler_params=_cparams(),
    )(core, g, r1)


def _adamw_update(gv, w_ref, m_ref, v_ref, d_ref, nm_ref, nv_ref):
    nm = B1 * m_ref[...] + (1.0 - B1) * gv
    nv = B2 * v_ref[...] + (1.0 - B2) * jnp.square(gv)
    m_hat = nm / (1.0 - B1 ** STEP)
    v_hat = nv / (1.0 - B2 ** STEP)
    d_ref[...] = -LR * (m_hat / (jnp.sqrt(v_hat) + EPS) + WD * w_ref[...])
    nm_ref[...] = nm
    nv_ref[...] = nv


def _reduce_adamw(g, r1, r2, where, w, m, v, name):
    _, R, Cc = g.shape
    tr, tc = _tiles(R, Cc)

    def body(where_ref, g_ref, r1_ref, r2_ref, w_ref, m_ref, v_ref, o_ref, d_ref, nm_ref, nv_ref):
        gv = ((g_ref[...] + r1_ref[...]) + r2_ref[0].astype(F32)) + (r2_ref[1].astype(F32) + r2_ref[2].astype(F32))
        o_ref[...] = gv
        _adamw_update(gv, w_ref, m_ref, v_ref, d_ref, nm_ref, nv_ref)

    blk = pl.BlockSpec((tr, tc), lambda i, j, w: (i, j))
    return pl.pallas_call(
        body, name=name,
        grid_spec=pltpu.PrefetchScalarGridSpec(
            num_scalar_prefetch=1, grid=(R // tr, Cc // tc),
            in_specs=[pl.BlockSpec((None, tr, tc), lambda i, j, w: (w[0], i, j)),
                      pl.BlockSpec((None, tr, tc), lambda i, j, w: (w[1], i, j)),
                      pl.BlockSpec((3, tr, tc), lambda i, j, w: (0, i, j)), blk, blk, blk],
            out_specs=[blk] * 4),
        out_shape=[_sds((R, Cc), F32)] * 4, compiler_params=_cparams(),
    )(where, g, r1, r2, w, m, v)


def _adamw(w, g, m, v, name):
    R, Cc = w.shape
    tr, tc = _tiles(R, Cc)

    def body(w_ref, g_ref, m_ref, v_ref, d_ref, nm_ref, nv_ref):
        _adamw_update(g_ref[...], w_ref, m_ref, v_ref, d_ref, nm_ref, nv_ref)

    blk = pl.BlockSpec((tr, tc), lambda i, j: (i, j))
    return pl.pallas_call(
        body, name=name, grid=(R // tr, Cc // tc), in_specs=[blk] * 4, out_specs=[blk] * 3,
        out_shape=[_sds((R, Cc), F32)] * 3, compiler_params=_cparams(),
    )(w, g, m, v)


SMALL = ("ffn1_norm", "mix_norm", "ffn2_norm", "q_norm_a", "k_norm_a", "q_norm_b", "k_norm_b", "forget_bias",
         "rel_bias_table")
LARGE = ("ffn1_w_in", "ffn1_w_out", "w_in", "w_out", "ffn2_w_in", "ffn2_w_out")
ORDER = ("ffn1_norm", "ffn1_w_in", "ffn1_w_out", "mix_norm", "w_in", "q_norm_a", "k_norm_a", "q_norm_b", "k_norm_b",
         "forget_bias", "rel_bias_table", "w_out", "ffn2_norm", "ffn2_w_in", "ffn2_w_out")


def _pack_small(vals):
    rows = []
    for name in SMALL:
        flat = vals[name].reshape(-1)
        pad = (-flat.shape[0]) % HEAD
        rows.append(jnp.pad(flat, (0, pad)).reshape(-1, HEAD))
    return jnp.concatenate(rows, axis=0)


def _unpack_small(packed, like):
    out, r = {}, 0
    for name in SMALL:
        size = like[name].size
        nrow = -(-size // HEAD)
        out[name] = packed[r:r + nrow].reshape(-1)[:size].reshape(like[name].shape)
        r += nrow
    return out


def kernel(x, ffn1_norm, ffn1_w_in, ffn1_w_out, mix_norm, w_in, q_norm_a, k_norm_a, q_norm_b, k_norm_b, forget_bias, rel_bias_table, w_out, ffn2_norm, ffn2_w_in, ffn2_w_out, loss_target, m_ffn1_norm, m_ffn1_w_in, m_ffn1_w_out, m_mix_norm, m_w_in, m_q_norm_a, m_k_norm_a, m_q_norm_b, m_k_norm_b, m_forget_bias, m_rel_bias_table, m_w_out, m_ffn2_norm, m_ffn2_w_in, m_ffn2_w_out, v_ffn1_norm, v_ffn1_w_in, v_ffn1_w_out, v_mix_norm, v_w_in, v_q_norm_a, v_k_norm_a, v_q_norm_b, v_k_norm_b, v_forget_bias, v_rel_bias_table, v_w_out, v_ffn2_norm, v_ffn2_w_in, v_ffn2_w_out):
    w = dict(ffn1_norm=ffn1_norm, ffn1_w_in=ffn1_w_in, ffn1_w_out=ffn1_w_out, mix_norm=mix_norm, w_in=w_in,
             q_norm_a=q_norm_a, k_norm_a=k_norm_a, q_norm_b=q_norm_b, k_norm_b=k_norm_b, forget_bias=forget_bias,
             rel_bias_table=rel_bias_table, w_out=w_out, ffn2_norm=ffn2_norm, ffn2_w_in=ffn2_w_in, ffn2_w_out=ffn2_w_out)
    m = dict(ffn1_norm=m_ffn1_norm, ffn1_w_in=m_ffn1_w_in, ffn1_w_out=m_ffn1_w_out, mix_norm=m_mix_norm, w_in=m_w_in,
             q_norm_a=m_q_norm_a, k_norm_a=m_k_norm_a, q_norm_b=m_q_norm_b, k_norm_b=m_k_norm_b,
             forget_bias=m_forget_bias, rel_bias_table=m_rel_bias_table, w_out=m_w_out, ffn2_norm=m_ffn2_norm,
             ffn2_w_in=m_ffn2_w_in, ffn2_w_out=m_ffn2_w_out)
    v = dict(ffn1_norm=v_ffn1_norm, ffn1_w_in=v_ffn1_w_in, ffn1_w_out=v_ffn1_w_out, mix_norm=v_mix_norm, w_in=v_w_in,
             q_norm_a=v_q_norm_a, k_norm_a=v_k_norm_a, q_norm_b=v_q_norm_b, k_norm_b=v_k_norm_b,
             forget_bias=v_forget_bias, rel_bias_table=v_rel_bias_table, w_out=v_w_out, ffn2_norm=v_ffn2_norm,
             ffn2_w_in=v_ffn2_w_in, ffn2_w_out=v_ffn2_w_out)
    T, D = x.shape[1], x.shape[2]
    C = D // 2
    H = C // HEAD
    ff_shard = ffn1_w_out.shape[1]

    f1i, = _run_side(_all_gather_relayed([ffn1_w_in[0].T.astype(BF16)]), "gather_ffn1")
    wts = dict(ffn1_in=f1i.reshape(2, N_DEV, ff_shard, D))
    xi, yi, ci = lax.axis_index("x"), lax.axis_index("y"), lax.axis_index("c")
    core = jnp.reshape(ci, (1,)).astype(jnp.int32)
    where = jnp.stack([4 * xi + 2 * yi + ci, 2 * xi + yi]).astype(jnp.int32)
    gs, r1, ps, r2 = {}, {}, {}, {}

    def shard(name):
        name, _, part = name.partition(":")
        s = (w[name][0].T if name.endswith("w_in") else w[name][0]).astype(BF16)
        first = -(-(s.shape[0] // 2) // 16) * 16
        return {"": s, "first": s[:first], "rest": s[first:]}[part]

    def by_destination(name, grads):
        key = name + "_t" if name.endswith("w_in") else name
        gs[name] = grads[key].reshape(N_DEV, -1, D)
        return grads.get(key + "_bf16", grads[key]).reshape(N_DEV, -1, D)

    def chip_sums(names):
        for name in names:
            ps[name] = _chip_sum(gs[name], r1[name], core, "chip_sum_" + name)
        return [ps[name] for name in names]

    class Plan:
        carried = {"ffn1_up": ("gather", ("ffn1_w_out", "w_in:first")), "ffn1_down": ("gather", ("w_in:rest", "w_out")),
                   "dil_fwd": ("gather", ("ffn2_w_out",)), "fox_fwd": ("gather", ("ffn2_w_in",)),
                   "dil_bwd": ("in_chip", ("ffn2_w_in", "ffn2_w_out")), "fox_bwd": ("between", ("ffn2_w_in", "ffn2_w_out")),
                   "ffn1_bwd": ("in_chip", ("w_in", "w_out")), "ffn1_dwin": ("between", ("w_in", "w_out")),
                   "ffn1_rms_bwd": ("in_chip", ("ffn1_w_in",)), "ffn1_dwout": ("between", ("ffn1_w_in",))}

        def before(self, host, wts, grads):
            if host not in self.carried:
                return None
            kind, names = self.carried[host]
            if kind == "gather":
                return _all_gather([shard(n) for n in names])
            if kind == "in_chip":
                return _exchange_in_chip([by_destination(n, grads) for n in names])
            return _exchange_between_chips(chip_sums(names))

        def after(self, host, res, wts, grads):
            kind, names = self.carried[host]
            if host == "ffn1_up":
                wts.update(ffn1_out=res[0], w_in_first_rows=res[1])
            elif host == "ffn1_down":
                w_in_t = jnp.concatenate([wts.pop("w_in_first_rows"), res[0]], axis=1).reshape(-1, D)
                wts.update(w_in_t=w_in_t, w_f_t=jnp.pad(w_in_t[6 * C:], ((0, HEAD - H), (0, 0))),
                           w_o=res[1].reshape(2 * C, D))
            elif host == "dil_fwd":
                wts.update(ffn2_out=res[0])
            elif host == "fox_fwd":
                wts.update(ffn2_in=res[0].reshape(2, N_DEV, ff_shard, D))
            else:
                (r1 if kind == "in_chip" else r2).update(zip(names, res))

    small = {name: w[name] for name in SMALL}
    loss_row, grad_x, grads = _local_step(x[0], loss_target[0], small, wts, Plan())

    tail = ("ffn1_w_out",)
    r1[tail[0]], = _run_side(_exchange_in_chip([by_destination(tail[0], grads)]), "reduce_in_chip_tail")
    r2.update(zip(tail, _run_side(_exchange_between_chips(chip_sums(tail)), "reduce_between_chips_tail")))

    packed = _pack_small(grads)
    nsmall = packed.shape[0]
    packed = jnp.concatenate([packed, loss_row, jnp.zeros(((-nsmall - 1) % 8, HEAD), F32)], axis=0)
    reduced = _all_reduce_small(packed, "reduce_small")
    loss = reduced[nsmall, 0]
    g_small = _unpack_small(reduced[:nsmall], small)

    grad, delta, new_m, new_v = dict(g_small), {}, {}, {}
    for name in LARGE:
        to = (lambda t: t[0].T) if name.endswith("w_in") else (lambda t: t[0])
        back = (lambda t: t.T[None]) if name.endswith("w_in") else (lambda t: t[None])
        res = _reduce_adamw(gs[name], r1[name], r2[name], where, to(w[name]), to(m[name]), to(v[name]), "adamw_" + name)
        grad[name], delta[name], new_m[name], new_v[name] = (back(t) for t in res)
    d, nm, nv = _adamw(_pack_small(w), reduced[:nsmall], _pack_small(m), _pack_small(v), "adamw_small")
    delta.update(_unpack_small(d, small))
    new_m.update(_unpack_small(nm, small))
    new_v.update(_unpack_small(nv, small))
    return (loss, grad_x[None], *[grad[n] for n in ORDER], *[delta[n] for n in ORDER],
            *[new_m[n] for n in ORDER], *[new_v[n] for n in ORDER])
```

```python
import functools
import math

import numpy as np
import jax
import jax.numpy as jnp
from jax import lax
from jax.experimental import pallas as pl
from jax.experimental.pallas import tpu as pltpu

F32, BF16 = jnp.float32, jnp.bfloat16
HEAD = 128
NSLAB = 16
BLK = 128
DILATIONS = (1, 4, 16)
NUM_BUCKETS, MAX_DISTANCE = 32, 2048
RMS_EPS = 1e-6
NEG = -1e30
SCALE = HEAD ** -0.5
LR, B1, B2, EPS, WD, STEP = 0.001, 0.9, 0.999, 1e-08, 0.01, 10
N_DEV = 8
VMEM_LIMIT_BYTES = 56 << 20
MESH = pl.DeviceIdType.MESH


def _cparams(**kw):
    return pltpu.CompilerParams(vmem_limit_bytes=VMEM_LIMIT_BYTES, **kw)


def _nn(a, b):
    return jnp.dot(a, b, preferred_element_type=F32)


def _nt(a, b):
    return lax.dot_general(a, b, (((1,), (1,)), ((), ())), preferred_element_type=F32)


def _tn(a, b):
    return lax.dot_general(a, b, (((0,), (0,)), ((), ())), preferred_element_type=F32)


def _sds(shape, dtype):
    return jax.ShapeDtypeStruct(shape, dtype)


ANY = pl.BlockSpec(memory_space=pl.ANY)


class _Side:
    def __init__(self, ins, outs, sems, start, finish):
        self.ins, self.outs, self.sems, self.start, self.finish = list(ins), list(outs), list(sems), start, finish


def _call(body, *, name, grid, in_specs, out_specs, out_shape, args, scratch_shapes=(), side=None):
    in_specs, out_specs, out_shape = list(in_specs), list(out_specs), list(out_shape)
    scratch_shapes = list(scratch_shapes)
    if side is None:
        return pl.pallas_call(body, name=name, grid=grid, in_specs=in_specs, out_specs=out_specs, out_shape=out_shape,
                              scratch_shapes=scratch_shapes, compiler_params=_cparams())(*args)
    ni, no, ns = len(args), len(out_shape), len(scratch_shapes)
    si, so = len(side.ins), len(side.outs)

    def fused(*refs):
        h_in, s_in = refs[:ni], refs[ni:ni + si]
        h_out, s_out = refs[ni + si:ni + si + no], refs[ni + si + no:ni + si + no + so]
        h_scr, s_sem = refs[ni + si + no + so:ni + si + no + so + ns], refs[ni + si + no + so + ns:]
        ids = [pl.program_id(k) for k in range(len(grid))]
        first = functools.reduce(jnp.logical_and, [i == 0 for i in ids])
        last = functools.reduce(jnp.logical_and, [i == n - 1 for i, n in zip(ids, grid)])

        @pl.when(first)
        def _():
            side.start(s_in, s_out, s_sem)

        body(*h_in, *h_out, *h_scr)

        @pl.when(last)
        def _():
            side.finish(s_in, s_out, s_sem)

    res = pl.pallas_call(
        fused, name=name, grid=grid, in_specs=in_specs + [ANY] * si, out_specs=out_specs + [ANY] * so,
        out_shape=out_shape + side.outs, scratch_shapes=scratch_shapes + side.sems, compiler_params=_cparams(),
    )(*args, *side.ins)
    return list(res[:no]), list(res[no:])


def _ffn_fwd(x, g, win, wout, tm, name, side=None, target=None):
    T, D = x.shape
    nc, tf = win.shape[1], win.shape[2]
    down, loss = wout is not None, target is not None
    assert down or not loss

    def body(x_ref, g_ref, win_ref, *refs):
        ins, outs = refs[:down + loss], refs[down + loss:]
        h_ref, gu_ref, act_ref = outs[down:down + 3]
        i, j = pl.program_id(0), pl.program_id(1)

        @pl.when(j == 0)
        def _():
            xv = x_ref[...]
            r = lax.rsqrt(jnp.mean(xv * xv, axis=-1, keepdims=True) + RMS_EPS)
            h_ref[...] = (xv * r * g_ref[...]).astype(BF16)
            if down:
                outs[0][...] = jnp.zeros_like(outs[0])

        if loss:
            @pl.when((i == 0) & (j == 0))
            def _():
                outs[-1][...] = jnp.zeros_like(outs[-1])

        hb = h_ref[...]
        gt = _nt(win_ref[0], hb)
        up = _nt(win_ref[1], hb)
        gu_ref[0] = gt.astype(BF16)
        gu_ref[1] = up.astype(BF16)
        act = (gt * jax.nn.sigmoid(gt) * up).astype(BF16)
        act_ref[...] = act
        if down:
            y_ref = outs[0]
            y_ref[...] += _tn(act, ins[0][...])

            @pl.when(j == nc - 1)
            def _():
                y = x_ref[...] + 0.5 * y_ref[...]
                if not loss:
                    y_ref[...] = y
                    return
                err = y - ins[1][...]
                y_ref[...] = err * (1.0 / D)
                tot = 0.5 * jnp.sum(jnp.mean(err * err, axis=-1, keepdims=True), axis=0, keepdims=True)
                lane = lax.broadcasted_iota(jnp.int32, (1, HEAD), 1)
                outs[-1][...] += jnp.where(lane == 0, tot, 0.0)

    row = pl.BlockSpec((tm, D), lambda i, j: (i, 0))
    return _call(
        body, name=name, grid=(T // tm, nc), side=side,
        in_specs=[row, pl.BlockSpec((1, D), lambda i, j: (0, 0)),
                  pl.BlockSpec((2, None, tf, D), lambda i, j: (0, j, 0, 0))]
        + ([pl.BlockSpec((None, tf, D), lambda i, j: (j, 0, 0))] if down else []) + ([row] if loss else []),
        out_specs=([row] if down else [])
        + [row, pl.BlockSpec((2, None, tf, tm), lambda i, j: (0, j, 0, i)),
           pl.BlockSpec((None, tf, tm), lambda i, j: (j, 0, i))]
        + ([pl.BlockSpec((1, HEAD), lambda i, j: (0, 0))] if loss else []),
        out_shape=([_sds((T, D), F32)] if down else [])
        + [_sds((T, D), BF16), _sds((2, nc, tf, T), BF16), _sds((nc, tf, T), BF16)]
        + ([_sds((1, HEAD), F32)] if loss else []),
        args=(x, g, win) + ((wout,) if down else ()) + ((target,) if loss else ()))


def _ffn_down(x, act, wout, tm, name, side=None):
    T, D = x.shape
    nc, tf = wout.shape[0], wout.shape[1]

    def body(x_ref, act_ref, wout_ref, y_ref):
        j = pl.program_id(1)

        @pl.when(j == 0)
        def _():
            y_ref[...] = jnp.zeros_like(y_ref)

        y_ref[...] += _tn(act_ref[...], wout_ref[...])

        @pl.when(j == nc - 1)
        def _():
            y_ref[...] = x_ref[...] + 0.5 * y_ref[...]

    row = pl.BlockSpec((tm, D), lambda i, j: (i, 0))
    res = _call(
        body, name=name, grid=(T // tm, nc), side=side,
        in_specs=[row, pl.BlockSpec((None, tf, tm), lambda i, j: (j, 0, i)),
                  pl.BlockSpec((None, tf, D), lambda i, j: (j, 0, 0))],
        out_specs=[row], out_shape=[_sds((T, D), F32)], args=(x, act, wout))
    return res[0] if side is None else (res[0][0], res[1])


def _ffn_bwd(dy, gu, win, wout, tm, name, side=None):
    T, D = dy.shape
    nc, tf = wout.shape[0], wout.shape[1]

    def body(dy_ref, gu_ref, win_ref, wout_ref, dh_ref, dgu_ref, dyb_ref):
        j = pl.program_id(1)

        @pl.when(j == 0)
        def _():
            dh_ref[...] = jnp.zeros_like(dh_ref)
            dyb_ref[...] = (0.5 * dy_ref[...]).astype(BF16)

        dact = _nt(wout_ref[...], dyb_ref[...])
        gt = gu_ref[0].astype(F32)
        up = gu_ref[1].astype(F32)
        s = jax.nn.sigmoid(gt)
        dgb = (dact * up * (s * (1.0 + gt * (1.0 - s)))).astype(BF16)
        dub = (dact * (gt * s)).astype(BF16)
        dgu_ref[0] = dgb
        dgu_ref[1] = dub
        dh_ref[...] += _tn(dgb, win_ref[0]) + _tn(dub, win_ref[1])

    return _call(
        body, name=name, grid=(T // tm, nc), side=side,
        in_specs=[pl.BlockSpec((tm, D), lambda i, j: (i, 0)),
                  pl.BlockSpec((2, None, tf, tm), lambda i, j: (0, j, 0, i)),
                  pl.BlockSpec((2, None, tf, D), lambda i, j: (0, j, 0, 0)),
                  pl.BlockSpec((None, tf, D), lambda i, j: (j, 0, 0))],
        out_specs=[pl.BlockSpec((tm, D), lambda i, j: (i, 0)),
                   pl.BlockSpec((2, None, tf, tm), lambda i, j: (0, j, 0, i)),
                   pl.BlockSpec((tm, D), lambda i, j: (i, 0))],
        out_shape=[_sds((T, D), F32), _sds((2, nc, tf, T), BF16), _sds((T, D), BF16)],
        args=(dy, gu, win, wout))


def _rms_bwd(x, g, dh, dres, tm, name, side=None):
    T, D = x.shape

    def body(x_ref, g_ref, dh_ref, dres_ref, dx_ref, dg_ref):
        @pl.when(pl.program_id(0) == 0)
        def _():
            dg_ref[...] = jnp.zeros_like(dg_ref)

        xv = x_ref[...]
        r = lax.rsqrt(jnp.mean(xv * xv, axis=-1, keepdims=True) + RMS_EPS)
        xhat = xv * r
        dh = dh_ref[...]
        gd = dh * g_ref[...]
        dx_ref[...] = dres_ref[...] + r * (gd - xhat * jnp.mean(gd * xhat, axis=-1, keepdims=True))
        dg_ref[...] += jnp.sum(dh * xhat, axis=0, keepdims=True)

    row = pl.BlockSpec((tm, D), lambda i: (i, 0))
    one = pl.BlockSpec((1, D), lambda i: (0, 0))
    return _call(body, name=name, grid=(T // tm,), side=side, in_specs=[row, one, row, row], out_specs=[row, one],
                 out_shape=[_sds((T, D), F32), _sds((1, D), F32)], args=(x, g, dh, dres))


def _mm_tn(a, b, *, bm, bn, bt, name, b_slabs=False, side=None, rows=None, m_off=0, into=None, a_rows=False,
           twin=False):
    nz, T, M = (a.shape[0], a.shape[2], a.shape[1]) if a_rows else a.shape
    if b_slabs:
        N = b.shape[1] // NSLAB
        assert bt == T // NSLAB
        b_spec = pl.BlockSpec((bt, bn), lambda n, z, m, t: (0, t * (N // bn) + n))
    else:
        N = b.shape[1]
        b_spec = pl.BlockSpec((bt, bn), lambda n, z, m, t: (t, n))
    assert M % bm == 0 and N % bn == 0 and T % bt == 0, (M, bm, N, bn, T, bt)

    def body(a_ref, b_ref, *rest):
        c_ref = rest[-2] if twin else rest[-1]

        @pl.when(pl.program_id(3) == 0)
        def _():
            c_ref[...] = jnp.zeros_like(c_ref)

        ab, bb = a_ref[...].astype(BF16), b_ref[...].astype(BF16)
        c_ref[...] += _nn(ab, bb) if a_rows else _tn(ab, bb)
        if twin:
            @pl.when(pl.program_id(3) == T // bt - 1)
            def _():
                rest[-1][...] = c_ref[...].astype(BF16)

    grid = (N // bn, nz, M // bm, T // bt)
    a_spec = (pl.BlockSpec((None, bm, bt), lambda n, z, m, t: (z, m, t)) if a_rows
              else pl.BlockSpec((None, bt, bm), lambda n, z, m, t: (z, t, m)))
    in_specs = [a_spec, b_spec]
    out_spec = pl.BlockSpec((None, bm, bn), lambda n, z, m, t: (z, m + m_off, n))
    out_shape = _sds((nz, M if rows is None else rows, N), F32)
    if into is not None:
        assert side is None and not twin and into.shape == out_shape.shape
        return pl.pallas_call(body, name=name, grid=grid, in_specs=in_specs + [ANY], out_specs=out_spec,
                              out_shape=out_shape, input_output_aliases={2: 0}, compiler_params=_cparams())(a, b, into)
    outs = [out_shape] + ([_sds(out_shape.shape, BF16)] if twin else [])
    res = _call(body, name=name, grid=grid, side=side, in_specs=in_specs, out_specs=[out_spec] * len(outs),
                out_shape=outs, args=(a, b))
    mine = res if side is None else res[0]
    mine = tuple(mine) if twin else mine[0]
    return mine if side is None else (mine, res[1])


def _tok_spec(layout, tm, n16, C, bc, colmap):
    if layout == "nat":
        return pl.BlockSpec((tm, bc), lambda i, k: (i, colmap(k)))
    assert tm % n16 == 0
    if layout == "slab":
        return pl.BlockSpec((tm // n16, n16, bc), lambda i, k: (i, 0, colmap(k)))
    assert bc == C
    return pl.BlockSpec((n16, (tm // n16) * C), lambda i, k: (0, i))


def _tok_load(ref, layout, sp):
    if layout == "nat":
        return ref[...]
    if layout == "slab":
        return ref[...].reshape(-1, ref.shape[-1])
    c = ref.shape[1] // sp
    return jnp.concatenate([ref[:, s * c:(s + 1) * c] for s in range(sp)], axis=0)


def _tok_store(ref, layout, sp, val, cols=None, accumulate=False):
    def put(idx, v):
        if accumulate:
            ref[idx] += v
        else:
            ref[idx] = v

    lanes = slice(None) if cols is None else slice(cols[0], cols[0] + cols[1])
    if layout == "nat":
        put((slice(None), lanes), val)
    elif layout == "slab":
        put((slice(None), slice(None), lanes), val.reshape(sp, ref.shape[1], val.shape[-1]))
    else:
        assert cols is None
        c, n = ref.shape[1] // sp, ref.shape[0]
        for s in range(sp):
            put((slice(None), slice(s * c, (s + 1) * c)), val[s * n:(s + 1) * n])


def _proj(x, g, wt, gains, modes, *, tn, w_off, slabs, tm, normed_dtype, name, small_wt=None):
    T, D = x.shape
    ntile = len(modes)
    N = ntile * tn
    n16 = T // NSLAB
    in_layout, out_layout = ("view", "slab") if slabs else ("nat", "nat")
    sp = tm // n16
    extra = int(small_wt is not None)
    assert not (extra and slabs)
    x_in = x.reshape(n16, NSLAB * D) if slabs else x
    x_spec = _tok_spec(in_layout, tm, n16, D, D, lambda n: 0)
    oshape = lambda c: (NSLAB, n16, c) if slabs else (T, c)
    ospec = lambda bc, cm: _tok_spec(out_layout, tm, n16, None, bc, cm)

    def body(x_ref, g_ref, w_ref, gains_ref, *refs):
        raw_ref, nrm_ref, h_ref = refs[-3 - extra:len(refs) - extra]
        n = pl.program_id(1)

        @pl.when(n == 0)
        def _():
            xv = _tok_load(x_ref, in_layout, sp)
            r = lax.rsqrt(jnp.mean(xv * xv, axis=-1, keepdims=True) + RMS_EPS)
            hb = (xv * r * g_ref[...]).astype(BF16)
            _tok_store(h_ref, out_layout, sp, hb)
            if extra:
                refs[-1][...] = _nt(hb, refs[0][...])

        y = _nt(_tok_load(h_ref, out_layout, sp), w_ref[...])
        _tok_store(raw_ref, out_layout, sp, y)
        for t, mode in enumerate(modes):
            @pl.when(n == t)
            def _(t=t, mode=mode):
                if not mode:
                    _tok_store(nrm_ref, out_layout, sp, y.astype(nrm_ref.dtype))
                    return
                gain = gains_ref[t]
                for k in range(tn // HEAD):
                    yk = y[:, k * HEAD:(k + 1) * HEAD]
                    r = lax.rsqrt(jnp.mean(yk * yk, axis=-1, keepdims=True) + RMS_EPS)
                    _tok_store(nrm_ref, out_layout, sp, (yk * r * gain).astype(nrm_ref.dtype), cols=(k * HEAD, HEAD))

    return pl.pallas_call(
        body, name=name, grid=(T // tm, ntile),
        in_specs=[x_spec, pl.BlockSpec((1, D), lambda i, n: (0, 0)),
                  pl.BlockSpec((tn, D), lambda i, n: (n + w_off, 0)),
                  pl.BlockSpec((ntile, 1, HEAD), lambda i, n: (0, 0, 0))]
        + ([pl.BlockSpec((HEAD, D), lambda i, n: (0, 0))] if extra else []),
        out_specs=[ospec(tn, lambda n: n), ospec(tn, lambda n: n), ospec(D, lambda n: 0)]
        + ([ospec(HEAD, lambda n: 0)] if extra else []),
        out_shape=[_sds(oshape(N), F32), _sds(oshape(N), normed_dtype), _sds(oshape(D), BF16)]
        + ([_sds((T, HEAD), F32)] if extra else []),
        compiler_params=_cparams(),
    )(x_in, g, wt, gains, *([small_wt] if extra else []))


def _mm(a, w, *, nt, tk, tm, a_layout, out_layout, resid=None, name, w_off=0, n_out=None, small=None):
    if a_layout == "slab":
        T, K = a.shape[0] * a.shape[1], a.shape[2]
    else:
        T, K = a.shape
    N = (w.shape[0] if nt else w.shape[1]) if n_out is None else n_out
    n16 = T // NSLAB
    nk = K // tk
    sp = tm // n16
    a_in = a.reshape(n16, NSLAB * K) if a_layout == "view" else a
    w_spec = (pl.BlockSpec((N, tk), lambda i, k: (w_off, k)) if nt
              else pl.BlockSpec((tk, N), lambda i, k: (k + w_off, 0)))
    o_spec = _tok_spec(out_layout, tm, n16, N, N, lambda k: 0)
    oshape = {"nat": (T, N), "slab": (NSLAB, n16, N), "view": (n16, NSLAB * N)}[out_layout]
    has_resid = resid is not None

    def body(*refs):
        a_ref, w_ref = refs[0], refs[1]
        o_ref = refs[-1]
        k = pl.program_id(1)

        @pl.when(k == 0)
        def _():
            o_ref[...] = refs[2][...] if has_resid else jnp.zeros_like(o_ref)
            if small is not None:
                o_ref[...] += _nn(refs[-3][...].astype(BF16), refs[-2][...])

        ab = _tok_load(a_ref, a_layout, sp).astype(BF16)
        _tok_store(o_ref, out_layout, sp, _nt(ab, w_ref[...]) if nt else _nn(ab, w_ref[...]), accumulate=True)

    ins = [a_in, w]
    in_specs = [_tok_spec(a_layout, tm, n16, K, tk, lambda k: k), w_spec]
    if has_resid:
        ins.append(resid.reshape(n16, NSLAB * N) if out_layout == "view" else resid)
        in_specs.append(o_spec)
    if small is not None:
        assert out_layout == "nat" and small[1].shape == (HEAD, N)
        ins += list(small)
        in_specs += [pl.BlockSpec((tm, HEAD), lambda i, k: (i, 0)), pl.BlockSpec((HEAD, N), lambda i, k: (0, 0))]
    out = pl.pallas_call(
        body, name=name, grid=(T // tm, nk), in_specs=in_specs, out_specs=o_spec,
        out_shape=_sds(oshape, F32), compiler_params=_cparams(),
    )(*ins)
    return out.reshape(T, N) if out_layout == "view" else out


def _log_sigmoid(z):
    return jnp.minimum(z, 0.0) - jnp.log(1.0 + jnp.exp(-jnp.abs(z)))


def _fox_gate_fwd(f_raw, fbias, name):
    T = f_raw.shape[0]
    cb = 256

    def body(f_ref, b_ref, c_ref):
        row = lax.broadcasted_iota(jnp.int32, (cb, cb), 0)
        col = lax.broadcasted_iota(jnp.int32, (cb, cb), 1)
        tri = (col <= row).astype(F32)
        carry = jnp.zeros((1, HEAD), F32)
        for i in range(T // cb):
            lf = _log_sigmoid(f_ref[i * cb:(i + 1) * cb, :] + b_ref[...])
            c = jnp.dot(tri, lf, preferred_element_type=F32, precision=lax.Precision.HIGHEST) + carry
            c_ref[i * cb:(i + 1) * cb, :] = c
            carry = c[cb - 1:cb, :]

    return pl.pallas_call(body, name=name, out_shape=_sds((T, HEAD), F32), compiler_params=_cparams())(f_raw, fbias)


def _fox_gate_bwd(f_raw, fbias, dc, name):
    T = f_raw.shape[0]
    cb = 256

    def body(f_ref, b_ref, dc_ref, df_ref, db_ref):
        row = lax.broadcasted_iota(jnp.int32, (cb, cb), 0)
        col = lax.broadcasted_iota(jnp.int32, (cb, cb), 1)
        tri = (col >= row).astype(F32)
        carry = jnp.zeros((1, HEAD), F32)
        dbias = jnp.zeros((1, HEAD), F32)
        for i in reversed(range(T // cb)):
            dlf = jnp.dot(tri, dc_ref[i * cb:(i + 1) * cb, :], preferred_element_type=F32,
                          precision=lax.Precision.HIGHEST) + carry
            carry = dlf[0:1, :]
            z = f_ref[i * cb:(i + 1) * cb, :] + b_ref[...]
            df = dlf * jax.nn.sigmoid(-z)
            df_ref[i * cb:(i + 1) * cb, :] = df
            dbias = dbias + jnp.sum(df, axis=0, keepdims=True)
        db_ref[...] = dbias

    return pl.pallas_call(body, name=name, out_shape=[_sds((T, HEAD), F32), _sds((1, HEAD), F32)],
                          compiler_params=_cparams())(f_raw, fbias, dc)


def _fox_fwd(qkv, c_col, c_row, tq, name, side=None):
    T = qkv.shape[0]
    H = qkv.shape[1] // (3 * HEAD)
    nq = T // tq
    c_blocks = c_row.reshape(H, nq, 1, tq)

    def body(q_ref, k_ref, v_ref, cq_ref, ck_ref, o_ref, lse_ref):
        qi = pl.program_id(1)
        q, cq = q_ref[...], cq_ref[...]
        causal = lax.broadcasted_iota(jnp.int32, (tq, tq), 1) <= lax.broadcasted_iota(jnp.int32, (tq, tq), 0)

        def key_block(ki, carry, diagonal):
            m, l, acc = carry
            rows = pl.ds(pl.multiple_of(ki * tq, tq), tq)
            s = _nt(q, k_ref[rows, :]) * SCALE + cq - ck_ref[ki]
            if diagonal:
                s = jnp.where(causal, s, NEG)
            m_new = jnp.maximum(m, jnp.max(s, axis=-1, keepdims=True))
            alpha = jnp.exp(m - m_new)
            p = jnp.exp(s - m_new)
            l = alpha * l + jnp.sum(p, axis=-1, keepdims=True)
            acc = alpha * acc + _nn(p.astype(BF16), v_ref[rows, :])
            return m_new, l, acc

        init = (jnp.full((tq, 1), NEG, F32), jnp.zeros((tq, 1), F32), jnp.zeros((tq, HEAD), F32))
        carry = lax.fori_loop(0, qi, lambda ki, c: key_block(ki, c, False), init)
        m, l, acc = key_block(qi, carry, True)
        o_ref[...] = acc / l
        lse_ref[...] = m + jnp.log(l)

    return _call(
        body, name=name, grid=(H, nq), side=side,
        in_specs=[pl.BlockSpec((tq, HEAD), lambda h, qi: (qi, h)),
                  pl.BlockSpec((T, HEAD), lambda h, qi: (0, H + h)),
                  pl.BlockSpec((T, HEAD), lambda h, qi: (0, 2 * H + h)),
                  pl.BlockSpec((None, tq, 1), lambda h, qi: (h, qi, 0)),
                  pl.BlockSpec((None, nq, 1, tq), lambda h, qi: (h, 0, 0, 0))],
        out_specs=[pl.BlockSpec((tq, HEAD), lambda h, qi: (qi, h)),
                   pl.BlockSpec((None, tq, 1), lambda h, qi: (h, qi, 0))],
        out_shape=[_sds((T, H * HEAD), F32), _sds((H, T, 1), F32)],
        args=(qkv, qkv, qkv, c_col, c_blocks))


def _fox_bwd(qkv, c_col, c_row, out, dout, lse, tq, name, side=None):
    T = qkv.shape[0]
    H = qkv.shape[1] // (3 * HEAD)
    nq = T // tq

    def body(q_ref, k_ref, v_ref, cq_ref, ck_ref, o_ref, do_ref, lse_ref, dq_ref, dk_ref, dv_ref, dck_ref, dcq_ref,
             delta_s):
        ki = pl.program_id(1)

        @pl.when(ki == 0)
        def _():
            dq_ref[...] = jnp.zeros_like(dq_ref)
            dcq_ref[...] = jnp.zeros_like(dcq_ref)
            delta_s[...] = jnp.sum(do_ref[...] * o_ref[...], axis=-1, keepdims=True)

        k, v, ck = k_ref[...], v_ref[...], ck_ref[...]
        causal = lax.broadcasted_iota(jnp.int32, (tq, tq), 1) <= lax.broadcasted_iota(jnp.int32, (tq, tq), 0)

        def query_block(qi, carry, diagonal):
            dk, dv, dck = carry
            rows = pl.ds(pl.multiple_of(qi * tq, tq), tq)
            q = q_ref[rows, :]
            s = _nt(q, k) * SCALE + cq_ref[rows, :] - ck
            if diagonal:
                s = jnp.where(causal, s, NEG)
            p = jnp.exp(s - lse_ref[rows, :])
            dob = do_ref[rows, :].astype(BF16)
            ds = p * (_nt(dob, v) - delta_s[rows, :])
            dsb = ds.astype(BF16)
            dq_ref[rows, :] += _nn(dsb, k) * SCALE
            dcq_ref[rows, :] += jnp.sum(ds, axis=-1, keepdims=True)
            return dk + _tn(dsb, q), dv + _tn(p.astype(BF16), dob), dck - jnp.sum(ds, axis=0, keepdims=True)

        init = (jnp.zeros((tq, HEAD), F32), jnp.zeros((tq, HEAD), F32), jnp.zeros((1, tq), F32))
        carry = query_block(ki, init, True)
        dk, dv, dck = lax.fori_loop(ki + 1, nq, lambda qi, c: query_block(qi, c, False), carry)
        dk_ref[...] = dk * SCALE
        dv_ref[...] = dv
        dck_ref[...] = dck

    head = lambda off: pl.BlockSpec((T, HEAD), lambda h, ki: (0, off + h))
    col = pl.BlockSpec((None, T, 1), lambda h, ki: (h, 0, 0))
    return _call(
        body, name=name, grid=(H, nq), side=side,
        in_specs=[head(0),
                  pl.BlockSpec((tq, HEAD), lambda h, ki: (ki, H + h)),
                  pl.BlockSpec((tq, HEAD), lambda h, ki: (ki, 2 * H + h)),
                  col, pl.BlockSpec((None, 1, tq), lambda h, ki: (h, 0, ki)), head(0), head(0), col],
        out_specs=[head(0),
                   pl.BlockSpec((tq, HEAD), lambda h, ki: (ki, h)),
                   pl.BlockSpec((tq, HEAD), lambda h, ki: (ki, h)),
                   pl.BlockSpec((None, 1, tq), lambda h, ki: (h, 0, ki)), col],
        out_shape=[_sds((T, H * HEAD), F32), _sds((T, H * HEAD), F32), _sds((T, H * HEAD), F32), _sds((H, 1, T), F32),
                   _sds((H, T, 1), F32)],
        scratch_shapes=[pltpu.VMEM((T, 1), F32)],
        args=(qkv, qkv, qkv, c_col, c_row, out, dout, lse))


def _t5_bucket(dist):
    max_exact = NUM_BUCKETS // 2
    d = dist.astype(np.float32)
    large = max_exact + (np.log(np.maximum(d, np.float32(1.0)) / np.float32(max_exact))
                         / np.float32(math.log(MAX_DISTANCE / max_exact))
                         * np.float32(NUM_BUCKETS - max_exact)).astype(np.int32)
    large = np.minimum(large, NUM_BUCKETS - 1)
    return np.where(dist < max_exact, dist, large)


def _bucket_maps():
    maps = []
    for d in DILATIONS:
        e = NSLAB // d
        rows = BLK // e
        idx = np.arange(BLK)
        pos = e * (idx % rows) + idx // rows
        qpos = pos[:, None] + BLK
        kpos = np.concatenate([pos, pos + BLK])[None, :]
        delta = qpos - kpos
        band = (delta >= 0) & (delta <= BLK)
        bucket = _t5_bucket(np.clip(delta, 0, None) * d)
        maps.append(np.where(band, bucket, -1).astype(np.int32))
    return np.stack(maps)


def _dil_geometry(T):
    n16 = T // NSLAB
    geo = []
    for d in DILATIONS:
        e = NSLAB // d
        rows = BLK // e
        nblk = n16 // rows
        geo.append((d, e, rows, nblk))
    return geo


DIL_INTERLEAVE_FWD = {1: 4, 4: 8, 16: 8}
DIL_INTERLEAVE_BWD = {1: 8, 4: 8, 16: 8}


def _dil_interleave(per_step, nblocks):
    while per_step > 1 and (nblocks % per_step or nblocks // per_step < 2):
        per_step -= 1
    return per_step


def _dil_bias(tab_ref, bkt_ref, bias_s, h):
    for p in range(len(DILATIONS)):
        bk = bkt_ref[p]
        bias = jnp.full((BLK, 2 * BLK), NEG, F32)
        for b in range(NUM_BUCKETS):
            bias = jnp.where(bk == b, tab_ref[b, h], bias)
        bias_s[p] = bias


def _dil_rows(d, e, rows, sub, blk):
    start = pl.multiple_of(blk * rows, rows)
    return [(sub + d * j, pl.ds(start, rows)) for j in range(e)]


def _gather(ref, idx):
    return jnp.concatenate([ref[s, r, :] for s, r in idx], axis=0)


def _scatter(ref, idx, val, rows):
    for j, (s, r) in enumerate(idx):
        ref[s, r, :] = val[j * rows:(j + 1) * rows]


def _scatter_add(ref, idx, val, rows):
    for j, (s, r) in enumerate(idx):
        ref[s, r, :] += val[j * rows:(j + 1) * rows]


def _dil_fwd(qkv, table, name, side=None):
    n16 = qkv.shape[1]
    T = NSLAB * n16
    H = qkv.shape[2] // (3 * HEAD)
    geo = _dil_geometry(T)
    bkt = jnp.asarray(_bucket_maps())

    def body(tab_ref, bkt_ref, q_ref, k_ref, v_ref, o_ref, lse_ref, bias_s, m_s, l_s):
        h = pl.program_id(0)
        _dil_bias(tab_ref, bkt_ref, bias_s, h)
        first_mask = lax.broadcasted_iota(jnp.int32, (BLK, 2 * BLK), 1) < BLK

        starts = len(DILATIONS) - 1

        def load(p, d, e, rows, sub, blk):
            cur = _dil_rows(d, e, rows, sub, blk)
            prev = _dil_rows(d, e, rows, sub, jnp.maximum(blk - 1, 0))
            q = _gather(q_ref, cur).astype(BF16)
            kk = jnp.concatenate([_gather(k_ref, prev), _gather(k_ref, cur)], axis=0).astype(BF16)
            vv = jnp.concatenate([_gather(v_ref, prev), _gather(v_ref, cur)], axis=0).astype(BF16)
            old = None if p == starts else (_gather(m_s, cur), _gather(l_s, cur), _gather(o_ref, cur))
            return cur, blk, q, kk, vv, old

        def compute(p, blk, q, kk, vv, old):
            s = _nt(q, kk) * SCALE + bias_s[p]
            s = jnp.where(first_mask & (blk == 0), NEG, s)
            m_blk = jnp.max(s, axis=-1, keepdims=True)
            if old is None:
                m_new = m_blk
                pr = jnp.exp(s - m_new)
                l_new = jnp.sum(pr, axis=-1, keepdims=True)
                acc = _nn(pr.astype(BF16), vv)
            else:
                m_old, l_old, acc_old = old
                m_new = jnp.maximum(m_old, m_blk)
                alpha = jnp.exp(m_old - m_new)
                pr = jnp.exp(s - m_new)
                l_new = alpha * l_old + jnp.sum(pr, axis=-1, keepdims=True)
                acc = alpha * acc_old + _nn(pr.astype(BF16), vv)
            if p == 0:
                return acc / l_new, m_new + jnp.log(l_new), None
            return acc, m_new, l_new

        def store(p, rows, cur, acc, m_new, l_new):
            _scatter(o_ref, cur, acc, rows)
            if p == 0:
                _scatter(lse_ref, cur, m_new, rows)
            else:
                _scatter(m_s, cur, m_new, rows)
                _scatter(l_s, cur, l_new, rows)

        for p in reversed(range(len(DILATIONS))):
            d, e, rows, nblk = geo[p]
            per_step = _dil_interleave(DIL_INTERLEAVE_FWD[d], d * nblk)

            def step(i, carry, p=p, d=d, e=e, rows=rows, nblk=nblk, per_step=per_step):
                ids = [i + u * (d * nblk // per_step) for u in range(per_step)]
                loaded = [load(p, d, e, rows, j // nblk, j % nblk) for j in ids]
                done = [(cur, compute(p, blk, q, kk, vv, old)) for cur, blk, q, kk, vv, old in loaded]
                for cur, res in done:
                    store(p, rows, cur, *res)
                return carry

            lax.fori_loop(0, d * nblk // per_step, step, 0)

    head = lambda off: pl.BlockSpec((NSLAB, n16, HEAD), lambda h: (0, 0, off + h))
    return _call(
        body, name=name, grid=(H,), side=side,
        in_specs=[pl.BlockSpec(memory_space=pltpu.SMEM), pl.BlockSpec((3, BLK, 2 * BLK), lambda h: (0, 0, 0)),
                  head(0), head(H), head(2 * H)],
        out_specs=[head(0), pl.BlockSpec((None, NSLAB, n16, 1), lambda h: (h, 0, 0, 0))],
        out_shape=[_sds((NSLAB, n16, H * HEAD), F32), _sds((H, NSLAB, n16, 1), F32)],
        scratch_shapes=[pltpu.VMEM((3, BLK, 2 * BLK), F32), pltpu.VMEM((NSLAB, n16, 1), F32),
                        pltpu.VMEM((NSLAB, n16, 1), F32)],
        args=(table, bkt, qkv, qkv, qkv))


def _dil_bwd(qkv, table, out, dout, lse, name, side=None):
    n16 = qkv.shape[1]
    T = NSLAB * n16
    H = qkv.shape[2] // (3 * HEAD)
    geo = _dil_geometry(T)
    bkt = jnp.asarray(_bucket_maps())

    def body(tab_ref, bkt_ref, q_ref, k_ref, v_ref, o_ref, do_ref, lse_ref,
             dq_ref, dk_ref, dv_ref, dtab_ref, bias_s, dbias_s, delta_s):
        h = pl.program_id(0)
        _dil_bias(tab_ref, bkt_ref, bias_s, h)
        first_mask = lax.broadcasted_iota(jnp.int32, (BLK, 2 * BLK), 1) < BLK
        dbias_s[...] = jnp.zeros_like(dbias_s)
        dq_ref[...] = jnp.zeros_like(dq_ref)
        dk_ref[...] = jnp.zeros_like(dk_ref)
        dv_ref[...] = jnp.zeros_like(dv_ref)
        for r in range(NSLAB):
            delta_s[r] = jnp.sum(do_ref[r] * o_ref[r], axis=-1, keepdims=True)

        def load(d, e, rows, sub, blk):
            cur = _dil_rows(d, e, rows, sub, blk)
            prev = _dil_rows(d, e, rows, sub, jnp.maximum(blk - 1, 0))
            q = _gather(q_ref, cur).astype(BF16)
            kk = jnp.concatenate([_gather(k_ref, prev), _gather(k_ref, cur)], axis=0).astype(BF16)
            vv = jnp.concatenate([_gather(v_ref, prev), _gather(v_ref, cur)], axis=0).astype(BF16)
            dob = _gather(do_ref, cur).astype(BF16)
            return cur, prev, blk, q, kk, vv, dob, _gather(lse_ref, cur), _gather(delta_s, cur)

        def compute(p, blk, q, kk, vv, dob, lse, delta):
            s = _nt(q, kk) * SCALE + bias_s[p]
            s = jnp.where(first_mask & (blk == 0), NEG, s)
            pr = jnp.exp(s - lse)
            ds = pr * (_nt(dob, vv) - delta)
            dsb = ds.astype(BF16)
            return ds, _nn(dsb, kk) * SCALE, _tn(dsb, q) * SCALE, _tn(pr.astype(BF16), dob)

        def store(rows, cur, prev, dq, dkk, dvv):
            _scatter_add(dq_ref, cur, dq, rows)
            _scatter_add(dk_ref, prev, dkk[:BLK], rows)
            _scatter_add(dk_ref, cur, dkk[BLK:], rows)
            _scatter_add(dv_ref, prev, dvv[:BLK], rows)
            _scatter_add(dv_ref, cur, dvv[BLK:], rows)

        for p in range(len(DILATIONS)):
            d, e, rows, nblk = geo[p]
            per_step = _dil_interleave(DIL_INTERLEAVE_BWD[d], d * nblk)

            def step(i, carry, p=p, d=d, e=e, rows=rows, nblk=nblk, per_step=per_step):
                ids = [i + u * (d * nblk // per_step) for u in range(per_step)]
                loaded = [load(d, e, rows, j // nblk, j % nblk) for j in ids]
                done = [(cur, prev, compute(p, *rest)) for cur, prev, *rest in loaded]
                dbias_s[p] += functools.reduce(jnp.add, [res[0] for _, _, res in done])
                for cur, prev, res in done:
                    store(rows, cur, prev, *res[1:])
                return carry

            lax.fori_loop(0, d * nblk // per_step, step, 0)

        lane = lax.broadcasted_iota(jnp.int32, (1, HEAD), 1)
        row = jnp.zeros((1, HEAD), F32)
        for b in range(NUM_BUCKETS):
            tot = jnp.zeros((1, 1), F32)
            for p in range(len(DILATIONS)):
                hit = jnp.where(bkt_ref[p] == b, dbias_s[p], 0.0)
                tot = tot + jnp.sum(jnp.sum(hit, axis=0, keepdims=True), axis=1, keepdims=True)
            row = jnp.where(lane == b, tot, row)
        dtab_ref[...] = row

    head = lambda off: pl.BlockSpec((NSLAB, n16, HEAD), lambda h: (0, 0, off + h))
    return _call(
        body, name=name, grid=(H,), side=side,
        in_specs=[pl.BlockSpec(memory_space=pltpu.SMEM), pl.BlockSpec((3, BLK, 2 * BLK), lambda h: (0, 0, 0)),
                  head(0), head(H), head(2 * H), head(0), head(0),
                  pl.BlockSpec((None, NSLAB, n16, 1), lambda h: (h, 0, 0, 0))],
        out_specs=[head(0), head(0), head(0), pl.BlockSpec((None, 1, HEAD), lambda h: (h, 0, 0))],
        out_shape=[_sds((NSLAB, n16, H * HEAD), F32)] * 3 + [_sds((H, 1, HEAD), F32)],
        scratch_shapes=[pltpu.VMEM((3, BLK, 2 * BLK), F32), pltpu.VMEM((3, BLK, 2 * BLK), F32),
                        pltpu.VMEM((NSLAB, n16, 1), F32)],
        args=(table, bkt, qkv, qkv, qkv, out, dout, lse))


def _qknorm_bwd(raw, dq, dk, dv, gains, tm, name):
    T, N = raw.shape
    C = N // 3

    def body(raw_ref, dq_ref, dk_ref, dv_ref, gains_ref, dp_ref, dg_ref):
        @pl.when(pl.program_id(0) == 0)
        def _():
            dg_ref[...] = jnp.zeros_like(dg_ref)

        for t, d_ref in enumerate((dq_ref, dk_ref)):
            gain = gains_ref[t]
            dgain = jnp.zeros((1, HEAD), F32)
            for k in range(C // HEAD):
                y = raw_ref[:, t * C + k * HEAD:t * C + (k + 1) * HEAD]
                dn = d_ref[:, k * HEAD:(k + 1) * HEAD]
                r = lax.rsqrt(jnp.mean(y * y, axis=-1, keepdims=True) + RMS_EPS)
                yhat = y * r
                gd = dn * gain
                dy = r * (gd - yhat * jnp.mean(gd * yhat, axis=-1, keepdims=True))
                dp_ref[:, t * C + k * HEAD:t * C + (k + 1) * HEAD] = dy.astype(BF16)
                dgain = dgain + jnp.sum(dn * yhat, axis=0, keepdims=True)
            dg_ref[t] += dgain
        dp_ref[:, 2 * C:] = dv_ref[...].astype(BF16)

    third = pl.BlockSpec((tm, C), lambda i: (i, 0))
    return pl.pallas_call(
        body, name=name, grid=(T // tm,),
        in_specs=[pl.BlockSpec((tm, N), lambda i: (i, 0)), third, third, third,
                  pl.BlockSpec((2, 1, HEAD), lambda i: (0, 0, 0))],
        out_specs=[pl.BlockSpec((tm, N), lambda i: (i, 0)), pl.BlockSpec((2, 1, HEAD), lambda i: (0, 0, 0))],
        out_shape=[_sds((T, N), BF16), _sds((2, 1, HEAD), F32)], compiler_params=_cparams(),
    )(raw, dq, dk, dv, gains)


def _pad_lanes(v, width=HEAD):
    return jnp.pad(v, ((0, 0), (0, width - v.shape[1])))


def _local_step(x, target, small, wts, plan=None):
    grads = {}

    def hosted(host, fn, *args, **kw):
        side = plan.before(host, wts, grads) if plan is not None else None
        if side is None:
            return fn(*args, name=host, **kw)
        res, side_res = fn(*args, name=host, side=side, **kw)
        plan.after(host, side_res, wts, grads)
        return res

    T, D = x.shape
    C = D // 2
    H = C // HEAD
    n16 = T // NSLAB
    tm = min(512, T)
    tmm = min(1024, T)
    tms = 4 * n16
    tq = min(512, T)
    bn = min(1024, D)
    g1, gm, g2 = small["ffn1_norm"], small["mix_norm"], small["ffn2_norm"]
    gains_a = jnp.stack([small["q_norm_a"], small["k_norm_a"], jnp.ones_like(small["q_norm_a"])])
    gains_b = jnp.stack([small["q_norm_b"], small["k_norm_b"], jnp.ones_like(small["q_norm_b"])])
    fbias = _pad_lanes(small["forget_bias"])
    table = small["rel_bias_table"]

    h1, gu1, act1 = hosted("ffn1_up", _ffn_fwd, x, g1, wts["ffn1_in"], None, tmm)
    x1 = hosted("ffn1_down", _ffn_down, x, act1, wts["ffn1_out"], tmm)
    w_in_t, w_f_t, w_o = wts["w_in_t"], wts["w_f_t"], wts["w_o"]
    raw_a, nrm_a, h2a = _proj(x1, gm, w_in_t, gains_a, (True, True, False), tn=C, w_off=0, slabs=True, tm=tms,
                              normed_dtype=F32, name="proj_a")
    raw_b, nrm_b, h2b, f_raw = _proj(x1, gm, w_in_t, gains_b, (True, True, False), tn=C, w_off=3, slabs=False, tm=tmm,
                                     normed_dtype=BF16, name="proj_b", small_wt=w_f_t)
    c = _fox_gate_fwd(f_raw, fbias, "fox_gate_fwd")
    c_heads = c[:, :H].T
    c_col, c_row = c_heads[:, :, None], c_heads[:, None, :]
    out_a, lse_a = hosted("dil_fwd", _dil_fwd, nrm_a, table)
    out_b, lse_b = hosted("fox_fwd", _fox_fwd, nrm_b, c_col, c_row, tq)
    x2a = _mm(out_a, w_o, nt=False, tk=C, tm=tms, a_layout="slab", out_layout="view", resid=x1, name="out_a")
    x2 = _mm(out_b, w_o, nt=False, tk=C, tm=tmm, a_layout="nat", out_layout="nat", resid=x2a, w_off=1, name="out_b")
    dy, h3, gu3, act3, loss_row = _ffn_fwd(x2, g2, wts["ffn2_in"], wts["ffn2_out"], tm, "ffn2_fwd", target=target)

    def ffn_backward(tag, xin, g, h, gu, act, win, wout, dres):
        nc, tf = wout.shape[0], wout.shape[1]
        dh, dgu, dyb = hosted(tag + "_bwd", _ffn_bwd, dres, gu, win, wout, tm)
        grads[tag + "_w_in_t"], grads[tag + "_w_in_t_bf16"] = hosted(
            tag + "_dwin", _mm_tn, dgu.reshape(2 * nc, tf, T), h, bm=tf, bn=bn, bt=T, a_rows=True, twin=True)
        dxin, grads[tag + "_norm"] = hosted(tag + "_rms_bwd", _rms_bwd, xin, g, dh, dres, tm)
        grads[tag + "_w_out"], grads[tag + "_w_out_bf16"] = hosted(
            tag + "_dwout", _mm_tn, act, dyb, bm=tf, bn=bn, bt=T, a_rows=True, twin=True)
        return dxin

    dx2 = ffn_backward("ffn2", x2, g2, h3, gu3, act3, wts["ffn2_in"], wts["ffn2_out"], dy)

    dmix_a = _mm(dx2, w_o, nt=True, tk=D, tm=tms, a_layout="view", out_layout="slab", n_out=C, name="dmix_a")
    dmix_b = _mm(dx2, w_o, nt=True, tk=D, tm=tmm, a_layout="nat", out_layout="nat", n_out=C, w_off=1, name="dmix_b")
    dwo = _mm_tn(out_a.reshape(1, T, C), dx2.reshape(n16, NSLAB * D), bm=C, bn=bn, bt=n16, b_slabs=True,
                 rows=2 * C, name="dwo_a")
    dwo = _mm_tn(out_b.reshape(1, T, C), dx2, bm=C, bn=bn, bt=tm, rows=2 * C, m_off=1, into=dwo, name="dwo_b")
    grads["w_out"] = dwo[0]

    dqa, dka, dva, dtab = hosted("dil_bwd", _dil_bwd, nrm_a, table, out_a, dmix_a, lse_a)
    dqb, dkb, dvb, dck, dcq = hosted("fox_bwd", _fox_bwd, nrm_b, c_col, c_row, out_b, dmix_b, lse_b, tq)
    grads["rel_bias_table"] = dtab[:, 0, :NUM_BUCKETS].T
    dc = _pad_lanes((dck[:, 0, :] + dcq[:, :, 0]).T)
    df, dfb = _fox_gate_bwd(f_raw, fbias, dc, "fox_gate_bwd")
    grads["forget_bias"] = dfb[:, :H]

    flat = lambda a: a.reshape(T, a.shape[-1])
    dproj_a, dgain_a = _qknorm_bwd(flat(raw_a), flat(dqa), flat(dka), flat(dva), gains_a[:2], min(256, T), "qknorm_bwd_a")
    dproj_b, dgain_b = _qknorm_bwd(raw_b, dqb, dkb, dvb, gains_b[:2], min(256, T), "qknorm_bwd_b")
    grads["q_norm_a"], grads["k_norm_a"] = dgain_a[0], dgain_a[1]
    grads["q_norm_b"], grads["k_norm_b"] = dgain_b[0], dgain_b[1]
    dproj_a = dproj_a.reshape(NSLAB, n16, 3 * C)

    dh2 = _mm(dproj_a, w_in_t, nt=False, tk=C, tm=tms, a_layout="slab", out_layout="view", name="dh2_a")
    dh2 = _mm(dproj_b, w_in_t, nt=False, tk=C, tm=tmm, a_layout="nat", out_layout="nat", resid=dh2, w_off=3,
              small=(df, w_f_t), name="dh2_b")
    dx1, grads["mix_norm"] = _rms_bwd(x1, gm, dh2, dx2, tm, "mix_rms_bwd")
    bt = min(2048, T)
    dwt = _mm_tn(flat(dproj_a)[None], flat(h2a), bm=C, bn=bn, bt=bt, rows=6 * C + H, name="dw_a")
    dwt = _mm_tn(dproj_b[None], h2b, bm=C, bn=bn, bt=bt, rows=6 * C + H, m_off=3, into=dwt, name="dw_b")
    dwt = _mm_tn(df[None, :, :H], h2b, bm=H, bn=bn, bt=bt, rows=6 * C + H, m_off=6 * C // H, into=dwt, name="dw_f")
    grads["w_in_t"] = dwt[0]

    grad_x = ffn_backward("ffn1", x, g1, h1, gu1, act1, wts["ffn1_in"], wts["ffn1_out"], dx1)
    return loss_row, grad_x, grads


def _place():
    x, y, c = lax.axis_index("x"), lax.axis_index("y"), lax.axis_index("c")
    other_chips = [(1 - x, y), (x, 1 - y), (1 - x, 1 - y)]
    return x, y, c, other_chips


def _run_side(side, name):
    def body(*refs):
        si, so = len(side.ins), len(side.outs)
        side.start(refs[:si], refs[si:si + so], refs[si + so:])
        side.finish(refs[:si], refs[si:si + so], refs[si + so:])

    return pl.pallas_call(body, name=name, in_specs=[ANY] * len(side.ins), out_specs=[ANY] * len(side.outs),
                          out_shape=side.outs, scratch_shapes=side.sems)(*side.ins)


def _all_gather(shards):
    n = len(shards)

    def plan(ins, outs, sems):
        send_sems, recv_sems, local_sems = sems
        x, y, c, chips = _place()
        me, sibling = (x, y, c), (x, y, 1 - c)

        def copy(a, k, block, to, src=None):
            px, py, pc = block
            dst = outs[a].at[4 * px + 2 * py + pc]
            return pltpu.make_async_remote_copy(
                src_ref=dst if src is None else src, dst_ref=dst, send_sem=send_sems.at[7 * a + k],
                recv_sem=recv_sems.at[7 * a + k], device_id=to, device_id_type=MESH)

        mine = [pltpu.make_async_copy(ins[a], outs[a].at[4 * x + 2 * y + c], local_sems.at[a]) for a in range(n)]
        first = []
        for a in range(n):
            first.append(copy(a, 0, me, sibling, src=ins[a]))
            first += [copy(a, 1 + j, me, (*chip, c), src=ins[a]) for j, chip in enumerate(chips)]
        return copy, mine, first, me, sibling, c, chips

    def start(ins, outs, sems):
        _, mine, first, *_ = plan(ins, outs, sems)
        for cp in mine + first:
            cp.start()

    def finish(ins, outs, sems):
        copy, mine, first, me, sibling, c, chips = plan(ins, outs, sems)
        passed = []
        for a in range(n):
            for j, chip in enumerate(chips):
                copy(a, 1 + j, (*chip, c), me).wait_recv()
                fwd = copy(a, 4 + j, (*chip, c), sibling)
                fwd.start()
                passed.append(fwd)
        for a in range(n):
            copy(a, 0, sibling, me).wait_recv()
            for j, chip in enumerate(chips):
                copy(a, 4 + j, (*chip, 1 - c), me).wait_recv()
        for cp in first + passed:
            cp.wait_send()
        for cp in mine:
            cp.wait()

    return _Side(shards, [_sds((N_DEV,) + s.shape, s.dtype) for s in shards],
                 [pltpu.SemaphoreType.DMA((7 * n,)), pltpu.SemaphoreType.DMA((7 * n,)), pltpu.SemaphoreType.DMA((n,))],
                 start, finish)


def _all_gather_relayed(shards):
    n = len(shards)
    halves = [-(-(s.shape[0] // 2) // 16) * 16 for s in shards]

    def body_parts(ins, outs, sems):
        send_sems, recv_sems, local_sems = sems
        x, y, c, _ = _place()
        me, sib, xn, yn, dg = (x, y, c), (x, y, 1 - c), (1 - x, y, c), (x, 1 - y, c), (1 - x, 1 - y, c)

        def rows(a, block, part):
            px, py, pc = block
            whole = outs[a].at[4 * px + 2 * py + pc]
            if part is None:
                return whole
            return whole.at[pl.ds(0, halves[a])] if part == 0 else whole.at[pl.ds(halves[a], shards[a].shape[0] - halves[a])]

        def copy(a, k, block, part, to, src=None):
            dst = rows(a, block, part)
            return pltpu.make_async_remote_copy(
                src_ref=dst if src is None else src, dst_ref=dst, send_sem=send_sems.at[9 * a + k],
                recv_sem=recv_sems.at[9 * a + k], device_id=to, device_id_type=MESH)

        flip = lambda dev: (dev[0], dev[1], 1 - dev[2])
        mine = [pltpu.make_async_copy(ins[a], rows(a, me, None), local_sems.at[a]) for a in range(n)]
        own = [[copy(a, 0, me, None, sib, src=ins[a]), copy(a, 1, me, None, xn, src=ins[a]),
                copy(a, 2, me, None, yn, src=ins[a])] for a in range(n)]
        relays = lambda a: [(1, [copy(a, 3, xn, 0, yn), copy(a, 5, xn, None, sib)]),
                            (2, [copy(a, 4, yn, 1, xn), copy(a, 6, yn, None, sib)]),
                            (3, [copy(a, 7, dg, 0, sib)]), (4, [copy(a, 8, dg, 1, sib)])]
        lands = {0: (sib, None), 1: (xn, None), 2: (yn, None), 3: (dg, 0), 4: (dg, 1), 5: (flip(xn), None),
                 6: (flip(yn), None), 7: (flip(dg), 0), 8: (flip(dg), 1)}
        arrival = lambda a, k: copy(a, k, lands[k][0], lands[k][1], me)
        return mine, own, relays, arrival

    def start(ins, outs, sems):
        mine, own, _, _ = body_parts(ins, outs, sems)
        for cp in mine + [cp for per in own for cp in per]:
            cp.start()

    def finish(ins, outs, sems):
        mine, own, relays, arrival = body_parts(ins, outs, sems)
        sent = [cp for per in own for cp in per]
        relays = [relays(a) for a in range(n)]
        for stage in range(4):
            for a in range(n):
                after, passes = relays[a][stage]
                arrival(a, after).wait_recv()
                for cp in passes:
                    cp.start()
                sent += passes
        for a in range(n):
            for k in (0, 5, 6, 7, 8):
                arrival(a, k).wait_recv()
        for cp in sent:
            cp.wait_send()
        for cp in mine:
            cp.wait()

    return _Side(shards, [_sds((N_DEV,) + s.shape, s.dtype) for s in shards],
                 [pltpu.SemaphoreType.DMA((9 * n,)), pltpu.SemaphoreType.DMA((9 * n,)), pltpu.SemaphoreType.DMA((n,))],
                 start, finish)


def _exchange_in_chip(gs):
    n = len(gs)

    def copies(ins, outs, sems):
        x, y, c, _ = _place()
        return [pltpu.make_async_remote_copy(
            src_ref=ins[a].at[2 * q + 1 - c], dst_ref=outs[a].at[q], send_sem=sems[0].at[4 * a + q],
            recv_sem=sems[1].at[4 * a + q], device_id=(x, y, 1 - c), device_id_type=MESH)
            for a in range(n) for q in range(4)]

    def start(ins, outs, sems):
        for cp in copies(ins, outs, sems):
            cp.start()

    def finish(ins, outs, sems):
        for cp in copies(ins, outs, sems):
            cp.wait()

    return _Side(gs, [_sds((4,) + g.shape[1:], g.dtype) for g in gs],
                 [pltpu.SemaphoreType.DMA((4 * n,)), pltpu.SemaphoreType.DMA((4 * n,))], start, finish)


def _exchange_between_chips(ps):
    n = len(ps)

    def copies(ins, outs, sems):
        x, y, c, chips = _place()
        return [pltpu.make_async_remote_copy(
            src_ref=ins[a].at[2 * cx + cy], dst_ref=outs[a].at[j], send_sem=sems[0].at[3 * a + j],
            recv_sem=sems[1].at[3 * a + j], device_id=(cx, cy, c), device_id_type=MESH)
            for a in range(n) for j, (cx, cy) in enumerate(chips)]

    def start(ins, outs, sems):
        for cp in copies(ins, outs, sems):
            cp.start()

    def finish(ins, outs, sems):
        for cp in copies(ins, outs, sems):
            cp.wait()

    return _Side(ps, [_sds((3,) + p.shape[1:], p.dtype) for p in ps],
                 [pltpu.SemaphoreType.DMA((3 * n,)), pltpu.SemaphoreType.DMA((3 * n,))], start, finish)


def _all_reduce_small(v, name):
    R = v.shape[0]

    def body(v_ref, sum_ref, all_ref, send_sems, recv_sems):
        x, y, c, _ = _place()
        k = 4 * x + 2 * y + c
        all_ref[k] = v_ref[...]
        copies = []
        for rel in range(1, N_DEV):
            fx, fy, fc = (rel >> 2) & 1, (rel >> 1) & 1, rel & 1
            peer = (1 - x if fx else x, 1 - y if fy else y, 1 - c if fc else c)
            copies.append(pltpu.make_async_remote_copy(
                src_ref=v_ref, dst_ref=all_ref.at[k], send_sem=send_sems.at[rel - 1], recv_sem=recv_sems.at[rel - 1],
                device_id=peer, device_id_type=MESH))
        for cp in copies:
            cp.start()
        for rel in range(1, N_DEV):
            fx, fy, fc = (rel >> 2) & 1, (rel >> 1) & 1, rel & 1
            src = 4 * (1 - x if fx else x) + 2 * (1 - y if fy else y) + (1 - c if fc else c)
            pltpu.make_async_remote_copy(
                src_ref=v_ref, dst_ref=all_ref.at[src], send_sem=send_sems.at[rel - 1], recv_sem=recv_sems.at[rel - 1],
                device_id=(x, y, c), device_id_type=MESH).wait_recv()
        for cp in copies:
            cp.wait_send()
        tot = all_ref[0]
        for d in range(1, N_DEV):
            tot = tot + all_ref[d]
        sum_ref[...] = tot

    vm = pl.BlockSpec(memory_space=pltpu.VMEM)
    return pl.pallas_call(
        body, name=name, in_specs=[vm], out_specs=[vm, vm],
        out_shape=[_sds((R, HEAD), F32), _sds((N_DEV, R, HEAD), F32)],
        scratch_shapes=[pltpu.SemaphoreType.DMA((N_DEV - 1,)), pltpu.SemaphoreType.DMA((N_DEV - 1,))],
    )(v)[0]


def _tiles(rows, cols):
    tr = next((cand for cand in (688, 512, 256) if rows % cand == 0), rows)
    tc = 512 if (cols % 512 == 0 and tr * cols * 4 > (2 << 20)) else cols
    return tr, tc


def _chip_sum(g, r1, core, name):
    _, R, Cc = g.shape
    tr, tc = _tiles(R, Cc)

    def body(core_ref, g_ref, r_ref, p_ref):
        p_ref[...] = (g_ref[...] + r_ref[...]).astype(BF16)

    blk = lambda f: pl.BlockSpec((None, tr, tc), f)
    return pl.pallas_call(
        body, name=name,
        grid_spec=pltpu.PrefetchScalarGridSpec(
            num_scalar_prefetch=1, grid=(4, R // tr, Cc // tc),
            in_specs=[blk(lambda q, i, j, core: (2 * q + core[0], i, j)), blk(lambda q, i, j, core: (q, i, j))],
            out_specs=blk(lambda q, i, j, core: (q, i, j))),
        out_shape=_sds((4, R, Cc), BF16), compiler_params=_cparams(),
    )(core, g, r1)


def _adamw_update(gv, w_ref, m_ref, v_ref, d_ref, nm_ref, nv_ref):
    nm = B1 * m_ref[...] + (1.0 - B1) * gv
    nv = B2 * v_ref[...] + (1.0 - B2) * jnp.square(gv)
    m_hat = nm / (1.0 - B1 ** STEP)
    v_hat = nv / (1.0 - B2 ** STEP)
    d_ref[...] = -LR * (m_hat / (jnp.sqrt(v_hat) + EPS) + WD * w_ref[...])
    nm_ref[...] = nm
    nv_ref[...] = nv


def _reduce_adamw(g, r1, r2, where, w, m, v, name):
    _, R, Cc = g.shape
    tr, tc = _tiles(R, Cc)

    def body(where_ref, g_ref, r1_ref, r2_ref, w_ref, m_ref, v_ref, o_ref, d_ref, nm_ref, nv_ref):
        gv = ((g_ref[...] + r1_ref[...]) + r2_ref[0].astype(F32)) + (r2_ref[1].astype(F32) + r2_ref[2].astype(F32))
        o_ref[...] = gv
        _adamw_update(gv, w_ref, m_ref, v_ref, d_ref, nm_ref, nv_ref)

    blk = pl.BlockSpec((tr, tc), lambda i, j, w: (i, j))
    return pl.pallas_call(
        body, name=name,
        grid_spec=pltpu.PrefetchScalarGridSpec(
            num_scalar_prefetch=1, grid=(R // tr, Cc // tc),
            in_specs=[pl.BlockSpec((None, tr, tc), lambda i, j, w: (w[0], i, j)),
                      pl.BlockSpec((None, tr, tc), lambda i, j, w: (w[1], i, j)),
                      pl.BlockSpec((3, tr, tc), lambda i, j, w: (0, i, j)), blk, blk, blk],
            out_specs=[blk] * 4),
        out_shape=[_sds((R, Cc), F32)] * 4, compiler_params=_cparams(),
    )(where, g, r1, r2, w, m, v)


def _adamw(w, g, m, v, name):
    R, Cc = w.shape
    tr, tc = _tiles(R, Cc)

    def body(w_ref, g_ref, m_ref, v_ref, d_ref, nm_ref, nv_ref):
        _adamw_update(g_ref[...], w_ref, m_ref, v_ref, d_ref, nm_ref, nv_ref)

    blk = pl.BlockSpec((tr, tc), lambda i, j: (i, j))
    return pl.pallas_call(
        body, name=name, grid=(R // tr, Cc // tc), in_specs=[blk] * 4, out_specs=[blk] * 3,
        out_shape=[_sds((R, Cc), F32)] * 3, compiler_params=_cparams(),
    )(w, g, m, v)


SMALL = ("ffn1_norm", "mix_norm", "ffn2_norm", "q_norm_a", "k_norm_a", "q_norm_b", "k_norm_b", "forget_bias",
         "rel_bias_table")
LARGE = ("ffn1_w_in", "ffn1_w_out", "w_in", "w_out", "ffn2_w_in", "ffn2_w_out")
ORDER = ("ffn1_norm", "ffn1_w_in", "ffn1_w_out", "mix_norm", "w_in", "q_norm_a", "k_norm_a", "q_norm_b", "k_norm_b",
         "forget_bias", "rel_bias_table", "w_out", "ffn2_norm", "ffn2_w_in", "ffn2_w_out")


def _pack_small(vals):
    rows = []
    for name in SMALL:
        flat = vals[name].reshape(-1)
        pad = (-flat.shape[0]) % HEAD
        rows.append(jnp.pad(flat, (0, pad)).reshape(-1, HEAD))
    return jnp.concatenate(rows, axis=0)


def _unpack_small(packed, like):
    out, r = {}, 0
    for name in SMALL:
        size = like[name].size
        nrow = -(-size // HEAD)
        out[name] = packed[r:r + nrow].reshape(-1)[:size].reshape(like[name].shape)
        r += nrow
    return out


def kernel(x, ffn1_norm, ffn1_w_in, ffn1_w_out, mix_norm, w_in, q_norm_a, k_norm_a, q_norm_b, k_norm_b, forget_bias, rel_bias_table, w_out, ffn2_norm, ffn2_w_in, ffn2_w_out, loss_target, m_ffn1_norm, m_ffn1_w_in, m_ffn1_w_out, m_mix_norm, m_w_in, m_q_norm_a, m_k_norm_a, m_q_norm_b, m_k_norm_b, m_forget_bias, m_rel_bias_table, m_w_out, m_ffn2_norm, m_ffn2_w_in, m_ffn2_w_out, v_ffn1_norm, v_ffn1_w_in, v_ffn1_w_out, v_mix_norm, v_w_in, v_q_norm_a, v_k_norm_a, v_q_norm_b, v_k_norm_b, v_forget_bias, v_rel_bias_table, v_w_out, v_ffn2_norm, v_ffn2_w_in, v_ffn2_w_out):
    w = dict(ffn1_norm=ffn1_norm, ffn1_w_in=ffn1_w_in, ffn1_w_out=ffn1_w_out, mix_norm=mix_norm, w_in=w_in,
             q_norm_a=q_norm_a, k_norm_a=k_norm_a, q_norm_b=q_norm_b, k_norm_b=k_norm_b, forget_bias=forget_bias,
             rel_bias_table=rel_bias_table, w_out=w_out, ffn2_norm=ffn2_norm, ffn2_w_in=ffn2_w_in, ffn2_w_out=ffn2_w_out)
    m = dict(ffn1_norm=m_ffn1_norm, ffn1_w_in=m_ffn1_w_in, ffn1_w_out=m_ffn1_w_out, mix_norm=m_mix_norm, w_in=m_w_in,
             q_norm_a=m_q_norm_a, k_norm_a=m_k_norm_a, q_norm_b=m_q_norm_b, k_norm_b=m_k_norm_b,
             forget_bias=m_forget_bias, rel_bias_table=m_rel_bias_table, w_out=m_w_out, ffn2_norm=m_ffn2_norm,
             ffn2_w_in=m_ffn2_w_in, ffn2_w_out=m_ffn2_w_out)
    v = dict(ffn1_norm=v_ffn1_norm, ffn1_w_in=v_ffn1_w_in, ffn1_w_out=v_ffn1_w_out, mix_norm=v_mix_norm, w_in=v_w_in,
             q_norm_a=v_q_norm_a, k_norm_a=v_k_norm_a, q_norm_b=v_q_norm_b, k_norm_b=v_k_norm_b,
             forget_bias=v_forget_bias, rel_bias_table=v_rel_bias_table, w_out=v_w_out, ffn2_norm=v_ffn2_norm,
             ffn2_w_in=v_ffn2_w_in, ffn2_w_out=v_ffn2_w_out)
    T, D = x.shape[1], x.shape[2]
    C = D // 2
    H = C // HEAD
    ff_shard = ffn1_w_out.shape[1]

    f1i, = _run_side(_all_gather_relayed([ffn1_w_in[0].T.astype(BF16)]), "gather_ffn1")
    wts = dict(ffn1_in=f1i.reshape(2, N_DEV, ff_shard, D))
    xi, yi, ci = lax.axis_index("x"), lax.axis_index("y"), lax.axis_index("c")
    core = jnp.reshape(ci, (1,)).astype(jnp.int32)
    where = jnp.stack([4 * xi + 2 * yi + ci, 2 * xi + yi]).astype(jnp.int32)
    gs, r1, ps, r2 = {}, {}, {}, {}

    def shard(name):
        name, _, part = name.partition(":")
        s = (w[name][0].T if name.endswith("w_in") else w[name][0]).astype(BF16)
        first = -(-(s.shape[0] // 2) // 16) * 16
        return {"": s, "first": s[:first], "rest": s[first:]}[part]

    def by_destination(name, grads):
        key = name + "_t" if name.endswith("w_in") else name
        gs[name] = grads[key].reshape(N_DEV, -1, D)
        return grads.get(key + "_bf16", grads[key]).reshape(N_DEV, -1, D)

    def chip_sums(names):
        for name in names:
            ps[name] = _chip_sum(gs[name], r1[name], core, "chip_sum_" + name)
        return [ps[name] for name in names]

    class Plan:
        carried = {"ffn1_up": ("gather", ("ffn1_w_out", "w_in:first")), "ffn1_down": ("gather", ("w_in:rest", "w_out")),
                   "dil_fwd": ("gather", ("ffn2_w_out",)), "fox_fwd": ("gather", ("ffn2_w_in",)),
                   "dil_bwd": ("in_chip", ("ffn2_w_in", "ffn2_w_out")), "fox_bwd": ("between", ("ffn2_w_in", "ffn2_w_out")),
                   "ffn1_bwd": ("in_chip", ("w_in", "w_out")), "ffn1_dwin": ("between", ("w_in", "w_out")),
                   "ffn1_rms_bwd": ("in_chip", ("ffn1_w_in",)), "ffn1_dwout": ("between", ("ffn1_w_in",))}

        def before(self, host, wts, grads):
            if host not in self.carried:
                return None
            kind, names = self.carried[host]
            if kind == "gather":
                return _all_gather([shard(n) for n in names])
            if kind == "in_chip":
                return _exchange_in_chip([by_destination(n, grads) for n in names])
            return _exchange_between_chips(chip_sums(names))

        def after(self, host, res, wts, grads):
            kind, names = self.carried[host]
            if host == "ffn1_up":
                wts.update(ffn1_out=res[0], w_in_first_rows=res[1])
            elif host == "ffn1_down":
                w_in_t = jnp.concatenate([wts.pop("w_in_first_rows"), res[0]], axis=1).reshape(-1, D)
                wts.update(w_in_t=w_in_t, w_f_t=jnp.pad(w_in_t[6 * C:], ((0, HEAD - H), (0, 0))),
                           w_o=res[1].reshape(2 * C, D))
            elif host == "dil_fwd":
                wts.update(ffn2_out=res[0])
            elif host == "fox_fwd":
                wts.update(ffn2_in=res[0].reshape(2, N_DEV, ff_shard, D))
            else:
                (r1 if kind == "in_chip" else r2).update(zip(names, res))

    small = {name: w[name] for name in SMALL}
    loss_row, grad_x, grads = _local_step(x[0], loss_target[0], small, wts, Plan())

    tail = ("ffn1_w_out",)
    r1[tail[0]], = _run_side(_exchange_in_chip([by_destination(tail[0], grads)]), "reduce_in_chip_tail")
    r2.update(zip(tail, _run_side(_exchange_between_chips(chip_sums(tail)), "reduce_between_chips_tail")))

    packed = _pack_small(grads)
    nsmall = packed.shape[0]
    packed = jnp.concatenate([packed, loss_row, jnp.zeros(((-nsmall - 1) % 8, HEAD), F32)], axis=0)
    reduced = _all_reduce_small(packed, "reduce_small")
    loss = reduced[nsmall, 0]
    g_small = _unpack_small(reduced[:nsmall], small)

    grad, delta, new_m, new_v = dict(g_small), {}, {}, {}
    for name in LARGE:
        to = (lambda t: t[0].T) if name.endswith("w_in") else (lambda t: t[0])
        back = (lambda t: t.T[None]) if name.endswith("w_in") else (lambda t: t[None])
        res = _reduce_adamw(gs[name], r1[name], r2[name], where, to(w[name]), to(m[name]), to(v[name]), "adamw_" + name)
        grad[name], delta[name], new_m[name], new_v[name] = (back(t) for t in res)
    d, nm, nv = _adamw(_pack_small(w), reduced[:nsmall], _pack_small(m), _pack_small(v), "adamw_small")
    delta.update(_unpack_small(d, small))
    new_m.update(_unpack_small(nm, small))
    new_v.update(_unpack_small(nv, small))
    return (loss, grad_x[None], *[grad[n] for n in ORDER], *[delta[n] for n in ORDER],
            *[new_m[n] for n in ORDER], *[new_v[n] for n in ORDER])
```

```python
import functools
import math

import numpy as np
import jax
import jax.numpy as jnp
from jax import lax
from jax.experimental import pallas as pl
from jax.experimental.pallas import tpu as pltpu

F32, BF16 = jnp.float32, jnp.bfloat16
HEAD = 128
NSLAB = 16
BLK = 128
DILATIONS = (1, 4, 16)
NUM_BUCKETS, MAX_DISTANCE = 32, 2048
RMS_EPS = 1e-6
NEG = -1e30
SCALE = HEAD ** -0.5
LR, B1, B2, EPS, WD, STEP = 0.001, 0.9, 0.999, 1e-08, 0.01, 10
N_DEV = 8
VMEM_LIMIT_BYTES = 56 << 20
ELEMENTWISE_BLOCK_BYTES = 2 << 20
BF16_TILE_ROWS = 16
MESH = pl.DeviceIdType.MESH


def _cparams(**kw):
    return pltpu.CompilerParams(vmem_limit_bytes=VMEM_LIMIT_BYTES, **kw)


def _nn(a, b):
    return jnp.dot(a, b, preferred_element_type=F32)


def _nt(a, b):
    return lax.dot_general(a, b, (((1,), (1,)), ((), ())), preferred_element_type=F32)


def _tn(a, b):
    return lax.dot_general(a, b, (((0,), (0,)), ((), ())), preferred_element_type=F32)


def _sds(shape, dtype):
    return jax.ShapeDtypeStruct(shape, dtype)


ANY = pl.BlockSpec(memory_space=pl.ANY)


class _Side:
    def __init__(self, ins, outs, sems, start, finish):
        self.ins, self.outs, self.sems, self.start, self.finish = list(ins), list(outs), list(sems), start, finish


def _call(body, *, name, grid, in_specs, out_specs, out_shape, args, scratch_shapes=(), side=None):
    in_specs, out_specs, out_shape = list(in_specs), list(out_specs), list(out_shape)
    scratch_shapes = list(scratch_shapes)
    if side is None:
        return pl.pallas_call(body, name=name, grid=grid, in_specs=in_specs, out_specs=out_specs, out_shape=out_shape,
                              scratch_shapes=scratch_shapes, compiler_params=_cparams())(*args)
    ni, no, ns = len(args), len(out_shape), len(scratch_shapes)
    si, so = len(side.ins), len(side.outs)

    def fused(*refs):
        h_in, s_in = refs[:ni], refs[ni:ni + si]
        h_out, s_out = refs[ni + si:ni + si + no], refs[ni + si + no:ni + si + no + so]
        h_scr, s_sem = refs[ni + si + no + so:ni + si + no + so + ns], refs[ni + si + no + so + ns:]
        ids = [pl.program_id(k) for k in range(len(grid))]
        first = functools.reduce(jnp.logical_and, [i == 0 for i in ids])
        last = functools.reduce(jnp.logical_and, [i == n - 1 for i, n in zip(ids, grid)])

        @pl.when(first)
        def _():
            side.start(s_in, s_out, s_sem)

        body(*h_in, *h_out, *h_scr)

        @pl.when(last)
        def _():
            side.finish(s_in, s_out, s_sem)

    res = pl.pallas_call(
        fused, name=name, grid=grid, in_specs=in_specs + [ANY] * si, out_specs=out_specs + [ANY] * so,
        out_shape=out_shape + side.outs, scratch_shapes=scratch_shapes + side.sems, compiler_params=_cparams(),
    )(*args, *side.ins)
    return list(res[:no]), list(res[no:])


def _ffn_fwd(x, g, win, wout, tm, name, side=None, target=None):
    T, D = x.shape
    nc, tf = win.shape[1], win.shape[2]
    down, loss = wout is not None, target is not None
    assert down or not loss

    def body(x_ref, g_ref, win_ref, *refs):
        ins, outs = refs[:down + loss], refs[down + loss:]
        h_ref, gu_ref, act_ref = outs[down:down + 3]
        i, j = pl.program_id(0), pl.program_id(1)

        @pl.when(j == 0)
        def _():
            xv = x_ref[...]
            r = lax.rsqrt(jnp.mean(xv * xv, axis=-1, keepdims=True) + RMS_EPS)
            h_ref[...] = (xv * r * g_ref[...]).astype(BF16)
            if down:
                outs[0][...] = jnp.zeros_like(outs[0])

        if loss:
            @pl.when((i == 0) & (j == 0))
            def _():
                outs[-1][...] = jnp.zeros_like(outs[-1])

        hb = h_ref[...]
        gt = _nt(win_ref[0], hb)
        up = _nt(win_ref[1], hb)
        gu_ref[0] = gt.astype(BF16)
        gu_ref[1] = up.astype(BF16)
        act = (gt * jax.nn.sigmoid(gt) * up).astype(BF16)
        act_ref[...] = act
        if down:
            y_ref = outs[0]
            y_ref[...] += _tn(act, ins[0][...])

            @pl.when(j == nc - 1)
            def _():
                y = x_ref[...] + 0.5 * y_ref[...]
                if not loss:
                    y_ref[...] = y
                    return
                err = y - ins[1][...]
                y_ref[...] = err * (1.0 / D)
                tot = 0.5 * jnp.sum(jnp.mean(err * err, axis=-1, keepdims=True), axis=0, keepdims=True)
                lane = lax.broadcasted_iota(jnp.int32, (1, HEAD), 1)
                outs[-1][...] += jnp.where(lane == 0, tot, 0.0)

    row = pl.BlockSpec((tm, D), lambda i, j: (i, 0))
    return _call(
        body, name=name, grid=(T // tm, nc), side=side,
        in_specs=[row, pl.BlockSpec((1, D), lambda i, j: (0, 0)),
                  pl.BlockSpec((2, None, tf, D), lambda i, j: (0, j, 0, 0))]
        + ([pl.BlockSpec((None, tf, D), lambda i, j: (j, 0, 0))] if down else []) + ([row] if loss else []),
        out_specs=([row] if down else [])
        + [row, pl.BlockSpec((2, None, tf, tm), lambda i, j: (0, j, 0, i)),
           pl.BlockSpec((None, tf, tm), lambda i, j: (j, 0, i))]
        + ([pl.BlockSpec((1, HEAD), lambda i, j: (0, 0))] if loss else []),
        out_shape=([_sds((T, D), F32)] if down else [])
        + [_sds((T, D), BF16), _sds((2, nc, tf, T), BF16), _sds((nc, tf, T), BF16)]
        + ([_sds((1, HEAD), F32)] if loss else []),
        args=(x, g, win) + ((wout,) if down else ()) + ((target,) if loss else ()))


def _ffn_down(x, act, wout, tm, name, side=None):
    T, D = x.shape
    nc, tf = wout.shape[0], wout.shape[1]

    def body(x_ref, act_ref, wout_ref, y_ref):
        j = pl.program_id(1)

        @pl.when(j == 0)
        def _():
            y_ref[...] = jnp.zeros_like(y_ref)

        y_ref[...] += _tn(act_ref[...], wout_ref[...])

        @pl.when(j == nc - 1)
        def _():
            y_ref[...] = x_ref[...] + 0.5 * y_ref[...]

    row = pl.BlockSpec((tm, D), lambda i, j: (i, 0))
    res = _call(
        body, name=name, grid=(T // tm, nc), side=side,
        in_specs=[row, pl.BlockSpec((None, tf, tm), lambda i, j: (j, 0, i)),
                  pl.BlockSpec((None, tf, D), lambda i, j: (j, 0, 0))],
        out_specs=[row], out_shape=[_sds((T, D), F32)], args=(x, act, wout))
    return res[0] if side is None else (res[0][0], res[1])


def _ffn_bwd(dy, gu, win, wout, tm, name, side=None):
    T, D = dy.shape
    nc, tf = wout.shape[0], wout.shape[1]

    def body(dy_ref, gu_ref, win_ref, wout_ref, dh_ref, dgu_ref, dyb_ref):
        j = pl.program_id(1)

        @pl.when(j == 0)
        def _():
            dh_ref[...] = jnp.zeros_like(dh_ref)
            dyb_ref[...] = (0.5 * dy_ref[...]).astype(BF16)

        dact = _nt(wout_ref[...], dyb_ref[...])
        gt = gu_ref[0].astype(F32)
        up = gu_ref[1].astype(F32)
        s = jax.nn.sigmoid(gt)
        dgb = (dact * up * (s * (1.0 + gt * (1.0 - s)))).astype(BF16)
        dub = (dact * (gt * s)).astype(BF16)
        dgu_ref[0] = dgb
        dgu_ref[1] = dub
        dh_ref[...] += _tn(dgb, win_ref[0]) + _tn(dub, win_ref[1])

    return _call(
        body, name=name, grid=(T // tm, nc), side=side,
        in_specs=[pl.BlockSpec((tm, D), lambda i, j: (i, 0)),
                  pl.BlockSpec((2, None, tf, tm), lambda i, j: (0, j, 0, i)),
                  pl.BlockSpec((2, None, tf, D), lambda i, j: (0, j, 0, 0)),
                  pl.BlockSpec((None, tf, D), lambda i, j: (j, 0, 0))],
        out_specs=[pl.BlockSpec((tm, D), lambda i, j: (i, 0)),
                   pl.BlockSpec((2, None, tf, tm), lambda i, j: (0, j, 0, i)),
                   pl.BlockSpec((tm, D), lambda i, j: (i, 0))],
        out_shape=[_sds((T, D), F32), _sds((2, nc, tf, T), BF16), _sds((T, D), BF16)],
        args=(dy, gu, win, wout))


def _rms_bwd(x, g, dh, dres, tm, name, side=None):
    T, D = x.shape

    def body(x_ref, g_ref, dh_ref, dres_ref, dx_ref, dg_ref):
        @pl.when(pl.program_id(0) == 0)
        def _():
            dg_ref[...] = jnp.zeros_like(dg_ref)

        xv = x_ref[...]
        r = lax.rsqrt(jnp.mean(xv * xv, axis=-1, keepdims=True) + RMS_EPS)
        xhat = xv * r
        dh = dh_ref[...]
        gd = dh * g_ref[...]
        dx_ref[...] = dres_ref[...] + r * (gd - xhat * jnp.mean(gd * xhat, axis=-1, keepdims=True))
        dg_ref[...] += jnp.sum(dh * xhat, axis=0, keepdims=True)

    row = pl.BlockSpec((tm, D), lambda i: (i, 0))
    one = pl.BlockSpec((1, D), lambda i: (0, 0))
    return _call(body, name=name, grid=(T // tm,), side=side, in_specs=[row, one, row, row], out_specs=[row, one],
                 out_shape=[_sds((T, D), F32), _sds((1, D), F32)], args=(x, g, dh, dres))


def _mm_tn(a, b, *, bm, bn, bt, name, b_slabs=False, side=None, rows=None, m_off=0, into=None, a_rows=False,
           twin=False):
    nz, T, M = (a.shape[0], a.shape[2], a.shape[1]) if a_rows else a.shape
    if b_slabs:
        N = b.shape[1] // NSLAB
        assert bt == T // NSLAB
        b_spec = pl.BlockSpec((bt, bn), lambda n, z, m, t: (0, t * (N // bn) + n))
    else:
        N = b.shape[1]
        b_spec = pl.BlockSpec((bt, bn), lambda n, z, m, t: (t, n))
    assert M % bm == 0 and N % bn == 0 and T % bt == 0, (M, bm, N, bn, T, bt)

    def body(a_ref, b_ref, *rest):
        c_ref = rest[-2] if twin else rest[-1]

        @pl.when(pl.program_id(3) == 0)
        def _():
            c_ref[...] = jnp.zeros_like(c_ref)

        ab, bb = a_ref[...].astype(BF16), b_ref[...].astype(BF16)
        c_ref[...] += _nn(ab, bb) if a_rows else _tn(ab, bb)
        if twin:
            @pl.when(pl.program_id(3) == T // bt - 1)
            def _():
                rest[-1][...] = c_ref[...].astype(BF16)

    grid = (N // bn, nz, M // bm, T // bt)
    a_spec = (pl.BlockSpec((None, bm, bt), lambda n, z, m, t: (z, m, t)) if a_rows
              else pl.BlockSpec((None, bt, bm), lambda n, z, m, t: (z, t, m)))
    in_specs = [a_spec, b_spec]
    out_spec = pl.BlockSpec((None, bm, bn), lambda n, z, m, t: (z, m + m_off, n))
    out_shape = _sds((nz, M if rows is None else rows, N), F32)
    if into is not None:
        assert side is None and not twin and into.shape == out_shape.shape
        return pl.pallas_call(body, name=name, grid=grid, in_specs=in_specs + [ANY], out_specs=out_spec,
                              out_shape=out_shape, input_output_aliases={2: 0}, compiler_params=_cparams())(a, b, into)
    outs = [out_shape] + ([_sds(out_shape.shape, BF16)] if twin else [])
    res = _call(body, name=name, grid=grid, side=side, in_specs=in_specs, out_specs=[out_spec] * len(outs),
                out_shape=outs, args=(a, b))
    mine = res if side is None else res[0]
    mine = tuple(mine) if twin else mine[0]
    return mine if side is None else (mine, res[1])


def _tok_spec(layout, tm, n16, C, bc, colmap):
    if layout == "nat":
        return pl.BlockSpec((tm, bc), lambda i, k: (i, colmap(k)))
    assert tm % n16 == 0
    if layout == "slab":
        return pl.BlockSpec((tm // n16, n16, bc), lambda i, k: (i, 0, colmap(k)))
    assert bc == C
    return pl.BlockSpec((n16, (tm // n16) * C), lambda i, k: (0, i))


def _tok_load(ref, layout, sp):
    if layout == "nat":
        return ref[...]
    if layout == "slab":
        return ref[...].reshape(-1, ref.shape[-1])
    c = ref.shape[1] // sp
    return jnp.concatenate([ref[:, s * c:(s + 1) * c] for s in range(sp)], axis=0)


def _tok_store(ref, layout, sp, val, cols=None, accumulate=False):
    def put(idx, v):
        if accumulate:
            ref[idx] += v
        else:
            ref[idx] = v

    lanes = slice(None) if cols is None else slice(cols[0], cols[0] + cols[1])
    if layout == "nat":
        put((slice(None), lanes), val)
    elif layout == "slab":
        put((slice(None), slice(None), lanes), val.reshape(sp, ref.shape[1], val.shape[-1]))
    else:
        assert cols is None
        c, n = ref.shape[1] // sp, ref.shape[0]
        for s in range(sp):
            put((slice(None), slice(s * c, (s + 1) * c)), val[s * n:(s + 1) * n])


def _proj(x, g, wt, gains, modes, *, tn, w_off, slabs, tm, normed_dtype, name, small_wt=None):
    T, D = x.shape
    ntile = len(modes)
    N = ntile * tn
    n16 = T // NSLAB
    in_layout, out_layout = ("view", "slab") if slabs else ("nat", "nat")
    sp = tm // n16
    extra = int(small_wt is not None)
    assert not (extra and slabs)
    x_in = x.reshape(n16, NSLAB * D) if slabs else x
    x_spec = _tok_spec(in_layout, tm, n16, D, D, lambda n: 0)
    oshape = lambda c: (NSLAB, n16, c) if slabs else (T, c)
    ospec = lambda bc, cm: _tok_spec(out_layout, tm, n16, None, bc, cm)

    def body(x_ref, g_ref, w_ref, gains_ref, *refs):
        raw_ref, nrm_ref, h_ref = refs[-3 - extra:len(refs) - extra]
        n = pl.program_id(1)

        @pl.when(n == 0)
        def _():
            xv = _tok_load(x_ref, in_layout, sp)
            r = lax.rsqrt(jnp.mean(xv * xv, axis=-1, keepdims=True) + RMS_EPS)
            hb = (xv * r * g_ref[...]).astype(BF16)
            _tok_store(h_ref, out_layout, sp, hb)
            if extra:
                refs[-1][...] = _nt(hb, refs[0][...])

        y = _nt(_tok_load(h_ref, out_layout, sp), w_ref[...])
        _tok_store(raw_ref, out_layout, sp, y)
        for t, mode in enumerate(modes):
            @pl.when(n == t)
            def _(t=t, mode=mode):
                if not mode:
                    _tok_store(nrm_ref, out_layout, sp, y.astype(nrm_ref.dtype))
                    return
                gain = gains_ref[t]
                for k in range(tn // HEAD):
                    yk = y[:, k * HEAD:(k + 1) * HEAD]
                    r = lax.rsqrt(jnp.mean(yk * yk, axis=-1, keepdims=True) + RMS_EPS)
                    _tok_store(nrm_ref, out_layout, sp, (yk * r * gain).astype(nrm_ref.dtype), cols=(k * HEAD, HEAD))

    return pl.pallas_call(
        body, name=name, grid=(T // tm, ntile),
        in_specs=[x_spec, pl.BlockSpec((1, D), lambda i, n: (0, 0)),
                  pl.BlockSpec((tn, D), lambda i, n: (n + w_off, 0)),
                  pl.BlockSpec((ntile, 1, HEAD), lambda i, n: (0, 0, 0))]
        + ([pl.BlockSpec((HEAD, D), lambda i, n: (0, 0))] if extra else []),
        out_specs=[ospec(tn, lambda n: n), ospec(tn, lambda n: n), ospec(D, lambda n: 0)]
        + ([ospec(HEAD, lambda n: 0)] if extra else []),
        out_shape=[_sds(oshape(N), F32), _sds(oshape(N), normed_dtype), _sds(oshape(D), BF16)]
        + ([_sds((T, HEAD), F32)] if extra else []),
        compiler_params=_cparams(),
    )(x_in, g, wt, gains, *([small_wt] if extra else []))


def _mm(a, w, *, nt, tk, tm, a_layout, out_layout, resid=None, name, w_off=0, n_out=None, small=None):
    if a_layout == "slab":
        T, K = a.shape[0] * a.shape[1], a.shape[2]
    else:
        T, K = a.shape
    N = (w.shape[0] if nt else w.shape[1]) if n_out is None else n_out
    n16 = T // NSLAB
    nk = K // tk
    sp = tm // n16
    a_in = a.reshape(n16, NSLAB * K) if a_layout == "view" else a
    w_spec = (pl.BlockSpec((N, tk), lambda i, k: (w_off, k)) if nt
              else pl.BlockSpec((tk, N), lambda i, k: (k + w_off, 0)))
    o_spec = _tok_spec(out_layout, tm, n16, N, N, lambda k: 0)
    oshape = {"nat": (T, N), "slab": (NSLAB, n16, N), "view": (n16, NSLAB * N)}[out_layout]
    has_resid = resid is not None

    def body(*refs):
        a_ref, w_ref = refs[0], refs[1]
        o_ref = refs[-1]
        k = pl.program_id(1)

        @pl.when(k == 0)
        def _():
            o_ref[...] = refs[2][...] if has_resid else jnp.zeros_like(o_ref)
            if small is not None:
                o_ref[...] += _nn(refs[-3][...].astype(BF16), refs[-2][...])

        ab = _tok_load(a_ref, a_layout, sp).astype(BF16)
        _tok_store(o_ref, out_layout, sp, _nt(ab, w_ref[...]) if nt else _nn(ab, w_ref[...]), accumulate=True)

    ins = [a_in, w]
    in_specs = [_tok_spec(a_layout, tm, n16, K, tk, lambda k: k), w_spec]
    if has_resid:
        ins.append(resid.reshape(n16, NSLAB * N) if out_layout == "view" else resid)
        in_specs.append(o_spec)
    if small is not None:
        assert out_layout == "nat" and small[1].shape == (HEAD, N)
        ins += list(small)
        in_specs += [pl.BlockSpec((tm, HEAD), lambda i, k: (i, 0)), pl.BlockSpec((HEAD, N), lambda i, k: (0, 0))]
    out = pl.pallas_call(
        body, name=name, grid=(T // tm, nk), in_specs=in_specs, out_specs=o_spec,
        out_shape=_sds(oshape, F32), compiler_params=_cparams(),
    )(*ins)
    return out.reshape(T, N) if out_layout == "view" else out


def _log_sigmoid(z):
    return jnp.minimum(z, 0.0) - jnp.log(1.0 + jnp.exp(-jnp.abs(z)))


def _fox_gate_fwd(f_raw, fbias, name):
    T = f_raw.shape[0]
    cb = 256

    def body(f_ref, b_ref, c_ref):
        row = lax.broadcasted_iota(jnp.int32, (cb, cb), 0)
        col = lax.broadcasted_iota(jnp.int32, (cb, cb), 1)
        tri = (col <= row).astype(F32)
        carry = jnp.zeros((1, HEAD), F32)
        for i in range(T // cb):
            lf = _log_sigmoid(f_ref[i * cb:(i + 1) * cb, :] + b_ref[...])
            c = jnp.dot(tri, lf, preferred_element_type=F32, precision=lax.Precision.HIGHEST) + carry
            c_ref[i * cb:(i + 1) * cb, :] = c
            carry = c[cb - 1:cb, :]

    return pl.pallas_call(body, name=name, out_shape=_sds((T, HEAD), F32), compiler_params=_cparams())(f_raw, fbias)


def _fox_gate_bwd(f_raw, fbias, dc, name):
    T = f_raw.shape[0]
    cb = 256

    def body(f_ref, b_ref, dc_ref, df_ref, db_ref):
        row = lax.broadcasted_iota(jnp.int32, (cb, cb), 0)
        col = lax.broadcasted_iota(jnp.int32, (cb, cb), 1)
        tri = (col >= row).astype(F32)
        carry = jnp.zeros((1, HEAD), F32)
        dbias = jnp.zeros((1, HEAD), F32)
        for i in reversed(range(T // cb)):
            dlf = jnp.dot(tri, dc_ref[i * cb:(i + 1) * cb, :], preferred_element_type=F32,
                          precision=lax.Precision.HIGHEST) + carry
            carry = dlf[0:1, :]
            z = f_ref[i * cb:(i + 1) * cb, :] + b_ref[...]
            df = dlf * jax.nn.sigmoid(-z)
            df_ref[i * cb:(i + 1) * cb, :] = df
            dbias = dbias + jnp.sum(df, axis=0, keepdims=True)
        db_ref[...] = dbias

    return pl.pallas_call(body, name=name, out_shape=[_sds((T, HEAD), F32), _sds((1, HEAD), F32)],
                          compiler_params=_cparams())(f_raw, fbias, dc)


def _fox_fwd(qkv, c_col, c_row, tq, name, side=None):
    T = qkv.shape[0]
    H = qkv.shape[1] // (3 * HEAD)
    nq = T // tq
    c_blocks = c_row.reshape(H, nq, 1, tq)

    def body(q_ref, k_ref, v_ref, cq_ref, ck_ref, o_ref, lse_ref):
        qi = pl.program_id(1)
        q, cq = q_ref[...], cq_ref[...]
        causal = lax.broadcasted_iota(jnp.int32, (tq, tq), 1) <= lax.broadcasted_iota(jnp.int32, (tq, tq), 0)

        def key_block(ki, carry, diagonal):
            m, l, acc = carry
            rows = pl.ds(pl.multiple_of(ki * tq, tq), tq)
            s = _nt(q, k_ref[rows, :]) * SCALE + cq - ck_ref[ki]
            if diagonal:
                s = jnp.where(causal, s, NEG)
            m_new = jnp.maximum(m, jnp.max(s, axis=-1, keepdims=True))
            alpha = jnp.exp(m - m_new)
            p = jnp.exp(s - m_new)
            l = alpha * l + jnp.sum(p, axis=-1, keepdims=True)
            acc = alpha * acc + _nn(p.astype(BF16), v_ref[rows, :])
            return m_new, l, acc

        init = (jnp.full((tq, 1), NEG, F32), jnp.zeros((tq, 1), F32), jnp.zeros((tq, HEAD), F32))
        carry = lax.fori_loop(0, qi, lambda ki, c: key_block(ki, c, False), init)
        m, l, acc = key_block(qi, carry, True)
        o_ref[...] = acc / l
        lse_ref[...] = m + jnp.log(l)

    return _call(
        body, name=name, grid=(H, nq), side=side,
        in_specs=[pl.BlockSpec((tq, HEAD), lambda h, qi: (qi, h)),
                  pl.BlockSpec((T, HEAD), lambda h, qi: (0, H + h)),
                  pl.BlockSpec((T, HEAD), lambda h, qi: (0, 2 * H + h)),
                  pl.BlockSpec((None, tq, 1), lambda h, qi: (h, qi, 0)),
                  pl.BlockSpec((None, nq, 1, tq), lambda h, qi: (h, 0, 0, 0))],
        out_specs=[pl.BlockSpec((tq, HEAD), lambda h, qi: (qi, h)),
                   pl.BlockSpec((None, tq, 1), lambda h, qi: (h, qi, 0))],
        out_shape=[_sds((T, H * HEAD), F32), _sds((H, T, 1), F32)],
        args=(qkv, qkv, qkv, c_col, c_blocks))


def _fox_bwd(qkv, c_col, c_row, out, dout, lse, tq, name, side=None):
    T = qkv.shape[0]
    H = qkv.shape[1] // (3 * HEAD)
    nq = T // tq

    def body(q_ref, k_ref, v_ref, cq_ref, ck_ref, o_ref, do_ref, lse_ref, dq_ref, dk_ref, dv_ref, dck_ref, dcq_ref,
             delta_s):
        ki = pl.program_id(1)

        @pl.when(ki == 0)
        def _():
            dq_ref[...] = jnp.zeros_like(dq_ref)
            dcq_ref[...] = jnp.zeros_like(dcq_ref)
            delta_s[...] = jnp.sum(do_ref[...] * o_ref[...], axis=-1, keepdims=True)

        k, v, ck = k_ref[...], v_ref[...], ck_ref[...]
        causal = lax.broadcasted_iota(jnp.int32, (tq, tq), 1) <= lax.broadcasted_iota(jnp.int32, (tq, tq), 0)

        def query_block(qi, carry, diagonal):
            dk, dv, dck = carry
            rows = pl.ds(pl.multiple_of(qi * tq, tq), tq)
            q = q_ref[rows, :]
            s = _nt(q, k) * SCALE + cq_ref[rows, :] - ck
            if diagonal:
                s = jnp.where(causal, s, NEG)
            p = jnp.exp(s - lse_ref[rows, :])
            dob = do_ref[rows, :].astype(BF16)
            ds = p * (_nt(dob, v) - delta_s[rows, :])
            dsb = ds.astype(BF16)
            dq_ref[rows, :] += _nn(dsb, k) * SCALE
            dcq_ref[rows, :] += jnp.sum(ds, axis=-1, keepdims=True)
            return dk + _tn(dsb, q), dv + _tn(p.astype(BF16), dob), dck - jnp.sum(ds, axis=0, keepdims=True)

        init = (jnp.zeros((tq, HEAD), F32), jnp.zeros((tq, HEAD), F32), jnp.zeros((1, tq), F32))
        carry = query_block(ki, init, True)
        dk, dv, dck = lax.fori_loop(ki + 1, nq, lambda qi, c: query_block(qi, c, False), carry)
        dk_ref[...] = dk * SCALE
        dv_ref[...] = dv
        dck_ref[...] = dck

    head = lambda off: pl.BlockSpec((T, HEAD), lambda h, ki: (0, off + h))
    col = pl.BlockSpec((None, T, 1), lambda h, ki: (h, 0, 0))
    return _call(
        body, name=name, grid=(H, nq), side=side,
        in_specs=[head(0),
                  pl.BlockSpec((tq, HEAD), lambda h, ki: (ki, H + h)),
                  pl.BlockSpec((tq, HEAD), lambda h, ki: (ki, 2 * H + h)),
                  col, pl.BlockSpec((None, 1, tq), lambda h, ki: (h, 0, ki)), head(0), head(0), col],
        out_specs=[head(0),
                   pl.BlockSpec((tq, HEAD), lambda h, ki: (ki, h)),
                   pl.BlockSpec((tq, HEAD), lambda h, ki: (ki, h)),
                   pl.BlockSpec((None, 1, tq), lambda h, ki: (h, 0, ki)), col],
        out_shape=[_sds((T, H * HEAD), F32), _sds((T, H * HEAD), F32), _sds((T, H * HEAD), F32), _sds((H, 1, T), F32),
                   _sds((H, T, 1), F32)],
        scratch_shapes=[pltpu.VMEM((T, 1), F32)],
        args=(qkv, qkv, qkv, c_col, c_row, out, dout, lse))


def _t5_bucket(dist):
    max_exact = NUM_BUCKETS // 2
    d = dist.astype(np.float32)
    large = max_exact + (np.log(np.maximum(d, np.float32(1.0)) / np.float32(max_exact))
                         / np.float32(math.log(MAX_DISTANCE / max_exact))
                         * np.float32(NUM_BUCKETS - max_exact)).astype(np.int32)
    large = np.minimum(large, NUM_BUCKETS - 1)
    return np.where(dist < max_exact, dist, large)


def _bucket_maps():
    maps = []
    for d in DILATIONS:
        e = NSLAB // d
        rows = BLK // e
        idx = np.arange(BLK)
        pos = e * (idx % rows) + idx // rows
        qpos = pos[:, None] + BLK
        kpos = np.concatenate([pos, pos + BLK])[None, :]
        delta = qpos - kpos
        band = (delta >= 0) & (delta <= BLK)
        bucket = _t5_bucket(np.clip(delta, 0, None) * d)
        maps.append(np.where(band, bucket, -1).astype(np.int32))
    return np.stack(maps)


def _dil_geometry(T):
    n16 = T // NSLAB
    geo = []
    for d in DILATIONS:
        e = NSLAB // d
        rows = BLK // e
        nblk = n16 // rows
        geo.append((d, e, rows, nblk))
    return geo


DIL_INTERLEAVE_FWD = {1: 4, 4: 8, 16: 8}
DIL_INTERLEAVE_BWD = {1: 8, 4: 8, 16: 8}


def _dil_interleave(per_step, nblocks):
    while per_step > 1 and (nblocks % per_step or nblocks // per_step < 2):
        per_step -= 1
    return per_step


def _dil_bias(tab_ref, bkt_ref, bias_s, h):
    for p in range(len(DILATIONS)):
        bk = bkt_ref[p]
        bias = jnp.full((BLK, 2 * BLK), NEG, F32)
        for b in range(NUM_BUCKETS):
            bias = jnp.where(bk == b, tab_ref[b, h], bias)
        bias_s[p] = bias


def _dil_rows(d, e, rows, sub, blk):
    start = pl.multiple_of(blk * rows, rows)
    return [(sub + d * j, pl.ds(start, rows)) for j in range(e)]


def _gather(ref, idx):
    return jnp.concatenate([ref[s, r, :] for s, r in idx], axis=0)


def _scatter(ref, idx, val, rows):
    for j, (s, r) in enumerate(idx):
        ref[s, r, :] = val[j * rows:(j + 1) * rows]


def _scatter_add(ref, idx, val, rows):
    for j, (s, r) in enumerate(idx):
        ref[s, r, :] += val[j * rows:(j + 1) * rows]


def _dil_fwd(qkv, table, name, side=None):
    n16 = qkv.shape[1]
    T = NSLAB * n16
    H = qkv.shape[2] // (3 * HEAD)
    geo = _dil_geometry(T)
    bkt = jnp.asarray(_bucket_maps())

    def body(tab_ref, bkt_ref, q_ref, k_ref, v_ref, o_ref, lse_ref, bias_s, m_s, l_s):
        h = pl.program_id(0)
        _dil_bias(tab_ref, bkt_ref, bias_s, h)
        first_mask = lax.broadcasted_iota(jnp.int32, (BLK, 2 * BLK), 1) < BLK

        starts = len(DILATIONS) - 1

        def load(p, d, e, rows, sub, blk):
            cur = _dil_rows(d, e, rows, sub, blk)
            prev = _dil_rows(d, e, rows, sub, jnp.maximum(blk - 1, 0))
            q = _gather(q_ref, cur).astype(BF16)
            kk = jnp.concatenate([_gather(k_ref, prev), _gather(k_ref, cur)], axis=0).astype(BF16)
            vv = jnp.concatenate([_gather(v_ref, prev), _gather(v_ref, cur)], axis=0).astype(BF16)
            old = None if p == starts else (_gather(m_s, cur), _gather(l_s, cur), _gather(o_ref, cur))
            return cur, blk, q, kk, vv, old

        def compute(p, blk, q, kk, vv, old):
            s = _nt(q, kk) * SCALE + bias_s[p]
            s = jnp.where(first_mask & (blk == 0), NEG, s)
            m_blk = jnp.max(s, axis=-1, keepdims=True)
            if old is None:
                m_new = m_blk
                pr = jnp.exp(s - m_new)
                l_new = jnp.sum(pr, axis=-1, keepdims=True)
                acc = _nn(pr.astype(BF16), vv)
            else:
                m_old, l_old, acc_old = old
                m_new = jnp.maximum(m_old, m_blk)
                alpha = jnp.exp(m_old - m_new)
                pr = jnp.exp(s - m_new)
                l_new = alpha * l_old + jnp.sum(pr, axis=-1, keepdims=True)
                acc = alpha * acc_old + _nn(pr.astype(BF16), vv)
            if p == 0:
                return acc / l_new, m_new + jnp.log(l_new), None
            return acc, m_new, l_new

        def store(p, rows, cur, acc, m_new, l_new):
            _scatter(o_ref, cur, acc, rows)
            if p == 0:
                _scatter(lse_ref, cur, m_new, rows)
            else:
                _scatter(m_s, cur, m_new, rows)
                _scatter(l_s, cur, l_new, rows)

        for p in reversed(range(len(DILATIONS))):
            d, e, rows, nblk = geo[p]
            per_step = _dil_interleave(DIL_INTERLEAVE_FWD[d], d * nblk)

            def step(i, carry, p=p, d=d, e=e, rows=rows, nblk=nblk, per_step=per_step):
                ids = [i + u * (d * nblk // per_step) for u in range(per_step)]
                loaded = [load(p, d, e, rows, j // nblk, j % nblk) for j in ids]
                done = [(cur, compute(p, blk, q, kk, vv, old)) for cur, blk, q, kk, vv, old in loaded]
                for cur, res in done:
                    store(p, rows, cur, *res)
                return carry

            lax.fori_loop(0, d * nblk // per_step, step, 0)

    head = lambda off: pl.BlockSpec((NSLAB, n16, HEAD), lambda h: (0, 0, off + h))
    return _call(
        body, name=name, grid=(H,), side=side,
        in_specs=[pl.BlockSpec(memory_space=pltpu.SMEM), pl.BlockSpec((3, BLK, 2 * BLK), lambda h: (0, 0, 0)),
                  head(0), head(H), head(2 * H)],
        out_specs=[head(0), pl.BlockSpec((None, NSLAB, n16, 1), lambda h: (h, 0, 0, 0))],
        out_shape=[_sds((NSLAB, n16, H * HEAD), F32), _sds((H, NSLAB, n16, 1), F32)],
        scratch_shapes=[pltpu.VMEM((3, BLK, 2 * BLK), F32), pltpu.VMEM((NSLAB, n16, 1), F32),
                        pltpu.VMEM((NSLAB, n16, 1), F32)],
        args=(table, bkt, qkv, qkv, qkv))


def _dil_bwd(qkv, table, out, dout, lse, name, side=None):
    n16 = qkv.shape[1]
    T = NSLAB * n16
    H = qkv.shape[2] // (3 * HEAD)
    geo = _dil_geometry(T)
    bkt = jnp.asarray(_bucket_maps())

    def body(tab_ref, bkt_ref, q_ref, k_ref, v_ref, o_ref, do_ref, lse_ref,
             dq_ref, dk_ref, dv_ref, dtab_ref, bias_s, dbias_s, delta_s):
        h = pl.program_id(0)
        _dil_bias(tab_ref, bkt_ref, bias_s, h)
        first_mask = lax.broadcasted_iota(jnp.int32, (BLK, 2 * BLK), 1) < BLK
        dbias_s[...] = jnp.zeros_like(dbias_s)
        dq_ref[...] = jnp.zeros_like(dq_ref)
        dk_ref[...] = jnp.zeros_like(dk_ref)
        dv_ref[...] = jnp.zeros_like(dv_ref)
        for r in range(NSLAB):
            delta_s[r] = jnp.sum(do_ref[r] * o_ref[r], axis=-1, keepdims=True)

        def load(d, e, rows, sub, blk):
            cur = _dil_rows(d, e, rows, sub, blk)
            prev = _dil_rows(d, e, rows, sub, jnp.maximum(blk - 1, 0))
            q = _gather(q_ref, cur).astype(BF16)
            kk = jnp.concatenate([_gather(k_ref, prev), _gather(k_ref, cur)], axis=0).astype(BF16)
            vv = jnp.concatenate([_gather(v_ref, prev), _gather(v_ref, cur)], axis=0).astype(BF16)
            dob = _gather(do_ref, cur).astype(BF16)
            return cur, prev, blk, q, kk, vv, dob, _gather(lse_ref, cur), _gather(delta_s, cur)

        def compute(p, blk, q, kk, vv, dob, lse, delta):
            s = _nt(q, kk) * SCALE + bias_s[p]
            s = jnp.where(first_mask & (blk == 0), NEG, s)
            pr = jnp.exp(s - lse)
            ds = pr * (_nt(dob, vv) - delta)
            dsb = ds.astype(BF16)
            return ds, _nn(dsb, kk) * SCALE, _tn(dsb, q) * SCALE, _tn(pr.astype(BF16), dob)

        def store(rows, cur, prev, dq, dkk, dvv):
            _scatter_add(dq_ref, cur, dq, rows)
            _scatter_add(dk_ref, prev, dkk[:BLK], rows)
            _scatter_add(dk_ref, cur, dkk[BLK:], rows)
            _scatter_add(dv_ref, prev, dvv[:BLK], rows)
            _scatter_add(dv_ref, cur, dvv[BLK:], rows)

        for p in range(len(DILATIONS)):
            d, e, rows, nblk = geo[p]
            per_step = _dil_interleave(DIL_INTERLEAVE_BWD[d], d * nblk)

            def step(i, carry, p=p, d=d, e=e, rows=rows, nblk=nblk, per_step=per_step):
                ids = [i + u * (d * nblk // per_step) for u in range(per_step)]
                loaded = [load(d, e, rows, j // nblk, j % nblk) for j in ids]
                done = [(cur, prev, compute(p, *rest)) for cur, prev, *rest in loaded]
                dbias_s[p] += functools.reduce(jnp.add, [res[0] for _, _, res in done])
                for cur, prev, res in done:
                    store(rows, cur, prev, *res[1:])
                return carry

            lax.fori_loop(0, d * nblk // per_step, step, 0)

        lane = lax.broadcasted_iota(jnp.int32, (1, HEAD), 1)
        row = jnp.zeros((1, HEAD), F32)
        for b in range(NUM_BUCKETS):
            tot = jnp.zeros((1, 1), F32)
            for p in range(len(DILATIONS)):
                hit = jnp.where(bkt_ref[p] == b, dbias_s[p], 0.0)
                tot = tot + jnp.sum(jnp.sum(hit, axis=0, keepdims=True), axis=1, keepdims=True)
            row = jnp.where(lane == b, tot, row)
        dtab_ref[...] = row

    head = lambda off: pl.BlockSpec((NSLAB, n16, HEAD), lambda h: (0, 0, off + h))
    return _call(
        body, name=name, grid=(H,), side=side,
        in_specs=[pl.BlockSpec(memory_space=pltpu.SMEM), pl.BlockSpec((3, BLK, 2 * BLK), lambda h: (0, 0, 0)),
                  head(0), head(H), head(2 * H), head(0), head(0),
                  pl.BlockSpec((None, NSLAB, n16, 1), lambda h: (h, 0, 0, 0))],
        out_specs=[head(0), head(0), head(0), pl.BlockSpec((None, 1, HEAD), lambda h: (h, 0, 0))],
        out_shape=[_sds((NSLAB, n16, H * HEAD), F32)] * 3 + [_sds((H, 1, HEAD), F32)],
        scratch_shapes=[pltpu.VMEM((3, BLK, 2 * BLK), F32), pltpu.VMEM((3, BLK, 2 * BLK), F32),
                        pltpu.VMEM((NSLAB, n16, 1), F32)],
        args=(table, bkt, qkv, qkv, qkv, out, dout, lse))


def _qknorm_bwd(raw, dq, dk, dv, gains, tm, name):
    T, N = raw.shape
    C = N // 3

    def body(raw_ref, dq_ref, dk_ref, dv_ref, gains_ref, dp_ref, dg_ref):
        @pl.when(pl.program_id(0) == 0)
        def _():
            dg_ref[...] = jnp.zeros_like(dg_ref)

        for t, d_ref in enumerate((dq_ref, dk_ref)):
            gain = gains_ref[t]
            dgain = jnp.zeros((1, HEAD), F32)
            for k in range(C // HEAD):
                y = raw_ref[:, t * C + k * HEAD:t * C + (k + 1) * HEAD]
                dn = d_ref[:, k * HEAD:(k + 1) * HEAD]
                r = lax.rsqrt(jnp.mean(y * y, axis=-1, keepdims=True) + RMS_EPS)
                yhat = y * r
                gd = dn * gain
                dy = r * (gd - yhat * jnp.mean(gd * yhat, axis=-1, keepdims=True))
                dp_ref[:, t * C + k * HEAD:t * C + (k + 1) * HEAD] = dy.astype(BF16)
                dgain = dgain + jnp.sum(dn * yhat, axis=0, keepdims=True)
            dg_ref[t] += dgain
        dp_ref[:, 2 * C:] = dv_ref[...].astype(BF16)

    third = pl.BlockSpec((tm, C), lambda i: (i, 0))
    return pl.pallas_call(
        body, name=name, grid=(T // tm,),
        in_specs=[pl.BlockSpec((tm, N), lambda i: (i, 0)), third, third, third,
                  pl.BlockSpec((2, 1, HEAD), lambda i: (0, 0, 0))],
        out_specs=[pl.BlockSpec((tm, N), lambda i: (i, 0)), pl.BlockSpec((2, 1, HEAD), lambda i: (0, 0, 0))],
        out_shape=[_sds((T, N), BF16), _sds((2, 1, HEAD), F32)], compiler_params=_cparams(),
    )(raw, dq, dk, dv, gains)


def _pad_lanes(v, width=HEAD):
    return jnp.pad(v, ((0, 0), (0, width - v.shape[1])))


def _local_step(x, target, small, wts, plan=None):
    grads = {}

    def hosted(host, fn, *args, **kw):
        side = plan.before(host, wts, grads) if plan is not None else None
        if side is None:
            return fn(*args, name=host, **kw)
        res, side_res = fn(*args, name=host, side=side, **kw)
        plan.after(host, side_res, wts, grads)
        return res

    T, D = x.shape
    C = D // 2
    H = C // HEAD
    n16 = T // NSLAB
    tm = min(512, T)
    tmm = min(1024, T)
    tms = 4 * n16
    tq = min(512, T)
    bn = min(1024, D)
    g1, gm, g2 = small["ffn1_norm"], small["mix_norm"], small["ffn2_norm"]
    gains_a = jnp.stack([small["q_norm_a"], small["k_norm_a"], jnp.ones_like(small["q_norm_a"])])
    gains_b = jnp.stack([small["q_norm_b"], small["k_norm_b"], jnp.ones_like(small["q_norm_b"])])
    fbias = _pad_lanes(small["forget_bias"])
    table = small["rel_bias_table"]

    h1, gu1, act1 = hosted("ffn1_up", _ffn_fwd, x, g1, wts["ffn1_in"], None, tmm)
    x1 = hosted("ffn1_down", _ffn_down, x, act1, wts["ffn1_out"], tmm)
    w_in_t, w_f_t, w_o = wts["w_in_t"], wts["w_f_t"], wts["w_o"]
    raw_a, nrm_a, h2a = _proj(x1, gm, w_in_t, gains_a, (True, True, False), tn=C, w_off=0, slabs=True, tm=tms,
                              normed_dtype=F32, name="proj_a")
    raw_b, nrm_b, h2b, f_raw = _proj(x1, gm, w_in_t, gains_b, (True, True, False), tn=C, w_off=3, slabs=False, tm=tmm,
                                     normed_dtype=BF16, name="proj_b", small_wt=w_f_t)
    c = _fox_gate_fwd(f_raw, fbias, "fox_gate_fwd")
    c_heads = c[:, :H].T
    c_col, c_row = c_heads[:, :, None], c_heads[:, None, :]
    out_a, lse_a = hosted("dil_fwd", _dil_fwd, nrm_a, table)
    out_b, lse_b = hosted("fox_fwd", _fox_fwd, nrm_b, c_col, c_row, tq)
    x2a = _mm(out_a, w_o, nt=False, tk=C, tm=tms, a_layout="slab", out_layout="view", resid=x1, name="out_a")
    x2 = _mm(out_b, w_o, nt=False, tk=C, tm=tmm, a_layout="nat", out_layout="nat", resid=x2a, w_off=1, name="out_b")
    dy, h3, gu3, act3, loss_row = _ffn_fwd(x2, g2, wts["ffn2_in"], wts["ffn2_out"], tm, "ffn2_fwd", target=target)

    def ffn_backward(tag, xin, g, h, gu, act, win, wout, dres):
        nc, tf = wout.shape[0], wout.shape[1]
        dh, dgu, dyb = hosted(tag + "_bwd", _ffn_bwd, dres, gu, win, wout, tm)
        grads[tag + "_w_in_t"], grads[tag + "_w_in_t_bf16"] = hosted(
            tag + "_dwin", _mm_tn, dgu.reshape(2 * nc, tf, T), h, bm=tf, bn=bn, bt=T, a_rows=True, twin=True)
        dxin, grads[tag + "_norm"] = hosted(tag + "_rms_bwd", _rms_bwd, xin, g, dh, dres, tm)
        grads[tag + "_w_out"], grads[tag + "_w_out_bf16"] = hosted(
            tag + "_dwout", _mm_tn, act, dyb, bm=tf, bn=bn, bt=T, a_rows=True, twin=True)
        return dxin

    dx2 = ffn_backward("ffn2", x2, g2, h3, gu3, act3, wts["ffn2_in"], wts["ffn2_out"], dy)

    dmix_a = _mm(dx2, w_o, nt=True, tk=D, tm=tms, a_layout="view", out_layout="slab", n_out=C, name="dmix_a")
    dmix_b = _mm(dx2, w_o, nt=True, tk=D, tm=tmm, a_layout="nat", out_layout="nat", n_out=C, w_off=1, name="dmix_b")
    dwo = _mm_tn(out_a.reshape(1, T, C), dx2.reshape(n16, NSLAB * D), bm=C, bn=bn, bt=n16, b_slabs=True,
                 rows=2 * C, name="dwo_a")
    dwo = _mm_tn(out_b.reshape(1, T, C), dx2, bm=C, bn=bn, bt=tm, rows=2 * C, m_off=1, into=dwo, name="dwo_b")
    grads["w_out"] = dwo[0]

    dqa, dka, dva, dtab = hosted("dil_bwd", _dil_bwd, nrm_a, table, out_a, dmix_a, lse_a)
    dqb, dkb, dvb, dck, dcq = hosted("fox_bwd", _fox_bwd, nrm_b, c_col, c_row, out_b, dmix_b, lse_b, tq)
    grads["rel_bias_table"] = dtab[:, 0, :NUM_BUCKETS].T
    dc = _pad_lanes((dck[:, 0, :] + dcq[:, :, 0]).T)
    df, dfb = _fox_gate_bwd(f_raw, fbias, dc, "fox_gate_bwd")
    grads["forget_bias"] = dfb[:, :H]

    flat = lambda a: a.reshape(T, a.shape[-1])
    dproj_a, dgain_a = _qknorm_bwd(flat(raw_a), flat(dqa), flat(dka), flat(dva), gains_a[:2], min(256, T), "qknorm_bwd_a")
    dproj_b, dgain_b = _qknorm_bwd(raw_b, dqb, dkb, dvb, gains_b[:2], min(256, T), "qknorm_bwd_b")
    grads["q_norm_a"], grads["k_norm_a"] = dgain_a[0], dgain_a[1]
    grads["q_norm_b"], grads["k_norm_b"] = dgain_b[0], dgain_b[1]
    dproj_a = dproj_a.reshape(NSLAB, n16, 3 * C)

    dh2 = _mm(dproj_a, w_in_t, nt=False, tk=C, tm=tms, a_layout="slab", out_layout="view", name="dh2_a")
    dh2 = _mm(dproj_b, w_in_t, nt=False, tk=C, tm=tmm, a_layout="nat", out_layout="nat", resid=dh2, w_off=3,
              small=(df, w_f_t), name="dh2_b")
    dx1, grads["mix_norm"] = _rms_bwd(x1, gm, dh2, dx2, tm, "mix_rms_bwd")
    bt = min(2048, T)
    dwt = _mm_tn(flat(dproj_a)[None], flat(h2a), bm=C, bn=bn, bt=bt, rows=6 * C + H, name="dw_a")
    dwt = _mm_tn(dproj_b[None], h2b, bm=C, bn=bn, bt=bt, rows=6 * C + H, m_off=3, into=dwt, name="dw_b")
    dwt = _mm_tn(df[None, :, :H], h2b, bm=H, bn=bn, bt=bt, rows=6 * C + H, m_off=6 * C // H, into=dwt, name="dw_f")
    grads["w_in_t"] = dwt[0]

    grad_x = ffn_backward("ffn1", x, g1, h1, gu1, act1, wts["ffn1_in"], wts["ffn1_out"], dx1)
    return loss_row, grad_x, grads


def _place():
    x, y, c = lax.axis_index("x"), lax.axis_index("y"), lax.axis_index("c")
    other_chips = [(1 - x, y), (x, 1 - y), (1 - x, 1 - y)]
    return x, y, c, other_chips


def _run_side(side, name):
    def body(*refs):
        si, so = len(side.ins), len(side.outs)
        side.start(refs[:si], refs[si:si + so], refs[si + so:])
        side.finish(refs[:si], refs[si:si + so], refs[si + so:])

    return pl.pallas_call(body, name=name, in_specs=[ANY] * len(side.ins), out_specs=[ANY] * len(side.outs),
                          out_shape=side.outs, scratch_shapes=side.sems)(*side.ins)


def _all_gather(shards):
    n = len(shards)

    def plan(ins, outs, sems):
        send_sems, recv_sems, local_sems = sems
        x, y, c, chips = _place()
        me, sibling = (x, y, c), (x, y, 1 - c)

        def copy(a, k, block, to, src=None):
            px, py, pc = block
            dst = outs[a].at[4 * px + 2 * py + pc]
            return pltpu.make_async_remote_copy(
                src_ref=dst if src is None else src, dst_ref=dst, send_sem=send_sems.at[7 * a + k],
                recv_sem=recv_sems.at[7 * a + k], device_id=to, device_id_type=MESH)

        mine = [pltpu.make_async_copy(ins[a], outs[a].at[4 * x + 2 * y + c], local_sems.at[a]) for a in range(n)]
        first = []
        for a in range(n):
            first.append(copy(a, 0, me, sibling, src=ins[a]))
            first += [copy(a, 1 + j, me, (*chip, c), src=ins[a]) for j, chip in enumerate(chips)]
        return copy, mine, first, me, sibling, c, chips

    def start(ins, outs, sems):
        _, mine, first, *_ = plan(ins, outs, sems)
        for cp in mine + first:
            cp.start()

    def finish(ins, outs, sems):
        copy, mine, first, me, sibling, c, chips = plan(ins, outs, sems)
        passed = []
        for a in range(n):
            for j, chip in enumerate(chips):
                copy(a, 1 + j, (*chip, c), me).wait_recv()
                fwd = copy(a, 4 + j, (*chip, c), sibling)
                fwd.start()
                passed.append(fwd)
        for a in range(n):
            copy(a, 0, sibling, me).wait_recv()
            for j, chip in enumerate(chips):
                copy(a, 4 + j, (*chip, 1 - c), me).wait_recv()
        for cp in first + passed:
            cp.wait_send()
        for cp in mine:
            cp.wait()

    return _Side(shards, [_sds((N_DEV,) + s.shape, s.dtype) for s in shards],
                 [pltpu.SemaphoreType.DMA((7 * n,)), pltpu.SemaphoreType.DMA((7 * n,)), pltpu.SemaphoreType.DMA((n,))],
                 start, finish)


def _all_gather_relayed(shards):
    n = len(shards)
    halves = [-(-(s.shape[0] // 2) // BF16_TILE_ROWS) * BF16_TILE_ROWS for s in shards]

    def body_parts(ins, outs, sems):
        send_sems, recv_sems, local_sems = sems
        x, y, c, _ = _place()
        me, sib, xn, yn, dg = (x, y, c), (x, y, 1 - c), (1 - x, y, c), (x, 1 - y, c), (1 - x, 1 - y, c)

        def rows(a, block, part):
            px, py, pc = block
            whole = outs[a].at[4 * px + 2 * py + pc]
            if part is None:
                return whole
            return whole.at[pl.ds(0, halves[a])] if part == 0 else whole.at[pl.ds(halves[a], shards[a].shape[0] - halves[a])]

        def copy(a, k, block, part, to, src=None):
            dst = rows(a, block, part)
            return pltpu.make_async_remote_copy(
                src_ref=dst if src is None else src, dst_ref=dst, send_sem=send_sems.at[9 * a + k],
                recv_sem=recv_sems.at[9 * a + k], device_id=to, device_id_type=MESH)

        flip = lambda dev: (dev[0], dev[1], 1 - dev[2])
        mine = [pltpu.make_async_copy(ins[a], rows(a, me, None), local_sems.at[a]) for a in range(n)]
        own = [[copy(a, 0, me, None, sib, src=ins[a]), copy(a, 1, me, None, xn, src=ins[a]),
                copy(a, 2, me, None, yn, src=ins[a])] for a in range(n)]
        relays = lambda a: [(1, [copy(a, 3, xn, 0, yn), copy(a, 5, xn, None, sib)]),
                            (2, [copy(a, 4, yn, 1, xn), copy(a, 6, yn, None, sib)]),
                            (3, [copy(a, 7, dg, 0, sib)]), (4, [copy(a, 8, dg, 1, sib)])]
        lands = {0: (sib, None), 1: (xn, None), 2: (yn, None), 3: (dg, 0), 4: (dg, 1), 5: (flip(xn), None),
                 6: (flip(yn), None), 7: (flip(dg), 0), 8: (flip(dg), 1)}
        arrival = lambda a, k: copy(a, k, lands[k][0], lands[k][1], me)
        return mine, own, relays, arrival

    def start(ins, outs, sems):
        mine, own, _, _ = body_parts(ins, outs, sems)
        for cp in mine + [cp for per in own for cp in per]:
            cp.start()

    def finish(ins, outs, sems):
        mine, own, relays, arrival = body_parts(ins, outs, sems)
        sent = [cp for per in own for cp in per]
        relays = [relays(a) for a in range(n)]
        for stage in range(4):
            for a in range(n):
                after, passes = relays[a][stage]
                arrival(a, after).wait_recv()
                for cp in passes:
                    cp.start()
                sent += passes
        for a in range(n):
            for k in (0, 5, 6, 7, 8):
                arrival(a, k).wait_recv()
        for cp in sent:
            cp.wait_send()
        for cp in mine:
            cp.wait()

    return _Side(shards, [_sds((N_DEV,) + s.shape, s.dtype) for s in shards],
                 [pltpu.SemaphoreType.DMA((9 * n,)), pltpu.SemaphoreType.DMA((9 * n,)), pltpu.SemaphoreType.DMA((n,))],
                 start, finish)


def _exchange_in_chip(gs):
    n = len(gs)

    def copies(ins, outs, sems):
        x, y, c, _ = _place()
        return [pltpu.make_async_remote_copy(
            src_ref=ins[a].at[2 * q + 1 - c], dst_ref=outs[a].at[q], send_sem=sems[0].at[4 * a + q],
            recv_sem=sems[1].at[4 * a + q], device_id=(x, y, 1 - c), device_id_type=MESH)
            for a in range(n) for q in range(4)]

    def start(ins, outs, sems):
        for cp in copies(ins, outs, sems):
            cp.start()

    def finish(ins, outs, sems):
        for cp in copies(ins, outs, sems):
            cp.wait()

    return _Side(gs, [_sds((4,) + g.shape[1:], g.dtype) for g in gs],
                 [pltpu.SemaphoreType.DMA((4 * n,)), pltpu.SemaphoreType.DMA((4 * n,))], start, finish)


def _exchange_between_chips(ps):
    n = len(ps)

    def copies(ins, outs, sems):
        x, y, c, chips = _place()
        return [pltpu.make_async_remote_copy(
            src_ref=ins[a].at[2 * cx + cy], dst_ref=outs[a].at[j], send_sem=sems[0].at[3 * a + j],
            recv_sem=sems[1].at[3 * a + j], device_id=(cx, cy, c), device_id_type=MESH)
            for a in range(n) for j, (cx, cy) in enumerate(chips)]

    def start(ins, outs, sems):
        for cp in copies(ins, outs, sems):
            cp.start()

    def finish(ins, outs, sems):
        for cp in copies(ins, outs, sems):
            cp.wait()

    return _Side(ps, [_sds((3,) + p.shape[1:], p.dtype) for p in ps],
                 [pltpu.SemaphoreType.DMA((3 * n,)), pltpu.SemaphoreType.DMA((3 * n,))], start, finish)


def _all_reduce_small(v, name):
    R = v.shape[0]

    def body(v_ref, sum_ref, all_ref, send_sems, recv_sems):
        x, y, c, _ = _place()
        k = 4 * x + 2 * y + c
        all_ref[k] = v_ref[...]
        copies = []
        for rel in range(1, N_DEV):
            fx, fy, fc = (rel >> 2) & 1, (rel >> 1) & 1, rel & 1
            peer = (1 - x if fx else x, 1 - y if fy else y, 1 - c if fc else c)
            copies.append(pltpu.make_async_remote_copy(
                src_ref=v_ref, dst_ref=all_ref.at[k], send_sem=send_sems.at[rel - 1], recv_sem=recv_sems.at[rel - 1],
                device_id=peer, device_id_type=MESH))
        for cp in copies:
            cp.start()
        for rel in range(1, N_DEV):
            fx, fy, fc = (rel >> 2) & 1, (rel >> 1) & 1, rel & 1
            src = 4 * (1 - x if fx else x) + 2 * (1 - y if fy else y) + (1 - c if fc else c)
            pltpu.make_async_remote_copy(
                src_ref=v_ref, dst_ref=all_ref.at[src], send_sem=send_sems.at[rel - 1], recv_sem=recv_sems.at[rel - 1],
                device_id=(x, y, c), device_id_type=MESH).wait_recv()
        for cp in copies:
            cp.wait_send()
        tot = all_ref[0]
        for d in range(1, N_DEV):
            tot = tot + all_ref[d]
        sum_ref[...] = tot

    vm = pl.BlockSpec(memory_space=pltpu.VMEM)
    return pl.pallas_call(
        body, name=name, in_specs=[vm], out_specs=[vm, vm],
        out_shape=[_sds((R, HEAD), F32), _sds((N_DEV, R, HEAD), F32)],
        scratch_shapes=[pltpu.SemaphoreType.DMA((N_DEV - 1,)), pltpu.SemaphoreType.DMA((N_DEV - 1,))],
    )(v)[0]


def _tiles(rows, cols):
    tc = 512 if cols % 512 == 0 else cols
    tr = rows
    while tr * tc * 4 > ELEMENTWISE_BLOCK_BYTES and tr % (2 * BF16_TILE_ROWS) == 0:
        tr //= 2
    return tr, tc


def _chip_sum(g, r1, place, name):
    _, R, Cc = g.shape
    tr, tc = _tiles(R, Cc)

    def body(place_ref, g_ref, r_ref, p_ref):
        p_ref[...] = (g_ref[...] + r_ref[...]).astype(BF16)

    chip = lambda k, place: k + (k >= place[1]).astype(jnp.int32)
    blk = lambda f: pl.BlockSpec((None, tr, tc), f)
    return pl.pallas_call(
        body, name=name,
        grid_spec=pltpu.PrefetchScalarGridSpec(
            num_scalar_prefetch=1, grid=(3, R // tr, Cc // tc),
            in_specs=[blk(lambda k, i, j, place: (2 * chip(k, place) + place[0], i, j)),
                      blk(lambda k, i, j, place: (chip(k, place), i, j))],
            out_specs=blk(lambda k, i, j, place: (chip(k, place), i, j))),
        out_shape=_sds((4, R, Cc), BF16), compiler_params=_cparams(),
    )(place, g, r1)


def _adamw_update(gv, w_ref, m_ref, v_ref, d_ref, nm_ref, nv_ref):
    nm = B1 * m_ref[...] + (1.0 - B1) * gv
    nv = B2 * v_ref[...] + (1.0 - B2) * jnp.square(gv)
    m_hat = nm / (1.0 - B1 ** STEP)
    v_hat = nv / (1.0 - B2 ** STEP)
    d_ref[...] = -LR * (m_hat / (jnp.sqrt(v_hat) + EPS) + WD * w_ref[...])
    nm_ref[...] = nm
    nv_ref[...] = nv


def _reduce_adamw(g, r1, r2, where, w, m, v, name):
    _, R, Cc = g.shape
    tr, tc = _tiles(R, Cc)

    def body(where_ref, g_ref, r1_ref, r2_ref, w_ref, m_ref, v_ref, o_ref, d_ref, nm_ref, nv_ref):
        gv = ((g_ref[...] + r1_ref[...]) + r2_ref[0].astype(F32)) + (r2_ref[1].astype(F32) + r2_ref[2].astype(F32))
        o_ref[...] = gv
        _adamw_update(gv, w_ref, m_ref, v_ref, d_ref, nm_ref, nv_ref)

    blk = pl.BlockSpec((tr, tc), lambda i, j, w: (i, j))
    return pl.pallas_call(
        body, name=name,
        grid_spec=pltpu.PrefetchScalarGridSpec(
            num_scalar_prefetch=1, grid=(R // tr, Cc // tc),
            in_specs=[pl.BlockSpec((None, tr, tc), lambda i, j, w: (w[0], i, j)),
                      pl.BlockSpec((None, tr, tc), lambda i, j, w: (w[1], i, j)),
                      pl.BlockSpec((3, tr, tc), lambda i, j, w: (0, i, j)), blk, blk, blk],
            out_specs=[blk] * 4),
        out_shape=[_sds((R, Cc), F32)] * 4, compiler_params=_cparams(),
    )(where, g, r1, r2, w, m, v)


def _adamw(w, g, m, v, name):
    R, Cc = w.shape
    tr, tc = _tiles(R, Cc)

    def body(w_ref, g_ref, m_ref, v_ref, d_ref, nm_ref, nv_ref):
        _adamw_update(g_ref[...], w_ref, m_ref, v_ref, d_ref, nm_ref, nv_ref)

    blk = pl.BlockSpec((tr, tc), lambda i, j: (i, j))
    return pl.pallas_call(
        body, name=name, grid=(R // tr, Cc // tc), in_specs=[blk] * 4, out_specs=[blk] * 3,
        out_shape=[_sds((R, Cc), F32)] * 3, compiler_params=_cparams(),
    )(w, g, m, v)


SMALL = ("ffn1_norm", "mix_norm", "ffn2_norm", "q_norm_a", "k_norm_a", "q_norm_b", "k_norm_b", "forget_bias",
         "rel_bias_table")
LARGE = ("ffn1_w_in", "ffn1_w_out", "w_in", "w_out", "ffn2_w_in", "ffn2_w_out")
ORDER = ("ffn1_norm", "ffn1_w_in", "ffn1_w_out", "mix_norm", "w_in", "q_norm_a", "k_norm_a", "q_norm_b", "k_norm_b",
         "forget_bias", "rel_bias_table", "w_out", "ffn2_norm", "ffn2_w_in", "ffn2_w_out")


def _pack_small(vals):
    rows = []
    for name in SMALL:
        flat = vals[name].reshape(-1)
        pad = (-flat.shape[0]) % HEAD
        rows.append(jnp.pad(flat, (0, pad)).reshape(-1, HEAD))
    return jnp.concatenate(rows, axis=0)


def _unpack_small(packed, like):
    out, r = {}, 0
    for name in SMALL:
        size = like[name].size
        nrow = -(-size // HEAD)
        out[name] = packed[r:r + nrow].reshape(-1)[:size].reshape(like[name].shape)
        r += nrow
    return out


def kernel(x, ffn1_norm, ffn1_w_in, ffn1_w_out, mix_norm, w_in, q_norm_a, k_norm_a, q_norm_b, k_norm_b, forget_bias, rel_bias_table, w_out, ffn2_norm, ffn2_w_in, ffn2_w_out, loss_target, m_ffn1_norm, m_ffn1_w_in, m_ffn1_w_out, m_mix_norm, m_w_in, m_q_norm_a, m_k_norm_a, m_q_norm_b, m_k_norm_b, m_forget_bias, m_rel_bias_table, m_w_out, m_ffn2_norm, m_ffn2_w_in, m_ffn2_w_out, v_ffn1_norm, v_ffn1_w_in, v_ffn1_w_out, v_mix_norm, v_w_in, v_q_norm_a, v_k_norm_a, v_q_norm_b, v_k_norm_b, v_forget_bias, v_rel_bias_table, v_w_out, v_ffn2_norm, v_ffn2_w_in, v_ffn2_w_out):
    w = dict(ffn1_norm=ffn1_norm, ffn1_w_in=ffn1_w_in, ffn1_w_out=ffn1_w_out, mix_norm=mix_norm, w_in=w_in,
             q_norm_a=q_norm_a, k_norm_a=k_norm_a, q_norm_b=q_norm_b, k_norm_b=k_norm_b, forget_bias=forget_bias,
             rel_bias_table=rel_bias_table, w_out=w_out, ffn2_norm=ffn2_norm, ffn2_w_in=ffn2_w_in, ffn2_w_out=ffn2_w_out)
    m = dict(ffn1_norm=m_ffn1_norm, ffn1_w_in=m_ffn1_w_in, ffn1_w_out=m_ffn1_w_out, mix_norm=m_mix_norm, w_in=m_w_in,
             q_norm_a=m_q_norm_a, k_norm_a=m_k_norm_a, q_norm_b=m_q_norm_b, k_norm_b=m_k_norm_b,
             forget_bias=m_forget_bias, rel_bias_table=m_rel_bias_table, w_out=m_w_out, ffn2_norm=m_ffn2_norm,
             ffn2_w_in=m_ffn2_w_in, ffn2_w_out=m_ffn2_w_out)
    v = dict(ffn1_norm=v_ffn1_norm, ffn1_w_in=v_ffn1_w_in, ffn1_w_out=v_ffn1_w_out, mix_norm=v_mix_norm, w_in=v_w_in,
             q_norm_a=v_q_norm_a, k_norm_a=v_k_norm_a, q_norm_b=v_q_norm_b, k_norm_b=v_k_norm_b,
             forget_bias=v_forget_bias, rel_bias_table=v_rel_bias_table, w_out=v_w_out, ffn2_norm=v_ffn2_norm,
             ffn2_w_in=v_ffn2_w_in, ffn2_w_out=v_ffn2_w_out)
    T, D = x.shape[1], x.shape[2]
    C = D // 2
    H = C // HEAD
    ff_shard = ffn1_w_out.shape[1]

    f1i, = _run_side(_all_gather_relayed([ffn1_w_in[0].T.astype(BF16)]), "gather_ffn1")
    wts = dict(ffn1_in=f1i.reshape(2, N_DEV, ff_shard, D))
    xi, yi, ci = lax.axis_index("x"), lax.axis_index("y"), lax.axis_index("c")
    place = jnp.stack([ci, 2 * xi + yi]).astype(jnp.int32)
    where = jnp.stack([4 * xi + 2 * yi + ci, 2 * xi + yi]).astype(jnp.int32)
    gs, r1, ps, r2 = {}, {}, {}, {}

    def shard(name):
        name, _, part = name.partition(":")
        s = (w[name][0].T if name.endswith("w_in") else w[name][0]).astype(BF16)
        first = -(-(s.shape[0] // 2) // BF16_TILE_ROWS) * BF16_TILE_ROWS
        return {"": s, "first": s[:first], "rest": s[first:]}[part]

    def by_destination(name, grads):
        key = name + "_t" if name.endswith("w_in") else name
        gs[name] = grads[key].reshape(N_DEV, -1, D)
        return grads.get(key + "_bf16", grads[key]).reshape(N_DEV, -1, D)

    def chip_sums(names):
        for name in names:
            ps[name] = _chip_sum(gs[name], r1[name], place, "chip_sum_" + name)
        return [ps[name] for name in names]

    class Plan:
        carried = {"ffn1_up": ("gather", ("ffn1_w_out", "w_in:first")), "ffn1_down": ("gather", ("w_in:rest", "w_out")),
                   "dil_fwd": ("gather", ("ffn2_w_out",)), "fox_fwd": ("gather", ("ffn2_w_in",)),
                   "dil_bwd": ("in_chip", ("ffn2_w_in", "ffn2_w_out")), "fox_bwd": ("between", ("ffn2_w_in", "ffn2_w_out")),
                   "ffn1_bwd": ("in_chip", ("w_in", "w_out")), "ffn1_dwin": ("between", ("w_in", "w_out")),
                   "ffn1_rms_bwd": ("in_chip", ("ffn1_w_in",)), "ffn1_dwout": ("between", ("ffn1_w_in",))}

        def before(self, host, wts, grads):
            if host not in self.carried:
                return None
            kind, names = self.carried[host]
            if kind == "gather":
                return _all_gather([shard(n) for n in names])
            if kind == "in_chip":
                return _exchange_in_chip([by_destination(n, grads) for n in names])
            return _exchange_between_chips(chip_sums(names))

        def after(self, host, res, wts, grads):
            kind, names = self.carried[host]
            if host == "ffn1_up":
                wts.update(ffn1_out=res[0], w_in_first_rows=res[1])
            elif host == "ffn1_down":
                w_in_t = jnp.concatenate([wts.pop("w_in_first_rows"), res[0]], axis=1).reshape(-1, D)
                wts.update(w_in_t=w_in_t, w_f_t=jnp.pad(w_in_t[6 * C:], ((0, HEAD - H), (0, 0))),
                           w_o=res[1].reshape(2 * C, D))
            elif host == "dil_fwd":
                wts.update(ffn2_out=res[0])
            elif host == "fox_fwd":
                wts.update(ffn2_in=res[0].reshape(2, N_DEV, ff_shard, D))
            else:
                (r1 if kind == "in_chip" else r2).update(zip(names, res))

    small = {name: w[name] for name in SMALL}
    loss_row, grad_x, grads = _local_step(x[0], loss_target[0], small, wts, Plan())

    tail = ("ffn1_w_out",)
    r1[tail[0]], = _run_side(_exchange_in_chip([by_destination(tail[0], grads)]), "reduce_in_chip_tail")
    r2.update(zip(tail, _run_side(_exchange_between_chips(chip_sums(tail)), "reduce_between_chips_tail")))

    packed = _pack_small(grads)
    nsmall = packed.shape[0]
    packed = jnp.concatenate([packed, loss_row, jnp.zeros(((-nsmall - 1) % 8, HEAD), F32)], axis=0)
    reduced = _all_reduce_small(packed, "reduce_small")
    loss = reduced[nsmall, 0]
    g_small = _unpack_small(reduced[:nsmall], small)

    grad, delta, new_m, new_v = dict(g_small), {}, {}, {}
    for name in LARGE:
        to = (lambda t: t[0].T) if name.endswith("w_in") else (lambda t: t[0])
        back = (lambda t: t.T[None]) if name.endswith("w_in") else (lambda t: t[None])
        res = _reduce_adamw(gs[name], r1[name], r2[name], where, to(w[name]), to(m[name]), to(v[name]), "adamw_" + name)
        grad[name], delta[name], new_m[name], new_v[name] = (back(t) for t in res)
    d, nm, nv = _adamw(_pack_small(w), reduced[:nsmall], _pack_small(m), _pack_small(v), "adamw_small")
    delta.update(_unpack_small(d, small))
    new_m.update(_unpack_small(nm, small))
    new_v.update(_unpack_small(nv, small))
    return (loss, grad_x[None], *[grad[n] for n in ORDER], *[delta[n] for n in ORDER],
            *[new_m[n] for n in ORDER], *[new_v[n] for n in ORDER])
```

```python
import functools
import math

import numpy as np
import jax
import jax.numpy as jnp
from jax import lax
from jax.experimental import pallas as pl
from jax.experimental.pallas import tpu as pltpu

F32, BF16 = jnp.float32, jnp.bfloat16
HEAD = 128
NSLAB = 16
BLK = 128
DILATIONS = (1, 4, 16)
NUM_BUCKETS, MAX_DISTANCE = 32, 2048
RMS_EPS = 1e-6
NEG = -1e30
SCALE = HEAD ** -0.5
LR, B1, B2, EPS, WD, STEP = 0.001, 0.9, 0.999, 1e-08, 0.01, 10
N_DEV = 8
VMEM_LIMIT_BYTES = 56 << 20
ELEMENTWISE_BLOCK_BYTES = 2 << 20
BF16_TILE_ROWS = 16
MESH = pl.DeviceIdType.MESH


def _cparams(**kw):
    return pltpu.CompilerParams(vmem_limit_bytes=VMEM_LIMIT_BYTES, **kw)


def _nn(a, b):
    return jnp.dot(a, b, preferred_element_type=F32)


def _nt(a, b):
    return lax.dot_general(a, b, (((1,), (1,)), ((), ())), preferred_element_type=F32)


def _tn(a, b):
    return lax.dot_general(a, b, (((0,), (0,)), ((), ())), preferred_element_type=F32)


def _sds(shape, dtype):
    return jax.ShapeDtypeStruct(shape, dtype)


ANY = pl.BlockSpec(memory_space=pl.ANY)


class _Side:
    def __init__(self, ins, outs, sems, start, finish):
        self.ins, self.outs, self.sems, self.start, self.finish = list(ins), list(outs), list(sems), start, finish


def _call(body, *, name, grid, in_specs, out_specs, out_shape, args, scratch_shapes=(), side=None):
    in_specs, out_specs, out_shape = list(in_specs), list(out_specs), list(out_shape)
    scratch_shapes = list(scratch_shapes)
    if side is None:
        return pl.pallas_call(body, name=name, grid=grid, in_specs=in_specs, out_specs=out_specs, out_shape=out_shape,
                              scratch_shapes=scratch_shapes, compiler_params=_cparams())(*args)
    ni, no, ns = len(args), len(out_shape), len(scratch_shapes)
    si, so = len(side.ins), len(side.outs)

    def fused(*refs):
        h_in, s_in = refs[:ni], refs[ni:ni + si]
        h_out, s_out = refs[ni + si:ni + si + no], refs[ni + si + no:ni + si + no + so]
        h_scr, s_sem = refs[ni + si + no + so:ni + si + no + so + ns], refs[ni + si + no + so + ns:]
        ids = [pl.program_id(k) for k in range(len(grid))]
        first = functools.reduce(jnp.logical_and, [i == 0 for i in ids])
        last = functools.reduce(jnp.logical_and, [i == n - 1 for i, n in zip(ids, grid)])

        @pl.when(first)
        def _():
            side.start(s_in, s_out, s_sem)

        body(*h_in, *h_out, *h_scr)

        @pl.when(last)
        def _():
            side.finish(s_in, s_out, s_sem)

    res = pl.pallas_call(
        fused, name=name, grid=grid, in_specs=in_specs + [ANY] * si, out_specs=out_specs + [ANY] * so,
        out_shape=out_shape + side.outs, scratch_shapes=scratch_shapes + side.sems, compiler_params=_cparams(),
    )(*args, *side.ins)
    return list(res[:no]), list(res[no:])


def _ffn_fwd(x, g, win, wout, tm, name, side=None, target=None):
    T, D = x.shape
    nc, tf = win.shape[1], win.shape[2]
    down, loss = wout is not None, target is not None
    assert down or not loss

    def body(x_ref, g_ref, win_ref, *refs):
        ins, outs = refs[:down + loss], refs[down + loss:]
        h_ref, gu_ref, act_ref = outs[down:down + 3]
        i, j = pl.program_id(0), pl.program_id(1)

        @pl.when(j == 0)
        def _():
            xv = x_ref[...]
            r = lax.rsqrt(jnp.mean(xv * xv, axis=-1, keepdims=True) + RMS_EPS)
            h_ref[...] = (xv * r * g_ref[...]).astype(BF16)
            if down:
                outs[0][...] = jnp.zeros_like(outs[0])

        if loss:
            @pl.when((i == 0) & (j == 0))
            def _():
                outs[-1][...] = jnp.zeros_like(outs[-1])

        hb = h_ref[...]
        gt = _nt(win_ref[0], hb)
        up = _nt(win_ref[1], hb)
        gu_ref[0] = gt.astype(BF16)
        gu_ref[1] = up.astype(BF16)
        act = (gt * jax.nn.sigmoid(gt) * up).astype(BF16)
        act_ref[...] = act
        if down:
            y_ref = outs[0]
            y_ref[...] += _tn(act, ins[0][...])

            @pl.when(j == nc - 1)
            def _():
                y = x_ref[...] + 0.5 * y_ref[...]
                if not loss:
                    y_ref[...] = y
                    return
                err = y - ins[1][...]
                y_ref[...] = err * (1.0 / D)
                tot = 0.5 * jnp.sum(jnp.mean(err * err, axis=-1, keepdims=True), axis=0, keepdims=True)
                lane = lax.broadcasted_iota(jnp.int32, (1, HEAD), 1)
                outs[-1][...] += jnp.where(lane == 0, tot, 0.0)

    row = pl.BlockSpec((tm, D), lambda i, j: (i, 0))
    return _call(
        body, name=name, grid=(T // tm, nc), side=side,
        in_specs=[row, pl.BlockSpec((1, D), lambda i, j: (0, 0)),
                  pl.BlockSpec((2, None, tf, D), lambda i, j: (0, j, 0, 0))]
        + ([pl.BlockSpec((None, tf, D), lambda i, j: (j, 0, 0))] if down else []) + ([row] if loss else []),
        out_specs=([row] if down else [])
        + [row, pl.BlockSpec((2, None, tf, tm), lambda i, j: (0, j, 0, i)),
           pl.BlockSpec((None, tf, tm), lambda i, j: (j, 0, i))]
        + ([pl.BlockSpec((1, HEAD), lambda i, j: (0, 0))] if loss else []),
        out_shape=([_sds((T, D), F32)] if down else [])
        + [_sds((T, D), BF16), _sds((2, nc, tf, T), BF16), _sds((nc, tf, T), BF16)]
        + ([_sds((1, HEAD), F32)] if loss else []),
        args=(x, g, win) + ((wout,) if down else ()) + ((target,) if loss else ()))


def _ffn_down(x, act, wout, tm, name, side=None):
    T, D = x.shape
    nc, tf = wout.shape[0], wout.shape[1]

    def body(x_ref, act_ref, wout_ref, y_ref):
        j = pl.program_id(1)

        @pl.when(j == 0)
        def _():
            y_ref[...] = jnp.zeros_like(y_ref)

        y_ref[...] += _tn(act_ref[...], wout_ref[...])

        @pl.when(j == nc - 1)
        def _():
            y_ref[...] = x_ref[...] + 0.5 * y_ref[...]

    row = pl.BlockSpec((tm, D), lambda i, j: (i, 0))
    res = _call(
        body, name=name, grid=(T // tm, nc), side=side,
        in_specs=[row, pl.BlockSpec((None, tf, tm), lambda i, j: (j, 0, i)),
                  pl.BlockSpec((None, tf, D), lambda i, j: (j, 0, 0))],
        out_specs=[row], out_shape=[_sds((T, D), F32)], args=(x, act, wout))
    return res[0] if side is None else (res[0][0], res[1])


def _ffn_bwd(dy, gu, win, wout, tm, name, side=None):
    T, D = dy.shape
    nc, tf = wout.shape[0], wout.shape[1]

    def body(dy_ref, gu_ref, win_ref, wout_ref, dh_ref, dgu_ref, dyb_ref):
        j = pl.program_id(1)

        @pl.when(j == 0)
        def _():
            dh_ref[...] = jnp.zeros_like(dh_ref)
            dyb_ref[...] = (0.5 * dy_ref[...]).astype(BF16)

        dact = _nt(wout_ref[...], dyb_ref[...])
        gt = gu_ref[0].astype(F32)
        up = gu_ref[1].astype(F32)
        s = jax.nn.sigmoid(gt)
        dgb = (dact * up * (s * (1.0 + gt * (1.0 - s)))).astype(BF16)
        dub = (dact * (gt * s)).astype(BF16)
        dgu_ref[0] = dgb
        dgu_ref[1] = dub
        dh_ref[...] += _tn(dgb, win_ref[0]) + _tn(dub, win_ref[1])

    return _call(
        body, name=name, grid=(T // tm, nc), side=side,
        in_specs=[pl.BlockSpec((tm, D), lambda i, j: (i, 0)),
                  pl.BlockSpec((2, None, tf, tm), lambda i, j: (0, j, 0, i)),
                  pl.BlockSpec((2, None, tf, D), lambda i, j: (0, j, 0, 0)),
                  pl.BlockSpec((None, tf, D), lambda i, j: (j, 0, 0))],
        out_specs=[pl.BlockSpec((tm, D), lambda i, j: (i, 0)),
                   pl.BlockSpec((2, None, tf, tm), lambda i, j: (0, j, 0, i)),
                   pl.BlockSpec((tm, D), lambda i, j: (i, 0))],
        out_shape=[_sds((T, D), F32), _sds((2, nc, tf, T), BF16), _sds((T, D), BF16)],
        args=(dy, gu, win, wout))


def _rms_bwd(x, g, dh, dres, tm, name, side=None):
    T, D = x.shape

    def body(x_ref, g_ref, dh_ref, dres_ref, dx_ref, dg_ref):
        @pl.when(pl.program_id(0) == 0)
        def _():
            dg_ref[...] = jnp.zeros_like(dg_ref)

        xv = x_ref[...]
        r = lax.rsqrt(jnp.mean(xv * xv, axis=-1, keepdims=True) + RMS_EPS)
        xhat = xv * r
        dh = dh_ref[...]
        gd = dh * g_ref[...]
        dx_ref[...] = dres_ref[...] + r * (gd - xhat * jnp.mean(gd * xhat, axis=-1, keepdims=True))
        dg_ref[...] += jnp.sum(dh * xhat, axis=0, keepdims=True)

    row = pl.BlockSpec((tm, D), lambda i: (i, 0))
    one = pl.BlockSpec((1, D), lambda i: (0, 0))
    return _call(body, name=name, grid=(T // tm,), side=side, in_specs=[row, one, row, row], out_specs=[row, one],
                 out_shape=[_sds((T, D), F32), _sds((1, D), F32)], args=(x, g, dh, dres))


def _mm_tn(a, b, *, bm, bn, bt, name, side=None, rows=None, m_off=0, into=None, a_rows=False, twin=False):
    nz, T, M = (a.shape[0], a.shape[2], a.shape[1]) if a_rows else a.shape
    N = b.shape[1]
    b_spec = pl.BlockSpec((bt, bn), lambda n, z, m, t: (t, n))
    assert M % bm == 0 and N % bn == 0 and T % bt == 0, (M, bm, N, bn, T, bt)

    def body(a_ref, b_ref, *rest):
        c_ref = rest[-2] if twin else rest[-1]

        @pl.when(pl.program_id(3) == 0)
        def _():
            c_ref[...] = jnp.zeros_like(c_ref)

        ab, bb = a_ref[...].astype(BF16), b_ref[...].astype(BF16)
        c_ref[...] += _nn(ab, bb) if a_rows else _tn(ab, bb)
        if twin:
            @pl.when(pl.program_id(3) == T // bt - 1)
            def _():
                rest[-1][...] = c_ref[...].astype(BF16)

    grid = (N // bn, nz, M // bm, T // bt)
    a_spec = (pl.BlockSpec((None, bm, bt), lambda n, z, m, t: (z, m, t)) if a_rows
              else pl.BlockSpec((None, bt, bm), lambda n, z, m, t: (z, t, m)))
    in_specs = [a_spec, b_spec]
    out_spec = pl.BlockSpec((None, bm, bn), lambda n, z, m, t: (z, m + m_off, n))
    out_shape = _sds((nz, M if rows is None else rows, N), F32)
    if into is not None:
        assert side is None and not twin and into.shape == out_shape.shape
        return pl.pallas_call(body, name=name, grid=grid, in_specs=in_specs + [ANY], out_specs=out_spec,
                              out_shape=out_shape, input_output_aliases={2: 0}, compiler_params=_cparams())(a, b, into)
    outs = [out_shape] + ([_sds(out_shape.shape, BF16)] if twin else [])
    res = _call(body, name=name, grid=grid, side=side, in_specs=in_specs, out_specs=[out_spec] * len(outs),
                out_shape=outs, args=(a, b))
    mine = res if side is None else res[0]
    mine = tuple(mine) if twin else mine[0]
    return mine if side is None else (mine, res[1])


def _tok_spec(layout, tm, n16, C, bc, colmap):
    if layout == "nat":
        return pl.BlockSpec((tm, bc), lambda i, k: (i, colmap(k)))
    assert tm % n16 == 0
    if layout == "slab":
        return pl.BlockSpec((tm // n16, n16, bc), lambda i, k: (i, 0, colmap(k)))
    assert bc == C
    return pl.BlockSpec((n16, (tm // n16) * C), lambda i, k: (0, i))


def _tok_load(ref, layout, sp):
    if layout == "nat":
        return ref[...]
    if layout == "slab":
        return ref[...].reshape(-1, ref.shape[-1])
    c = ref.shape[1] // sp
    return jnp.concatenate([ref[:, s * c:(s + 1) * c] for s in range(sp)], axis=0)


def _tok_store(ref, layout, sp, val, cols=None, accumulate=False):
    def put(idx, v):
        if accumulate:
            ref[idx] += v
        else:
            ref[idx] = v

    lanes = slice(None) if cols is None else slice(cols[0], cols[0] + cols[1])
    if layout == "nat":
        put((slice(None), lanes), val)
    elif layout == "slab":
        put((slice(None), slice(None), lanes), val.reshape(sp, ref.shape[1], val.shape[-1]))
    else:
        assert cols is None
        c, n = ref.shape[1] // sp, ref.shape[0]
        for s in range(sp):
            put((slice(None), slice(s * c, (s + 1) * c)), val[s * n:(s + 1) * n])


def _proj(x, g, wt, gains, modes, *, tn, w_off, slabs, tm, normed_dtype, name, small_wt=None):
    T, D = x.shape
    ntile = len(modes)
    N = ntile * tn
    n16 = T // NSLAB
    in_layout, out_layout = ("view", "slab") if slabs else ("nat", "nat")
    sp = tm // n16
    extra = int(small_wt is not None)
    assert not (extra and slabs)
    x_in = x.reshape(n16, NSLAB * D) if slabs else x
    x_spec = _tok_spec(in_layout, tm, n16, D, D, lambda n: 0)
    oshape = lambda c: (NSLAB, n16, c) if slabs else (T, c)
    ospec = lambda bc, cm: _tok_spec(out_layout, tm, n16, None, bc, cm)

    def body(x_ref, g_ref, w_ref, gains_ref, *refs):
        raw_ref, nrm_ref, h_ref = refs[-3 - extra:len(refs) - extra]
        n = pl.program_id(1)

        @pl.when(n == 0)
        def _():
            xv = _tok_load(x_ref, in_layout, sp)
            r = lax.rsqrt(jnp.mean(xv * xv, axis=-1, keepdims=True) + RMS_EPS)
            hb = (xv * r * g_ref[...]).astype(BF16)
            _tok_store(h_ref, out_layout, sp, hb)
            if extra:
                refs[-1][...] = _nt(hb, refs[0][...])

        y = _nt(_tok_load(h_ref, out_layout, sp), w_ref[...])
        _tok_store(raw_ref, out_layout, sp, y)
        for t, mode in enumerate(modes):
            @pl.when(n == t)
            def _(t=t, mode=mode):
                if not mode:
                    _tok_store(nrm_ref, out_layout, sp, y.astype(nrm_ref.dtype))
                    return
                gain = gains_ref[t]
                for k in range(tn // HEAD):
                    yk = y[:, k * HEAD:(k + 1) * HEAD]
                    r = lax.rsqrt(jnp.mean(yk * yk, axis=-1, keepdims=True) + RMS_EPS)
                    _tok_store(nrm_ref, out_layout, sp, (yk * r * gain).astype(nrm_ref.dtype), cols=(k * HEAD, HEAD))

    return pl.pallas_call(
        body, name=name, grid=(T // tm, ntile),
        in_specs=[x_spec, pl.BlockSpec((1, D), lambda i, n: (0, 0)),
                  pl.BlockSpec((tn, D), lambda i, n: (n + w_off, 0)),
                  pl.BlockSpec((ntile, 1, HEAD), lambda i, n: (0, 0, 0))]
        + ([pl.BlockSpec((HEAD, D), lambda i, n: (0, 0))] if extra else []),
        out_specs=[ospec(tn, lambda n: n), ospec(tn, lambda n: n), ospec(D, lambda n: 0)]
        + ([ospec(HEAD, lambda n: 0)] if extra else []),
        out_shape=[_sds(oshape(N), F32), _sds(oshape(N), normed_dtype), _sds(oshape(D), BF16)]
        + ([_sds((T, HEAD), F32)] if extra else []),
        compiler_params=_cparams(),
    )(x_in, g, wt, gains, *([small_wt] if extra else []))


def _mm(a, w, *, nt, tk, tm, a_layout, out_layout, resid=None, name, w_off=0, n_out=None, small=None):
    if a_layout == "slab":
        T, K = a.shape[0] * a.shape[1], a.shape[2]
    else:
        T, K = a.shape
    N = (w.shape[0] if nt else w.shape[1]) if n_out is None else n_out
    n16 = T // NSLAB
    nk = K // tk
    sp = tm // n16
    a_in = a.reshape(n16, NSLAB * K) if a_layout == "view" else a
    w_spec = (pl.BlockSpec((N, tk), lambda i, k: (w_off, k)) if nt
              else pl.BlockSpec((tk, N), lambda i, k: (k + w_off, 0)))
    o_spec = _tok_spec(out_layout, tm, n16, N, N, lambda k: 0)
    oshape = {"nat": (T, N), "slab": (NSLAB, n16, N), "view": (n16, NSLAB * N)}[out_layout]
    has_resid = resid is not None

    def body(*refs):
        a_ref, w_ref = refs[0], refs[1]
        o_ref = refs[-1]
        k = pl.program_id(1)

        @pl.when(k == 0)
        def _():
            o_ref[...] = refs[2][...] if has_resid else jnp.zeros_like(o_ref)
            if small is not None:
                o_ref[...] += _nn(refs[-3][...].astype(BF16), refs[-2][...])

        ab = _tok_load(a_ref, a_layout, sp).astype(BF16)
        _tok_store(o_ref, out_layout, sp, _nt(ab, w_ref[...]) if nt else _nn(ab, w_ref[...]), accumulate=True)

    ins = [a_in, w]
    in_specs = [_tok_spec(a_layout, tm, n16, K, tk, lambda k: k), w_spec]
    if has_resid:
        ins.append(resid.reshape(n16, NSLAB * N) if out_layout == "view" else resid)
        in_specs.append(o_spec)
    if small is not None:
        assert out_layout == "nat" and small[1].shape == (HEAD, N)
        ins += list(small)
        in_specs += [pl.BlockSpec((tm, HEAD), lambda i, k: (i, 0)), pl.BlockSpec((HEAD, N), lambda i, k: (0, 0))]
    out = pl.pallas_call(
        body, name=name, grid=(T // tm, nk), in_specs=in_specs, out_specs=o_spec,
        out_shape=_sds(oshape, F32), compiler_params=_cparams(),
    )(*ins)
    return out.reshape(T, N) if out_layout == "view" else out


def _log_sigmoid(z):
    return jnp.minimum(z, 0.0) - jnp.log(1.0 + jnp.exp(-jnp.abs(z)))


def _fox_gate_fwd(f_raw, fbias, name):
    T = f_raw.shape[0]
    cb = 256

    def body(f_ref, b_ref, c_ref):
        row = lax.broadcasted_iota(jnp.int32, (cb, cb), 0)
        col = lax.broadcasted_iota(jnp.int32, (cb, cb), 1)
        tri = (col <= row).astype(F32)
        carry = jnp.zeros((1, HEAD), F32)
        for i in range(T // cb):
            lf = _log_sigmoid(f_ref[i * cb:(i + 1) * cb, :] + b_ref[...])
            c = jnp.dot(tri, lf, preferred_element_type=F32, precision=lax.Precision.HIGHEST) + carry
            c_ref[i * cb:(i + 1) * cb, :] = c
            carry = c[cb - 1:cb, :]

    return pl.pallas_call(body, name=name, out_shape=_sds((T, HEAD), F32), compiler_params=_cparams())(f_raw, fbias)


def _fox_gate_bwd(f_raw, fbias, dc, name):
    T = f_raw.shape[0]
    cb = 256

    def body(f_ref, b_ref, dc_ref, df_ref, db_ref):
        row = lax.broadcasted_iota(jnp.int32, (cb, cb), 0)
        col = lax.broadcasted_iota(jnp.int32, (cb, cb), 1)
        tri = (col >= row).astype(F32)
        carry = jnp.zeros((1, HEAD), F32)
        dbias = jnp.zeros((1, HEAD), F32)
        for i in reversed(range(T // cb)):
            dlf = jnp.dot(tri, dc_ref[i * cb:(i + 1) * cb, :], preferred_element_type=F32,
                          precision=lax.Precision.HIGHEST) + carry
            carry = dlf[0:1, :]
            z = f_ref[i * cb:(i + 1) * cb, :] + b_ref[...]
            df = dlf * jax.nn.sigmoid(-z)
            df_ref[i * cb:(i + 1) * cb, :] = df
            dbias = dbias + jnp.sum(df, axis=0, keepdims=True)
        db_ref[...] = dbias

    return pl.pallas_call(body, name=name, out_shape=[_sds((T, HEAD), F32), _sds((1, HEAD), F32)],
                          compiler_params=_cparams())(f_raw, fbias, dc)


def _fox_fwd(qkv, c_col, c_row, tq, name, side=None):
    T = qkv.shape[0]
    H = qkv.shape[1] // (3 * HEAD)
    nq = T // tq
    c_blocks = c_row.reshape(H, nq, 1, tq)

    def body(q_ref, k_ref, v_ref, cq_ref, ck_ref, o_ref, lse_ref):
        qi = pl.program_id(1)
        q, cq = q_ref[...], cq_ref[...]
        causal = lax.broadcasted_iota(jnp.int32, (tq, tq), 1) <= lax.broadcasted_iota(jnp.int32, (tq, tq), 0)

        def key_block(ki, carry, diagonal):
            m, l, acc = carry
            rows = pl.ds(pl.multiple_of(ki * tq, tq), tq)
            s = _nt(q, k_ref[rows, :]) * SCALE + cq - ck_ref[ki]
            if diagonal:
                s = jnp.where(causal, s, NEG)
            m_new = jnp.maximum(m, jnp.max(s, axis=-1, keepdims=True))
            alpha = jnp.exp(m - m_new)
            p = jnp.exp(s - m_new)
            l = alpha * l + jnp.sum(p, axis=-1, keepdims=True)
            acc = alpha * acc + _nn(p.astype(BF16), v_ref[rows, :])
            return m_new, l, acc

        init = (jnp.full((tq, 1), NEG, F32), jnp.zeros((tq, 1), F32), jnp.zeros((tq, HEAD), F32))
        carry = lax.fori_loop(0, qi, lambda ki, c: key_block(ki, c, False), init)
        m, l, acc = key_block(qi, carry, True)
        o_ref[...] = acc / l
        lse_ref[...] = m + jnp.log(l)

    return _call(
        body, name=name, grid=(H, nq), side=side,
        in_specs=[pl.BlockSpec((tq, HEAD), lambda h, qi: (qi, h)),
                  pl.BlockSpec((T, HEAD), lambda h, qi: (0, H + h)),
                  pl.BlockSpec((T, HEAD), lambda h, qi: (0, 2 * H + h)),
                  pl.BlockSpec((None, tq, 1), lambda h, qi: (h, qi, 0)),
                  pl.BlockSpec((None, nq, 1, tq), lambda h, qi: (h, 0, 0, 0))],
        out_specs=[pl.BlockSpec((tq, HEAD), lambda h, qi: (qi, h)),
                   pl.BlockSpec((None, tq, 1), lambda h, qi: (h, qi, 0))],
        out_shape=[_sds((T, H * HEAD), F32), _sds((H, T, 1), F32)],
        args=(qkv, qkv, qkv, c_col, c_blocks))


def _fox_bwd(qkv, c_col, c_row, out, dout, lse, tq, name, side=None, do_off=0):
    T = qkv.shape[0]
    H = qkv.shape[1] // (3 * HEAD)
    nq = T // tq

    def body(q_ref, k_ref, v_ref, cq_ref, ck_ref, o_ref, do_ref, lse_ref, dq_ref, dk_ref, dv_ref, dck_ref, dcq_ref,
             delta_s):
        ki = pl.program_id(1)

        @pl.when(ki == 0)
        def _():
            dq_ref[...] = jnp.zeros_like(dq_ref)
            dcq_ref[...] = jnp.zeros_like(dcq_ref)
            delta_s[...] = jnp.sum(do_ref[...] * o_ref[...], axis=-1, keepdims=True)

        k, v, ck = k_ref[...], v_ref[...], ck_ref[...]
        causal = lax.broadcasted_iota(jnp.int32, (tq, tq), 1) <= lax.broadcasted_iota(jnp.int32, (tq, tq), 0)

        def query_block(qi, carry, diagonal):
            dk, dv, dck = carry
            rows = pl.ds(pl.multiple_of(qi * tq, tq), tq)
            q = q_ref[rows, :]
            s = _nt(q, k) * SCALE + cq_ref[rows, :] - ck
            if diagonal:
                s = jnp.where(causal, s, NEG)
            p = jnp.exp(s - lse_ref[rows, :])
            dob = do_ref[rows, :].astype(BF16)
            ds = p * (_nt(dob, v) - delta_s[rows, :])
            dsb = ds.astype(BF16)
            dq_ref[rows, :] += _nn(dsb, k) * SCALE
            dcq_ref[rows, :] += jnp.sum(ds, axis=-1, keepdims=True)
            return dk + _tn(dsb, q), dv + _tn(p.astype(BF16), dob), dck - jnp.sum(ds, axis=0, keepdims=True)

        init = (jnp.zeros((tq, HEAD), F32), jnp.zeros((tq, HEAD), F32), jnp.zeros((1, tq), F32))
        carry = query_block(ki, init, True)
        dk, dv, dck = lax.fori_loop(ki + 1, nq, lambda qi, c: query_block(qi, c, False), carry)
        dk_ref[...] = dk * SCALE
        dv_ref[...] = dv
        dck_ref[...] = dck

    head = lambda off: pl.BlockSpec((T, HEAD), lambda h, ki: (0, off + h))
    col = pl.BlockSpec((None, T, 1), lambda h, ki: (h, 0, 0))
    return _call(
        body, name=name, grid=(H, nq), side=side,
        in_specs=[head(0),
                  pl.BlockSpec((tq, HEAD), lambda h, ki: (ki, H + h)),
                  pl.BlockSpec((tq, HEAD), lambda h, ki: (ki, 2 * H + h)),
                  col, pl.BlockSpec((None, 1, tq), lambda h, ki: (h, 0, ki)), head(0), head(do_off), col],
        out_specs=[head(0),
                   pl.BlockSpec((tq, HEAD), lambda h, ki: (ki, h)),
                   pl.BlockSpec((tq, HEAD), lambda h, ki: (ki, h)),
                   pl.BlockSpec((None, 1, tq), lambda h, ki: (h, 0, ki)), col],
        out_shape=[_sds((T, H * HEAD), F32), _sds((T, H * HEAD), F32), _sds((T, H * HEAD), F32), _sds((H, 1, T), F32),
                   _sds((H, T, 1), F32)],
        scratch_shapes=[pltpu.VMEM((T, 1), F32)],
        args=(qkv, qkv, qkv, c_col, c_row, out, dout, lse))


def _t5_bucket(dist):
    max_exact = NUM_BUCKETS // 2
    d = dist.astype(np.float32)
    large = max_exact + (np.log(np.maximum(d, np.float32(1.0)) / np.float32(max_exact))
                         / np.float32(math.log(MAX_DISTANCE / max_exact))
                         * np.float32(NUM_BUCKETS - max_exact)).astype(np.int32)
    large = np.minimum(large, NUM_BUCKETS - 1)
    return np.where(dist < max_exact, dist, large)


def _bucket_maps():
    maps = []
    for d in DILATIONS:
        e = NSLAB // d
        rows = BLK // e
        idx = np.arange(BLK)
        pos = e * (idx % rows) + idx // rows
        qpos = pos[:, None] + BLK
        kpos = np.concatenate([pos, pos + BLK])[None, :]
        delta = qpos - kpos
        band = (delta >= 0) & (delta <= BLK)
        bucket = _t5_bucket(np.clip(delta, 0, None) * d)
        maps.append(np.where(band, bucket, -1).astype(np.int32))
    return np.stack(maps)


def _dil_geometry(T):
    n16 = T // NSLAB
    geo = []
    for d in DILATIONS:
        e = NSLAB // d
        rows = BLK // e
        nblk = n16 // rows
        geo.append((d, e, rows, nblk))
    return geo


DIL_INTERLEAVE_FWD = {1: 4, 4: 8, 16: 8}
DIL_INTERLEAVE_BWD = {1: 8, 4: 8, 16: 8}


def _dil_interleave(per_step, nblocks):
    while per_step > 1 and (nblocks % per_step or nblocks // per_step < 2):
        per_step -= 1
    return per_step


def _dil_bias(tab_ref, bkt_ref, bias_s, h):
    for p in range(len(DILATIONS)):
        bk = bkt_ref[p]
        bias = jnp.full((BLK, 2 * BLK), NEG, F32)
        for b in range(NUM_BUCKETS):
            bias = jnp.where(bk == b, tab_ref[b, h], bias)
        bias_s[p] = bias


def _dil_rows(d, e, rows, sub, blk):
    start = pl.multiple_of(blk * rows, rows)
    return [(sub + d * j, pl.ds(start, rows)) for j in range(e)]


def _gather(ref, idx):
    return jnp.concatenate([ref[s, r, :] for s, r in idx], axis=0)


def _scatter(ref, idx, val, rows):
    for j, (s, r) in enumerate(idx):
        ref[s, r, :] = val[j * rows:(j + 1) * rows]


def _scatter_add(ref, idx, val, rows):
    for j, (s, r) in enumerate(idx):
        ref[s, r, :] += val[j * rows:(j + 1) * rows]


def _dil_fwd(qkv, table, name, side=None):
    n16 = qkv.shape[1]
    T = NSLAB * n16
    H = qkv.shape[2] // (3 * HEAD)
    geo = _dil_geometry(T)
    bkt = jnp.asarray(_bucket_maps())

    def body(tab_ref, bkt_ref, q_ref, k_ref, v_ref, o_ref, lse_ref, onat_ref, bias_s, m_s, l_s):
        h = pl.program_id(0)
        _dil_bias(tab_ref, bkt_ref, bias_s, h)
        first_mask = lax.broadcasted_iota(jnp.int32, (BLK, 2 * BLK), 1) < BLK

        starts = len(DILATIONS) - 1

        def load(p, d, e, rows, sub, blk):
            cur = _dil_rows(d, e, rows, sub, blk)
            prev = _dil_rows(d, e, rows, sub, jnp.maximum(blk - 1, 0))
            q = _gather(q_ref, cur).astype(BF16)
            kk = jnp.concatenate([_gather(k_ref, prev), _gather(k_ref, cur)], axis=0).astype(BF16)
            vv = jnp.concatenate([_gather(v_ref, prev), _gather(v_ref, cur)], axis=0).astype(BF16)
            old = None if p == starts else (_gather(m_s, cur), _gather(l_s, cur), _gather(o_ref, cur))
            return cur, blk, q, kk, vv, old

        def compute(p, blk, q, kk, vv, old):
            s = _nt(q, kk) * SCALE + bias_s[p]
            s = jnp.where(first_mask & (blk == 0), NEG, s)
            m_blk = jnp.max(s, axis=-1, keepdims=True)
            if old is None:
                m_new = m_blk
                pr = jnp.exp(s - m_new)
                l_new = jnp.sum(pr, axis=-1, keepdims=True)
                acc = _nn(pr.astype(BF16), vv)
            else:
                m_old, l_old, acc_old = old
                m_new = jnp.maximum(m_old, m_blk)
                alpha = jnp.exp(m_old - m_new)
                pr = jnp.exp(s - m_new)
                l_new = alpha * l_old + jnp.sum(pr, axis=-1, keepdims=True)
                acc = alpha * acc_old + _nn(pr.astype(BF16), vv)
            if p == 0:
                return acc / l_new, m_new + jnp.log(l_new), None
            return acc, m_new, l_new

        def store(p, rows, cur, acc, m_new, l_new):
            _scatter(o_ref, cur, acc, rows)
            if p == 0:
                _scatter(lse_ref, cur, m_new, rows)
            else:
                _scatter(m_s, cur, m_new, rows)
                _scatter(l_s, cur, l_new, rows)

        for p in reversed(range(len(DILATIONS))):
            d, e, rows, nblk = geo[p]
            per_step = _dil_interleave(DIL_INTERLEAVE_FWD[d], d * nblk)

            def step(i, carry, p=p, d=d, e=e, rows=rows, nblk=nblk, per_step=per_step):
                ids = [i + u * (d * nblk // per_step) for u in range(per_step)]
                loaded = [load(p, d, e, rows, j // nblk, j % nblk) for j in ids]
                done = [(cur, compute(p, blk, q, kk, vv, old)) for cur, blk, q, kk, vv, old in loaded]
                for cur, res in done:
                    store(p, rows, cur, *res)
                return carry

            lax.fori_loop(0, d * nblk // per_step, step, 0)

        for r in range(NSLAB):
            onat_ref[pl.ds(r, n16, stride=NSLAB), :] = o_ref[r]

    head = lambda off: pl.BlockSpec((NSLAB, n16, HEAD), lambda h: (0, 0, off + h))
    return _call(
        body, name=name, grid=(H,), side=side,
        in_specs=[pl.BlockSpec(memory_space=pltpu.SMEM), pl.BlockSpec((3, BLK, 2 * BLK), lambda h: (0, 0, 0)),
                  head(0), head(H), head(2 * H)],
        out_specs=[head(0), pl.BlockSpec((None, NSLAB, n16, 1), lambda h: (h, 0, 0, 0)),
                   pl.BlockSpec((T, HEAD), lambda h: (0, h))],
        out_shape=[_sds((NSLAB, n16, H * HEAD), F32), _sds((H, NSLAB, n16, 1), F32), _sds((T, H * HEAD), F32)],
        scratch_shapes=[pltpu.VMEM((3, BLK, 2 * BLK), F32), pltpu.VMEM((NSLAB, n16, 1), F32),
                        pltpu.VMEM((NSLAB, n16, 1), F32)],
        args=(table, bkt, qkv, qkv, qkv))


def _dil_bwd(qkv, table, out, dout, lse, name, side=None):
    n16 = qkv.shape[1]
    T = NSLAB * n16
    H = qkv.shape[2] // (3 * HEAD)
    geo = _dil_geometry(T)
    bkt = jnp.asarray(_bucket_maps())

    def body(tab_ref, bkt_ref, q_ref, k_ref, v_ref, o_ref, dnat_ref, lse_ref,
             dq_ref, dk_ref, dv_ref, dtab_ref, bias_s, dbias_s, delta_s, do_ref):
        h = pl.program_id(0)
        _dil_bias(tab_ref, bkt_ref, bias_s, h)
        first_mask = lax.broadcasted_iota(jnp.int32, (BLK, 2 * BLK), 1) < BLK
        dbias_s[...] = jnp.zeros_like(dbias_s)
        dq_ref[...] = jnp.zeros_like(dq_ref)
        dk_ref[...] = jnp.zeros_like(dk_ref)
        dv_ref[...] = jnp.zeros_like(dv_ref)
        for r in range(NSLAB):
            do_ref[r] = dnat_ref[pl.ds(r, n16, stride=NSLAB), :]
            delta_s[r] = jnp.sum(do_ref[r] * o_ref[r], axis=-1, keepdims=True)

        def load(d, e, rows, sub, blk):
            cur = _dil_rows(d, e, rows, sub, blk)
            prev = _dil_rows(d, e, rows, sub, jnp.maximum(blk - 1, 0))
            q = _gather(q_ref, cur).astype(BF16)
            kk = jnp.concatenate([_gather(k_ref, prev), _gather(k_ref, cur)], axis=0).astype(BF16)
            vv = jnp.concatenate([_gather(v_ref, prev), _gather(v_ref, cur)], axis=0).astype(BF16)
            dob = _gather(do_ref, cur).astype(BF16)
            return cur, prev, blk, q, kk, vv, dob, _gather(lse_ref, cur), _gather(delta_s, cur)

        def compute(p, blk, q, kk, vv, dob, lse, delta):
            s = _nt(q, kk) * SCALE + bias_s[p]
            s = jnp.where(first_mask & (blk == 0), NEG, s)
            pr = jnp.exp(s - lse)
            ds = pr * (_nt(dob, vv) - delta)
            dsb = ds.astype(BF16)
            return ds, _nn(dsb, kk) * SCALE, _tn(dsb, q) * SCALE, _tn(pr.astype(BF16), dob)

        def store(rows, cur, prev, dq, dkk, dvv):
            _scatter_add(dq_ref, cur, dq, rows)
            _scatter_add(dk_ref, prev, dkk[:BLK], rows)
            _scatter_add(dk_ref, cur, dkk[BLK:], rows)
            _scatter_add(dv_ref, prev, dvv[:BLK], rows)
            _scatter_add(dv_ref, cur, dvv[BLK:], rows)

        for p in range(len(DILATIONS)):
            d, e, rows, nblk = geo[p]
            per_step = _dil_interleave(DIL_INTERLEAVE_BWD[d], d * nblk)

            def step(i, carry, p=p, d=d, e=e, rows=rows, nblk=nblk, per_step=per_step):
                ids = [i + u * (d * nblk // per_step) for u in range(per_step)]
                loaded = [load(d, e, rows, j // nblk, j % nblk) for j in ids]
                done = [(cur, prev, compute(p, *rest)) for cur, prev, *rest in loaded]
                dbias_s[p] += functools.reduce(jnp.add, [res[0] for _, _, res in done])
                for cur, prev, res in done:
                    store(rows, cur, prev, *res[1:])
                return carry

            lax.fori_loop(0, d * nblk // per_step, step, 0)

        lane = lax.broadcasted_iota(jnp.int32, (1, HEAD), 1)
        row = jnp.zeros((1, HEAD), F32)
        for b in range(NUM_BUCKETS):
            tot = jnp.zeros((1, 1), F32)
            for p in range(len(DILATIONS)):
                hit = jnp.where(bkt_ref[p] == b, dbias_s[p], 0.0)
                tot = tot + jnp.sum(jnp.sum(hit, axis=0, keepdims=True), axis=1, keepdims=True)
            row = jnp.where(lane == b, tot, row)
        dtab_ref[...] = row

    head = lambda off: pl.BlockSpec((NSLAB, n16, HEAD), lambda h: (0, 0, off + h))
    return _call(
        body, name=name, grid=(H,), side=side,
        in_specs=[pl.BlockSpec(memory_space=pltpu.SMEM), pl.BlockSpec((3, BLK, 2 * BLK), lambda h: (0, 0, 0)),
                  head(0), head(H), head(2 * H), head(0), pl.BlockSpec((T, HEAD), lambda h: (0, h)),
                  pl.BlockSpec((None, NSLAB, n16, 1), lambda h: (h, 0, 0, 0))],
        out_specs=[head(0), head(0), head(0), pl.BlockSpec((None, 1, HEAD), lambda h: (h, 0, 0))],
        out_shape=[_sds((NSLAB, n16, H * HEAD), F32)] * 3 + [_sds((H, 1, HEAD), F32)],
        scratch_shapes=[pltpu.VMEM((3, BLK, 2 * BLK), F32), pltpu.VMEM((3, BLK, 2 * BLK), F32),
                        pltpu.VMEM((NSLAB, n16, 1), F32), pltpu.VMEM((NSLAB, n16, HEAD), F32)],
        args=(table, bkt, qkv, qkv, qkv, out, dout, lse))


def _qknorm_bwd(raw, dq, dk, dv, gains, tm, name):
    T, N = raw.shape
    C = N // 3

    def body(raw_ref, dq_ref, dk_ref, dv_ref, gains_ref, dp_ref, dg_ref):
        @pl.when(pl.program_id(0) == 0)
        def _():
            dg_ref[...] = jnp.zeros_like(dg_ref)

        for t, d_ref in enumerate((dq_ref, dk_ref)):
            gain = gains_ref[t]
            dgain = jnp.zeros((1, HEAD), F32)
            for k in range(C // HEAD):
                y = raw_ref[:, t * C + k * HEAD:t * C + (k + 1) * HEAD]
                dn = d_ref[:, k * HEAD:(k + 1) * HEAD]
                r = lax.rsqrt(jnp.mean(y * y, axis=-1, keepdims=True) + RMS_EPS)
                yhat = y * r
                gd = dn * gain
                dy = r * (gd - yhat * jnp.mean(gd * yhat, axis=-1, keepdims=True))
                dp_ref[:, t * C + k * HEAD:t * C + (k + 1) * HEAD] = dy.astype(BF16)
                dgain = dgain + jnp.sum(dn * yhat, axis=0, keepdims=True)
            dg_ref[t] += dgain
        dp_ref[:, 2 * C:] = dv_ref[...].astype(BF16)

    third = pl.BlockSpec((tm, C), lambda i: (i, 0))
    return pl.pallas_call(
        body, name=name, grid=(T // tm,),
        in_specs=[pl.BlockSpec((tm, N), lambda i: (i, 0)), third, third, third,
                  pl.BlockSpec((2, 1, HEAD), lambda i: (0, 0, 0))],
        out_specs=[pl.BlockSpec((tm, N), lambda i: (i, 0)), pl.BlockSpec((2, 1, HEAD), lambda i: (0, 0, 0))],
        out_shape=[_sds((T, N), BF16), _sds((2, 1, HEAD), F32)], compiler_params=_cparams(),
    )(raw, dq, dk, dv, gains)


def _pad_lanes(v, width=HEAD):
    return jnp.pad(v, ((0, 0), (0, width - v.shape[1])))


def _local_step(x, target, small, wts, plan=None):
    grads = {}

    def hosted(host, fn, *args, **kw):
        side = plan.before(host, wts, grads) if plan is not None else None
        if side is None:
            return fn(*args, name=host, **kw)
        res, side_res = fn(*args, name=host, side=side, **kw)
        plan.after(host, side_res, wts, grads)
        return res

    T, D = x.shape
    C = D // 2
    H = C // HEAD
    n16 = T // NSLAB
    tm = min(512, T)
    tmm = min(1024, T)
    tms = 4 * n16
    tq = min(512, T)
    bn = min(1024, D)
    g1, gm, g2 = small["ffn1_norm"], small["mix_norm"], small["ffn2_norm"]
    gains_a = jnp.stack([small["q_norm_a"], small["k_norm_a"], jnp.ones_like(small["q_norm_a"])])
    gains_b = jnp.stack([small["q_norm_b"], small["k_norm_b"], jnp.ones_like(small["q_norm_b"])])
    fbias = _pad_lanes(small["forget_bias"])
    table = small["rel_bias_table"]

    h1, gu1, act1 = hosted("ffn1_up", _ffn_fwd, x, g1, wts["ffn1_in"], None, tmm)
    x1 = hosted("ffn1_down", _ffn_down, x, act1, wts["ffn1_out"], tmm)
    w_in_t, w_f_t, w_o = wts["w_in_t"], wts["w_f_t"], wts["w_o"]
    raw_a, nrm_a, h2a = _proj(x1, gm, w_in_t, gains_a, (True, True, False), tn=C, w_off=0, slabs=True, tm=tms,
                              normed_dtype=F32, name="proj_a")
    raw_b, nrm_b, h2b, f_raw = _proj(x1, gm, w_in_t, gains_b, (True, True, False), tn=C, w_off=3, slabs=False, tm=tmm,
                                     normed_dtype=BF16, name="proj_b", small_wt=w_f_t)
    c = _fox_gate_fwd(f_raw, fbias, "fox_gate_fwd")
    c_heads = c[:, :H].T
    c_col, c_row = c_heads[:, :, None], c_heads[:, None, :]
    out_a, lse_a, out_a_nat = hosted("dil_fwd", _dil_fwd, nrm_a, table)
    out_b, lse_b = hosted("fox_fwd", _fox_fwd, nrm_b, c_col, c_row, tq)
    x2a = _mm(out_a_nat, w_o, nt=False, tk=C, tm=tmm, a_layout="nat", out_layout="nat", resid=x1, name="out_a")
    x2 = _mm(out_b, w_o, nt=False, tk=C, tm=tmm, a_layout="nat", out_layout="nat", resid=x2a, w_off=1, name="out_b")
    dy, h3, gu3, act3, loss_row = _ffn_fwd(x2, g2, wts["ffn2_in"], wts["ffn2_out"], tm, "ffn2_fwd", target=target)

    def ffn_backward(tag, xin, g, h, gu, act, win, wout, dres):
        nc, tf = wout.shape[0], wout.shape[1]
        dh, dgu, dyb = hosted(tag + "_bwd", _ffn_bwd, dres, gu, win, wout, tm)
        grads[tag + "_w_in_t"], grads[tag + "_w_in_t_bf16"] = hosted(
            tag + "_dwin", _mm_tn, dgu.reshape(2 * nc, tf, T), h, bm=tf, bn=bn, bt=T, a_rows=True, twin=True)
        dxin, grads[tag + "_norm"] = hosted(tag + "_rms_bwd", _rms_bwd, xin, g, dh, dres, tm)
        grads[tag + "_w_out"], grads[tag + "_w_out_bf16"] = hosted(
            tag + "_dwout", _mm_tn, act, dyb, bm=tf, bn=bn, bt=T, a_rows=True, twin=True)
        return dxin

    dx2 = ffn_backward("ffn2", x2, g2, h3, gu3, act3, wts["ffn2_in"], wts["ffn2_out"], dy)

    dmix = _mm(dx2, w_o, nt=True, tk=D, tm=tmm, a_layout="nat", out_layout="nat", name="dmix")
    dwo = _mm_tn(out_a_nat.reshape(1, T, C), dx2, bm=C, bn=bn, bt=tm, rows=2 * C, name="dwo_a")
    dwo = _mm_tn(out_b.reshape(1, T, C), dx2, bm=C, bn=bn, bt=tm, rows=2 * C, m_off=1, into=dwo, name="dwo_b")
    grads["w_out"] = dwo[0]

    dqa, dka, dva, dtab = hosted("dil_bwd", _dil_bwd, nrm_a, table, out_a, dmix, lse_a)
    dqb, dkb, dvb, dck, dcq = hosted("fox_bwd", _fox_bwd, nrm_b, c_col, c_row, out_b, dmix, lse_b, tq, do_off=H)
    grads["rel_bias_table"] = dtab[:, 0, :NUM_BUCKETS].T
    dc = _pad_lanes((dck[:, 0, :] + dcq[:, :, 0]).T)
    df, dfb = _fox_gate_bwd(f_raw, fbias, dc, "fox_gate_bwd")
    grads["forget_bias"] = dfb[:, :H]

    flat = lambda a: a.reshape(T, a.shape[-1])
    dproj_a, dgain_a = _qknorm_bwd(flat(raw_a), flat(dqa), flat(dka), flat(dva), gains_a[:2], min(256, T), "qknorm_bwd_a")
    dproj_b, dgain_b = _qknorm_bwd(raw_b, dqb, dkb, dvb, gains_b[:2], min(256, T), "qknorm_bwd_b")
    grads["q_norm_a"], grads["k_norm_a"] = dgain_a[0], dgain_a[1]
    grads["q_norm_b"], grads["k_norm_b"] = dgain_b[0], dgain_b[1]
    dproj_a = dproj_a.reshape(NSLAB, n16, 3 * C)

    dh2 = _mm(dproj_a, w_in_t, nt=False, tk=C, tm=tms, a_layout="slab", out_layout="view", name="dh2_a")
    dh2 = _mm(dproj_b, w_in_t, nt=False, tk=C, tm=tmm, a_layout="nat", out_layout="nat", resid=dh2, w_off=3,
              small=(df, w_f_t), name="dh2_b")
    dx1, grads["mix_norm"] = _rms_bwd(x1, gm, dh2, dx2, tm, "mix_rms_bwd")
    bt = min(2048, T)
    dwt = _mm_tn(flat(dproj_a)[None], flat(h2a), bm=C, bn=bn, bt=bt, rows=6 * C + H, name="dw_a")
    dwt = _mm_tn(dproj_b[None], h2b, bm=C, bn=bn, bt=bt, rows=6 * C + H, m_off=3, into=dwt, name="dw_b")
    dwt = _mm_tn(df[None, :, :H], h2b, bm=H, bn=bn, bt=bt, rows=6 * C + H, m_off=6 * C // H, into=dwt, name="dw_f")
    grads["w_in_t"] = dwt[0]

    grad_x = ffn_backward("ffn1", x, g1, h1, gu1, act1, wts["ffn1_in"], wts["ffn1_out"], dx1)
    return loss_row, grad_x, grads


def _place():
    x, y, c = lax.axis_index("x"), lax.axis_index("y"), lax.axis_index("c")
    other_chips = [(1 - x, y), (x, 1 - y), (1 - x, 1 - y)]
    return x, y, c, other_chips


def _run_side(side, name):
    def body(*refs):
        si, so = len(side.ins), len(side.outs)
        side.start(refs[:si], refs[si:si + so], refs[si + so:])
        side.finish(refs[:si], refs[si:si + so], refs[si + so:])

    return pl.pallas_call(body, name=name, in_specs=[ANY] * len(side.ins), out_specs=[ANY] * len(side.outs),
                          out_shape=side.outs, scratch_shapes=side.sems)(*side.ins)


def _all_gather(shards):
    n = len(shards)

    def plan(ins, outs, sems):
        send_sems, recv_sems, local_sems = sems
        x, y, c, chips = _place()
        me, sibling = (x, y, c), (x, y, 1 - c)

        def copy(a, k, block, to, src=None):
            px, py, pc = block
            dst = outs[a].at[4 * px + 2 * py + pc]
            return pltpu.make_async_remote_copy(
                src_ref=dst if src is None else src, dst_ref=dst, send_sem=send_sems.at[7 * a + k],
                recv_sem=recv_sems.at[7 * a + k], device_id=to, device_id_type=MESH)

        mine = [pltpu.make_async_copy(ins[a], outs[a].at[4 * x + 2 * y + c], local_sems.at[a]) for a in range(n)]
        first = []
        for a in range(n):
            first.append(copy(a, 0, me, sibling, src=ins[a]))
            first += [copy(a, 1 + j, me, (*chip, c), src=ins[a]) for j, chip in enumerate(chips)]
        return copy, mine, first, me, sibling, c, chips

    def start(ins, outs, sems):
        _, mine, first, *_ = plan(ins, outs, sems)
        for cp in mine + first:
            cp.start()

    def finish(ins, outs, sems):
        copy, mine, first, me, sibling, c, chips = plan(ins, outs, sems)
        passed = []
        for a in range(n):
            for j, chip in enumerate(chips):
                copy(a, 1 + j, (*chip, c), me).wait_recv()
                fwd = copy(a, 4 + j, (*chip, c), sibling)
                fwd.start()
                passed.append(fwd)
        for a in range(n):
            copy(a, 0, sibling, me).wait_recv()
            for j, chip in enumerate(chips):
                copy(a, 4 + j, (*chip, 1 - c), me).wait_recv()
        for cp in first + passed:
            cp.wait_send()
        for cp in mine:
            cp.wait()

    return _Side(shards, [_sds((N_DEV,) + s.shape, s.dtype) for s in shards],
                 [pltpu.SemaphoreType.DMA((7 * n,)), pltpu.SemaphoreType.DMA((7 * n,)), pltpu.SemaphoreType.DMA((n,))],
                 start, finish)


def _all_gather_relayed(shards):
    n = len(shards)
    halves = [-(-(s.shape[0] // 2) // BF16_TILE_ROWS) * BF16_TILE_ROWS for s in shards]

    def body_parts(ins, outs, sems):
        send_sems, recv_sems, local_sems = sems
        x, y, c, _ = _place()
        me, sib, xn, yn, dg = (x, y, c), (x, y, 1 - c), (1 - x, y, c), (x, 1 - y, c), (1 - x, 1 - y, c)

        def rows(a, block, part):
            px, py, pc = block
            whole = outs[a].at[4 * px + 2 * py + pc]
            if part is None:
                return whole
            return whole.at[pl.ds(0, halves[a])] if part == 0 else whole.at[pl.ds(halves[a], shards[a].shape[0] - halves[a])]

        def copy(a, k, block, part, to, src=None):
            dst = rows(a, block, part)
            return pltpu.make_async_remote_copy(
                src_ref=dst if src is None else src, dst_ref=dst, send_sem=send_sems.at[9 * a + k],
                recv_sem=recv_sems.at[9 * a + k], device_id=to, device_id_type=MESH)

        flip = lambda dev: (dev[0], dev[1], 1 - dev[2])
        mine = [pltpu.make_async_copy(ins[a], rows(a, me, None), local_sems.at[a]) for a in range(n)]
        own = [[copy(a, 0, me, None, sib, src=ins[a]), copy(a, 1, me, None, xn, src=ins[a]),
                copy(a, 2, me, None, yn, src=ins[a])] for a in range(n)]
        relays = lambda a: [(1, [copy(a, 3, xn, 0, yn), copy(a, 5, xn, None, sib)]),
                            (2, [copy(a, 4, yn, 1, xn), copy(a, 6, yn, None, sib)]),
                            (3, [copy(a, 7, dg, 0, sib)]), (4, [copy(a, 8, dg, 1, sib)])]
        lands = {0: (sib, None), 1: (xn, None), 2: (yn, None), 3: (dg, 0), 4: (dg, 1), 5: (flip(xn), None),
                 6: (flip(yn), None), 7: (flip(dg), 0), 8: (flip(dg), 1)}
        arrival = lambda a, k: copy(a, k, lands[k][0], lands[k][1], me)
        return mine, own, relays, arrival

    def start(ins, outs, sems):
        mine, own, _, _ = body_parts(ins, outs, sems)
        for cp in mine + [cp for per in own for cp in per]:
            cp.start()

    def finish(ins, outs, sems):
        mine, own, relays, arrival = body_parts(ins, outs, sems)
        sent = [cp for per in own for cp in per]
        relays = [relays(a) for a in range(n)]
        for stage in range(4):
            for a in range(n):
                after, passes = relays[a][stage]
                arrival(a, after).wait_recv()
                for cp in passes:
                    cp.start()
                sent += passes
        for a in range(n):
            for k in (0, 5, 6, 7, 8):
                arrival(a, k).wait_recv()
        for cp in sent:
            cp.wait_send()
        for cp in mine:
            cp.wait()

    return _Side(shards, [_sds((N_DEV,) + s.shape, s.dtype) for s in shards],
                 [pltpu.SemaphoreType.DMA((9 * n,)), pltpu.SemaphoreType.DMA((9 * n,)), pltpu.SemaphoreType.DMA((n,))],
                 start, finish)


def _exchange_in_chip(gs):
    n = len(gs)

    def copies(ins, outs, sems):
        x, y, c, _ = _place()
        return [pltpu.make_async_remote_copy(
            src_ref=ins[a].at[2 * q + 1 - c], dst_ref=outs[a].at[q], send_sem=sems[0].at[4 * a + q],
            recv_sem=sems[1].at[4 * a + q], device_id=(x, y, 1 - c), device_id_type=MESH)
            for a in range(n) for q in range(4)]

    def start(ins, outs, sems):
        for cp in copies(ins, outs, sems):
            cp.start()

    def finish(ins, outs, sems):
        for cp in copies(ins, outs, sems):
            cp.wait()

    return _Side(gs, [_sds((4,) + g.shape[1:], g.dtype) for g in gs],
                 [pltpu.SemaphoreType.DMA((4 * n,)), pltpu.SemaphoreType.DMA((4 * n,))], start, finish)


def _exchange_between_chips(ps):
    n = len(ps)

    def copies(ins, outs, sems):
        x, y, c, chips = _place()
        return [pltpu.make_async_remote_copy(
            src_ref=ins[a].at[2 * cx + cy], dst_ref=outs[a].at[j], send_sem=sems[0].at[3 * a + j],
            recv_sem=sems[1].at[3 * a + j], device_id=(cx, cy, c), device_id_type=MESH)
            for a in range(n) for j, (cx, cy) in enumerate(chips)]

    def start(ins, outs, sems):
        for cp in copies(ins, outs, sems):
            cp.start()

    def finish(ins, outs, sems):
        for cp in copies(ins, outs, sems):
            cp.wait()

    return _Side(ps, [_sds((3,) + p.shape[1:], p.dtype) for p in ps],
                 [pltpu.SemaphoreType.DMA((3 * n,)), pltpu.SemaphoreType.DMA((3 * n,))], start, finish)


def _all_reduce_small(v, name):
    R = v.shape[0]

    def body(v_ref, sum_ref, all_ref, send_sems, recv_sems):
        x, y, c, _ = _place()
        k = 4 * x + 2 * y + c
        all_ref[k] = v_ref[...]
        copies = []
        for rel in range(1, N_DEV):
            fx, fy, fc = (rel >> 2) & 1, (rel >> 1) & 1, rel & 1
            peer = (1 - x if fx else x, 1 - y if fy else y, 1 - c if fc else c)
            copies.append(pltpu.make_async_remote_copy(
                src_ref=v_ref, dst_ref=all_ref.at[k], send_sem=send_sems.at[rel - 1], recv_sem=recv_sems.at[rel - 1],
                device_id=peer, device_id_type=MESH))
        for cp in copies:
            cp.start()
        for rel in range(1, N_DEV):
            fx, fy, fc = (rel >> 2) & 1, (rel >> 1) & 1, rel & 1
            src = 4 * (1 - x if fx else x) + 2 * (1 - y if fy else y) + (1 - c if fc else c)
            pltpu.make_async_remote_copy(
                src_ref=v_ref, dst_ref=all_ref.at[src], send_sem=send_sems.at[rel - 1], recv_sem=recv_sems.at[rel - 1],
                device_id=(x, y, c), device_id_type=MESH).wait_recv()
        for cp in copies:
            cp.wait_send()
        tot = all_ref[0]
        for d in range(1, N_DEV):
            tot = tot + all_ref[d]
        sum_ref[...] = tot

    vm = pl.BlockSpec(memory_space=pltpu.VMEM)
    return pl.pallas_call(
        body, name=name, in_specs=[vm], out_specs=[vm, vm],
        out_shape=[_sds((R, HEAD), F32), _sds((N_DEV, R, HEAD), F32)],
        scratch_shapes=[pltpu.SemaphoreType.DMA((N_DEV - 1,)), pltpu.SemaphoreType.DMA((N_DEV - 1,))],
    )(v)[0]


def _tiles(rows, cols):
    tc = 512 if cols % 512 == 0 else cols
    tr = rows
    while tr * tc * 4 > ELEMENTWISE_BLOCK_BYTES and tr % (2 * BF16_TILE_ROWS) == 0:
        tr //= 2
    return tr, tc


def _chip_sum(g, r1, place, name):
    _, R, Cc = g.shape
    tr, tc = _tiles(R, Cc)

    def body(place_ref, g_ref, r_ref, p_ref):
        p_ref[...] = (g_ref[...] + r_ref[...]).astype(BF16)

    chip = lambda k, place: k + (k >= place[1]).astype(jnp.int32)
    blk = lambda f: pl.BlockSpec((None, tr, tc), f)
    return pl.pallas_call(
        body, name=name,
        grid_spec=pltpu.PrefetchScalarGridSpec(
            num_scalar_prefetch=1, grid=(3, R // tr, Cc // tc),
            in_specs=[blk(lambda k, i, j, place: (2 * chip(k, place) + place[0], i, j)),
                      blk(lambda k, i, j, place: (chip(k, place), i, j))],
            out_specs=blk(lambda k, i, j, place: (chip(k, place), i, j))),
        out_shape=_sds((4, R, Cc), BF16), compiler_params=_cparams(),
    )(place, g, r1)


def _adamw_update(gv, w_ref, m_ref, v_ref, d_ref, nm_ref, nv_ref):
    nm = B1 * m_ref[...] + (1.0 - B1) * gv
    nv = B2 * v_ref[...] + (1.0 - B2) * jnp.square(gv)
    m_hat = nm / (1.0 - B1 ** STEP)
    v_hat = nv / (1.0 - B2 ** STEP)
    d_ref[...] = -LR * (m_hat / (jnp.sqrt(v_hat) + EPS) + WD * w_ref[...])
    nm_ref[...] = nm
    nv_ref[...] = nv


def _reduce_adamw(g, r1, r2, where, w, m, v, name):
    _, R, Cc = g.shape
    tr, tc = _tiles(R, Cc)

    def body(where_ref, g_ref, r1_ref, r2_ref, w_ref, m_ref, v_ref, o_ref, d_ref, nm_ref, nv_ref):
        gv = ((g_ref[...] + r1_ref[...]) + r2_ref[0].astype(F32)) + (r2_ref[1].astype(F32) + r2_ref[2].astype(F32))
        o_ref[...] = gv
        _adamw_update(gv, w_ref, m_ref, v_ref, d_ref, nm_ref, nv_ref)

    blk = pl.BlockSpec((tr, tc), lambda i, j, w: (i, j))
    return pl.pallas_call(
        body, name=name,
        grid_spec=pltpu.PrefetchScalarGridSpec(
            num_scalar_prefetch=1, grid=(R // tr, Cc // tc),
            in_specs=[pl.BlockSpec((None, tr, tc), lambda i, j, w: (w[0], i, j)),
                      pl.BlockSpec((None, tr, tc), lambda i, j, w: (w[1], i, j)),
                      pl.BlockSpec((3, tr, tc), lambda i, j, w: (0, i, j)), blk, blk, blk],
            out_specs=[blk] * 4),
        out_shape=[_sds((R, Cc), F32)] * 4, compiler_params=_cparams(),
    )(where, g, r1, r2, w, m, v)


def _adamw(w, g, m, v, name):
    R, Cc = w.shape
    tr, tc = _tiles(R, Cc)

    def body(w_ref, g_ref, m_ref, v_ref, d_ref, nm_ref, nv_ref):
        _adamw_update(g_ref[...], w_ref, m_ref, v_ref, d_ref, nm_ref, nv_ref)

    blk = pl.BlockSpec((tr, tc), lambda i, j: (i, j))
    return pl.pallas_call(
        body, name=name, grid=(R // tr, Cc // tc), in_specs=[blk] * 4, out_specs=[blk] * 3,
        out_shape=[_sds((R, Cc), F32)] * 3, compiler_params=_cparams(),
    )(w, g, m, v)


SMALL = ("ffn1_norm", "mix_norm", "ffn2_norm", "q_norm_a", "k_norm_a", "q_norm_b", "k_norm_b", "forget_bias",
         "rel_bias_table")
LARGE = ("ffn1_w_in", "ffn1_w_out", "w_in", "w_out", "ffn2_w_in", "ffn2_w_out")
ORDER = ("ffn1_norm", "ffn1_w_in", "ffn1_w_out", "mix_norm", "w_in", "q_norm_a", "k_norm_a", "q_norm_b", "k_norm_b",
         "forget_bias", "rel_bias_table", "w_out", "ffn2_norm", "ffn2_w_in", "ffn2_w_out")


def _pack_small(vals):
    rows = []
    for name in SMALL:
        flat = vals[name].reshape(-1)
        pad = (-flat.shape[0]) % HEAD
        rows.append(jnp.pad(flat, (0, pad)).reshape(-1, HEAD))
    return jnp.concatenate(rows, axis=0)


def _unpack_small(packed, like):
    out, r = {}, 0
    for name in SMALL:
        size = like[name].size
        nrow = -(-size // HEAD)
        out[name] = packed[r:r + nrow].reshape(-1)[:size].reshape(like[name].shape)
        r += nrow
    return out


def kernel(x, ffn1_norm, ffn1_w_in, ffn1_w_out, mix_norm, w_in, q_norm_a, k_norm_a, q_norm_b, k_norm_b, forget_bias, rel_bias_table, w_out, ffn2_norm, ffn2_w_in, ffn2_w_out, loss_target, m_ffn1_norm, m_ffn1_w_in, m_ffn1_w_out, m_mix_norm, m_w_in, m_q_norm_a, m_k_norm_a, m_q_norm_b, m_k_norm_b, m_forget_bias, m_rel_bias_table, m_w_out, m_ffn2_norm, m_ffn2_w_in, m_ffn2_w_out, v_ffn1_norm, v_ffn1_w_in, v_ffn1_w_out, v_mix_norm, v_w_in, v_q_norm_a, v_k_norm_a, v_q_norm_b, v_k_norm_b, v_forget_bias, v_rel_bias_table, v_w_out, v_ffn2_norm, v_ffn2_w_in, v_ffn2_w_out):
    w = dict(ffn1_norm=ffn1_norm, ffn1_w_in=ffn1_w_in, ffn1_w_out=ffn1_w_out, mix_norm=mix_norm, w_in=w_in,
             q_norm_a=q_norm_a, k_norm_a=k_norm_a, q_norm_b=q_norm_b, k_norm_b=k_norm_b, forget_bias=forget_bias,
             rel_bias_table=rel_bias_table, w_out=w_out, ffn2_norm=ffn2_norm, ffn2_w_in=ffn2_w_in, ffn2_w_out=ffn2_w_out)
    m = dict(ffn1_norm=m_ffn1_norm, ffn1_w_in=m_ffn1_w_in, ffn1_w_out=m_ffn1_w_out, mix_norm=m_mix_norm, w_in=m_w_in,
             q_norm_a=m_q_norm_a, k_norm_a=m_k_norm_a, q_norm_b=m_q_norm_b, k_norm_b=m_k_norm_b,
             forget_bias=m_forget_bias, rel_bias_table=m_rel_bias_table, w_out=m_w_out, ffn2_norm=m_ffn2_norm,
             ffn2_w_in=m_ffn2_w_in, ffn2_w_out=m_ffn2_w_out)
    v = dict(ffn1_norm=v_ffn1_norm, ffn1_w_in=v_ffn1_w_in, ffn1_w_out=v_ffn1_w_out, mix_norm=v_mix_norm, w_in=v_w_in,
             q_norm_a=v_q_norm_a, k_norm_a=v_k_norm_a, q_norm_b=v_q_norm_b, k_norm_b=v_k_norm_b,
             forget_bias=v_forget_bias, rel_bias_table=v_rel_bias_table, w_out=v_w_out, ffn2_norm=v_ffn2_norm,
             ffn2_w_in=v_ffn2_w_in, ffn2_w_out=v_ffn2_w_out)
    T, D = x.shape[1], x.shape[2]
    C = D // 2
    H = C // HEAD
    ff_shard = ffn1_w_out.shape[1]

    f1i, = _run_side(_all_gather_relayed([ffn1_w_in[0].T.astype(BF16)]), "gather_ffn1")
    wts = dict(ffn1_in=f1i.reshape(2, N_DEV, ff_shard, D))
    xi, yi, ci = lax.axis_index("x"), lax.axis_index("y"), lax.axis_index("c")
    place = jnp.stack([ci, 2 * xi + yi]).astype(jnp.int32)
    where = jnp.stack([4 * xi + 2 * yi + ci, 2 * xi + yi]).astype(jnp.int32)
    gs, r1, ps, r2 = {}, {}, {}, {}

    def shard(name):
        name, _, part = name.partition(":")
        s = (w[name][0].T if name.endswith("w_in") else w[name][0]).astype(BF16)
        first = -(-(s.shape[0] // 2) // BF16_TILE_ROWS) * BF16_TILE_ROWS
        return {"": s, "first": s[:first], "rest": s[first:]}[part]

    def by_destination(name, grads):
        key = name + "_t" if name.endswith("w_in") else name
        gs[name] = grads[key].reshape(N_DEV, -1, D)
        return grads.get(key + "_bf16", grads[key]).reshape(N_DEV, -1, D)

    def chip_sums(names):
        for name in names:
            ps[name] = _chip_sum(gs[name], r1[name], place, "chip_sum_" + name)
        return [ps[name] for name in names]

    class Plan:
        carried = {"ffn1_up": ("gather", ("ffn1_w_out", "w_in:first")), "ffn1_down": ("gather", ("w_in:rest", "w_out")),
                   "dil_fwd": ("gather", ("ffn2_w_out",)), "fox_fwd": ("gather", ("ffn2_w_in",)),
                   "dil_bwd": ("in_chip", ("ffn2_w_in", "ffn2_w_out")), "fox_bwd": ("between", ("ffn2_w_in", "ffn2_w_out")),
                   "ffn1_bwd": ("in_chip", ("w_in", "w_out")), "ffn1_dwin": ("between", ("w_in", "w_out")),
                   "ffn1_rms_bwd": ("in_chip", ("ffn1_w_in",)), "ffn1_dwout": ("between", ("ffn1_w_in",))}

        def before(self, host, wts, grads):
            if host not in self.carried:
                return None
            kind, names = self.carried[host]
            if kind == "gather":
                return _all_gather([shard(n) for n in names])
            if kind == "in_chip":
                return _exchange_in_chip([by_destination(n, grads) for n in names])
            return _exchange_between_chips(chip_sums(names))

        def after(self, host, res, wts, grads):
            kind, names = self.carried[host]
            if host == "ffn1_up":
                wts.update(ffn1_out=res[0], w_in_first_rows=res[1])
            elif host == "ffn1_down":
                w_in_t = jnp.concatenate([wts.pop("w_in_first_rows"), res[0]], axis=1).reshape(-1, D)
                wts.update(w_in_t=w_in_t, w_f_t=jnp.pad(w_in_t[6 * C:], ((0, HEAD - H), (0, 0))),
                           w_o=res[1].reshape(2 * C, D))
            elif host == "dil_fwd":
                wts.update(ffn2_out=res[0])
            elif host == "fox_fwd":
                wts.update(ffn2_in=res[0].reshape(2, N_DEV, ff_shard, D))
            else:
                (r1 if kind == "in_chip" else r2).update(zip(names, res))

    small = {name: w[name] for name in SMALL}
    loss_row, grad_x, grads = _local_step(x[0], loss_target[0], small, wts, Plan())

    tail = ("ffn1_w_out",)
    r1[tail[0]], = _run_side(_exchange_in_chip([by_destination(tail[0], grads)]), "reduce_in_chip_tail")
    r2.update(zip(tail, _run_side(_exchange_between_chips(chip_sums(tail)), "reduce_between_chips_tail")))

    packed = _pack_small(grads)
    nsmall = packed.shape[0]
    packed = jnp.concatenate([packed, loss_row, jnp.zeros(((-nsmall - 1) % 8, HEAD), F32)], axis=0)
    reduced = _all_reduce_small(packed, "reduce_small")
    loss = reduced[nsmall, 0]
    g_small = _unpack_small(reduced[:nsmall], small)

    grad, delta, new_m, new_v = dict(g_small), {}, {}, {}
    for name in LARGE:
        to = (lambda t: t[0].T) if name.endswith("w_in") else (lambda t: t[0])
        back = (lambda t: t.T[None]) if name.endswith("w_in") else (lambda t: t[None])
        res = _reduce_adamw(gs[name], r1[name], r2[name], where, to(w[name]), to(m[name]), to(v[name]), "adamw_" + name)
        grad[name], delta[name], new_m[name], new_v[name] = (back(t) for t in res)
    d, nm, nv = _adamw(_pack_small(w), reduced[:nsmall], _pack_small(m), _pack_small(v), "adamw_small")
    delta.update(_unpack_small(d, small))
    new_m.update(_unpack_small(nm, small))
    new_v.update(_unpack_small(nv, small))
    return (loss, grad_x[None], *[grad[n] for n in ORDER], *[delta[n] for n in ORDER],
            *[new_m[n] for n in ORDER], *[new_v[n] for n in ORDER])
```

```python
import functools
import math

import numpy as np
import jax
import jax.numpy as jnp
from jax import lax
from jax.experimental import pallas as pl
from jax.experimental.pallas import tpu as pltpu

F32, BF16 = jnp.float32, jnp.bfloat16
HEAD = 128
NSLAB = 16
BLK = 128
DILATIONS = (1, 4, 16)
NUM_BUCKETS, MAX_DISTANCE = 32, 2048
RMS_EPS = 1e-6
NEG = -1e30
SCALE = HEAD ** -0.5
LR, B1, B2, EPS, WD, STEP = 0.001, 0.9, 0.999, 1e-08, 0.01, 10
N_DEV = 8
VMEM_LIMIT_BYTES = 56 << 20
ELEMENTWISE_BLOCK_BYTES = 2 << 20
BF16_TILE_ROWS = 16
MESH = pl.DeviceIdType.MESH


def _cparams(**kw):
    return pltpu.CompilerParams(vmem_limit_bytes=VMEM_LIMIT_BYTES, **kw)


def _nn(a, b):
    return jnp.dot(a, b, preferred_element_type=F32)


def _nt(a, b):
    return lax.dot_general(a, b, (((1,), (1,)), ((), ())), preferred_element_type=F32)


def _tn(a, b):
    return lax.dot_general(a, b, (((0,), (0,)), ((), ())), preferred_element_type=F32)


def _sds(shape, dtype):
    return jax.ShapeDtypeStruct(shape, dtype)


ANY = pl.BlockSpec(memory_space=pl.ANY)


class _Side:
    def __init__(self, ins, outs, sems, start, finish):
        self.ins, self.outs, self.sems, self.start, self.finish = list(ins), list(outs), list(sems), start, finish


def _call(body, *, name, grid, in_specs, out_specs, out_shape, args, scratch_shapes=(), side=None):
    in_specs, out_specs, out_shape = list(in_specs), list(out_specs), list(out_shape)
    scratch_shapes = list(scratch_shapes)
    if side is None:
        return pl.pallas_call(body, name=name, grid=grid, in_specs=in_specs, out_specs=out_specs, out_shape=out_shape,
                              scratch_shapes=scratch_shapes, compiler_params=_cparams())(*args)
    ni, no, ns = len(args), len(out_shape), len(scratch_shapes)
    si, so = len(side.ins), len(side.outs)

    def fused(*refs):
        h_in, s_in = refs[:ni], refs[ni:ni + si]
        h_out, s_out = refs[ni + si:ni + si + no], refs[ni + si + no:ni + si + no + so]
        h_scr, s_sem = refs[ni + si + no + so:ni + si + no + so + ns], refs[ni + si + no + so + ns:]
        ids = [pl.program_id(k) for k in range(len(grid))]
        first = functools.reduce(jnp.logical_and, [i == 0 for i in ids])
        last = functools.reduce(jnp.logical_and, [i == n - 1 for i, n in zip(ids, grid)])

        @pl.when(first)
        def _():
            side.start(s_in, s_out, s_sem)

        body(*h_in, *h_out, *h_scr)

        @pl.when(last)
        def _():
            side.finish(s_in, s_out, s_sem)

    res = pl.pallas_call(
        fused, name=name, grid=grid, in_specs=in_specs + [ANY] * si, out_specs=out_specs + [ANY] * so,
        out_shape=out_shape + side.outs, scratch_shapes=scratch_shapes + side.sems, compiler_params=_cparams(),
    )(*args, *side.ins)
    return list(res[:no]), list(res[no:])


def _ffn_fwd(x, g, win, wout, tm, name, side=None, target=None):
    T, D = x.shape
    nc, tf = win.shape[1], win.shape[2]
    down, loss = wout is not None, target is not None
    assert down or not loss

    def body(x_ref, g_ref, win_ref, *refs):
        ins, outs = refs[:down + loss], refs[down + loss:]
        h_ref, gu_ref, act_ref = outs[down:down + 3]
        i, j = pl.program_id(0), pl.program_id(1)

        @pl.when(j == 0)
        def _():
            xv = x_ref[...]
            r = lax.rsqrt(jnp.mean(xv * xv, axis=-1, keepdims=True) + RMS_EPS)
            h_ref[...] = (xv * r * g_ref[...]).astype(BF16)
            if down:
                outs[0][...] = jnp.zeros_like(outs[0])

        if loss:
            @pl.when((i == 0) & (j == 0))
            def _():
                outs[-1][...] = jnp.zeros_like(outs[-1])

        hb = h_ref[...]
        gt = _nt(win_ref[0], hb)
        up = _nt(win_ref[1], hb)
        gu_ref[0] = gt.astype(BF16)
        gu_ref[1] = up.astype(BF16)
        act = (gt * jax.nn.sigmoid(gt) * up).astype(BF16)
        act_ref[...] = act
        if down:
            y_ref = outs[0]
            y_ref[...] += _tn(act, ins[0][...])

            @pl.when(j == nc - 1)
            def _():
                y = x_ref[...] + 0.5 * y_ref[...]
                if not loss:
                    y_ref[...] = y
                    return
                err = y - ins[1][...]
                y_ref[...] = err * (1.0 / D)
                tot = 0.5 * jnp.sum(jnp.mean(err * err, axis=-1, keepdims=True), axis=0, keepdims=True)
                lane = lax.broadcasted_iota(jnp.int32, (1, HEAD), 1)
                outs[-1][...] += jnp.where(lane == 0, tot, 0.0)

    row = pl.BlockSpec((tm, D), lambda i, j: (i, 0))
    return _call(
        body, name=name, grid=(T // tm, nc), side=side,
        in_specs=[row, pl.BlockSpec((1, D), lambda i, j: (0, 0)),
                  pl.BlockSpec((2, None, tf, D), lambda i, j: (0, j, 0, 0))]
        + ([pl.BlockSpec((None, tf, D), lambda i, j: (j, 0, 0))] if down else []) + ([row] if loss else []),
        out_specs=([row] if down else [])
        + [row, pl.BlockSpec((2, None, tf, tm), lambda i, j: (0, j, 0, i)),
           pl.BlockSpec((None, tf, tm), lambda i, j: (j, 0, i))]
        + ([pl.BlockSpec((1, HEAD), lambda i, j: (0, 0))] if loss else []),
        out_shape=([_sds((T, D), F32)] if down else [])
        + [_sds((T, D), BF16), _sds((2, nc, tf, T), BF16), _sds((nc, tf, T), BF16)]
        + ([_sds((1, HEAD), F32)] if loss else []),
        args=(x, g, win) + ((wout,) if down else ()) + ((target,) if loss else ()))


def _ffn_down(x, act, wout, tm, name, side=None):
    T, D = x.shape
    nc, tf = wout.shape[0], wout.shape[1]

    def body(x_ref, act_ref, wout_ref, y_ref):
        j = pl.program_id(1)

        @pl.when(j == 0)
        def _():
            y_ref[...] = jnp.zeros_like(y_ref)

        y_ref[...] += _tn(act_ref[...], wout_ref[...])

        @pl.when(j == nc - 1)
        def _():
            y_ref[...] = x_ref[...] + 0.5 * y_ref[...]

    row = pl.BlockSpec((tm, D), lambda i, j: (i, 0))
    res = _call(
        body, name=name, grid=(T // tm, nc), side=side,
        in_specs=[row, pl.BlockSpec((None, tf, tm), lambda i, j: (j, 0, i)),
                  pl.BlockSpec((None, tf, D), lambda i, j: (j, 0, 0))],
        out_specs=[row], out_shape=[_sds((T, D), F32)], args=(x, act, wout))
    return res[0] if side is None else (res[0][0], res[1])


def _ffn_bwd(dy, gu, win, wout, tm, name, side=None):
    T, D = dy.shape
    nc, tf = wout.shape[0], wout.shape[1]

    def body(dy_ref, gu_ref, win_ref, wout_ref, dh_ref, dgu_ref, dyb_ref):
        j = pl.program_id(1)

        @pl.when(j == 0)
        def _():
            dh_ref[...] = jnp.zeros_like(dh_ref)
            dyb_ref[...] = (0.5 * dy_ref[...]).astype(BF16)

        dact = _nt(wout_ref[...], dyb_ref[...])
        gt = gu_ref[0].astype(F32)
        up = gu_ref[1].astype(F32)
        s = jax.nn.sigmoid(gt)
        dgb = (dact * up * (s * (1.0 + gt * (1.0 - s)))).astype(BF16)
        dub = (dact * (gt * s)).astype(BF16)
        dgu_ref[0] = dgb
        dgu_ref[1] = dub
        dh_ref[...] += _tn(dgb, win_ref[0]) + _tn(dub, win_ref[1])

    return _call(
        body, name=name, grid=(T // tm, nc), side=side,
        in_specs=[pl.BlockSpec((tm, D), lambda i, j: (i, 0)),
                  pl.BlockSpec((2, None, tf, tm), lambda i, j: (0, j, 0, i)),
                  pl.BlockSpec((2, None, tf, D), lambda i, j: (0, j, 0, 0)),
                  pl.BlockSpec((None, tf, D), lambda i, j: (j, 0, 0))],
        out_specs=[pl.BlockSpec((tm, D), lambda i, j: (i, 0)),
                   pl.BlockSpec((2, None, tf, tm), lambda i, j: (0, j, 0, i)),
                   pl.BlockSpec((tm, D), lambda i, j: (i, 0))],
        out_shape=[_sds((T, D), F32), _sds((2, nc, tf, T), BF16), _sds((T, D), BF16)],
        args=(dy, gu, win, wout))


def _rms_bwd(x, g, dh, dres, tm, name, side=None):
    T, D = x.shape

    def body(x_ref, g_ref, dh_ref, dres_ref, dx_ref, dg_ref):
        @pl.when(pl.program_id(0) == 0)
        def _():
            dg_ref[...] = jnp.zeros_like(dg_ref)

        xv = x_ref[...]
        r = lax.rsqrt(jnp.mean(xv * xv, axis=-1, keepdims=True) + RMS_EPS)
        xhat = xv * r
        dh = dh_ref[...]
        gd = dh * g_ref[...]
        dx_ref[...] = dres_ref[...] + r * (gd - xhat * jnp.mean(gd * xhat, axis=-1, keepdims=True))
        dg_ref[...] += jnp.sum(dh * xhat, axis=0, keepdims=True)

    row = pl.BlockSpec((tm, D), lambda i: (i, 0))
    one = pl.BlockSpec((1, D), lambda i: (0, 0))
    return _call(body, name=name, grid=(T // tm,), side=side, in_specs=[row, one, row, row], out_specs=[row, one],
                 out_shape=[_sds((T, D), F32), _sds((1, D), F32)], args=(x, g, dh, dres))


def _mm_tn(a, b, *, bm, bn, bt, name, side=None, rows=None, m_off=0, into=None, a_rows=False, twin=False):
    nz, T, M = (a.shape[0], a.shape[2], a.shape[1]) if a_rows else a.shape
    N = b.shape[1]
    b_spec = pl.BlockSpec((bt, bn), lambda n, z, m, t: (t, n))
    assert M % bm == 0 and N % bn == 0 and T % bt == 0, (M, bm, N, bn, T, bt)

    def body(a_ref, b_ref, *rest):
        c_ref = rest[-2] if twin else rest[-1]

        @pl.when(pl.program_id(3) == 0)
        def _():
            c_ref[...] = jnp.zeros_like(c_ref)

        ab, bb = a_ref[...].astype(BF16), b_ref[...].astype(BF16)
        c_ref[...] += _nn(ab, bb) if a_rows else _tn(ab, bb)
        if twin:
            @pl.when(pl.program_id(3) == T // bt - 1)
            def _():
                rest[-1][...] = c_ref[...].astype(BF16)

    grid = (N // bn, nz, M // bm, T // bt)
    a_spec = (pl.BlockSpec((None, bm, bt), lambda n, z, m, t: (z, m, t)) if a_rows
              else pl.BlockSpec((None, bt, bm), lambda n, z, m, t: (z, t, m)))
    in_specs = [a_spec, b_spec]
    out_spec = pl.BlockSpec((None, bm, bn), lambda n, z, m, t: (z, m + m_off, n))
    out_shape = _sds((nz, M if rows is None else rows, N), F32)
    if into is not None:
        assert side is None and not twin and into.shape == out_shape.shape
        return pl.pallas_call(body, name=name, grid=grid, in_specs=in_specs + [ANY], out_specs=out_spec,
                              out_shape=out_shape, input_output_aliases={2: 0}, compiler_params=_cparams())(a, b, into)
    outs = [out_shape] + ([_sds(out_shape.shape, BF16)] if twin else [])
    res = _call(body, name=name, grid=grid, side=side, in_specs=in_specs, out_specs=[out_spec] * len(outs),
                out_shape=outs, args=(a, b))
    mine = res if side is None else res[0]
    mine = tuple(mine) if twin else mine[0]
    return mine if side is None else (mine, res[1])


def _tok_spec(layout, tm, n16, C, bc, colmap):
    if layout == "nat":
        return pl.BlockSpec((tm, bc), lambda i, k: (i, colmap(k)))
    assert tm % n16 == 0
    if layout == "slab":
        return pl.BlockSpec((tm // n16, n16, bc), lambda i, k: (i, 0, colmap(k)))
    assert bc == C
    return pl.BlockSpec((n16, (tm // n16) * C), lambda i, k: (0, i))


def _tok_load(ref, layout, sp):
    if layout == "nat":
        return ref[...]
    if layout == "slab":
        return ref[...].reshape(-1, ref.shape[-1])
    c = ref.shape[1] // sp
    return jnp.concatenate([ref[:, s * c:(s + 1) * c] for s in range(sp)], axis=0)


def _tok_store(ref, layout, sp, val, cols=None, accumulate=False):
    def put(idx, v):
        if accumulate:
            ref[idx] += v
        else:
            ref[idx] = v

    lanes = slice(None) if cols is None else slice(cols[0], cols[0] + cols[1])
    if layout == "nat":
        put((slice(None), lanes), val)
    elif layout == "slab":
        put((slice(None), slice(None), lanes), val.reshape(sp, ref.shape[1], val.shape[-1]))
    else:
        assert cols is None
        c, n = ref.shape[1] // sp, ref.shape[0]
        for s in range(sp):
            put((slice(None), slice(s * c, (s + 1) * c)), val[s * n:(s + 1) * n])


def _proj(x, g, wt, gains, modes, *, tn, w_off, slabs, tm, normed_dtype, name, small_wt=None):
    T, D = x.shape
    ntile = len(modes)
    N = ntile * tn
    n16 = T // NSLAB
    in_layout, out_layout = ("view", "slab") if slabs else ("nat", "nat")
    sp = tm // n16
    extra = int(small_wt is not None)
    assert not (extra and slabs)
    x_in = x.reshape(n16, NSLAB * D) if slabs else x
    x_spec = _tok_spec(in_layout, tm, n16, D, D, lambda n: 0)
    oshape = lambda c: (NSLAB, n16, c) if slabs else (T, c)
    ospec = lambda bc, cm: _tok_spec(out_layout, tm, n16, None, bc, cm)

    def body(x_ref, g_ref, w_ref, gains_ref, *refs):
        raw_ref, nrm_ref, h_ref = refs[-3 - extra:len(refs) - extra]
        n = pl.program_id(1)

        @pl.when(n == 0)
        def _():
            xv = _tok_load(x_ref, in_layout, sp)
            r = lax.rsqrt(jnp.mean(xv * xv, axis=-1, keepdims=True) + RMS_EPS)
            hb = (xv * r * g_ref[...]).astype(BF16)
            _tok_store(h_ref, out_layout, sp, hb)
            if extra:
                refs[-1][...] = _nt(hb, refs[0][...])

        y = _nt(_tok_load(h_ref, out_layout, sp), w_ref[...])
        _tok_store(raw_ref, out_layout, sp, y)
        for t, mode in enumerate(modes):
            @pl.when(n == t)
            def _(t=t, mode=mode):
                if not mode:
                    _tok_store(nrm_ref, out_layout, sp, y.astype(nrm_ref.dtype))
                    return
                gain = gains_ref[t]
                for k in range(tn // HEAD):
                    yk = y[:, k * HEAD:(k + 1) * HEAD]
                    r = lax.rsqrt(jnp.mean(yk * yk, axis=-1, keepdims=True) + RMS_EPS)
                    _tok_store(nrm_ref, out_layout, sp, (yk * r * gain).astype(nrm_ref.dtype), cols=(k * HEAD, HEAD))

    return pl.pallas_call(
        body, name=name, grid=(T // tm, ntile),
        in_specs=[x_spec, pl.BlockSpec((1, D), lambda i, n: (0, 0)),
                  pl.BlockSpec((tn, D), lambda i, n: (n + w_off, 0)),
                  pl.BlockSpec((ntile, 1, HEAD), lambda i, n: (0, 0, 0))]
        + ([pl.BlockSpec((HEAD, D), lambda i, n: (0, 0))] if extra else []),
        out_specs=[ospec(tn, lambda n: n), ospec(tn, lambda n: n), ospec(D, lambda n: 0)]
        + ([ospec(HEAD, lambda n: 0)] if extra else []),
        out_shape=[_sds(oshape(N), F32), _sds(oshape(N), normed_dtype), _sds(oshape(D), BF16)]
        + ([_sds((T, HEAD), F32)] if extra else []),
        compiler_params=_cparams(),
    )(x_in, g, wt, gains, *([small_wt] if extra else []))


def _mm(a, w, *, nt, tk, tm, a_layout, out_layout, resid=None, name, w_off=0, n_out=None, small=None):
    if a_layout == "slab":
        T, K = a.shape[0] * a.shape[1], a.shape[2]
    else:
        T, K = a.shape
    N = (w.shape[0] if nt else w.shape[1]) if n_out is None else n_out
    n16 = T // NSLAB
    nk = K // tk
    sp = tm // n16
    a_in = a.reshape(n16, NSLAB * K) if a_layout == "view" else a
    w_spec = (pl.BlockSpec((N, tk), lambda i, k: (w_off, k)) if nt
              else pl.BlockSpec((tk, N), lambda i, k: (k + w_off, 0)))
    o_spec = _tok_spec(out_layout, tm, n16, N, N, lambda k: 0)
    oshape = {"nat": (T, N), "slab": (NSLAB, n16, N), "view": (n16, NSLAB * N)}[out_layout]
    has_resid = resid is not None

    def body(*refs):
        a_ref, w_ref = refs[0], refs[1]
        o_ref = refs[-1]
        k = pl.program_id(1)

        @pl.when(k == 0)
        def _():
            o_ref[...] = refs[2][...] if has_resid else jnp.zeros_like(o_ref)
            if small is not None:
                o_ref[...] += _nn(refs[-3][...].astype(BF16), refs[-2][...])

        ab = _tok_load(a_ref, a_layout, sp).astype(BF16)
        _tok_store(o_ref, out_layout, sp, _nt(ab, w_ref[...]) if nt else _nn(ab, w_ref[...]), accumulate=True)

    ins = [a_in, w]
    in_specs = [_tok_spec(a_layout, tm, n16, K, tk, lambda k: k), w_spec]
    if has_resid:
        ins.append(resid.reshape(n16, NSLAB * N) if out_layout == "view" else resid)
        in_specs.append(o_spec)
    if small is not None:
        assert out_layout == "nat" and small[1].shape == (HEAD, N)
        ins += list(small)
        in_specs += [pl.BlockSpec((tm, HEAD), lambda i, k: (i, 0)), pl.BlockSpec((HEAD, N), lambda i, k: (0, 0))]
    out = pl.pallas_call(
        body, name=name, grid=(T // tm, nk), in_specs=in_specs, out_specs=o_spec,
        out_shape=_sds(oshape, F32), compiler_params=_cparams(),
    )(*ins)
    return out.reshape(T, N) if out_layout == "view" else out


def _log_sigmoid(z):
    return jnp.minimum(z, 0.0) - jnp.log(1.0 + jnp.exp(-jnp.abs(z)))


def _fox_gate_fwd(f_raw, fbias, name):
    T = f_raw.shape[0]
    cb = 256

    def body(f_ref, b_ref, c_ref):
        row = lax.broadcasted_iota(jnp.int32, (cb, cb), 0)
        col = lax.broadcasted_iota(jnp.int32, (cb, cb), 1)
        tri = (col <= row).astype(F32)
        carry = jnp.zeros((1, HEAD), F32)
        for i in range(T // cb):
            lf = _log_sigmoid(f_ref[i * cb:(i + 1) * cb, :] + b_ref[...])
            c = jnp.dot(tri, lf, preferred_element_type=F32, precision=lax.Precision.HIGHEST) + carry
            c_ref[i * cb:(i + 1) * cb, :] = c
            carry = c[cb - 1:cb, :]

    return pl.pallas_call(body, name=name, out_shape=_sds((T, HEAD), F32), compiler_params=_cparams())(f_raw, fbias)


def _fox_gate_bwd(f_raw, fbias, dc, name):
    T = f_raw.shape[0]
    cb = 256

    def body(f_ref, b_ref, dc_ref, df_ref, db_ref):
        row = lax.broadcasted_iota(jnp.int32, (cb, cb), 0)
        col = lax.broadcasted_iota(jnp.int32, (cb, cb), 1)
        tri = (col >= row).astype(F32)
        carry = jnp.zeros((1, HEAD), F32)
        dbias = jnp.zeros((1, HEAD), F32)
        for i in reversed(range(T // cb)):
            dlf = jnp.dot(tri, dc_ref[i * cb:(i + 1) * cb, :], preferred_element_type=F32,
                          precision=lax.Precision.HIGHEST) + carry
            carry = dlf[0:1, :]
            z = f_ref[i * cb:(i + 1) * cb, :] + b_ref[...]
            df = dlf * jax.nn.sigmoid(-z)
            df_ref[i * cb:(i + 1) * cb, :] = df
            dbias = dbias + jnp.sum(df, axis=0, keepdims=True)
        db_ref[...] = dbias

    return pl.pallas_call(body, name=name, out_shape=[_sds((T, HEAD), F32), _sds((1, HEAD), F32)],
                          compiler_params=_cparams())(f_raw, fbias, dc)


def _fox_fwd(qkv, c_col, c_row, tq, name, side=None):
    T = qkv.shape[0]
    H = qkv.shape[1] // (3 * HEAD)
    nq = T // tq
    c_blocks = c_row.reshape(H, nq, 1, tq)

    def body(q_ref, k_ref, v_ref, cq_ref, ck_ref, o_ref, lse_ref):
        qi = pl.program_id(1)
        q, cq = q_ref[...], cq_ref[...]
        causal = lax.broadcasted_iota(jnp.int32, (tq, tq), 1) <= lax.broadcasted_iota(jnp.int32, (tq, tq), 0)

        def key_block(ki, carry, diagonal):
            m, l, acc = carry
            rows = pl.ds(pl.multiple_of(ki * tq, tq), tq)
            s = _nt(q, k_ref[rows, :]) * SCALE + cq - ck_ref[ki]
            if diagonal:
                s = jnp.where(causal, s, NEG)
            m_new = jnp.maximum(m, jnp.max(s, axis=-1, keepdims=True))
            alpha = jnp.exp(m - m_new)
            p = jnp.exp(s - m_new)
            l = alpha * l + jnp.sum(p, axis=-1, keepdims=True)
            acc = alpha * acc + _nn(p.astype(BF16), v_ref[rows, :])
            return m_new, l, acc

        init = (jnp.full((tq, 1), NEG, F32), jnp.zeros((tq, 1), F32), jnp.zeros((tq, HEAD), F32))
        carry = lax.fori_loop(0, qi, lambda ki, c: key_block(ki, c, False), init)
        m, l, acc = key_block(qi, carry, True)
        o_ref[...] = acc / l
        lse_ref[...] = m + jnp.log(l)

    return _call(
        body, name=name, grid=(H, nq), side=side,
        in_specs=[pl.BlockSpec((tq, HEAD), lambda h, qi: (qi, h)),
                  pl.BlockSpec((T, HEAD), lambda h, qi: (0, H + h)),
                  pl.BlockSpec((T, HEAD), lambda h, qi: (0, 2 * H + h)),
                  pl.BlockSpec((None, tq, 1), lambda h, qi: (h, qi, 0)),
                  pl.BlockSpec((None, nq, 1, tq), lambda h, qi: (h, 0, 0, 0))],
        out_specs=[pl.BlockSpec((tq, HEAD), lambda h, qi: (qi, h)),
                   pl.BlockSpec((None, tq, 1), lambda h, qi: (h, qi, 0))],
        out_shape=[_sds((T, H * HEAD), F32), _sds((H, T, 1), F32)],
        args=(qkv, qkv, qkv, c_col, c_blocks))


def _fox_bwd(qkv, c_col, c_row, out, dout, lse, tq, name, side=None, do_off=0):
    T = qkv.shape[0]
    H = qkv.shape[1] // (3 * HEAD)
    nq = T // tq

    def body(q_ref, k_ref, v_ref, cq_ref, ck_ref, o_ref, do_ref, lse_ref, dq_ref, dk_ref, dv_ref, dck_ref, dcq_ref,
             delta_s):
        ki = pl.program_id(1)

        @pl.when(ki == 0)
        def _():
            dq_ref[...] = jnp.zeros_like(dq_ref)
            dcq_ref[...] = jnp.zeros_like(dcq_ref)
            delta_s[...] = jnp.sum(do_ref[...] * o_ref[...], axis=-1, keepdims=True)

        k, v, ck = k_ref[...], v_ref[...], ck_ref[...]
        causal = lax.broadcasted_iota(jnp.int32, (tq, tq), 1) <= lax.broadcasted_iota(jnp.int32, (tq, tq), 0)

        def query_block(qi, carry, diagonal):
            dk, dv, dck = carry
            rows = pl.ds(pl.multiple_of(qi * tq, tq), tq)
            q = q_ref[rows, :]
            s = _nt(q, k) * SCALE + cq_ref[rows, :] - ck
            if diagonal:
                s = jnp.where(causal, s, NEG)
            p = jnp.exp(s - lse_ref[rows, :])
            dob = do_ref[rows, :].astype(BF16)
            ds = p * (_nt(dob, v) - delta_s[rows, :])
            dsb = ds.astype(BF16)
            dq_ref[rows, :] += _nn(dsb, k) * SCALE
            dcq_ref[rows, :] += jnp.sum(ds, axis=-1, keepdims=True)
            return dk + _tn(dsb, q), dv + _tn(p.astype(BF16), dob), dck - jnp.sum(ds, axis=0, keepdims=True)

        init = (jnp.zeros((tq, HEAD), F32), jnp.zeros((tq, HEAD), F32), jnp.zeros((1, tq), F32))
        carry = query_block(ki, init, True)
        dk, dv, dck = lax.fori_loop(ki + 1, nq, lambda qi, c: query_block(qi, c, False), carry)
        dk_ref[...] = dk * SCALE
        dv_ref[...] = dv
        dck_ref[...] = dck

    head = lambda off: pl.BlockSpec((T, HEAD), lambda h, ki: (0, off + h))
    col = pl.BlockSpec((None, T, 1), lambda h, ki: (h, 0, 0))
    return _call(
        body, name=name, grid=(H, nq), side=side,
        in_specs=[head(0),
                  pl.BlockSpec((tq, HEAD), lambda h, ki: (ki, H + h)),
                  pl.BlockSpec((tq, HEAD), lambda h, ki: (ki, 2 * H + h)),
                  col, pl.BlockSpec((None, 1, tq), lambda h, ki: (h, 0, ki)), head(0), head(do_off), col],
        out_specs=[head(0),
                   pl.BlockSpec((tq, HEAD), lambda h, ki: (ki, h)),
                   pl.BlockSpec((tq, HEAD), lambda h, ki: (ki, h)),
                   pl.BlockSpec((None, 1, tq), lambda h, ki: (h, 0, ki)), col],
        out_shape=[_sds((T, H * HEAD), F32), _sds((T, H * HEAD), F32), _sds((T, H * HEAD), F32), _sds((H, 1, T), F32),
                   _sds((H, T, 1), F32)],
        scratch_shapes=[pltpu.VMEM((T, 1), F32)],
        args=(qkv, qkv, qkv, c_col, c_row, out, dout, lse))


def _t5_bucket(dist):
    max_exact = NUM_BUCKETS // 2
    d = dist.astype(np.float32)
    large = max_exact + (np.log(np.maximum(d, np.float32(1.0)) / np.float32(max_exact))
                         / np.float32(math.log(MAX_DISTANCE / max_exact))
                         * np.float32(NUM_BUCKETS - max_exact)).astype(np.int32)
    large = np.minimum(large, NUM_BUCKETS - 1)
    return np.where(dist < max_exact, dist, large)


def _bucket_maps():
    maps = []
    for d in DILATIONS:
        e = NSLAB // d
        rows = BLK // e
        idx = np.arange(BLK)
        pos = e * (idx % rows) + idx // rows
        qpos = pos[:, None] + BLK
        kpos = np.concatenate([pos, pos + BLK])[None, :]
        delta = qpos - kpos
        band = (delta >= 0) & (delta <= BLK)
        bucket = _t5_bucket(np.clip(delta, 0, None) * d)
        maps.append(np.where(band, bucket, -1).astype(np.int32))
    return np.stack(maps)


def _dil_geometry(T):
    n16 = T // NSLAB
    geo = []
    for d in DILATIONS:
        e = NSLAB // d
        rows = BLK // e
        nblk = n16 // rows
        geo.append((d, e, rows, nblk))
    return geo


DIL_INTERLEAVE_FWD = {1: 4, 4: 8, 16: 8}
DIL_INTERLEAVE_BWD = {1: 8, 4: 8, 16: 8}


def _dil_interleave(per_step, nblocks):
    while per_step > 1 and (nblocks % per_step or nblocks // per_step < 2):
        per_step -= 1
    return per_step


def _dil_bias(tab_ref, bkt_ref, bias_s, h):
    for p in range(len(DILATIONS)):
        bk = bkt_ref[p]
        bias = jnp.full((BLK, 2 * BLK), NEG, F32)
        for b in range(NUM_BUCKETS):
            bias = jnp.where(bk == b, tab_ref[b, h], bias)
        bias_s[p] = bias


def _dil_rows(d, e, rows, sub, blk):
    start = pl.multiple_of(blk * rows, rows)
    return [(sub + d * j, pl.ds(start, rows)) for j in range(e)]


def _to_slabs(pairs, n16):
    for r in range(NSLAB):
        for nat_ref, slab_ref in pairs:
            slab_ref[r] = nat_ref[pl.ds(r, n16, stride=NSLAB), :]


def _from_slabs(pairs, n16):
    for r in range(NSLAB):
        for slab_ref, nat_ref in pairs:
            nat_ref[pl.ds(r, n16, stride=NSLAB), :] = slab_ref[r]


def _gather(ref, idx):
    return jnp.concatenate([ref[s, r, :] for s, r in idx], axis=0)


def _scatter(ref, idx, val, rows):
    for j, (s, r) in enumerate(idx):
        ref[s, r, :] = val[j * rows:(j + 1) * rows]


def _scatter_add(ref, idx, val, rows):
    for j, (s, r) in enumerate(idx):
        ref[s, r, :] += val[j * rows:(j + 1) * rows]


def _dil_fwd(qkv, table, name, side=None):
    T = qkv.shape[0]
    n16 = T // NSLAB
    H = qkv.shape[1] // (3 * HEAD)
    geo = _dil_geometry(T)
    bkt = jnp.asarray(_bucket_maps())

    def body(tab_ref, bkt_ref, qn_ref, kn_ref, vn_ref, o_ref, lse_ref, onat_ref, bias_s, m_s, l_s, q_ref, k_ref, v_ref):
        h = pl.program_id(0)
        _dil_bias(tab_ref, bkt_ref, bias_s, h)
        first_mask = lax.broadcasted_iota(jnp.int32, (BLK, 2 * BLK), 1) < BLK
        _to_slabs(((qn_ref, q_ref), (kn_ref, k_ref), (vn_ref, v_ref)), n16)

        starts = len(DILATIONS) - 1

        def load(p, d, e, rows, sub, blk):
            cur = _dil_rows(d, e, rows, sub, blk)
            prev = _dil_rows(d, e, rows, sub, jnp.maximum(blk - 1, 0))
            q = _gather(q_ref, cur).astype(BF16)
            kk = jnp.concatenate([_gather(k_ref, prev), _gather(k_ref, cur)], axis=0).astype(BF16)
            vv = jnp.concatenate([_gather(v_ref, prev), _gather(v_ref, cur)], axis=0).astype(BF16)
            old = None if p == starts else (_gather(m_s, cur), _gather(l_s, cur), _gather(o_ref, cur))
            return cur, blk, q, kk, vv, old

        def compute(p, blk, q, kk, vv, old):
            s = _nt(q, kk) * SCALE + bias_s[p]
            s = jnp.where(first_mask & (blk == 0), NEG, s)
            m_blk = jnp.max(s, axis=-1, keepdims=True)
            if old is None:
                m_new = m_blk
                pr = jnp.exp(s - m_new)
                l_new = jnp.sum(pr, axis=-1, keepdims=True)
                acc = _nn(pr.astype(BF16), vv)
            else:
                m_old, l_old, acc_old = old
                m_new = jnp.maximum(m_old, m_blk)
                alpha = jnp.exp(m_old - m_new)
                pr = jnp.exp(s - m_new)
                l_new = alpha * l_old + jnp.sum(pr, axis=-1, keepdims=True)
                acc = alpha * acc_old + _nn(pr.astype(BF16), vv)
            if p == 0:
                return acc / l_new, m_new + jnp.log(l_new), None
            return acc, m_new, l_new

        def store(p, rows, cur, acc, m_new, l_new):
            _scatter(o_ref, cur, acc, rows)
            if p == 0:
                _scatter(lse_ref, cur, m_new, rows)
            else:
                _scatter(m_s, cur, m_new, rows)
                _scatter(l_s, cur, l_new, rows)

        for p in reversed(range(len(DILATIONS))):
            d, e, rows, nblk = geo[p]
            per_step = _dil_interleave(DIL_INTERLEAVE_FWD[d], d * nblk)

            def step(i, carry, p=p, d=d, e=e, rows=rows, nblk=nblk, per_step=per_step):
                ids = [i + u * (d * nblk // per_step) for u in range(per_step)]
                loaded = [load(p, d, e, rows, j // nblk, j % nblk) for j in ids]
                done = [(cur, compute(p, blk, q, kk, vv, old)) for cur, blk, q, kk, vv, old in loaded]
                for cur, res in done:
                    store(p, rows, cur, *res)
                return carry

            lax.fori_loop(0, d * nblk // per_step, step, 0)

        _from_slabs(((o_ref, onat_ref),), n16)

    head = lambda off: pl.BlockSpec((NSLAB, n16, HEAD), lambda h: (0, 0, off + h))
    nat = lambda off: pl.BlockSpec((T, HEAD), lambda h: (0, off + h))
    slabs = pltpu.VMEM((NSLAB, n16, HEAD), F32)
    return _call(
        body, name=name, grid=(H,), side=side,
        in_specs=[pl.BlockSpec(memory_space=pltpu.SMEM), pl.BlockSpec((3, BLK, 2 * BLK), lambda h: (0, 0, 0)),
                  nat(0), nat(H), nat(2 * H)],
        out_specs=[head(0), pl.BlockSpec((None, NSLAB, n16, 1), lambda h: (h, 0, 0, 0)), nat(0)],
        out_shape=[_sds((NSLAB, n16, H * HEAD), F32), _sds((H, NSLAB, n16, 1), F32), _sds((T, H * HEAD), F32)],
        scratch_shapes=[pltpu.VMEM((3, BLK, 2 * BLK), F32), pltpu.VMEM((NSLAB, n16, 1), F32),
                        pltpu.VMEM((NSLAB, n16, 1), F32), slabs, slabs, slabs],
        args=(table, bkt, qkv, qkv, qkv))


def _dil_bwd(qkv, table, out, dout, lse, name, side=None):
    T = qkv.shape[0]
    n16 = T // NSLAB
    H = qkv.shape[1] // (3 * HEAD)
    geo = _dil_geometry(T)
    bkt = jnp.asarray(_bucket_maps())

    def body(tab_ref, bkt_ref, qn_ref, kn_ref, vn_ref, o_ref, dnat_ref, lse_ref,
             dqn_ref, dkn_ref, dvn_ref, dtab_ref, bias_s, dbias_s, delta_s, do_ref, q_ref, k_ref, v_ref,
             dq_ref, dk_ref, dv_ref):
        h = pl.program_id(0)
        _dil_bias(tab_ref, bkt_ref, bias_s, h)
        first_mask = lax.broadcasted_iota(jnp.int32, (BLK, 2 * BLK), 1) < BLK
        dbias_s[...] = jnp.zeros_like(dbias_s)
        dq_ref[...] = jnp.zeros_like(dq_ref)
        dk_ref[...] = jnp.zeros_like(dk_ref)
        dv_ref[...] = jnp.zeros_like(dv_ref)
        _to_slabs(((qn_ref, q_ref), (kn_ref, k_ref), (vn_ref, v_ref), (dnat_ref, do_ref)), n16)
        for r in range(NSLAB):
            delta_s[r] = jnp.sum(do_ref[r] * o_ref[r], axis=-1, keepdims=True)

        def load(d, e, rows, sub, blk):
            cur = _dil_rows(d, e, rows, sub, blk)
            prev = _dil_rows(d, e, rows, sub, jnp.maximum(blk - 1, 0))
            q = _gather(q_ref, cur).astype(BF16)
            kk = jnp.concatenate([_gather(k_ref, prev), _gather(k_ref, cur)], axis=0).astype(BF16)
            vv = jnp.concatenate([_gather(v_ref, prev), _gather(v_ref, cur)], axis=0).astype(BF16)
            dob = _gather(do_ref, cur).astype(BF16)
            return cur, prev, blk, q, kk, vv, dob, _gather(lse_ref, cur), _gather(delta_s, cur)

        def compute(p, blk, q, kk, vv, dob, lse, delta):
            s = _nt(q, kk) * SCALE + bias_s[p]
            s = jnp.where(first_mask & (blk == 0), NEG, s)
            pr = jnp.exp(s - lse)
            ds = pr * (_nt(dob, vv) - delta)
            dsb = ds.astype(BF16)
            return ds, _nn(dsb, kk) * SCALE, _tn(dsb, q) * SCALE, _tn(pr.astype(BF16), dob)

        def store(rows, cur, prev, dq, dkk, dvv):
            _scatter_add(dq_ref, cur, dq, rows)
            _scatter_add(dk_ref, prev, dkk[:BLK], rows)
            _scatter_add(dk_ref, cur, dkk[BLK:], rows)
            _scatter_add(dv_ref, prev, dvv[:BLK], rows)
            _scatter_add(dv_ref, cur, dvv[BLK:], rows)

        for p in range(len(DILATIONS)):
            d, e, rows, nblk = geo[p]
            per_step = _dil_interleave(DIL_INTERLEAVE_BWD[d], d * nblk)

            def step(i, carry, p=p, d=d, e=e, rows=rows, nblk=nblk, per_step=per_step):
                ids = [i + u * (d * nblk // per_step) for u in range(per_step)]
                loaded = [load(d, e, rows, j // nblk, j % nblk) for j in ids]
                done = [(cur, prev, compute(p, *rest)) for cur, prev, *rest in loaded]
                dbias_s[p] += functools.reduce(jnp.add, [res[0] for _, _, res in done])
                for cur, prev, res in done:
                    store(rows, cur, prev, *res[1:])
                return carry

            lax.fori_loop(0, d * nblk // per_step, step, 0)

        lane = lax.broadcasted_iota(jnp.int32, (1, HEAD), 1)
        row = jnp.zeros((1, HEAD), F32)
        for b in range(NUM_BUCKETS):
            tot = jnp.zeros((1, 1), F32)
            for p in range(len(DILATIONS)):
                hit = jnp.where(bkt_ref[p] == b, dbias_s[p], 0.0)
                tot = tot + jnp.sum(jnp.sum(hit, axis=0, keepdims=True), axis=1, keepdims=True)
            row = jnp.where(lane == b, tot, row)
        dtab_ref[...] = row
        _from_slabs(((dq_ref, dqn_ref), (dk_ref, dkn_ref), (dv_ref, dvn_ref)), n16)

    head = lambda off: pl.BlockSpec((NSLAB, n16, HEAD), lambda h: (0, 0, off + h))
    nat = lambda off: pl.BlockSpec((T, HEAD), lambda h: (0, off + h))
    slabs = pltpu.VMEM((NSLAB, n16, HEAD), F32)
    return _call(
        body, name=name, grid=(H,), side=side,
        in_specs=[pl.BlockSpec(memory_space=pltpu.SMEM), pl.BlockSpec((3, BLK, 2 * BLK), lambda h: (0, 0, 0)),
                  nat(0), nat(H), nat(2 * H), head(0), nat(0),
                  pl.BlockSpec((None, NSLAB, n16, 1), lambda h: (h, 0, 0, 0))],
        out_specs=[nat(0), nat(0), nat(0), pl.BlockSpec((None, 1, HEAD), lambda h: (h, 0, 0))],
        out_shape=[_sds((T, H * HEAD), F32)] * 3 + [_sds((H, 1, HEAD), F32)],
        scratch_shapes=[pltpu.VMEM((3, BLK, 2 * BLK), F32), pltpu.VMEM((3, BLK, 2 * BLK), F32),
                        pltpu.VMEM((NSLAB, n16, 1), F32)] + [slabs] * 7,
        args=(table, bkt, qkv, qkv, qkv, out, dout, lse))


def _qknorm_bwd(raw, dq, dk, dv, gains, tm, name):
    T, N = raw.shape
    C = N // 3

    def body(raw_ref, dq_ref, dk_ref, dv_ref, gains_ref, dp_ref, dg_ref):
        @pl.when(pl.program_id(0) == 0)
        def _():
            dg_ref[...] = jnp.zeros_like(dg_ref)

        for t, d_ref in enumerate((dq_ref, dk_ref)):
            gain = gains_ref[t]
            dgain = jnp.zeros((1, HEAD), F32)
            for k in range(C // HEAD):
                y = raw_ref[:, t * C + k * HEAD:t * C + (k + 1) * HEAD]
                dn = d_ref[:, k * HEAD:(k + 1) * HEAD]
                r = lax.rsqrt(jnp.mean(y * y, axis=-1, keepdims=True) + RMS_EPS)
                yhat = y * r
                gd = dn * gain
                dy = r * (gd - yhat * jnp.mean(gd * yhat, axis=-1, keepdims=True))
                dp_ref[:, t * C + k * HEAD:t * C + (k + 1) * HEAD] = dy.astype(BF16)
                dgain = dgain + jnp.sum(dn * yhat, axis=0, keepdims=True)
            dg_ref[t] += dgain
        dp_ref[:, 2 * C:] = dv_ref[...].astype(BF16)

    third = pl.BlockSpec((tm, C), lambda i: (i, 0))
    return pl.pallas_call(
        body, name=name, grid=(T // tm,),
        in_specs=[pl.BlockSpec((tm, N), lambda i: (i, 0)), third, third, third,
                  pl.BlockSpec((2, 1, HEAD), lambda i: (0, 0, 0))],
        out_specs=[pl.BlockSpec((tm, N), lambda i: (i, 0)), pl.BlockSpec((2, 1, HEAD), lambda i: (0, 0, 0))],
        out_shape=[_sds((T, N), BF16), _sds((2, 1, HEAD), F32)], compiler_params=_cparams(),
    )(raw, dq, dk, dv, gains)


def _pad_lanes(v, width=HEAD):
    return jnp.pad(v, ((0, 0), (0, width - v.shape[1])))


def _local_step(x, target, small, wts, plan=None):
    grads = {}

    def hosted(host, fn, *args, **kw):
        side = plan.before(host, wts, grads) if plan is not None else None
        if side is None:
            return fn(*args, name=host, **kw)
        res, side_res = fn(*args, name=host, side=side, **kw)
        plan.after(host, side_res, wts, grads)
        return res

    T, D = x.shape
    C = D // 2
    H = C // HEAD
    n16 = T // NSLAB
    tm = min(512, T)
    tmm = min(1024, T)
    tms = 4 * n16
    tq = min(512, T)
    bn = min(1024, D)
    g1, gm, g2 = small["ffn1_norm"], small["mix_norm"], small["ffn2_norm"]
    gains_a = jnp.stack([small["q_norm_a"], small["k_norm_a"], jnp.ones_like(small["q_norm_a"])])
    gains_b = jnp.stack([small["q_norm_b"], small["k_norm_b"], jnp.ones_like(small["q_norm_b"])])
    fbias = _pad_lanes(small["forget_bias"])
    table = small["rel_bias_table"]

    h1, gu1, act1 = hosted("ffn1_up", _ffn_fwd, x, g1, wts["ffn1_in"], None, tmm)
    x1 = hosted("ffn1_down", _ffn_down, x, act1, wts["ffn1_out"], tmm)
    w_in_t, w_f_t, w_o = wts["w_in_t"], wts["w_f_t"], wts["w_o"]
    raw_a, nrm_a, h2a = _proj(x1, gm, w_in_t, gains_a, (True, True, False), tn=C, w_off=0, slabs=False, tm=tmm,
                              normed_dtype=F32, name="proj_a")
    raw_b, nrm_b, h2b, f_raw = _proj(x1, gm, w_in_t, gains_b, (True, True, False), tn=C, w_off=3, slabs=False, tm=tmm,
                                     normed_dtype=BF16, name="proj_b", small_wt=w_f_t)
    c = _fox_gate_fwd(f_raw, fbias, "fox_gate_fwd")
    c_heads = c[:, :H].T
    c_col, c_row = c_heads[:, :, None], c_heads[:, None, :]
    out_a, lse_a, out_a_nat = hosted("dil_fwd", _dil_fwd, nrm_a, table)
    out_b, lse_b = hosted("fox_fwd", _fox_fwd, nrm_b, c_col, c_row, tq)
    x2a = _mm(out_a_nat, w_o, nt=False, tk=C, tm=tmm, a_layout="nat", out_layout="nat", resid=x1, name="out_a")
    x2 = _mm(out_b, w_o, nt=False, tk=C, tm=tmm, a_layout="nat", out_layout="nat", resid=x2a, w_off=1, name="out_b")
    dy, h3, gu3, act3, loss_row = _ffn_fwd(x2, g2, wts["ffn2_in"], wts["ffn2_out"], tm, "ffn2_fwd", target=target)

    def ffn_backward(tag, xin, g, h, gu, act, win, wout, dres):
        nc, tf = wout.shape[0], wout.shape[1]
        dh, dgu, dyb = hosted(tag + "_bwd", _ffn_bwd, dres, gu, win, wout, tm)
        grads[tag + "_w_in_t"], grads[tag + "_w_in_t_bf16"] = hosted(
            tag + "_dwin", _mm_tn, dgu.reshape(2 * nc, tf, T), h, bm=tf, bn=bn, bt=T, a_rows=True, twin=True)
        dxin, grads[tag + "_norm"] = hosted(tag + "_rms_bwd", _rms_bwd, xin, g, dh, dres, tm)
        grads[tag + "_w_out"], grads[tag + "_w_out_bf16"] = hosted(
            tag + "_dwout", _mm_tn, act, dyb, bm=tf, bn=bn, bt=T, a_rows=True, twin=True)
        return dxin

    dx2 = ffn_backward("ffn2", x2, g2, h3, gu3, act3, wts["ffn2_in"], wts["ffn2_out"], dy)

    dmix = _mm(dx2, w_o, nt=True, tk=D, tm=tmm, a_layout="nat", out_layout="nat", name="dmix")
    dwo = _mm_tn(out_a_nat.reshape(1, T, C), dx2, bm=C, bn=bn, bt=tm, rows=2 * C, name="dwo_a")
    dwo = _mm_tn(out_b.reshape(1, T, C), dx2, bm=C, bn=bn, bt=tm, rows=2 * C, m_off=1, into=dwo, name="dwo_b")
    grads["w_out"] = dwo[0]

    dqa, dka, dva, dtab = hosted("dil_bwd", _dil_bwd, nrm_a, table, out_a, dmix, lse_a)
    dqb, dkb, dvb, dck, dcq = hosted("fox_bwd", _fox_bwd, nrm_b, c_col, c_row, out_b, dmix, lse_b, tq, do_off=H)
    grads["rel_bias_table"] = dtab[:, 0, :NUM_BUCKETS].T
    dc = _pad_lanes((dck[:, 0, :] + dcq[:, :, 0]).T)
    df, dfb = _fox_gate_bwd(f_raw, fbias, dc, "fox_gate_bwd")
    grads["forget_bias"] = dfb[:, :H]

    flat = lambda a: a.reshape(T, a.shape[-1])
    dproj_a, dgain_a = _qknorm_bwd(flat(raw_a), flat(dqa), flat(dka), flat(dva), gains_a[:2], min(256, T), "qknorm_bwd_a")
    dproj_b, dgain_b = _qknorm_bwd(raw_b, dqb, dkb, dvb, gains_b[:2], min(256, T), "qknorm_bwd_b")
    grads["q_norm_a"], grads["k_norm_a"] = dgain_a[0], dgain_a[1]
    grads["q_norm_b"], grads["k_norm_b"] = dgain_b[0], dgain_b[1]

    dh2 = _mm(dproj_a, w_in_t, nt=False, tk=C, tm=tmm, a_layout="nat", out_layout="nat", name="dh2_a")
    dh2 = _mm(dproj_b, w_in_t, nt=False, tk=C, tm=tmm, a_layout="nat", out_layout="nat", resid=dh2, w_off=3,
              small=(df, w_f_t), name="dh2_b")
    dx1, grads["mix_norm"] = _rms_bwd(x1, gm, dh2, dx2, tm, "mix_rms_bwd")
    bt = min(2048, T)
    dwt = _mm_tn(flat(dproj_a)[None], flat(h2a), bm=C, bn=bn, bt=bt, rows=6 * C + H, name="dw_a")
    dwt = _mm_tn(dproj_b[None], h2b, bm=C, bn=bn, bt=bt, rows=6 * C + H, m_off=3, into=dwt, name="dw_b")
    dwt = _mm_tn(df[None, :, :H], h2b, bm=H, bn=bn, bt=bt, rows=6 * C + H, m_off=6 * C // H, into=dwt, name="dw_f")
    grads["w_in_t"] = dwt[0]

    grad_x = ffn_backward("ffn1", x, g1, h1, gu1, act1, wts["ffn1_in"], wts["ffn1_out"], dx1)
    return loss_row, grad_x, grads


def _place():
    x, y, c = lax.axis_index("x"), lax.axis_index("y"), lax.axis_index("c")
    other_chips = [(1 - x, y), (x, 1 - y), (1 - x, 1 - y)]
    return x, y, c, other_chips


def _run_side(side, name):
    def body(*refs):
        si, so = len(side.ins), len(side.outs)
        side.start(refs[:si], refs[si:si + so], refs[si + so:])
        side.finish(refs[:si], refs[si:si + so], refs[si + so:])

    return pl.pallas_call(body, name=name, in_specs=[ANY] * len(side.ins), out_specs=[ANY] * len(side.outs),
                          out_shape=side.outs, scratch_shapes=side.sems)(*side.ins)


def _all_gather(shards):
    n = len(shards)

    def plan(ins, outs, sems):
        send_sems, recv_sems, local_sems = sems
        x, y, c, chips = _place()
        me, sibling = (x, y, c), (x, y, 1 - c)

        def copy(a, k, block, to, src=None):
            px, py, pc = block
            dst = outs[a].at[4 * px + 2 * py + pc]
            return pltpu.make_async_remote_copy(
                src_ref=dst if src is None else src, dst_ref=dst, send_sem=send_sems.at[7 * a + k],
                recv_sem=recv_sems.at[7 * a + k], device_id=to, device_id_type=MESH)

        mine = [pltpu.make_async_copy(ins[a], outs[a].at[4 * x + 2 * y + c], local_sems.at[a]) for a in range(n)]
        first = []
        for a in range(n):
            first.append(copy(a, 0, me, sibling, src=ins[a]))
            first += [copy(a, 1 + j, me, (*chip, c), src=ins[a]) for j, chip in enumerate(chips)]
        return copy, mine, first, me, sibling, c, chips

    def start(ins, outs, sems):
        _, mine, first, *_ = plan(ins, outs, sems)
        for cp in mine + first:
            cp.start()

    def finish(ins, outs, sems):
        copy, mine, first, me, sibling, c, chips = plan(ins, outs, sems)
        passed = []
        for a in range(n):
            for j, chip in enumerate(chips):
                copy(a, 1 + j, (*chip, c), me).wait_recv()
                fwd = copy(a, 4 + j, (*chip, c), sibling)
                fwd.start()
                passed.append(fwd)
        for a in range(n):
            copy(a, 0, sibling, me).wait_recv()
            for j, chip in enumerate(chips):
                copy(a, 4 + j, (*chip, 1 - c), me).wait_recv()
        for cp in first + passed:
            cp.wait_send()
        for cp in mine:
            cp.wait()

    return _Side(shards, [_sds((N_DEV,) + s.shape, s.dtype) for s in shards],
                 [pltpu.SemaphoreType.DMA((7 * n,)), pltpu.SemaphoreType.DMA((7 * n,)), pltpu.SemaphoreType.DMA((n,))],
                 start, finish)


def _all_gather_relayed(shards):
    n = len(shards)
    halves = [-(-(s.shape[0] // 2) // BF16_TILE_ROWS) * BF16_TILE_ROWS for s in shards]

    def body_parts(ins, outs, sems):
        send_sems, recv_sems, local_sems = sems
        x, y, c, _ = _place()
        me, sib, xn, yn, dg = (x, y, c), (x, y, 1 - c), (1 - x, y, c), (x, 1 - y, c), (1 - x, 1 - y, c)

        def rows(a, block, part):
            px, py, pc = block
            whole = outs[a].at[4 * px + 2 * py + pc]
            if part is None:
                return whole
            return whole.at[pl.ds(0, halves[a])] if part == 0 else whole.at[pl.ds(halves[a], shards[a].shape[0] - halves[a])]

        def copy(a, k, block, part, to, src=None):
            dst = rows(a, block, part)
            return pltpu.make_async_remote_copy(
                src_ref=dst if src is None else src, dst_ref=dst, send_sem=send_sems.at[9 * a + k],
                recv_sem=recv_sems.at[9 * a + k], device_id=to, device_id_type=MESH)

        flip = lambda dev: (dev[0], dev[1], 1 - dev[2])
        mine = [pltpu.make_async_copy(ins[a], rows(a, me, None), local_sems.at[a]) for a in range(n)]
        own = [[copy(a, 0, me, None, sib, src=ins[a]), copy(a, 1, me, None, xn, src=ins[a]),
                copy(a, 2, me, None, yn, src=ins[a])] for a in range(n)]
        relays = lambda a: [(1, [copy(a, 3, xn, 0, yn), copy(a, 5, xn, None, sib)]),
                            (2, [copy(a, 4, yn, 1, xn), copy(a, 6, yn, None, sib)]),
                            (3, [copy(a, 7, dg, 0, sib)]), (4, [copy(a, 8, dg, 1, sib)])]
        lands = {0: (sib, None), 1: (xn, None), 2: (yn, None), 3: (dg, 0), 4: (dg, 1), 5: (flip(xn), None),
                 6: (flip(yn), None), 7: (flip(dg), 0), 8: (flip(dg), 1)}
        arrival = lambda a, k: copy(a, k, lands[k][0], lands[k][1], me)
        return mine, own, relays, arrival

    def start(ins, outs, sems):
        mine, own, _, _ = body_parts(ins, outs, sems)
        for cp in mine + [cp for per in own for cp in per]:
            cp.start()

    def finish(ins, outs, sems):
        mine, own, relays, arrival = body_parts(ins, outs, sems)
        sent = [cp for per in own for cp in per]
        relays = [relays(a) for a in range(n)]
        for stage in range(4):
            for a in range(n):
                after, passes = relays[a][stage]
                arrival(a, after).wait_recv()
                for cp in passes:
                    cp.start()
                sent += passes
        for a in range(n):
            for k in (0, 5, 6, 7, 8):
                arrival(a, k).wait_recv()
        for cp in sent:
            cp.wait_send()
        for cp in mine:
            cp.wait()

    return _Side(shards, [_sds((N_DEV,) + s.shape, s.dtype) for s in shards],
                 [pltpu.SemaphoreType.DMA((9 * n,)), pltpu.SemaphoreType.DMA((9 * n,)), pltpu.SemaphoreType.DMA((n,))],
                 start, finish)


def _exchange_in_chip(gs):
    n = len(gs)

    def copies(ins, outs, sems):
        x, y, c, _ = _place()
        return [pltpu.make_async_remote_copy(
            src_ref=ins[a].at[2 * q + 1 - c], dst_ref=outs[a].at[q], send_sem=sems[0].at[4 * a + q],
            recv_sem=sems[1].at[4 * a + q], device_id=(x, y, 1 - c), device_id_type=MESH)
            for a in range(n) for q in range(4)]

    def start(ins, outs, sems):
        for cp in copies(ins, outs, sems):
            cp.start()

    def finish(ins, outs, sems):
        for cp in copies(ins, outs, sems):
            cp.wait()

    return _Side(gs, [_sds((4,) + g.shape[1:], g.dtype) for g in gs],
                 [pltpu.SemaphoreType.DMA((4 * n,)), pltpu.SemaphoreType.DMA((4 * n,))], start, finish)


def _exchange_between_chips(ps):
    n = len(ps)

    def copies(ins, outs, sems):
        x, y, c, chips = _place()
        return [pltpu.make_async_remote_copy(
            src_ref=ins[a].at[2 * cx + cy], dst_ref=outs[a].at[j], send_sem=sems[0].at[3 * a + j],
            recv_sem=sems[1].at[3 * a + j], device_id=(cx, cy, c), device_id_type=MESH)
            for a in range(n) for j, (cx, cy) in enumerate(chips)]

    def start(ins, outs, sems):
        for cp in copies(ins, outs, sems):
            cp.start()

    def finish(ins, outs, sems):
        for cp in copies(ins, outs, sems):
            cp.wait()

    return _Side(ps, [_sds((3,) + p.shape[1:], p.dtype) for p in ps],
                 [pltpu.SemaphoreType.DMA((3 * n,)), pltpu.SemaphoreType.DMA((3 * n,))], start, finish)


def _all_reduce_small(v, name):
    R = v.shape[0]

    def body(v_ref, sum_ref, all_ref, send_sems, recv_sems):
        x, y, c, _ = _place()
        k = 4 * x + 2 * y + c
        all_ref[k] = v_ref[...]
        copies = []
        for rel in range(1, N_DEV):
            fx, fy, fc = (rel >> 2) & 1, (rel >> 1) & 1, rel & 1
            peer = (1 - x if fx else x, 1 - y if fy else y, 1 - c if fc else c)
            copies.append(pltpu.make_async_remote_copy(
                src_ref=v_ref, dst_ref=all_ref.at[k], send_sem=send_sems.at[rel - 1], recv_sem=recv_sems.at[rel - 1],
                device_id=peer, device_id_type=MESH))
        for cp in copies:
            cp.start()
        for rel in range(1, N_DEV):
            fx, fy, fc = (rel >> 2) & 1, (rel >> 1) & 1, rel & 1
            src = 4 * (1 - x if fx else x) + 2 * (1 - y if fy else y) + (1 - c if fc else c)
            pltpu.make_async_remote_copy(
                src_ref=v_ref, dst_ref=all_ref.at[src], send_sem=send_sems.at[rel - 1], recv_sem=recv_sems.at[rel - 1],
                device_id=(x, y, c), device_id_type=MESH).wait_recv()
        for cp in copies:
            cp.wait_send()
        tot = all_ref[0]
        for d in range(1, N_DEV):
            tot = tot + all_ref[d]
        sum_ref[...] = tot

    vm = pl.BlockSpec(memory_space=pltpu.VMEM)
    return pl.pallas_call(
        body, name=name, in_specs=[vm], out_specs=[vm, vm],
        out_shape=[_sds((R, HEAD), F32), _sds((N_DEV, R, HEAD), F32)],
        scratch_shapes=[pltpu.SemaphoreType.DMA((N_DEV - 1,)), pltpu.SemaphoreType.DMA((N_DEV - 1,))],
    )(v)[0]


def _tiles(rows, cols):
    tc = 512 if cols % 512 == 0 else cols
    tr = rows
    while tr * tc * 4 > ELEMENTWISE_BLOCK_BYTES and tr % (2 * BF16_TILE_ROWS) == 0:
        tr //= 2
    return tr, tc


def _chip_sum(g, r1, place, name):
    _, R, Cc = g.shape
    tr, tc = _tiles(R, Cc)

    def body(place_ref, g_ref, r_ref, p_ref):
        p_ref[...] = (g_ref[...] + r_ref[...]).astype(BF16)

    chip = lambda k, place: k + (k >= place[1]).astype(jnp.int32)
    blk = lambda f: pl.BlockSpec((None, tr, tc), f)
    return pl.pallas_call(
        body, name=name,
        grid_spec=pltpu.PrefetchScalarGridSpec(
            num_scalar_prefetch=1, grid=(3, R // tr, Cc // tc),
            in_specs=[blk(lambda k, i, j, place: (2 * chip(k, place) + place[0], i, j)),
                      blk(lambda k, i, j, place: (chip(k, place), i, j))],
            out_specs=blk(lambda k, i, j, place: (chip(k, place), i, j))),
        out_shape=_sds((4, R, Cc), BF16), compiler_params=_cparams(),
    )(place, g, r1)


def _adamw_update(gv, w_ref, m_ref, v_ref, d_ref, nm_ref, nv_ref):
    nm = B1 * m_ref[...] + (1.0 - B1) * gv
    nv = B2 * v_ref[...] + (1.0 - B2) * jnp.square(gv)
    m_hat = nm / (1.0 - B1 ** STEP)
    v_hat = nv / (1.0 - B2 ** STEP)
    d_ref[...] = -LR * (m_hat / (jnp.sqrt(v_hat) + EPS) + WD * w_ref[...])
    nm_ref[...] = nm
    nv_ref[...] = nv


def _reduce_adamw(g, r1, r2, where, w, m, v, name):
    _, R, Cc = g.shape
    tr, tc = _tiles(R, Cc)

    def body(where_ref, g_ref, r1_ref, r2_ref, w_ref, m_ref, v_ref, o_ref, d_ref, nm_ref, nv_ref):
        gv = ((g_ref[...] + r1_ref[...]) + r2_ref[0].astype(F32)) + (r2_ref[1].astype(F32) + r2_ref[2].astype(F32))
        o_ref[...] = gv
        _adamw_update(gv, w_ref, m_ref, v_ref, d_ref, nm_ref, nv_ref)

    blk = pl.BlockSpec((tr, tc), lambda i, j, w: (i, j))
    return pl.pallas_call(
        body, name=name,
        grid_spec=pltpu.PrefetchScalarGridSpec(
            num_scalar_prefetch=1, grid=(R // tr, Cc // tc),
            in_specs=[pl.BlockSpec((None, tr, tc), lambda i, j, w: (w[0], i, j)),
                      pl.BlockSpec((None, tr, tc), lambda i, j, w: (w[1], i, j)),
                      pl.BlockSpec((3, tr, tc), lambda i, j, w: (0, i, j)), blk, blk, blk],
            out_specs=[blk] * 4),
        out_shape=[_sds((R, Cc), F32)] * 4, compiler_params=_cparams(),
    )(where, g, r1, r2, w, m, v)


def _adamw(w, g, m, v, name):
    R, Cc = w.shape
    tr, tc = _tiles(R, Cc)

    def body(w_ref, g_ref, m_ref, v_ref, d_ref, nm_ref, nv_ref):
        _adamw_update(g_ref[...], w_ref, m_ref, v_ref, d_ref, nm_ref, nv_ref)

    blk = pl.BlockSpec((tr, tc), lambda i, j: (i, j))
    return pl.pallas_call(
        body, name=name, grid=(R // tr, Cc // tc), in_specs=[blk] * 4, out_specs=[blk] * 3,
        out_shape=[_sds((R, Cc), F32)] * 3, compiler_params=_cparams(),
    )(w, g, m, v)


SMALL = ("ffn1_norm", "mix_norm", "ffn2_norm", "q_norm_a", "k_norm_a", "q_norm_b", "k_norm_b", "forget_bias",
         "rel_bias_table")
LARGE = ("ffn1_w_in", "ffn1_w_out", "w_in", "w_out", "ffn2_w_in", "ffn2_w_out")
ORDER = ("ffn1_norm", "ffn1_w_in", "ffn1_w_out", "mix_norm", "w_in", "q_norm_a", "k_norm_a", "q_norm_b", "k_norm_b",
         "forget_bias", "rel_bias_table", "w_out", "ffn2_norm", "ffn2_w_in", "ffn2_w_out")


def _pack_small(vals):
    rows = []
    for name in SMALL:
        flat = vals[name].reshape(-1)
        pad = (-flat.shape[0]) % HEAD
        rows.append(jnp.pad(flat, (0, pad)).reshape(-1, HEAD))
    return jnp.concatenate(rows, axis=0)


def _unpack_small(packed, like):
    out, r = {}, 0
    for name in SMALL:
        size = like[name].size
        nrow = -(-size // HEAD)
        out[name] = packed[r:r + nrow].reshape(-1)[:size].reshape(like[name].shape)
        r += nrow
    return out


def kernel(x, ffn1_norm, ffn1_w_in, ffn1_w_out, mix_norm, w_in, q_norm_a, k_norm_a, q_norm_b, k_norm_b, forget_bias, rel_bias_table, w_out, ffn2_norm, ffn2_w_in, ffn2_w_out, loss_target, m_ffn1_norm, m_ffn1_w_in, m_ffn1_w_out, m_mix_norm, m_w_in, m_q_norm_a, m_k_norm_a, m_q_norm_b, m_k_norm_b, m_forget_bias, m_rel_bias_table, m_w_out, m_ffn2_norm, m_ffn2_w_in, m_ffn2_w_out, v_ffn1_norm, v_ffn1_w_in, v_ffn1_w_out, v_mix_norm, v_w_in, v_q_norm_a, v_k_norm_a, v_q_norm_b, v_k_norm_b, v_forget_bias, v_rel_bias_table, v_w_out, v_ffn2_norm, v_ffn2_w_in, v_ffn2_w_out):
    w = dict(ffn1_norm=ffn1_norm, ffn1_w_in=ffn1_w_in, ffn1_w_out=ffn1_w_out, mix_norm=mix_norm, w_in=w_in,
             q_norm_a=q_norm_a, k_norm_a=k_norm_a, q_norm_b=q_norm_b, k_norm_b=k_norm_b, forget_bias=forget_bias,
             rel_bias_table=rel_bias_table, w_out=w_out, ffn2_norm=ffn2_norm, ffn2_w_in=ffn2_w_in, ffn2_w_out=ffn2_w_out)
    m = dict(ffn1_norm=m_ffn1_norm, ffn1_w_in=m_ffn1_w_in, ffn1_w_out=m_ffn1_w_out, mix_norm=m_mix_norm, w_in=m_w_in,
             q_norm_a=m_q_norm_a, k_norm_a=m_k_norm_a, q_norm_b=m_q_norm_b, k_norm_b=m_k_norm_b,
             forget_bias=m_forget_bias, rel_bias_table=m_rel_bias_table, w_out=m_w_out, ffn2_norm=m_ffn2_norm,
             ffn2_w_in=m_ffn2_w_in, ffn2_w_out=m_ffn2_w_out)
    v = dict(ffn1_norm=v_ffn1_norm, ffn1_w_in=v_ffn1_w_in, ffn1_w_out=v_ffn1_w_out, mix_norm=v_mix_norm, w_in=v_w_in,
             q_norm_a=v_q_norm_a, k_norm_a=v_k_norm_a, q_norm_b=v_q_norm_b, k_norm_b=v_k_norm_b,
             forget_bias=v_forget_bias, rel_bias_table=v_rel_bias_table, w_out=v_w_out, ffn2_norm=v_ffn2_norm,
             ffn2_w_in=v_ffn2_w_in, ffn2_w_out=v_ffn2_w_out)
    T, D = x.shape[1], x.shape[2]
    C = D // 2
    H = C // HEAD
    ff_shard = ffn1_w_out.shape[1]

    f1i, = _run_side(_all_gather_relayed([ffn1_w_in[0].T.astype(BF16)]), "gather_ffn1")
    wts = dict(ffn1_in=f1i.reshape(2, N_DEV, ff_shard, D))
    xi, yi, ci = lax.axis_index("x"), lax.axis_index("y"), lax.axis_index("c")
    place = jnp.stack([ci, 2 * xi + yi]).astype(jnp.int32)
    where = jnp.stack([4 * xi + 2 * yi + ci, 2 * xi + yi]).astype(jnp.int32)
    gs, r1, ps, r2 = {}, {}, {}, {}

    def shard(name):
        name, _, part = name.partition(":")
        s = (w[name][0].T if name.endswith("w_in") else w[name][0]).astype(BF16)
        first = -(-(s.shape[0] // 2) // BF16_TILE_ROWS) * BF16_TILE_ROWS
        return {"": s, "first": s[:first], "rest": s[first:]}[part]

    def by_destination(name, grads):
        key = name + "_t" if name.endswith("w_in") else name
        gs[name] = grads[key].reshape(N_DEV, -1, D)
        return grads.get(key + "_bf16", grads[key]).reshape(N_DEV, -1, D)

    def chip_sums(names):
        for name in names:
            ps[name] = _chip_sum(gs[name], r1[name], place, "chip_sum_" + name)
        return [ps[name] for name in names]

    class Plan:
        carried = {"ffn1_up": ("gather", ("ffn1_w_out", "w_in:first")), "ffn1_down": ("gather", ("w_in:rest", "w_out")),
                   "dil_fwd": ("gather", ("ffn2_w_out",)), "fox_fwd": ("gather", ("ffn2_w_in",)),
                   "dil_bwd": ("in_chip", ("ffn2_w_in", "ffn2_w_out")), "fox_bwd": ("between", ("ffn2_w_in", "ffn2_w_out")),
                   "ffn1_bwd": ("in_chip", ("w_in", "w_out")), "ffn1_dwin": ("between", ("w_in", "w_out")),
                   "ffn1_rms_bwd": ("in_chip", ("ffn1_w_in",)), "ffn1_dwout": ("between", ("ffn1_w_in",))}

        def before(self, host, wts, grads):
            if host not in self.carried:
                return None
            kind, names = self.carried[host]
            if kind == "gather":
                return _all_gather([shard(n) for n in names])
            if kind == "in_chip":
                return _exchange_in_chip([by_destination(n, grads) for n in names])
            return _exchange_between_chips(chip_sums(names))

        def after(self, host, res, wts, grads):
            kind, names = self.carried[host]
            if host == "ffn1_up":
                wts.update(ffn1_out=res[0], w_in_first_rows=res[1])
            elif host == "ffn1_down":
                w_in_t = jnp.concatenate([wts.pop("w_in_first_rows"), res[0]], axis=1).reshape(-1, D)
                wts.update(w_in_t=w_in_t, w_f_t=jnp.pad(w_in_t[6 * C:], ((0, HEAD - H), (0, 0))),
                           w_o=res[1].reshape(2 * C, D))
            elif host == "dil_fwd":
                wts.update(ffn2_out=res[0])
            elif host == "fox_fwd":
                wts.update(ffn2_in=res[0].reshape(2, N_DEV, ff_shard, D))
            else:
                (r1 if kind == "in_chip" else r2).update(zip(names, res))

    small = {name: w[name] for name in SMALL}
    loss_row, grad_x, grads = _local_step(x[0], loss_target[0], small, wts, Plan())

    tail = ("ffn1_w_out",)
    r1[tail[0]], = _run_side(_exchange_in_chip([by_destination(tail[0], grads)]), "reduce_in_chip_tail")
    r2.update(zip(tail, _run_side(_exchange_between_chips(chip_sums(tail)), "reduce_between_chips_tail")))

    packed = _pack_small(grads)
    nsmall = packed.shape[0]
    packed = jnp.concatenate([packed, loss_row, jnp.zeros(((-nsmall - 1) % 8, HEAD), F32)], axis=0)
    reduced = _all_reduce_small(packed, "reduce_small")
    loss = reduced[nsmall, 0]
    g_small = _unpack_small(reduced[:nsmall], small)

    grad, delta, new_m, new_v = dict(g_small), {}, {}, {}
    for name in LARGE:
        to = (lambda t: t[0].T) if name.endswith("w_in") else (lambda t: t[0])
        back = (lambda t: t.T[None]) if name.endswith("w_in") else (lambda t: t[None])
        res = _reduce_adamw(gs[name], r1[name], r2[name], where, to(w[name]), to(m[name]), to(v[name]), "adamw_" + name)
        grad[name], delta[name], new_m[name], new_v[name] = (back(t) for t in res)
    d, nm, nv = _adamw(_pack_small(w), reduced[:nsmall], _pack_small(m), _pack_small(v), "adamw_small")
    delta.update(_unpack_small(d, small))
    new_m.update(_unpack_small(nm, small))
    new_v.update(_unpack_small(nv, small))
    return (loss, grad_x[None], *[grad[n] for n in ORDER], *[delta[n] for n in ORDER],
            *[new_m[n] for n in ORDER], *[new_v[n] for n in ORDER])
```

```python
import functools
import math

import numpy as np
import jax
import jax.numpy as jnp
from jax import lax
from jax.experimental import pallas as pl
from jax.experimental.pallas import tpu as pltpu

F32, BF16 = jnp.float32, jnp.bfloat16
HEAD = 128
NSLAB = 16
BLK = 128
DILATIONS = (1, 4, 16)
NUM_BUCKETS, MAX_DISTANCE = 32, 2048
RMS_EPS = 1e-6
NEG = -1e30
SCALE = HEAD ** -0.5
LR, B1, B2, EPS, WD, STEP = 0.001, 0.9, 0.999, 1e-08, 0.01, 10
N_DEV = 8
VMEM_LIMIT_BYTES = 56 << 20
ELEMENTWISE_BLOCK_BYTES = 2 << 20
BF16_TILE_ROWS = 16
MESH = pl.DeviceIdType.MESH


def _cparams(**kw):
    return pltpu.CompilerParams(vmem_limit_bytes=VMEM_LIMIT_BYTES, **kw)


def _nn(a, b):
    return jnp.dot(a, b, preferred_element_type=F32)


def _nt(a, b):
    return lax.dot_general(a, b, (((1,), (1,)), ((), ())), preferred_element_type=F32)


def _tn(a, b):
    return lax.dot_general(a, b, (((0,), (0,)), ((), ())), preferred_element_type=F32)


def _sds(shape, dtype):
    return jax.ShapeDtypeStruct(shape, dtype)


ANY = pl.BlockSpec(memory_space=pl.ANY)


class _Side:
    def __init__(self, ins, outs, sems, start, finish):
        self.ins, self.outs, self.sems, self.start, self.finish = list(ins), list(outs), list(sems), start, finish


def _call(body, *, name, grid, in_specs, out_specs, out_shape, args, scratch_shapes=(), side=None):
    in_specs, out_specs, out_shape = list(in_specs), list(out_specs), list(out_shape)
    scratch_shapes = list(scratch_shapes)
    if side is None:
        return pl.pallas_call(body, name=name, grid=grid, in_specs=in_specs, out_specs=out_specs, out_shape=out_shape,
                              scratch_shapes=scratch_shapes, compiler_params=_cparams())(*args)
    ni, no, ns = len(args), len(out_shape), len(scratch_shapes)
    si, so = len(side.ins), len(side.outs)

    def fused(*refs):
        h_in, s_in = refs[:ni], refs[ni:ni + si]
        h_out, s_out = refs[ni + si:ni + si + no], refs[ni + si + no:ni + si + no + so]
        h_scr, s_sem = refs[ni + si + no + so:ni + si + no + so + ns], refs[ni + si + no + so + ns:]
        ids = [pl.program_id(k) for k in range(len(grid))]
        first = functools.reduce(jnp.logical_and, [i == 0 for i in ids])
        last = functools.reduce(jnp.logical_and, [i == n - 1 for i, n in zip(ids, grid)])

        @pl.when(first)
        def _():
            side.start(s_in, s_out, s_sem)

        body(*h_in, *h_out, *h_scr)

        @pl.when(last)
        def _():
            side.finish(s_in, s_out, s_sem)

    res = pl.pallas_call(
        fused, name=name, grid=grid, in_specs=in_specs + [ANY] * si, out_specs=out_specs + [ANY] * so,
        out_shape=out_shape + side.outs, scratch_shapes=scratch_shapes + side.sems, compiler_params=_cparams(),
    )(*args, *side.ins)
    return list(res[:no]), list(res[no:])


def _ffn_fwd(x, g, win, wout, tm, name, side=None, target=None):
    T, D = x.shape
    nc, tf = win.shape[1], win.shape[2]
    down, loss = wout is not None, target is not None
    assert down or not loss

    def body(x_ref, g_ref, win_ref, *refs):
        ins, outs = refs[:down + loss], refs[down + loss:]
        h_ref, gu_ref, act_ref = outs[down:down + 3]
        i, j = pl.program_id(0), pl.program_id(1)

        @pl.when(j == 0)
        def _():
            xv = x_ref[...]
            r = lax.rsqrt(jnp.mean(xv * xv, axis=-1, keepdims=True) + RMS_EPS)
            h_ref[...] = (xv * r * g_ref[...]).astype(BF16)
            if down:
                outs[0][...] = jnp.zeros_like(outs[0])

        if loss:
            @pl.when((i == 0) & (j == 0))
            def _():
                outs[-1][...] = jnp.zeros_like(outs[-1])

        hb = h_ref[...]
        gt = _nt(win_ref[0], hb)
        up = _nt(win_ref[1], hb)
        gu_ref[0] = gt.astype(BF16)
        gu_ref[1] = up.astype(BF16)
        act = (gt * jax.nn.sigmoid(gt) * up).astype(BF16)
        act_ref[...] = act
        if down:
            y_ref = outs[0]
            y_ref[...] += _tn(act, ins[0][...])

            @pl.when(j == nc - 1)
            def _():
                y = x_ref[...] + 0.5 * y_ref[...]
                if not loss:
                    y_ref[...] = y
                    return
                err = y - ins[1][...]
                y_ref[...] = err * (1.0 / D)
                tot = 0.5 * jnp.sum(jnp.mean(err * err, axis=-1, keepdims=True), axis=0, keepdims=True)
                lane = lax.broadcasted_iota(jnp.int32, (1, HEAD), 1)
                outs[-1][...] += jnp.where(lane == 0, tot, 0.0)

    row = pl.BlockSpec((tm, D), lambda i, j: (i, 0))
    return _call(
        body, name=name, grid=(T // tm, nc), side=side,
        in_specs=[row, pl.BlockSpec((1, D), lambda i, j: (0, 0)),
                  pl.BlockSpec((2, None, tf, D), lambda i, j: (0, j, 0, 0))]
        + ([pl.BlockSpec((None, tf, D), lambda i, j: (j, 0, 0))] if down else []) + ([row] if loss else []),
        out_specs=([row] if down else [])
        + [row, pl.BlockSpec((2, None, tf, tm), lambda i, j: (0, j, 0, i)),
           pl.BlockSpec((None, tf, tm), lambda i, j: (j, 0, i))]
        + ([pl.BlockSpec((1, HEAD), lambda i, j: (0, 0))] if loss else []),
        out_shape=([_sds((T, D), F32)] if down else [])
        + [_sds((T, D), BF16), _sds((2, nc, tf, T), BF16), _sds((nc, tf, T), BF16)]
        + ([_sds((1, HEAD), F32)] if loss else []),
        args=(x, g, win) + ((wout,) if down else ()) + ((target,) if loss else ()))


def _ffn_down(x, act, wout, tm, name, side=None):
    T, D = x.shape
    nc, tf = wout.shape[0], wout.shape[1]

    def body(x_ref, act_ref, wout_ref, y_ref):
        j = pl.program_id(1)

        @pl.when(j == 0)
        def _():
            y_ref[...] = jnp.zeros_like(y_ref)

        y_ref[...] += _tn(act_ref[...], wout_ref[...])

        @pl.when(j == nc - 1)
        def _():
            y_ref[...] = x_ref[...] + 0.5 * y_ref[...]

    row = pl.BlockSpec((tm, D), lambda i, j: (i, 0))
    res = _call(
        body, name=name, grid=(T // tm, nc), side=side,
        in_specs=[row, pl.BlockSpec((None, tf, tm), lambda i, j: (j, 0, i)),
                  pl.BlockSpec((None, tf, D), lambda i, j: (j, 0, 0))],
        out_specs=[row], out_shape=[_sds((T, D), F32)], args=(x, act, wout))
    return res[0] if side is None else (res[0][0], res[1])


def _ffn_bwd(dy, gu, win, wout, tm, name, side=None):
    T, D = dy.shape
    nc, tf = wout.shape[0], wout.shape[1]

    def body(dy_ref, gu_ref, win_ref, wout_ref, dh_ref, dgu_ref, dyb_ref):
        j = pl.program_id(1)

        @pl.when(j == 0)
        def _():
            dh_ref[...] = jnp.zeros_like(dh_ref)
            dyb_ref[...] = (0.5 * dy_ref[...]).astype(BF16)

        dact = _nt(wout_ref[...], dyb_ref[...])
        gt = gu_ref[0].astype(F32)
        up = gu_ref[1].astype(F32)
        s = jax.nn.sigmoid(gt)
        dgb = (dact * up * (s * (1.0 + gt * (1.0 - s)))).astype(BF16)
        dub = (dact * (gt * s)).astype(BF16)
        dgu_ref[0] = dgb
        dgu_ref[1] = dub
        dh_ref[...] += _tn(dgb, win_ref[0]) + _tn(dub, win_ref[1])

    return _call(
        body, name=name, grid=(T // tm, nc), side=side,
        in_specs=[pl.BlockSpec((tm, D), lambda i, j: (i, 0)),
                  pl.BlockSpec((2, None, tf, tm), lambda i, j: (0, j, 0, i)),
                  pl.BlockSpec((2, None, tf, D), lambda i, j: (0, j, 0, 0)),
                  pl.BlockSpec((None, tf, D), lambda i, j: (j, 0, 0))],
        out_specs=[pl.BlockSpec((tm, D), lambda i, j: (i, 0)),
                   pl.BlockSpec((2, None, tf, tm), lambda i, j: (0, j, 0, i)),
                   pl.BlockSpec((tm, D), lambda i, j: (i, 0))],
        out_shape=[_sds((T, D), F32), _sds((2, nc, tf, T), BF16), _sds((T, D), BF16)],
        args=(dy, gu, win, wout))


def _rms_bwd(x, g, dh, dres, tm, name, side=None):
    T, D = x.shape

    def body(x_ref, g_ref, dh_ref, dres_ref, dx_ref, dg_ref):
        @pl.when(pl.program_id(0) == 0)
        def _():
            dg_ref[...] = jnp.zeros_like(dg_ref)

        xv = x_ref[...]
        r = lax.rsqrt(jnp.mean(xv * xv, axis=-1, keepdims=True) + RMS_EPS)
        xhat = xv * r
        dh = dh_ref[...]
        gd = dh * g_ref[...]
        dx_ref[...] = dres_ref[...] + r * (gd - xhat * jnp.mean(gd * xhat, axis=-1, keepdims=True))
        dg_ref[...] += jnp.sum(dh * xhat, axis=0, keepdims=True)

    row = pl.BlockSpec((tm, D), lambda i: (i, 0))
    one = pl.BlockSpec((1, D), lambda i: (0, 0))
    return _call(body, name=name, grid=(T // tm,), side=side, in_specs=[row, one, row, row], out_specs=[row, one],
                 out_shape=[_sds((T, D), F32), _sds((1, D), F32)], args=(x, g, dh, dres))


def _mm_tn(a, b, *, bm, bn, bt, name, side=None, rows=None, m_off=0, into=None, a_rows=False, twin=False):
    nz, T, M = (a.shape[0], a.shape[2], a.shape[1]) if a_rows else a.shape
    N = b.shape[1]
    b_spec = pl.BlockSpec((bt, bn), lambda n, z, m, t: (t, n))
    assert M % bm == 0 and N % bn == 0 and T % bt == 0, (M, bm, N, bn, T, bt)

    def body(a_ref, b_ref, *rest):
        c_ref = rest[-2] if twin else rest[-1]

        @pl.when(pl.program_id(3) == 0)
        def _():
            c_ref[...] = jnp.zeros_like(c_ref)

        ab, bb = a_ref[...].astype(BF16), b_ref[...].astype(BF16)
        c_ref[...] += _nn(ab, bb) if a_rows else _tn(ab, bb)
        if twin:
            @pl.when(pl.program_id(3) == T // bt - 1)
            def _():
                rest[-1][...] = c_ref[...].astype(BF16)

    grid = (N // bn, nz, M // bm, T // bt)
    a_spec = (pl.BlockSpec((None, bm, bt), lambda n, z, m, t: (z, m, t)) if a_rows
              else pl.BlockSpec((None, bt, bm), lambda n, z, m, t: (z, t, m)))
    in_specs = [a_spec, b_spec]
    out_spec = pl.BlockSpec((None, bm, bn), lambda n, z, m, t: (z, m + m_off, n))
    out_shape = _sds((nz, M if rows is None else rows, N), F32)
    if into is not None:
        assert side is None and not twin and into.shape == out_shape.shape
        return pl.pallas_call(body, name=name, grid=grid, in_specs=in_specs + [ANY], out_specs=out_spec,
                              out_shape=out_shape, input_output_aliases={2: 0}, compiler_params=_cparams())(a, b, into)
    outs = [out_shape] + ([_sds(out_shape.shape, BF16)] if twin else [])
    res = _call(body, name=name, grid=grid, side=side, in_specs=in_specs, out_specs=[out_spec] * len(outs),
                out_shape=outs, args=(a, b))
    mine = res if side is None else res[0]
    mine = tuple(mine) if twin else mine[0]
    return mine if side is None else (mine, res[1])


def _tok_spec(layout, tm, n16, C, bc, colmap):
    if layout == "nat":
        return pl.BlockSpec((tm, bc), lambda i, k: (i, colmap(k)))
    assert tm % n16 == 0
    if layout == "slab":
        return pl.BlockSpec((tm // n16, n16, bc), lambda i, k: (i, 0, colmap(k)))
    assert bc == C
    return pl.BlockSpec((n16, (tm // n16) * C), lambda i, k: (0, i))


def _tok_load(ref, layout, sp):
    if layout == "nat":
        return ref[...]
    if layout == "slab":
        return ref[...].reshape(-1, ref.shape[-1])
    c = ref.shape[1] // sp
    return jnp.concatenate([ref[:, s * c:(s + 1) * c] for s in range(sp)], axis=0)


def _tok_store(ref, layout, sp, val, cols=None, accumulate=False):
    def put(idx, v):
        if accumulate:
            ref[idx] += v
        else:
            ref[idx] = v

    lanes = slice(None) if cols is None else slice(cols[0], cols[0] + cols[1])
    if layout == "nat":
        put((slice(None), lanes), val)
    elif layout == "slab":
        put((slice(None), slice(None), lanes), val.reshape(sp, ref.shape[1], val.shape[-1]))
    else:
        assert cols is None
        c, n = ref.shape[1] // sp, ref.shape[0]
        for s in range(sp):
            put((slice(None), slice(s * c, (s + 1) * c)), val[s * n:(s + 1) * n])


def _proj(x, g, wt, gains, modes, *, tn, w_off, slabs, tm, normed_dtype, name, small_wt=None):
    T, D = x.shape
    ntile = len(modes)
    N = ntile * tn
    n16 = T // NSLAB
    in_layout, out_layout = ("view", "slab") if slabs else ("nat", "nat")
    sp = tm // n16
    extra = int(small_wt is not None)
    assert not (extra and slabs)
    x_in = x.reshape(n16, NSLAB * D) if slabs else x
    x_spec = _tok_spec(in_layout, tm, n16, D, D, lambda n: 0)
    oshape = lambda c: (NSLAB, n16, c) if slabs else (T, c)
    ospec = lambda bc, cm: _tok_spec(out_layout, tm, n16, None, bc, cm)

    def body(x_ref, g_ref, w_ref, gains_ref, *refs):
        raw_ref, nrm_ref, h_ref = refs[-3 - extra:len(refs) - extra]
        n = pl.program_id(1)

        @pl.when(n == 0)
        def _():
            xv = _tok_load(x_ref, in_layout, sp)
            r = lax.rsqrt(jnp.mean(xv * xv, axis=-1, keepdims=True) + RMS_EPS)
            hb = (xv * r * g_ref[...]).astype(BF16)
            _tok_store(h_ref, out_layout, sp, hb)
            if extra:
                refs[-1][...] = _nt(hb, refs[0][...])

        y = _nt(_tok_load(h_ref, out_layout, sp), w_ref[...])
        _tok_store(raw_ref, out_layout, sp, y.astype(BF16))
        for t, mode in enumerate(modes):
            @pl.when(n == t)
            def _(t=t, mode=mode):
                if not mode:
                    _tok_store(nrm_ref, out_layout, sp, y.astype(nrm_ref.dtype))
                    return
                gain = gains_ref[t]
                for k in range(tn // HEAD):
                    yk = y[:, k * HEAD:(k + 1) * HEAD]
                    r = lax.rsqrt(jnp.mean(yk * yk, axis=-1, keepdims=True) + RMS_EPS)
                    _tok_store(nrm_ref, out_layout, sp, (yk * r * gain).astype(nrm_ref.dtype), cols=(k * HEAD, HEAD))

    return pl.pallas_call(
        body, name=name, grid=(T // tm, ntile),
        in_specs=[x_spec, pl.BlockSpec((1, D), lambda i, n: (0, 0)),
                  pl.BlockSpec((tn, D), lambda i, n: (n + w_off, 0)),
                  pl.BlockSpec((ntile, 1, HEAD), lambda i, n: (0, 0, 0))]
        + ([pl.BlockSpec((HEAD, D), lambda i, n: (0, 0))] if extra else []),
        out_specs=[ospec(tn, lambda n: n), ospec(tn, lambda n: n), ospec(D, lambda n: 0)]
        + ([ospec(HEAD, lambda n: 0)] if extra else []),
        out_shape=[_sds(oshape(N), BF16), _sds(oshape(N), normed_dtype), _sds(oshape(D), BF16)]
        + ([_sds((T, HEAD), F32)] if extra else []),
        compiler_params=_cparams(),
    )(x_in, g, wt, gains, *([small_wt] if extra else []))


def _mm(a, w, *, nt, tk, tm, a_layout, out_layout, resid=None, name, w_off=0, n_out=None, small=None):
    if a_layout == "slab":
        T, K = a.shape[0] * a.shape[1], a.shape[2]
    else:
        T, K = a.shape
    N = (w.shape[0] if nt else w.shape[1]) if n_out is None else n_out
    n16 = T // NSLAB
    nk = K // tk
    sp = tm // n16
    a_in = a.reshape(n16, NSLAB * K) if a_layout == "view" else a
    w_spec = (pl.BlockSpec((N, tk), lambda i, k: (w_off, k)) if nt
              else pl.BlockSpec((tk, N), lambda i, k: (k + w_off, 0)))
    o_spec = _tok_spec(out_layout, tm, n16, N, N, lambda k: 0)
    oshape = {"nat": (T, N), "slab": (NSLAB, n16, N), "view": (n16, NSLAB * N)}[out_layout]
    has_resid = resid is not None

    def body(*refs):
        a_ref, w_ref = refs[0], refs[1]
        o_ref = refs[-1]
        k = pl.program_id(1)

        @pl.when(k == 0)
        def _():
            o_ref[...] = refs[2][...] if has_resid else jnp.zeros_like(o_ref)
            if small is not None:
                o_ref[...] += _nn(refs[-3][...].astype(BF16), refs[-2][...])

        ab = _tok_load(a_ref, a_layout, sp).astype(BF16)
        _tok_store(o_ref, out_layout, sp, _nt(ab, w_ref[...]) if nt else _nn(ab, w_ref[...]), accumulate=True)

    ins = [a_in, w]
    in_specs = [_tok_spec(a_layout, tm, n16, K, tk, lambda k: k), w_spec]
    if has_resid:
        ins.append(resid.reshape(n16, NSLAB * N) if out_layout == "view" else resid)
        in_specs.append(o_spec)
    if small is not None:
        assert out_layout == "nat" and small[1].shape == (HEAD, N)
        ins += list(small)
        in_specs += [pl.BlockSpec((tm, HEAD), lambda i, k: (i, 0)), pl.BlockSpec((HEAD, N), lambda i, k: (0, 0))]
    out = pl.pallas_call(
        body, name=name, grid=(T // tm, nk), in_specs=in_specs, out_specs=o_spec,
        out_shape=_sds(oshape, F32), compiler_params=_cparams(),
    )(*ins)
    return out.reshape(T, N) if out_layout == "view" else out


def _log_sigmoid(z):
    return jnp.minimum(z, 0.0) - jnp.log(1.0 + jnp.exp(-jnp.abs(z)))


def _fox_gate_fwd(f_raw, fbias, name):
    T = f_raw.shape[0]
    cb = 256

    def body(f_ref, b_ref, c_ref):
        row = lax.broadcasted_iota(jnp.int32, (cb, cb), 0)
        col = lax.broadcasted_iota(jnp.int32, (cb, cb), 1)
        tri = (col <= row).astype(F32)
        carry = jnp.zeros((1, HEAD), F32)
        for i in range(T // cb):
            lf = _log_sigmoid(f_ref[i * cb:(i + 1) * cb, :] + b_ref[...])
            c = jnp.dot(tri, lf, preferred_element_type=F32, precision=lax.Precision.HIGHEST) + carry
            c_ref[i * cb:(i + 1) * cb, :] = c
            carry = c[cb - 1:cb, :]

    return pl.pallas_call(body, name=name, out_shape=_sds((T, HEAD), F32), compiler_params=_cparams())(f_raw, fbias)


def _fox_gate_bwd(f_raw, fbias, dc, name):
    T = f_raw.shape[0]
    cb = 256

    def body(f_ref, b_ref, dc_ref, df_ref, db_ref):
        row = lax.broadcasted_iota(jnp.int32, (cb, cb), 0)
        col = lax.broadcasted_iota(jnp.int32, (cb, cb), 1)
        tri = (col >= row).astype(F32)
        carry = jnp.zeros((1, HEAD), F32)
        dbias = jnp.zeros((1, HEAD), F32)
        for i in reversed(range(T // cb)):
            dlf = jnp.dot(tri, dc_ref[i * cb:(i + 1) * cb, :], preferred_element_type=F32,
                          precision=lax.Precision.HIGHEST) + carry
            carry = dlf[0:1, :]
            z = f_ref[i * cb:(i + 1) * cb, :] + b_ref[...]
            df = dlf * jax.nn.sigmoid(-z)
            df_ref[i * cb:(i + 1) * cb, :] = df
            dbias = dbias + jnp.sum(df, axis=0, keepdims=True)
        db_ref[...] = dbias

    return pl.pallas_call(body, name=name, out_shape=[_sds((T, HEAD), F32), _sds((1, HEAD), F32)],
                          compiler_params=_cparams())(f_raw, fbias, dc)


def _fox_fwd(qkv, c_col, c_row, tq, name, side=None):
    T = qkv.shape[0]
    H = qkv.shape[1] // (3 * HEAD)
    nq = T // tq
    c_blocks = c_row.reshape(H, nq, 1, tq)

    def body(q_ref, k_ref, v_ref, cq_ref, ck_ref, o_ref, lse_ref):
        qi = pl.program_id(1)
        q, cq = q_ref[...], cq_ref[...]
        causal = lax.broadcasted_iota(jnp.int32, (tq, tq), 1) <= lax.broadcasted_iota(jnp.int32, (tq, tq), 0)

        def key_block(ki, carry, diagonal):
            m, l, acc = carry
            rows = pl.ds(pl.multiple_of(ki * tq, tq), tq)
            s = _nt(q, k_ref[rows, :]) * SCALE + cq - ck_ref[ki]
            if diagonal:
                s = jnp.where(causal, s, NEG)
            m_new = jnp.maximum(m, jnp.max(s, axis=-1, keepdims=True))
            alpha = jnp.exp(m - m_new)
            p = jnp.exp(s - m_new)
            l = alpha * l + jnp.sum(p, axis=-1, keepdims=True)
            acc = alpha * acc + _nn(p.astype(BF16), v_ref[rows, :])
            return m_new, l, acc

        init = (jnp.full((tq, 1), NEG, F32), jnp.zeros((tq, 1), F32), jnp.zeros((tq, HEAD), F32))
        carry = lax.fori_loop(0, qi, lambda ki, c: key_block(ki, c, False), init)
        m, l, acc = key_block(qi, carry, True)
        o_ref[...] = acc / l
        lse_ref[...] = m + jnp.log(l)

    return _call(
        body, name=name, grid=(H, nq), side=side,
        in_specs=[pl.BlockSpec((tq, HEAD), lambda h, qi: (qi, h)),
                  pl.BlockSpec((T, HEAD), lambda h, qi: (0, H + h)),
                  pl.BlockSpec((T, HEAD), lambda h, qi: (0, 2 * H + h)),
                  pl.BlockSpec((None, tq, 1), lambda h, qi: (h, qi, 0)),
                  pl.BlockSpec((None, nq, 1, tq), lambda h, qi: (h, 0, 0, 0))],
        out_specs=[pl.BlockSpec((tq, HEAD), lambda h, qi: (qi, h)),
                   pl.BlockSpec((None, tq, 1), lambda h, qi: (h, qi, 0))],
        out_shape=[_sds((T, H * HEAD), F32), _sds((H, T, 1), F32)],
        args=(qkv, qkv, qkv, c_col, c_blocks))


def _fox_bwd(qkv, c_col, c_row, out, dout, lse, tq, name, side=None, do_off=0):
    T = qkv.shape[0]
    H = qkv.shape[1] // (3 * HEAD)
    nq = T // tq

    def body(q_ref, k_ref, v_ref, cq_ref, ck_ref, o_ref, do_ref, lse_ref, dq_ref, dk_ref, dv_ref, dck_ref, dcq_ref,
             delta_s):
        ki = pl.program_id(1)

        @pl.when(ki == 0)
        def _():
            dq_ref[...] = jnp.zeros_like(dq_ref)
            dcq_ref[...] = jnp.zeros_like(dcq_ref)
            delta_s[...] = jnp.sum(do_ref[...] * o_ref[...], axis=-1, keepdims=True)

        k, v, ck = k_ref[...], v_ref[...], ck_ref[...]
        causal = lax.broadcasted_iota(jnp.int32, (tq, tq), 1) <= lax.broadcasted_iota(jnp.int32, (tq, tq), 0)

        def query_block(qi, carry, diagonal):
            dk, dv, dck = carry
            rows = pl.ds(pl.multiple_of(qi * tq, tq), tq)
            q = q_ref[rows, :]
            s = _nt(q, k) * SCALE + cq_ref[rows, :] - ck
            if diagonal:
                s = jnp.where(causal, s, NEG)
            p = jnp.exp(s - lse_ref[rows, :])
            dob = do_ref[rows, :].astype(BF16)
            ds = p * (_nt(dob, v) - delta_s[rows, :])
            dsb = ds.astype(BF16)
            dq_ref[rows, :] += _nn(dsb, k) * SCALE
            dcq_ref[rows, :] += jnp.sum(ds, axis=-1, keepdims=True)
            return dk + _tn(dsb, q), dv + _tn(p.astype(BF16), dob), dck - jnp.sum(ds, axis=0, keepdims=True)

        init = (jnp.zeros((tq, HEAD), F32), jnp.zeros((tq, HEAD), F32), jnp.zeros((1, tq), F32))
        carry = query_block(ki, init, True)
        dk, dv, dck = lax.fori_loop(ki + 1, nq, lambda qi, c: query_block(qi, c, False), carry)
        dk_ref[...] = dk * SCALE
        dv_ref[...] = dv
        dck_ref[...] = dck

    head = lambda off: pl.BlockSpec((T, HEAD), lambda h, ki: (0, off + h))
    col = pl.BlockSpec((None, T, 1), lambda h, ki: (h, 0, 0))
    return _call(
        body, name=name, grid=(H, nq), side=side,
        in_specs=[head(0),
                  pl.BlockSpec((tq, HEAD), lambda h, ki: (ki, H + h)),
                  pl.BlockSpec((tq, HEAD), lambda h, ki: (ki, 2 * H + h)),
                  col, pl.BlockSpec((None, 1, tq), lambda h, ki: (h, 0, ki)), head(0), head(do_off), col],
        out_specs=[head(0),
                   pl.BlockSpec((tq, HEAD), lambda h, ki: (ki, h)),
                   pl.BlockSpec((tq, HEAD), lambda h, ki: (ki, h)),
                   pl.BlockSpec((None, 1, tq), lambda h, ki: (h, 0, ki)), col],
        out_shape=[_sds((T, H * HEAD), F32), _sds((T, H * HEAD), F32), _sds((T, H * HEAD), F32), _sds((H, 1, T), F32),
                   _sds((H, T, 1), F32)],
        scratch_shapes=[pltpu.VMEM((T, 1), F32)],
        args=(qkv, qkv, qkv, c_col, c_row, out, dout, lse))


def _t5_bucket(dist):
    max_exact = NUM_BUCKETS // 2
    d = dist.astype(np.float32)
    large = max_exact + (np.log(np.maximum(d, np.float32(1.0)) / np.float32(max_exact))
                         / np.float32(math.log(MAX_DISTANCE / max_exact))
                         * np.float32(NUM_BUCKETS - max_exact)).astype(np.int32)
    large = np.minimum(large, NUM_BUCKETS - 1)
    return np.where(dist < max_exact, dist, large)


def _bucket_maps():
    maps = []
    for d in DILATIONS:
        e = NSLAB // d
        rows = BLK // e
        idx = np.arange(BLK)
        pos = e * (idx % rows) + idx // rows
        qpos = pos[:, None] + BLK
        kpos = np.concatenate([pos, pos + BLK])[None, :]
        delta = qpos - kpos
        band = (delta >= 0) & (delta <= BLK)
        bucket = _t5_bucket(np.clip(delta, 0, None) * d)
        maps.append(np.where(band, bucket, -1).astype(np.int32))
    return np.stack(maps)


def _dil_geometry(T):
    n16 = T // NSLAB
    geo = []
    for d in DILATIONS:
        e = NSLAB // d
        rows = BLK // e
        nblk = n16 // rows
        geo.append((d, e, rows, nblk))
    return geo


DIL_INTERLEAVE_FWD = {1: 4, 4: 8, 16: 8}
DIL_INTERLEAVE_BWD = {1: 8, 4: 8, 16: 8}


def _dil_interleave(per_step, nblocks):
    while per_step > 1 and (nblocks % per_step or nblocks // per_step < 2):
        per_step -= 1
    return per_step


def _dil_bias(tab_ref, bkt_ref, bias_s, h):
    for p in range(len(DILATIONS)):
        bk = bkt_ref[p]
        bias = jnp.full((BLK, 2 * BLK), NEG, F32)
        for b in range(NUM_BUCKETS):
            bias = jnp.where(bk == b, tab_ref[b, h], bias)
        bias_s[p] = bias


def _dil_rows(d, e, rows, sub, blk):
    start = pl.multiple_of(blk * rows, rows)
    return [(sub + d * j, pl.ds(start, rows)) for j in range(e)]


def _gather(ref, idx):
    return jnp.concatenate([ref[s, r, :] for s, r in idx], axis=0)


def _scatter(ref, idx, val, rows):
    for j, (s, r) in enumerate(idx):
        ref[s, r, :] = val[j * rows:(j + 1) * rows]


def _scatter_add(ref, idx, val, rows):
    for j, (s, r) in enumerate(idx):
        ref[s, r, :] += val[j * rows:(j + 1) * rows]


def _dil_fwd(qkv, table, name, side=None):
    n16 = qkv.shape[1]
    T = NSLAB * n16
    H = qkv.shape[2] // (3 * HEAD)
    geo = _dil_geometry(T)
    bkt = jnp.asarray(_bucket_maps())

    def body(tab_ref, bkt_ref, q_ref, k_ref, v_ref, o_ref, lse_ref, onat_ref, bias_s, m_s, l_s):
        h = pl.program_id(0)
        _dil_bias(tab_ref, bkt_ref, bias_s, h)
        first_mask = lax.broadcasted_iota(jnp.int32, (BLK, 2 * BLK), 1) < BLK

        starts = len(DILATIONS) - 1

        def load(p, d, e, rows, sub, blk):
            cur = _dil_rows(d, e, rows, sub, blk)
            prev = _dil_rows(d, e, rows, sub, jnp.maximum(blk - 1, 0))
            q = _gather(q_ref, cur).astype(BF16)
            kk = jnp.concatenate([_gather(k_ref, prev), _gather(k_ref, cur)], axis=0).astype(BF16)
            vv = jnp.concatenate([_gather(v_ref, prev), _gather(v_ref, cur)], axis=0).astype(BF16)
            old = None if p == starts else (_gather(m_s, cur), _gather(l_s, cur), _gather(o_ref, cur))
            return cur, blk, q, kk, vv, old

        def compute(p, blk, q, kk, vv, old):
            s = _nt(q, kk) * SCALE + bias_s[p]
            s = jnp.where(first_mask & (blk == 0), NEG, s)
            m_blk = jnp.max(s, axis=-1, keepdims=True)
            if old is None:
                m_new = m_blk
                pr = jnp.exp(s - m_new)
                l_new = jnp.sum(pr, axis=-1, keepdims=True)
                acc = _nn(pr.astype(BF16), vv)
            else:
                m_old, l_old, acc_old = old
                m_new = jnp.maximum(m_old, m_blk)
                alpha = jnp.exp(m_old - m_new)
                pr = jnp.exp(s - m_new)
                l_new = alpha * l_old + jnp.sum(pr, axis=-1, keepdims=True)
                acc = alpha * acc_old + _nn(pr.astype(BF16), vv)
            if p == 0:
                return acc / l_new, m_new + jnp.log(l_new), None
            return acc, m_new, l_new

        def store(p, rows, cur, acc, m_new, l_new):
            _scatter(o_ref, cur, acc, rows)
            if p == 0:
                _scatter(lse_ref, cur, m_new, rows)
            else:
                _scatter(m_s, cur, m_new, rows)
                _scatter(l_s, cur, l_new, rows)

        for p in reversed(range(len(DILATIONS))):
            d, e, rows, nblk = geo[p]
            per_step = _dil_interleave(DIL_INTERLEAVE_FWD[d], d * nblk)

            def step(i, carry, p=p, d=d, e=e, rows=rows, nblk=nblk, per_step=per_step):
                ids = [i + u * (d * nblk // per_step) for u in range(per_step)]
                loaded = [load(p, d, e, rows, j // nblk, j % nblk) for j in ids]
                done = [(cur, compute(p, blk, q, kk, vv, old)) for cur, blk, q, kk, vv, old in loaded]
                for cur, res in done:
                    store(p, rows, cur, *res)
                return carry

            lax.fori_loop(0, d * nblk // per_step, step, 0)

        for r in range(NSLAB):
            onat_ref[pl.ds(r, n16, stride=NSLAB), :] = o_ref[r]

    head = lambda off: pl.BlockSpec((NSLAB, n16, HEAD), lambda h: (0, 0, off + h))
    return _call(
        body, name=name, grid=(H,), side=side,
        in_specs=[pl.BlockSpec(memory_space=pltpu.SMEM), pl.BlockSpec((3, BLK, 2 * BLK), lambda h: (0, 0, 0)),
                  head(0), head(H), head(2 * H)],
        out_specs=[head(0), pl.BlockSpec((None, NSLAB, n16, 1), lambda h: (h, 0, 0, 0)),
                   pl.BlockSpec((T, HEAD), lambda h: (0, h))],
        out_shape=[_sds((NSLAB, n16, H * HEAD), F32), _sds((H, NSLAB, n16, 1), F32), _sds((T, H * HEAD), F32)],
        scratch_shapes=[pltpu.VMEM((3, BLK, 2 * BLK), F32), pltpu.VMEM((NSLAB, n16, 1), F32),
                        pltpu.VMEM((NSLAB, n16, 1), F32)],
        args=(table, bkt, qkv, qkv, qkv))


def _dil_bwd(qkv, table, out, dout, lse, name, side=None):
    n16 = qkv.shape[1]
    T = NSLAB * n16
    H = qkv.shape[2] // (3 * HEAD)
    geo = _dil_geometry(T)
    bkt = jnp.asarray(_bucket_maps())

    def body(tab_ref, bkt_ref, q_ref, k_ref, v_ref, o_ref, dnat_ref, lse_ref,
             dq_ref, dk_ref, dv_ref, dtab_ref, bias_s, dbias_s, delta_s, do_ref):
        h = pl.program_id(0)
        _dil_bias(tab_ref, bkt_ref, bias_s, h)
        first_mask = lax.broadcasted_iota(jnp.int32, (BLK, 2 * BLK), 1) < BLK
        dbias_s[...] = jnp.zeros_like(dbias_s)
        dq_ref[...] = jnp.zeros_like(dq_ref)
        dk_ref[...] = jnp.zeros_like(dk_ref)
        dv_ref[...] = jnp.zeros_like(dv_ref)
        for r in range(NSLAB):
            do_ref[r] = dnat_ref[pl.ds(r, n16, stride=NSLAB), :]
            delta_s[r] = jnp.sum(do_ref[r] * o_ref[r], axis=-1, keepdims=True)

        def load(d, e, rows, sub, blk):
            cur = _dil_rows(d, e, rows, sub, blk)
            prev = _dil_rows(d, e, rows, sub, jnp.maximum(blk - 1, 0))
            q = _gather(q_ref, cur).astype(BF16)
            kk = jnp.concatenate([_gather(k_ref, prev), _gather(k_ref, cur)], axis=0).astype(BF16)
            vv = jnp.concatenate([_gather(v_ref, prev), _gather(v_ref, cur)], axis=0).astype(BF16)
            dob = _gather(do_ref, cur).astype(BF16)
            return cur, prev, blk, q, kk, vv, dob, _gather(lse_ref, cur), _gather(delta_s, cur)

        def compute(p, blk, q, kk, vv, dob, lse, delta):
            s = _nt(q, kk) * SCALE + bias_s[p]
            s = jnp.where(first_mask & (blk == 0), NEG, s)
            pr = jnp.exp(s - lse)
            ds = pr * (_nt(dob, vv) - delta)
            dsb = ds.astype(BF16)
            return ds, _nn(dsb, kk) * SCALE, _tn(dsb, q) * SCALE, _tn(pr.astype(BF16), dob)

        def store(rows, cur, prev, dq, dkk, dvv):
            _scatter_add(dq_ref, cur, dq, rows)
            _scatter_add(dk_ref, prev, dkk[:BLK], rows)
            _scatter_add(dk_ref, cur, dkk[BLK:], rows)
            _scatter_add(dv_ref, prev, dvv[:BLK], rows)
            _scatter_add(dv_ref, cur, dvv[BLK:], rows)

        for p in range(len(DILATIONS)):
            d, e, rows, nblk = geo[p]
            per_step = _dil_interleave(DIL_INTERLEAVE_BWD[d], d * nblk)

            def step(i, carry, p=p, d=d, e=e, rows=rows, nblk=nblk, per_step=per_step):
                ids = [i + u * (d * nblk // per_step) for u in range(per_step)]
                loaded = [load(d, e, rows, j // nblk, j % nblk) for j in ids]
                done = [(cur, prev, compute(p, *rest)) for cur, prev, *rest in loaded]
                dbias_s[p] += functools.reduce(jnp.add, [res[0] for _, _, res in done])
                for cur, prev, res in done:
                    store(rows, cur, prev, *res[1:])
                return carry

            lax.fori_loop(0, d * nblk // per_step, step, 0)

        lane = lax.broadcasted_iota(jnp.int32, (1, HEAD), 1)
        row = jnp.zeros((1, HEAD), F32)
        for b in range(NUM_BUCKETS):
            tot = jnp.zeros((1, 1), F32)
            for p in range(len(DILATIONS)):
                hit = jnp.where(bkt_ref[p] == b, dbias_s[p], 0.0)
                tot = tot + jnp.sum(jnp.sum(hit, axis=0, keepdims=True), axis=1, keepdims=True)
            row = jnp.where(lane == b, tot, row)
        dtab_ref[...] = row

    head = lambda off: pl.BlockSpec((NSLAB, n16, HEAD), lambda h: (0, 0, off + h))
    return _call(
        body, name=name, grid=(H,), side=side,
        in_specs=[pl.BlockSpec(memory_space=pltpu.SMEM), pl.BlockSpec((3, BLK, 2 * BLK), lambda h: (0, 0, 0)),
                  head(0), head(H), head(2 * H), head(0), pl.BlockSpec((T, HEAD), lambda h: (0, h)),
                  pl.BlockSpec((None, NSLAB, n16, 1), lambda h: (h, 0, 0, 0))],
        out_specs=[head(0), head(0), head(0), pl.BlockSpec((None, 1, HEAD), lambda h: (h, 0, 0))],
        out_shape=[_sds((NSLAB, n16, H * HEAD), F32)] * 3 + [_sds((H, 1, HEAD), F32)],
        scratch_shapes=[pltpu.VMEM((3, BLK, 2 * BLK), F32), pltpu.VMEM((3, BLK, 2 * BLK), F32),
                        pltpu.VMEM((NSLAB, n16, 1), F32), pltpu.VMEM((NSLAB, n16, HEAD), F32)],
        args=(table, bkt, qkv, qkv, qkv, out, dout, lse))


def _qknorm_bwd(raw, dq, dk, dv, gains, tm, name):
    T, N = raw.shape
    C = N // 3

    def body(raw_ref, dq_ref, dk_ref, dv_ref, gains_ref, dp_ref, dg_ref):
        @pl.when(pl.program_id(0) == 0)
        def _():
            dg_ref[...] = jnp.zeros_like(dg_ref)

        for t, d_ref in enumerate((dq_ref, dk_ref)):
            gain = gains_ref[t]
            dgain = jnp.zeros((1, HEAD), F32)
            for k in range(C // HEAD):
                y = raw_ref[:, t * C + k * HEAD:t * C + (k + 1) * HEAD].astype(F32)
                dn = d_ref[:, k * HEAD:(k + 1) * HEAD]
                r = lax.rsqrt(jnp.mean(y * y, axis=-1, keepdims=True) + RMS_EPS)
                yhat = y * r
                gd = dn * gain
                dy = r * (gd - yhat * jnp.mean(gd * yhat, axis=-1, keepdims=True))
                dp_ref[:, t * C + k * HEAD:t * C + (k + 1) * HEAD] = dy.astype(BF16)
                dgain = dgain + jnp.sum(dn * yhat, axis=0, keepdims=True)
            dg_ref[t] += dgain
        dp_ref[:, 2 * C:] = dv_ref[...].astype(BF16)

    third = pl.BlockSpec((tm, C), lambda i: (i, 0))
    return pl.pallas_call(
        body, name=name, grid=(T // tm,),
        in_specs=[pl.BlockSpec((tm, N), lambda i: (i, 0)), third, third, third,
                  pl.BlockSpec((2, 1, HEAD), lambda i: (0, 0, 0))],
        out_specs=[pl.BlockSpec((tm, N), lambda i: (i, 0)), pl.BlockSpec((2, 1, HEAD), lambda i: (0, 0, 0))],
        out_shape=[_sds((T, N), BF16), _sds((2, 1, HEAD), F32)], compiler_params=_cparams(),
    )(raw, dq, dk, dv, gains)


def _pad_lanes(v, width=HEAD):
    return jnp.pad(v, ((0, 0), (0, width - v.shape[1])))


def _local_step(x, target, small, wts, plan=None):
    grads = {}

    def hosted(host, fn, *args, **kw):
        side = plan.before(host, wts, grads) if plan is not None else None
        if side is None:
            return fn(*args, name=host, **kw)
        res, side_res = fn(*args, name=host, side=side, **kw)
        plan.after(host, side_res, wts, grads)
        return res

    T, D = x.shape
    C = D // 2
    H = C // HEAD
    n16 = T // NSLAB
    tm = min(512, T)
    tmm = min(1024, T)
    tms = 4 * n16
    tq = min(512, T)
    bn = min(1024, D)
    g1, gm, g2 = small["ffn1_norm"], small["mix_norm"], small["ffn2_norm"]
    gains_a = jnp.stack([small["q_norm_a"], small["k_norm_a"], jnp.ones_like(small["q_norm_a"])])
    gains_b = jnp.stack([small["q_norm_b"], small["k_norm_b"], jnp.ones_like(small["q_norm_b"])])
    fbias = _pad_lanes(small["forget_bias"])
    table = small["rel_bias_table"]

    h1, gu1, act1 = hosted("ffn1_up", _ffn_fwd, x, g1, wts["ffn1_in"], None, tmm)
    x1 = hosted("ffn1_down", _ffn_down, x, act1, wts["ffn1_out"], tmm)
    w_in_t, w_f_t, w_o = wts["w_in_t"], wts["w_f_t"], wts["w_o"]
    raw_a, nrm_a, h2a = _proj(x1, gm, w_in_t, gains_a, (True, True, False), tn=C, w_off=0, slabs=True, tm=tms,
                              normed_dtype=F32, name="proj_a")
    raw_b, nrm_b, h2b, f_raw = _proj(x1, gm, w_in_t, gains_b, (True, True, False), tn=C, w_off=3, slabs=False, tm=tmm,
                                     normed_dtype=BF16, name="proj_b", small_wt=w_f_t)
    c = _fox_gate_fwd(f_raw, fbias, "fox_gate_fwd")
    c_heads = c[:, :H].T
    c_col, c_row = c_heads[:, :, None], c_heads[:, None, :]
    out_a, lse_a, out_a_nat = hosted("dil_fwd", _dil_fwd, nrm_a, table)
    out_b, lse_b = hosted("fox_fwd", _fox_fwd, nrm_b, c_col, c_row, tq)
    x2a = _mm(out_a_nat, w_o, nt=False, tk=C, tm=tmm, a_layout="nat", out_layout="nat", resid=x1, name="out_a")
    x2 = _mm(out_b, w_o, nt=False, tk=C, tm=tmm, a_layout="nat", out_layout="nat", resid=x2a, w_off=1, name="out_b")
    dy, h3, gu3, act3, loss_row = _ffn_fwd(x2, g2, wts["ffn2_in"], wts["ffn2_out"], tm, "ffn2_fwd", target=target)

    def ffn_backward(tag, xin, g, h, gu, act, win, wout, dres):
        nc, tf = wout.shape[0], wout.shape[1]
        dh, dgu, dyb = hosted(tag + "_bwd", _ffn_bwd, dres, gu, win, wout, tm)
        grads[tag + "_w_in_t"], grads[tag + "_w_in_t_bf16"] = hosted(
            tag + "_dwin", _mm_tn, dgu.reshape(2 * nc, tf, T), h, bm=tf, bn=bn, bt=T, a_rows=True, twin=True)
        dxin, grads[tag + "_norm"] = hosted(tag + "_rms_bwd", _rms_bwd, xin, g, dh, dres, tm)
        grads[tag + "_w_out"], grads[tag + "_w_out_bf16"] = hosted(
            tag + "_dwout", _mm_tn, act, dyb, bm=tf, bn=bn, bt=T, a_rows=True, twin=True)
        return dxin

    dx2 = ffn_backward("ffn2", x2, g2, h3, gu3, act3, wts["ffn2_in"], wts["ffn2_out"], dy)

    dmix = _mm(dx2, w_o, nt=True, tk=D, tm=tmm, a_layout="nat", out_layout="nat", name="dmix")
    dwo = _mm_tn(out_a_nat.reshape(1, T, C), dx2, bm=C, bn=bn, bt=tm, rows=2 * C, name="dwo_a")
    dwo = _mm_tn(out_b.reshape(1, T, C), dx2, bm=C, bn=bn, bt=tm, rows=2 * C, m_off=1, into=dwo, name="dwo_b")
    grads["w_out"] = dwo[0]

    dqa, dka, dva, dtab = hosted("dil_bwd", _dil_bwd, nrm_a, table, out_a, dmix, lse_a)
    dqb, dkb, dvb, dck, dcq = hosted("fox_bwd", _fox_bwd, nrm_b, c_col, c_row, out_b, dmix, lse_b, tq, do_off=H)
    grads["rel_bias_table"] = dtab[:, 0, :NUM_BUCKETS].T
    dc = _pad_lanes((dck[:, 0, :] + dcq[:, :, 0]).T)
    df, dfb = _fox_gate_bwd(f_raw, fbias, dc, "fox_gate_bwd")
    grads["forget_bias"] = dfb[:, :H]

    flat = lambda a: a.reshape(T, a.shape[-1])
    dproj_a, dgain_a = _qknorm_bwd(flat(raw_a), flat(dqa), flat(dka), flat(dva), gains_a[:2], min(256, T), "qknorm_bwd_a")
    dproj_b, dgain_b = _qknorm_bwd(raw_b, dqb, dkb, dvb, gains_b[:2], min(256, T), "qknorm_bwd_b")
    grads["q_norm_a"], grads["k_norm_a"] = dgain_a[0], dgain_a[1]
    grads["q_norm_b"], grads["k_norm_b"] = dgain_b[0], dgain_b[1]
    dproj_a = dproj_a.reshape(NSLAB, n16, 3 * C)

    dh2 = _mm(dproj_a, w_in_t, nt=False, tk=C, tm=tms, a_layout="slab", out_layout="view", name="dh2_a")
    dh2 = _mm(dproj_b, w_in_t, nt=False, tk=C, tm=tmm, a_layout="nat", out_layout="nat", resid=dh2, w_off=3,
              small=(df, w_f_t), name="dh2_b")
    dx1, grads["mix_norm"] = _rms_bwd(x1, gm, dh2, dx2, tm, "mix_rms_bwd")
    bt = min(2048, T)
    dwt = _mm_tn(flat(dproj_a)[None], flat(h2a), bm=C, bn=bn, bt=bt, rows=6 * C + H, name="dw_a")
    dwt = _mm_tn(dproj_b[None], h2b, bm=C, bn=bn, bt=bt, rows=6 * C + H, m_off=3, into=dwt, name="dw_b")
    dwt = _mm_tn(df[None, :, :H], h2b, bm=H, bn=bn, bt=bt, rows=6 * C + H, m_off=6 * C // H, into=dwt, name="dw_f")
    grads["w_in_t"] = dwt[0]

    grad_x = ffn_backward("ffn1", x, g1, h1, gu1, act1, wts["ffn1_in"], wts["ffn1_out"], dx1)
    return loss_row, grad_x, grads


def _place():
    x, y, c = lax.axis_index("x"), lax.axis_index("y"), lax.axis_index("c")
    other_chips = [(1 - x, y), (x, 1 - y), (1 - x, 1 - y)]
    return x, y, c, other_chips


def _run_side(side, name):
    def body(*refs):
        si, so = len(side.ins), len(side.outs)
        side.start(refs[:si], refs[si:si + so], refs[si + so:])
        side.finish(refs[:si], refs[si:si + so], refs[si + so:])

    return pl.pallas_call(body, name=name, in_specs=[ANY] * len(side.ins), out_specs=[ANY] * len(side.outs),
                          out_shape=side.outs, scratch_shapes=side.sems)(*side.ins)


def _all_gather(shards):
    n = len(shards)

    def plan(ins, outs, sems):
        send_sems, recv_sems, local_sems = sems
        x, y, c, chips = _place()
        me, sibling = (x, y, c), (x, y, 1 - c)

        def copy(a, k, block, to, src=None):
            px, py, pc = block
            dst = outs[a].at[4 * px + 2 * py + pc]
            return pltpu.make_async_remote_copy(
                src_ref=dst if src is None else src, dst_ref=dst, send_sem=send_sems.at[7 * a + k],
                recv_sem=recv_sems.at[7 * a + k], device_id=to, device_id_type=MESH)

        mine = [pltpu.make_async_copy(ins[a], outs[a].at[4 * x + 2 * y + c], local_sems.at[a]) for a in range(n)]
        first = []
        for a in range(n):
            first.append(copy(a, 0, me, sibling, src=ins[a]))
            first += [copy(a, 1 + j, me, (*chip, c), src=ins[a]) for j, chip in enumerate(chips)]
        return copy, mine, first, me, sibling, c, chips

    def start(ins, outs, sems):
        _, mine, first, *_ = plan(ins, outs, sems)
        for cp in mine + first:
            cp.start()

    def finish(ins, outs, sems):
        copy, mine, first, me, sibling, c, chips = plan(ins, outs, sems)
        passed = []
        for a in range(n):
            for j, chip in enumerate(chips):
                copy(a, 1 + j, (*chip, c), me).wait_recv()
                fwd = copy(a, 4 + j, (*chip, c), sibling)
                fwd.start()
                passed.append(fwd)
        for a in range(n):
            copy(a, 0, sibling, me).wait_recv()
            for j, chip in enumerate(chips):
                copy(a, 4 + j, (*chip, 1 - c), me).wait_recv()
        for cp in first + passed:
            cp.wait_send()
        for cp in mine:
            cp.wait()

    return _Side(shards, [_sds((N_DEV,) + s.shape, s.dtype) for s in shards],
                 [pltpu.SemaphoreType.DMA((7 * n,)), pltpu.SemaphoreType.DMA((7 * n,)), pltpu.SemaphoreType.DMA((n,))],
                 start, finish)


def _all_gather_relayed(shards):
    n = len(shards)
    halves = [-(-(s.shape[0] // 2) // BF16_TILE_ROWS) * BF16_TILE_ROWS for s in shards]

    def body_parts(ins, outs, sems):
        send_sems, recv_sems, local_sems = sems
        x, y, c, _ = _place()
        me, sib, xn, yn, dg = (x, y, c), (x, y, 1 - c), (1 - x, y, c), (x, 1 - y, c), (1 - x, 1 - y, c)

        def rows(a, block, part):
            px, py, pc = block
            whole = outs[a].at[4 * px + 2 * py + pc]
            if part is None:
                return whole
            return whole.at[pl.ds(0, halves[a])] if part == 0 else whole.at[pl.ds(halves[a], shards[a].shape[0] - halves[a])]

        def copy(a, k, block, part, to, src=None):
            dst = rows(a, block, part)
            return pltpu.make_async_remote_copy(
                src_ref=dst if src is None else src, dst_ref=dst, send_sem=send_sems.at[9 * a + k],
                recv_sem=recv_sems.at[9 * a + k], device_id=to, device_id_type=MESH)

        flip = lambda dev: (dev[0], dev[1], 1 - dev[2])
        mine = [pltpu.make_async_copy(ins[a], rows(a, me, None), local_sems.at[a]) for a in range(n)]
        own = [[copy(a, 0, me, None, sib, src=ins[a]), copy(a, 1, me, None, xn, src=ins[a]),
                copy(a, 2, me, None, yn, src=ins[a])] for a in range(n)]
        relays = lambda a: [(1, [copy(a, 3, xn, 0, yn), copy(a, 5, xn, None, sib)]),
                            (2, [copy(a, 4, yn, 1, xn), copy(a, 6, yn, None, sib)]),
                            (3, [copy(a, 7, dg, 0, sib)]), (4, [copy(a, 8, dg, 1, sib)])]
        lands = {0: (sib, None), 1: (xn, None), 2: (yn, None), 3: (dg, 0), 4: (dg, 1), 5: (flip(xn), None),
                 6: (flip(yn), None), 7: (flip(dg), 0), 8: (flip(dg), 1)}
        arrival = lambda a, k: copy(a, k, lands[k][0], lands[k][1], me)
        return mine, own, relays, arrival

    def start(ins, outs, sems):
        mine, own, _, _ = body_parts(ins, outs, sems)
        for cp in mine + [cp for per in own for cp in per]:
            cp.start()

    def finish(ins, outs, sems):
        mine, own, relays, arrival = body_parts(ins, outs, sems)
        sent = [cp for per in own for cp in per]
        relays = [relays(a) for a in range(n)]
        for stage in range(4):
            for a in range(n):
                after, passes = relays[a][stage]
                arrival(a, after).wait_recv()
                for cp in passes:
                    cp.start()
                sent += passes
        for a in range(n):
            for k in (0, 5, 6, 7, 8):
                arrival(a, k).wait_recv()
        for cp in sent:
            cp.wait_send()
        for cp in mine:
            cp.wait()

    return _Side(shards, [_sds((N_DEV,) + s.shape, s.dtype) for s in shards],
                 [pltpu.SemaphoreType.DMA((9 * n,)), pltpu.SemaphoreType.DMA((9 * n,)), pltpu.SemaphoreType.DMA((n,))],
                 start, finish)


def _exchange_in_chip(gs):
    n = len(gs)

    def copies(ins, outs, sems):
        x, y, c, _ = _place()
        return [pltpu.make_async_remote_copy(
            src_ref=ins[a].at[2 * q + 1 - c], dst_ref=outs[a].at[q], send_sem=sems[0].at[4 * a + q],
            recv_sem=sems[1].at[4 * a + q], device_id=(x, y, 1 - c), device_id_type=MESH)
            for a in range(n) for q in range(4)]

    def start(ins, outs, sems):
        for cp in copies(ins, outs, sems):
            cp.start()

    def finish(ins, outs, sems):
        for cp in copies(ins, outs, sems):
            cp.wait()

    return _Side(gs, [_sds((4,) + g.shape[1:], g.dtype) for g in gs],
                 [pltpu.SemaphoreType.DMA((4 * n,)), pltpu.SemaphoreType.DMA((4 * n,))], start, finish)


def _exchange_between_chips(ps):
    n = len(ps)

    def copies(ins, outs, sems):
        x, y, c, chips = _place()
        return [pltpu.make_async_remote_copy(
            src_ref=ins[a].at[2 * cx + cy], dst_ref=outs[a].at[j], send_sem=sems[0].at[3 * a + j],
            recv_sem=sems[1].at[3 * a + j], device_id=(cx, cy, c), device_id_type=MESH)
            for a in range(n) for j, (cx, cy) in enumerate(chips)]

    def start(ins, outs, sems):
        for cp in copies(ins, outs, sems):
            cp.start()

    def finish(ins, outs, sems):
        for cp in copies(ins, outs, sems):
            cp.wait()

    return _Side(ps, [_sds((3,) + p.shape[1:], p.dtype) for p in ps],
                 [pltpu.SemaphoreType.DMA((3 * n,)), pltpu.SemaphoreType.DMA((3 * n,))], start, finish)


def _all_reduce_small(v, name):
    R = v.shape[0]

    def body(v_ref, sum_ref, all_ref, send_sems, recv_sems):
        x, y, c, _ = _place()
        k = 4 * x + 2 * y + c
        all_ref[k] = v_ref[...]
        copies = []
        for rel in range(1, N_DEV):
            fx, fy, fc = (rel >> 2) & 1, (rel >> 1) & 1, rel & 1
            peer = (1 - x if fx else x, 1 - y if fy else y, 1 - c if fc else c)
            copies.append(pltpu.make_async_remote_copy(
                src_ref=v_ref, dst_ref=all_ref.at[k], send_sem=send_sems.at[rel - 1], recv_sem=recv_sems.at[rel - 1],
                device_id=peer, device_id_type=MESH))
        for cp in copies:
            cp.start()
        for rel in range(1, N_DEV):
            fx, fy, fc = (rel >> 2) & 1, (rel >> 1) & 1, rel & 1
            src = 4 * (1 - x if fx else x) + 2 * (1 - y if fy else y) + (1 - c if fc else c)
            pltpu.make_async_remote_copy(
                src_ref=v_ref, dst_ref=all_ref.at[src], send_sem=send_sems.at[rel - 1], recv_sem=recv_sems.at[rel - 1],
                device_id=(x, y, c), device_id_type=MESH).wait_recv()
        for cp in copies:
            cp.wait_send()
        tot = all_ref[0]
        for d in range(1, N_DEV):
            tot = tot + all_ref[d]
        sum_ref[...] = tot

    vm = pl.BlockSpec(memory_space=pltpu.VMEM)
    return pl.pallas_call(
        body, name=name, in_specs=[vm], out_specs=[vm, vm],
        out_shape=[_sds((R, HEAD), F32), _sds((N_DEV, R, HEAD), F32)],
        scratch_shapes=[pltpu.SemaphoreType.DMA((N_DEV - 1,)), pltpu.SemaphoreType.DMA((N_DEV - 1,))],
    )(v)[0]


def _tiles(rows, cols):
    tc = 512 if cols % 512 == 0 else cols
    tr = rows
    while tr * tc * 4 > ELEMENTWISE_BLOCK_BYTES and tr % (2 * BF16_TILE_ROWS) == 0:
        tr //= 2
    return tr, tc


def _chip_sum(g, r1, place, name):
    _, R, Cc = g.shape
    tr, tc = _tiles(R, Cc)

    def body(place_ref, g_ref, r_ref, p_ref):
        p_ref[...] = (g_ref[...] + r_ref[...]).astype(BF16)

    chip = lambda k, place: k + (k >= place[1]).astype(jnp.int32)
    blk = lambda f: pl.BlockSpec((None, tr, tc), f)
    return pl.pallas_call(
        body, name=name,
        grid_spec=pltpu.PrefetchScalarGridSpec(
            num_scalar_prefetch=1, grid=(3, R // tr, Cc // tc),
            in_specs=[blk(lambda k, i, j, place: (2 * chip(k, place) + place[0], i, j)),
                      blk(lambda k, i, j, place: (chip(k, place), i, j))],
            out_specs=blk(lambda k, i, j, place: (chip(k, place), i, j))),
        out_shape=_sds((4, R, Cc), BF16), compiler_params=_cparams(),
    )(place, g, r1)


def _adamw_update(gv, w_ref, m_ref, v_ref, d_ref, nm_ref, nv_ref):
    nm = B1 * m_ref[...] + (1.0 - B1) * gv
    nv = B2 * v_ref[...] + (1.0 - B2) * jnp.square(gv)
    m_hat = nm / (1.0 - B1 ** STEP)
    v_hat = nv / (1.0 - B2 ** STEP)
    d_ref[...] = -LR * (m_hat / (jnp.sqrt(v_hat) + EPS) + WD * w_ref[...])
    nm_ref[...] = nm
    nv_ref[...] = nv


def _reduce_adamw(g, r1, r2, where, w, m, v, name):
    _, R, Cc = g.shape
    tr, tc = _tiles(R, Cc)

    def body(where_ref, g_ref, r1_ref, r2_ref, w_ref, m_ref, v_ref, o_ref, d_ref, nm_ref, nv_ref):
        gv = ((g_ref[...] + r1_ref[...]) + r2_ref[0].astype(F32)) + (r2_ref[1].astype(F32) + r2_ref[2].astype(F32))
        o_ref[...] = gv
        _adamw_update(gv, w_ref, m_ref, v_ref, d_ref, nm_ref, nv_ref)

    blk = pl.BlockSpec((tr, tc), lambda i, j, w: (i, j))
    return pl.pallas_call(
        body, name=name,
        grid_spec=pltpu.PrefetchScalarGridSpec(
            num_scalar_prefetch=1, grid=(R // tr, Cc // tc),
            in_specs=[pl.BlockSpec((None, tr, tc), lambda i, j, w: (w[0], i, j)),
                      pl.BlockSpec((None, tr, tc), lambda i, j, w: (w[1], i, j)),
                      pl.BlockSpec((3, tr, tc), lambda i, j, w: (0, i, j)), blk, blk, blk],
            out_specs=[blk] * 4),
        out_shape=[_sds((R, Cc), F32)] * 4, compiler_params=_cparams(),
    )(where, g, r1, r2, w, m, v)


def _adamw(w, g, m, v, name):
    R, Cc = w.shape
    tr, tc = _tiles(R, Cc)

    def body(w_ref, g_ref, m_ref, v_ref, d_ref, nm_ref, nv_ref):
        _adamw_update(g_ref[...], w_ref, m_ref, v_ref, d_ref, nm_ref, nv_ref)

    blk = pl.BlockSpec((tr, tc), lambda i, j: (i, j))
    return pl.pallas_call(
        body, name=name, grid=(R // tr, Cc // tc), in_specs=[blk] * 4, out_specs=[blk] * 3,
        out_shape=[_sds((R, Cc), F32)] * 3, compiler_params=_cparams(),
    )(w, g, m, v)


SMALL = ("ffn1_norm", "mix_norm", "ffn2_norm", "q_norm_a", "k_norm_a", "q_norm_b", "k_norm_b", "forget_bias",
         "rel_bias_table")
LARGE = ("ffn1_w_in", "ffn1_w_out", "w_in", "w_out", "ffn2_w_in", "ffn2_w_out")
ORDER = ("ffn1_norm", "ffn1_w_in", "ffn1_w_out", "mix_norm", "w_in", "q_norm_a", "k_norm_a", "q_norm_b", "k_norm_b",
         "forget_bias", "rel_bias_table", "w_out", "ffn2_norm", "ffn2_w_in", "ffn2_w_out")


def _pack_small(vals):
    rows = []
    for name in SMALL:
        flat = vals[name].reshape(-1)
        pad = (-flat.shape[0]) % HEAD
        rows.append(jnp.pad(flat, (0, pad)).reshape(-1, HEAD))
    return jnp.concatenate(rows, axis=0)


def _unpack_small(packed, like):
    out, r = {}, 0
    for name in SMALL:
        size = like[name].size
        nrow = -(-size // HEAD)
        out[name] = packed[r:r + nrow].reshape(-1)[:size].reshape(like[name].shape)
        r += nrow
    return out


def kernel(x, ffn1_norm, ffn1_w_in, ffn1_w_out, mix_norm, w_in, q_norm_a, k_norm_a, q_norm_b, k_norm_b, forget_bias, rel_bias_table, w_out, ffn2_norm, ffn2_w_in, ffn2_w_out, loss_target, m_ffn1_norm, m_ffn1_w_in, m_ffn1_w_out, m_mix_norm, m_w_in, m_q_norm_a, m_k_norm_a, m_q_norm_b, m_k_norm_b, m_forget_bias, m_rel_bias_table, m_w_out, m_ffn2_norm, m_ffn2_w_in, m_ffn2_w_out, v_ffn1_norm, v_ffn1_w_in, v_ffn1_w_out, v_mix_norm, v_w_in, v_q_norm_a, v_k_norm_a, v_q_norm_b, v_k_norm_b, v_forget_bias, v_rel_bias_table, v_w_out, v_ffn2_norm, v_ffn2_w_in, v_ffn2_w_out):
    w = dict(ffn1_norm=ffn1_norm, ffn1_w_in=ffn1_w_in, ffn1_w_out=ffn1_w_out, mix_norm=mix_norm, w_in=w_in,
             q_norm_a=q_norm_a, k_norm_a=k_norm_a, q_norm_b=q_norm_b, k_norm_b=k_norm_b, forget_bias=forget_bias,
             rel_bias_table=rel_bias_table, w_out=w_out, ffn2_norm=ffn2_norm, ffn2_w_in=ffn2_w_in, ffn2_w_out=ffn2_w_out)
    m = dict(ffn1_norm=m_ffn1_norm, ffn1_w_in=m_ffn1_w_in, ffn1_w_out=m_ffn1_w_out, mix_norm=m_mix_norm, w_in=m_w_in,
             q_norm_a=m_q_norm_a, k_norm_a=m_k_norm_a, q_norm_b=m_q_norm_b, k_norm_b=m_k_norm_b,
             forget_bias=m_forget_bias, rel_bias_table=m_rel_bias_table, w_out=m_w_out, ffn2_norm=m_ffn2_norm,
             ffn2_w_in=m_ffn2_w_in, ffn2_w_out=m_ffn2_w_out)
    v = dict(ffn1_norm=v_ffn1_norm, ffn1_w_in=v_ffn1_w_in, ffn1_w_out=v_ffn1_w_out, mix_norm=v_mix_norm, w_in=v_w_in,
             q_norm_a=v_q_norm_a, k_norm_a=v_k_norm_a, q_norm_b=v_q_norm_b, k_norm_b=v_k_norm_b,
             forget_bias=v_forget_bias, rel_bias_table=v_rel_bias_table, w_out=v_w_out, ffn2_norm=v_ffn2_norm,
             ffn2_w_in=v_ffn2_w_in, ffn2_w_out=v_ffn2_w_out)
    T, D = x.shape[1], x.shape[2]
    C = D // 2
    H = C // HEAD
    ff_shard = ffn1_w_out.shape[1]

    f1i, = _run_side(_all_gather_relayed([ffn1_w_in[0].T.astype(BF16)]), "gather_ffn1")
    wts = dict(ffn1_in=f1i.reshape(2, N_DEV, ff_shard, D))
    xi, yi, ci = lax.axis_index("x"), lax.axis_index("y"), lax.axis_index("c")
    place = jnp.stack([ci, 2 * xi + yi]).astype(jnp.int32)
    where = jnp.stack([4 * xi + 2 * yi + ci, 2 * xi + yi]).astype(jnp.int32)
    gs, r1, ps, r2 = {}, {}, {}, {}

    def shard(name):
        name, _, part = name.partition(":")
        s = (w[name][0].T if name.endswith("w_in") else w[name][0]).astype(BF16)
        first = -(-(s.shape[0] // 2) // BF16_TILE_ROWS) * BF16_TILE_ROWS
        return {"": s, "first": s[:first], "rest": s[first:]}[part]

    def by_destination(name, grads):
        key = name + "_t" if name.endswith("w_in") else name
        gs[name] = grads[key].reshape(N_DEV, -1, D)
        return grads.get(key + "_bf16", grads[key]).reshape(N_DEV, -1, D)

    def chip_sums(names):
        for name in names:
            ps[name] = _chip_sum(gs[name], r1[name], place, "chip_sum_" + name)
        return [ps[name] for name in names]

    class Plan:
        carried = {"ffn1_up": ("gather", ("ffn1_w_out", "w_in:first")), "ffn1_down": ("gather", ("w_in:rest", "w_out")),
                   "dil_fwd": ("gather", ("ffn2_w_out",)), "fox_fwd": ("gather", ("ffn2_w_in",)),
                   "dil_bwd": ("in_chip", ("ffn2_w_in", "ffn2_w_out")), "fox_bwd": ("between", ("ffn2_w_in", "ffn2_w_out")),
                   "ffn1_bwd": ("in_chip", ("w_in", "w_out")), "ffn1_dwin": ("between", ("w_in", "w_out")),
                   "ffn1_rms_bwd": ("in_chip", ("ffn1_w_in",)), "ffn1_dwout": ("between", ("ffn1_w_in",))}

        def before(self, host, wts, grads):
            if host not in self.carried:
                return None
            kind, names = self.carried[host]
            if kind == "gather":
                return _all_gather([shard(n) for n in names])
            if kind == "in_chip":
                return _exchange_in_chip([by_destination(n, grads) for n in names])
            return _exchange_between_chips(chip_sums(names))

        def after(self, host, res, wts, grads):
            kind, names = self.carried[host]
            if host == "ffn1_up":
                wts.update(ffn1_out=res[0], w_in_first_rows=res[1])
            elif host == "ffn1_down":
                w_in_t = jnp.concatenate([wts.pop("w_in_first_rows"), res[0]], axis=1).reshape(-1, D)
                wts.update(w_in_t=w_in_t, w_f_t=jnp.pad(w_in_t[6 * C:], ((0, HEAD - H), (0, 0))),
                           w_o=res[1].reshape(2 * C, D))
            elif host == "dil_fwd":
                wts.update(ffn2_out=res[0])
            elif host == "fox_fwd":
                wts.update(ffn2_in=res[0].reshape(2, N_DEV, ff_shard, D))
            else:
                (r1 if kind == "in_chip" else r2).update(zip(names, res))

    small = {name: w[name] for name in SMALL}
    loss_row, grad_x, grads = _local_step(x[0], loss_target[0], small, wts, Plan())

    tail = ("ffn1_w_out",)
    r1[tail[0]], = _run_side(_exchange_in_chip([by_destination(tail[0], grads)]), "reduce_in_chip_tail")
    r2.update(zip(tail, _run_side(_exchange_between_chips(chip_sums(tail)), "reduce_between_chips_tail")))

    packed = _pack_small(grads)
    nsmall = packed.shape[0]
    packed = jnp.concatenate([packed, loss_row, jnp.zeros(((-nsmall - 1) % 8, HEAD), F32)], axis=0)
    reduced = _all_reduce_small(packed, "reduce_small")
    loss = reduced[nsmall, 0]
    g_small = _unpack_small(reduced[:nsmall], small)

    grad, delta, new_m, new_v = dict(g_small), {}, {}, {}
    for name in LARGE:
        to = (lambda t: t[0].T) if name.endswith("w_in") else (lambda t: t[0])
        back = (lambda t: t.T[None]) if name.endswith("w_in") else (lambda t: t[None])
        res = _reduce_adamw(gs[name], r1[name], r2[name], where, to(w[name]), to(m[name]), to(v[name]), "adamw_" + name)
        grad[name], delta[name], new_m[name], new_v[name] = (back(t) for t in res)
    d, nm, nv = _adamw(_pack_small(w), reduced[:nsmall], _pack_small(m), _pack_small(v), "adamw_small")
    delta.update(_unpack_small(d, small))
    new_m.update(_unpack_small(nm, small))
    new_v.update(_unpack_small(nv, small))
    return (loss, grad_x[None], *[grad[n] for n in ORDER], *[delta[n] for n in ORDER],
            *[new_m[n] for n in ORDER], *[new_v[n] for n in ORDER])
```

```python
import functools
import math

import numpy as np
import jax
import jax.numpy as jnp
from jax import lax
from jax.experimental import pallas as pl
from jax.experimental.pallas import tpu as pltpu

F32, BF16 = jnp.float32, jnp.bfloat16
HEAD = 128
NSLAB = 16
BLK = 128
DILATIONS = (1, 4, 16)
NUM_BUCKETS, MAX_DISTANCE = 32, 2048
RMS_EPS = 1e-6
NEG = -1e30
SCALE = HEAD ** -0.5
LR, B1, B2, EPS, WD, STEP = 0.001, 0.9, 0.999, 1e-08, 0.01, 10
N_DEV = 8
VMEM_LIMIT_BYTES = 56 << 20
ELEMENTWISE_BLOCK_BYTES = 2 << 20
BF16_TILE_ROWS = 16
MESH = pl.DeviceIdType.MESH


def _cparams(**kw):
    return pltpu.CompilerParams(vmem_limit_bytes=VMEM_LIMIT_BYTES, **kw)


def _nn(a, b):
    return jnp.dot(a, b, preferred_element_type=F32)


def _nt(a, b):
    return lax.dot_general(a, b, (((1,), (1,)), ((), ())), preferred_element_type=F32)


def _tn(a, b):
    return lax.dot_general(a, b, (((0,), (0,)), ((), ())), preferred_element_type=F32)


def _sds(shape, dtype):
    return jax.ShapeDtypeStruct(shape, dtype)


ANY = pl.BlockSpec(memory_space=pl.ANY)


class _Side:
    def __init__(self, ins, outs, sems, start, finish):
        self.ins, self.outs, self.sems, self.start, self.finish = list(ins), list(outs), list(sems), start, finish


def _call(body, *, name, grid, in_specs, out_specs, out_shape, args, scratch_shapes=(), side=None):
    in_specs, out_specs, out_shape = list(in_specs), list(out_specs), list(out_shape)
    scratch_shapes = list(scratch_shapes)
    if side is None:
        return pl.pallas_call(body, name=name, grid=grid, in_specs=in_specs, out_specs=out_specs, out_shape=out_shape,
                              scratch_shapes=scratch_shapes, compiler_params=_cparams())(*args)
    ni, no, ns = len(args), len(out_shape), len(scratch_shapes)
    si, so = len(side.ins), len(side.outs)

    def fused(*refs):
        h_in, s_in = refs[:ni], refs[ni:ni + si]
        h_out, s_out = refs[ni + si:ni + si + no], refs[ni + si + no:ni + si + no + so]
        h_scr, s_sem = refs[ni + si + no + so:ni + si + no + so + ns], refs[ni + si + no + so + ns:]
        ids = [pl.program_id(k) for k in range(len(grid))]
        first = functools.reduce(jnp.logical_and, [i == 0 for i in ids])
        last = functools.reduce(jnp.logical_and, [i == n - 1 for i, n in zip(ids, grid)])

        @pl.when(first)
        def _():
            side.start(s_in, s_out, s_sem)

        body(*h_in, *h_out, *h_scr)

        @pl.when(last)
        def _():
            side.finish(s_in, s_out, s_sem)

    res = pl.pallas_call(
        fused, name=name, grid=grid, in_specs=in_specs + [ANY] * si, out_specs=out_specs + [ANY] * so,
        out_shape=out_shape + side.outs, scratch_shapes=scratch_shapes + side.sems, compiler_params=_cparams(),
    )(*args, *side.ins)
    return list(res[:no]), list(res[no:])


def _ffn_fwd(x, g, win, wout, tm, name, side=None, target=None):
    T, D = x.shape
    nc, tf = win.shape[1], win.shape[2]
    down, loss = wout is not None, target is not None
    assert down or not loss

    def body(x_ref, g_ref, win_ref, *refs):
        ins, outs = refs[:down + loss], refs[down + loss:]
        h_ref, gu_ref, act_ref = outs[down:down + 3]
        i, j = pl.program_id(0), pl.program_id(1)

        @pl.when(j == 0)
        def _():
            xv = x_ref[...]
            r = lax.rsqrt(jnp.mean(xv * xv, axis=-1, keepdims=True) + RMS_EPS)
            h_ref[...] = (xv * r * g_ref[...]).astype(BF16)
            if down:
                outs[0][...] = jnp.zeros_like(outs[0])

        if loss:
            @pl.when((i == 0) & (j == 0))
            def _():
                outs[-1][...] = jnp.zeros_like(outs[-1])

        hb = h_ref[...]
        gt = _nt(win_ref[0], hb)
        up = _nt(win_ref[1], hb)
        gu_ref[0] = gt.astype(BF16)
        gu_ref[1] = up.astype(BF16)
        act = (gt * jax.nn.sigmoid(gt) * up).astype(BF16)
        act_ref[...] = act
        if down:
            y_ref = outs[0]
            y_ref[...] += _tn(act, ins[0][...])

            @pl.when(j == nc - 1)
            def _():
                y = x_ref[...] + 0.5 * y_ref[...]
                if not loss:
                    y_ref[...] = y
                    return
                err = y - ins[1][...]
                y_ref[...] = err * (1.0 / D)
                tot = 0.5 * jnp.sum(jnp.mean(err * err, axis=-1, keepdims=True), axis=0, keepdims=True)
                lane = lax.broadcasted_iota(jnp.int32, (1, HEAD), 1)
                outs[-1][...] += jnp.where(lane == 0, tot, 0.0)

    row = pl.BlockSpec((tm, D), lambda i, j: (i, 0))
    return _call(
        body, name=name, grid=(T // tm, nc), side=side,
        in_specs=[row, pl.BlockSpec((1, D), lambda i, j: (0, 0)),
                  pl.BlockSpec((2, None, tf, D), lambda i, j: (0, j, 0, 0))]
        + ([pl.BlockSpec((None, tf, D), lambda i, j: (j, 0, 0))] if down else []) + ([row] if loss else []),
        out_specs=([row] if down else [])
        + [row, pl.BlockSpec((2, None, tf, tm), lambda i, j: (0, j, 0, i)),
           pl.BlockSpec((None, tf, tm), lambda i, j: (j, 0, i))]
        + ([pl.BlockSpec((1, HEAD), lambda i, j: (0, 0))] if loss else []),
        out_shape=([_sds((T, D), F32)] if down else [])
        + [_sds((T, D), BF16), _sds((2, nc, tf, T), BF16), _sds((nc, tf, T), BF16)]
        + ([_sds((1, HEAD), F32)] if loss else []),
        args=(x, g, win) + ((wout,) if down else ()) + ((target,) if loss else ()))


def _ffn_down(x, act, wout, tm, name, side=None):
    T, D = x.shape
    nc, tf = wout.shape[0], wout.shape[1]

    def body(x_ref, act_ref, wout_ref, y_ref):
        j = pl.program_id(1)

        @pl.when(j == 0)
        def _():
            y_ref[...] = jnp.zeros_like(y_ref)

        y_ref[...] += _tn(act_ref[...], wout_ref[...])

        @pl.when(j == nc - 1)
        def _():
            y_ref[...] = x_ref[...] + 0.5 * y_ref[...]

    row = pl.BlockSpec((tm, D), lambda i, j: (i, 0))
    res = _call(
        body, name=name, grid=(T // tm, nc), side=side,
        in_specs=[row, pl.BlockSpec((None, tf, tm), lambda i, j: (j, 0, i)),
                  pl.BlockSpec((None, tf, D), lambda i, j: (j, 0, 0))],
        out_specs=[row], out_shape=[_sds((T, D), F32)], args=(x, act, wout))
    return res[0] if side is None else (res[0][0], res[1])


def _ffn_bwd(dy, gu, win, wout, tm, name, side=None):
    T, D = dy.shape
    nc, tf = wout.shape[0], wout.shape[1]

    def body(dy_ref, gu_ref, win_ref, wout_ref, dh_ref, dgu_ref, dyb_ref):
        j = pl.program_id(1)

        @pl.when(j == 0)
        def _():
            dh_ref[...] = jnp.zeros_like(dh_ref)
            dyb_ref[...] = (0.5 * dy_ref[...]).astype(BF16)

        dact = _nt(wout_ref[...], dyb_ref[...])
        gt = gu_ref[0].astype(F32)
        up = gu_ref[1].astype(F32)
        s = jax.nn.sigmoid(gt)
        dgb = (dact * up * (s * (1.0 + gt * (1.0 - s)))).astype(BF16)
        dub = (dact * (gt * s)).astype(BF16)
        dgu_ref[0] = dgb
        dgu_ref[1] = dub
        dh_ref[...] += _tn(dgb, win_ref[0]) + _tn(dub, win_ref[1])

    return _call(
        body, name=name, grid=(T // tm, nc), side=side,
        in_specs=[pl.BlockSpec((tm, D), lambda i, j: (i, 0)),
                  pl.BlockSpec((2, None, tf, tm), lambda i, j: (0, j, 0, i)),
                  pl.BlockSpec((2, None, tf, D), lambda i, j: (0, j, 0, 0)),
                  pl.BlockSpec((None, tf, D), lambda i, j: (j, 0, 0))],
        out_specs=[pl.BlockSpec((tm, D), lambda i, j: (i, 0)),
                   pl.BlockSpec((2, None, tf, tm), lambda i, j: (0, j, 0, i)),
                   pl.BlockSpec((tm, D), lambda i, j: (i, 0))],
        out_shape=[_sds((T, D), F32), _sds((2, nc, tf, T), BF16), _sds((T, D), BF16)],
        args=(dy, gu, win, wout))


def _rms_bwd(x, g, dh, dres, tm, name, side=None):
    T, D = x.shape

    def body(x_ref, g_ref, dh_ref, dres_ref, dx_ref, dg_ref):
        @pl.when(pl.program_id(0) == 0)
        def _():
            dg_ref[...] = jnp.zeros_like(dg_ref)

        xv = x_ref[...]
        r = lax.rsqrt(jnp.mean(xv * xv, axis=-1, keepdims=True) + RMS_EPS)
        xhat = xv * r
        dh = dh_ref[...]
        gd = dh * g_ref[...]
        dx_ref[...] = dres_ref[...] + r * (gd - xhat * jnp.mean(gd * xhat, axis=-1, keepdims=True))
        dg_ref[...] += jnp.sum(dh * xhat, axis=0, keepdims=True)

    row = pl.BlockSpec((tm, D), lambda i: (i, 0))
    one = pl.BlockSpec((1, D), lambda i: (0, 0))
    return _call(body, name=name, grid=(T // tm,), side=side, in_specs=[row, one, row, row], out_specs=[row, one],
                 out_shape=[_sds((T, D), F32), _sds((1, D), F32)], args=(x, g, dh, dres))


def _mm_tn(a, b, *, bm, bn, bt, name, side=None, rows=None, m_off=0, into=None, a_rows=False, twin=False):
    nz, T, M = (a.shape[0], a.shape[2], a.shape[1]) if a_rows else a.shape
    N = b.shape[1]
    b_spec = pl.BlockSpec((bt, bn), lambda n, z, m, t: (t, n))
    assert M % bm == 0 and N % bn == 0 and T % bt == 0, (M, bm, N, bn, T, bt)

    def body(a_ref, b_ref, *rest):
        c_ref = rest[-2] if twin else rest[-1]

        @pl.when(pl.program_id(3) == 0)
        def _():
            c_ref[...] = jnp.zeros_like(c_ref)

        ab, bb = a_ref[...].astype(BF16), b_ref[...].astype(BF16)
        c_ref[...] += _nn(ab, bb) if a_rows else _tn(ab, bb)
        if twin:
            @pl.when(pl.program_id(3) == T // bt - 1)
            def _():
                rest[-1][...] = c_ref[...].astype(BF16)

    grid = (N // bn, nz, M // bm, T // bt)
    a_spec = (pl.BlockSpec((None, bm, bt), lambda n, z, m, t: (z, m, t)) if a_rows
              else pl.BlockSpec((None, bt, bm), lambda n, z, m, t: (z, t, m)))
    in_specs = [a_spec, b_spec]
    out_spec = pl.BlockSpec((None, bm, bn), lambda n, z, m, t: (z, m + m_off, n))
    out_shape = _sds((nz, M if rows is None else rows, N), F32)
    if into is not None:
        assert side is None and not twin and into.shape == out_shape.shape
        return pl.pallas_call(body, name=name, grid=grid, in_specs=in_specs + [ANY], out_specs=out_spec,
                              out_shape=out_shape, input_output_aliases={2: 0}, compiler_params=_cparams())(a, b, into)
    outs = [out_shape] + ([_sds(out_shape.shape, BF16)] if twin else [])
    res = _call(body, name=name, grid=grid, side=side, in_specs=in_specs, out_specs=[out_spec] * len(outs),
                out_shape=outs, args=(a, b))
    mine = res if side is None else res[0]
    mine = tuple(mine) if twin else mine[0]
    return mine if side is None else (mine, res[1])


def _tok_spec(layout, tm, n16, C, bc, colmap):
    if layout == "nat":
        return pl.BlockSpec((tm, bc), lambda i, k: (i, colmap(k)))
    assert tm % n16 == 0
    if layout == "slab":
        return pl.BlockSpec((tm // n16, n16, bc), lambda i, k: (i, 0, colmap(k)))
    assert bc == C
    return pl.BlockSpec((n16, (tm // n16) * C), lambda i, k: (0, i))


def _tok_load(ref, layout, sp):
    if layout == "nat":
        return ref[...]
    if layout == "slab":
        return ref[...].reshape(-1, ref.shape[-1])
    c = ref.shape[1] // sp
    return jnp.concatenate([ref[:, s * c:(s + 1) * c] for s in range(sp)], axis=0)


def _tok_store(ref, layout, sp, val, cols=None, accumulate=False):
    def put(idx, v):
        if accumulate:
            ref[idx] += v
        else:
            ref[idx] = v

    lanes = slice(None) if cols is None else slice(cols[0], cols[0] + cols[1])
    if layout == "nat":
        put((slice(None), lanes), val)
    elif layout == "slab":
        put((slice(None), slice(None), lanes), val.reshape(sp, ref.shape[1], val.shape[-1]))
    else:
        assert cols is None
        c, n = ref.shape[1] // sp, ref.shape[0]
        for s in range(sp):
            put((slice(None), slice(s * c, (s + 1) * c)), val[s * n:(s + 1) * n])


def _proj(x, g, wt, gains, modes, *, tn, w_off, slabs, tm, normed_dtype, name, small_wt=None):
    T, D = x.shape
    ntile = len(modes)
    N = ntile * tn
    n16 = T // NSLAB
    in_layout, out_layout = ("view", "slab") if slabs else ("nat", "nat")
    sp = tm // n16
    extra = int(small_wt is not None)
    assert not (extra and slabs)
    x_in = x.reshape(n16, NSLAB * D) if slabs else x
    x_spec = _tok_spec(in_layout, tm, n16, D, D, lambda n: 0)
    oshape = lambda c: (NSLAB, n16, c) if slabs else (T, c)
    ospec = lambda bc, cm: _tok_spec(out_layout, tm, n16, None, bc, cm)

    def body(x_ref, g_ref, w_ref, gains_ref, *refs):
        raw_ref, nrm_ref, h_ref = refs[-3 - extra:len(refs) - extra]
        n = pl.program_id(1)

        @pl.when(n == 0)
        def _():
            xv = _tok_load(x_ref, in_layout, sp)
            r = lax.rsqrt(jnp.mean(xv * xv, axis=-1, keepdims=True) + RMS_EPS)
            hb = (xv * r * g_ref[...]).astype(BF16)
            _tok_store(h_ref, out_layout, sp, hb)
            if extra:
                refs[-1][...] = _nt(hb, refs[0][...])

        y = _nt(_tok_load(h_ref, out_layout, sp), w_ref[...])
        _tok_store(raw_ref, out_layout, sp, y.astype(BF16))
        for t, mode in enumerate(modes):
            @pl.when(n == t)
            def _(t=t, mode=mode):
                if not mode:
                    _tok_store(nrm_ref, out_layout, sp, y.astype(nrm_ref.dtype))
                    return
                gain = gains_ref[t]
                for k in range(tn // HEAD):
                    yk = y[:, k * HEAD:(k + 1) * HEAD]
                    r = lax.rsqrt(jnp.mean(yk * yk, axis=-1, keepdims=True) + RMS_EPS)
                    _tok_store(nrm_ref, out_layout, sp, (yk * r * gain).astype(nrm_ref.dtype), cols=(k * HEAD, HEAD))

    return pl.pallas_call(
        body, name=name, grid=(T // tm, ntile),
        in_specs=[x_spec, pl.BlockSpec((1, D), lambda i, n: (0, 0)),
                  pl.BlockSpec((tn, D), lambda i, n: (n + w_off, 0)),
                  pl.BlockSpec((ntile, 1, HEAD), lambda i, n: (0, 0, 0))]
        + ([pl.BlockSpec((HEAD, D), lambda i, n: (0, 0))] if extra else []),
        out_specs=[ospec(tn, lambda n: n), ospec(tn, lambda n: n), ospec(D, lambda n: 0)]
        + ([ospec(HEAD, lambda n: 0)] if extra else []),
        out_shape=[_sds(oshape(N), BF16), _sds(oshape(N), normed_dtype), _sds(oshape(D), BF16)]
        + ([_sds((T, HEAD), F32)] if extra else []),
        compiler_params=_cparams(),
    )(x_in, g, wt, gains, *([small_wt] if extra else []))


def _mm(a, w, *, nt, tk, tm, a_layout, out_layout, resid=None, name, w_off=0, n_out=None, small=None):
    if a_layout == "slab":
        T, K = a.shape[0] * a.shape[1], a.shape[2]
    else:
        T, K = a.shape
    N = (w.shape[0] if nt else w.shape[1]) if n_out is None else n_out
    n16 = T // NSLAB
    nk = K // tk
    sp = tm // n16
    a_in = a.reshape(n16, NSLAB * K) if a_layout == "view" else a
    w_spec = (pl.BlockSpec((N, tk), lambda i, k: (w_off, k)) if nt
              else pl.BlockSpec((tk, N), lambda i, k: (k + w_off, 0)))
    o_spec = _tok_spec(out_layout, tm, n16, N, N, lambda k: 0)
    oshape = {"nat": (T, N), "slab": (NSLAB, n16, N), "view": (n16, NSLAB * N)}[out_layout]
    has_resid = resid is not None

    def body(*refs):
        a_ref, w_ref = refs[0], refs[1]
        o_ref = refs[-1]
        k = pl.program_id(1)

        @pl.when(k == 0)
        def _():
            o_ref[...] = refs[2][...] if has_resid else jnp.zeros_like(o_ref)
            if small is not None:
                o_ref[...] += _nn(refs[-3][...].astype(BF16), refs[-2][...])

        ab = _tok_load(a_ref, a_layout, sp).astype(BF16)
        _tok_store(o_ref, out_layout, sp, _nt(ab, w_ref[...]) if nt else _nn(ab, w_ref[...]), accumulate=True)

    ins = [a_in, w]
    in_specs = [_tok_spec(a_layout, tm, n16, K, tk, lambda k: k), w_spec]
    if has_resid:
        ins.append(resid.reshape(n16, NSLAB * N) if out_layout == "view" else resid)
        in_specs.append(o_spec)
    if small is not None:
        assert out_layout == "nat" and small[1].shape == (HEAD, N)
        ins += list(small)
        in_specs += [pl.BlockSpec((tm, HEAD), lambda i, k: (i, 0)), pl.BlockSpec((HEAD, N), lambda i, k: (0, 0))]
    out = pl.pallas_call(
        body, name=name, grid=(T // tm, nk), in_specs=in_specs, out_specs=o_spec,
        out_shape=_sds(oshape, F32), compiler_params=_cparams(),
    )(*ins)
    return out.reshape(T, N) if out_layout == "view" else out


def _log_sigmoid(z):
    return jnp.minimum(z, 0.0) - jnp.log(1.0 + jnp.exp(-jnp.abs(z)))


def _fox_gate_fwd(f_raw, fbias, name):
    T = f_raw.shape[0]
    cb = 256

    def body(f_ref, b_ref, c_ref):
        row = lax.broadcasted_iota(jnp.int32, (cb, cb), 0)
        col = lax.broadcasted_iota(jnp.int32, (cb, cb), 1)
        tri = (col <= row).astype(F32)
        carry = jnp.zeros((1, HEAD), F32)
        for i in range(T // cb):
            lf = _log_sigmoid(f_ref[i * cb:(i + 1) * cb, :] + b_ref[...])
            c = jnp.dot(tri, lf, preferred_element_type=F32, precision=lax.Precision.HIGHEST) + carry
            c_ref[i * cb:(i + 1) * cb, :] = c
            carry = c[cb - 1:cb, :]

    return pl.pallas_call(body, name=name, out_shape=_sds((T, HEAD), F32), compiler_params=_cparams())(f_raw, fbias)


def _fox_gate_bwd(f_raw, fbias, dc, name):
    T = f_raw.shape[0]
    cb = 256

    def body(f_ref, b_ref, dc_ref, df_ref, db_ref):
        row = lax.broadcasted_iota(jnp.int32, (cb, cb), 0)
        col = lax.broadcasted_iota(jnp.int32, (cb, cb), 1)
        tri = (col >= row).astype(F32)
        carry = jnp.zeros((1, HEAD), F32)
        dbias = jnp.zeros((1, HEAD), F32)
        for i in reversed(range(T // cb)):
            dlf = jnp.dot(tri, dc_ref[i * cb:(i + 1) * cb, :], preferred_element_type=F32,
                          precision=lax.Precision.HIGHEST) + carry
            carry = dlf[0:1, :]
            z = f_ref[i * cb:(i + 1) * cb, :] + b_ref[...]
            df = dlf * jax.nn.sigmoid(-z)
            df_ref[i * cb:(i + 1) * cb, :] = df
            dbias = dbias + jnp.sum(df, axis=0, keepdims=True)
        db_ref[...] = dbias

    return pl.pallas_call(body, name=name, out_shape=[_sds((T, HEAD), F32), _sds((1, HEAD), F32)],
                          compiler_params=_cparams())(f_raw, fbias, dc)


def _fox_fwd(qkv, c_col, c_row, tq, name, side=None):
    T = qkv.shape[0]
    H = qkv.shape[1] // (3 * HEAD)
    nq = T // tq
    c_blocks = c_row.reshape(H, nq, 1, tq)

    def body(q_ref, k_ref, v_ref, cq_ref, ck_ref, o_ref, lse_ref):
        qi = pl.program_id(1)
        q, cq = q_ref[...], cq_ref[...]
        causal = lax.broadcasted_iota(jnp.int32, (tq, tq), 1) <= lax.broadcasted_iota(jnp.int32, (tq, tq), 0)

        def key_block(ki, carry, diagonal):
            m, l, acc = carry
            rows = pl.ds(pl.multiple_of(ki * tq, tq), tq)
            s = _nt(q, k_ref[rows, :]) * SCALE + cq - ck_ref[ki]
            if diagonal:
                s = jnp.where(causal, s, NEG)
            m_new = jnp.maximum(m, jnp.max(s, axis=-1, keepdims=True))
            alpha = jnp.exp(m - m_new)
            p = jnp.exp(s - m_new)
            l = alpha * l + jnp.sum(p, axis=-1, keepdims=True)
            acc = alpha * acc + _nn(p.astype(BF16), v_ref[rows, :])
            return m_new, l, acc

        init = (jnp.full((tq, 1), NEG, F32), jnp.zeros((tq, 1), F32), jnp.zeros((tq, HEAD), F32))
        carry = lax.fori_loop(0, qi, lambda ki, c: key_block(ki, c, False), init)
        m, l, acc = key_block(qi, carry, True)
        o_ref[...] = acc / l
        lse_ref[...] = m + jnp.log(l)

    return _call(
        body, name=name, grid=(H, nq), side=side,
        in_specs=[pl.BlockSpec((tq, HEAD), lambda h, qi: (qi, h)),
                  pl.BlockSpec((T, HEAD), lambda h, qi: (0, H + h)),
                  pl.BlockSpec((T, HEAD), lambda h, qi: (0, 2 * H + h)),
                  pl.BlockSpec((None, tq, 1), lambda h, qi: (h, qi, 0)),
                  pl.BlockSpec((None, nq, 1, tq), lambda h, qi: (h, 0, 0, 0))],
        out_specs=[pl.BlockSpec((tq, HEAD), lambda h, qi: (qi, h)),
                   pl.BlockSpec((None, tq, 1), lambda h, qi: (h, qi, 0))],
        out_shape=[_sds((T, H * HEAD), F32), _sds((H, T, 1), F32)],
        args=(qkv, qkv, qkv, c_col, c_blocks))


def _fox_bwd(qkv, c_col, c_row, out, dout, lse, tq, name, side=None, do_off=0):
    T = qkv.shape[0]
    H = qkv.shape[1] // (3 * HEAD)
    nq = T // tq

    def body(q_ref, k_ref, v_ref, cq_ref, ck_ref, o_ref, do_ref, lse_ref, dq_ref, dk_ref, dv_ref, dck_ref, dcq_ref,
             delta_s):
        ki = pl.program_id(1)

        @pl.when(ki == 0)
        def _():
            dq_ref[...] = jnp.zeros_like(dq_ref)
            dcq_ref[...] = jnp.zeros_like(dcq_ref)
            delta_s[...] = jnp.sum(do_ref[...] * o_ref[...], axis=-1, keepdims=True)

        k, v, ck = k_ref[...], v_ref[...], ck_ref[...]
        causal = lax.broadcasted_iota(jnp.int32, (tq, tq), 1) <= lax.broadcasted_iota(jnp.int32, (tq, tq), 0)

        def query_block(qi, carry, diagonal):
            dk, dv, dck = carry
            rows = pl.ds(pl.multiple_of(qi * tq, tq), tq)
            q = q_ref[rows, :]
            s = _nt(q, k) * SCALE + cq_ref[rows, :] - ck
            if diagonal:
                s = jnp.where(causal, s, NEG)
            p = jnp.exp(s - lse_ref[rows, :])
            dob = do_ref[rows, :].astype(BF16)
            ds = p * (_nt(dob, v) - delta_s[rows, :])
            dsb = ds.astype(BF16)
            dq_ref[rows, :] += _nn(dsb, k) * SCALE
            dcq_ref[rows, :] += jnp.sum(ds, axis=-1, keepdims=True)
            return dk + _tn(dsb, q), dv + _tn(p.astype(BF16), dob), dck - jnp.sum(ds, axis=0, keepdims=True)

        init = (jnp.zeros((tq, HEAD), F32), jnp.zeros((tq, HEAD), F32), jnp.zeros((1, tq), F32))
        carry = query_block(ki, init, True)
        dk, dv, dck = lax.fori_loop(ki + 1, nq, lambda qi, c: query_block(qi, c, False), carry)
        dk_ref[...] = (dk * SCALE).astype(BF16)
        dv_ref[...] = dv.astype(BF16)
        dck_ref[...] = dck

    head = lambda off: pl.BlockSpec((T, HEAD), lambda h, ki: (0, off + h))
    col = pl.BlockSpec((None, T, 1), lambda h, ki: (h, 0, 0))
    return _call(
        body, name=name, grid=(H, nq), side=side,
        in_specs=[head(0),
                  pl.BlockSpec((tq, HEAD), lambda h, ki: (ki, H + h)),
                  pl.BlockSpec((tq, HEAD), lambda h, ki: (ki, 2 * H + h)),
                  col, pl.BlockSpec((None, 1, tq), lambda h, ki: (h, 0, ki)), head(0), head(do_off), col],
        out_specs=[head(0),
                   pl.BlockSpec((tq, HEAD), lambda h, ki: (ki, h)),
                   pl.BlockSpec((tq, HEAD), lambda h, ki: (ki, h)),
                   pl.BlockSpec((None, 1, tq), lambda h, ki: (h, 0, ki)), col],
        out_shape=[_sds((T, H * HEAD), F32), _sds((T, H * HEAD), BF16), _sds((T, H * HEAD), BF16), _sds((H, 1, T), F32),
                   _sds((H, T, 1), F32)],
        scratch_shapes=[pltpu.VMEM((T, 1), F32)],
        args=(qkv, qkv, qkv, c_col, c_row, out, dout, lse))


def _t5_bucket(dist):
    max_exact = NUM_BUCKETS // 2
    d = dist.astype(np.float32)
    large = max_exact + (np.log(np.maximum(d, np.float32(1.0)) / np.float32(max_exact))
                         / np.float32(math.log(MAX_DISTANCE / max_exact))
                         * np.float32(NUM_BUCKETS - max_exact)).astype(np.int32)
    large = np.minimum(large, NUM_BUCKETS - 1)
    return np.where(dist < max_exact, dist, large)


def _bucket_maps():
    maps = []
    for d in DILATIONS:
        e = NSLAB // d
        rows = BLK // e
        idx = np.arange(BLK)
        pos = e * (idx % rows) + idx // rows
        qpos = pos[:, None] + BLK
        kpos = np.concatenate([pos, pos + BLK])[None, :]
        delta = qpos - kpos
        band = (delta >= 0) & (delta <= BLK)
        bucket = _t5_bucket(np.clip(delta, 0, None) * d)
        maps.append(np.where(band, bucket, -1).astype(np.int32))
    return np.stack(maps)


def _dil_geometry(T):
    n16 = T // NSLAB
    geo = []
    for d in DILATIONS:
        e = NSLAB // d
        rows = BLK // e
        nblk = n16 // rows
        geo.append((d, e, rows, nblk))
    return geo


DIL_INTERLEAVE_FWD = {1: 4, 4: 8, 16: 8}
DIL_INTERLEAVE_BWD = {1: 8, 4: 8, 16: 8}


def _dil_interleave(per_step, nblocks):
    while per_step > 1 and (nblocks % per_step or nblocks // per_step < 2):
        per_step -= 1
    return per_step


def _dil_bias(tab_ref, bkt_ref, bias_s, h):
    for p in range(len(DILATIONS)):
        bk = bkt_ref[p]
        bias = jnp.full((BLK, 2 * BLK), NEG, F32)
        for b in range(NUM_BUCKETS):
            bias = jnp.where(bk == b, tab_ref[b, h], bias)
        bias_s[p] = bias


def _dil_rows(d, e, rows, sub, blk):
    start = pl.multiple_of(blk * rows, rows)
    return [(sub + d * j, pl.ds(start, rows)) for j in range(e)]


def _gather(ref, idx):
    return jnp.concatenate([ref[s, r, :] for s, r in idx], axis=0)


def _scatter(ref, idx, val, rows):
    for j, (s, r) in enumerate(idx):
        ref[s, r, :] = val[j * rows:(j + 1) * rows]


def _scatter_add(ref, idx, val, rows):
    for j, (s, r) in enumerate(idx):
        ref[s, r, :] += val[j * rows:(j + 1) * rows]


def _dil_fwd(qkv, table, name, side=None):
    n16 = qkv.shape[1]
    T = NSLAB * n16
    H = qkv.shape[2] // (3 * HEAD)
    geo = _dil_geometry(T)
    bkt = jnp.asarray(_bucket_maps())

    def body(tab_ref, bkt_ref, q_ref, k_ref, v_ref, o_ref, lse_ref, onat_ref, bias_s, m_s, l_s):
        h = pl.program_id(0)
        _dil_bias(tab_ref, bkt_ref, bias_s, h)
        first_mask = lax.broadcasted_iota(jnp.int32, (BLK, 2 * BLK), 1) < BLK

        starts = len(DILATIONS) - 1

        def load(p, d, e, rows, sub, blk):
            cur = _dil_rows(d, e, rows, sub, blk)
            prev = _dil_rows(d, e, rows, sub, jnp.maximum(blk - 1, 0))
            q = _gather(q_ref, cur).astype(BF16)
            kk = jnp.concatenate([_gather(k_ref, prev), _gather(k_ref, cur)], axis=0).astype(BF16)
            vv = jnp.concatenate([_gather(v_ref, prev), _gather(v_ref, cur)], axis=0).astype(BF16)
            old = None if p == starts else (_gather(m_s, cur), _gather(l_s, cur), _gather(o_ref, cur))
            return cur, blk, q, kk, vv, old

        def compute(p, blk, q, kk, vv, old):
            s = _nt(q, kk) * SCALE + bias_s[p]
            s = jnp.where(first_mask & (blk == 0), NEG, s)
            m_blk = jnp.max(s, axis=-1, keepdims=True)
            if old is None:
                m_new = m_blk
                pr = jnp.exp(s - m_new)
                l_new = jnp.sum(pr, axis=-1, keepdims=True)
                acc = _nn(pr.astype(BF16), vv)
            else:
                m_old, l_old, acc_old = old
                m_new = jnp.maximum(m_old, m_blk)
                alpha = jnp.exp(m_old - m_new)
                pr = jnp.exp(s - m_new)
                l_new = alpha * l_old + jnp.sum(pr, axis=-1, keepdims=True)
                acc = alpha * acc_old + _nn(pr.astype(BF16), vv)
            if p == 0:
                return acc / l_new, m_new + jnp.log(l_new), None
            return acc, m_new, l_new

        def store(p, rows, cur, acc, m_new, l_new):
            _scatter(o_ref, cur, acc, rows)
            if p == 0:
                _scatter(lse_ref, cur, m_new, rows)
            else:
                _scatter(m_s, cur, m_new, rows)
                _scatter(l_s, cur, l_new, rows)

        for p in reversed(range(len(DILATIONS))):
            d, e, rows, nblk = geo[p]
            per_step = _dil_interleave(DIL_INTERLEAVE_FWD[d], d * nblk)

            def step(i, carry, p=p, d=d, e=e, rows=rows, nblk=nblk, per_step=per_step):
                ids = [i + u * (d * nblk // per_step) for u in range(per_step)]
                loaded = [load(p, d, e, rows, j // nblk, j % nblk) for j in ids]
                done = [(cur, compute(p, blk, q, kk, vv, old)) for cur, blk, q, kk, vv, old in loaded]
                for cur, res in done:
                    store(p, rows, cur, *res)
                return carry

            lax.fori_loop(0, d * nblk // per_step, step, 0)

        for r in range(NSLAB):
            onat_ref[pl.ds(r, n16, stride=NSLAB), :] = o_ref[r]

    head = lambda off: pl.BlockSpec((NSLAB, n16, HEAD), lambda h: (0, 0, off + h))
    return _call(
        body, name=name, grid=(H,), side=side,
        in_specs=[pl.BlockSpec(memory_space=pltpu.SMEM), pl.BlockSpec((3, BLK, 2 * BLK), lambda h: (0, 0, 0)),
                  head(0), head(H), head(2 * H)],
        out_specs=[head(0), pl.BlockSpec((None, NSLAB, n16, 1), lambda h: (h, 0, 0, 0)),
                   pl.BlockSpec((T, HEAD), lambda h: (0, h))],
        out_shape=[_sds((NSLAB, n16, H * HEAD), F32), _sds((H, NSLAB, n16, 1), F32), _sds((T, H * HEAD), F32)],
        scratch_shapes=[pltpu.VMEM((3, BLK, 2 * BLK), F32), pltpu.VMEM((NSLAB, n16, 1), F32),
                        pltpu.VMEM((NSLAB, n16, 1), F32)],
        args=(table, bkt, qkv, qkv, qkv))


def _dil_bwd(qkv, table, out, dout, lse, name, side=None):
    n16 = qkv.shape[1]
    T = NSLAB * n16
    H = qkv.shape[2] // (3 * HEAD)
    geo = _dil_geometry(T)
    bkt = jnp.asarray(_bucket_maps())

    def body(tab_ref, bkt_ref, q_ref, k_ref, v_ref, o_ref, dnat_ref, lse_ref,
             dq_ref, dk_ref, dv_ref, dtab_ref, bias_s, dbias_s, delta_s, do_ref):
        h = pl.program_id(0)
        _dil_bias(tab_ref, bkt_ref, bias_s, h)
        first_mask = lax.broadcasted_iota(jnp.int32, (BLK, 2 * BLK), 1) < BLK
        dbias_s[...] = jnp.zeros_like(dbias_s)
        dq_ref[...] = jnp.zeros_like(dq_ref)
        dk_ref[...] = jnp.zeros_like(dk_ref)
        dv_ref[...] = jnp.zeros_like(dv_ref)
        for r in range(NSLAB):
            do_ref[r] = dnat_ref[pl.ds(r, n16, stride=NSLAB), :]
            delta_s[r] = jnp.sum(do_ref[r] * o_ref[r], axis=-1, keepdims=True)

        def load(d, e, rows, sub, blk):
            cur = _dil_rows(d, e, rows, sub, blk)
            prev = _dil_rows(d, e, rows, sub, jnp.maximum(blk - 1, 0))
            q = _gather(q_ref, cur).astype(BF16)
            kk = jnp.concatenate([_gather(k_ref, prev), _gather(k_ref, cur)], axis=0).astype(BF16)
            vv = jnp.concatenate([_gather(v_ref, prev), _gather(v_ref, cur)], axis=0).astype(BF16)
            dob = _gather(do_ref, cur).astype(BF16)
            return cur, prev, blk, q, kk, vv, dob, _gather(lse_ref, cur), _gather(delta_s, cur)

        def compute(p, blk, q, kk, vv, dob, lse, delta):
            s = _nt(q, kk) * SCALE + bias_s[p]
            s = jnp.where(first_mask & (blk == 0), NEG, s)
            pr = jnp.exp(s - lse)
            ds = pr * (_nt(dob, vv) - delta)
            dsb = ds.astype(BF16)
            return ds, _nn(dsb, kk) * SCALE, _tn(dsb, q) * SCALE, _tn(pr.astype(BF16), dob)

        def store(rows, cur, prev, dq, dkk, dvv):
            _scatter_add(dq_ref, cur, dq, rows)
            _scatter_add(dk_ref, prev, dkk[:BLK], rows)
            _scatter_add(dk_ref, cur, dkk[BLK:], rows)
            _scatter_add(dv_ref, prev, dvv[:BLK], rows)
            _scatter_add(dv_ref, cur, dvv[BLK:], rows)

        for p in range(len(DILATIONS)):
            d, e, rows, nblk = geo[p]
            per_step = _dil_interleave(DIL_INTERLEAVE_BWD[d], d * nblk)

            def step(i, carry, p=p, d=d, e=e, rows=rows, nblk=nblk, per_step=per_step):
                ids = [i + u * (d * nblk // per_step) for u in range(per_step)]
                loaded = [load(d, e, rows, j // nblk, j % nblk) for j in ids]
                done = [(cur, prev, compute(p, *rest)) for cur, prev, *rest in loaded]
                dbias_s[p] += functools.reduce(jnp.add, [res[0] for _, _, res in done])
                for cur, prev, res in done:
                    store(rows, cur, prev, *res[1:])
                return carry

            lax.fori_loop(0, d * nblk // per_step, step, 0)

        lane = lax.broadcasted_iota(jnp.int32, (1, HEAD), 1)
        row = jnp.zeros((1, HEAD), F32)
        for b in range(NUM_BUCKETS):
            tot = jnp.zeros((1, 1), F32)
            for p in range(len(DILATIONS)):
                hit = jnp.where(bkt_ref[p] == b, dbias_s[p], 0.0)
                tot = tot + jnp.sum(jnp.sum(hit, axis=0, keepdims=True), axis=1, keepdims=True)
            row = jnp.where(lane == b, tot, row)
        dtab_ref[...] = row

    head = lambda off: pl.BlockSpec((NSLAB, n16, HEAD), lambda h: (0, 0, off + h))
    return _call(
        body, name=name, grid=(H,), side=side,
        in_specs=[pl.BlockSpec(memory_space=pltpu.SMEM), pl.BlockSpec((3, BLK, 2 * BLK), lambda h: (0, 0, 0)),
                  head(0), head(H), head(2 * H), head(0), pl.BlockSpec((T, HEAD), lambda h: (0, h)),
                  pl.BlockSpec((None, NSLAB, n16, 1), lambda h: (h, 0, 0, 0))],
        out_specs=[head(0), head(0), head(0), pl.BlockSpec((None, 1, HEAD), lambda h: (h, 0, 0))],
        out_shape=[_sds((NSLAB, n16, H * HEAD), F32)] * 3 + [_sds((H, 1, HEAD), F32)],
        scratch_shapes=[pltpu.VMEM((3, BLK, 2 * BLK), F32), pltpu.VMEM((3, BLK, 2 * BLK), F32),
                        pltpu.VMEM((NSLAB, n16, 1), F32), pltpu.VMEM((NSLAB, n16, HEAD), F32)],
        args=(table, bkt, qkv, qkv, qkv, out, dout, lse))


def _qknorm_bwd(raw, dq, dk, dv, gains, tm, name):
    T, N = raw.shape
    C = N // 3

    def body(raw_ref, dq_ref, dk_ref, dv_ref, gains_ref, dp_ref, dg_ref):
        @pl.when(pl.program_id(0) == 0)
        def _():
            dg_ref[...] = jnp.zeros_like(dg_ref)

        for t, d_ref in enumerate((dq_ref, dk_ref)):
            gain = gains_ref[t]
            dgain = jnp.zeros((1, HEAD), F32)
            for k in range(C // HEAD):
                y = raw_ref[:, t * C + k * HEAD:t * C + (k + 1) * HEAD].astype(F32)
                dn = d_ref[:, k * HEAD:(k + 1) * HEAD]
                r = lax.rsqrt(jnp.mean(y * y, axis=-1, keepdims=True) + RMS_EPS)
                yhat = y * r
                gd = dn * gain
                dy = r * (gd - yhat * jnp.mean(gd * yhat, axis=-1, keepdims=True))
                dp_ref[:, t * C + k * HEAD:t * C + (k + 1) * HEAD] = dy.astype(BF16)
                dgain = dgain + jnp.sum(dn * yhat, axis=0, keepdims=True)
            dg_ref[t] += dgain
        dp_ref[:, 2 * C:] = dv_ref[...].astype(BF16)

    third = pl.BlockSpec((tm, C), lambda i: (i, 0))
    return pl.pallas_call(
        body, name=name, grid=(T // tm,),
        in_specs=[pl.BlockSpec((tm, N), lambda i: (i, 0)), third, third, third,
                  pl.BlockSpec((2, 1, HEAD), lambda i: (0, 0, 0))],
        out_specs=[pl.BlockSpec((tm, N), lambda i: (i, 0)), pl.BlockSpec((2, 1, HEAD), lambda i: (0, 0, 0))],
        out_shape=[_sds((T, N), BF16), _sds((2, 1, HEAD), F32)], compiler_params=_cparams(),
    )(raw, dq, dk, dv, gains)


def _pad_lanes(v, width=HEAD):
    return jnp.pad(v, ((0, 0), (0, width - v.shape[1])))


def _local_step(x, target, small, wts, plan=None):
    grads = {}

    def hosted(host, fn, *args, **kw):
        side = plan.before(host, wts, grads) if plan is not None else None
        if side is None:
            return fn(*args, name=host, **kw)
        res, side_res = fn(*args, name=host, side=side, **kw)
        plan.after(host, side_res, wts, grads)
        return res

    T, D = x.shape
    C = D // 2
    H = C // HEAD
    n16 = T // NSLAB
    tm = min(512, T)
    tmm = min(1024, T)
    tms = 4 * n16
    tq = min(512, T)
    bn = min(1024, D)
    g1, gm, g2 = small["ffn1_norm"], small["mix_norm"], small["ffn2_norm"]
    gains_a = jnp.stack([small["q_norm_a"], small["k_norm_a"], jnp.ones_like(small["q_norm_a"])])
    gains_b = jnp.stack([small["q_norm_b"], small["k_norm_b"], jnp.ones_like(small["q_norm_b"])])
    fbias = _pad_lanes(small["forget_bias"])
    table = small["rel_bias_table"]

    h1, gu1, act1 = hosted("ffn1_up", _ffn_fwd, x, g1, wts["ffn1_in"], None, tmm)
    x1 = hosted("ffn1_down", _ffn_down, x, act1, wts["ffn1_out"], tmm)
    w_in_t, w_f_t, w_o = wts["w_in_t"], wts["w_f_t"], wts["w_o"]
    raw_a, nrm_a, h2a = _proj(x1, gm, w_in_t, gains_a, (True, True, False), tn=C, w_off=0, slabs=True, tm=tms,
                              normed_dtype=F32, name="proj_a")
    raw_b, nrm_b, h2b, f_raw = _proj(x1, gm, w_in_t, gains_b, (True, True, False), tn=C, w_off=3, slabs=False, tm=tmm,
                                     normed_dtype=BF16, name="proj_b", small_wt=w_f_t)
    c = _fox_gate_fwd(f_raw, fbias, "fox_gate_fwd")
    c_heads = c[:, :H].T
    c_col, c_row = c_heads[:, :, None], c_heads[:, None, :]
    out_a, lse_a, out_a_nat = hosted("dil_fwd", _dil_fwd, nrm_a, table)
    out_b, lse_b = hosted("fox_fwd", _fox_fwd, nrm_b, c_col, c_row, tq)
    x2a = _mm(out_a_nat, w_o, nt=False, tk=C, tm=tmm, a_layout="nat", out_layout="nat", resid=x1, name="out_a")
    x2 = _mm(out_b, w_o, nt=False, tk=C, tm=tmm, a_layout="nat", out_layout="nat", resid=x2a, w_off=1, name="out_b")
    dy, h3, gu3, act3, loss_row = _ffn_fwd(x2, g2, wts["ffn2_in"], wts["ffn2_out"], tm, "ffn2_fwd", target=target)

    def ffn_backward(tag, xin, g, h, gu, act, win, wout, dres):
        nc, tf = wout.shape[0], wout.shape[1]
        dh, dgu, dyb = hosted(tag + "_bwd", _ffn_bwd, dres, gu, win, wout, tm)
        grads[tag + "_w_in_t"], grads[tag + "_w_in_t_bf16"] = hosted(
            tag + "_dwin", _mm_tn, dgu.reshape(2 * nc, tf, T), h, bm=tf, bn=bn, bt=T, a_rows=True, twin=True)
        dxin, grads[tag + "_norm"] = hosted(tag + "_rms_bwd", _rms_bwd, xin, g, dh, dres, tm)
        grads[tag + "_w_out"], grads[tag + "_w_out_bf16"] = hosted(
            tag + "_dwout", _mm_tn, act, dyb, bm=tf, bn=bn, bt=T, a_rows=True, twin=True)
        return dxin

    dx2 = ffn_backward("ffn2", x2, g2, h3, gu3, act3, wts["ffn2_in"], wts["ffn2_out"], dy)

    dmix = _mm(dx2, w_o, nt=True, tk=D, tm=tmm, a_layout="nat", out_layout="nat", name="dmix")
    dwo = _mm_tn(out_a_nat.reshape(1, T, C), dx2, bm=C, bn=bn, bt=tm, rows=2 * C, name="dwo_a")
    dwo = _mm_tn(out_b.reshape(1, T, C), dx2, bm=C, bn=bn, bt=tm, rows=2 * C, m_off=1, into=dwo, name="dwo_b")
    grads["w_out"] = dwo[0]

    dqa, dka, dva, dtab = hosted("dil_bwd", _dil_bwd, nrm_a, table, out_a, dmix, lse_a)
    dqb, dkb, dvb, dck, dcq = hosted("fox_bwd", _fox_bwd, nrm_b, c_col, c_row, out_b, dmix, lse_b, tq, do_off=H)
    grads["rel_bias_table"] = dtab[:, 0, :NUM_BUCKETS].T
    dc = _pad_lanes((dck[:, 0, :] + dcq[:, :, 0]).T)
    df, dfb = _fox_gate_bwd(f_raw, fbias, dc, "fox_gate_bwd")
    grads["forget_bias"] = dfb[:, :H]

    flat = lambda a: a.reshape(T, a.shape[-1])
    dproj_a, dgain_a = _qknorm_bwd(flat(raw_a), flat(dqa), flat(dka), flat(dva), gains_a[:2], min(256, T), "qknorm_bwd_a")
    dproj_b, dgain_b = _qknorm_bwd(raw_b, dqb, dkb, dvb, gains_b[:2], min(256, T), "qknorm_bwd_b")
    grads["q_norm_a"], grads["k_norm_a"] = dgain_a[0], dgain_a[1]
    grads["q_norm_b"], grads["k_norm_b"] = dgain_b[0], dgain_b[1]
    dproj_a = dproj_a.reshape(NSLAB, n16, 3 * C)

    dh2 = _mm(dproj_a, w_in_t, nt=False, tk=C, tm=tms, a_layout="slab", out_layout="view", name="dh2_a")
    dh2 = _mm(dproj_b, w_in_t, nt=False, tk=C, tm=tmm, a_layout="nat", out_layout="nat", resid=dh2, w_off=3,
              small=(df, w_f_t), name="dh2_b")
    dx1, grads["mix_norm"] = _rms_bwd(x1, gm, dh2, dx2, tm, "mix_rms_bwd")
    bt = min(2048, T)
    dwt = _mm_tn(flat(dproj_a)[None], flat(h2a), bm=C, bn=bn, bt=bt, rows=6 * C + H, name="dw_a")
    dwt = _mm_tn(dproj_b[None], h2b, bm=C, bn=bn, bt=bt, rows=6 * C + H, m_off=3, into=dwt, name="dw_b")
    dwt = _mm_tn(df[None, :, :H], h2b, bm=H, bn=bn, bt=bt, rows=6 * C + H, m_off=6 * C // H, into=dwt, name="dw_f")
    grads["w_in_t"] = dwt[0]

    grad_x = ffn_backward("ffn1", x, g1, h1, gu1, act1, wts["ffn1_in"], wts["ffn1_out"], dx1)
    return loss_row, grad_x, grads


def _place():
    x, y, c = lax.axis_index("x"), lax.axis_index("y"), lax.axis_index("c")
    other_chips = [(1 - x, y), (x, 1 - y), (1 - x, 1 - y)]
    return x, y, c, other_chips


def _run_side(side, name):
    def body(*refs):
        si, so = len(side.ins), len(side.outs)
        side.start(refs[:si], refs[si:si + so], refs[si + so:])
        side.finish(refs[:si], refs[si:si + so], refs[si + so:])

    return pl.pallas_call(body, name=name, in_specs=[ANY] * len(side.ins), out_specs=[ANY] * len(side.outs),
                          out_shape=side.outs, scratch_shapes=side.sems)(*side.ins)


def _all_gather(shards):
    n = len(shards)

    def plan(ins, outs, sems):
        send_sems, recv_sems, local_sems = sems
        x, y, c, chips = _place()
        me, sibling = (x, y, c), (x, y, 1 - c)

        def copy(a, k, block, to, src=None):
            px, py, pc = block
            dst = outs[a].at[4 * px + 2 * py + pc]
            return pltpu.make_async_remote_copy(
                src_ref=dst if src is None else src, dst_ref=dst, send_sem=send_sems.at[7 * a + k],
                recv_sem=recv_sems.at[7 * a + k], device_id=to, device_id_type=MESH)

        mine = [pltpu.make_async_copy(ins[a], outs[a].at[4 * x + 2 * y + c], local_sems.at[a]) for a in range(n)]
        first = []
        for a in range(n):
            first.append(copy(a, 0, me, sibling, src=ins[a]))
            first += [copy(a, 1 + j, me, (*chip, c), src=ins[a]) for j, chip in enumerate(chips)]
        return copy, mine, first, me, sibling, c, chips

    def start(ins, outs, sems):
        _, mine, first, *_ = plan(ins, outs, sems)
        for cp in mine + first:
            cp.start()

    def finish(ins, outs, sems):
        copy, mine, first, me, sibling, c, chips = plan(ins, outs, sems)
        passed = []
        for a in range(n):
            for j, chip in enumerate(chips):
                copy(a, 1 + j, (*chip, c), me).wait_recv()
                fwd = copy(a, 4 + j, (*chip, c), sibling)
                fwd.start()
                passed.append(fwd)
        for a in range(n):
            copy(a, 0, sibling, me).wait_recv()
            for j, chip in enumerate(chips):
                copy(a, 4 + j, (*chip, 1 - c), me).wait_recv()
        for cp in first + passed:
            cp.wait_send()
        for cp in mine:
            cp.wait()

    return _Side(shards, [_sds((N_DEV,) + s.shape, s.dtype) for s in shards],
                 [pltpu.SemaphoreType.DMA((7 * n,)), pltpu.SemaphoreType.DMA((7 * n,)), pltpu.SemaphoreType.DMA((n,))],
                 start, finish)


def _all_gather_relayed(shards):
    n = len(shards)
    halves = [-(-(s.shape[0] // 2) // BF16_TILE_ROWS) * BF16_TILE_ROWS for s in shards]

    def body_parts(ins, outs, sems):
        send_sems, recv_sems, local_sems = sems
        x, y, c, _ = _place()
        me, sib, xn, yn, dg = (x, y, c), (x, y, 1 - c), (1 - x, y, c), (x, 1 - y, c), (1 - x, 1 - y, c)

        def rows(a, block, part):
            px, py, pc = block
            whole = outs[a].at[4 * px + 2 * py + pc]
            if part is None:
                return whole
            return whole.at[pl.ds(0, halves[a])] if part == 0 else whole.at[pl.ds(halves[a], shards[a].shape[0] - halves[a])]

        def copy(a, k, block, part, to, src=None):
            dst = rows(a, block, part)
            return pltpu.make_async_remote_copy(
                src_ref=dst if src is None else src, dst_ref=dst, send_sem=send_sems.at[9 * a + k],
                recv_sem=recv_sems.at[9 * a + k], device_id=to, device_id_type=MESH)

        flip = lambda dev: (dev[0], dev[1], 1 - dev[2])
        mine = [pltpu.make_async_copy(ins[a], rows(a, me, None), local_sems.at[a]) for a in range(n)]
        own = [[copy(a, 0, me, None, sib, src=ins[a]), copy(a, 1, me, None, xn, src=ins[a]),
                copy(a, 2, me, None, yn, src=ins[a])] for a in range(n)]
        relays = lambda a: [(1, [copy(a, 3, xn, 0, yn), copy(a, 5, xn, None, sib)]),
                            (2, [copy(a, 4, yn, 1, xn), copy(a, 6, yn, None, sib)]),
                            (3, [copy(a, 7, dg, 0, sib)]), (4, [copy(a, 8, dg, 1, sib)])]
        lands = {0: (sib, None), 1: (xn, None), 2: (yn, None), 3: (dg, 0), 4: (dg, 1), 5: (flip(xn), None),
                 6: (flip(yn), None), 7: (flip(dg), 0), 8: (flip(dg), 1)}
        arrival = lambda a, k: copy(a, k, lands[k][0], lands[k][1], me)
        return mine, own, relays, arrival

    def start(ins, outs, sems):
        mine, own, _, _ = body_parts(ins, outs, sems)
        for cp in mine + [cp for per in own for cp in per]:
            cp.start()

    def finish(ins, outs, sems):
        mine, own, relays, arrival = body_parts(ins, outs, sems)
        sent = [cp for per in own for cp in per]
        relays = [relays(a) for a in range(n)]
        for stage in range(4):
            for a in range(n):
                after, passes = relays[a][stage]
                arrival(a, after).wait_recv()
                for cp in passes:
                    cp.start()
                sent += passes
        for a in range(n):
            for k in (0, 5, 6, 7, 8):
                arrival(a, k).wait_recv()
        for cp in sent:
            cp.wait_send()
        for cp in mine:
            cp.wait()

    return _Side(shards, [_sds((N_DEV,) + s.shape, s.dtype) for s in shards],
                 [pltpu.SemaphoreType.DMA((9 * n,)), pltpu.SemaphoreType.DMA((9 * n,)), pltpu.SemaphoreType.DMA((n,))],
                 start, finish)


def _exchange_in_chip(gs):
    n = len(gs)

    def copies(ins, outs, sems):
        x, y, c, _ = _place()
        return [pltpu.make_async_remote_copy(
            src_ref=ins[a].at[2 * q + 1 - c], dst_ref=outs[a].at[q], send_sem=sems[0].at[4 * a + q],
            recv_sem=sems[1].at[4 * a + q], device_id=(x, y, 1 - c), device_id_type=MESH)
            for a in range(n) for q in range(4)]

    def start(ins, outs, sems):
        for cp in copies(ins, outs, sems):
            cp.start()

    def finish(ins, outs, sems):
        for cp in copies(ins, outs, sems):
            cp.wait()

    return _Side(gs, [_sds((4,) + g.shape[1:], g.dtype) for g in gs],
                 [pltpu.SemaphoreType.DMA((4 * n,)), pltpu.SemaphoreType.DMA((4 * n,))], start, finish)


def _exchange_between_chips(ps):
    n = len(ps)

    def copies(ins, outs, sems):
        x, y, c, chips = _place()
        return [pltpu.make_async_remote_copy(
            src_ref=ins[a].at[2 * cx + cy], dst_ref=outs[a].at[j], send_sem=sems[0].at[3 * a + j],
            recv_sem=sems[1].at[3 * a + j], device_id=(cx, cy, c), device_id_type=MESH)
            for a in range(n) for j, (cx, cy) in enumerate(chips)]

    def start(ins, outs, sems):
        for cp in copies(ins, outs, sems):
            cp.start()

    def finish(ins, outs, sems):
        for cp in copies(ins, outs, sems):
            cp.wait()

    return _Side(ps, [_sds((3,) + p.shape[1:], p.dtype) for p in ps],
                 [pltpu.SemaphoreType.DMA((3 * n,)), pltpu.SemaphoreType.DMA((3 * n,))], start, finish)


def _all_reduce_small(v, name):
    R = v.shape[0]

    def body(v_ref, sum_ref, all_ref, send_sems, recv_sems):
        x, y, c, _ = _place()
        k = 4 * x + 2 * y + c
        all_ref[k] = v_ref[...]
        copies = []
        for rel in range(1, N_DEV):
            fx, fy, fc = (rel >> 2) & 1, (rel >> 1) & 1, rel & 1
            peer = (1 - x if fx else x, 1 - y if fy else y, 1 - c if fc else c)
            copies.append(pltpu.make_async_remote_copy(
                src_ref=v_ref, dst_ref=all_ref.at[k], send_sem=send_sems.at[rel - 1], recv_sem=recv_sems.at[rel - 1],
                device_id=peer, device_id_type=MESH))
        for cp in copies:
            cp.start()
        for rel in range(1, N_DEV):
            fx, fy, fc = (rel >> 2) & 1, (rel >> 1) & 1, rel & 1
            src = 4 * (1 - x if fx else x) + 2 * (1 - y if fy else y) + (1 - c if fc else c)
            pltpu.make_async_remote_copy(
                src_ref=v_ref, dst_ref=all_ref.at[src], send_sem=send_sems.at[rel - 1], recv_sem=recv_sems.at[rel - 1],
                device_id=(x, y, c), device_id_type=MESH).wait_recv()
        for cp in copies:
            cp.wait_send()
        tot = all_ref[0]
        for d in range(1, N_DEV):
            tot = tot + all_ref[d]
        sum_ref[...] = tot

    vm = pl.BlockSpec(memory_space=pltpu.VMEM)
    return pl.pallas_call(
        body, name=name, in_specs=[vm], out_specs=[vm, vm],
        out_shape=[_sds((R, HEAD), F32), _sds((N_DEV, R, HEAD), F32)],
        scratch_shapes=[pltpu.SemaphoreType.DMA((N_DEV - 1,)), pltpu.SemaphoreType.DMA((N_DEV - 1,))],
    )(v)[0]


def _tiles(rows, cols):
    tc = 512 if cols % 512 == 0 else cols
    tr = rows
    while tr * tc * 4 > ELEMENTWISE_BLOCK_BYTES and tr % (2 * BF16_TILE_ROWS) == 0:
        tr //= 2
    return tr, tc


def _chip_sum(g, r1, place, name):
    _, R, Cc = g.shape
    tr, tc = _tiles(R, Cc)

    def body(place_ref, g_ref, r_ref, p_ref):
        p_ref[...] = (g_ref[...] + r_ref[...]).astype(BF16)

    chip = lambda k, place: k + (k >= place[1]).astype(jnp.int32)
    blk = lambda f: pl.BlockSpec((None, tr, tc), f)
    return pl.pallas_call(
        body, name=name,
        grid_spec=pltpu.PrefetchScalarGridSpec(
            num_scalar_prefetch=1, grid=(3, R // tr, Cc // tc),
            in_specs=[blk(lambda k, i, j, place: (2 * chip(k, place) + place[0], i, j)),
                      blk(lambda k, i, j, place: (chip(k, place), i, j))],
            out_specs=blk(lambda k, i, j, place: (chip(k, place), i, j))),
        out_shape=_sds((4, R, Cc), BF16), compiler_params=_cparams(),
    )(place, g, r1)


def _adamw_update(gv, w_ref, m_ref, v_ref, d_ref, nm_ref, nv_ref):
    nm = B1 * m_ref[...] + (1.0 - B1) * gv
    nv = B2 * v_ref[...] + (1.0 - B2) * jnp.square(gv)
    m_hat = nm / (1.0 - B1 ** STEP)
    v_hat = nv / (1.0 - B2 ** STEP)
    d_ref[...] = -LR * (m_hat / (jnp.sqrt(v_hat) + EPS) + WD * w_ref[...])
    nm_ref[...] = nm
    nv_ref[...] = nv


def _reduce_adamw(g, r1, r2, where, w, m, v, name):
    _, R, Cc = g.shape
    tr, tc = _tiles(R, Cc)

    def body(where_ref, g_ref, r1_ref, r2_ref, w_ref, m_ref, v_ref, o_ref, d_ref, nm_ref, nv_ref):
        gv = ((g_ref[...] + r1_ref[...]) + r2_ref[0].astype(F32)) + (r2_ref[1].astype(F32) + r2_ref[2].astype(F32))
        o_ref[...] = gv
        _adamw_update(gv, w_ref, m_ref, v_ref, d_ref, nm_ref, nv_ref)

    blk = pl.BlockSpec((tr, tc), lambda i, j, w: (i, j))
    return pl.pallas_call(
        body, name=name,
        grid_spec=pltpu.PrefetchScalarGridSpec(
            num_scalar_prefetch=1, grid=(R // tr, Cc // tc),
            in_specs=[pl.BlockSpec((None, tr, tc), lambda i, j, w: (w[0], i, j)),
                      pl.BlockSpec((None, tr, tc), lambda i, j, w: (w[1], i, j)),
                      pl.BlockSpec((3, tr, tc), lambda i, j, w: (0, i, j)), blk, blk, blk],
            out_specs=[blk] * 4),
        out_shape=[_sds((R, Cc), F32)] * 4, compiler_params=_cparams(),
    )(where, g, r1, r2, w, m, v)


def _adamw(w, g, m, v, name):
    R, Cc = w.shape
    tr, tc = _tiles(R, Cc)

    def body(w_ref, g_ref, m_ref, v_ref, d_ref, nm_ref, nv_ref):
        _adamw_update(g_ref[...], w_ref, m_ref, v_ref, d_ref, nm_ref, nv_ref)

    blk = pl.BlockSpec((tr, tc), lambda i, j: (i, j))
    return pl.pallas_call(
        body, name=name, grid=(R // tr, Cc // tc), in_specs=[blk] * 4, out_specs=[blk] * 3,
        out_shape=[_sds((R, Cc), F32)] * 3, compiler_params=_cparams(),
    )(w, g, m, v)


SMALL = ("ffn1_norm", "mix_norm", "ffn2_norm", "q_norm_a", "k_norm_a", "q_norm_b", "k_norm_b", "forget_bias",
         "rel_bias_table")
LARGE = ("ffn1_w_in", "ffn1_w_out", "w_in", "w_out", "ffn2_w_in", "ffn2_w_out")
ORDER = ("ffn1_norm", "ffn1_w_in", "ffn1_w_out", "mix_norm", "w_in", "q_norm_a", "k_norm_a", "q_norm_b", "k_norm_b",
         "forget_bias", "rel_bias_table", "w_out", "ffn2_norm", "ffn2_w_in", "ffn2_w_out")


def _pack_small(vals):
    rows = []
    for name in SMALL:
        flat = vals[name].reshape(-1)
        pad = (-flat.shape[0]) % HEAD
        rows.append(jnp.pad(flat, (0, pad)).reshape(-1, HEAD))
    return jnp.concatenate(rows, axis=0)


def _unpack_small(packed, like):
    out, r = {}, 0
    for name in SMALL:
        size = like[name].size
        nrow = -(-size // HEAD)
        out[name] = packed[r:r + nrow].reshape(-1)[:size].reshape(like[name].shape)
        r += nrow
    return out


def kernel(x, ffn1_norm, ffn1_w_in, ffn1_w_out, mix_norm, w_in, q_norm_a, k_norm_a, q_norm_b, k_norm_b, forget_bias, rel_bias_table, w_out, ffn2_norm, ffn2_w_in, ffn2_w_out, loss_target, m_ffn1_norm, m_ffn1_w_in, m_ffn1_w_out, m_mix_norm, m_w_in, m_q_norm_a, m_k_norm_a, m_q_norm_b, m_k_norm_b, m_forget_bias, m_rel_bias_table, m_w_out, m_ffn2_norm, m_ffn2_w_in, m_ffn2_w_out, v_ffn1_norm, v_ffn1_w_in, v_ffn1_w_out, v_mix_norm, v_w_in, v_q_norm_a, v_k_norm_a, v_q_norm_b, v_k_norm_b, v_forget_bias, v_rel_bias_table, v_w_out, v_ffn2_norm, v_ffn2_w_in, v_ffn2_w_out):
    w = dict(ffn1_norm=ffn1_norm, ffn1_w_in=ffn1_w_in, ffn1_w_out=ffn1_w_out, mix_norm=mix_norm, w_in=w_in,
             q_norm_a=q_norm_a, k_norm_a=k_norm_a, q_norm_b=q_norm_b, k_norm_b=k_norm_b, forget_bias=forget_bias,
             rel_bias_table=rel_bias_table, w_out=w_out, ffn2_norm=ffn2_norm, ffn2_w_in=ffn2_w_in, ffn2_w_out=ffn2_w_out)
    m = dict(ffn1_norm=m_ffn1_norm, ffn1_w_in=m_ffn1_w_in, ffn1_w_out=m_ffn1_w_out, mix_norm=m_mix_norm, w_in=m_w_in,
             q_norm_a=m_q_norm_a, k_norm_a=m_k_norm_a, q_norm_b=m_q_norm_b, k_norm_b=m_k_norm_b,
             forget_bias=m_forget_bias, rel_bias_table=m_rel_bias_table, w_out=m_w_out, ffn2_norm=m_ffn2_norm,
             ffn2_w_in=m_ffn2_w_in, ffn2_w_out=m_ffn2_w_out)
    v = dict(ffn1_norm=v_ffn1_norm, ffn1_w_in=v_ffn1_w_in, ffn1_w_out=v_ffn1_w_out, mix_norm=v_mix_norm, w_in=v_w_in,
             q_norm_a=v_q_norm_a, k_norm_a=v_k_norm_a, q_norm_b=v_q_norm_b, k_norm_b=v_k_norm_b,
             forget_bias=v_forget_bias, rel_bias_table=v_rel_bias_table, w_out=v_w_out, ffn2_norm=v_ffn2_norm,
             ffn2_w_in=v_ffn2_w_in, ffn2_w_out=v_ffn2_w_out)
    T, D = x.shape[1], x.shape[2]
    C = D // 2
    H = C // HEAD
    ff_shard = ffn1_w_out.shape[1]

    f1i, = _run_side(_all_gather_relayed([ffn1_w_in[0].T.astype(BF16)]), "gather_ffn1")
    wts = dict(ffn1_in=f1i.reshape(2, N_DEV, ff_shard, D))
    xi, yi, ci = lax.axis_index("x"), lax.axis_index("y"), lax.axis_index("c")
    place = jnp.stack([ci, 2 * xi + yi]).astype(jnp.int32)
    where = jnp.stack([4 * xi + 2 * yi + ci, 2 * xi + yi]).astype(jnp.int32)
    gs, r1, ps, r2 = {}, {}, {}, {}

    def shard(name):
        name, _, part = name.partition(":")
        s = (w[name][0].T if name.endswith("w_in") else w[name][0]).astype(BF16)
        first = -(-(s.shape[0] // 2) // BF16_TILE_ROWS) * BF16_TILE_ROWS
        return {"": s, "first": s[:first], "rest": s[first:]}[part]

    def by_destination(name, grads):
        key = name + "_t" if name.endswith("w_in") else name
        gs[name] = grads[key].reshape(N_DEV, -1, D)
        return grads.get(key + "_bf16", grads[key]).reshape(N_DEV, -1, D)

    def chip_sums(names):
        for name in names:
            ps[name] = _chip_sum(gs[name], r1[name], place, "chip_sum_" + name)
        return [ps[name] for name in names]

    class Plan:
        carried = {"ffn1_up": ("gather", ("ffn1_w_out", "w_in:first")), "ffn1_down": ("gather", ("w_in:rest", "w_out")),
                   "dil_fwd": ("gather", ("ffn2_w_out",)), "fox_fwd": ("gather", ("ffn2_w_in",)),
                   "dil_bwd": ("in_chip", ("ffn2_w_in", "ffn2_w_out")), "fox_bwd": ("between", ("ffn2_w_in", "ffn2_w_out")),
                   "ffn1_bwd": ("in_chip", ("w_in", "w_out")), "ffn1_dwin": ("between", ("w_in", "w_out")),
                   "ffn1_rms_bwd": ("in_chip", ("ffn1_w_in",)), "ffn1_dwout": ("between", ("ffn1_w_in",))}

        def before(self, host, wts, grads):
            if host not in self.carried:
                return None
            kind, names = self.carried[host]
            if kind == "gather":
                return _all_gather([shard(n) for n in names])
            if kind == "in_chip":
                return _exchange_in_chip([by_destination(n, grads) for n in names])
            return _exchange_between_chips(chip_sums(names))

        def after(self, host, res, wts, grads):
            kind, names = self.carried[host]
            if host == "ffn1_up":
                wts.update(ffn1_out=res[0], w_in_first_rows=res[1])
            elif host == "ffn1_down":
                w_in_t = jnp.concatenate([wts.pop("w_in_first_rows"), res[0]], axis=1).reshape(-1, D)
                wts.update(w_in_t=w_in_t, w_f_t=jnp.pad(w_in_t[6 * C:], ((0, HEAD - H), (0, 0))),
                           w_o=res[1].reshape(2 * C, D))
            elif host == "dil_fwd":
                wts.update(ffn2_out=res[0])
            elif host == "fox_fwd":
                wts.update(ffn2_in=res[0].reshape(2, N_DEV, ff_shard, D))
            else:
                (r1 if kind == "in_chip" else r2).update(zip(names, res))

    small = {name: w[name] for name in SMALL}
    loss_row, grad_x, grads = _local_step(x[0], loss_target[0], small, wts, Plan())

    tail = ("ffn1_w_out",)
    r1[tail[0]], = _run_side(_exchange_in_chip([by_destination(tail[0], grads)]), "reduce_in_chip_tail")
    r2.update(zip(tail, _run_side(_exchange_between_chips(chip_sums(tail)), "reduce_between_chips_tail")))

    packed = _pack_small(grads)
    nsmall = packed.shape[0]
    packed = jnp.concatenate([packed, loss_row, jnp.zeros(((-nsmall - 1) % 8, HEAD), F32)], axis=0)
    reduced = _all_reduce_small(packed, "reduce_small")
    loss = reduced[nsmall, 0]
    g_small = _unpack_small(reduced[:nsmall], small)

    grad, delta, new_m, new_v = dict(g_small), {}, {}, {}
    for name in LARGE:
        to = (lambda t: t[0].T) if name.endswith("w_in") else (lambda t: t[0])
        back = (lambda t: t.T[None]) if name.endswith("w_in") else (lambda t: t[None])
        res = _reduce_adamw(gs[name], r1[name], r2[name], where, to(w[name]), to(m[name]), to(v[name]), "adamw_" + name)
        grad[name], delta[name], new_m[name], new_v[name] = (back(t) for t in res)
    d, nm, nv = _adamw(_pack_small(w), reduced[:nsmall], _pack_small(m), _pack_small(v), "adamw_small")
    delta.update(_unpack_small(d, small))
    new_m.update(_unpack_small(nm, small))
    new_v.update(_unpack_small(nv, small))
    return (loss, grad_x[None], *[grad[n] for n in ORDER], *[delta[n] for n in ORDER],
            *[new_m[n] for n in ORDER], *[new_v[n] for n in ORDER])
```
